```python
import jax, jax.numpy as jnp
from jax import lax
import numpy as np

D_MODEL = 1024
BATCH = 2
SEQ = 8192
DEPTH = 2

HEAD_DIM = 64
MLA_HEADS = 6
MLA_NOPE = 64
MLA_ROPE = 32
MLA_V = 64
MLA_Q_LORA = 256
MLA_KV_LORA = 128
NSA_HEADS = 6
NSA_KV_HEADS = 2
NSA_GROUP = NSA_HEADS // NSA_KV_HEADS
NSA_BRANCHES = 3
CMP_LEN = 32
CMP_STRIDE = 16
CMP_HIDDEN = 128
SEL_LEN = 64
SEL_TOPK = 16
WINDOW = 512
SB_HEADS = 4
D_MIX = MLA_HEADS * MLA_V + NSA_HEADS * HEAD_DIM + SB_HEADS * HEAD_DIM
D_FF = 2816
ROPE_THETA = 500000.0
PARTIAL_ROT = HEAD_DIM // 4
Q_BLOCK = 128
EPS = 1e-6
NEG_INF = -1e30
FORCE_SCORE = 1e4
NSA_KV_W = NSA_KV_HEADS * HEAD_DIM
IN_WIDTHS = (MLA_Q_LORA, MLA_KV_LORA, MLA_ROPE,
             NSA_HEADS * HEAD_DIM,
             NSA_KV_W, NSA_KV_W, NSA_KV_W, NSA_KV_W, NSA_KV_W, NSA_KV_W,
             NSA_HEADS * NSA_BRANCHES,
             SB_HEADS * HEAD_DIM, SB_HEADS * HEAD_DIM, SB_HEADS * HEAD_DIM)
D_IN = sum(IN_WIDTHS)

kernel_name = 'hybrid_mla_nsa_stickbreak_macaron'


def rmsnorm(x, g):
    xf = x.astype(jnp.float32)
    y = xf * lax.rsqrt(jnp.mean(xf * xf, axis=-1, keepdims=True) + EPS)
    return (y * g.astype(jnp.float32)).astype(x.dtype)


def rope(x, pos, rot_dim):
    half = rot_dim // 2
    inv_freq = ROPE_THETA ** (-jnp.arange(half, dtype=jnp.float32) / half)
    ang = pos.astype(jnp.float32)[:, None] * inv_freq[None, :]
    cos = jnp.cos(ang)[:, None, :]
    sin = jnp.sin(ang)[:, None, :]
    xr = x[..., :rot_dim].astype(jnp.float32)
    x1, x2 = xr[..., :half], xr[..., half:]
    rot = jnp.concatenate([x1 * cos - x2 * sin, x2 * cos + x1 * sin], axis=-1)
    return jnp.concatenate([rot.astype(x.dtype), x[..., rot_dim:]], axis=-1)


def masked_softmax(s, mask):
    return jax.nn.softmax(jnp.where(mask, s.astype(jnp.float32), NEG_INF), axis=-1)


def swiglu(h, w_gate, w_up, w_down):
    return (jax.nn.silu(h @ w_gate) * (h @ w_up)) @ w_down


def split_heads(t, n_heads):
    b, s, _ = t.shape
    return t.reshape(b, s, n_heads, -1)


def sweep_query_blocks(block_fn, n_blocks):
    out = lax.map(block_fn, jnp.arange(n_blocks, dtype=jnp.int32))
    nb, b, q, h, d = out.shape
    return out.transpose(1, 0, 2, 3, 4).reshape(b, nb * q, h, d)


def mla_attention(c_q, c_kv, k_rope, q_norm, w_uq, kv_norm, w_ukv, pos):
    b, s, _ = c_q.shape
    q = (rmsnorm(c_q, q_norm) @ w_uq).reshape(b, s, MLA_HEADS, MLA_NOPE + MLA_ROPE)
    q_nope = q[..., :MLA_NOPE]
    q_pe = rope(q[..., MLA_NOPE:], pos, MLA_ROPE)
    kv = (rmsnorm(c_kv, kv_norm) @ w_ukv).reshape(b, s, MLA_HEADS, MLA_NOPE + MLA_V)
    k_nope, v = kv[..., :MLA_NOPE], kv[..., MLA_NOPE:]
    k_pe = rope(k_rope[:, :, None, :], pos, MLA_ROPE)[:, :, 0]
    scale = (MLA_NOPE + MLA_ROPE) ** -0.5

    def block(i):
        q0 = i * Q_BLOCK
        qn = lax.dynamic_slice_in_dim(q_nope, q0, Q_BLOCK, axis=1)
        qp = lax.dynamic_slice_in_dim(q_pe, q0, Q_BLOCK, axis=1)
        sc = (jnp.einsum('bqhd,bkhd->bhqk', qn, k_nope)
              + jnp.einsum('bqhr,bkr->bhqk', qp, k_pe)).astype(jnp.float32) * scale
        qpos = q0 + jnp.arange(Q_BLOCK)
        p = masked_softmax(sc, pos[None, :] <= qpos[:, None]).astype(v.dtype)
        return jnp.einsum('bhqk,bkhd->bqhd', p, v)

    return sweep_query_blocks(block, s // Q_BLOCK)


def compress_blocks(x, pos_emb, w1, w2):
    b, s, g, d = x.shape
    chunks = x.reshape(b, s // CMP_STRIDE, CMP_STRIDE, g, d)
    blocks = jnp.concatenate([chunks[:, :-1], chunks[:, 1:]], axis=2)
    blocks = blocks + pos_emb[None, None, :, None, :]
    flat = blocks.transpose(0, 1, 3, 2, 4).reshape(b, s // CMP_STRIDE - 1, g, CMP_LEN * d)
    return jax.nn.gelu(flat @ w1) @ w2


def nsa_attention(q, k_cmp, v_cmp, k_sel, v_sel, k_win, v_win, gates,
                  cmp_pos_k, cmp_w1_k, cmp_w2_k, cmp_pos_v, cmp_w1_v, cmp_w2_v, pos):
    b, s, h, d = q.shape
    q = rope(q, pos, PARTIAL_ROT)
    k_cmp = rope(k_cmp, pos, PARTIAL_ROT)
    k_sel = rope(k_sel, pos, PARTIAL_ROT)
    k_win = rope(k_win, pos, PARTIAL_ROT)
    scale = d ** -0.5
    kc = compress_blocks(k_cmp, cmp_pos_k, cmp_w1_k, cmp_w2_k)
    vc = compress_blocks(v_cmp, cmp_pos_v, cmp_w1_v, cmp_w2_v)
    n_cmp = s // CMP_STRIDE - 1
    cmp_start = jnp.arange(n_cmp) * CMP_STRIDE
    cmp_end = cmp_start + CMP_LEN - 1
    n_slc = s // SEL_LEN
    n_top = min(SEL_TOPK, n_slc)
    slc_start = jnp.arange(n_slc) * SEL_LEN
    overlap = ((cmp_start[:, None] < slc_start[None, :] + SEL_LEN)
               & (cmp_start[:, None] + CMP_LEN > slc_start[None, :])).astype(jnp.float32)
    ksb = k_sel.reshape(b, n_slc, SEL_LEN, NSA_KV_HEADS, d).transpose(0, 3, 1, 2, 4)
    vsb = v_sel.reshape(b, n_slc, SEL_LEN, NSA_KV_HEADS, d).transpose(0, 3, 1, 2, 4)
    b_ix = jnp.arange(b)[:, None, None, None]
    g_ix = jnp.arange(NSA_KV_HEADS)[None, :, None, None]
    blk = jnp.arange(n_slc)
    kw = jnp.pad(k_win, ((0, 0), (WINDOW, 0), (0, 0), (0, 0)))
    vw = jnp.pad(v_win, ((0, 0), (WINDOW, 0), (0, 0), (0, 0)))

    def block(i):
        q0 = i * Q_BLOCK
        qpos = q0 + jnp.arange(Q_BLOCK)
        qb = lax.dynamic_slice_in_dim(q, q0, Q_BLOCK, axis=1).reshape(b, Q_BLOCK, NSA_KV_HEADS, NSA_GROUP, d)
        sc = jnp.einsum('bqgrd,bcgd->bgrqc', qb, kc).astype(jnp.float32) * scale
        cmask = cmp_end[None, :] <= qpos[:, None]
        p_cmp = masked_softmax(sc, cmask)
        p_cmp = jnp.where(cmask.any(axis=-1)[:, None], p_cmp, 0.0)
        o_cmp = jnp.einsum('bgrqc,bcgd->bqgrd', p_cmp.astype(vc.dtype), vc)
        score = jnp.einsum('bgrqc,cn->bgqn', p_cmp, overlap)
        cur = qpos // SEL_LEN
        forced = (blk[None, :] == 0) | (blk[None, :] == cur[:, None]) | (blk[None, :] == cur[:, None] - 1)
        future = blk[None, :] > cur[:, None]
        score = jnp.where(forced, FORCE_SCORE, jnp.where(future, -1.0, score))
        _, idx = lax.top_k(score, n_top)
        kg = ksb[b_ix, g_ix, idx]
        vg = vsb[b_ix, g_ix, idx]
        tok = idx[..., None] * SEL_LEN + jnp.arange(SEL_LEN)
        smask = (tok <= qpos[None, None, :, None, None])[:, :, None]
        ss = jnp.einsum('bqgrd,bgqnld->bgrqnl', qb, kg).astype(jnp.float32) * scale
        ss = jnp.where(smask, ss, NEG_INF).reshape(b, NSA_KV_HEADS, NSA_GROUP, Q_BLOCK, n_top * SEL_LEN)
        p_sel = jax.nn.softmax(ss, axis=-1).astype(vg.dtype).reshape(b, NSA_KV_HEADS, NSA_GROUP, Q_BLOCK, n_top, SEL_LEN)
        o_sel = jnp.einsum('bgrqnl,bgqnld->bqgrd', p_sel, vg)
        kwb = lax.dynamic_slice_in_dim(kw, q0, WINDOW + Q_BLOCK, axis=1)
        vwb = lax.dynamic_slice_in_dim(vw, q0, WINDOW + Q_BLOCK, axis=1)
        wpos = q0 - WINDOW + jnp.arange(WINDOW + Q_BLOCK)
        wmask = ((wpos[None, :] <= qpos[:, None]) & (wpos[None, :] > qpos[:, None] - WINDOW)
                 & (wpos[None, :] >= 0))
        sw = jnp.einsum('bqgrd,bkgd->bgrqk', qb, kwb).astype(jnp.float32) * scale
        p_win = masked_softmax(sw, wmask).astype(vwb.dtype)
        o_win = jnp.einsum('bgrqk,bkgd->bqgrd', p_win, vwb)
        g = lax.dynamic_slice_in_dim(gates, q0, Q_BLOCK, axis=1).reshape(b, Q_BLOCK, NSA_KV_HEADS, NSA_GROUP, NSA_BRANCHES)
        o = g[..., 0:1] * o_cmp + g[..., 1:2] * o_sel + g[..., 2:3] * o_win
        return o.reshape(b, Q_BLOCK, h, d)

    return sweep_query_blocks(block, s // Q_BLOCK)


def stick_breaking_attention(q, k, v):
    b, s, h, d = q.shape
    scale = d ** -0.5
    kpos = jnp.arange(s)

    def block(i):
        q0 = i * Q_BLOCK
        qpos = q0 + jnp.arange(Q_BLOCK)
        qb = lax.dynamic_slice_in_dim(q, q0, Q_BLOCK, axis=1)
        z = jnp.einsum('bqhd,bkhd->bhqk', qb, k).astype(jnp.float32) * scale
        strict = kpos[None, :] < qpos[:, None]
        log_beta = jax.nn.log_sigmoid(z)
        log_rem = jnp.where(strict, jax.nn.log_sigmoid(-z), 0.0)
        suffix = lax.cumsum(log_rem, axis=3, reverse=True) - log_rem
        a = jnp.where(strict, jnp.exp(log_beta + suffix), 0.0).astype(v.dtype)
        return jnp.einsum('bhqk,bkhd->bqhd', a, v)

    return sweep_query_blocks(block, s // Q_BLOCK)


def setup_inputs(seed: int = 0) -> dict:
    key = jax.random.key(seed)
    ks = iter(jax.random.split(key, 32))
    f32 = jnp.float32
    L = DEPTH

    def nrm(shape, fan_in):
        return jax.random.normal(next(ks), shape, f32) * (fan_in ** -0.5)

    def gain(shape):
        return 1.0 + 0.02 * jax.random.normal(next(ks), shape, f32)

    return {
        'x': jax.random.normal(next(ks), (BATCH, SEQ, D_MODEL), f32),
        'ffn1_norm': gain((L, D_MODEL)),
        'ffn1_w_gate': nrm((L, D_MODEL, D_FF), D_MODEL),
        'ffn1_w_up': nrm((L, D_MODEL, D_FF), D_MODEL),
        'ffn1_w_down': nrm((L, D_FF, D_MODEL), D_FF),
        'mix_norm': gain((L, D_MODEL)),
        'w_in': nrm((L, D_MODEL, D_IN), D_MODEL),
        'mla_q_norm': gain((L, MLA_Q_LORA)),
        'mla_w_uq': nrm((L, MLA_Q_LORA, MLA_HEADS * (MLA_NOPE + MLA_ROPE)), MLA_Q_LORA),
        'mla_kv_norm': gain((L, MLA_KV_LORA)),
        'mla_w_ukv': nrm((L, MLA_KV_LORA, MLA_HEADS * (MLA_NOPE + MLA_V)), MLA_KV_LORA),
        'nsa_gate_bias': 0.1 * jax.random.normal(next(ks), (L, NSA_HEADS * NSA_BRANCHES), f32),
        'nsa_cmp_pos_k': 0.1 * jax.random.normal(next(ks), (L, CMP_LEN, HEAD_DIM), f32),
        'nsa_cmp_w1_k': nrm((L, CMP_LEN * HEAD_DIM, CMP_HIDDEN), CMP_LEN * HEAD_DIM),
        'nsa_cmp_w2_k': nrm((L, CMP_HIDDEN, HEAD_DIM), CMP_HIDDEN),
        'nsa_cmp_pos_v': 0.1 * jax.random.normal(next(ks), (L, CMP_LEN, HEAD_DIM), f32),
        'nsa_cmp_w1_v': nrm((L, CMP_LEN * HEAD_DIM, CMP_HIDDEN), CMP_LEN * HEAD_DIM),
        'nsa_cmp_w2_v': nrm((L, CMP_HIDDEN, HEAD_DIM), CMP_HIDDEN),
        'w_out': nrm((L, D_MIX, D_MODEL), D_MIX),
        'ffn2_norm': gain((L, D_MODEL)),
        'ffn2_w_gate': nrm((L, D_MODEL, D_FF), D_MODEL),
        'ffn2_w_up': nrm((L, D_MODEL, D_FF), D_MODEL),
        'ffn2_w_down': nrm((L, D_FF, D_MODEL), D_FF),
        'final_norm': gain((D_MODEL,)),
    }


def reference(x, ffn1_norm, ffn1_w_gate, ffn1_w_up, ffn1_w_down, mix_norm, w_in,
              mla_q_norm, mla_w_uq, mla_kv_norm, mla_w_ukv,
              nsa_gate_bias, nsa_cmp_pos_k, nsa_cmp_w1_k, nsa_cmp_w2_k,
              nsa_cmp_pos_v, nsa_cmp_w1_v, nsa_cmp_w2_v, w_out,
              ffn2_norm, ffn2_w_gate, ffn2_w_up, ffn2_w_down, final_norm):
    b, s, _ = x.shape
    pos = jnp.arange(s, dtype=jnp.int32)
    split_at = [int(c) for c in np.cumsum(IN_WIDTHS)[:-1]]
    for l in range(DEPTH):
        x = x + 0.5 * swiglu(rmsnorm(x, ffn1_norm[l]), ffn1_w_gate[l], ffn1_w_up[l], ffn1_w_down[l])
        h = rmsnorm(x, mix_norm[l])
        (c_q, c_kv, k_rope, n_q, n_kc, n_vc, n_ks, n_vs, n_kw, n_vw, n_gate,
         sb_q, sb_k, sb_v) = jnp.split(h @ w_in[l], split_at, axis=-1)
        o_mla = mla_attention(c_q, c_kv, k_rope, mla_q_norm[l], mla_w_uq[l],
                              mla_kv_norm[l], mla_w_ukv[l], pos)
        gates = jax.nn.sigmoid(n_gate + nsa_gate_bias[l]).reshape(b, s, NSA_HEADS, NSA_BRANCHES)
        o_nsa = nsa_attention(split_heads(n_q, NSA_HEADS),
                              split_heads(n_kc, NSA_KV_HEADS), split_heads(n_vc, NSA_KV_HEADS),
                              split_heads(n_ks, NSA_KV_HEADS), split_heads(n_vs, NSA_KV_HEADS),
                              split_heads(n_kw, NSA_KV_HEADS), split_heads(n_vw, NSA_KV_HEADS),
                              gates, nsa_cmp_pos_k[l], nsa_cmp_w1_k[l], nsa_cmp_w2_k[l],
                              nsa_cmp_pos_v[l], nsa_cmp_w1_v[l], nsa_cmp_w2_v[l], pos)
        o_sb = stick_breaking_attention(split_heads(sb_q, SB_HEADS), split_heads(sb_k, SB_HEADS),
                                        split_heads(sb_v, SB_HEADS))
        o = jnp.concatenate([o_mla.reshape(b, s, -1), o_nsa.reshape(b, s, -1),
                             o_sb.reshape(b, s, -1)], axis=-1)
        x = x + o @ w_out[l]
        x = x + 0.5 * swiglu(rmsnorm(x, ffn2_norm[l]), ffn2_w_gate[l], ffn2_w_up[l], ffn2_w_down[l])
    return rmsnorm(x, final_norm)
```

```python
import functools
import math

import numpy as np
import jax
import jax.numpy as jnp
from jax import lax
from jax.experimental import pallas as pl
from jax.experimental.pallas import tpu as pltpu

D_MODEL = 1024
HEAD_DIM = 64
MLA_HEADS = 6
MLA_NOPE = 64
MLA_ROPE = 32
MLA_V = 64
MLA_Q_LORA = 256
MLA_KV_LORA = 128
NSA_HEADS = 6
NSA_KV_HEADS = 2
NSA_GROUP = NSA_HEADS // NSA_KV_HEADS
NSA_BRANCHES = 3
CMP_LEN = 32
CMP_STRIDE = 16
CMP_HIDDEN = 128
SEL_LEN = 64
SEL_TOPK = 16
WINDOW = 512
SB_HEADS = 4
D_FF = 2816
ROPE_THETA = 500000.0
PARTIAL_ROT = HEAD_DIM // 4
EPS = 1e-6
NEG_INF = -1e30
FORCE_SCORE = 1e4
N_GATES = NSA_HEADS * NSA_BRANCHES

LANES = 128
VMEM_LIMIT = 56 * 1024 * 1024

_MXU = jnp.bfloat16
_F32 = jnp.float32

_IN_WIDTHS = (MLA_Q_LORA, MLA_KV_LORA, MLA_ROPE, NSA_HEADS * HEAD_DIM) + (NSA_KV_HEADS * HEAD_DIM,) * 6 + (
    N_GATES, SB_HEADS * HEAD_DIM, SB_HEADS * HEAD_DIM, SB_HEADS * HEAD_DIM)
_IN_OFF = np.concatenate([[0], np.cumsum(_IN_WIDTHS)])
(_O_CQ, _O_CKV, _O_KR, _O_NQ, _O_NKC, _O_NVC, _O_NKS, _O_NVS, _O_NKW, _O_NVW, _O_GATE, _O_SBQ, _O_SBK,
 _O_SBV) = [int(v) for v in _IN_OFF[:-1]]

_S_CQ, _S_CKV, _S_KR, _S_KRS = 0, 2, 3, 4
_S_NQ, _S_NQS = 5, 11
_S_KC, _S_KCS, _S_VC = 17, 19, 21
_S_KS, _S_KSS, _S_VS = 23, 25, 27
_S_KW, _S_KWS, _S_VW = 29, 31, 33
_S_GATE = 35
_S_SBQ, _S_SBK, _S_SBV = 36, 40, 44
_N_SLOTS = 48


def _dot(a, b):
    return jnp.dot(a.astype(_MXU), b.astype(_MXU), preferred_element_type=_F32)


def _dot_nt(a, b):
    return lax.dot_general(a.astype(_MXU), b.astype(_MXU), (((1,), (1,)), ((), ())),
                           preferred_element_type=_F32)


def _dot_split(a, b):
    hi = a.astype(_MXU)
    lo = (a - hi.astype(_F32)).astype(_MXU)
    return (jnp.dot(hi, b, preferred_element_type=_F32) + jnp.dot(lo, b, preferred_element_type=_F32))


def _rms(x, g):
    return x * lax.rsqrt(jnp.mean(x * x, axis=-1, keepdims=True) + EPS) * g


def _params(*sem):
    return pltpu.CompilerParams(dimension_semantics=sem, vmem_limit_bytes=VMEM_LIMIT)


def _ffn_kernel(x_ref, g_ref, wg_ref, wu_ref, wd_ref, fg_ref, o_ref, h_ref, acc_ref, *, final_norm):
    j = pl.program_id(1)

    @pl.when(j == 0)
    def _():
        h_ref[...] = _rms(x_ref[...], g_ref[...]).astype(h_ref.dtype)
        acc_ref[...] = jnp.zeros_like(acc_ref)

    h = h_ref[...]
    gate = jnp.dot(h, wg_ref[...], preferred_element_type=_F32)
    up = jnp.dot(h, wu_ref[...], preferred_element_type=_F32)
    act = gate * jax.nn.sigmoid(gate) * up
    acc_ref[...] += _dot(act, wd_ref[...])

    @pl.when(j == pl.num_programs(1) - 1)
    def _():
        y = x_ref[...] + 0.5 * acc_ref[...]
        if final_norm:
            y = _rms(y, fg_ref[...])
        o_ref[...] = y


def _ffn(x2d, g, wg, wu, wd, fg, final_norm):
    rows = x2d.shape[0]
    tm = min(1024, rows)
    tf = 256
    grid = (rows // tm, D_FF // tf)
    return pl.pallas_call(
        functools.partial(_ffn_kernel, final_norm=final_norm),
        grid=grid,
        in_specs=[
            pl.BlockSpec((tm, D_MODEL), lambda i, j: (i, 0)),
            pl.BlockSpec((1, D_MODEL), lambda i, j: (0, 0)),
            pl.BlockSpec((D_MODEL, tf), lambda i, j: (0, j)),
            pl.BlockSpec((D_MODEL, tf), lambda i, j: (0, j)),
            pl.BlockSpec((tf, D_MODEL), lambda i, j: (j, 0)),
            pl.BlockSpec((1, D_MODEL), lambda i, j: (0, 0)),
        ],
        out_specs=pl.BlockSpec((tm, D_MODEL), lambda i, j: (i, 0)),
        out_shape=jax.ShapeDtypeStruct((rows, D_MODEL), _F32),
        scratch_shapes=[pltpu.VMEM((tm, D_MODEL), _MXU), pltpu.VMEM((tm, D_MODEL), _F32)],
        compiler_params=_params("parallel", "arbitrary"),
        name="ffn",
    )(x2d, g, wg, wu, wd, fg)


def _proj_kernel(x_ref, g_ref, w_ref, qn_ref, wuq_ref, wuqs_ref, kvn_ref, wuk_ref, wuv_ref, gb_ref,
                 cq_ref, sq_ref, ck_ref, sk_ref, c64_ref, s64_ref,
                 mq_ref, mk_ref, mv_ref, nq_ref, nkc_ref, nvc_ref, nks_ref, nvs_ref, nkw_ref, nvw_ref,
                 gate_ref, sbq_ref, sbk_ref, sbv_ref):
    hn = _rms(x_ref[0], g_ref[...]).astype(_MXU)

    def proj(s0, s1):
        return jnp.dot(hn, w_ref[:, s0 * LANES:s1 * LANES], preferred_element_type=_F32)

    def slot(p, s):
        return p[:, s * LANES:(s + 1) * LANES]

    def head(p, s):
        return p[:, s * LANES:s * LANES + HEAD_DIM]

    p = proj(_S_CQ, _S_KRS + 1)
    cq = _rms(p[:, :MLA_Q_LORA], qn_ref[...])
    ckv = _rms(slot(p, _S_CKV), kvn_ref[...])
    q = _dot(cq, wuq_ref[...])
    q_partner = _dot(cq, wuqs_ref[...])
    kpe = slot(p, _S_KR) * ck_ref[...] + slot(p, _S_KRS) * sk_ref[...]
    kn = _dot(ckv, wuk_ref[...])
    v = _dot(ckv, wuv_ref[...])
    for h in range(MLA_HEADS):
        mq_ref[0, h] = (slot(q, h) * cq_ref[...] + slot(q_partner, h) * sq_ref[...]).astype(mq_ref.dtype)
        mk_ref[0, h] = (slot(kn, h) + kpe).astype(mk_ref.dtype)
        mv_ref[0, h] = head(v, h).astype(mv_ref.dtype)

    c64 = c64_ref[...]
    s64 = s64_ref[...]
    scale = HEAD_DIM ** -0.5

    p = proj(_S_NQ, _S_KC)
    for h in range(NSA_HEADS):
        nq_ref[0, h] = ((head(p, h) * c64 + head(p, NSA_HEADS + h) * s64) * scale).astype(nq_ref.dtype)

    p = proj(_S_KC, _S_GATE + 1)
    base = _S_KC
    for k_ref, v_ref, sk, sks, sv in ((nkc_ref, nvc_ref, _S_KC, _S_KCS, _S_VC),
                                      (nks_ref, nvs_ref, _S_KS, _S_KSS, _S_VS),
                                      (nkw_ref, nvw_ref, _S_KW, _S_KWS, _S_VW)):
        for g in range(NSA_KV_HEADS):
            k_ref[0, g] = (head(p, sk - base + g) * c64 + head(p, sks - base + g) * s64).astype(k_ref.dtype)
            v_ref[0, g] = head(p, sv - base + g).astype(v_ref.dtype)
    gate_ref[0] = jax.nn.sigmoid(p[:, (_S_GATE - base) * LANES:(_S_GATE - base) * LANES + N_GATES] + gb_ref[...])

    p = proj(_S_SBQ, _N_SLOTS)
    for h in range(SB_HEADS):
        sbq_ref[0, h] = (head(p, h) * scale).astype(sbq_ref.dtype)
        sbk_ref[0, h] = head(p, SB_HEADS + h).astype(sbk_ref.dtype)
        sbv_ref[0, h] = head(p, 2 * SB_HEADS + h).astype(sbv_ref.dtype)


def _proj(x, g, w_ext, qn, wuq, wuqs, kvn, wuk, wuv, gb, tabs):
    b, s, _ = x.shape
    ts = min(256, s)
    cq, sq, ck, sk, c64, s64 = tabs

    def full(a):
        return pl.BlockSpec(a.shape, lambda bi, i: (0,) * a.ndim)

    def tab(a):
        return pl.BlockSpec((ts, a.shape[1]), lambda bi, i: (i, 0))

    def heads(n, d):
        return (pl.BlockSpec((1, n, ts, d), lambda bi, i: (bi, 0, i, 0)),
                jax.ShapeDtypeStruct((b, n, s, d), _MXU))

    outs = [heads(MLA_HEADS, LANES), heads(MLA_HEADS, LANES), heads(MLA_HEADS, MLA_V),
            heads(NSA_HEADS, HEAD_DIM)] + [heads(NSA_KV_HEADS, HEAD_DIM) for _ in range(6)] + [
        (pl.BlockSpec((1, ts, N_GATES), lambda bi, i: (bi, i, 0)), jax.ShapeDtypeStruct((b, s, N_GATES), _F32))
    ] + [heads(SB_HEADS, HEAD_DIM) for _ in range(3)]
    return pl.pallas_call(
        _proj_kernel,
        grid=(b, s // ts),
        in_specs=[pl.BlockSpec((1, ts, D_MODEL), lambda bi, i: (bi, i, 0)), full(g), full(w_ext), full(qn),
                  full(wuq), full(wuqs), full(kvn), full(wuk), full(wuv), full(gb),
                  tab(cq), tab(sq), tab(ck), tab(sk), tab(c64), tab(s64)],
        out_specs=[o[0] for o in outs],
        out_shape=[o[1] for o in outs],
        compiler_params=_params("parallel", "parallel"),
        name="proj",
    )(x, g, w_ext, qn, wuq, wuqs, kvn, wuk, wuv, gb, cq, sq, ck, sk, c64, s64)


def _softmax_step(carry, s, v):
    m, l, acc = carry
    m_new = jnp.maximum(m, jnp.max(s, axis=-1, keepdims=True))
    alpha = jnp.exp(m - m_new)
    p = jnp.exp(s - m_new)
    l = alpha * l + jnp.sum(p, axis=-1, keepdims=True)
    acc = alpha * acc + _dot(p, v)
    return m_new, l, acc


def _softmax_init(rows, d):
    return (jnp.full((rows, 1), NEG_INF, _F32), jnp.zeros((rows, 1), _F32), jnp.zeros((rows, d), _F32))


def _mla_kernel(q_ref, k_ref, v_ref, o_ref, *, t):
    qi = pl.program_id(2)
    q = q_ref[0, 0]

    def step(j, carry, diag):
        off = pl.multiple_of(j * t, t)
        s = _dot_nt(q, k_ref[0, 0, pl.ds(off, t), :])
        if diag:
            row = lax.broadcasted_iota(jnp.int32, (t, t), 0)
            col = lax.broadcasted_iota(jnp.int32, (t, t), 1)
            s = jnp.where(col <= row, s, NEG_INF)
        return _softmax_step(carry, s, v_ref[0, 0, pl.ds(off, t), :])

    carry = lax.fori_loop(0, qi, functools.partial(step, diag=False), _softmax_init(t, MLA_V))
    _, l, acc = step(qi, carry, True)
    o_ref[0, 0] = (acc / l).astype(o_ref.dtype)


def _mla_attention(q, k, v):
    b, h, s, _ = q.shape
    t = min(512, s)
    return pl.pallas_call(
        functools.partial(_mla_kernel, t=t),
        grid=(b, h, s // t),
        in_specs=[pl.BlockSpec((1, 1, t, LANES), lambda bi, hi, i: (bi, hi, i, 0)),
                  pl.BlockSpec((1, 1, s, LANES), lambda bi, hi, i: (bi, hi, 0, 0)),
                  pl.BlockSpec((1, 1, s, MLA_V), lambda bi, hi, i: (bi, hi, 0, 0))],
        out_specs=pl.BlockSpec((1, 1, t, MLA_V), lambda bi, hi, i: (bi, hi, i, 0)),
        out_shape=jax.ShapeDtypeStruct((b, h, s, MLA_V), _MXU),
        compiler_params=_params("parallel", "parallel", "arbitrary"),
        name="mla_attn",
    )(q, k, v)


def _sb_kernel(q_ref, k_ref, v_ref, u_ref, o_ref, *, t):
    qi = pl.program_id(2)
    q = q_ref[0, 0]
    u = u_ref[...]

    def step(j, carry, diag):
        rem, acc = carry
        off = pl.multiple_of(j * t, t)
        z = _dot_nt(q, k_ref[0, 0, pl.ds(off, t), :])
        log_beta = jnp.minimum(z, 0.0) - jnp.log1p(jnp.exp(-jnp.abs(z)))
        log_rem = log_beta - z
        if diag:
            row = lax.broadcasted_iota(jnp.int32, (t, t), 0)
            col = lax.broadcasted_iota(jnp.int32, (t, t), 1)
            strict = col < row
            log_rem = jnp.where(strict, log_rem, 0.0)
        suffix = _dot_split(log_rem, u)
        a = jnp.exp(log_beta + suffix + rem)
        if diag:
            a = jnp.where(strict, a, 0.0)
        acc = acc + _dot(a, v_ref[0, 0, pl.ds(off, t), :])
        rem = rem + suffix[:, 0:1] + log_rem[:, 0:1]
        return rem, acc

    carry = step(qi, (jnp.zeros((t, 1), _F32), jnp.zeros((t, HEAD_DIM), _F32)), True)
    _, acc = lax.fori_loop(0, qi, lambda i, c: step(qi - 1 - i, c, False), carry)
    o_ref[0, 0] = acc.astype(o_ref.dtype)


def _sb_attention(q, k, v):
    b, h, s, d = q.shape
    t = min(256, s)
    idx = np.arange(t)
    u = jnp.asarray(idx[:, None] > idx[None, :], _MXU)
    return pl.pallas_call(
        functools.partial(_sb_kernel, t=t),
        grid=(b, h, s // t),
        in_specs=[pl.BlockSpec((1, 1, t, d), lambda bi, hi, i: (bi, hi, i, 0)),
                  pl.BlockSpec((1, 1, s, d), lambda bi, hi, i: (bi, hi, 0, 0)),
                  pl.BlockSpec((1, 1, s, d), lambda bi, hi, i: (bi, hi, 0, 0)),
                  pl.BlockSpec((t, t), lambda bi, hi, i: (0, 0))],
        out_specs=pl.BlockSpec((1, 1, t, d), lambda bi, hi, i: (bi, hi, i, 0)),
        out_shape=jax.ShapeDtypeStruct((b, h, s, d), _MXU),
        compiler_params=_params("parallel", "parallel", "arbitrary"),
        name="sb_attn",
    )(q, k, v, u)


def _compress_kernel(xk_ref, xv_ref, w1k_ref, w2k_ref, pk_ref, w1v_ref, w2v_ref, pv_ref, ok_ref, ov_ref):
    def one(x_ref, w1_ref, w2_ref, p_ref, o_ref):
        x = x_ref[0, 0]
        n = x.shape[0]
        first = jnp.dot(x, w1_ref[0], preferred_element_type=_F32)
        second = jnp.dot(x, w1_ref[1], preferred_element_type=_F32)
        pos = _dot(p_ref[0], w1_ref[0]) + _dot(p_ref[1], w1_ref[1])
        hid = first + pltpu.roll(second, n - 1, 0) + pos[0:1]
        hid = 0.5 * hid * (1.0 + jnp.tanh(math.sqrt(2.0 / math.pi) * (hid + 0.044715 * hid * hid * hid)))
        o_ref[0, 0] = _dot(hid, w2_ref[...]).astype(o_ref.dtype)

    one(xk_ref, w1k_ref, w2k_ref, pk_ref, ok_ref)
    one(xv_ref, w1v_ref, w2v_ref, pv_ref, ov_ref)


def _compress(xk, xv, w1k, w2k, pk, w1v, w2v, pv):
    b, g, s, d = xk.shape
    n = s // CMP_STRIDE
    xk = xk.reshape(b, g, n, CMP_STRIDE * d)
    xv = xv.reshape(b, g, n, CMP_STRIDE * d)

    def full(a):
        return pl.BlockSpec(a.shape, lambda bi, gi: (0,) * a.ndim)

    xspec = pl.BlockSpec((1, 1, n, CMP_STRIDE * d), lambda bi, gi: (bi, gi, 0, 0))
    ospec = pl.BlockSpec((1, 1, n, d), lambda bi, gi: (bi, gi, 0, 0))
    oshape = jax.ShapeDtypeStruct((b, g, n, d), _MXU)
    return pl.pallas_call(
        _compress_kernel,
        grid=(b, g),
        in_specs=[xspec, xspec, full(w1k), full(w2k), full(pk), full(w1v), full(w2v), full(pv)],
        out_specs=[ospec, ospec],
        out_shape=[oshape, oshape],
        compiler_params=_params("parallel", "parallel"),
        name="nsa_compress",
    )(xk, xv, w1k, w2k, pk, w1v, w2v, pv)


def _cmp_kernel(q_ref, kc_ref, vc_ref, ov_ref, o_ref, sel_ref, *, tq, n_top):
    q0 = pl.program_id(2) * tq
    kc = kc_ref[0, 0]
    vc = vc_ref[0, 0]
    ncp = kc.shape[0]
    ns = sel_ref.shape[-1]
    qpos = q0 + lax.broadcasted_iota(jnp.int32, (tq, 1), 0)
    cmp_end = lax.broadcasted_iota(jnp.int32, (1, ncp), 1) * CMP_STRIDE + (CMP_LEN - 1)
    cmask = cmp_end <= qpos
    any_valid = qpos >= CMP_LEN - 1
    p_sum = jnp.zeros((tq, ncp), _F32)
    for r in range(NSA_GROUP):
        s = jnp.where(cmask, _dot_nt(q_ref[0, r], kc), NEG_INF)
        e = jnp.exp(s - jnp.max(s, axis=-1, keepdims=True))
        p = e / jnp.sum(e, axis=-1, keepdims=True)
        p = jnp.where(any_valid, p, 0.0)
        o_ref[0, r] = _dot(p, vc)
        p_sum = p_sum + p
    score = _dot_split(p_sum, ov_ref[...])
    cur = jnp.right_shift(qpos, int(math.log2(SEL_LEN)))
    blk =lax.broadcasted_iota(jnp.int32, (1, ns), 1)
    forced = (blk == 0) | (blk == cur) | (blk == cur - 1)
    score = jnp.where(forced, FORCE_SCORE, jnp.where(blk > cur, -1.0, score))
    blk_f = blk.astype(_F32)
    sel = jnp.zeros((tq, ns), _F32)
    for _ in range(n_top):
        top = jnp.max(score, axis=-1, keepdims=True)
        first = jnp.min(jnp.where(score == top, blk_f, float(ns)), axis=-1, keepdims=True)
        pick = blk_f == first
        sel = jnp.where(pick, 1.0, sel)
        score = jnp.where(pick, -3e38, score)
    sel_ref[0, 0] = sel.astype(sel_ref.dtype)


def _cmp_select(q, kc, vc):
    b, h, s, d = q.shape
    g = kc.shape[1]
    ncp = kc.shape[2]
    ns = s // SEL_LEN
    n_top = min(SEL_TOPK, ns)
    tq = min(128, s)
    c0 = np.arange(ncp)[:, None] * CMP_STRIDE
    n0 = np.arange(ns)[None, :] * SEL_LEN
    overlap = jnp.asarray((c0 < n0 + SEL_LEN) & (c0 + CMP_LEN > n0), _MXU)
    return pl.pallas_call(
        functools.partial(_cmp_kernel, tq=tq, n_top=n_top),
        grid=(b, g, s // tq),
        in_specs=[pl.BlockSpec((1, NSA_GROUP, tq, d), lambda bi, gi, i: (bi, gi, i, 0)),
                  pl.BlockSpec((1, 1, ncp, d), lambda bi, gi, i: (bi, gi, 0, 0)),
                  pl.BlockSpec((1, 1, ncp, d), lambda bi, gi, i: (bi, gi, 0, 0)),
                  pl.BlockSpec((ncp, ns), lambda bi, gi, i: (0, 0))],
        out_specs=[pl.BlockSpec((1, NSA_GROUP, tq, d), lambda bi, gi, i: (bi, gi, i, 0)),
                   pl.BlockSpec((1, 1, tq, ns), lambda bi, gi, i: (bi, gi, i, 0))],
        out_shape=[jax.ShapeDtypeStruct((b, h, s, d), _F32), jax.ShapeDtypeStruct((b, g, s, ns), _MXU)],
        compiler_params=_params("parallel", "parallel", "arbitrary"),
        name="nsa_cmp_select",
    )(q, kc, vc, overlap)


def _sel_kernel(q_ref, k_ref, v_ref, sel_ref, e_ref, o_ref, *, tq, tk):
    q0 = pl.program_id(2) * tq
    sel = sel_ref[0, 0]
    last = (q0 + tq - 1) // tk

    def step(j, carry, causal):
        off = pl.multiple_of(j * tk, tk)
        k = k_ref[0, 0, pl.ds(off, tk), :]
        v = v_ref[0, 0, pl.ds(off, tk), :]
        chosen = jnp.dot(sel, e_ref[j], preferred_element_type=_F32) > 0.5
        if causal:
            row = q0 + lax.broadcasted_iota(jnp.int32, (tq, tk), 0)
            col = j * tk + lax.broadcasted_iota(jnp.int32, (tq, tk), 1)
            keep = col <= row
        out = []
        for r in range(NSA_GROUP):
            s = jnp.where(chosen, _dot_nt(q_ref[0, r], k), NEG_INF)
            if causal:
                s = jnp.where(keep, s, NEG_INF)
            out.append(_softmax_step(carry[r], s, v))
        return tuple(out)

    init = tuple(_softmax_init(tq, HEAD_DIM) for _ in range(NSA_GROUP))
    carry = lax.fori_loop(0, last, functools.partial(step, causal=False), init)
    carry = step(last, carry, True)
    for r in range(NSA_GROUP):
        _, l, acc = carry[r]
        o_ref[0, r] = acc / l


def _sel_attention(q, k, v, sel):
    b, h, s, d = q.shape
    g = k.shape[1]
    ns = sel.shape[-1]
    tq = min(128, s)
    tk = min(512, s)
    tok_blk = np.arange(s) // SEL_LEN
    expand = (np.arange(ns)[:, None] == tok_blk[None, :]).reshape(ns, s // tk, tk).transpose(1, 0, 2)
    expand = jnp.asarray(expand, _MXU)
    return pl.pallas_call(
        functools.partial(_sel_kernel, tq=tq, tk=tk),
        grid=(b, g, s // tq),
        in_specs=[pl.BlockSpec((1, NSA_GROUP, tq, d), lambda bi, gi, i: (bi, gi, i, 0)),
                  pl.BlockSpec((1, 1, s, d), lambda bi, gi, i: (bi, gi, 0, 0)),
                  pl.BlockSpec((1, 1, s, d), lambda bi, gi, i: (bi, gi, 0, 0)),
                  pl.BlockSpec((1, 1, tq, ns), lambda bi, gi, i: (bi, gi, i, 0)),
                  pl.BlockSpec((s // tk, ns, tk), lambda bi, gi, i: (0, 0, 0))],
        out_specs=pl.BlockSpec((1, NSA_GROUP, tq, d), lambda bi, gi, i: (bi, gi, i, 0)),
        out_shape=jax.ShapeDtypeStruct((b, h, s, d), _F32),
        compiler_params=_params("parallel", "parallel", "arbitrary"),
        name="nsa_selected",
    )(q, k, v, sel, expand)


def _win_kernel(q_ref, k_ref, v_ref, o_ref, *, tq, span):
    q0 = pl.program_id(2) * tq
    start = pl.multiple_of(jnp.maximum(q0 - WINDOW, 0), tq)
    k = k_ref[0, 0, pl.ds(start, span), :]
    v = v_ref[0, 0, pl.ds(start, span), :]
    row = q0 + lax.broadcasted_iota(jnp.int32, (tq, span), 0)
    col = start + lax.broadcasted_iota(jnp.int32, (tq, span), 1)
    lo = row - WINDOW
    for r in range(NSA_GROUP):
        s = _dot_nt(q_ref[0, r], k)
        s = jnp.where(col <= row, s, NEG_INF)
        s = jnp.where(col > lo, s, NEG_INF)
        e = jnp.exp(s - jnp.max(s, axis=-1, keepdims=True))
        o_ref[0, r] = _dot(e, v) / jnp.sum(e, axis=-1, keepdims=True)


def _win_attention(q, k, v):
    b, h, s, d = q.shape
    g = k.shape[1]
    tq = min(128, s)
    span = min(WINDOW + tq, s)
    return pl.pallas_call(
        functools.partial(_win_kernel, tq=tq, span=span),
        grid=(b, g, s // tq),
        in_specs=[pl.BlockSpec((1, NSA_GROUP, tq, d), lambda bi, gi, i: (bi, gi, i, 0)),
                  pl.BlockSpec((1, 1, s, d), lambda bi, gi, i: (bi, gi, 0, 0)),
                  pl.BlockSpec((1, 1, s, d), lambda bi, gi, i: (bi, gi, 0, 0))],
        out_specs=pl.BlockSpec((1, NSA_GROUP, tq, d), lambda bi, gi, i: (bi, gi, i, 0)),
        out_shape=jax.ShapeDtypeStruct((b, h, s, d), _F32),
        compiler_params=_params("parallel", "parallel", "arbitrary"),
        name="nsa_window",
    )(q, k, v)


def _out_kernel(x_ref, mla_ref, cmp_ref, sel_ref, win_ref, gate_ref, sb_ref, w_ref, o_ref):
    acc = x_ref[0]
    for h in range(MLA_HEADS):
        acc = acc + jnp.dot(mla_ref[0, h], w_ref[h], preferred_element_type=_F32)
    gate = gate_ref[0]
    for h in range(NSA_HEADS):
        c = NSA_BRANCHES * h
        o = (gate[:, c:c + 1] * cmp_ref[0, h] + gate[:, c + 1:c + 2] * sel_ref[0, h]
             + gate[:, c + 2:c + 3] * win_ref[0, h])
        acc = acc + _dot(o, w_ref[MLA_HEADS + h])
    for h in range(SB_HEADS):
        acc = acc + jnp.dot(sb_ref[0, h], w_ref[MLA_HEADS + NSA_HEADS + h], preferred_element_type=_F32)
    o_ref[0] = acc


def _out_proj(x, o_mla, o_cmp, o_sel, o_win, gates, o_sb, w_heads):
    b, s, _ = x.shape
    ts = min(512, s)

    def heads(a):
        return pl.BlockSpec((1, a.shape[1], ts, a.shape[3]), lambda bi, i: (bi, 0, i, 0))

    xspec = pl.BlockSpec((1, ts, D_MODEL), lambda bi, i: (bi, i, 0))
    return pl.pallas_call(
        _out_kernel,
        grid=(b, s // ts),
        in_specs=[xspec, heads(o_mla), heads(o_cmp), heads(o_sel), heads(o_win),
                  pl.BlockSpec((1, ts, N_GATES), lambda bi, i: (bi, i, 0)), heads(o_sb),
                  pl.BlockSpec(w_heads.shape, lambda bi, i: (0, 0, 0))],
        out_specs=xspec,
        out_shape=jax.ShapeDtypeStruct(x.shape, _F32),
        compiler_params=_params("parallel", "parallel"),
        name="out_proj",
    )(x, o_mla, o_cmp, o_sel, o_win, gates, o_sb, w_heads)


def _gather_cols(w, idx):
    idx = np.asarray(idx)
    cols = jnp.take(w, jnp.asarray(np.maximum(idx, 0)), axis=1)
    return jnp.where(jnp.asarray(idx >= 0)[None, :], cols, 0.0).astype(_MXU)


def _swap_halves(rot):
    return (np.arange(rot) + rot // 2) % rot


def _w_in_index():
    idx = np.full((_N_SLOTS * LANES,), -1, np.int64)

    def put(slot, lane, src):
        src = np.asarray(src)
        idx[slot * LANES + lane:slot * LANES + lane + len(src)] = src

    put(_S_CQ, 0, _O_CQ + np.arange(MLA_Q_LORA))
    put(_S_CKV, 0, _O_CKV + np.arange(MLA_KV_LORA))
    put(_S_KR, MLA_NOPE, _O_KR + np.arange(MLA_ROPE))
    put(_S_KRS, MLA_NOPE, _O_KR + _swap_halves(MLA_ROPE))
    for h in range(NSA_HEADS):
        put(_S_NQ + h, 0, _O_NQ + h * HEAD_DIM + np.arange(HEAD_DIM))
        put(_S_NQS + h, 0, _O_NQ + h * HEAD_DIM + _swap_halves(PARTIAL_ROT))
    for sk, sks, sv, ok, ov in ((_S_KC, _S_KCS, _S_VC, _O_NKC, _O_NVC), (_S_KS, _S_KSS, _S_VS, _O_NKS, _O_NVS),
                                (_S_KW, _S_KWS, _S_VW, _O_NKW, _O_NVW)):
        for g in range(NSA_KV_HEADS):
            put(sk + g, 0, ok + g * HEAD_DIM + np.arange(HEAD_DIM))
            put(sks + g, 0, ok + g * HEAD_DIM + _swap_halves(PARTIAL_ROT))
            put(sv + g, 0, ov + g * HEAD_DIM + np.arange(HEAD_DIM))
    put(_S_GATE, 0, _O_GATE + np.arange(N_GATES))
    for h in range(SB_HEADS):
        put(_S_SBQ + h, 0, _O_SBQ + h * HEAD_DIM + np.arange(HEAD_DIM))
        put(_S_SBK + h, 0, _O_SBK + h * HEAD_DIM + np.arange(HEAD_DIM))
        put(_S_SBV + h, 0, _O_SBV + h * HEAD_DIM + np.arange(HEAD_DIM))
    return idx


def _mla_up_index():
    qd = MLA_NOPE + MLA_ROPE
    kd = MLA_NOPE + MLA_V
    uq = np.full((MLA_HEADS * LANES,), -1, np.int64)
    uqs = uq.copy()
    uk = uq.copy()
    uv = uq.copy()
    for h in range(MLA_HEADS):
        uq[h * LANES:h * LANES + qd] = h * qd + np.arange(qd)
        uqs[h * LANES + MLA_NOPE:h * LANES + qd] = h * qd + MLA_NOPE + _swap_halves(MLA_ROPE)
        uk[h * LANES:h * LANES + MLA_NOPE] = h * kd + np.arange(MLA_NOPE)
        uv[h * LANES:h * LANES + MLA_V] = h * kd + MLA_NOPE + np.arange(MLA_V)
    return uq, uqs, uk, uv


def _rope_tables(s):
    pos = jnp.arange(s, dtype=_F32)

    def cs(rot):
        half = rot // 2
        inv_freq = ROPE_THETA ** (-jnp.arange(half, dtype=_F32) / half)
        ang = pos[:, None] * inv_freq[None, :]
        c, sn = jnp.cos(ang), jnp.sin(ang)
        return jnp.concatenate([c, c], axis=1), jnp.concatenate([-sn, sn], axis=1)

    c, sn = cs(MLA_ROPE)
    ones = jnp.ones((s, MLA_NOPE), _F32)
    zeros = jnp.zeros((s, MLA_NOPE), _F32)
    pad = jnp.zeros((s, LANES - MLA_NOPE - MLA_ROPE), _F32)
    ck = jnp.concatenate([ones, c, pad], axis=1)
    sk = jnp.concatenate([zeros, sn, pad], axis=1)
    q_scale = (MLA_NOPE + MLA_ROPE) ** -0.5
    c, sn = cs(PARTIAL_ROT)
    c64 = jnp.concatenate([c, jnp.ones((s, HEAD_DIM - PARTIAL_ROT), _F32)], axis=1)
    s64 = jnp.concatenate([sn, jnp.zeros((s, HEAD_DIM - PARTIAL_ROT), _F32)], axis=1)
    return ck * q_scale, sk * q_scale, ck, sk, c64, s64


def kernel(x, ffn1_norm, ffn1_w_gate, ffn1_w_up, ffn1_w_down, mix_norm, w_in, mla_q_norm, mla_w_uq, mla_kv_norm,
           mla_w_ukv, nsa_gate_bias, nsa_cmp_pos_k, nsa_cmp_w1_k, nsa_cmp_w2_k, nsa_cmp_pos_v, nsa_cmp_w1_v,
           nsa_cmp_w2_v, w_out, ffn2_norm, ffn2_w_gate, ffn2_w_up, ffn2_w_down, final_norm):
    b, s, d = x.shape
    depth = w_in.shape[0]
    tabs = _rope_tables(s)
    in_idx = _w_in_index()
    uq_idx, uqs_idx, uk_idx, uv_idx = _mla_up_index()
    half = CMP_LEN * HEAD_DIM // 2
    fg = final_norm.reshape(1, d)

    def cmp_weights(w1, w2, pos):
        pos = jnp.broadcast_to(pos.reshape(2, 1, half), (2, 8, half)).astype(_MXU)
        return w1.reshape(2, half, CMP_HIDDEN).astype(_MXU), w2.astype(_MXU), pos

    for l in range(depth):
        x2d = _ffn(x.reshape(b * s, d), ffn1_norm[l].reshape(1, d), ffn1_w_gate[l].astype(_MXU),
                   ffn1_w_up[l].astype(_MXU), ffn1_w_down[l].astype(_MXU), fg, False)
        x = x2d.reshape(b, s, d)
        (mq, mk, mv, nq, nkc, nvc, nks, nvs, nkw, nvw, gates, sbq, sbk, sbv) = _proj(
            x, mix_norm[l].reshape(1, d), _gather_cols(w_in[l], in_idx),
            mla_q_norm[l].reshape(1, -1), _gather_cols(mla_w_uq[l], uq_idx), _gather_cols(mla_w_uq[l], uqs_idx),
            mla_kv_norm[l].reshape(1, -1), _gather_cols(mla_w_ukv[l], uk_idx), _gather_cols(mla_w_ukv[l], uv_idx),
            nsa_gate_bias[l].reshape(1, -1), tabs)
        o_mla = _mla_attention(mq, mk, mv)
        kc, vc = _compress(nkc, nvc, *cmp_weights(nsa_cmp_w1_k[l], nsa_cmp_w2_k[l], nsa_cmp_pos_k[l]),
                           *cmp_weights(nsa_cmp_w1_v[l], nsa_cmp_w2_v[l], nsa_cmp_pos_v[l]))
        o_cmp, sel = _cmp_select(nq, kc, vc)
        o_sel = _sel_attention(nq, nks, nvs, sel)
        o_win = _win_attention(nq, nkw, nvw)
        o_sb = _sb_attention(sbq, sbk, sbv)
        w_heads = w_out[l].reshape(MLA_HEADS + NSA_HEADS + SB_HEADS, HEAD_DIM, d).astype(_MXU)
        x = _out_proj(x, o_mla, o_cmp, o_sel, o_win, gates, o_sb, w_heads)
        x2d = _ffn(x.reshape(b * s, d), ffn2_norm[l].reshape(1, d), ffn2_w_gate[l].astype(_MXU),
                   ffn2_w_up[l].astype(_MXU), ffn2_w_down[l].astype(_MXU), fg, l == depth - 1)
        x = x2d.reshape(b, s, d)
    return x
```

```python
import functools
import math

import numpy as np
import jax
import jax.numpy as jnp
from jax import lax
from jax.experimental import pallas as pl
from jax.experimental.pallas import tpu as pltpu

D_MODEL = 1024
HEAD_DIM = 64
MLA_HEADS = 6
MLA_NOPE = 64
MLA_ROPE = 32
MLA_V = 64
MLA_Q_LORA = 256
MLA_KV_LORA = 128
NSA_HEADS = 6
NSA_KV_HEADS = 2
NSA_GROUP = NSA_HEADS // NSA_KV_HEADS
NSA_BRANCHES = 3
CMP_LEN = 32
CMP_STRIDE = 16
CMP_HIDDEN = 128
SEL_LEN = 64
SEL_TOPK = 16
WINDOW = 512
SB_HEADS = 4
D_FF = 2816
ROPE_THETA = 500000.0
PARTIAL_ROT = HEAD_DIM // 4
EPS = 1e-6
NEG_INF = -1e30
FORCE_SCORE = 1e4
F32_EXP_ZERO = -104.0
N_GATES = NSA_HEADS * NSA_BRANCHES

LANES = 128
VMEM_LIMIT = 56 * 1024 * 1024

_MXU = jnp.bfloat16
_F32 = jnp.float32

_IN_WIDTHS = (MLA_Q_LORA, MLA_KV_LORA, MLA_ROPE, NSA_HEADS * HEAD_DIM) + (NSA_KV_HEADS * HEAD_DIM,) * 6 + (
    N_GATES, SB_HEADS * HEAD_DIM, SB_HEADS * HEAD_DIM, SB_HEADS * HEAD_DIM)
_IN_OFF = np.concatenate([[0], np.cumsum(_IN_WIDTHS)])
(_O_CQ, _O_CKV, _O_KR, _O_NQ, _O_NKC, _O_NVC, _O_NKS, _O_NVS, _O_NKW, _O_NVW, _O_GATE, _O_SBQ, _O_SBK,
 _O_SBV) = [int(v) for v in _IN_OFF[:-1]]

_S_CQ, _S_CKV, _S_KR, _S_KRS = 0, 2, 3, 4
_S_NQ, _S_NQS = 5, 11
_S_KC, _S_KCS, _S_VC = 17, 19, 21
_S_KS, _S_KSS, _S_VS = 23, 25, 27
_S_KW, _S_KWS, _S_VW = 29, 31, 33
_S_GATE = 35
_S_SBQ, _S_SBK, _S_SBV = 36, 40, 44
_N_SLOTS = 48


def _dot(a, b):
    return jnp.dot(a.astype(_MXU), b.astype(_MXU), preferred_element_type=_F32)


def _dot_nt(a, b):
    return lax.dot_general(a.astype(_MXU), b.astype(_MXU), (((1,), (1,)), ((), ())),
                           preferred_element_type=_F32)


def _dot_split(a, b):
    hi = a.astype(_MXU)
    lo = (a - hi.astype(_F32)).astype(_MXU)
    return (jnp.dot(hi, b, preferred_element_type=_F32) + jnp.dot(lo, b, preferred_element_type=_F32))


def _rms(x, g):
    return x * lax.rsqrt(jnp.mean(x * x, axis=-1, keepdims=True) + EPS) * g


def _params(*sem):
    return pltpu.CompilerParams(dimension_semantics=sem, vmem_limit_bytes=VMEM_LIMIT)


def _ffn_kernel(x_ref, g_ref, wg_ref, wu_ref, wd_ref, fg_ref, o_ref, h_ref, acc_ref, *, final_norm):
    j = pl.program_id(1)

    @pl.when(j == 0)
    def _():
        h_ref[...] = _rms(x_ref[...], g_ref[...]).astype(h_ref.dtype)
        acc_ref[...] = jnp.zeros_like(acc_ref)

    h = h_ref[...]
    gate = jnp.dot(h, wg_ref[...], preferred_element_type=_F32)
    up = jnp.dot(h, wu_ref[...], preferred_element_type=_F32)
    act = gate * jax.nn.sigmoid(gate) * up
    acc_ref[...] += _dot(act, wd_ref[...])

    @pl.when(j == pl.num_programs(1) - 1)
    def _():
        y = x_ref[...] + 0.5 * acc_ref[...]
        if final_norm:
            y = _rms(y, fg_ref[...])
        o_ref[...] = y


def _ffn(x2d, g, wg, wu, wd, fg, final_norm):
    rows = x2d.shape[0]
    tm = min(1024, rows)
    tf = 256
    grid = (rows // tm, D_FF // tf)
    return pl.pallas_call(
        functools.partial(_ffn_kernel, final_norm=final_norm),
        grid=grid,
        in_specs=[
            pl.BlockSpec((tm, D_MODEL), lambda i, j: (i, 0)),
            pl.BlockSpec((1, D_MODEL), lambda i, j: (0, 0)),
            pl.BlockSpec((D_MODEL, tf), lambda i, j: (0, j)),
            pl.BlockSpec((D_MODEL, tf), lambda i, j: (0, j)),
            pl.BlockSpec((tf, D_MODEL), lambda i, j: (j, 0)),
            pl.BlockSpec((1, D_MODEL), lambda i, j: (0, 0)),
        ],
        out_specs=pl.BlockSpec((tm, D_MODEL), lambda i, j: (i, 0)),
        out_shape=jax.ShapeDtypeStruct((rows, D_MODEL), _F32),
        scratch_shapes=[pltpu.VMEM((tm, D_MODEL), _MXU), pltpu.VMEM((tm, D_MODEL), _F32)],
        compiler_params=_params("parallel", "arbitrary"),
        name="ffn",
    )(x2d, g, wg, wu, wd, fg)


def _proj_kernel(x_ref, g_ref, w_ref, qn_ref, wuq_ref, wuqs_ref, kvn_ref, wuk_ref, wuv_ref, gb_ref,
                 cq_ref, sq_ref, ck_ref, sk_ref, c64_ref, s64_ref,
                 mq_ref, mk_ref, mv_ref, nq_ref, nkc_ref, nvc_ref, nks_ref, nvs_ref, nkw_ref, nvw_ref,
                 gate_ref, sbq_ref, sbk_ref, sbv_ref):
    hn = _rms(x_ref[0], g_ref[...]).astype(_MXU)

    def proj(s0, s1):
        return jnp.dot(hn, w_ref[:, s0 * LANES:s1 * LANES], preferred_element_type=_F32)

    def slot(p, s):
        return p[:, s * LANES:(s + 1) * LANES]

    def head(p, s):
        return p[:, s * LANES:s * LANES + HEAD_DIM]

    p = proj(_S_CQ, _S_KRS + 1)
    cq = _rms(p[:, :MLA_Q_LORA], qn_ref[...])
    ckv = _rms(slot(p, _S_CKV), kvn_ref[...])
    q = _dot(cq, wuq_ref[...])
    q_partner = _dot(cq, wuqs_ref[...])
    kpe = slot(p, _S_KR) * ck_ref[...] + slot(p, _S_KRS) * sk_ref[...]
    kn = _dot(ckv, wuk_ref[...])
    v = _dot(ckv, wuv_ref[...])
    for h in range(MLA_HEADS):
        mq_ref[0, h] = (slot(q, h) * cq_ref[...] + slot(q_partner, h) * sq_ref[...]).astype(mq_ref.dtype)
        mk_ref[0, h] = (slot(kn, h) + kpe).astype(mk_ref.dtype)
        mv_ref[0, h] = head(v, h).astype(mv_ref.dtype)

    c64 = c64_ref[...]
    s64 = s64_ref[...]
    scale = HEAD_DIM ** -0.5

    p = proj(_S_NQ, _S_KC)
    for h in range(NSA_HEADS):
        nq_ref[0, h] = ((head(p, h) * c64 + head(p, NSA_HEADS + h) * s64) * scale).astype(nq_ref.dtype)

    p = proj(_S_KC, _S_GATE + 1)
    base = _S_KC
    for k_ref, v_ref, sk, sks, sv in ((nkc_ref, nvc_ref, _S_KC, _S_KCS, _S_VC),
                                      (nks_ref, nvs_ref, _S_KS, _S_KSS, _S_VS),
                                      (nkw_ref, nvw_ref, _S_KW, _S_KWS, _S_VW)):
        for g in range(NSA_KV_HEADS):
            k_ref[0, g] = (head(p, sk - base + g) * c64 + head(p, sks - base + g) * s64).astype(k_ref.dtype)
            v_ref[0, g] = head(p, sv - base + g).astype(v_ref.dtype)
    gate_ref[0] = jax.nn.sigmoid(p[:, (_S_GATE - base) * LANES:(_S_GATE - base) * LANES + N_GATES] + gb_ref[...])

    p = proj(_S_SBQ, _N_SLOTS)
    for h in range(SB_HEADS):
        sbq_ref[0, h] = (head(p, h) * scale).astype(sbq_ref.dtype)
        sbk_ref[0, h] = head(p, SB_HEADS + h).astype(sbk_ref.dtype)
        sbv_ref[0, h] = head(p, 2 * SB_HEADS + h).astype(sbv_ref.dtype)


def _proj(x, g, w_ext, qn, wuq, wuqs, kvn, wuk, wuv, gb, tabs):
    b, s, _ = x.shape
    ts = min(256, s)
    cq, sq, ck, sk, c64, s64 = tabs

    def full(a):
        return pl.BlockSpec(a.shape, lambda bi, i: (0,) * a.ndim)

    def tab(a):
        return pl.BlockSpec((ts, a.shape[1]), lambda bi, i: (i, 0))

    def heads(n, d):
        return (pl.BlockSpec((1, n, ts, d), lambda bi, i: (bi, 0, i, 0)),
                jax.ShapeDtypeStruct((b, n, s, d), _MXU))

    outs = [heads(MLA_HEADS, LANES), heads(MLA_HEADS, LANES), heads(MLA_HEADS, MLA_V),
            heads(NSA_HEADS, HEAD_DIM)] + [heads(NSA_KV_HEADS, HEAD_DIM) for _ in range(6)] + [
        (pl.BlockSpec((1, ts, N_GATES), lambda bi, i: (bi, i, 0)), jax.ShapeDtypeStruct((b, s, N_GATES), _F32))
    ] + [heads(SB_HEADS, HEAD_DIM) for _ in range(3)]
    return pl.pallas_call(
        _proj_kernel,
        grid=(b, s // ts),
        in_specs=[pl.BlockSpec((1, ts, D_MODEL), lambda bi, i: (bi, i, 0)), full(g), full(w_ext), full(qn),
                  full(wuq), full(wuqs), full(kvn), full(wuk), full(wuv), full(gb),
                  tab(cq), tab(sq), tab(ck), tab(sk), tab(c64), tab(s64)],
        out_specs=[o[0] for o in outs],
        out_shape=[o[1] for o in outs],
        compiler_params=_params("parallel", "parallel"),
        name="proj",
    )(x, g, w_ext, qn, wuq, wuqs, kvn, wuk, wuv, gb, cq, sq, ck, sk, c64, s64)


def _softmax_step(carry, s, v):
    m, l, acc = carry
    m_new = jnp.maximum(m, jnp.max(s, axis=-1, keepdims=True))
    alpha = jnp.exp(m - m_new)
    p = jnp.exp(s - m_new)
    l = alpha * l + jnp.sum(p, axis=-1, keepdims=True)
    acc = alpha * acc + _dot(p, v)
    return m_new, l, acc


def _softmax_init(rows, d):
    return (jnp.full((rows, 1), NEG_INF, _F32), jnp.zeros((rows, 1), _F32), jnp.zeros((rows, d), _F32))


def _mla_kernel(q_ref, k_ref, v_ref, o_ref, *, t):
    qi = pl.program_id(2)
    q = q_ref[0, 0]

    def step(j, carry, diag):
        off = pl.multiple_of(j * t, t)
        s = _dot_nt(q, k_ref[0, 0, pl.ds(off, t), :])
        if diag:
            row = lax.broadcasted_iota(jnp.int32, (t, t), 0)
            col = lax.broadcasted_iota(jnp.int32, (t, t), 1)
            s = jnp.where(col <= row, s, NEG_INF)
        return _softmax_step(carry, s, v_ref[0, 0, pl.ds(off, t), :])

    carry = lax.fori_loop(0, qi, functools.partial(step, diag=False), _softmax_init(t, MLA_V))
    _, l, acc = step(qi, carry, True)
    o_ref[0, 0] = (acc / l).astype(o_ref.dtype)


def _mla_attention(q, k, v):
    b, h, s, _ = q.shape
    t = min(512, s)
    return pl.pallas_call(
        functools.partial(_mla_kernel, t=t),
        grid=(b, h, s // t),
        in_specs=[pl.BlockSpec((1, 1, t, LANES), lambda bi, hi, i: (bi, hi, i, 0)),
                  pl.BlockSpec((1, 1, s, LANES), lambda bi, hi, i: (bi, hi, 0, 0)),
                  pl.BlockSpec((1, 1, s, MLA_V), lambda bi, hi, i: (bi, hi, 0, 0))],
        out_specs=pl.BlockSpec((1, 1, t, MLA_V), lambda bi, hi, i: (bi, hi, i, 0)),
        out_shape=jax.ShapeDtypeStruct((b, h, s, MLA_V), _MXU),
        compiler_params=_params("parallel", "parallel", "arbitrary"),
        name="mla_attn",
    )(q, k, v)


def _sb_kernel(q_ref, k_ref, v_ref, u_ref, o_ref, *, t):
    qi = pl.program_id(2)
    q = q_ref[0, 0]
    u = u_ref[...]

    def step(j, carry, diag):
        rem, acc = carry
        off = pl.multiple_of(j * t, t)
        z = _dot_nt(q, k_ref[0, 0, pl.ds(off, t), :])
        log_beta = jnp.minimum(z, 0.0) - jnp.log1p(jnp.exp(-jnp.abs(z)))
        log_rem = log_beta - z
        if diag:
            row = lax.broadcasted_iota(jnp.int32, (t, t), 0)
            col = lax.broadcasted_iota(jnp.int32, (t, t), 1)
            strict = col < row
            log_rem = jnp.where(strict, log_rem, 0.0)
        suffix = _dot_split(log_rem, u)
        a = jnp.exp(log_beta + suffix + rem)
        if diag:
            a = jnp.where(strict, a, 0.0)
        acc = acc + _dot(a, v_ref[0, 0, pl.ds(off, t), :])
        rem = rem + suffix[:, 0:1] + log_rem[:, 0:1]
        return rem, acc

    rem, acc = step(qi, (jnp.zeros((t, 1), _F32), jnp.zeros((t, HEAD_DIM), _F32)), True)

    def live(c):
        return jnp.logical_and(c[0] >= 0, jnp.max(c[1]) > F32_EXP_ZERO)

    def earlier(c):
        return (c[0] - 1,) + step(c[0], c[1:], False)

    _, _, acc = lax.while_loop(live, earlier, (qi - 1, rem, acc))
    o_ref[0, 0] = acc.astype(o_ref.dtype)


def _sb_attention(q, k, v):
    b, h, s, d = q.shape
    t = min(256, s)
    idx = np.arange(t)
    u = jnp.asarray(idx[:, None] > idx[None, :], _MXU)
    return pl.pallas_call(
        functools.partial(_sb_kernel, t=t),
        grid=(b, h, s // t),
        in_specs=[pl.BlockSpec((1, 1, t, d), lambda bi, hi, i: (bi, hi, i, 0)),
                  pl.BlockSpec((1, 1, s, d), lambda bi, hi, i: (bi, hi, 0, 0)),
                  pl.BlockSpec((1, 1, s, d), lambda bi, hi, i: (bi, hi, 0, 0)),
                  pl.BlockSpec((t, t), lambda bi, hi, i: (0, 0))],
        out_specs=pl.BlockSpec((1, 1, t, d), lambda bi, hi, i: (bi, hi, i, 0)),
        out_shape=jax.ShapeDtypeStruct((b, h, s, d), _MXU),
        compiler_params=_params("parallel", "parallel", "arbitrary"),
        name="sb_attn",
    )(q, k, v, u)


def _compress_kernel(xk_ref, xv_ref, w1k_ref, w2k_ref, pk_ref, w1v_ref, w2v_ref, pv_ref, ok_ref, ov_ref):
    def one(x_ref, w1_ref, w2_ref, p_ref, o_ref):
        x = x_ref[0, 0]
        n = x.shape[0]
        first = jnp.dot(x, w1_ref[0], preferred_element_type=_F32)
        second = jnp.dot(x, w1_ref[1], preferred_element_type=_F32)
        pos = _dot(p_ref[0], w1_ref[0]) + _dot(p_ref[1], w1_ref[1])
        hid = first + pltpu.roll(second, n - 1, 0) + pos[0:1]
        hid = 0.5 * hid * (1.0 + jnp.tanh(math.sqrt(2.0 / math.pi) * (hid + 0.044715 * hid * hid * hid)))
        o_ref[0, 0] = _dot(hid, w2_ref[...]).astype(o_ref.dtype)

    one(xk_ref, w1k_ref, w2k_ref, pk_ref, ok_ref)
    one(xv_ref, w1v_ref, w2v_ref, pv_ref, ov_ref)


def _compress(xk, xv, w1k, w2k, pk, w1v, w2v, pv):
    b, g, s, d = xk.shape
    n = s // CMP_STRIDE
    xk = xk.reshape(b, g, n, CMP_STRIDE * d)
    xv = xv.reshape(b, g, n, CMP_STRIDE * d)

    def full(a):
        return pl.BlockSpec(a.shape, lambda bi, gi: (0,) * a.ndim)

    xspec = pl.BlockSpec((1, 1, n, CMP_STRIDE * d), lambda bi, gi: (bi, gi, 0, 0))
    ospec = pl.BlockSpec((1, 1, n, d), lambda bi, gi: (bi, gi, 0, 0))
    oshape = jax.ShapeDtypeStruct((b, g, n, d), _MXU)
    return pl.pallas_call(
        _compress_kernel,
        grid=(b, g),
        in_specs=[xspec, xspec, full(w1k), full(w2k), full(pk), full(w1v), full(w2v), full(pv)],
        out_specs=[ospec, ospec],
        out_shape=[oshape, oshape],
        compiler_params=_params("parallel", "parallel"),
        name="nsa_compress",
    )(xk, xv, w1k, w2k, pk, w1v, w2v, pv)


def _cmp_kernel(q_ref, kc_ref, vc_ref, ov_ref, o_ref, sel_ref, *, tq, n_top):
    q0 = pl.program_id(2) * tq
    kc = kc_ref[0, 0]
    vc = vc_ref[0, 0]
    ncp = kc.shape[0]
    ns = sel_ref.shape[-1]
    rows = NSA_GROUP * tq
    q = q_ref[0].reshape(rows, q_ref.shape[-1])
    qpos = q0 + (lax.broadcasted_iota(jnp.int32, (rows, 1), 0) & (tq - 1))
    cmp_end = lax.broadcasted_iota(jnp.int32, (1, ncp), 1) * CMP_STRIDE + (CMP_LEN - 1)
    s = jnp.where(cmp_end <= qpos, _dot_nt(q, kc), NEG_INF)
    e = jnp.exp(s - jnp.max(s, axis=-1, keepdims=True))
    inv = jnp.where(qpos >= CMP_LEN - 1, 1.0 / jnp.sum(e, axis=-1, keepdims=True), 0.0)
    p = e * inv
    o_ref[0] = _dot(p, vc).reshape(NSA_GROUP, tq, HEAD_DIM)
    p_sum = sum(p[r * tq:(r + 1) * tq] for r in range(NSA_GROUP))
    hi = p_sum.astype(_MXU)
    lo = (p_sum - hi.astype(_F32)).astype(_MXU)
    score = _dot_nt(ov_ref[...], hi) + _dot_nt(ov_ref[...], lo)
    cur = jnp.right_shift(q0 + lax.broadcasted_iota(jnp.int32, (1, tq), 1), int(math.log2(SEL_LEN)))
    blk = lax.broadcasted_iota(jnp.int32, (ns, 1), 0)
    forced = (blk == 0) | (blk == cur) | (blk == cur - 1)
    score = jnp.where(forced, FORCE_SCORE, jnp.where(blk > cur, -1.0, score))
    blk_f = blk.astype(_F32)
    sel = jnp.zeros((ns, tq), _F32)
    for _ in range(n_top):
        top = jnp.max(score, axis=0, keepdims=True)
        first = jnp.min(jnp.where(score == top, blk_f, float(ns)), axis=0, keepdims=True)
        pick = blk_f == first
        sel = jnp.where(pick, 1.0, sel)
        score = jnp.where(pick, -3e38, score)
    sel_ref[0, 0] = (sel.T - 1.0).astype(sel_ref.dtype)


def _cmp_select(q, kc, vc):
    b, h, s, d = q.shape
    g = kc.shape[1]
    ncp = kc.shape[2]
    ns = s // SEL_LEN
    n_top = min(SEL_TOPK, ns)
    tq = min(256, s)
    assert tq & (tq - 1) == 0
    c0 =np.arange(ncp)[:, None] * CMP_STRIDE
    n0 = np.arange(ns)[None, :] * SEL_LEN
    overlap = jnp.asarray(((c0 < n0 + SEL_LEN) & (c0 + CMP_LEN > n0)).T, _MXU)
    return pl.pallas_call(
        functools.partial(_cmp_kernel, tq=tq, n_top=n_top),
        grid=(b, g, s // tq),
        in_specs=[pl.BlockSpec((1, NSA_GROUP, tq, d), lambda bi, gi, i: (bi, gi, i, 0)),
                  pl.BlockSpec((1, 1, ncp, d), lambda bi, gi, i: (bi, gi, 0, 0)),
                  pl.BlockSpec((1, 1, ncp, d), lambda bi, gi, i: (bi, gi, 0, 0)),
                  pl.BlockSpec((ns, ncp), lambda bi, gi, i: (0, 0))],
        out_specs=[pl.BlockSpec((1, NSA_GROUP, tq, d), lambda bi, gi, i: (bi, gi, i, 0)),
                   pl.BlockSpec((1, 1, tq, ns), lambda bi, gi, i: (bi, gi, i, 0))],
        out_shape=[jax.ShapeDtypeStruct((b, h, s, d), _F32), jax.ShapeDtypeStruct((b, g, s, ns), _MXU)],
        compiler_params=_params("parallel", "parallel", "arbitrary"),
        name="nsa_cmp_select",
    )(q, kc, vc, overlap)


def _sel_kernel(q_ref, k_ref, v_ref, o_ref, *, tq, tk):
    q0 = pl.program_id(2) * tq
    last = (q0 + tq - 1) // tk
    rows = NSA_GROUP * tq
    q = q_ref[0].reshape(rows, q_ref.shape[-1])

    def step(j, carry, causal):
        off = pl.multiple_of(j * tk, tk)
        s = _dot_nt(q, k_ref[0, 0, pl.ds(off, tk), :])
        if causal:
            row = q0 + (lax.broadcasted_iota(jnp.int32, (rows, tk), 0) & (tq - 1))
            col = j * tk + lax.broadcasted_iota(jnp.int32, (rows, tk), 1)
            s = jnp.where(col <= row, s, NEG_INF)
        return _softmax_step(carry, s, v_ref[0, 0, pl.ds(off, tk), :])

    carry = lax.fori_loop(0, last, functools.partial(step, causal=False), _softmax_init(rows, HEAD_DIM))
    _, l, acc = step(last, carry, True)
    o_ref[0] = (acc / l).reshape(NSA_GROUP, tq, HEAD_DIM)


def _sel_attention(q, k, v, sel_m1):
    b, h, s, d = q.shape
    g = k.shape[1]
    ns = sel_m1.shape[-1]
    tq = min(256, s)
    tk = min(512, s)
    assert tq & (tq - 1) == 0 and s % tk == 0
    onehot = (np.arange(s)[:, None] // SEL_LEN == np.arange(ns)[None, :]) * -NEG_INF
    k = jnp.concatenate([k, jnp.broadcast_to(jnp.asarray(onehot, _MXU), (b, g, s, ns))], axis=-1)
    q = jnp.concatenate([q, jnp.repeat(sel_m1, NSA_GROUP, axis=1)], axis=-1)
    da = d + ns
    return pl.pallas_call(
        functools.partial(_sel_kernel, tq=tq, tk=tk),
        grid=(b, g, s // tq),
        in_specs=[pl.BlockSpec((1, NSA_GROUP, tq, da), lambda bi, gi, i: (bi, gi, i, 0)),
                  pl.BlockSpec((1, 1, s, da), lambda bi, gi, i: (bi, gi, 0, 0)),
                  pl.BlockSpec((1, 1, s, d), lambda bi, gi, i: (bi, gi, 0, 0))],
        out_specs=pl.BlockSpec((1, NSA_GROUP, tq, d), lambda bi, gi, i: (bi, gi, i, 0)),
        out_shape=jax.ShapeDtypeStruct((b, h, s, d), _F32),
        compiler_params=_params("parallel", "parallel", "arbitrary"),
        name="nsa_selected",
    )(q, k, v)


def _win_kernel(q_ref, k_ref, v_ref, o_ref, *, tq, span):
    q0 = pl.program_id(2) * tq
    start = pl.multiple_of(jnp.maximum(q0 - WINDOW, 0), tq)
    k = k_ref[0, 0, pl.ds(start, span), :]
    v = v_ref[0, 0, pl.ds(start, span), :]
    rows = NSA_GROUP * tq
    q = q_ref[0].reshape(rows, q_ref.shape[-1])
    row = q0 + (lax.broadcasted_iota(jnp.int32, (rows, span), 0) & (tq - 1))
    col = start + lax.broadcasted_iota(jnp.int32, (rows, span), 1)
    s = _dot_nt(q, k)
    s = jnp.where(col <= row, s, NEG_INF)
    s = jnp.where(col > row - WINDOW, s, NEG_INF)
    e = jnp.exp(s - jnp.max(s, axis=-1, keepdims=True))
    o = _dot(e, v) * (1.0 / jnp.sum(e, axis=-1, keepdims=True))
    o_ref[0] = o.reshape(NSA_GROUP, tq, HEAD_DIM)


def _win_attention(q, k, v):
    b, h, s, d = q.shape
    g = k.shape[1]
    tq = min(128, s)
    span = min(WINDOW + tq, s)
    return pl.pallas_call(
        functools.partial(_win_kernel, tq=tq, span=span),
        grid=(b, g, s // tq),
        in_specs=[pl.BlockSpec((1, NSA_GROUP, tq, d), lambda bi, gi, i: (bi, gi, i, 0)),
                  pl.BlockSpec((1, 1, s, d), lambda bi, gi, i: (bi, gi, 0, 0)),
                  pl.BlockSpec((1, 1, s, d), lambda bi, gi, i: (bi, gi, 0, 0))],
        out_specs=pl.BlockSpec((1, NSA_GROUP, tq, d), lambda bi, gi, i: (bi, gi, i, 0)),
        out_shape=jax.ShapeDtypeStruct((b, h, s, d), _F32),
        compiler_params=_params("parallel", "parallel", "arbitrary"),
        name="nsa_window",
    )(q, k, v)


def _out_kernel(x_ref, mla_ref, cmp_ref, sel_ref, win_ref, gate_ref, sb_ref, w_ref, o_ref):
    acc = x_ref[0]
    for h in range(MLA_HEADS):
        acc = acc + jnp.dot(mla_ref[0, h], w_ref[h], preferred_element_type=_F32)
    gate = gate_ref[0]
    for h in range(NSA_HEADS):
        c = NSA_BRANCHES * h
        o = (gate[:, c:c + 1] * cmp_ref[0, h] + gate[:, c + 1:c + 2] * sel_ref[0, h]
             + gate[:, c + 2:c + 3] * win_ref[0, h])
        acc = acc + _dot(o, w_ref[MLA_HEADS + h])
    for h in range(SB_HEADS):
        acc = acc + jnp.dot(sb_ref[0, h], w_ref[MLA_HEADS + NSA_HEADS + h], preferred_element_type=_F32)
    o_ref[0] = acc


def _out_proj(x, o_mla, o_cmp, o_sel, o_win, gates, o_sb, w_heads):
    b, s, _ = x.shape
    ts = min(512, s)

    def heads(a):
        return pl.BlockSpec((1, a.shape[1], ts, a.shape[3]), lambda bi, i: (bi, 0, i, 0))

    xspec = pl.BlockSpec((1, ts, D_MODEL), lambda bi, i: (bi, i, 0))
    return pl.pallas_call(
        _out_kernel,
        grid=(b, s // ts),
        in_specs=[xspec, heads(o_mla), heads(o_cmp), heads(o_sel), heads(o_win),
                  pl.BlockSpec((1, ts, N_GATES), lambda bi, i: (bi, i, 0)), heads(o_sb),
                  pl.BlockSpec(w_heads.shape, lambda bi, i: (0, 0, 0))],
        out_specs=xspec,
        out_shape=jax.ShapeDtypeStruct(x.shape, _F32),
        compiler_params=_params("parallel", "parallel"),
        name="out_proj",
    )(x, o_mla, o_cmp, o_sel, o_win, gates, o_sb, w_heads)


def _gather_cols(w, idx):
    idx = np.asarray(idx)
    cols = jnp.take(w, jnp.asarray(np.maximum(idx, 0)), axis=1)
    return jnp.where(jnp.asarray(idx >= 0)[None, :], cols, 0.0).astype(_MXU)


def _swap_halves(rot):
    return (np.arange(rot) + rot // 2) % rot


def _w_in_index():
    idx = np.full((_N_SLOTS * LANES,), -1, np.int64)

    def put(slot, lane, src):
        src = np.asarray(src)
        idx[slot * LANES + lane:slot * LANES + lane + len(src)] = src

    put(_S_CQ, 0, _O_CQ + np.arange(MLA_Q_LORA))
    put(_S_CKV, 0, _O_CKV + np.arange(MLA_KV_LORA))
    put(_S_KR, MLA_NOPE, _O_KR + np.arange(MLA_ROPE))
    put(_S_KRS, MLA_NOPE, _O_KR + _swap_halves(MLA_ROPE))
    for h in range(NSA_HEADS):
        put(_S_NQ + h, 0, _O_NQ + h * HEAD_DIM + np.arange(HEAD_DIM))
        put(_S_NQS + h, 0, _O_NQ + h * HEAD_DIM + _swap_halves(PARTIAL_ROT))
    for sk, sks, sv, ok, ov in ((_S_KC, _S_KCS, _S_VC, _O_NKC, _O_NVC), (_S_KS, _S_KSS, _S_VS, _O_NKS, _O_NVS),
                                (_S_KW, _S_KWS, _S_VW, _O_NKW, _O_NVW)):
        for g in range(NSA_KV_HEADS):
            put(sk + g, 0, ok + g * HEAD_DIM + np.arange(HEAD_DIM))
            put(sks + g, 0, ok + g * HEAD_DIM + _swap_halves(PARTIAL_ROT))
            put(sv + g, 0, ov + g * HEAD_DIM + np.arange(HEAD_DIM))
    put(_S_GATE, 0, _O_GATE + np.arange(N_GATES))
    for h in range(SB_HEADS):
        put(_S_SBQ + h, 0, _O_SBQ + h * HEAD_DIM + np.arange(HEAD_DIM))
        put(_S_SBK + h, 0, _O_SBK + h * HEAD_DIM + np.arange(HEAD_DIM))
        put(_S_SBV + h, 0, _O_SBV + h * HEAD_DIM + np.arange(HEAD_DIM))
    return idx


def _mla_up_index():
    qd = MLA_NOPE + MLA_ROPE
    kd = MLA_NOPE + MLA_V
    uq = np.full((MLA_HEADS * LANES,), -1, np.int64)
    uqs = uq.copy()
    uk = uq.copy()
    uv = uq.copy()
    for h in range(MLA_HEADS):
        uq[h * LANES:h * LANES + qd] = h * qd + np.arange(qd)
        uqs[h * LANES + MLA_NOPE:h * LANES + qd] = h * qd + MLA_NOPE + _swap_halves(MLA_ROPE)
        uk[h * LANES:h * LANES + MLA_NOPE] = h * kd + np.arange(MLA_NOPE)
        uv[h * LANES:h * LANES + MLA_V] = h * kd + MLA_NOPE + np.arange(MLA_V)
    return uq, uqs, uk, uv


def _rope_tables(s):
    pos = jnp.arange(s, dtype=_F32)

    def cs(rot):
        half = rot // 2
        inv_freq = ROPE_THETA ** (-jnp.arange(half, dtype=_F32) / half)
        ang = pos[:, None] * inv_freq[None, :]
        c, sn = jnp.cos(ang), jnp.sin(ang)
        return jnp.concatenate([c, c], axis=1), jnp.concatenate([-sn, sn], axis=1)

    c, sn = cs(MLA_ROPE)
    ones = jnp.ones((s, MLA_NOPE), _F32)
    zeros = jnp.zeros((s, MLA_NOPE), _F32)
    pad = jnp.zeros((s, LANES - MLA_NOPE - MLA_ROPE), _F32)
    ck = jnp.concatenate([ones, c, pad], axis=1)
    sk = jnp.concatenate([zeros, sn, pad], axis=1)
    q_scale = (MLA_NOPE + MLA_ROPE) ** -0.5
    c, sn = cs(PARTIAL_ROT)
    c64 = jnp.concatenate([c, jnp.ones((s, HEAD_DIM - PARTIAL_ROT), _F32)], axis=1)
    s64 = jnp.concatenate([sn, jnp.zeros((s, HEAD_DIM - PARTIAL_ROT), _F32)], axis=1)
    return ck * q_scale, sk * q_scale, ck, sk, c64, s64


def kernel(x, ffn1_norm, ffn1_w_gate, ffn1_w_up, ffn1_w_down, mix_norm, w_in, mla_q_norm, mla_w_uq, mla_kv_norm,
           mla_w_ukv, nsa_gate_bias, nsa_cmp_pos_k, nsa_cmp_w1_k, nsa_cmp_w2_k, nsa_cmp_pos_v, nsa_cmp_w1_v,
           nsa_cmp_w2_v, w_out, ffn2_norm, ffn2_w_gate, ffn2_w_up, ffn2_w_down, final_norm):
    b, s, d = x.shape
    depth = w_in.shape[0]
    tabs = _rope_tables(s)
    in_idx = _w_in_index()
    uq_idx, uqs_idx, uk_idx, uv_idx = _mla_up_index()
    half = CMP_LEN * HEAD_DIM // 2
    fg = final_norm.reshape(1, d)

    def cmp_weights(w1, w2, pos):
        pos = jnp.broadcast_to(pos.reshape(2, 1, half), (2, 8, half)).astype(_MXU)
        return w1.reshape(2, half, CMP_HIDDEN).astype(_MXU), w2.astype(_MXU), pos

    for l in range(depth):
        x2d = _ffn(x.reshape(b * s, d), ffn1_norm[l].reshape(1, d), ffn1_w_gate[l].astype(_MXU),
                   ffn1_w_up[l].astype(_MXU), ffn1_w_down[l].astype(_MXU), fg, False)
        x = x2d.reshape(b, s, d)
        (mq, mk, mv, nq, nkc, nvc, nks, nvs, nkw, nvw, gates, sbq, sbk, sbv) = _proj(
            x, mix_norm[l].reshape(1, d), _gather_cols(w_in[l], in_idx),
            mla_q_norm[l].reshape(1, -1), _gather_cols(mla_w_uq[l], uq_idx), _gather_cols(mla_w_uq[l], uqs_idx),
            mla_kv_norm[l].reshape(1, -1), _gather_cols(mla_w_ukv[l], uk_idx), _gather_cols(mla_w_ukv[l], uv_idx),
            nsa_gate_bias[l].reshape(1, -1), tabs)
        o_mla = _mla_attention(mq, mk, mv)
        kc, vc = _compress(nkc, nvc, *cmp_weights(nsa_cmp_w1_k[l], nsa_cmp_w2_k[l], nsa_cmp_pos_k[l]),
                           *cmp_weights(nsa_cmp_w1_v[l], nsa_cmp_w2_v[l], nsa_cmp_pos_v[l]))
        o_cmp, sel = _cmp_select(nq, kc, vc)
        o_sel = _sel_attention(nq, nks, nvs, sel)
        o_win = _win_attention(nq, nkw, nvw)
        o_sb = _sb_attention(sbq, sbk, sbv)
        w_heads = w_out[l].reshape(MLA_HEADS + NSA_HEADS + SB_HEADS, HEAD_DIM, d).astype(_MXU)
        x = _out_proj(x, o_mla, o_cmp, o_sel, o_win, gates, o_sb, w_heads)
        x2d = _ffn(x.reshape(b * s, d), ffn2_norm[l].reshape(1, d), ffn2_w_gate[l].astype(_MXU),
                   ffn2_w_up[l].astype(_MXU), ffn2_w_down[l].astype(_MXU), fg, l == depth - 1)
        x = x2d.reshape(b, s, d)
    return x
```

```python
import functools
import math

import numpy as np
import jax
import jax.numpy as jnp
from jax import lax
from jax.experimental import pallas as pl
from jax.experimental.pallas import tpu as pltpu

D_MODEL = 1024
HEAD_DIM = 64
MLA_HEADS = 6
MLA_NOPE = 64
MLA_ROPE = 32
MLA_V = 64
MLA_Q_LORA = 256
MLA_KV_LORA = 128
NSA_HEADS = 6
NSA_KV_HEADS = 2
NSA_GROUP = NSA_HEADS // NSA_KV_HEADS
NSA_BRANCHES = 3
CMP_LEN = 32
CMP_STRIDE = 16
CMP_HIDDEN = 128
SEL_LEN = 64
SEL_TOPK = 16
WINDOW = 512
SB_HEADS = 4
D_FF = 2816
ROPE_THETA = 500000.0
PARTIAL_ROT = HEAD_DIM // 4
EPS = 1e-6
NEG_INF = -1e30
FORCE_SCORE = 1e4
F32_EXP_ZERO = -104.0
LOG2_E = math.log2(math.e)
N_GATES = NSA_HEADS * NSA_BRANCHES

LANES = 128
ONES_PAD = 16
VMEM_LIMIT = 56 * 1024 * 1024

_MXU = jnp.bfloat16
_F32 = jnp.float32

_IN_WIDTHS = (MLA_Q_LORA, MLA_KV_LORA, MLA_ROPE, NSA_HEADS * HEAD_DIM) + (NSA_KV_HEADS * HEAD_DIM,) * 6 + (
    N_GATES, SB_HEADS * HEAD_DIM, SB_HEADS * HEAD_DIM, SB_HEADS * HEAD_DIM)
_IN_OFF = np.concatenate([[0], np.cumsum(_IN_WIDTHS)])
(_O_CQ, _O_CKV, _O_KR, _O_NQ, _O_NKC, _O_NVC, _O_NKS, _O_NVS, _O_NKW, _O_NVW, _O_GATE, _O_SBQ, _O_SBK,
 _O_SBV) = [int(v) for v in _IN_OFF[:-1]]

_S_CQ, _S_CKV, _S_KR, _S_KRS = 0, 2, 3, 4
_S_NQ, _S_NQS = 5, 11
_S_KC, _S_KCS, _S_VC = 17, 19, 21
_S_KS, _S_KSS, _S_VS = 23, 25, 27
_S_KW, _S_KWS, _S_VW = 29, 31, 33
_S_GATE = 35
_S_SBQ, _S_SBK, _S_SBV = 36, 40, 44
_N_SLOTS = 48


def _dot(a, b):
    return jnp.dot(a.astype(_MXU), b.astype(_MXU), preferred_element_type=_F32)


def _dot_nt(a, b):
    return lax.dot_general(a.astype(_MXU), b.astype(_MXU), (((1,), (1,)), ((), ())),
                           preferred_element_type=_F32)


def _dot_split(a, b):
    hi = a.astype(_MXU)
    lo = (a - hi.astype(_F32)).astype(_MXU)
    return (jnp.dot(hi, b, preferred_element_type=_F32) + jnp.dot(lo, b, preferred_element_type=_F32))


def _rms(x, g):
    return x * lax.rsqrt(jnp.mean(x * x, axis=-1, keepdims=True) + EPS) * g


def _params(*sem):
    return pltpu.CompilerParams(dimension_semantics=sem, vmem_limit_bytes=VMEM_LIMIT)


def _ffn_kernel(x_ref, g_ref, wg_ref, wu_ref, wd_ref, fg_ref, o_ref, h_ref, acc_ref, *, final_norm):
    j = pl.program_id(1)

    @pl.when(j == 0)
    def _():
        h_ref[...] = _rms(x_ref[...], g_ref[...]).astype(h_ref.dtype)
        acc_ref[...] = jnp.zeros_like(acc_ref)

    h = h_ref[...]
    gate = jnp.dot(h, wg_ref[...], preferred_element_type=_F32)
    up = jnp.dot(h, wu_ref[...], preferred_element_type=_F32)
    act = gate * jax.nn.sigmoid(gate) * up
    acc_ref[...] += _dot(act, wd_ref[...])

    @pl.when(j == pl.num_programs(1) - 1)
    def _():
        y = x_ref[...] + 0.5 * acc_ref[...]
        if final_norm:
            y = _rms(y, fg_ref[...])
        o_ref[...] = y


def _ffn(x2d, g, wg, wu, wd, fg, final_norm):
    rows = x2d.shape[0]
    tm = min(1024, rows)
    tf = 256
    grid = (rows // tm, D_FF // tf)
    return pl.pallas_call(
        functools.partial(_ffn_kernel, final_norm=final_norm),
        grid=grid,
        in_specs=[
            pl.BlockSpec((tm, D_MODEL), lambda i, j: (i, 0)),
            pl.BlockSpec((1, D_MODEL), lambda i, j: (0, 0)),
            pl.BlockSpec((D_MODEL, tf), lambda i, j: (0, j)),
            pl.BlockSpec((D_MODEL, tf), lambda i, j: (0, j)),
            pl.BlockSpec((tf, D_MODEL), lambda i, j: (j, 0)),
            pl.BlockSpec((1, D_MODEL), lambda i, j: (0, 0)),
        ],
        out_specs=pl.BlockSpec((tm, D_MODEL), lambda i, j: (i, 0)),
        out_shape=jax.ShapeDtypeStruct((rows, D_MODEL), _F32),
        scratch_shapes=[pltpu.VMEM((tm, D_MODEL), _MXU), pltpu.VMEM((tm, D_MODEL), _F32)],
        compiler_params=_params("parallel", "arbitrary"),
        name="ffn",
    )(x2d, g, wg, wu, wd, fg)


def _proj_kernel(x_ref, g_ref, w_ref, qn_ref, wuq_ref, wuqs_ref, kvn_ref, wuk_ref, wuv_ref, gb_ref,
                 cq_ref, sq_ref, ck_ref, sk_ref, c64_ref, s64_ref,
                 mq_ref, mk_ref, mv_ref, nq_ref, nkc_ref, nvc_ref, nks_ref, nvs_ref, nkw_ref, nvw_ref,
                 gate_ref, sbq_ref, sbk_ref, sbv_ref):
    hn = _rms(x_ref[0], g_ref[...]).astype(_MXU)

    def proj(s0, s1):
        return jnp.dot(hn, w_ref[:, s0 * LANES:s1 * LANES], preferred_element_type=_F32)

    def slot(p, s):
        return p[:, s * LANES:(s + 1) * LANES]

    def head(p, s):
        return p[:, s * LANES:s * LANES + HEAD_DIM]

    p = proj(_S_CQ, _S_KRS + 1)
    cq = _rms(p[:, :MLA_Q_LORA], qn_ref[...])
    ckv = _rms(slot(p, _S_CKV), kvn_ref[...])
    q = _dot(cq, wuq_ref[...])
    q_partner = _dot(cq, wuqs_ref[...])
    kpe = slot(p, _S_KR) * ck_ref[...] + slot(p, _S_KRS) * sk_ref[...]
    kn = _dot(ckv, wuk_ref[...])
    for h in range(MLA_HEADS):
        mq_ref[0, h] = (slot(q, h) * cq_ref[...] + slot(q_partner, h) * sq_ref[...]).astype(mq_ref.dtype)
        mk_ref[0, h] = (slot(kn, h) + kpe).astype(mk_ref.dtype)
        mv_ref[0, h, 0] = _ones_row_pad(_dot_nt(wuv_ref[h], ckv)).astype(mv_ref.dtype)

    c64 = c64_ref[...]
    s64 = s64_ref[...]
    scale = HEAD_DIM ** -0.5

    p = proj(_S_NQ, _S_KC)
    for h in range(NSA_HEADS):
        nq_ref[0, h] = ((head(p, h) * c64 + head(p, NSA_HEADS + h) * s64) * scale).astype(nq_ref.dtype)

    p = proj(_S_KC, _S_GATE + 1)
    base = _S_KC
    for k_ref, v_ref, sk, sks, sv in ((nkc_ref, nvc_ref, _S_KC, _S_KCS, _S_VC),
                                      (nks_ref, nvs_ref, _S_KS, _S_KSS, _S_VS),
                                      (nkw_ref, nvw_ref, _S_KW, _S_KWS, _S_VW)):
        for g in range(NSA_KV_HEADS):
            k_ref[0, g] = (head(p, sk - base + g) * c64 + head(p, sks - base + g) * s64).astype(k_ref.dtype)
            v_ref[0, g] = head(p, sv - base + g).astype(v_ref.dtype)
    gate_ref[0] = jax.nn.sigmoid(p[:, (_S_GATE - base) * LANES:(_S_GATE - base) * LANES + N_GATES] + gb_ref[...])

    p = proj(_S_SBQ, _N_SLOTS)
    for h in range(SB_HEADS):
        sbq_ref[0, h] = (head(p, h) * scale).astype(sbq_ref.dtype)
        sbk_ref[0, h] = head(p, SB_HEADS + h).astype(sbk_ref.dtype)
        sbv_ref[0, h] = head(p, 2 * SB_HEADS + h).astype(sbv_ref.dtype)


def _proj(x, g, w_ext, qn, wuq, wuqs, kvn, wuk, wuv, gb, tabs):
    b, s, _ = x.shape
    ts = min(256, s)
    cq, sq, ck, sk, c64, s64 = tabs

    def full(a):
        return pl.BlockSpec(a.shape, lambda bi, i: (0,) * a.ndim)

    def tab(a):
        return pl.BlockSpec((ts, a.shape[1]), lambda bi, i: (i, 0))

    def heads(n, d):
        return (pl.BlockSpec((1, n, ts, d), lambda bi, i: (bi, 0, i, 0)),
                jax.ShapeDtypeStruct((b, n, s, d), _MXU))

    mvt = (pl.BlockSpec((1, MLA_HEADS, 1, MLA_V + ONES_PAD, ts), lambda bi, i: (bi, 0, i, 0, 0)),
           jax.ShapeDtypeStruct((b, MLA_HEADS, s // ts, MLA_V + ONES_PAD, ts), _MXU))
    outs = [heads(MLA_HEADS, LANES), heads(MLA_HEADS, LANES), mvt,
            heads(NSA_HEADS, HEAD_DIM)] + [heads(NSA_KV_HEADS, HEAD_DIM) for _ in range(6)] + [
        (pl.BlockSpec((1, ts, N_GATES), lambda bi, i: (bi, i, 0)), jax.ShapeDtypeStruct((b, s, N_GATES), _F32))
    ] + [heads(SB_HEADS, HEAD_DIM) for _ in range(3)]
    return pl.pallas_call(
        _proj_kernel,
        grid=(b, s // ts),
        in_specs=[pl.BlockSpec((1, ts, D_MODEL), lambda bi, i: (bi, i, 0)), full(g), full(w_ext), full(qn),
                  full(wuq), full(wuqs), full(kvn), full(wuk), full(wuv), full(gb),
                  tab(cq), tab(sq), tab(ck), tab(sk), tab(c64), tab(s64)],
        out_specs=[o[0] for o in outs],
        out_shape=[o[1] for o in outs],
        compiler_params=_params("parallel", "parallel"),
        name="proj",
    )(x, g, w_ext, qn, wuq, wuqs, kvn, wuk, wuv, gb, cq, sq, ck, sk, c64, s64)


def _softmax_step(carry, s, v):
    m, l, acc = carry
    m_new = jnp.maximum(m, jnp.max(s, axis=-1, keepdims=True))
    alpha = jnp.exp(m - m_new)
    p = jnp.exp(s - m_new)
    l = alpha * l + jnp.sum(p, axis=-1, keepdims=True)
    acc = alpha * acc + _dot(p, v)
    return m_new, l, acc


def _softmax_init(rows, d):
    return (jnp.full((rows, 1), NEG_INF, _F32), jnp.zeros((rows, 1), _F32), jnp.zeros((rows, d), _F32))


def _ones_row_pad(vt):
    first = lax.broadcasted_iota(jnp.int32, (ONES_PAD, vt.shape[1]), 0) == 0
    return jnp.concatenate([vt, jnp.where(first, 1.0, 0.0).astype(vt.dtype)], axis=0)


def _softmax_step_t(carry, st, vt_chunks):
    m, acc = carry
    m_new = jnp.maximum(m, jnp.max(st, axis=0, keepdims=True))
    alpha = jnp.exp2(m - m_new)
    pt = jnp.exp2(st - m_new).astype(_MXU)
    n = st.shape[0] // len(vt_chunks)
    pv = sum(jnp.dot(vt, pt[c * n:(c + 1) * n], preferred_element_type=_F32) for c, vt in enumerate(vt_chunks))
    return m_new, alpha * acc + pv


def _softmax_init_t(d, cols):
    return (jnp.full((1, cols), NEG_INF, _F32), jnp.zeros((d + ONES_PAD, cols), _F32))


def _softmax_finish_t(carry, d):
    _, acc = carry
    return acc[:d] * (1.0 / acc[d:d + 1])


def _two_chain_sweep(n_full, qk, soft, init):
    qk(1, 0)

    def body(j, carry):
        c0, c1 = carry
        qk(0, j)
        c1 = soft(1, j, c1, False)
        qk(1, j + 1)
        c0 = soft(0, j, c0, False)
        return c0, c1

    c0, c1 = lax.fori_loop(0, n_full, body, init)
    qk(0, n_full)
    c1 = soft(1, n_full, c1, True)
    c0 = soft(0, n_full, c0, True)
    return c0, c1


def _mla_kernel(q_ref, k_ref, vt_ref, o_ref, s0_ref, s1_ref, *, t, nsub):
    qi = pl.program_id(2)
    s_refs = (s0_ref, s1_ref)

    def qk(hh, j):
        off = pl.multiple_of(j * t, t)
        s_refs[hh][...] = _dot_nt(k_ref[0, hh, pl.ds(off, t), :], q_ref[0, hh])

    def soft(hh, j, carry, diag):
        st = s_refs[hh][...]
        if diag:
            key = lax.broadcasted_iota(jnp.int32, (t, t), 0)
            qry = lax.broadcasted_iota(jnp.int32, (t, t), 1)
            st = jnp.where(key <= qry, st, NEG_INF)
        return _softmax_step_t(carry, st, [vt_ref[0, hh, j * nsub + c] for c in range(nsub)])

    carry = _two_chain_sweep(qi, qk, soft, tuple(_softmax_init_t(MLA_V, t) for _ in range(2)))
    ot = jnp.concatenate([_softmax_finish_t(c, MLA_V) for c in carry], axis=0)
    o_ref[0] = ot.T.astype(o_ref.dtype)


def _mla_attention(q, k, vt):
    b, h, s, _ = q.shape
    tv = vt.shape[-1]
    dv = vt.shape[-2]
    t = min(512, s)
    assert h % 2 == 0 and 2 * MLA_V == LANES and t % tv == 0
    return pl.pallas_call(
        functools.partial(_mla_kernel, t=t, nsub=t // tv),
        grid=(b, h // 2, s // t),
        in_specs=[pl.BlockSpec((1, 2, t, LANES), lambda bi, hi, i: (bi, hi, i, 0)),
                  pl.BlockSpec((1, 2, s, LANES), lambda bi, hi, i: (bi, hi, 0, 0)),
                  pl.BlockSpec((1, 2, s // tv, dv, tv), lambda bi, hi, i: (bi, hi, 0, 0, 0))],
        out_specs=pl.BlockSpec((1, t, LANES), lambda bi, hi, i: (bi, i, hi)),
        out_shape=jax.ShapeDtypeStruct((b, s, h * MLA_V), _MXU),
        scratch_shapes=[pltpu.VMEM((t, t), _F32), pltpu.VMEM((t, t), _F32)],
        compiler_params=_params("parallel", "parallel", "arbitrary"),
        name="mla_attn",
    )(q, k, vt)


def _sb_kernel(q_ref, k_ref, v_ref, u_ref, o_ref, *, t):
    qi = pl.program_id(2)
    q = q_ref[0, 0]
    u = u_ref[...]

    def step(j, carry, diag):
        rem, acc = carry
        off = pl.multiple_of(j * t, t)
        z = _dot_nt(q, k_ref[0, 0, pl.ds(off, t), :])
        log_beta = jnp.minimum(z, 0.0) - jnp.log1p(jnp.exp(-jnp.abs(z)))
        log_rem = log_beta - z
        if diag:
            row = lax.broadcasted_iota(jnp.int32, (t, t), 0)
            col = lax.broadcasted_iota(jnp.int32, (t, t), 1)
            strict = col < row
            log_rem = jnp.where(strict, log_rem, 0.0)
        suffix = _dot_split(log_rem, u)
        a = jnp.exp(log_beta + suffix + rem)
        if diag:
            a = jnp.where(strict, a, 0.0)
        acc = acc + _dot(a, v_ref[0, 0, pl.ds(off, t), :])
        rem = rem + suffix[:, 0:1] + log_rem[:, 0:1]
        return rem, acc

    rem, acc = step(qi, (jnp.zeros((t, 1), _F32), jnp.zeros((t, HEAD_DIM), _F32)), True)

    def live(c):
        return jnp.logical_and(c[0] >= 0, jnp.max(c[1]) > F32_EXP_ZERO)

    def earlier(c):
        return (c[0] - 1,) + step(c[0], c[1:], False)

    _, _, acc = lax.while_loop(live, earlier, (qi - 1, rem, acc))
    o_ref[0, 0] = acc.astype(o_ref.dtype)


def _sb_attention(q, k, v):
    b, h, s, d = q.shape
    t = min(256, s)
    idx = np.arange(t)
    u = jnp.asarray(idx[:, None] > idx[None, :], _MXU)
    return pl.pallas_call(
        functools.partial(_sb_kernel, t=t),
        grid=(b, h, s // t),
        in_specs=[pl.BlockSpec((1, 1, t, d), lambda bi, hi, i: (bi, hi, i, 0)),
                  pl.BlockSpec((1, 1, s, d), lambda bi, hi, i: (bi, hi, 0, 0)),
                  pl.BlockSpec((1, 1, s, d), lambda bi, hi, i: (bi, hi, 0, 0)),
                  pl.BlockSpec((t, t), lambda bi, hi, i: (0, 0))],
        out_specs=pl.BlockSpec((1, 1, t, d), lambda bi, hi, i: (bi, hi, i, 0)),
        out_shape=jax.ShapeDtypeStruct((b, h, s, d), _MXU),
        compiler_params=_params("parallel", "parallel", "arbitrary"),
        name="sb_attn",
    )(q, k, v, u)


def _compress_kernel(xk_ref, xv_ref, w1k_ref, w2k_ref, pk_ref, w1v_ref, w2v_ref, pv_ref, ok_ref, ov_ref):
    def one(x_ref, w1_ref, w2_ref, p_ref, o_ref):
        x = x_ref[0, 0]
        n = x.shape[0]
        first = jnp.dot(x, w1_ref[0], preferred_element_type=_F32)
        second = jnp.dot(x, w1_ref[1], preferred_element_type=_F32)
        pos = _dot(p_ref[0], w1_ref[0]) + _dot(p_ref[1], w1_ref[1])
        hid = first + pltpu.roll(second, n - 1, 0) + pos[0:1]
        hid = 0.5 * hid * (1.0 + jnp.tanh(math.sqrt(2.0 / math.pi) * (hid + 0.044715 * hid * hid * hid)))
        o_ref[0, 0] = _dot(hid, w2_ref[...]).astype(o_ref.dtype)

    one(xk_ref, w1k_ref, w2k_ref, pk_ref, ok_ref)
    one(xv_ref, w1v_ref, w2v_ref, pv_ref, ov_ref)


def _compress(xk, xv, w1k, w2k, pk, w1v, w2v, pv):
    b, g, s, d = xk.shape
    n = s // CMP_STRIDE
    xk = xk.reshape(b, g, n, CMP_STRIDE * d)
    xv = xv.reshape(b, g, n, CMP_STRIDE * d)

    def full(a):
        return pl.BlockSpec(a.shape, lambda bi, gi: (0,) * a.ndim)

    xspec = pl.BlockSpec((1, 1, n, CMP_STRIDE * d), lambda bi, gi: (bi, gi, 0, 0))
    ospec = pl.BlockSpec((1, 1, n, d), lambda bi, gi: (bi, gi, 0, 0))
    oshape = jax.ShapeDtypeStruct((b, g, n, d), _MXU)
    return pl.pallas_call(
        _compress_kernel,
        grid=(b, g),
        in_specs=[xspec, xspec, full(w1k), full(w2k), full(pk), full(w1v), full(w2v), full(pv)],
        out_specs=[ospec, ospec],
        out_shape=[oshape, oshape],
        compiler_params=_params("parallel", "parallel"),
        name="nsa_compress",
    )(xk, xv, w1k, w2k, pk, w1v, w2v, pv)


def _cmp_kernel(q_ref, kc_ref, vc_ref, ov_ref, o_ref, sel_ref, *, tq, n_top):
    q0 = pl.program_id(2) * tq
    kc = kc_ref[0, 0]
    vc = vc_ref[0, 0]
    ncp = kc.shape[0]
    ns = sel_ref.shape[-1]
    rows = NSA_GROUP * tq
    q = q_ref[0].reshape(rows, q_ref.shape[-1])
    qpos = q0 + (lax.broadcasted_iota(jnp.int32, (rows, 1), 0) & (tq - 1))
    cmp_end = lax.broadcasted_iota(jnp.int32, (1, ncp), 1) * CMP_STRIDE + (CMP_LEN - 1)
    s = jnp.where(cmp_end <= qpos, _dot_nt(q, kc), NEG_INF)
    e = jnp.exp(s - jnp.max(s, axis=-1, keepdims=True))
    inv = jnp.where(qpos >= CMP_LEN - 1, 1.0 / jnp.sum(e, axis=-1, keepdims=True), 0.0)
    p = e * inv
    o_ref[0] = _dot(p, vc).reshape(NSA_GROUP, tq, HEAD_DIM)
    p_sum = sum(p[r * tq:(r + 1) * tq] for r in range(NSA_GROUP))
    hi = p_sum.astype(_MXU)
    lo = (p_sum - hi.astype(_F32)).astype(_MXU)
    score = _dot_nt(ov_ref[...], hi) + _dot_nt(ov_ref[...], lo)
    cur = jnp.right_shift(q0 + lax.broadcasted_iota(jnp.int32, (1, tq), 1), int(math.log2(SEL_LEN)))
    blk = lax.broadcasted_iota(jnp.int32, (ns, 1), 0)
    forced = (blk == 0) | (blk == cur) | (blk == cur - 1)
    score = jnp.where(forced, FORCE_SCORE, jnp.where(blk > cur, -1.0, score))
    blk_f = blk.astype(_F32)
    sel = jnp.zeros((ns, tq), _F32)
    for _ in range(n_top):
        top = jnp.max(score, axis=0, keepdims=True)
        first = jnp.min(jnp.where(score == top, blk_f, float(ns)), axis=0, keepdims=True)
        pick = blk_f == first
        sel = jnp.where(pick, 1.0, sel)
        score = jnp.where(pick, -3e38, score)
    sel_ref[0, 0] = (sel.T - 1.0).astype(sel_ref.dtype)


def _cmp_select(q, kc, vc):
    b, h, s, d = q.shape
    g = kc.shape[1]
    ncp = kc.shape[2]
    ns = s // SEL_LEN
    n_top = min(SEL_TOPK, ns)
    tq = min(256, s)
    assert tq & (tq - 1) == 0
    c0 =np.arange(ncp)[:, None] * CMP_STRIDE
    n0 = np.arange(ns)[None, :] * SEL_LEN
    overlap = jnp.asarray(((c0 < n0 + SEL_LEN) & (c0 + CMP_LEN > n0)).T, _MXU)
    return pl.pallas_call(
        functools.partial(_cmp_kernel, tq=tq, n_top=n_top),
        grid=(b, g, s // tq),
        in_specs=[pl.BlockSpec((1, NSA_GROUP, tq, d), lambda bi, gi, i: (bi, gi, i, 0)),
                  pl.BlockSpec((1, 1, ncp, d), lambda bi, gi, i: (bi, gi, 0, 0)),
                  pl.BlockSpec((1, 1, ncp, d), lambda bi, gi, i: (bi, gi, 0, 0)),
                  pl.BlockSpec((ns, ncp), lambda bi, gi, i: (0, 0))],
        out_specs=[pl.BlockSpec((1, NSA_GROUP, tq, d), lambda bi, gi, i: (bi, gi, i, 0)),
                   pl.BlockSpec((1, 1, tq, ns), lambda bi, gi, i: (bi, gi, i, 0))],
        out_shape=[jax.ShapeDtypeStruct((b, h, s, d), _F32), jax.ShapeDtypeStruct((b, g, s, ns), _MXU)],
        compiler_params=_params("parallel", "parallel", "arbitrary"),
        name="nsa_cmp_select",
    )(q, kc, vc, overlap)


def _sel_kernel(q_ref, k_ref, v_ref, o_ref, *, tq, tk):
    q0 = pl.program_id(2) * tq
    last = (q0 + tq - 1) // tk
    rows = NSA_GROUP * tq
    q = q_ref[0].reshape(rows, q_ref.shape[-1])

    def step(j, carry, causal):
        off = pl.multiple_of(j * tk, tk)
        s = _dot_nt(q, k_ref[0, 0, pl.ds(off, tk), :])
        if causal:
            row = q0 + (lax.broadcasted_iota(jnp.int32, (rows, tk), 0) & (tq - 1))
            col = j * tk + lax.broadcasted_iota(jnp.int32, (rows, tk), 1)
            s = jnp.where(col <= row, s, NEG_INF)
        return _softmax_step(carry, s, v_ref[0, 0, pl.ds(off, tk), :])

    carry = lax.fori_loop(0, last, functools.partial(step, causal=False), _softmax_init(rows, HEAD_DIM))
    _, l, acc = step(last, carry, True)
    o_ref[0] = (acc / l).reshape(NSA_GROUP, tq, HEAD_DIM)


def _sel_attention(q, k, v, sel_m1):
    b, h, s, d = q.shape
    g = k.shape[1]
    ns = sel_m1.shape[-1]
    tq = min(256, s)
    tk = min(512, s)
    assert tq & (tq - 1) == 0 and s % tk == 0
    onehot = (np.arange(s)[:, None] // SEL_LEN == np.arange(ns)[None, :]) * -NEG_INF
    k = jnp.concatenate([k, jnp.broadcast_to(jnp.asarray(onehot, _MXU), (b, g, s, ns))], axis=-1)
    q = jnp.concatenate([q, jnp.repeat(sel_m1, NSA_GROUP, axis=1)], axis=-1)
    da = d + ns
    return pl.pallas_call(
        functools.partial(_sel_kernel, tq=tq, tk=tk),
        grid=(b, g, s // tq),
        in_specs=[pl.BlockSpec((1, NSA_GROUP, tq, da), lambda bi, gi, i: (bi, gi, i, 0)),
                  pl.BlockSpec((1, 1, s, da), lambda bi, gi, i: (bi, gi, 0, 0)),
                  pl.BlockSpec((1, 1, s, d), lambda bi, gi, i: (bi, gi, 0, 0))],
        out_specs=pl.BlockSpec((1, NSA_GROUP, tq, d), lambda bi, gi, i: (bi, gi, i, 0)),
        out_shape=jax.ShapeDtypeStruct((b, h, s, d), _F32),
        compiler_params=_params("parallel", "parallel", "arbitrary"),
        name="nsa_selected",
    )(q, k, v)


def _win_kernel(q_ref, k_ref, v_ref, o_ref, *, tq, span):
    q0 = pl.program_id(2) * tq
    start = pl.multiple_of(jnp.maximum(q0 - WINDOW, 0), tq)
    k = k_ref[0, 0, pl.ds(start, span), :]
    v = v_ref[0, 0, pl.ds(start, span), :]
    rows = NSA_GROUP * tq
    q = q_ref[0].reshape(rows, q_ref.shape[-1])
    row = q0 + (lax.broadcasted_iota(jnp.int32, (rows, span), 0) & (tq - 1))
    col = start + lax.broadcasted_iota(jnp.int32, (rows, span), 1)
    s = _dot_nt(q, k)
    s = jnp.where(col <= row, s, NEG_INF)
    s = jnp.where(col > row - WINDOW, s, NEG_INF)
    e = jnp.exp(s - jnp.max(s, axis=-1, keepdims=True))
    o = _dot(e, v) * (1.0 / jnp.sum(e, axis=-1, keepdims=True))
    o_ref[0] = o.reshape(NSA_GROUP, tq, HEAD_DIM)


def _win_attention(q, k, v):
    b, h, s, d = q.shape
    g = k.shape[1]
    tq = min(128, s)
    span = min(WINDOW + tq, s)
    return pl.pallas_call(
        functools.partial(_win_kernel, tq=tq, span=span),
        grid=(b, g, s // tq),
        in_specs=[pl.BlockSpec((1, NSA_GROUP, tq, d), lambda bi, gi, i: (bi, gi, i, 0)),
                  pl.BlockSpec((1, 1, s, d), lambda bi, gi, i: (bi, gi, 0, 0)),
                  pl.BlockSpec((1, 1, s, d), lambda bi, gi, i: (bi, gi, 0, 0))],
        out_specs=pl.BlockSpec((1, NSA_GROUP, tq, d), lambda bi, gi, i: (bi, gi, i, 0)),
        out_shape=jax.ShapeDtypeStruct((b, h, s, d), _F32),
        compiler_params=_params("parallel", "parallel", "arbitrary"),
        name="nsa_window",
    )(q, k, v)


def _out_kernel(x_ref, mla_ref, cmp_ref, sel_ref, win_ref, gate_ref, sb_ref, w_ref, o_ref):
    def w_rows(first_head, n_heads):
        return w_ref[first_head * HEAD_DIM:(first_head + n_heads) * HEAD_DIM, :]

    acc = x_ref[0] + jnp.dot(mla_ref[0], w_rows(0, MLA_HEADS), preferred_element_type=_F32)
    gate = gate_ref[0]
    for h in range(NSA_HEADS):
        c = NSA_BRANCHES * h
        o = (gate[:, c:c + 1] * cmp_ref[0, h] + gate[:, c + 1:c + 2] * sel_ref[0, h]
             + gate[:, c + 2:c + 3] * win_ref[0, h])
        acc = acc + _dot(o, w_rows(MLA_HEADS + h, 1))
    for h in range(SB_HEADS):
        acc = acc + jnp.dot(sb_ref[0, h], w_rows(MLA_HEADS + NSA_HEADS + h, 1), preferred_element_type=_F32)
    o_ref[0] = acc


def _out_proj(x, o_mla, o_cmp, o_sel, o_win, gates, o_sb, w_heads):
    b, s, _ = x.shape
    ts = min(512, s)

    def heads(a):
        return pl.BlockSpec((1, a.shape[1], ts, a.shape[3]), lambda bi, i: (bi, 0, i, 0))

    def rows(a):
        return pl.BlockSpec((1, ts, a.shape[2]), lambda bi, i: (bi, i, 0))

    xspec = pl.BlockSpec((1, ts, D_MODEL), lambda bi, i: (bi, i, 0))
    return pl.pallas_call(
        _out_kernel,
        grid=(b, s // ts),
        in_specs=[xspec, rows(o_mla), heads(o_cmp), heads(o_sel), heads(o_win), rows(gates), heads(o_sb),
                  pl.BlockSpec(w_heads.shape, lambda bi, i: (0, 0))],
        out_specs=xspec,
        out_shape=jax.ShapeDtypeStruct(x.shape, _F32),
        compiler_params=_params("parallel", "parallel"),
        name="out_proj",
    )(x, o_mla, o_cmp, o_sel, o_win, gates, o_sb, w_heads)


def _gather_cols(w, idx):
    idx = np.asarray(idx)
    cols = jnp.take(w, jnp.asarray(np.maximum(idx, 0)), axis=1)
    return jnp.where(jnp.asarray(idx >= 0)[None, :], cols, 0.0).astype(_MXU)


def _swap_halves(rot):
    return (np.arange(rot) + rot // 2) % rot


def _w_in_index():
    idx = np.full((_N_SLOTS * LANES,), -1, np.int64)

    def put(slot, lane, src):
        src = np.asarray(src)
        idx[slot * LANES + lane:slot * LANES + lane + len(src)] = src

    put(_S_CQ, 0, _O_CQ + np.arange(MLA_Q_LORA))
    put(_S_CKV, 0, _O_CKV + np.arange(MLA_KV_LORA))
    put(_S_KR, MLA_NOPE, _O_KR + np.arange(MLA_ROPE))
    put(_S_KRS, MLA_NOPE, _O_KR + _swap_halves(MLA_ROPE))
    for h in range(NSA_HEADS):
        put(_S_NQ + h, 0, _O_NQ + h * HEAD_DIM + np.arange(HEAD_DIM))
        put(_S_NQS + h, 0, _O_NQ + h * HEAD_DIM + _swap_halves(PARTIAL_ROT))
    for sk, sks, sv, ok, ov in ((_S_KC, _S_KCS, _S_VC, _O_NKC, _O_NVC), (_S_KS, _S_KSS, _S_VS, _O_NKS, _O_NVS),
                                (_S_KW, _S_KWS, _S_VW, _O_NKW, _O_NVW)):
        for g in range(NSA_KV_HEADS):
            put(sk + g, 0, ok + g * HEAD_DIM + np.arange(HEAD_DIM))
            put(sks + g, 0, ok + g * HEAD_DIM + _swap_halves(PARTIAL_ROT))
            put(sv + g, 0, ov + g * HEAD_DIM + np.arange(HEAD_DIM))
    put(_S_GATE, 0, _O_GATE + np.arange(N_GATES))
    for h in range(SB_HEADS):
        put(_S_SBQ + h, 0, _O_SBQ + h * HEAD_DIM + np.arange(HEAD_DIM))
        put(_S_SBK + h, 0, _O_SBK + h * HEAD_DIM + np.arange(HEAD_DIM))
        put(_S_SBV + h, 0, _O_SBV + h * HEAD_DIM + np.arange(HEAD_DIM))
    return idx


def _mla_up_index():
    qd = MLA_NOPE + MLA_ROPE
    kd = MLA_NOPE + MLA_V
    uq = np.full((MLA_HEADS * LANES,), -1, np.int64)
    uqs = uq.copy()
    uk = uq.copy()
    uv = uq.copy()
    for h in range(MLA_HEADS):
        uq[h * LANES:h * LANES + qd] = h * qd + np.arange(qd)
        uqs[h * LANES + MLA_NOPE:h * LANES + qd] = h * qd + MLA_NOPE + _swap_halves(MLA_ROPE)
        uk[h * LANES:h * LANES + MLA_NOPE] = h * kd + np.arange(MLA_NOPE)
        uv[h * LANES:h * LANES + MLA_V] = h * kd + MLA_NOPE + np.arange(MLA_V)
    return uq, uqs, uk, uv


def _rope_tables(s):
    pos = jnp.arange(s, dtype=_F32)

    def cs(rot):
        half = rot // 2
        inv_freq = ROPE_THETA ** (-jnp.arange(half, dtype=_F32) / half)
        ang = pos[:, None] * inv_freq[None, :]
        c, sn = jnp.cos(ang), jnp.sin(ang)
        return jnp.concatenate([c, c], axis=1), jnp.concatenate([-sn, sn], axis=1)

    c, sn = cs(MLA_ROPE)
    ones = jnp.ones((s, MLA_NOPE), _F32)
    zeros = jnp.zeros((s, MLA_NOPE), _F32)
    pad = jnp.zeros((s, LANES - MLA_NOPE - MLA_ROPE), _F32)
    ck = jnp.concatenate([ones, c, pad], axis=1)
    sk = jnp.concatenate([zeros, sn, pad], axis=1)
    q_scale = (MLA_NOPE + MLA_ROPE) ** -0.5 * LOG2_E
    c, sn = cs(PARTIAL_ROT)
    c64 = jnp.concatenate([c, jnp.ones((s, HEAD_DIM - PARTIAL_ROT), _F32)], axis=1)
    s64 = jnp.concatenate([sn, jnp.zeros((s, HEAD_DIM - PARTIAL_ROT), _F32)], axis=1)
    return ck * q_scale, sk * q_scale, ck, sk, c64, s64


def kernel(x, ffn1_norm, ffn1_w_gate, ffn1_w_up, ffn1_w_down, mix_norm, w_in, mla_q_norm, mla_w_uq, mla_kv_norm,
           mla_w_ukv, nsa_gate_bias, nsa_cmp_pos_k, nsa_cmp_w1_k, nsa_cmp_w2_k, nsa_cmp_pos_v, nsa_cmp_w1_v,
           nsa_cmp_w2_v, w_out, ffn2_norm, ffn2_w_gate, ffn2_w_up, ffn2_w_down, final_norm):
    b, s, d = x.shape
    depth = w_in.shape[0]
    tabs = _rope_tables(s)
    in_idx = _w_in_index()
    uq_idx, uqs_idx, uk_idx, uv_idx = _mla_up_index()
    half = CMP_LEN * HEAD_DIM // 2
    fg = final_norm.reshape(1, d)

    def cmp_weights(w1, w2, pos):
        pos = jnp.broadcast_to(pos.reshape(2, 1, half), (2, 8, half)).astype(_MXU)
        return w1.reshape(2, half, CMP_HIDDEN).astype(_MXU), w2.astype(_MXU), pos

    for l in range(depth):
        x2d = _ffn(x.reshape(b * s, d), ffn1_norm[l].reshape(1, d), ffn1_w_gate[l].astype(_MXU),
                   ffn1_w_up[l].astype(_MXU), ffn1_w_down[l].astype(_MXU), fg, False)
        x = x2d.reshape(b, s, d)
        (mq, mk, mv, nq, nkc, nvc, nks, nvs, nkw, nvw, gates, sbq, sbk, sbv) = _proj(
            x, mix_norm[l].reshape(1, d), _gather_cols(w_in[l], in_idx),
            mla_q_norm[l].reshape(1, -1), _gather_cols(mla_w_uq[l], uq_idx), _gather_cols(mla_w_uq[l], uqs_idx),
            mla_kv_norm[l].reshape(1, -1), _gather_cols(mla_w_ukv[l], uk_idx),
            mla_w_ukv[l].reshape(MLA_KV_LORA, MLA_HEADS, 2, MLA_V)[:, :, 1].transpose(1, 2, 0).astype(_MXU),
            nsa_gate_bias[l].reshape(1, -1), tabs)
        o_mla = _mla_attention(mq, mk, mv)
        kc, vc = _compress(nkc, nvc, *cmp_weights(nsa_cmp_w1_k[l], nsa_cmp_w2_k[l], nsa_cmp_pos_k[l]),
                           *cmp_weights(nsa_cmp_w1_v[l], nsa_cmp_w2_v[l], nsa_cmp_pos_v[l]))
        o_cmp, sel = _cmp_select(nq, kc, vc)
        o_sel = _sel_attention(nq, nks, nvs, sel)
        o_win = _win_attention(nq, nkw, nvw)
        o_sb = _sb_attention(sbq, sbk, sbv)
        x = _out_proj(x, o_mla, o_cmp, o_sel, o_win, gates, o_sb, w_out[l].astype(_MXU))
        x2d = _ffn(x.reshape(b * s, d), ffn2_norm[l].reshape(1, d), ffn2_w_gate[l].astype(_MXU),
                   ffn2_w_up[l].astype(_MXU), ffn2_w_down[l].astype(_MXU), fg, l == depth - 1)
        x = x2d.reshape(b, s, d)
    return x
```

```python
import functools
import math

import numpy as np
import jax
import jax.numpy as jnp
from jax import lax
from jax.experimental import pallas as pl
from jax.experimental.pallas import tpu as pltpu

D_MODEL = 1024
HEAD_DIM = 64
MLA_HEADS = 6
MLA_NOPE = 64
MLA_ROPE = 32
MLA_V = 64
MLA_Q_LORA = 256
MLA_KV_LORA = 128
NSA_HEADS = 6
NSA_KV_HEADS = 2
NSA_GROUP = NSA_HEADS // NSA_KV_HEADS
NSA_BRANCHES = 3
CMP_LEN = 32
CMP_STRIDE = 16
CMP_HIDDEN = 128
SEL_LEN = 64
SEL_TOPK = 16
WINDOW = 512
SB_HEADS = 4
D_FF = 2816
ROPE_THETA = 500000.0
PARTIAL_ROT = HEAD_DIM // 4
EPS = 1e-6
NEG_INF = -1e30
FORCE_SCORE = 1e4
F32_EXP_ZERO = -104.0
LOG2_E = math.log2(math.e)
N_GATES = NSA_HEADS * NSA_BRANCHES

LANES = 128
TOKEN_CHUNK = 256
ONES_PAD = 16
VMEM_LIMIT = 56 * 1024 * 1024

_MXU = jnp.bfloat16
_F32 = jnp.float32

_IN_WIDTHS = (MLA_Q_LORA, MLA_KV_LORA, MLA_ROPE, NSA_HEADS * HEAD_DIM) + (NSA_KV_HEADS * HEAD_DIM,) * 6 + (
    N_GATES, SB_HEADS * HEAD_DIM, SB_HEADS * HEAD_DIM, SB_HEADS * HEAD_DIM)
_IN_OFF = np.concatenate([[0], np.cumsum(_IN_WIDTHS)])
(_O_CQ, _O_CKV, _O_KR, _O_NQ, _O_NKC, _O_NVC, _O_NKS, _O_NVS, _O_NKW, _O_NVW, _O_GATE, _O_SBQ, _O_SBK,
 _O_SBV) = [int(v) for v in _IN_OFF[:-1]]

_S_CQ, _S_CKV, _S_KR, _S_KRS = 0, 2, 3, 4
_S_NQ, _S_NQS = 5, 11
_S_KC, _S_KCS, _S_VC = 17, 19, 21
_S_KS, _S_KSS = 23, 25
_S_KW, _S_KWS = 27, 29
_S_SBQ, _S_SBK, _S_SBV = 31, 35, 39
_N_SLOTS = 43
_T_VS, _T_VW, _T_GATE = 0, NSA_KV_HEADS * HEAD_DIM, 2 * NSA_KV_HEADS * HEAD_DIM
GATE_ROWS = 24
_T_ROWS = _T_GATE + 32


def _dot(a, b):
    return jnp.dot(a.astype(_MXU), b.astype(_MXU), preferred_element_type=_F32)


def _dot_nt(a, b):
    return lax.dot_general(a.astype(_MXU), b.astype(_MXU), (((1,), (1,)), ((), ())),
                           preferred_element_type=_F32)


def _dot_split(a, b):
    hi = a.astype(_MXU)
    lo = (a - hi.astype(_F32)).astype(_MXU)
    return (jnp.dot(hi, b, preferred_element_type=_F32) + jnp.dot(lo, b, preferred_element_type=_F32))


def _dot_split_rhs(a, b):
    hi = b.astype(_MXU)
    lo = (b - hi.astype(_F32)).astype(_MXU)
    return (jnp.dot(a, hi, preferred_element_type=_F32) + jnp.dot(a, lo, preferred_element_type=_F32))


def _rms(x, g):
    return x * lax.rsqrt(jnp.mean(x * x, axis=-1, keepdims=True) + EPS) * g


def _params(*sem):
    return pltpu.CompilerParams(dimension_semantics=sem, vmem_limit_bytes=VMEM_LIMIT)


def _ffn_kernel(x_ref, g_ref, wg_ref, wu_ref, wd_ref, fg_ref, o_ref, h_ref, acc_ref, *, final_norm):
    j = pl.program_id(1)

    @pl.when(j == 0)
    def _():
        h_ref[...] = _rms(x_ref[...], g_ref[...]).astype(h_ref.dtype)
        acc_ref[...] = jnp.zeros_like(acc_ref)

    h = h_ref[...]
    gate = jnp.dot(h, wg_ref[...], preferred_element_type=_F32)
    up = jnp.dot(h, wu_ref[...], preferred_element_type=_F32)
    act = gate * jax.nn.sigmoid(gate) * up
    acc_ref[...] += _dot(act, wd_ref[...])

    @pl.when(j == pl.num_programs(1) - 1)
    def _():
        y = x_ref[...] + 0.5 * acc_ref[...]
        if final_norm:
            y = _rms(y, fg_ref[...])
        o_ref[...] = y


def _ffn(x2d, g, wg, wu, wd, fg, final_norm):
    rows = x2d.shape[0]
    tm = min(1024, rows)
    tf = 256
    grid = (rows // tm, D_FF // tf)
    return pl.pallas_call(
        functools.partial(_ffn_kernel, final_norm=final_norm),
        grid=grid,
        in_specs=[
            pl.BlockSpec((tm, D_MODEL), lambda i, j: (i, 0)),
            pl.BlockSpec((1, D_MODEL), lambda i, j: (0, 0)),
            pl.BlockSpec((D_MODEL, tf), lambda i, j: (0, j)),
            pl.BlockSpec((D_MODEL, tf), lambda i, j: (0, j)),
            pl.BlockSpec((tf, D_MODEL), lambda i, j: (j, 0)),
            pl.BlockSpec((1, D_MODEL), lambda i, j: (0, 0)),
        ],
        out_specs=pl.BlockSpec((tm, D_MODEL), lambda i, j: (i, 0)),
        out_shape=jax.ShapeDtypeStruct((rows, D_MODEL), _F32),
        scratch_shapes=[pltpu.VMEM((tm, D_MODEL), _MXU), pltpu.VMEM((tm, D_MODEL), _F32)],
        compiler_params=_params("parallel", "arbitrary"),
        name="ffn",
    )(x2d, g, wg, wu, wd, fg)


def _proj_kernel(x_ref, g_ref, w_ref, wt_ref, qn_ref, wuq_ref, wuqs_ref, kvn_ref, wuk_ref, wuv_ref, gb_ref,
                 cq_ref, sq_ref, ck_ref, sk_ref, c64_ref, s64_ref,
                 mq_ref, mk_ref, mv_ref, nq_ref, nkc_ref, nvc_ref, nks_ref, nvs_ref, nkw_ref, nvw_ref,
                 gate_ref, sbq_ref, sbk_ref, sbv_ref):
    hn = _rms(x_ref[0], g_ref[...]).astype(_MXU)

    def proj(s0, s1):
        return jnp.dot(hn, w_ref[:, s0 * LANES:s1 * LANES], preferred_element_type=_F32)

    def slot(p, s):
        return p[:, s * LANES:(s + 1) * LANES]

    def head(p, s):
        return p[:, s * LANES:s * LANES + HEAD_DIM]

    p = proj(_S_CQ, _S_KRS + 1)
    cq = _rms(p[:, :MLA_Q_LORA], qn_ref[...])
    ckv = _rms(slot(p, _S_CKV), kvn_ref[...])
    q = _dot(cq, wuq_ref[...])
    q_partner = _dot(cq, wuqs_ref[...])
    kpe = slot(p, _S_KR) * ck_ref[...] + slot(p, _S_KRS) * sk_ref[...]
    kn = _dot(ckv, wuk_ref[...])
    for h in range(MLA_HEADS):
        mq_ref[0, h] = (slot(q, h) * cq_ref[...] + slot(q_partner, h) * sq_ref[...]).astype(mq_ref.dtype)
        mk_ref[0, h] = (slot(kn, h) + kpe).astype(mk_ref.dtype)
        mv_ref[0, h, 0] = _ones_row_pad(_dot_nt(wuv_ref[h], ckv)).astype(mv_ref.dtype)

    c64 = c64_ref[...]
    s64 = s64_ref[...]
    scale = HEAD_DIM ** -0.5

    p = proj(_S_NQ, _S_KC)
    for h in range(NSA_HEADS):
        nq_ref[0, h] = ((head(p, h) * c64 + head(p, NSA_HEADS + h) * s64) * (scale * LOG2_E)).astype(nq_ref.dtype)

    p = proj(_S_KC, _S_SBQ)
    base = _S_KC
    for k_ref, sk, sks in ((nkc_ref, _S_KC, _S_KCS), (nks_ref, _S_KS, _S_KSS), (nkw_ref, _S_KW, _S_KWS)):
        for g in range(NSA_KV_HEADS):
            k_ref[0, g] = (head(p, sk - base + g) * c64 + head(p, sks - base + g) * s64).astype(k_ref.dtype)
    for g in range(NSA_KV_HEADS):
        nvc_ref[0, g] = head(p, _S_VC - base + g).astype(nvc_ref.dtype)

    pt = _dot_nt(wt_ref[...], hn)
    for g in range(NSA_KV_HEADS):
        lo = g * HEAD_DIM
        nvs_ref[0, g, 0] = _ones_row_pad(pt[_T_VS + lo:_T_VS + lo + HEAD_DIM]).astype(nvs_ref.dtype)
        nvw_ref[0, g, 0] = _ones_row_pad(pt[_T_VW + lo:_T_VW + lo + HEAD_DIM]).astype(nvw_ref.dtype)
    gate_ref[0] = jax.nn.sigmoid(pt[_T_GATE:_T_GATE + GATE_ROWS] + gb_ref[...])

    p = proj(_S_SBQ, _N_SLOTS)
    for h in range(SB_HEADS):
        sbq_ref[0, h] = (head(p, h) * scale).astype(sbq_ref.dtype)
        sbk_ref[0, h] = head(p, SB_HEADS + h).astype(sbk_ref.dtype)
        sbv_ref[0, h] = head(p, 2 * SB_HEADS + h).astype(sbv_ref.dtype)


def _proj(x, g, w_ext, w_t, qn, wuq, wuqs, kvn, wuk, wuv, gb, tabs):
    b, s, _ = x.shape
    ts = min(TOKEN_CHUNK, s)
    cq, sq, ck, sk, c64, s64 = tabs

    def full(a):
        return pl.BlockSpec(a.shape, lambda bi, i: (0,) * a.ndim)

    def tab(a):
        return pl.BlockSpec((ts, a.shape[1]), lambda bi, i: (i, 0))

    def heads(n, d):
        return (pl.BlockSpec((1, n, ts, d), lambda bi, i: (bi, 0, i, 0)),
                jax.ShapeDtypeStruct((b, n, s, d), _MXU))

    def values_t(n):
        return (pl.BlockSpec((1, n, 1, HEAD_DIM + ONES_PAD, ts), lambda bi, i: (bi, 0, i, 0, 0)),
                jax.ShapeDtypeStruct((b, n, s // ts, HEAD_DIM + ONES_PAD, ts), _MXU))

    kv = lambda: heads(NSA_KV_HEADS, HEAD_DIM)
    outs = [heads(MLA_HEADS, LANES), heads(MLA_HEADS, LANES), values_t(MLA_HEADS), heads(NSA_HEADS, HEAD_DIM),
            kv(), kv(), kv(), values_t(NSA_KV_HEADS), kv(), values_t(NSA_KV_HEADS),
            (pl.BlockSpec((1, GATE_ROWS, ts), lambda bi, i: (bi, 0, i)), jax.ShapeDtypeStruct((b, GATE_ROWS, s), _F32)),
            heads(SB_HEADS, HEAD_DIM), heads(SB_HEADS, HEAD_DIM), heads(SB_HEADS, HEAD_DIM)]
    return pl.pallas_call(
        _proj_kernel,
        grid=(b, s // ts),
        in_specs=[pl.BlockSpec((1, ts, D_MODEL), lambda bi, i: (bi, i, 0)), full(g), full(w_ext), full(w_t), full(qn),
                  full(wuq), full(wuqs), full(kvn), full(wuk), full(wuv), full(gb),
                  tab(cq), tab(sq), tab(ck), tab(sk), tab(c64), tab(s64)],
        out_specs=[o[0] for o in outs],
        out_shape=[o[1] for o in outs],
        compiler_params=_params("parallel", "parallel"),
        name="proj",
    )(x, g, w_ext, w_t, qn, wuq, wuqs, kvn, wuk, wuv, gb, cq, sq, ck, sk, c64, s64)


def _softmax_step(carry, s, v):
    m, l, acc = carry
    m_new = jnp.maximum(m, jnp.max(s, axis=-1, keepdims=True))
    alpha = jnp.exp(m - m_new)
    p = jnp.exp(s - m_new)
    l = alpha * l + jnp.sum(p, axis=-1, keepdims=True)
    acc = alpha * acc + _dot(p, v)
    return m_new, l, acc


def _softmax_init(rows, d):
    return (jnp.full((rows, 1), NEG_INF, _F32), jnp.zeros((rows, 1), _F32), jnp.zeros((rows, d), _F32))


def _ones_row_pad(vt):
    first = lax.broadcasted_iota(jnp.int32, (ONES_PAD, vt.shape[1]), 0) == 0
    return jnp.concatenate([vt, jnp.where(first, 1.0, 0.0).astype(vt.dtype)], axis=0)


def _softmax_step_t(carry, st, vt_chunks):
    m, acc = carry
    m_new = jnp.maximum(m, jnp.max(st, axis=0, keepdims=True))
    alpha = jnp.exp2(m - m_new)
    pt = jnp.exp2(st - m_new).astype(_MXU)
    n = st.shape[0] // len(vt_chunks)
    pv = sum(jnp.dot(vt, pt[c * n:(c + 1) * n], preferred_element_type=_F32) for c, vt in enumerate(vt_chunks))
    return m_new, alpha * acc + pv


def _softmax_init_t(d, cols):
    return (jnp.full((1, cols), NEG_INF, _F32), jnp.zeros((d + ONES_PAD, cols), _F32))


def _softmax_finish_t(carry, d):
    _, acc = carry
    return acc[:d] * (1.0 / acc[d:d + 1])


def _two_chain_sweep(n_full, qk, soft, init):
    qk(1, 0)

    def body(j, carry):
        c0, c1 = carry
        qk(0, j)
        c1 = soft(1, j, c1, False)
        qk(1, j + 1)
        c0 = soft(0, j, c0, False)
        return c0, c1

    c0, c1 = lax.fori_loop(0, n_full, body, init)
    qk(0, n_full)
    c1 = soft(1, n_full, c1, True)
    c0 = soft(0, n_full, c0, True)
    return c0, c1


def _mla_kernel(q_ref, k_ref, vt_ref, o_ref, s0_ref, s1_ref, *, t, nsub):
    qi = pl.program_id(2)
    s_refs = (s0_ref, s1_ref)

    def qk(hh, j):
        off = pl.multiple_of(j * t, t)
        s_refs[hh][...] = _dot_nt(k_ref[0, hh, pl.ds(off, t), :], q_ref[0, hh])

    def soft(hh, j, carry, diag):
        st = s_refs[hh][...]
        if diag:
            key = lax.broadcasted_iota(jnp.int32, (t, t), 0)
            qry = lax.broadcasted_iota(jnp.int32, (t, t), 1)
            st = jnp.where(key <= qry, st, NEG_INF)
        return _softmax_step_t(carry, st, [vt_ref[0, hh, j * nsub + c] for c in range(nsub)])

    carry = _two_chain_sweep(qi, qk, soft, tuple(_softmax_init_t(MLA_V, t) for _ in range(2)))
    ot = jnp.concatenate([_softmax_finish_t(c, MLA_V) for c in carry], axis=0)
    o_ref[0] = ot.T.astype(o_ref.dtype)


def _mla_attention(q, k, vt):
    b, h, s, _ = q.shape
    tv = vt.shape[-1]
    dv = vt.shape[-2]
    t = min(512, s)
    assert h % 2 == 0 and 2 * MLA_V == LANES and t % tv == 0
    return pl.pallas_call(
        functools.partial(_mla_kernel, t=t, nsub=t // tv),
        grid=(b, h // 2, s // t),
        in_specs=[pl.BlockSpec((1, 2, t, LANES), lambda bi, hi, i: (bi, hi, i, 0)),
                  pl.BlockSpec((1, 2, s, LANES), lambda bi, hi, i: (bi, hi, 0, 0)),
                  pl.BlockSpec((1, 2, s // tv, dv, tv), lambda bi, hi, i: (bi, hi, 0, 0, 0))],
        out_specs=pl.BlockSpec((1, t, LANES), lambda bi, hi, i: (bi, i, hi)),
        out_shape=jax.ShapeDtypeStruct((b, s, h * MLA_V), _MXU),
        scratch_shapes=[pltpu.VMEM((t, t), _F32), pltpu.VMEM((t, t), _F32)],
        compiler_params=_params("parallel", "parallel", "arbitrary"),
        name="mla_attn",
    )(q, k, vt)


def _sb_kernel(q_ref, k_ref, v_ref, u_ref, o_ref, *, t):
    qi = pl.program_id(2)
    q = q_ref[0, 0]
    u = u_ref[...]

    def step(j, carry, diag):
        rem, acc = carry
        off = pl.multiple_of(j * t, t)
        z = _dot_nt(q, k_ref[0, 0, pl.ds(off, t), :])
        log_beta = jnp.minimum(z, 0.0) - jnp.log1p(jnp.exp(-jnp.abs(z)))
        log_rem = log_beta - z
        if diag:
            row = lax.broadcasted_iota(jnp.int32, (t, t), 0)
            col = lax.broadcasted_iota(jnp.int32, (t, t), 1)
            strict = col < row
            log_rem = jnp.where(strict, log_rem, 0.0)
        suffix = _dot_split(log_rem, u)
        a = jnp.exp(log_beta + suffix + rem)
        if diag:
            a = jnp.where(strict, a, 0.0)
        acc = acc + _dot(a, v_ref[0, 0, pl.ds(off, t), :])
        rem = rem + suffix[:, 0:1] + log_rem[:, 0:1]
        return rem, acc

    rem, acc = step(qi, (jnp.zeros((t, 1), _F32), jnp.zeros((t, HEAD_DIM), _F32)), True)

    def live(c):
        return jnp.logical_and(c[0] >= 0, jnp.max(c[1]) > F32_EXP_ZERO)

    def earlier(c):
        return (c[0] - 1,) + step(c[0], c[1:], False)

    _, _, acc = lax.while_loop(live, earlier, (qi - 1, rem, acc))
    o_ref[0, 0] = acc.astype(o_ref.dtype)


def _sb_attention(q, k, v):
    b, h, s, d = q.shape
    t = min(256, s)
    idx = np.arange(t)
    u = jnp.asarray(idx[:, None] > idx[None, :], _MXU)
    return pl.pallas_call(
        functools.partial(_sb_kernel, t=t),
        grid=(b, h, s // t),
        in_specs=[pl.BlockSpec((1, 1, t, d), lambda bi, hi, i: (bi, hi, i, 0)),
                  pl.BlockSpec((1, 1, s, d), lambda bi, hi, i: (bi, hi, 0, 0)),
                  pl.BlockSpec((1, 1, s, d), lambda bi, hi, i: (bi, hi, 0, 0)),
                  pl.BlockSpec((t, t), lambda bi, hi, i: (0, 0))],
        out_specs=pl.BlockSpec((1, 1, t, d), lambda bi, hi, i: (bi, hi, i, 0)),
        out_shape=jax.ShapeDtypeStruct((b, h, s, d), _MXU),
        compiler_params=_params("parallel", "parallel", "arbitrary"),
        name="sb_attn",
    )(q, k, v, u)


def _compress_kernel(xk_ref, xv_ref, w1k_ref, w2k_ref, pk_ref, w1v_ref, w2v_ref, pv_ref, ok_ref, ov_ref):
    def hidden(x_ref, w1_ref, p_ref):
        x = x_ref[0, 0]
        n = x.shape[0]
        first = jnp.dot(x, w1_ref[0], preferred_element_type=_F32)
        second = jnp.dot(x, w1_ref[1], preferred_element_type=_F32)
        pos = _dot(p_ref[0], w1_ref[0]) + _dot(p_ref[1], w1_ref[1])
        hid = first + pltpu.roll(second, n - 1, 0) + pos[0:1]
        return 0.5 * hid * (1.0 + jnp.tanh(math.sqrt(2.0 / math.pi) * (hid + 0.044715 * hid * hid * hid)))

    ok_ref[0, 0] = _dot(hidden(xk_ref, w1k_ref, pk_ref), w2k_ref[...]).astype(ok_ref.dtype)
    ov_ref[0, 0] = _dot_nt(w2v_ref[...], hidden(xv_ref, w1v_ref, pv_ref)).astype(ov_ref.dtype)


def _compress(xk, xv, w1k, w2k, pk, w1v, w2v, pv):
    b, g, s, d = xk.shape
    n = s // CMP_STRIDE
    xk = xk.reshape(b, g, n, CMP_STRIDE * d)
    xv = xv.reshape(b, g, n, CMP_STRIDE * d)

    def full(a):
        return pl.BlockSpec(a.shape, lambda bi, gi: (0,) * a.ndim)

    xspec = pl.BlockSpec((1, 1, n, CMP_STRIDE * d), lambda bi, gi: (bi, gi, 0, 0))
    return pl.pallas_call(
        _compress_kernel,
        grid=(b, g),
        in_specs=[xspec, xspec, full(w1k), full(w2k), full(pk), full(w1v), full(w2v), full(pv)],
        out_specs=[pl.BlockSpec((1, 1, n, d), lambda bi, gi: (bi, gi, 0, 0)),
                   pl.BlockSpec((1, 1, d, n), lambda bi, gi: (bi, gi, 0, 0))],
        out_shape=[jax.ShapeDtypeStruct((b, g, n, d), _MXU), jax.ShapeDtypeStruct((b, g, d, n), _MXU)],
        compiler_params=_params("parallel", "parallel"),
        name="nsa_compress",
    )(xk, xv, w1k, w2k, pk, w1v, w2v, pv)


def _group_queries(q_ref, g, tq):
    return q_ref[0, g * NSA_GROUP:(g + 1) * NSA_GROUP].reshape(NSA_GROUP * tq, q_ref.shape[-1])


def _gated_heads(ot, gt_ref, g, branch, tq):
    out = []
    for r in range(NSA_GROUP):
        row = NSA_BRANCHES * (g * NSA_GROUP + r) + branch
        out.append(ot[:, r * tq:(r + 1) * tq] * gt_ref[0, row:row + 1, :])
    return out


def _cmp_kernel(q_ref, kc_ref, vct_ref, ov_ref, gt_ref, o_ref, sel_ref, s0_ref, s1_ref, *, tq, n_top):
    q0 = pl.program_id(1) * tq
    ncp = kc_ref.shape[2]
    ns = sel_ref.shape[-1]
    lanes = NSA_GROUP * tq
    s_refs = (s0_ref, s1_ref)
    for g in range(NSA_KV_HEADS):
        s_refs[g][...] = _dot_nt(kc_ref[0, g], _group_queries(q_ref, g, tq))
    qpos = q0 + (lax.broadcasted_iota(jnp.int32, (1, lanes), 1) & (tq - 1))
    cmp_end = lax.broadcasted_iota(jnp.int32, (ncp, 1), 0) * CMP_STRIDE + (CMP_LEN - 1)
    visible = cmp_end <= qpos
    cur = jnp.right_shift(q0 + lax.broadcasted_iota(jnp.int32, (1, tq), 1), int(math.log2(SEL_LEN)))
    blk = lax.broadcasted_iota(jnp.int32, (ns, 1), 0)
    forced = (blk == 0) | (blk == cur) | (blk == cur - 1)
    future = blk > cur
    blk_f = blk.astype(_F32)
    heads = []
    for g in range(NSA_KV_HEADS):
        st = jnp.where(visible, s_refs[g][...], NEG_INF)
        e = jnp.exp2(st - jnp.max(st, axis=0, keepdims=True))
        inv = jnp.where(qpos >= CMP_LEN - 1, 1.0 / jnp.sum(e, axis=0, keepdims=True), 0.0)
        pt = e * inv
        heads += _gated_heads(_dot(vct_ref[0, g], pt), gt_ref, g, 0, tq)
        p_sum = sum(pt[:, r * tq:(r + 1) * tq] for r in range(NSA_GROUP))
        score = _dot_split_rhs(ov_ref[...], p_sum)
        score = jnp.where(forced, FORCE_SCORE, jnp.where(future, -1.0, score))
        sel = jnp.zeros((ns, tq), _F32)
        for _ in range(n_top):
            top = jnp.max(score, axis=0, keepdims=True)
            first = jnp.min(jnp.where(score == top, blk_f, float(ns)), axis=0, keepdims=True)
            pick = blk_f == first
            sel = jnp.where(pick, 1.0, sel)
            score = jnp.where(pick, -3e38, score)
        sel_ref[0, g] = (sel.T - 1.0).astype(sel_ref.dtype)
    o_ref[0] = jnp.concatenate(heads, axis=0).T


def _cmp_select(q, kc, vct, gates_t):
    b, h, s, d = q.shape
    g = kc.shape[1]
    ncp = kc.shape[2]
    ns = s // SEL_LEN
    n_top = min(SEL_TOPK, ns)
    tq = min(256, s)
    assert tq & (tq - 1) == 0 and g == 2
    c0 = np.arange(ncp)[:, None] * CMP_STRIDE
    n0 = np.arange(ns)[None, :] * SEL_LEN
    overlap = jnp.asarray(((c0 < n0 + SEL_LEN) & (c0 + CMP_LEN > n0)).T, _MXU)
    return pl.pallas_call(
        functools.partial(_cmp_kernel, tq=tq, n_top=n_top),
        grid=(b, s // tq),
        in_specs=[pl.BlockSpec((1, h, tq, d), lambda bi, i: (bi, 0, i, 0)),
                  pl.BlockSpec((1, g, ncp, d), lambda bi, i: (bi, 0, 0, 0)),
                  pl.BlockSpec((1, g, d, ncp), lambda bi, i: (bi, 0, 0, 0)),
                  pl.BlockSpec((ns, ncp), lambda bi, i: (0, 0)),
                  pl.BlockSpec((1, GATE_ROWS, tq), lambda bi, i: (bi, 0, i))],
        out_specs=[pl.BlockSpec((1, tq, h * d), lambda bi, i: (bi, i, 0)),
                   pl.BlockSpec((1, g, tq, ns), lambda bi, i: (bi, 0, i, 0))],
        out_shape=[jax.ShapeDtypeStruct((b, s, h * d), _F32), jax.ShapeDtypeStruct((b, g, s, ns), _MXU)],
        scratch_shapes=[pltpu.VMEM((ncp, NSA_GROUP * tq), _F32) for _ in range(g)],
        compiler_params=_params("parallel", "arbitrary"),
        name="nsa_cmp_select",
    )(q, kc, vct, overlap, gates_t)


def _sel_kernel(q_ref, k_ref, vt_ref, gt_ref, o_ref, s0_ref, s1_ref, *, tq, tk, nsub):
    q0 = pl.program_id(1) * tq
    last = (q0 + tq - 1) // tk
    lanes = NSA_GROUP * tq
    s_refs = (s0_ref, s1_ref)

    def qk(g, j):
        off = pl.multiple_of(j * tk, tk)
        s_refs[g][...] = _dot_nt(k_ref[0, g, pl.ds(off, tk), :], _group_queries(q_ref, g, tq))

    def soft(g, j, carry, causal):
        st = s_refs[g][...]
        if causal:
            key = j * tk + lax.broadcasted_iota(jnp.int32, (tk, lanes), 0)
            qry = q0 + (lax.broadcasted_iota(jnp.int32, (tk, lanes), 1) & (tq - 1))
            st = jnp.where(key <= qry, st, NEG_INF)
        return _softmax_step_t(carry, st, [vt_ref[0, g, j * nsub + c] for c in range(nsub)])

    init = tuple(_softmax_init_t(HEAD_DIM, lanes) for _ in range(NSA_KV_HEADS))
    carry = _two_chain_sweep(last, qk, soft, init)
    heads = []
    for g in range(NSA_KV_HEADS):
        heads += _gated_heads(_softmax_finish_t(carry[g], HEAD_DIM), gt_ref, g, 1, tq)
    o_ref[0] = jnp.concatenate(heads, axis=0).T


def _sel_attention(q, k, vt, sel_m1, gates_t):
    b, h, s, d = q.shape
    g = k.shape[1]
    ns = sel_m1.shape[-1]
    tv = vt.shape[-1]
    tq = min(256, s)
    tk = min(512, s)
    assert tq & (tq - 1) == 0 and s % tk == 0 and tk % tv == 0 and g == 2
    onehot = (np.arange(s)[:, None] // SEL_LEN == np.arange(ns)[None, :]) * -NEG_INF
    k = jnp.concatenate([k, jnp.broadcast_to(jnp.asarray(onehot, _MXU), (b, g, s, ns))], axis=-1)
    q = jnp.concatenate([q, jnp.repeat(sel_m1, NSA_GROUP, axis=1)], axis=-1)
    da = d + ns
    return pl.pallas_call(
        functools.partial(_sel_kernel, tq=tq, tk=tk, nsub=tk // tv),
        grid=(b, s // tq),
        in_specs=[pl.BlockSpec((1, h, tq, da), lambda bi, i: (bi, 0, i, 0)),
                  pl.BlockSpec((1, g, s, da), lambda bi, i: (bi, 0, 0, 0)),
                  pl.BlockSpec((1, g) + vt.shape[2:], lambda bi, i: (bi, 0, 0, 0, 0)),
                  pl.BlockSpec((1, GATE_ROWS, tq), lambda bi, i: (bi, 0, i))],
        out_specs=pl.BlockSpec((1, tq, h * d), lambda bi, i: (bi, i, 0)),
        out_shape=jax.ShapeDtypeStruct((b, s, h * d), _F32),
        scratch_shapes=[pltpu.VMEM((tk, NSA_GROUP * tq), _F32) for _ in range(g)],
        compiler_params=_params("parallel", "arbitrary"),
        name="nsa_selected",
    )(q, k, vt, gates_t)


def _win_kernel(q_ref, k_ref, vt_ref, gt_ref, o_ref, s0_ref, s1_ref, *, tq, span, tv):
    q0 = pl.program_id(1) * tq
    start = pl.multiple_of(jnp.maximum(q0 - WINDOW, 0), tq)
    first_chunk = start // tv
    lanes = NSA_GROUP * tq
    s_refs = (s0_ref, s1_ref)
    for g in range(NSA_KV_HEADS):
        s_refs[g][...] = _dot_nt(k_ref[0, g, pl.ds(start, span), :], _group_queries(q_ref, g, tq))
    key = start + lax.broadcasted_iota(jnp.int32, (span, lanes), 0)
    qry = q0 + (lax.broadcasted_iota(jnp.int32, (span, lanes), 1) & (tq - 1))
    heads = []
    for g in range(NSA_KV_HEADS):
        st = jnp.where(key <= qry, s_refs[g][...], NEG_INF)
        st = jnp.where(key > qry - WINDOW, st, NEG_INF)
        carry = _softmax_step_t(_softmax_init_t(HEAD_DIM, lanes), st,
                                [vt_ref[0, g, first_chunk + c] for c in range(span // tv)])
        heads += _gated_heads(_softmax_finish_t(carry, HEAD_DIM), gt_ref, g, 2, tq)
    o_ref[0] = jnp.concatenate(heads, axis=0).T


def _win_attention(q, k, vt, gates_t):
    b, h, s, d = q.shape
    g = k.shape[1]
    tv = vt.shape[-1]
    tq = min(256, s)
    span = WINDOW + tq
    assert tq & (tq - 1) == 0 and s >= span and tq % tv == 0 and WINDOW % tv == 0 and g == 2
    return pl.pallas_call(
        functools.partial(_win_kernel, tq=tq, span=span, tv=tv),
        grid=(b, s // tq),
        in_specs=[pl.BlockSpec((1, h, tq, d), lambda bi, i: (bi, 0, i, 0)),
                  pl.BlockSpec((1, g, s, d), lambda bi, i: (bi, 0, 0, 0)),
                  pl.BlockSpec((1, g) + vt.shape[2:], lambda bi, i: (bi, 0, 0, 0, 0)),
                  pl.BlockSpec((1, GATE_ROWS, tq), lambda bi, i: (bi, 0, i))],
        out_specs=pl.BlockSpec((1, tq, h * d), lambda bi, i: (bi, i, 0)),
        out_shape=jax.ShapeDtypeStruct((b, s, h * d), _F32),
        scratch_shapes=[pltpu.VMEM((span, NSA_GROUP * tq), _F32) for _ in range(g)],
        compiler_params=_params("parallel", "arbitrary"),
        name="nsa_window",
    )(q, k, vt, gates_t)


def _out_kernel(x_ref, mla_ref, cmp_ref, sel_ref, win_ref, sb_ref, w_ref, o_ref):
    def w_rows(first_head, n_heads):
        return w_ref[first_head * HEAD_DIM:(first_head + n_heads) * HEAD_DIM, :]

    acc = x_ref[0] + jnp.dot(mla_ref[0], w_rows(0, MLA_HEADS), preferred_element_type=_F32)
    nsa = cmp_ref[0] + sel_ref[0] + win_ref[0]
    acc = acc + _dot(nsa, w_rows(MLA_HEADS, NSA_HEADS))
    for h in range(SB_HEADS):
        acc = acc + jnp.dot(sb_ref[0, h], w_rows(MLA_HEADS + NSA_HEADS + h, 1), preferred_element_type=_F32)
    o_ref[0] = acc


def _out_proj(x, o_mla, o_cmp, o_sel, o_win, o_sb, w_heads):
    b, s, _ = x.shape
    ts = min(512, s)

    def heads(a):
        return pl.BlockSpec((1, a.shape[1], ts, a.shape[3]), lambda bi, i: (bi, 0, i, 0))

    def rows(a):
        return pl.BlockSpec((1, ts, a.shape[2]), lambda bi, i: (bi, i, 0))

    xspec = pl.BlockSpec((1, ts, D_MODEL), lambda bi, i: (bi, i, 0))
    return pl.pallas_call(
        _out_kernel,
        grid=(b, s // ts),
        in_specs=[xspec, rows(o_mla), rows(o_cmp), rows(o_sel), rows(o_win), heads(o_sb),
                  pl.BlockSpec(w_heads.shape, lambda bi, i: (0, 0))],
        out_specs=xspec,
        out_shape=jax.ShapeDtypeStruct(x.shape, _F32),
        compiler_params=_params("parallel", "parallel"),
        name="out_proj",
    )(x, o_mla, o_cmp, o_sel, o_win, o_sb, w_heads)


def _gather_cols(w, idx):
    idx = np.asarray(idx)
    cols = jnp.take(w, jnp.asarray(np.maximum(idx, 0)), axis=1)
    return jnp.where(jnp.asarray(idx >= 0)[None, :], cols, 0.0).astype(_MXU)


def _swap_halves(rot):
    return (np.arange(rot) + rot // 2) % rot


def _w_in_index():
    idx = np.full((_N_SLOTS * LANES,), -1, np.int64)

    def put(slot, lane, src):
        src = np.asarray(src)
        idx[slot * LANES + lane:slot * LANES + lane + len(src)] = src

    put(_S_CQ, 0, _O_CQ + np.arange(MLA_Q_LORA))
    put(_S_CKV, 0, _O_CKV + np.arange(MLA_KV_LORA))
    put(_S_KR, MLA_NOPE, _O_KR + np.arange(MLA_ROPE))
    put(_S_KRS, MLA_NOPE, _O_KR + _swap_halves(MLA_ROPE))
    for h in range(NSA_HEADS):
        put(_S_NQ + h, 0, _O_NQ + h * HEAD_DIM + np.arange(HEAD_DIM))
        put(_S_NQS + h, 0, _O_NQ + h * HEAD_DIM + _swap_halves(PARTIAL_ROT))
    for sk, sks, ok in ((_S_KC, _S_KCS, _O_NKC), (_S_KS, _S_KSS, _O_NKS), (_S_KW, _S_KWS, _O_NKW)):
        for g in range(NSA_KV_HEADS):
            put(sk + g, 0, ok + g * HEAD_DIM + np.arange(HEAD_DIM))
            put(sks + g, 0, ok + g * HEAD_DIM + _swap_halves(PARTIAL_ROT))
    for g in range(NSA_KV_HEADS):
        put(_S_VC + g, 0, _O_NVC + g * HEAD_DIM + np.arange(HEAD_DIM))
    for h in range(SB_HEADS):
        put(_S_SBQ + h, 0, _O_SBQ + h * HEAD_DIM + np.arange(HEAD_DIM))
        put(_S_SBK + h, 0, _O_SBK + h * HEAD_DIM + np.arange(HEAD_DIM))
        put(_S_SBV + h, 0, _O_SBV + h * HEAD_DIM + np.arange(HEAD_DIM))
    return idx


def _mla_up_index():
    qd = MLA_NOPE + MLA_ROPE
    kd = MLA_NOPE + MLA_V
    uq = np.full((MLA_HEADS * LANES,), -1, np.int64)
    uqs = uq.copy()
    uk = uq.copy()
    for h in range(MLA_HEADS):
        uq[h * LANES:h * LANES + qd] = h * qd + np.arange(qd)
        uqs[h * LANES + MLA_NOPE:h * LANES + qd] = h * qd + MLA_NOPE + _swap_halves(MLA_ROPE)
        uk[h * LANES:h * LANES + MLA_NOPE] = h * kd + np.arange(MLA_NOPE)
    return uq, uqs, uk


def _transposed_weights(w_in_l, gate_bias):
    width = NSA_KV_HEADS * HEAD_DIM
    rows = jnp.concatenate([w_in_l[:, _O_NVS:_O_NVS + width], w_in_l[:, _O_NVW:_O_NVW + width],
                            w_in_l[:, _O_GATE:_O_GATE + N_GATES]], axis=1).T
    rows = jnp.pad(rows, ((0, _T_ROWS - rows.shape[0]), (0, 0))).astype(_MXU)
    bias = jnp.pad(gate_bias, (0, GATE_ROWS - N_GATES)).reshape(GATE_ROWS, 1)
    return rows, bias


def _rope_tables(s):
    pos = jnp.arange(s, dtype=_F32)

    def cs(rot):
        half = rot // 2
        inv_freq = ROPE_THETA ** (-jnp.arange(half, dtype=_F32) / half)
        ang = pos[:, None] * inv_freq[None, :]
        c, sn = jnp.cos(ang), jnp.sin(ang)
        return jnp.concatenate([c, c], axis=1), jnp.concatenate([-sn, sn], axis=1)

    c, sn = cs(MLA_ROPE)
    ones = jnp.ones((s, MLA_NOPE), _F32)
    zeros = jnp.zeros((s, MLA_NOPE), _F32)
    pad = jnp.zeros((s, LANES - MLA_NOPE - MLA_ROPE), _F32)
    ck = jnp.concatenate([ones, c, pad], axis=1)
    sk = jnp.concatenate([zeros, sn, pad], axis=1)
    q_scale = (MLA_NOPE + MLA_ROPE) ** -0.5 * LOG2_E
    c, sn = cs(PARTIAL_ROT)
    c64 = jnp.concatenate([c, jnp.ones((s, HEAD_DIM - PARTIAL_ROT), _F32)], axis=1)
    s64 = jnp.concatenate([sn, jnp.zeros((s, HEAD_DIM - PARTIAL_ROT), _F32)], axis=1)
    return ck * q_scale, sk * q_scale, ck, sk, c64, s64


def kernel(x, ffn1_norm, ffn1_w_gate, ffn1_w_up, ffn1_w_down, mix_norm, w_in, mla_q_norm, mla_w_uq, mla_kv_norm,
           mla_w_ukv, nsa_gate_bias, nsa_cmp_pos_k, nsa_cmp_w1_k, nsa_cmp_w2_k, nsa_cmp_pos_v, nsa_cmp_w1_v,
           nsa_cmp_w2_v, w_out, ffn2_norm, ffn2_w_gate, ffn2_w_up, ffn2_w_down, final_norm):
    b, s, d = x.shape
    depth = w_in.shape[0]
    tabs = _rope_tables(s)
    in_idx = _w_in_index()
    uq_idx, uqs_idx, uk_idx = _mla_up_index()
    half = CMP_LEN * HEAD_DIM // 2
    fg = final_norm.reshape(1, d)

    def cmp_weights(w1, w2, pos, transpose_out):
        pos = jnp.broadcast_to(pos.reshape(2, 1, half), (2, 8, half)).astype(_MXU)
        w2 = w2.T if transpose_out else w2
        return w1.reshape(2, half, CMP_HIDDEN).astype(_MXU), w2.astype(_MXU), pos

    for l in range(depth):
        x2d = _ffn(x.reshape(b * s, d), ffn1_norm[l].reshape(1, d), ffn1_w_gate[l].astype(_MXU),
                   ffn1_w_up[l].astype(_MXU), ffn1_w_down[l].astype(_MXU), fg, False)
        x = x2d.reshape(b, s, d)
        w_t, gate_bias = _transposed_weights(w_in[l], nsa_gate_bias[l])
        (mq, mk, mvt, nq, nkc, nvc, nks, nvst, nkw, nvwt, gates_t, sbq, sbk, sbv) = _proj(
            x, mix_norm[l].reshape(1, d), _gather_cols(w_in[l], in_idx), w_t,
            mla_q_norm[l].reshape(1, -1), _gather_cols(mla_w_uq[l], uq_idx), _gather_cols(mla_w_uq[l], uqs_idx),
            mla_kv_norm[l].reshape(1, -1), _gather_cols(mla_w_ukv[l], uk_idx),
            mla_w_ukv[l].reshape(MLA_KV_LORA, MLA_HEADS, 2, MLA_V)[:, :, 1].transpose(1, 2, 0).astype(_MXU),
            gate_bias, tabs)
        o_mla = _mla_attention(mq, mk, mvt)
        kc, vct = _compress(nkc, nvc, *cmp_weights(nsa_cmp_w1_k[l], nsa_cmp_w2_k[l], nsa_cmp_pos_k[l], False),
                            *cmp_weights(nsa_cmp_w1_v[l], nsa_cmp_w2_v[l], nsa_cmp_pos_v[l], True))
        o_cmp, sel = _cmp_select(nq, kc, vct, gates_t)
        o_sel = _sel_attention(nq, nks, nvst, sel, gates_t)
        o_win = _win_attention(nq, nkw, nvwt, gates_t)
        o_sb = _sb_attention(sbq, sbk, sbv)
        x = _out_proj(x, o_mla, o_cmp, o_sel, o_win, o_sb, w_out[l].astype(_MXU))
        x2d = _ffn(x.reshape(b * s, d), ffn2_norm[l].reshape(1, d), ffn2_w_gate[l].astype(_MXU),
                   ffn2_w_up[l].astype(_MXU), ffn2_w_down[l].astype(_MXU), fg, l == depth - 1)
        x = x2d.reshape(b, s, d)
    return x
```

```python
import functools
import math

import numpy as np
import jax
import jax.numpy as jnp
from jax import lax
from jax.experimental import pallas as pl
from jax.experimental.pallas import tpu as pltpu

D_MODEL = 1024
HEAD_DIM = 64
MLA_HEADS = 6
MLA_NOPE = 64
MLA_ROPE = 32
MLA_V = 64
MLA_Q_LORA = 256
MLA_KV_LORA = 128
NSA_HEADS = 6
NSA_KV_HEADS = 2
NSA_GROUP = NSA_HEADS // NSA_KV_HEADS
NSA_BRANCHES = 3
CMP_LEN = 32
CMP_STRIDE = 16
CMP_HIDDEN = 128
SEL_LEN = 64
SEL_TOPK = 16
WINDOW = 512
SB_HEADS = 4
D_FF = 2816
ROPE_THETA = 500000.0
PARTIAL_ROT = HEAD_DIM // 4
EPS = 1e-6
NEG_INF = -1e30
FORCE_SCORE = 1e4
F32_EXP_ZERO = -104.0
LOG2_E = math.log2(math.e)
N_GATES = NSA_HEADS * NSA_BRANCHES

LANES = 128
TOKEN_CHUNK = 256
ONES_PAD = 16
VMEM_LIMIT = 56 * 1024 * 1024

_MXU = jnp.bfloat16
_F32 = jnp.float32

_IN_WIDTHS = (MLA_Q_LORA, MLA_KV_LORA, MLA_ROPE, NSA_HEADS * HEAD_DIM) + (NSA_KV_HEADS * HEAD_DIM,) * 6 + (
    N_GATES, SB_HEADS * HEAD_DIM, SB_HEADS * HEAD_DIM, SB_HEADS * HEAD_DIM)
_IN_OFF = np.concatenate([[0], np.cumsum(_IN_WIDTHS)])
(_O_CQ, _O_CKV, _O_KR, _O_NQ, _O_NKC, _O_NVC, _O_NKS, _O_NVS, _O_NKW, _O_NVW, _O_GATE, _O_SBQ, _O_SBK,
 _O_SBV) = [int(v) for v in _IN_OFF[:-1]]

_S_CQ, _S_CKV, _S_KR, _S_KRS = 0, 2, 3, 4
_S_NQ, _S_NQS = 5, 11
_S_KC, _S_KCS, _S_VC = 17, 19, 21
_S_KS, _S_KSS = 23, 25
_S_KW, _S_KWS = 27, 29
_S_SBQ, _S_SBK = 31, 35
_N_SLOTS = 39
_T_VS, _T_VW, _T_GATE = 0, NSA_KV_HEADS * HEAD_DIM, 2 * NSA_KV_HEADS * HEAD_DIM
GATE_ROWS = 24
_T_SBV = _T_GATE + 32
_T_ROWS = _T_SBV + SB_HEADS * HEAD_DIM


def _dot(a, b):
    return jnp.dot(a.astype(_MXU), b.astype(_MXU), preferred_element_type=_F32)


def _dot_nt(a, b):
    return lax.dot_general(a.astype(_MXU), b.astype(_MXU), (((1,), (1,)), ((), ())),
                           preferred_element_type=_F32)


def _dot_split(a, b):
    hi = a.astype(_MXU)
    lo = (a - hi.astype(_F32)).astype(_MXU)
    return (jnp.dot(hi, b, preferred_element_type=_F32) + jnp.dot(lo, b, preferred_element_type=_F32))


def _dot_split_rhs(a, b):
    hi = b.astype(_MXU)
    lo = (b - hi.astype(_F32)).astype(_MXU)
    return (jnp.dot(a, hi, preferred_element_type=_F32) + jnp.dot(a, lo, preferred_element_type=_F32))


def _rms(x, g):
    return x * lax.rsqrt(jnp.mean(x * x, axis=-1, keepdims=True) + EPS) * g


def _params(*sem):
    return pltpu.CompilerParams(dimension_semantics=sem, vmem_limit_bytes=VMEM_LIMIT)


def _ffn_kernel(x_ref, g_ref, wg_ref, wu_ref, wd_ref, fg_ref, o_ref, h_ref, acc_ref, *, final_norm):
    j = pl.program_id(1)

    @pl.when(j == 0)
    def _():
        h_ref[...] = _rms(x_ref[...], g_ref[...]).astype(h_ref.dtype)
        acc_ref[...] = jnp.zeros_like(acc_ref)

    h = h_ref[...]
    gate = jnp.dot(h, wg_ref[...], preferred_element_type=_F32)
    up = jnp.dot(h, wu_ref[...], preferred_element_type=_F32)
    act = gate * jax.nn.sigmoid(gate) * up
    acc_ref[...] += _dot(act, wd_ref[...])

    @pl.when(j == pl.num_programs(1) - 1)
    def _():
        y = x_ref[...] + 0.5 * acc_ref[...]
        if final_norm:
            y = _rms(y, fg_ref[...])
        o_ref[...] = y


def _ffn(x2d, g, wg, wu, wd, fg, final_norm):
    rows = x2d.shape[0]
    tm = min(1024, rows)
    tf = 256
    grid = (rows // tm, D_FF // tf)
    return pl.pallas_call(
        functools.partial(_ffn_kernel, final_norm=final_norm),
        grid=grid,
        in_specs=[
            pl.BlockSpec((tm, D_MODEL), lambda i, j: (i, 0)),
            pl.BlockSpec((1, D_MODEL), lambda i, j: (0, 0)),
            pl.BlockSpec((D_MODEL, tf), lambda i, j: (0, j)),
            pl.BlockSpec((D_MODEL, tf), lambda i, j: (0, j)),
            pl.BlockSpec((tf, D_MODEL), lambda i, j: (j, 0)),
            pl.BlockSpec((1, D_MODEL), lambda i, j: (0, 0)),
        ],
        out_specs=pl.BlockSpec((tm, D_MODEL), lambda i, j: (i, 0)),
        out_shape=jax.ShapeDtypeStruct((rows, D_MODEL), _F32),
        scratch_shapes=[pltpu.VMEM((tm, D_MODEL), _MXU), pltpu.VMEM((tm, D_MODEL), _F32)],
        compiler_params=_params("parallel", "arbitrary"),
        name="ffn",
    )(x2d, g, wg, wu, wd, fg)


def _proj_kernel(x_ref, g_ref, w_ref, wt_ref, qn_ref, wuq_ref, wuqs_ref, kvn_ref, wuk_ref, wuv_ref, gb_ref,
                 cq_ref, sq_ref, ck_ref, sk_ref, c64_ref, s64_ref, oh_ref,
                 mq_ref, mk_ref, mv_ref, nq_ref, nkc_ref, nvc_ref, nks_ref, nvs_ref, nkw_ref, nvw_ref,
                 gate_ref, sbq_ref, sbk_ref, sbv_ref):
    hn = _rms(x_ref[0], g_ref[...]).astype(_MXU)

    def proj(s0, s1):
        return jnp.dot(hn, w_ref[:, s0 * LANES:s1 * LANES], preferred_element_type=_F32)

    def slot(p, s):
        return p[:, s * LANES:(s + 1) * LANES]

    def head(p, s):
        return p[:, s * LANES:s * LANES + HEAD_DIM]

    p = proj(_S_CQ, _S_KRS + 1)
    cq = _rms(p[:, :MLA_Q_LORA], qn_ref[...])
    ckv = _rms(slot(p, _S_CKV), kvn_ref[...])
    q = _dot(cq, wuq_ref[...])
    q_partner = _dot(cq, wuqs_ref[...])
    kpe = slot(p, _S_KR) * ck_ref[...] + slot(p, _S_KRS) * sk_ref[...]
    kn = _dot(ckv, wuk_ref[...])
    for h in range(MLA_HEADS):
        mq_ref[0, h] = (slot(q, h) * cq_ref[...] + slot(q_partner, h) * sq_ref[...]).astype(mq_ref.dtype)
        mk_ref[0, h] = (slot(kn, h) + kpe).astype(mk_ref.dtype)
        mv_ref[0, h, 0] = _ones_row_pad(_dot_nt(wuv_ref[h], ckv)).astype(mv_ref.dtype)

    c64 = c64_ref[...]
    s64 = s64_ref[...]
    scale = HEAD_DIM ** -0.5

    p = proj(_S_NQ, _S_KC)
    for h in range(NSA_HEADS):
        nq_ref[0, h] = ((head(p, h) * c64 + head(p, NSA_HEADS + h) * s64) * (scale * LOG2_E)).astype(nq_ref.dtype)

    p = proj(_S_KC, _S_SBQ)
    base = _S_KC
    ns = oh_ref.shape[-1]
    for k_ref, sk, sks, lane0 in ((nkc_ref, _S_KC, _S_KCS, 0), (nks_ref, _S_KS, _S_KSS, ns), (nkw_ref, _S_KW, _S_KWS, 0)):
        for g in range(NSA_KV_HEADS):
            k = (head(p, sk - base + g) * c64 + head(p, sks - base + g) * s64).astype(k_ref.dtype)
            k_ref[0, g, :, lane0:lane0 + HEAD_DIM] = k
    for g in range(NSA_KV_HEADS):
        nks_ref[0, g, :, 0:ns] = oh_ref[...]
    for g in range(NSA_KV_HEADS):
        nvc_ref[0, g] = head(p, _S_VC - base + g).astype(nvc_ref.dtype)

    pt = _dot_nt(wt_ref[...], hn)
    for g in range(NSA_KV_HEADS):
        lo = g * HEAD_DIM
        nvs_ref[0, g, 0] = _ones_row_pad(pt[_T_VS + lo:_T_VS + lo + HEAD_DIM]).astype(nvs_ref.dtype)
        nvw_ref[0, g, 0] = _ones_row_pad(pt[_T_VW + lo:_T_VW + lo + HEAD_DIM]).astype(nvw_ref.dtype)
    gate_ref[0] = jax.nn.sigmoid(pt[_T_GATE:_T_GATE + GATE_ROWS] + gb_ref[...])

    p = proj(_S_SBQ, _N_SLOTS)
    for h in range(SB_HEADS):
        sbq_ref[0, h] = (head(p, h) * scale).astype(sbq_ref.dtype)
        sbk_ref[0, h] = head(p, SB_HEADS + h).astype(sbk_ref.dtype)
        sbv_ref[0, h, 0] = pt[_T_SBV + h * HEAD_DIM:_T_SBV + (h + 1) * HEAD_DIM].astype(sbv_ref.dtype)


def _proj(x, g, w_ext, w_t, qn, wuq, wuqs, kvn, wuk, wuv, gb, tabs):
    b, s, _ = x.shape
    ts = min(TOKEN_CHUNK, s)
    cq, sq, ck, sk, c64, s64, onehot = tabs

    def full(a):
        return pl.BlockSpec(a.shape, lambda bi, i: (0,) * a.ndim)

    def tab(a):
        return pl.BlockSpec((ts, a.shape[1]), lambda bi, i: (i, 0))

    def heads(n, d):
        return (pl.BlockSpec((1, n, ts, d), lambda bi, i: (bi, 0, i, 0)),
                jax.ShapeDtypeStruct((b, n, s, d), _MXU))

    def values_t(n):
        return (pl.BlockSpec((1, n, 1, HEAD_DIM + ONES_PAD, ts), lambda bi, i: (bi, 0, i, 0, 0)),
                jax.ShapeDtypeStruct((b, n, s // ts, HEAD_DIM + ONES_PAD, ts), _MXU))

    kv = lambda: heads(NSA_KV_HEADS, HEAD_DIM)
    outs = [heads(MLA_HEADS, LANES), heads(MLA_HEADS, LANES), values_t(MLA_HEADS), heads(NSA_HEADS, HEAD_DIM),
            kv(), kv(), heads(NSA_KV_HEADS, onehot.shape[1] + HEAD_DIM), values_t(NSA_KV_HEADS),
            kv(), values_t(NSA_KV_HEADS),
            (pl.BlockSpec((1, GATE_ROWS, ts), lambda bi, i: (bi, 0, i)), jax.ShapeDtypeStruct((b, GATE_ROWS, s), _F32)),
            heads(SB_HEADS, HEAD_DIM), heads(SB_HEADS, HEAD_DIM),
            (pl.BlockSpec((1, SB_HEADS, 1, HEAD_DIM, ts), lambda bi, i: (bi, 0, i, 0, 0)),
             jax.ShapeDtypeStruct((b, SB_HEADS, s // ts, HEAD_DIM, ts), _MXU))]
    return pl.pallas_call(
        _proj_kernel,
        grid=(b, s // ts),
        in_specs=[pl.BlockSpec((1, ts, D_MODEL), lambda bi, i: (bi, i, 0)), full(g), full(w_ext), full(w_t), full(qn),
                  full(wuq), full(wuqs), full(kvn), full(wuk), full(wuv), full(gb),
                  tab(cq), tab(sq), tab(ck), tab(sk), tab(c64), tab(s64), tab(onehot)],
        out_specs=[o[0] for o in outs],
        out_shape=[o[1] for o in outs],
        compiler_params=_params("parallel", "parallel"),
        name="proj",
    )(x, g, w_ext, w_t, qn, wuq, wuqs, kvn, wuk, wuv, gb, cq, sq, ck, sk, c64, s64, onehot)


def _softmax_step(carry, s, v):
    m, l, acc = carry
    m_new = jnp.maximum(m, jnp.max(s, axis=-1, keepdims=True))
    alpha = jnp.exp(m - m_new)
    p = jnp.exp(s - m_new)
    l = alpha * l + jnp.sum(p, axis=-1, keepdims=True)
    acc = alpha * acc + _dot(p, v)
    return m_new, l, acc


def _softmax_init(rows, d):
    return (jnp.full((rows, 1), NEG_INF, _F32), jnp.zeros((rows, 1), _F32), jnp.zeros((rows, d), _F32))


def _ones_row_pad(vt):
    first = lax.broadcasted_iota(jnp.int32, (ONES_PAD, vt.shape[1]), 0) == 0
    return jnp.concatenate([vt, jnp.where(first, 1.0, 0.0).astype(vt.dtype)], axis=0)


def _softmax_step_t(carry, st, vt_chunks):
    m, acc = carry
    m_new = jnp.maximum(m, jnp.max(st, axis=0, keepdims=True))
    alpha = jnp.exp2(m - m_new)
    pt = jnp.exp2(st - m_new).astype(_MXU)
    n = st.shape[0] // len(vt_chunks)
    pv = sum(jnp.dot(vt, pt[c * n:(c + 1) * n], preferred_element_type=_F32) for c, vt in enumerate(vt_chunks))
    return m_new, alpha * acc + pv


def _softmax_init_t(d, cols):
    return (jnp.full((1, cols), NEG_INF, _F32), jnp.zeros((d + ONES_PAD, cols), _F32))


def _softmax_finish_t(carry, d):
    _, acc = carry
    return acc[:d] * (1.0 / acc[d:d + 1])


def _two_chain_sweep(n_full, qk, soft, init):
    qk(1, 0)

    def body(j, carry):
        c0, c1 = carry
        qk(0, j)
        c1 = soft(1, j, c1, False)
        qk(1, j + 1)
        c0 = soft(0, j, c0, False)
        return c0, c1

    c0, c1 = lax.fori_loop(0, n_full, body, init)
    qk(0, n_full)
    c1 = soft(1, n_full, c1, True)
    c0 = soft(0, n_full, c0, True)
    return c0, c1


def _mla_kernel(q_ref, k_ref, vt_ref, o_ref, s0_ref, s1_ref, *, t, nsub):
    qi = pl.program_id(2)
    s_refs = (s0_ref, s1_ref)

    def qk(hh, j):
        off = pl.multiple_of(j * t, t)
        s_refs[hh][...] = _dot_nt(k_ref[0, hh, pl.ds(off, t), :], q_ref[0, hh])

    def soft(hh, j, carry, diag):
        st = s_refs[hh][...]
        if diag:
            key = lax.broadcasted_iota(jnp.int32, (t, t), 0)
            qry = lax.broadcasted_iota(jnp.int32, (t, t), 1)
            st = jnp.where(key <= qry, st, NEG_INF)
        return _softmax_step_t(carry, st, [vt_ref[0, hh, j * nsub + c] for c in range(nsub)])

    carry = _two_chain_sweep(qi, qk, soft, tuple(_softmax_init_t(MLA_V, t) for _ in range(2)))
    ot = jnp.concatenate([_softmax_finish_t(c, MLA_V) for c in carry], axis=0)
    o_ref[0] = ot.T.astype(o_ref.dtype)


def _mla_attention(q, k, vt):
    b, h, s, _ = q.shape
    tv = vt.shape[-1]
    dv = vt.shape[-2]
    t = min(512, s)
    assert h % 2 == 0 and 2 * MLA_V == LANES and t % tv == 0
    return pl.pallas_call(
        functools.partial(_mla_kernel, t=t, nsub=t // tv),
        grid=(b, h // 2, s // t),
        in_specs=[pl.BlockSpec((1, 2, t, LANES), lambda bi, hi, i: (bi, hi, i, 0)),
                  pl.BlockSpec((1, 2, s, LANES), lambda bi, hi, i: (bi, hi, 0, 0)),
                  pl.BlockSpec((1, 2, s // tv, dv, tv), lambda bi, hi, i: (bi, hi, 0, 0, 0))],
        out_specs=pl.BlockSpec((1, t, LANES), lambda bi, hi, i: (bi, i, hi)),
        out_shape=jax.ShapeDtypeStruct((b, s, h * MLA_V), _MXU),
        scratch_shapes=[pltpu.VMEM((t, t), _F32), pltpu.VMEM((t, t), _F32)],
        compiler_params=_params("parallel", "parallel", "arbitrary"),
        name="mla_attn",
    )(q, k, vt)


def _sb_kernel(q_ref, k_ref, vt_ref, u_ref, o_ref, z_ref, lb_ref, hi_ref, lo_ref, a_ref, *, t):
    qi = pl.program_id(2)
    u = u_ref[...]

    def step(j, carry, diag):
        off = pl.multiple_of(j * t, t)
        if diag:
            key = lax.broadcasted_iota(jnp.int32, (t, t), 0)
            qry = lax.broadcasted_iota(jnp.int32, (t, t), 1)
            strict = key < qry
        for hh in range(2):
            z_ref[hh] = _dot_nt(k_ref[0, hh, pl.ds(off, t), :], q_ref[0, hh])
        first_rem = []
        for hh in range(2):
            z = z_ref[hh]
            log_beta = jnp.minimum(z, 0.0) - jnp.log(1.0 + jnp.exp(-jnp.abs(z)))
            log_rem = log_beta - z
            if diag:
                log_rem = jnp.where(strict, log_rem, 0.0)
            hi = log_rem.astype(_MXU)
            lb_ref[hh] = log_beta
            hi_ref[hh] = hi
            lo_ref[hh] = (log_rem - hi.astype(_F32)).astype(_MXU)
            first_rem.append(log_rem[0:1, :])
        out = []
        for hh in range(2):
            rem, acc = carry[hh]
            suffix = (jnp.dot(u, hi_ref[hh], preferred_element_type=_F32)
                      + jnp.dot(u, lo_ref[hh], preferred_element_type=_F32))
            a = jnp.exp(lb_ref[hh] + suffix + rem)
            if diag:
                a = jnp.where(strict, a, 0.0)
            a_ref[hh] = a.astype(_MXU)
            out.append((rem + suffix[0:1, :] + first_rem[hh], acc))
        alive = jnp.max(jnp.maximum(out[0][0], out[1][0])) > F32_EXP_ZERO
        out = tuple((rem, acc + jnp.dot(vt_ref[0, hh, j], a_ref[hh], preferred_element_type=_F32))
                    for hh, (rem, acc) in enumerate(out))
        return alive, out

    init = tuple((jnp.zeros((1, t), _F32), jnp.zeros((HEAD_DIM, t), _F32)) for _ in range(2))
    alive, carry = step(qi, init, True)

    def earlier(c):
        return (c[0] - 1,) + step(c[0], c[2], False)

    _, _, carry = lax.while_loop(lambda c: jnp.logical_and(c[0] >= 0, c[1]), earlier, (qi - 1, alive, carry))
    o_ref[0] = jnp.concatenate([acc for _, acc in carry], axis=0).T.astype(o_ref.dtype)


def _sb_attention(q, k, vt):
    b, h, s, d = q.shape
    t = vt.shape[-1]
    assert h % 2 == 0 and 2 * d == LANES and s % t == 0
    idx = np.arange(t)
    u = jnp.asarray(idx[None, :] > idx[:, None], _MXU)
    return pl.pallas_call(
        functools.partial(_sb_kernel, t=t),
        grid=(b, h // 2, s // t),
        in_specs=[pl.BlockSpec((1, 2, t, d), lambda bi, hi, i: (bi, hi, i, 0)),
                  pl.BlockSpec((1, 2, s, d), lambda bi, hi, i: (bi, hi, 0, 0)),
                  pl.BlockSpec((1, 2, s // t, d, t), lambda bi, hi, i: (bi, hi, 0, 0, 0)),
                  pl.BlockSpec((t, t), lambda bi, hi, i: (0, 0))],
        out_specs=pl.BlockSpec((1, t, LANES), lambda bi, hi, i: (bi, i, hi)),
        out_shape=jax.ShapeDtypeStruct((b, s, h * d), _MXU),
        scratch_shapes=[pltpu.VMEM((2, t, t), _F32), pltpu.VMEM((2, t, t), _F32), pltpu.VMEM((2, t, t), _MXU),
                        pltpu.VMEM((2, t, t), _MXU), pltpu.VMEM((2, t, t), _MXU)],
        compiler_params=_params("parallel", "parallel", "arbitrary"),
        name="sb_attn",
    )(q, k, vt, u)


def _compress_kernel(xk_ref, xv_ref, w1k_ref, w2k_ref, pk_ref, w1v_ref, w2v_ref, pv_ref, ok_ref, ov_ref):
    def hidden(x_ref, w1_ref, p_ref):
        x = x_ref[0, 0]
        n = x.shape[0]
        first = jnp.dot(x, w1_ref[0], preferred_element_type=_F32)
        second = jnp.dot(x, w1_ref[1], preferred_element_type=_F32)
        pos = _dot(p_ref[0], w1_ref[0]) + _dot(p_ref[1], w1_ref[1])
        hid = first + pltpu.roll(second, n - 1, 0) + pos[0:1]
        return 0.5 * hid * (1.0 + jnp.tanh(math.sqrt(2.0 / math.pi) * (hid + 0.044715 * hid * hid * hid)))

    ok_ref[0, 0] = _dot(hidden(xk_ref, w1k_ref, pk_ref), w2k_ref[...]).astype(ok_ref.dtype)
    ov_ref[0, 0] = _dot_nt(w2v_ref[...], hidden(xv_ref, w1v_ref, pv_ref)).astype(ov_ref.dtype)


def _compress(xk, xv, w1k, w2k, pk, w1v, w2v, pv):
    b, g, s, d = xk.shape
    n = s // CMP_STRIDE
    xk = xk.reshape(b, g, n, CMP_STRIDE * d)
    xv = xv.reshape(b, g, n, CMP_STRIDE * d)

    def full(a):
        return pl.BlockSpec(a.shape, lambda bi, gi: (0,) * a.ndim)

    xspec = pl.BlockSpec((1, 1, n, CMP_STRIDE * d), lambda bi, gi: (bi, gi, 0, 0))
    return pl.pallas_call(
        _compress_kernel,
        grid=(b, g),
        in_specs=[xspec, xspec, full(w1k), full(w2k), full(pk), full(w1v), full(w2v), full(pv)],
        out_specs=[pl.BlockSpec((1, 1, n, d), lambda bi, gi: (bi, gi, 0, 0)),
                   pl.BlockSpec((1, 1, d, n), lambda bi, gi: (bi, gi, 0, 0))],
        out_shape=[jax.ShapeDtypeStruct((b, g, n, d), _MXU), jax.ShapeDtypeStruct((b, g, d, n), _MXU)],
        compiler_params=_params("parallel", "parallel"),
        name="nsa_compress",
    )(xk, xv, w1k, w2k, pk, w1v, w2v, pv)


def _group_queries(q_ref, g, tq):
    return q_ref[0, g * NSA_GROUP:(g + 1) * NSA_GROUP].reshape(NSA_GROUP * tq, q_ref.shape[-1])


def _gated_heads(ot, gt_ref, g, branch, tq):
    out = []
    for r in range(NSA_GROUP):
        row = NSA_BRANCHES * (g * NSA_GROUP + r) + branch
        out.append(ot[:, r * tq:(r + 1) * tq] * gt_ref[0, row:row + 1, :])
    return out


def _cmp_kernel(q_ref, kc_ref, vct_ref, ov_ref, gt_ref, o_ref, qa_ref, s0_ref, s1_ref, *, tq, n_top):
    q0 = pl.program_id(1) * tq
    ncp = kc_ref.shape[2]
    ns = ov_ref.shape[0]
    lanes = NSA_GROUP * tq
    s_refs = (s0_ref, s1_ref)
    for g in range(NSA_KV_HEADS):
        s_refs[g][...] = _dot_nt(kc_ref[0, g], _group_queries(q_ref, g, tq))
    qpos = q0 + (lax.broadcasted_iota(jnp.int32, (1, lanes), 1) & (tq - 1))
    cmp_end = lax.broadcasted_iota(jnp.int32, (ncp, 1), 0) * CMP_STRIDE + (CMP_LEN - 1)
    visible = cmp_end <= qpos
    cur = jnp.right_shift(q0 + lax.broadcasted_iota(jnp.int32, (1, tq), 1), int(math.log2(SEL_LEN)))
    blk = lax.broadcasted_iota(jnp.int32, (ns, 1), 0)
    forced = (blk == 0) | (blk == cur) | (blk == cur - 1)
    future = blk > cur
    blk_f = blk.astype(_F32)
    heads = []
    for g in range(NSA_KV_HEADS):
        st = jnp.where(visible, s_refs[g][...], NEG_INF)
        e = jnp.exp2(st - jnp.max(st, axis=0, keepdims=True))
        inv = jnp.where(qpos >= CMP_LEN - 1, 1.0 / jnp.sum(e, axis=0, keepdims=True), 0.0)
        pt = e * inv
        heads += _gated_heads(_dot(vct_ref[0, g], pt), gt_ref, g, 0, tq)
        p_sum = sum(pt[:, r * tq:(r + 1) * tq] for r in range(NSA_GROUP))
        score = _dot_split_rhs(ov_ref[...], p_sum)
        score = jnp.where(forced, FORCE_SCORE, jnp.where(future, -1.0, score))
        sel = jnp.zeros((ns, tq), _F32)
        for _ in range(n_top):
            top = jnp.max(score, axis=0, keepdims=True)
            first = jnp.min(jnp.where(score == top, blk_f, float(ns)), axis=0, keepdims=True)
            pick = blk_f == first
            sel = jnp.where(pick, 1.0, sel)
            score = jnp.where(pick, -3e38, score)
        sel_m1 = (sel.T - 1.0).astype(qa_ref.dtype)
        for h in range(g * NSA_GROUP, (g + 1) * NSA_GROUP):
            qa_ref[0, h, :, 0:ns] = sel_m1
            qa_ref[0, h, :, ns:ns + HEAD_DIM] = q_ref[0, h]
    o_ref[0] = jnp.concatenate(heads, axis=0).T


def _cmp_select(q, kc, vct, gates_t):
    b, h, s, d = q.shape
    g = kc.shape[1]
    ncp = kc.shape[2]
    ns = s // SEL_LEN
    n_top = min(SEL_TOPK, ns)
    tq = min(256, s)
    assert tq & (tq - 1) == 0 and g == 2
    c0 = np.arange(ncp)[:, None] * CMP_STRIDE
    n0 = np.arange(ns)[None, :] * SEL_LEN
    overlap = jnp.asarray(((c0 < n0 + SEL_LEN) & (c0 + CMP_LEN > n0)).T, _MXU)
    return pl.pallas_call(
        functools.partial(_cmp_kernel, tq=tq, n_top=n_top),
        grid=(b, s // tq),
        in_specs=[pl.BlockSpec((1, h, tq, d), lambda bi, i: (bi, 0, i, 0)),
                  pl.BlockSpec((1, g, ncp, d), lambda bi, i: (bi, 0, 0, 0)),
                  pl.BlockSpec((1, g, d, ncp), lambda bi, i: (bi, 0, 0, 0)),
                  pl.BlockSpec((ns, ncp), lambda bi, i: (0, 0)),
                  pl.BlockSpec((1, GATE_ROWS, tq), lambda bi, i: (bi, 0, i))],
        out_specs=[pl.BlockSpec((1, tq, h * d), lambda bi, i: (bi, i, 0)),
                   pl.BlockSpec((1, h, tq, ns + d), lambda bi, i: (bi, 0, i, 0))],
        out_shape=[jax.ShapeDtypeStruct((b, s, h * d), _F32), jax.ShapeDtypeStruct((b, h, s, ns + d), _MXU)],
        scratch_shapes=[pltpu.VMEM((ncp, NSA_GROUP * tq), _F32) for _ in range(g)],
        compiler_params=_params("parallel", "arbitrary"),
        name="nsa_cmp_select",
    )(q, kc, vct, overlap, gates_t)


def _sel_kernel(q_ref, k_ref, vt_ref, gt_ref, o_ref, s0_ref, s1_ref, *, tq, tk, nsub):
    q0 = pl.program_id(1) * tq
    last = (q0 + tq - 1) // tk
    lanes = NSA_GROUP * tq
    s_refs = (s0_ref, s1_ref)

    def qk(g, j):
        off = pl.multiple_of(j * tk, tk)
        s_refs[g][...] = _dot_nt(k_ref[0, g, pl.ds(off, tk), :], _group_queries(q_ref, g, tq))

    def soft(g, j, carry, causal):
        st = s_refs[g][...]
        if causal:
            key = j * tk + lax.broadcasted_iota(jnp.int32, (tk, lanes), 0)
            qry = q0 + (lax.broadcasted_iota(jnp.int32, (tk, lanes), 1) & (tq - 1))
            st = jnp.where(key <= qry, st, NEG_INF)
        return _softmax_step_t(carry, st, [vt_ref[0, g, j * nsub + c] for c in range(nsub)])

    init = tuple(_softmax_init_t(HEAD_DIM, lanes) for _ in range(NSA_KV_HEADS))
    carry = _two_chain_sweep(last, qk, soft, init)
    heads = []
    for g in range(NSA_KV_HEADS):
        heads += _gated_heads(_softmax_finish_t(carry[g], HEAD_DIM), gt_ref, g, 1, tq)
    o_ref[0] = jnp.concatenate(heads, axis=0).T


def _sel_attention(q, k, vt, gates_t):
    b, h, s, da = q.shape
    g = k.shape[1]
    d = HEAD_DIM
    tv = vt.shape[-1]
    tq = min(256, s)
    tk = min(512, s)
    assert tq & (tq - 1) == 0 and s % tk == 0 and tk % tv == 0 and g == 2
    return pl.pallas_call(
        functools.partial(_sel_kernel, tq=tq, tk=tk, nsub=tk // tv),
        grid=(b, s // tq),
        in_specs=[pl.BlockSpec((1, h, tq, da), lambda bi, i: (bi, 0, i, 0)),
                  pl.BlockSpec((1, g, s, da), lambda bi, i: (bi, 0, 0, 0)),
                  pl.BlockSpec((1, g) + vt.shape[2:], lambda bi, i: (bi, 0, 0, 0, 0)),
                  pl.BlockSpec((1, GATE_ROWS, tq), lambda bi, i: (bi, 0, i))],
        out_specs=pl.BlockSpec((1, tq, h * d), lambda bi, i: (bi, i, 0)),
        out_shape=jax.ShapeDtypeStruct((b, s, h * d), _F32),
        scratch_shapes=[pltpu.VMEM((tk, NSA_GROUP * tq), _F32) for _ in range(g)],
        compiler_params=_params("parallel", "arbitrary"),
        name="nsa_selected",
    )(q, k, vt, gates_t)


def _win_kernel(q_ref, k_ref, vt_ref, gt_ref, o_ref, s0_ref, s1_ref, *, tq, span, tv):
    q0 = pl.program_id(1) * tq
    start = pl.multiple_of(jnp.maximum(q0 - WINDOW, 0), tq)
    first_chunk = start // tv
    lanes = NSA_GROUP * tq
    s_refs = (s0_ref, s1_ref)
    for g in range(NSA_KV_HEADS):
        s_refs[g][...] = _dot_nt(k_ref[0, g, pl.ds(start, span), :], _group_queries(q_ref, g, tq))
    key = start + lax.broadcasted_iota(jnp.int32, (span, lanes), 0)
    qry = q0 + (lax.broadcasted_iota(jnp.int32, (span, lanes), 1) & (tq - 1))
    heads = []
    for g in range(NSA_KV_HEADS):
        st = jnp.where(key <= qry, s_refs[g][...], NEG_INF)
        st = jnp.where(key > qry - WINDOW, st, NEG_INF)
        carry = _softmax_step_t(_softmax_init_t(HEAD_DIM, lanes), st,
                                [vt_ref[0, g, first_chunk + c] for c in range(span // tv)])
        heads += _gated_heads(_softmax_finish_t(carry, HEAD_DIM), gt_ref, g, 2, tq)
    o_ref[0] = jnp.concatenate(heads, axis=0).T


def _win_attention(q, k, vt, gates_t):
    b, h, s, d = q.shape
    g = k.shape[1]
    tv = vt.shape[-1]
    tq = min(256, s)
    span = WINDOW + tq
    assert tq & (tq - 1) == 0 and s >= span and tq % tv == 0 and WINDOW % tv == 0 and g == 2
    return pl.pallas_call(
        functools.partial(_win_kernel, tq=tq, span=span, tv=tv),
        grid=(b, s // tq),
        in_specs=[pl.BlockSpec((1, h, tq, d), lambda bi, i: (bi, 0, i, 0)),
                  pl.BlockSpec((1, g, s, d), lambda bi, i: (bi, 0, 0, 0)),
                  pl.BlockSpec((1, g) + vt.shape[2:], lambda bi, i: (bi, 0, 0, 0, 0)),
                  pl.BlockSpec((1, GATE_ROWS, tq), lambda bi, i: (bi, 0, i))],
        out_specs=pl.BlockSpec((1, tq, h * d), lambda bi, i: (bi, i, 0)),
        out_shape=jax.ShapeDtypeStruct((b, s, h * d), _F32),
        scratch_shapes=[pltpu.VMEM((span, NSA_GROUP * tq), _F32) for _ in range(g)],
        compiler_params=_params("parallel", "arbitrary"),
        name="nsa_window",
    )(q, k, vt, gates_t)


def _out_kernel(x_ref, mla_ref, cmp_ref, sel_ref, win_ref, sb_ref, w_ref, o_ref):
    def w_rows(first_head, n_heads):
        return w_ref[first_head * HEAD_DIM:(first_head + n_heads) * HEAD_DIM, :]

    acc = x_ref[0] + jnp.dot(mla_ref[0], w_rows(0, MLA_HEADS), preferred_element_type=_F32)
    nsa = cmp_ref[0] + sel_ref[0] + win_ref[0]
    acc = acc + _dot(nsa, w_rows(MLA_HEADS, NSA_HEADS))
    o_ref[0] = acc + jnp.dot(sb_ref[0], w_rows(MLA_HEADS + NSA_HEADS, SB_HEADS), preferred_element_type=_F32)


def _out_proj(x, o_mla, o_cmp, o_sel, o_win, o_sb, w_heads):
    b, s, _ = x.shape
    ts = min(512, s)

    def heads(a):
        return pl.BlockSpec((1, a.shape[1], ts, a.shape[3]), lambda bi, i: (bi, 0, i, 0))

    def rows(a):
        return pl.BlockSpec((1, ts, a.shape[2]), lambda bi, i: (bi, i, 0))

    xspec = pl.BlockSpec((1, ts, D_MODEL), lambda bi, i: (bi, i, 0))
    return pl.pallas_call(
        _out_kernel,
        grid=(b, s // ts),
        in_specs=[xspec, rows(o_mla), rows(o_cmp), rows(o_sel), rows(o_win), rows(o_sb),
                  pl.BlockSpec(w_heads.shape, lambda bi, i: (0, 0))],
        out_specs=xspec,
        out_shape=jax.ShapeDtypeStruct(x.shape, _F32),
        compiler_params=_params("parallel", "parallel"),
        name="out_proj",
    )(x, o_mla, o_cmp, o_sel, o_win, o_sb, w_heads)


def _gather_cols(w, idx):
    idx = np.asarray(idx)
    cols = jnp.take(w, jnp.asarray(np.maximum(idx, 0)), axis=1)
    return jnp.where(jnp.asarray(idx >= 0)[None, :], cols, 0.0).astype(_MXU)


def _swap_halves(rot):
    return (np.arange(rot) + rot // 2) % rot


def _w_in_index():
    idx = np.full((_N_SLOTS * LANES,), -1, np.int64)

    def put(slot, lane, src):
        src = np.asarray(src)
        idx[slot * LANES + lane:slot * LANES + lane + len(src)] = src

    put(_S_CQ, 0, _O_CQ + np.arange(MLA_Q_LORA))
    put(_S_CKV, 0, _O_CKV + np.arange(MLA_KV_LORA))
    put(_S_KR, MLA_NOPE, _O_KR + np.arange(MLA_ROPE))
    put(_S_KRS, MLA_NOPE, _O_KR + _swap_halves(MLA_ROPE))
    for h in range(NSA_HEADS):
        put(_S_NQ + h, 0, _O_NQ + h * HEAD_DIM + np.arange(HEAD_DIM))
        put(_S_NQS + h, 0, _O_NQ + h * HEAD_DIM + _swap_halves(PARTIAL_ROT))
    for sk, sks, ok in ((_S_KC, _S_KCS, _O_NKC), (_S_KS, _S_KSS, _O_NKS), (_S_KW, _S_KWS, _O_NKW)):
        for g in range(NSA_KV_HEADS):
            put(sk + g, 0, ok + g * HEAD_DIM + np.arange(HEAD_DIM))
            put(sks + g, 0, ok + g * HEAD_DIM + _swap_halves(PARTIAL_ROT))
    for g in range(NSA_KV_HEADS):
        put(_S_VC + g, 0, _O_NVC + g * HEAD_DIM + np.arange(HEAD_DIM))
    for h in range(SB_HEADS):
        put(_S_SBQ + h, 0, _O_SBQ + h * HEAD_DIM + np.arange(HEAD_DIM))
        put(_S_SBK + h, 0, _O_SBK + h * HEAD_DIM + np.arange(HEAD_DIM))
    return idx


def _mla_up_index():
    qd = MLA_NOPE + MLA_ROPE
    kd = MLA_NOPE + MLA_V
    uq = np.full((MLA_HEADS * LANES,), -1, np.int64)
    uqs = uq.copy()
    uk = uq.copy()
    for h in range(MLA_HEADS):
        uq[h * LANES:h * LANES + qd] = h * qd + np.arange(qd)
        uqs[h * LANES + MLA_NOPE:h * LANES + qd] = h * qd + MLA_NOPE + _swap_halves(MLA_ROPE)
        uk[h * LANES:h * LANES + MLA_NOPE] = h * kd + np.arange(MLA_NOPE)
    return uq, uqs, uk


def _transposed_weights(w_in_l, gate_bias):
    width = NSA_KV_HEADS * HEAD_DIM
    gate_rows = jnp.pad(w_in_l[:, _O_GATE:_O_GATE + N_GATES], ((0, 0), (0, _T_SBV - _T_GATE - N_GATES)))
    rows = jnp.concatenate([w_in_l[:, _O_NVS:_O_NVS + width], w_in_l[:, _O_NVW:_O_NVW + width], gate_rows,
                            w_in_l[:, _O_SBV:_O_SBV + SB_HEADS * HEAD_DIM]], axis=1).T.astype(_MXU)
    bias = jnp.pad(gate_bias, (0, GATE_ROWS - N_GATES)).reshape(GATE_ROWS, 1)
    return rows, bias


def _rope_tables(s):
    pos = jnp.arange(s, dtype=_F32)

    def cs(rot):
        half = rot // 2
        inv_freq = ROPE_THETA ** (-jnp.arange(half, dtype=_F32) / half)
        ang = pos[:, None] * inv_freq[None, :]
        c, sn = jnp.cos(ang), jnp.sin(ang)
        return jnp.concatenate([c, c], axis=1), jnp.concatenate([-sn, sn], axis=1)

    c, sn = cs(MLA_ROPE)
    ones = jnp.ones((s, MLA_NOPE), _F32)
    zeros = jnp.zeros((s, MLA_NOPE), _F32)
    pad = jnp.zeros((s, LANES - MLA_NOPE - MLA_ROPE), _F32)
    ck = jnp.concatenate([ones, c, pad], axis=1)
    sk = jnp.concatenate([zeros, sn, pad], axis=1)
    q_scale = (MLA_NOPE + MLA_ROPE) ** -0.5 * LOG2_E
    c, sn = cs(PARTIAL_ROT)
    c64 = jnp.concatenate([c, jnp.ones((s, HEAD_DIM - PARTIAL_ROT), _F32)], axis=1)
    s64 = jnp.concatenate([sn, jnp.zeros((s, HEAD_DIM - PARTIAL_ROT), _F32)], axis=1)
    ns = s // SEL_LEN
    onehot = (np.arange(s)[:, None] // SEL_LEN == np.arange(ns)[None, :]) * -NEG_INF
    return ck * q_scale, sk * q_scale, ck, sk, c64, s64, jnp.asarray(onehot, _MXU)


def kernel(x, ffn1_norm, ffn1_w_gate, ffn1_w_up, ffn1_w_down, mix_norm, w_in, mla_q_norm, mla_w_uq, mla_kv_norm,
           mla_w_ukv, nsa_gate_bias, nsa_cmp_pos_k, nsa_cmp_w1_k, nsa_cmp_w2_k, nsa_cmp_pos_v, nsa_cmp_w1_v,
           nsa_cmp_w2_v, w_out, ffn2_norm, ffn2_w_gate, ffn2_w_up, ffn2_w_down, final_norm):
    b, s, d = x.shape
    depth = w_in.shape[0]
    tabs = _rope_tables(s)
    in_idx = _w_in_index()
    uq_idx, uqs_idx, uk_idx = _mla_up_index()
    half = CMP_LEN * HEAD_DIM // 2
    fg = final_norm.reshape(1, d)

    def cmp_weights(w1, w2, pos, transpose_out):
        pos = jnp.broadcast_to(pos.reshape(2, 1, half), (2, 8, half)).astype(_MXU)
        w2 = w2.T if transpose_out else w2
        return w1.reshape(2, half, CMP_HIDDEN).astype(_MXU), w2.astype(_MXU), pos

    for l in range(depth):
        x2d = _ffn(x.reshape(b * s, d), ffn1_norm[l].reshape(1, d), ffn1_w_gate[l].astype(_MXU),
                   ffn1_w_up[l].astype(_MXU), ffn1_w_down[l].astype(_MXU), fg, False)
        x = x2d.reshape(b, s, d)
        w_t, gate_bias = _transposed_weights(w_in[l], nsa_gate_bias[l])
        (mq, mk, mvt, nq, nkc, nvc, nks, nvst, nkw, nvwt, gates_t, sbq, sbk, sbv) = _proj(
            x, mix_norm[l].reshape(1, d), _gather_cols(w_in[l], in_idx), w_t,
            mla_q_norm[l].reshape(1, -1), _gather_cols(mla_w_uq[l], uq_idx), _gather_cols(mla_w_uq[l], uqs_idx),
            mla_kv_norm[l].reshape(1, -1), _gather_cols(mla_w_ukv[l], uk_idx),
            mla_w_ukv[l].reshape(MLA_KV_LORA, MLA_HEADS, 2, MLA_V)[:, :, 1].transpose(1, 2, 0).astype(_MXU),
            gate_bias, tabs)
        o_mla = _mla_attention(mq, mk, mvt)
        kc, vct = _compress(nkc, nvc, *cmp_weights(nsa_cmp_w1_k[l], nsa_cmp_w2_k[l], nsa_cmp_pos_k[l], False),
                            *cmp_weights(nsa_cmp_w1_v[l], nsa_cmp_w2_v[l], nsa_cmp_pos_v[l], True))
        o_cmp, q_sel = _cmp_select(nq, kc, vct, gates_t)
        o_sel = _sel_attention(q_sel, nks, nvst, gates_t)
        o_win = _win_attention(nq, nkw, nvwt, gates_t)
        o_sb = _sb_attention(sbq, sbk, sbv)
        x = _out_proj(x, o_mla, o_cmp, o_sel, o_win, o_sb, w_out[l].astype(_MXU))
        x2d = _ffn(x.reshape(b * s, d), ffn2_norm[l].reshape(1, d), ffn2_w_gate[l].astype(_MXU),
                   ffn2_w_up[l].astype(_MXU), ffn2_w_down[l].astype(_MXU), fg, l == depth - 1)
        x = x2d.reshape(b, s, d)
    return x
```

```python
import functools
import math

import numpy as np
import jax
import jax.numpy as jnp
from jax import lax
from jax.experimental import pallas as pl
from jax.experimental.pallas import tpu as pltpu

D_MODEL = 1024
HEAD_DIM = 64
MLA_HEADS = 6
MLA_NOPE = 64
MLA_ROPE = 32
MLA_V = 64
MLA_Q_LORA = 256
MLA_KV_LORA = 128
NSA_HEADS = 6
NSA_KV_HEADS = 2
NSA_GROUP = NSA_HEADS // NSA_KV_HEADS
NSA_BRANCHES = 3
CMP_LEN = 32
CMP_STRIDE = 16
CMP_HIDDEN = 128
SEL_LEN = 64
SEL_TOPK = 16
WINDOW = 512
SB_HEADS = 4
D_FF = 2816
ROPE_THETA = 500000.0
PARTIAL_ROT = HEAD_DIM // 4
EPS = 1e-6
NEG_INF = -1e30
FORCE_SCORE = 1e4
F32_EXP_ZERO = -104.0
LOG2_E = math.log2(math.e)
N_GATES = NSA_HEADS * NSA_BRANCHES

LANES = 128
TOKEN_CHUNK = 256
ONES_PAD = 16
VMEM_LIMIT = 56 * 1024 * 1024

_MXU = jnp.bfloat16
_F32 = jnp.float32

_IN_WIDTHS = (MLA_Q_LORA, MLA_KV_LORA, MLA_ROPE, NSA_HEADS * HEAD_DIM) + (NSA_KV_HEADS * HEAD_DIM,) * 6 + (
    N_GATES, SB_HEADS * HEAD_DIM, SB_HEADS * HEAD_DIM, SB_HEADS * HEAD_DIM)
_IN_OFF = np.concatenate([[0], np.cumsum(_IN_WIDTHS)])
(_O_CQ, _O_CKV, _O_KR, _O_NQ, _O_NKC, _O_NVC, _O_NKS, _O_NVS, _O_NKW, _O_NVW, _O_GATE, _O_SBQ, _O_SBK,
 _O_SBV) = [int(v) for v in _IN_OFF[:-1]]

_S_CQ, _S_CKV, _S_KR, _S_KRS = 0, 2, 3, 4
_H_NQ, _H_NQS = 10, 16
_H_KC, _H_KCS, _H_VC = 22, 24, 26
_H_KS, _H_KSS = 28, 30
_H_KW, _H_KWS = 32, 34
_H_SBQ, _H_SBK = 36, 40
_N_HEAD_COLS = 44
_T_VS, _T_VW, _T_GATE = 0, NSA_KV_HEADS * HEAD_DIM, 2 * NSA_KV_HEADS * HEAD_DIM
GATE_ROWS = 24
_T_SBV = _T_GATE + 32
_T_ROWS = _T_SBV + SB_HEADS * HEAD_DIM


def _dot(a, b):
    return jnp.dot(a.astype(_MXU), b.astype(_MXU), preferred_element_type=_F32)


def _dot_nt(a, b):
    return lax.dot_general(a.astype(_MXU), b.astype(_MXU), (((1,), (1,)), ((), ())),
                           preferred_element_type=_F32)


def _dot_split(a, b):
    hi = a.astype(_MXU)
    lo = (a - hi.astype(_F32)).astype(_MXU)
    return (jnp.dot(hi, b, preferred_element_type=_F32) + jnp.dot(lo, b, preferred_element_type=_F32))


def _dot_split_rhs(a, b):
    hi = b.astype(_MXU)
    lo = (b - hi.astype(_F32)).astype(_MXU)
    return (jnp.dot(a, hi, preferred_element_type=_F32) + jnp.dot(a, lo, preferred_element_type=_F32))


def _rms(x, g):
    return x * lax.rsqrt(jnp.mean(x * x, axis=-1, keepdims=True) + EPS) * g


def _params(*sem):
    return pltpu.CompilerParams(dimension_semantics=sem, vmem_limit_bytes=VMEM_LIMIT)


def _ffn_kernel(x_ref, g_ref, wg_ref, wu_ref, wd_ref, fg_ref, o_ref, h_ref, acc_ref, *, final_norm):
    j = pl.program_id(1)

    @pl.when(j == 0)
    def _():
        h_ref[...] = _rms(x_ref[...], g_ref[...]).astype(h_ref.dtype)
        acc_ref[...] = jnp.zeros_like(acc_ref)

    h = h_ref[...]
    gate = jnp.dot(h, wg_ref[...], preferred_element_type=_F32)
    up = jnp.dot(h, wu_ref[...], preferred_element_type=_F32)
    act = gate * jax.nn.sigmoid(gate) * up
    acc_ref[...] += _dot(act, wd_ref[...])

    @pl.when(j == pl.num_programs(1) - 1)
    def _():
        y = x_ref[...] + 0.5 * acc_ref[...]
        if final_norm:
            y = _rms(y, fg_ref[...])
        o_ref[...] = y


def _ffn(x2d, g, wg, wu, wd, fg, final_norm):
    rows = x2d.shape[0]
    tm = min(512, rows)
    tf = D_FF // 2
    grid = (rows // tm, D_FF // tf)
    return pl.pallas_call(
        functools.partial(_ffn_kernel, final_norm=final_norm),
        grid=grid,
        in_specs=[
            pl.BlockSpec((tm, D_MODEL), lambda i, j: (i, 0)),
            pl.BlockSpec((1, D_MODEL), lambda i, j: (0, 0)),
            pl.BlockSpec((D_MODEL, tf), lambda i, j: (0, j)),
            pl.BlockSpec((D_MODEL, tf), lambda i, j: (0, j)),
            pl.BlockSpec((tf, D_MODEL), lambda i, j: (j, 0)),
            pl.BlockSpec((1, D_MODEL), lambda i, j: (0, 0)),
        ],
        out_specs=pl.BlockSpec((tm, D_MODEL), lambda i, j: (i, 0)),
        out_shape=jax.ShapeDtypeStruct((rows, D_MODEL), _F32),
        scratch_shapes=[pltpu.VMEM((tm, D_MODEL), _MXU), pltpu.VMEM((tm, D_MODEL), _F32)],
        compiler_params=_params("parallel", "arbitrary"),
        name="ffn",
    )(x2d, g, wg, wu, wd, fg)


def _proj_kernel(x_ref, g_ref, w_ref, wt_ref, qn_ref, wuq_ref, wuqs_ref, kvn_ref, wuk_ref, wuv_ref, gb_ref,
                 cq_ref, sq_ref, ck_ref, sk_ref, c64_ref, s64_ref, oh_ref,
                 mq_ref, mk_ref, mv_ref, nq_ref, nkc_ref, nvc_ref, nks_ref, nvs_ref, nkw_ref, nvw_ref,
                 gate_ref, sbq_ref, sbk_ref, sbv_ref):
    hn = _rms(x_ref[0], g_ref[...]).astype(_MXU)

    def proj(h0, h1):
        return jnp.dot(hn, w_ref[:, h0 * HEAD_DIM:h1 * HEAD_DIM], preferred_element_type=_F32)

    def slot(p, s):
        return p[:, s * LANES:(s + 1) * LANES]

    def head(p, i):
        return p[:, i * HEAD_DIM:(i + 1) * HEAD_DIM]

    p = proj(0, _H_NQ)
    cq = _rms(p[:, :MLA_Q_LORA], qn_ref[...])
    ckv = _rms(slot(p, _S_CKV), kvn_ref[...])
    q = _dot(cq, wuq_ref[...])
    q_partner = _dot(cq, wuqs_ref[...])
    kpe = slot(p, _S_KR) * ck_ref[...] + slot(p, _S_KRS) * sk_ref[...]
    kn = _dot(ckv, wuk_ref[...])
    for h in range(MLA_HEADS):
        mq_ref[0, h] = (slot(q, h) * cq_ref[...] + slot(q_partner, h) * sq_ref[...]).astype(mq_ref.dtype)
        mk_ref[0, h] = (slot(kn, h) + kpe).astype(mk_ref.dtype)
        mv_ref[0, h, 0] = _ones_row_pad(_dot_nt(wuv_ref[h], ckv)).astype(mv_ref.dtype)

    c64 = c64_ref[...]
    s64 = s64_ref[...]
    scale = HEAD_DIM ** -0.5

    p = proj(_H_NQ, _H_KC)
    for h in range(NSA_HEADS):
        nq_ref[0, h] = ((head(p, h) * c64 + head(p, NSA_HEADS + h) * s64) * (scale * LOG2_E)).astype(nq_ref.dtype)

    p = proj(_H_KC, _H_SBQ)
    base = _H_KC
    ns = oh_ref.shape[-1]
    for k_ref, hk, hks, lane0 in ((nkc_ref, _H_KC, _H_KCS, 0), (nks_ref, _H_KS, _H_KSS, ns), (nkw_ref, _H_KW, _H_KWS, 0)):
        for g in range(NSA_KV_HEADS):
            k = (head(p, hk - base + g) * c64 + head(p, hks - base + g) * s64).astype(k_ref.dtype)
            k_ref[0, g, :, lane0:lane0 + HEAD_DIM] = k
    for g in range(NSA_KV_HEADS):
        nks_ref[0, g, :, 0:ns] = oh_ref[...]
    for g in range(NSA_KV_HEADS):
        nvc_ref[0, g] = head(p, _H_VC - base + g).astype(nvc_ref.dtype)

    pt = _dot_nt(wt_ref[...], hn)
    for g in range(NSA_KV_HEADS):
        lo = g * HEAD_DIM
        nvs_ref[0, g, 0] = _ones_row_pad(pt[_T_VS + lo:_T_VS + lo + HEAD_DIM]).astype(nvs_ref.dtype)
        nvw_ref[0, g, 0] = _ones_row_pad(pt[_T_VW + lo:_T_VW + lo + HEAD_DIM]).astype(nvw_ref.dtype)
    gate_ref[0] = jax.nn.sigmoid(pt[_T_GATE:_T_GATE + GATE_ROWS] + gb_ref[...])

    p = proj(_H_SBQ, _N_HEAD_COLS)
    for h in range(SB_HEADS):
        sbq_ref[0, h] = (head(p, h) * scale).astype(sbq_ref.dtype)
        sbk_ref[0, h] = head(p, SB_HEADS + h).astype(sbk_ref.dtype)
        sbv_ref[0, h, 0] = pt[_T_SBV + h * HEAD_DIM:_T_SBV + (h + 1) * HEAD_DIM].astype(sbv_ref.dtype)


def _proj(x, g, w_ext, w_t, qn, wuq, wuqs, kvn, wuk, wuv, gb, tabs):
    b, s, _ = x.shape
    ts = min(TOKEN_CHUNK, s)
    cq, sq, ck, sk, c64, s64, onehot = tabs

    def full(a):
        return pl.BlockSpec(a.shape, lambda bi, i: (0,) * a.ndim)

    def tab(a):
        return pl.BlockSpec((ts, a.shape[1]), lambda bi, i: (i, 0))

    def heads(n, d):
        return (pl.BlockSpec((1, n, ts, d), lambda bi, i: (bi, 0, i, 0)),
                jax.ShapeDtypeStruct((b, n, s, d), _MXU))

    def values_t(n):
        return (pl.BlockSpec((1, n, 1, HEAD_DIM + ONES_PAD, ts), lambda bi, i: (bi, 0, i, 0, 0)),
                jax.ShapeDtypeStruct((b, n, s // ts, HEAD_DIM + ONES_PAD, ts), _MXU))

    kv = lambda: heads(NSA_KV_HEADS, HEAD_DIM)
    outs = [heads(MLA_HEADS, LANES), heads(MLA_HEADS, LANES), values_t(MLA_HEADS), heads(NSA_HEADS, HEAD_DIM),
            kv(), kv(), heads(NSA_KV_HEADS, onehot.shape[1] + HEAD_DIM), values_t(NSA_KV_HEADS),
            kv(), values_t(NSA_KV_HEADS),
            (pl.BlockSpec((1, GATE_ROWS, ts), lambda bi, i: (bi, 0, i)), jax.ShapeDtypeStruct((b, GATE_ROWS, s), _F32)),
            heads(SB_HEADS, HEAD_DIM), heads(SB_HEADS, HEAD_DIM),
            (pl.BlockSpec((1, SB_HEADS, 1, HEAD_DIM, ts), lambda bi, i: (bi, 0, i, 0, 0)),
             jax.ShapeDtypeStruct((b, SB_HEADS, s // ts, HEAD_DIM, ts), _MXU))]
    return pl.pallas_call(
        _proj_kernel,
        grid=(b, s // ts),
        in_specs=[pl.BlockSpec((1, ts, D_MODEL), lambda bi, i: (bi, i, 0)), full(g), full(w_ext), full(w_t), full(qn),
                  full(wuq), full(wuqs), full(kvn), full(wuk), full(wuv), full(gb),
                  tab(cq), tab(sq), tab(ck), tab(sk), tab(c64), tab(s64), tab(onehot)],
        out_specs=[o[0] for o in outs],
        out_shape=[o[1] for o in outs],
        compiler_params=_params("parallel", "parallel"),
        name="proj",
    )(x, g, w_ext, w_t, qn, wuq, wuqs, kvn, wuk, wuv, gb, cq, sq, ck, sk, c64, s64, onehot)


def _softmax_step(carry, s, v):
    m, l, acc = carry
    m_new = jnp.maximum(m, jnp.max(s, axis=-1, keepdims=True))
    alpha = jnp.exp(m - m_new)
    p = jnp.exp(s - m_new)
    l = alpha * l + jnp.sum(p, axis=-1, keepdims=True)
    acc = alpha * acc + _dot(p, v)
    return m_new, l, acc


def _softmax_init(rows, d):
    return (jnp.full((rows, 1), NEG_INF, _F32), jnp.zeros((rows, 1), _F32), jnp.zeros((rows, d), _F32))


def _ones_row_pad(vt):
    first = lax.broadcasted_iota(jnp.int32, (ONES_PAD, vt.shape[1]), 0) == 0
    return jnp.concatenate([vt, jnp.where(first, 1.0, 0.0).astype(vt.dtype)], axis=0)


def _softmax_step_t(carry, st, vt_chunks):
    m, acc = carry
    m_new = jnp.maximum(m, jnp.max(st, axis=0, keepdims=True))
    alpha = jnp.exp2(m - m_new)
    pt = jnp.exp2(st - m_new).astype(_MXU)
    n = st.shape[0] // len(vt_chunks)
    pv = sum(jnp.dot(vt, pt[c * n:(c + 1) * n], preferred_element_type=_F32) for c, vt in enumerate(vt_chunks))
    return m_new, alpha * acc + pv


def _softmax_init_t(d, cols):
    return (jnp.full((1, cols), NEG_INF, _F32), jnp.zeros((d + ONES_PAD, cols), _F32))


def _softmax_finish_t(carry, d):
    _, acc = carry
    return acc[:d] * (1.0 / acc[d:d + 1])


def _two_chain_sweep(n_full, qk, soft, init):
    qk(1, 0)

    def body(j, carry):
        c0, c1 = carry
        qk(0, j)
        c1 = soft(1, j, c1, False)
        qk(1, j + 1)
        c0 = soft(0, j, c0, False)
        return c0, c1

    c0, c1 = lax.fori_loop(0, n_full, body, init)
    qk(0, n_full)
    c1 = soft(1, n_full, c1, True)
    c0 = soft(0, n_full, c0, True)
    return c0, c1


def _mla_kernel(q_ref, k_ref, vt_ref, o_ref, s0_ref, s1_ref, *, t, nsub):
    qi = pl.program_id(2)
    s_refs = (s0_ref, s1_ref)

    def qk(hh, j):
        off = pl.multiple_of(j * t, t)
        s_refs[hh][...] = _dot_nt(k_ref[0, hh, pl.ds(off, t), :], q_ref[0, hh])

    def soft(hh, j, carry, diag):
        st = s_refs[hh][...]
        if diag:
            key = lax.broadcasted_iota(jnp.int32, (t, t), 0)
            qry = lax.broadcasted_iota(jnp.int32, (t, t), 1)
            st = jnp.where(key <= qry, st, NEG_INF)
        return _softmax_step_t(carry, st, [vt_ref[0, hh, j * nsub + c] for c in range(nsub)])

    carry = _two_chain_sweep(qi, qk, soft, tuple(_softmax_init_t(MLA_V, t) for _ in range(2)))
    ot = jnp.concatenate([_softmax_finish_t(c, MLA_V) for c in carry], axis=0)
    o_ref[0] = ot.T.astype(o_ref.dtype)


def _mla_attention(q, k, vt):
    b, h, s, _ = q.shape
    tv = vt.shape[-1]
    dv = vt.shape[-2]
    t = min(512, s)
    assert h % 2 == 0 and 2 * MLA_V == LANES and t % tv == 0
    return pl.pallas_call(
        functools.partial(_mla_kernel, t=t, nsub=t // tv),
        grid=(b, h // 2, s // t),
        in_specs=[pl.BlockSpec((1, 2, t, LANES), lambda bi, hi, i: (bi, hi, i, 0)),
                  pl.BlockSpec((1, 2, s, LANES), lambda bi, hi, i: (bi, hi, 0, 0)),
                  pl.BlockSpec((1, 2, s // tv, dv, tv), lambda bi, hi, i: (bi, hi, 0, 0, 0))],
        out_specs=pl.BlockSpec((1, t, LANES), lambda bi, hi, i: (bi, i, hi)),
        out_shape=jax.ShapeDtypeStruct((b, s, h * MLA_V), _MXU),
        scratch_shapes=[pltpu.VMEM((t, t), _F32), pltpu.VMEM((t, t), _F32)],
        compiler_params=_params("parallel", "parallel", "arbitrary"),
        name="mla_attn",
    )(q, k, vt)


def _sb_kernel(q_ref, k_ref, vt_ref, u_ref, o_ref, z_ref, lb_ref, hi_ref, lo_ref, a_ref, *, t):
    qi = pl.program_id(2)
    u = u_ref[...]

    def step(j, carry, diag):
        off = pl.multiple_of(j * t, t)
        if diag:
            key = lax.broadcasted_iota(jnp.int32, (t, t), 0)
            qry = lax.broadcasted_iota(jnp.int32, (t, t), 1)
            strict = key < qry
        for hh in range(2):
            z_ref[hh] = _dot_nt(k_ref[0, hh, pl.ds(off, t), :], q_ref[0, hh])
        first_rem = []
        for hh in range(2):
            z = z_ref[hh]
            log_beta = jnp.minimum(z, 0.0) - jnp.log(1.0 + jnp.exp(-jnp.abs(z)))
            log_rem = log_beta - z
            if diag:
                log_rem = jnp.where(strict, log_rem, 0.0)
            hi = log_rem.astype(_MXU)
            lb_ref[hh] = log_beta
            hi_ref[hh] = hi
            lo_ref[hh] = (log_rem - hi.astype(_F32)).astype(_MXU)
            first_rem.append(log_rem[0:1, :])
        out = []
        for hh in range(2):
            rem, acc = carry[hh]
            suffix = (jnp.dot(u, hi_ref[hh], preferred_element_type=_F32)
                      + jnp.dot(u, lo_ref[hh], preferred_element_type=_F32))
            a = jnp.exp(lb_ref[hh] + suffix + rem)
            if diag:
                a = jnp.where(strict, a, 0.0)
            a_ref[hh] = a.astype(_MXU)
            out.append((rem + suffix[0:1, :] + first_rem[hh], acc))
        alive = jnp.max(jnp.maximum(out[0][0], out[1][0])) > F32_EXP_ZERO
        out = tuple((rem, acc + jnp.dot(vt_ref[0, hh, j], a_ref[hh], preferred_element_type=_F32))
                    for hh, (rem, acc) in enumerate(out))
        return alive, out

    init = tuple((jnp.zeros((1, t), _F32), jnp.zeros((HEAD_DIM, t), _F32)) for _ in range(2))
    alive, carry = step(qi, init, True)

    def earlier(c):
        return (c[0] - 1,) + step(c[0], c[2], False)

    _, _, carry = lax.while_loop(lambda c: jnp.logical_and(c[0] >= 0, c[1]), earlier, (qi - 1, alive, carry))
    o_ref[0] = jnp.concatenate([acc for _, acc in carry], axis=0).T.astype(o_ref.dtype)


def _sb_attention(q, k, vt):
    b, h, s, d = q.shape
    t = vt.shape[-1]
    assert h % 2 == 0 and 2 * d == LANES and s % t == 0
    idx = np.arange(t)
    u = jnp.asarray(idx[None, :] > idx[:, None], _MXU)
    return pl.pallas_call(
        functools.partial(_sb_kernel, t=t),
        grid=(b, h // 2, s // t),
        in_specs=[pl.BlockSpec((1, 2, t, d), lambda bi, hi, i: (bi, hi, i, 0)),
                  pl.BlockSpec((1, 2, s, d), lambda bi, hi, i: (bi, hi, 0, 0)),
                  pl.BlockSpec((1, 2, s // t, d, t), lambda bi, hi, i: (bi, hi, 0, 0, 0)),
                  pl.BlockSpec((t, t), lambda bi, hi, i: (0, 0))],
        out_specs=pl.BlockSpec((1, t, LANES), lambda bi, hi, i: (bi, i, hi)),
        out_shape=jax.ShapeDtypeStruct((b, s, h * d), _MXU),
        scratch_shapes=[pltpu.VMEM((2, t, t), _F32), pltpu.VMEM((2, t, t), _F32), pltpu.VMEM((2, t, t), _MXU),
                        pltpu.VMEM((2, t, t), _MXU), pltpu.VMEM((2, t, t), _MXU)],
        compiler_params=_params("parallel", "parallel", "arbitrary"),
        name="sb_attn",
    )(q, k, vt, u)


def _compress_kernel(xk_ref, xv_ref, w1k_ref, w2k_ref, pk_ref, w1v_ref, w2v_ref, pv_ref, ok_ref, ov_ref):
    def hidden(x_ref, w1_ref, p_ref):
        x = x_ref[0, 0]
        n = x.shape[0]
        first = jnp.dot(x, w1_ref[0], preferred_element_type=_F32)
        second = jnp.dot(x, w1_ref[1], preferred_element_type=_F32)
        pos = _dot(p_ref[0], w1_ref[0]) + _dot(p_ref[1], w1_ref[1])
        hid = first + pltpu.roll(second, n - 1, 0) + pos[0:1]
        return 0.5 * hid * (1.0 + jnp.tanh(math.sqrt(2.0 / math.pi) * (hid + 0.044715 * hid * hid * hid)))

    ok_ref[0, 0] = _dot(hidden(xk_ref, w1k_ref, pk_ref), w2k_ref[...]).astype(ok_ref.dtype)
    ov_ref[0, 0] = _dot_nt(w2v_ref[...], hidden(xv_ref, w1v_ref, pv_ref)).astype(ov_ref.dtype)


def _compress(xk, xv, w1k, w2k, pk, w1v, w2v, pv):
    b, g, s, d = xk.shape
    n = s // CMP_STRIDE
    xk = xk.reshape(b, g, n, CMP_STRIDE * d)
    xv = xv.reshape(b, g, n, CMP_STRIDE * d)

    def full(a):
        return pl.BlockSpec(a.shape, lambda bi, gi: (0,) * a.ndim)

    xspec = pl.BlockSpec((1, 1, n, CMP_STRIDE * d), lambda bi, gi: (bi, gi, 0, 0))
    return pl.pallas_call(
        _compress_kernel,
        grid=(b, g),
        in_specs=[xspec, xspec, full(w1k), full(w2k), full(pk), full(w1v), full(w2v), full(pv)],
        out_specs=[pl.BlockSpec((1, 1, n, d), lambda bi, gi: (bi, gi, 0, 0)),
                   pl.BlockSpec((1, 1, d, n), lambda bi, gi: (bi, gi, 0, 0))],
        out_shape=[jax.ShapeDtypeStruct((b, g, n, d), _MXU), jax.ShapeDtypeStruct((b, g, d, n), _MXU)],
        compiler_params=_params("parallel", "parallel"),
        name="nsa_compress",
    )(xk, xv, w1k, w2k, pk, w1v, w2v, pv)


def _group_queries(q_ref, g, tq):
    return q_ref[0, g * NSA_GROUP:(g + 1) * NSA_GROUP].reshape(NSA_GROUP * tq, q_ref.shape[-1])


def _gated_heads(ot, gt_ref, g, branch, tq):
    out = []
    for r in range(NSA_GROUP):
        row = NSA_BRANCHES * (g * NSA_GROUP + r) + branch
        out.append(ot[:, r * tq:(r + 1) * tq] * gt_ref[0, row:row + 1, :])
    return out


def _cmp_kernel(q_ref, kc_ref, vct_ref, ov_ref, gt_ref, o_ref, qa_ref, s0_ref, s1_ref, *, tq, n_top):
    q0 = pl.program_id(1) * tq
    ncp = kc_ref.shape[2]
    ns = ov_ref.shape[0]
    lanes = NSA_GROUP * tq
    s_refs = (s0_ref, s1_ref)
    for g in range(NSA_KV_HEADS):
        s_refs[g][...] = _dot_nt(kc_ref[0, g], _group_queries(q_ref, g, tq))
    qpos = q0 + (lax.broadcasted_iota(jnp.int32, (1, lanes), 1) & (tq - 1))
    cmp_end = lax.broadcasted_iota(jnp.int32, (ncp, 1), 0) * CMP_STRIDE + (CMP_LEN - 1)
    visible = cmp_end <= qpos
    cur = jnp.right_shift(q0 + lax.broadcasted_iota(jnp.int32, (1, tq), 1), int(math.log2(SEL_LEN)))
    blk = lax.broadcasted_iota(jnp.int32, (ns, 1), 0)
    forced = (blk == 0) | (blk == cur) | (blk == cur - 1)
    future = blk > cur
    blk_f = blk.astype(_F32)
    heads = []
    for g in range(NSA_KV_HEADS):
        st = jnp.where(visible, s_refs[g][...], NEG_INF)
        e = jnp.exp2(st - jnp.max(st, axis=0, keepdims=True))
        inv = jnp.where(qpos >= CMP_LEN - 1, 1.0 / jnp.sum(e, axis=0, keepdims=True), 0.0)
        pt = e * inv
        heads += _gated_heads(_dot(vct_ref[0, g], pt), gt_ref, g, 0, tq)
        p_sum = sum(pt[:, r * tq:(r + 1) * tq] for r in range(NSA_GROUP))
        score = _dot_split_rhs(ov_ref[...], p_sum)
        score = jnp.where(forced, FORCE_SCORE, jnp.where(future, -1.0, score))
        sel = jnp.zeros((ns, tq), _F32)
        for _ in range(n_top):
            top = jnp.max(score, axis=0, keepdims=True)
            first = jnp.min(jnp.where(score == top, blk_f, float(ns)), axis=0, keepdims=True)
            pick = blk_f == first
            sel = jnp.where(pick, 1.0, sel)
            score = jnp.where(pick, -3e38, score)
        sel_m1 = (sel.T - 1.0).astype(qa_ref.dtype)
        for h in range(g * NSA_GROUP, (g + 1) * NSA_GROUP):
            qa_ref[0, h, :, 0:ns] = sel_m1
            qa_ref[0, h, :, ns:ns + HEAD_DIM] = q_ref[0, h]
    o_ref[0] = jnp.concatenate(heads, axis=0).T


def _cmp_select(q, kc, vct, gates_t):
    b, h, s, d = q.shape
    g = kc.shape[1]
    ncp = kc.shape[2]
    ns = s // SEL_LEN
    n_top = min(SEL_TOPK, ns)
    tq = min(256, s)
    assert tq & (tq - 1) == 0 and g == 2
    c0 = np.arange(ncp)[:, None] * CMP_STRIDE
    n0 = np.arange(ns)[None, :] * SEL_LEN
    overlap = jnp.asarray(((c0 < n0 + SEL_LEN) & (c0 + CMP_LEN > n0)).T, _MXU)
    return pl.pallas_call(
        functools.partial(_cmp_kernel, tq=tq, n_top=n_top),
        grid=(b, s // tq),
        in_specs=[pl.BlockSpec((1, h, tq, d), lambda bi, i: (bi, 0, i, 0)),
                  pl.BlockSpec((1, g, ncp, d), lambda bi, i: (bi, 0, 0, 0)),
                  pl.BlockSpec((1, g, d, ncp), lambda bi, i: (bi, 0, 0, 0)),
                  pl.BlockSpec((ns, ncp), lambda bi, i: (0, 0)),
                  pl.BlockSpec((1, GATE_ROWS, tq), lambda bi, i: (bi, 0, i))],
        out_specs=[pl.BlockSpec((1, tq, h * d), lambda bi, i: (bi, i, 0)),
                   pl.BlockSpec((1, h, tq, ns + d), lambda bi, i: (bi, 0, i, 0))],
        out_shape=[jax.ShapeDtypeStruct((b, s, h * d), _F32), jax.ShapeDtypeStruct((b, h, s, ns + d), _MXU)],
        scratch_shapes=[pltpu.VMEM((ncp, NSA_GROUP * tq), _F32) for _ in range(g)],
        compiler_params=_params("parallel", "arbitrary"),
        name="nsa_cmp_select",
    )(q, kc, vct, overlap, gates_t)


def _sel_kernel(q_ref, k_ref, vt_ref, gt_ref, o_ref, s0_ref, s1_ref, *, tq, tk, nsub):
    q0 = pl.program_id(1) * tq
    last = (q0 + tq - 1) // tk
    lanes = NSA_GROUP * tq
    s_refs = (s0_ref, s1_ref)

    def qk(g, j):
        off = pl.multiple_of(j * tk, tk)
        s_refs[g][...] = _dot_nt(k_ref[0, g, pl.ds(off, tk), :], _group_queries(q_ref, g, tq))

    def soft(g, j, carry, causal):
        st = s_refs[g][...]
        if causal:
            key = j * tk + lax.broadcasted_iota(jnp.int32, (tk, lanes), 0)
            qry = q0 + (lax.broadcasted_iota(jnp.int32, (tk, lanes), 1) & (tq - 1))
            st = jnp.where(key <= qry, st, NEG_INF)
        return _softmax_step_t(carry, st, [vt_ref[0, g, j * nsub + c] for c in range(nsub)])

    init = tuple(_softmax_init_t(HEAD_DIM, lanes) for _ in range(NSA_KV_HEADS))
    carry = _two_chain_sweep(last, qk, soft, init)
    heads = []
    for g in range(NSA_KV_HEADS):
        heads += _gated_heads(_softmax_finish_t(carry[g], HEAD_DIM), gt_ref, g, 1, tq)
    o_ref[0] = jnp.concatenate(heads, axis=0).T


def _sel_attention(q, k, vt, gates_t):
    b, h, s, da = q.shape
    g = k.shape[1]
    d = HEAD_DIM
    tv = vt.shape[-1]
    tq = min(256, s)
    tk = min(512, s)
    assert tq & (tq - 1) == 0 and s % tk == 0 and tk % tv == 0 and g == 2
    return pl.pallas_call(
        functools.partial(_sel_kernel, tq=tq, tk=tk, nsub=tk // tv),
        grid=(b, s // tq),
        in_specs=[pl.BlockSpec((1, h, tq, da), lambda bi, i: (bi, 0, i, 0)),
                  pl.BlockSpec((1, g, s, da), lambda bi, i: (bi, 0, 0, 0)),
                  pl.BlockSpec((1, g) + vt.shape[2:], lambda bi, i: (bi, 0, 0, 0, 0)),
                  pl.BlockSpec((1, GATE_ROWS, tq), lambda bi, i: (bi, 0, i))],
        out_specs=pl.BlockSpec((1, tq, h * d), lambda bi, i: (bi, i, 0)),
        out_shape=jax.ShapeDtypeStruct((b, s, h * d), _F32),
        scratch_shapes=[pltpu.VMEM((tk, NSA_GROUP * tq), _F32) for _ in range(g)],
        compiler_params=_params("parallel", "arbitrary"),
        name="nsa_selected",
    )(q, k, vt, gates_t)


def _win_kernel(q_ref, k_ref, vt_ref, gt_ref, o_ref, s0_ref, s1_ref, *, tq, span, tv):
    q0 = pl.program_id(1) * tq
    start = pl.multiple_of(jnp.maximum(q0 - WINDOW, 0), tq)
    first_chunk = start // tv
    lanes = NSA_GROUP * tq
    s_refs = (s0_ref, s1_ref)
    for g in range(NSA_KV_HEADS):
        s_refs[g][...] = _dot_nt(k_ref[0, g, pl.ds(start, span), :], _group_queries(q_ref, g, tq))
    key = start + lax.broadcasted_iota(jnp.int32, (span, lanes), 0)
    qry = q0 + (lax.broadcasted_iota(jnp.int32, (span, lanes), 1) & (tq - 1))
    heads = []
    for g in range(NSA_KV_HEADS):
        st = jnp.where(key <= qry, s_refs[g][...], NEG_INF)
        st = jnp.where(key > qry - WINDOW, st, NEG_INF)
        carry = _softmax_step_t(_softmax_init_t(HEAD_DIM, lanes), st,
                                [vt_ref[0, g, first_chunk + c] for c in range(span // tv)])
        heads += _gated_heads(_softmax_finish_t(carry, HEAD_DIM), gt_ref, g, 2, tq)
    o_ref[0] = jnp.concatenate(heads, axis=0).T


def _win_attention(q, k, vt, gates_t):
    b, h, s, d = q.shape
    g = k.shape[1]
    tv = vt.shape[-1]
    tq = min(256, s)
    span = WINDOW + tq
    assert tq & (tq - 1) == 0 and s >= span and tq % tv == 0 and WINDOW % tv == 0 and g == 2
    return pl.pallas_call(
        functools.partial(_win_kernel, tq=tq, span=span, tv=tv),
        grid=(b, s // tq),
        in_specs=[pl.BlockSpec((1, h, tq, d), lambda bi, i: (bi, 0, i, 0)),
                  pl.BlockSpec((1, g, s, d), lambda bi, i: (bi, 0, 0, 0)),
                  pl.BlockSpec((1, g) + vt.shape[2:], lambda bi, i: (bi, 0, 0, 0, 0)),
                  pl.BlockSpec((1, GATE_ROWS, tq), lambda bi, i: (bi, 0, i))],
        out_specs=pl.BlockSpec((1, tq, h * d), lambda bi, i: (bi, i, 0)),
        out_shape=jax.ShapeDtypeStruct((b, s, h * d), _F32),
        scratch_shapes=[pltpu.VMEM((span, NSA_GROUP * tq), _F32) for _ in range(g)],
        compiler_params=_params("parallel", "arbitrary"),
        name="nsa_window",
    )(q, k, vt, gates_t)


def _out_kernel(x_ref, mla_ref, cmp_ref, sel_ref, win_ref, sb_ref, w_ref, o_ref):
    def w_rows(first_head, n_heads):
        return w_ref[first_head * HEAD_DIM:(first_head + n_heads) * HEAD_DIM, :]

    acc = x_ref[0] + jnp.dot(mla_ref[0], w_rows(0, MLA_HEADS), preferred_element_type=_F32)
    nsa = cmp_ref[0] + sel_ref[0] + win_ref[0]
    acc = acc + _dot(nsa, w_rows(MLA_HEADS, NSA_HEADS))
    o_ref[0] = acc + jnp.dot(sb_ref[0], w_rows(MLA_HEADS + NSA_HEADS, SB_HEADS), preferred_element_type=_F32)


def _out_proj(x, o_mla, o_cmp, o_sel, o_win, o_sb, w_heads):
    b, s, _ = x.shape
    ts = min(512, s)

    def heads(a):
        return pl.BlockSpec((1, a.shape[1], ts, a.shape[3]), lambda bi, i: (bi, 0, i, 0))

    def rows(a):
        return pl.BlockSpec((1, ts, a.shape[2]), lambda bi, i: (bi, i, 0))

    xspec = pl.BlockSpec((1, ts, D_MODEL), lambda bi, i: (bi, i, 0))
    return pl.pallas_call(
        _out_kernel,
        grid=(b, s // ts),
        in_specs=[xspec, rows(o_mla), rows(o_cmp), rows(o_sel), rows(o_win), rows(o_sb),
                  pl.BlockSpec(w_heads.shape, lambda bi, i: (0, 0))],
        out_specs=xspec,
        out_shape=jax.ShapeDtypeStruct(x.shape, _F32),
        compiler_params=_params("parallel", "parallel"),
        name="out_proj",
    )(x, o_mla, o_cmp, o_sel, o_win, o_sb, w_heads)


def _gather_cols(w, idx):
    idx = np.asarray(idx)
    cols = jnp.take(w, jnp.asarray(np.maximum(idx, 0)), axis=1)
    return jnp.where(jnp.asarray(idx >= 0)[None, :], cols, 0.0).astype(_MXU)


def _swap_halves(rot):
    return (np.arange(rot) + rot // 2) % rot


def _w_in_index():
    idx = np.full((_N_HEAD_COLS * HEAD_DIM,), -1, np.int64)

    def put(col, src):
        src = np.asarray(src)
        idx[col:col + len(src)] = src

    def put_head(pos, src):
        put(pos * HEAD_DIM, src)

    put(_S_CQ * LANES, _O_CQ + np.arange(MLA_Q_LORA))
    put(_S_CKV * LANES, _O_CKV + np.arange(MLA_KV_LORA))
    put(_S_KR * LANES + MLA_NOPE, _O_KR + np.arange(MLA_ROPE))
    put(_S_KRS * LANES + MLA_NOPE, _O_KR + _swap_halves(MLA_ROPE))
    for h in range(NSA_HEADS):
        put_head(_H_NQ + h, _O_NQ + h * HEAD_DIM + np.arange(HEAD_DIM))
        put_head(_H_NQS + h, _O_NQ + h * HEAD_DIM + _swap_halves(PARTIAL_ROT))
    for hk, hks, ok in ((_H_KC, _H_KCS, _O_NKC), (_H_KS, _H_KSS, _O_NKS), (_H_KW, _H_KWS, _O_NKW)):
        for g in range(NSA_KV_HEADS):
            put_head(hk + g, ok + g * HEAD_DIM + np.arange(HEAD_DIM))
            put_head(hks + g, ok + g * HEAD_DIM + _swap_halves(PARTIAL_ROT))
    for g in range(NSA_KV_HEADS):
        put_head(_H_VC + g, _O_NVC + g * HEAD_DIM + np.arange(HEAD_DIM))
    for h in range(SB_HEADS):
        put_head(_H_SBQ + h, _O_SBQ + h * HEAD_DIM + np.arange(HEAD_DIM))
        put_head(_H_SBK + h, _O_SBK + h * HEAD_DIM + np.arange(HEAD_DIM))
    return idx


def _mla_up_index():
    qd = MLA_NOPE + MLA_ROPE
    kd = MLA_NOPE + MLA_V
    uq = np.full((MLA_HEADS * LANES,), -1, np.int64)
    uqs = uq.copy()
    uk = uq.copy()
    for h in range(MLA_HEADS):
        uq[h * LANES:h * LANES + qd] = h * qd + np.arange(qd)
        uqs[h * LANES + MLA_NOPE:h * LANES + qd] = h * qd + MLA_NOPE + _swap_halves(MLA_ROPE)
        uk[h * LANES:h * LANES + MLA_NOPE] = h * kd + np.arange(MLA_NOPE)
    return uq, uqs, uk


def _transposed_weights(w_in_l, gate_bias):
    width = NSA_KV_HEADS * HEAD_DIM
    gate_rows = jnp.pad(w_in_l[:, _O_GATE:_O_GATE + N_GATES], ((0, 0), (0, _T_SBV - _T_GATE - N_GATES)))
    rows = jnp.concatenate([w_in_l[:, _O_NVS:_O_NVS + width], w_in_l[:, _O_NVW:_O_NVW + width], gate_rows,
                            w_in_l[:, _O_SBV:_O_SBV + SB_HEADS * HEAD_DIM]], axis=1).T.astype(_MXU)
    bias = jnp.pad(gate_bias, (0, GATE_ROWS - N_GATES)).reshape(GATE_ROWS, 1)
    return rows, bias


def _rope_tables(s):
    pos = jnp.arange(s, dtype=_F32)

    def cs(rot):
        half = rot // 2
        inv_freq = ROPE_THETA ** (-jnp.arange(half, dtype=_F32) / half)
        ang = pos[:, None] * inv_freq[None, :]
        c, sn = jnp.cos(ang), jnp.sin(ang)
        return jnp.concatenate([c, c], axis=1), jnp.concatenate([-sn, sn], axis=1)

    c, sn = cs(MLA_ROPE)
    ones = jnp.ones((s, MLA_NOPE), _F32)
    zeros = jnp.zeros((s, MLA_NOPE), _F32)
    pad = jnp.zeros((s, LANES - MLA_NOPE - MLA_ROPE), _F32)
    ck = jnp.concatenate([ones, c, pad], axis=1)
    sk = jnp.concatenate([zeros, sn, pad], axis=1)
    q_scale = (MLA_NOPE + MLA_ROPE) ** -0.5 * LOG2_E
    c, sn = cs(PARTIAL_ROT)
    c64 = jnp.concatenate([c, jnp.ones((s, HEAD_DIM - PARTIAL_ROT), _F32)], axis=1)
    s64 = jnp.concatenate([sn, jnp.zeros((s, HEAD_DIM - PARTIAL_ROT), _F32)], axis=1)
    ns = s // SEL_LEN
    onehot = (np.arange(s)[:, None] // SEL_LEN == np.arange(ns)[None, :]) * -NEG_INF
    return ck * q_scale, sk * q_scale, ck, sk, c64, s64, jnp.asarray(onehot, _MXU)


def kernel(x, ffn1_norm, ffn1_w_gate, ffn1_w_up, ffn1_w_down, mix_norm, w_in, mla_q_norm, mla_w_uq, mla_kv_norm,
           mla_w_ukv, nsa_gate_bias, nsa_cmp_pos_k, nsa_cmp_w1_k, nsa_cmp_w2_k, nsa_cmp_pos_v, nsa_cmp_w1_v,
           nsa_cmp_w2_v, w_out, ffn2_norm, ffn2_w_gate, ffn2_w_up, ffn2_w_down, final_norm):
    b, s, d = x.shape
    depth = w_in.shape[0]
    tabs = _rope_tables(s)
    in_idx = _w_in_index()
    uq_idx, uqs_idx, uk_idx = _mla_up_index()
    half = CMP_LEN * HEAD_DIM // 2
    fg = final_norm.reshape(1, d)

    def cmp_weights(w1, w2, pos, transpose_out):
        pos = jnp.broadcast_to(pos.reshape(2, 1, half), (2, 8, half)).astype(_MXU)
        w2 = w2.T if transpose_out else w2
        return w1.reshape(2, half, CMP_HIDDEN).astype(_MXU), w2.astype(_MXU), pos

    for l in range(depth):
        x2d = _ffn(x.reshape(b * s, d), ffn1_norm[l].reshape(1, d), ffn1_w_gate[l].astype(_MXU),
                   ffn1_w_up[l].astype(_MXU), ffn1_w_down[l].astype(_MXU), fg, False)
        x = x2d.reshape(b, s, d)
        w_t, gate_bias = _transposed_weights(w_in[l], nsa_gate_bias[l])
        (mq, mk, mvt, nq, nkc, nvc, nks, nvst, nkw, nvwt, gates_t, sbq, sbk, sbv) = _proj(
            x, mix_norm[l].reshape(1, d), _gather_cols(w_in[l], in_idx), w_t,
            mla_q_norm[l].reshape(1, -1), _gather_cols(mla_w_uq[l], uq_idx), _gather_cols(mla_w_uq[l], uqs_idx),
            mla_kv_norm[l].reshape(1, -1), _gather_cols(mla_w_ukv[l], uk_idx),
            mla_w_ukv[l].reshape(MLA_KV_LORA, MLA_HEADS, 2, MLA_V)[:, :, 1].transpose(1, 2, 0).astype(_MXU),
            gate_bias, tabs)
        o_mla = _mla_attention(mq, mk, mvt)
        kc, vct = _compress(nkc, nvc, *cmp_weights(nsa_cmp_w1_k[l], nsa_cmp_w2_k[l], nsa_cmp_pos_k[l], False),
                            *cmp_weights(nsa_cmp_w1_v[l], nsa_cmp_w2_v[l], nsa_cmp_pos_v[l], True))
        o_cmp, q_sel = _cmp_select(nq, kc, vct, gates_t)
        o_sel = _sel_attention(q_sel, nks, nvst, gates_t)
        o_win = _win_attention(nq, nkw, nvwt, gates_t)
        o_sb = _sb_attention(sbq, sbk, sbv)
        x = _out_proj(x, o_mla, o_cmp, o_sel, o_win, o_sb, w_out[l].astype(_MXU))
        x2d = _ffn(x.reshape(b * s, d), ffn2_norm[l].reshape(1, d), ffn2_w_gate[l].astype(_MXU),
                   ffn2_w_up[l].astype(_MXU), ffn2_w_down[l].astype(_MXU), fg, l == depth - 1)
        x = x2d.reshape(b, s, d)
    return x
```

```python
import functools
import math

import numpy as np
import jax
import jax.numpy as jnp
from jax import lax
from jax.experimental import pallas as pl
from jax.experimental.pallas import tpu as pltpu

D_MODEL = 1024
HEAD_DIM = 64
MLA_HEADS = 6
MLA_NOPE = 64
MLA_ROPE = 32
MLA_V = 64
MLA_Q_LORA = 256
MLA_KV_LORA = 128
NSA_HEADS = 6
NSA_KV_HEADS = 2
NSA_GROUP = NSA_HEADS // NSA_KV_HEADS
NSA_BRANCHES = 3
CMP_LEN = 32
CMP_STRIDE = 16
CMP_HIDDEN = 128
SEL_LEN = 64
SEL_TOPK = 16
WINDOW = 512
SB_HEADS = 4
D_FF = 2816
ROPE_THETA = 500000.0
PARTIAL_ROT = HEAD_DIM // 4
EPS = 1e-6
NEG_INF = -1e30
FORCE_SCORE = 1e4
PICKED = -3e38
F32_EXP_ZERO = -104.0
LOG2_E = math.log2(math.e)
N_GATES = NSA_HEADS * NSA_BRANCHES

LANES = 128
SWEEP_UNROLL = 4
TOKEN_CHUNK = 256
ONES_PAD = 16
VMEM_LIMIT = 56 * 1024 * 1024

_MXU = jnp.bfloat16
_F32 = jnp.float32

_IN_WIDTHS = (MLA_Q_LORA, MLA_KV_LORA, MLA_ROPE, NSA_HEADS * HEAD_DIM) + (NSA_KV_HEADS * HEAD_DIM,) * 6 + (
    N_GATES, SB_HEADS * HEAD_DIM, SB_HEADS * HEAD_DIM, SB_HEADS * HEAD_DIM)
_IN_OFF = np.concatenate([[0], np.cumsum(_IN_WIDTHS)])
(_O_CQ, _O_CKV, _O_KR, _O_NQ, _O_NKC, _O_NVC, _O_NKS, _O_NVS, _O_NKW, _O_NVW, _O_GATE, _O_SBQ, _O_SBK,
 _O_SBV) = [int(v) for v in _IN_OFF[:-1]]

_S_CQ, _S_CKV, _S_KR, _S_KRS = 0, 2, 3, 4
_H_NQ, _H_NQS = 10, 16
_H_KC, _H_KCS, _H_VC = 22, 24, 26
_H_KS, _H_KSS = 28, 30
_H_KW, _H_KWS = 32, 34
_H_SBQ, _H_SBK = 36, 40
_N_HEAD_COLS = 44
_T_VS, _T_VW, _T_GATE = 0, NSA_KV_HEADS * HEAD_DIM, 2 * NSA_KV_HEADS * HEAD_DIM
GATE_ROWS = 24
_T_SBV = _T_GATE + 32
_T_ROWS = _T_SBV + SB_HEADS * HEAD_DIM


def _dot(a, b):
    return jnp.dot(a.astype(_MXU), b.astype(_MXU), preferred_element_type=_F32)


def _dot_nt(a, b):
    return lax.dot_general(a.astype(_MXU), b.astype(_MXU), (((1,), (1,)), ((), ())),
                           preferred_element_type=_F32)


def _dot_split(a, b):
    hi = a.astype(_MXU)
    lo = (a - hi.astype(_F32)).astype(_MXU)
    return (jnp.dot(hi, b, preferred_element_type=_F32) + jnp.dot(lo, b, preferred_element_type=_F32))


def _dot_split_rhs(a, b):
    hi = b.astype(_MXU)
    lo = (b - hi.astype(_F32)).astype(_MXU)
    return (jnp.dot(a, hi, preferred_element_type=_F32) + jnp.dot(a, lo, preferred_element_type=_F32))


def _rms(x, g):
    return x * lax.rsqrt(jnp.mean(x * x, axis=-1, keepdims=True) + EPS) * g


def _params(*sem):
    return pltpu.CompilerParams(dimension_semantics=sem, vmem_limit_bytes=VMEM_LIMIT)


def _ffn_kernel(x_ref, g_ref, wg_ref, wu_ref, wd_ref, fg_ref, o_ref, h_ref, acc_ref, *, final_norm):
    j = pl.program_id(1)

    @pl.when(j == 0)
    def _():
        h_ref[...] = _rms(x_ref[...], g_ref[...]).astype(h_ref.dtype)
        acc_ref[...] = jnp.zeros_like(acc_ref)

    h = h_ref[...]
    gate = jnp.dot(h, wg_ref[...], preferred_element_type=_F32)
    up = jnp.dot(h, wu_ref[...], preferred_element_type=_F32)
    act = gate * jax.nn.sigmoid(gate) * up
    acc_ref[...] += _dot(act, wd_ref[...])

    @pl.when(j == pl.num_programs(1) - 1)
    def _():
        y = x_ref[...] + 0.5 * acc_ref[...]
        if final_norm:
            y = _rms(y, fg_ref[...])
        o_ref[...] = y


def _ffn(x2d, g, wg, wu, wd, fg, final_norm):
    rows = x2d.shape[0]
    tm = min(512, rows)
    tf = D_FF // 2
    grid = (rows // tm, D_FF // tf)
    return pl.pallas_call(
        functools.partial(_ffn_kernel, final_norm=final_norm),
        grid=grid,
        in_specs=[
            pl.BlockSpec((tm, D_MODEL), lambda i, j: (i, 0)),
            pl.BlockSpec((1, D_MODEL), lambda i, j: (0, 0)),
            pl.BlockSpec((D_MODEL, tf), lambda i, j: (0, j)),
            pl.BlockSpec((D_MODEL, tf), lambda i, j: (0, j)),
            pl.BlockSpec((tf, D_MODEL), lambda i, j: (j, 0)),
            pl.BlockSpec((1, D_MODEL), lambda i, j: (0, 0)),
        ],
        out_specs=pl.BlockSpec((tm, D_MODEL), lambda i, j: (i, 0)),
        out_shape=jax.ShapeDtypeStruct((rows, D_MODEL), _F32),
        scratch_shapes=[pltpu.VMEM((tm, D_MODEL), _MXU), pltpu.VMEM((tm, D_MODEL), _F32)],
        compiler_params=_params("parallel", "arbitrary"),
        name="ffn",
    )(x2d, g, wg, wu, wd, fg)


def _proj_kernel(x_ref, g_ref, w_ref, wt_ref, qn_ref, wuq_ref, wuqs_ref, kvn_ref, wuk_ref, wuv_ref, gb_ref,
                 cq_ref, sq_ref, ck_ref, sk_ref, c64_ref, s64_ref, oh_ref,
                 mq_ref, mk_ref, mv_ref, nq_ref, nkc_ref, nvc_ref, nks_ref, nvs_ref, nkw_ref, nvw_ref,
                 gate_ref, sbq_ref, sbk_ref, sbv_ref):
    hn = _rms(x_ref[0], g_ref[...]).astype(_MXU)

    def proj(h0, h1):
        return jnp.dot(hn, w_ref[:, h0 * HEAD_DIM:h1 * HEAD_DIM], preferred_element_type=_F32)

    def slot(p, s):
        return p[:, s * LANES:(s + 1) * LANES]

    def head(p, i):
        return p[:, i * HEAD_DIM:(i + 1) * HEAD_DIM]

    p = proj(0, _H_NQ)
    cq = _rms(p[:, :MLA_Q_LORA], qn_ref[...])
    ckv = _rms(slot(p, _S_CKV), kvn_ref[...])
    q = _dot(cq, wuq_ref[...])
    q_partner = _dot(cq, wuqs_ref[...])
    kpe = slot(p, _S_KR) * ck_ref[...] + slot(p, _S_KRS) * sk_ref[...]
    kn = _dot(ckv, wuk_ref[...])
    for h in range(MLA_HEADS):
        mq_ref[0, h] = (slot(q, h) * cq_ref[...] + slot(q_partner, h) * sq_ref[...]).astype(mq_ref.dtype)
        mk_ref[0, h] = (slot(kn, h) + kpe).astype(mk_ref.dtype)
        mv_ref[0, h, 0] = _ones_row_pad(_dot_nt(wuv_ref[h], ckv)).astype(mv_ref.dtype)

    c64 = c64_ref[...]
    s64 = s64_ref[...]
    scale = HEAD_DIM ** -0.5

    p = proj(_H_NQ, _H_KC)
    for h in range(NSA_HEADS):
        nq_ref[0, h] = ((head(p, h) * c64 + head(p, NSA_HEADS + h) * s64) * (scale * LOG2_E)).astype(nq_ref.dtype)

    p = proj(_H_KC, _H_SBQ)
    base = _H_KC
    ns = oh_ref.shape[-1]
    for k_ref, hk, hks, lane0 in ((nkc_ref, _H_KC, _H_KCS, 0), (nks_ref, _H_KS, _H_KSS, ns), (nkw_ref, _H_KW, _H_KWS, 0)):
        for g in range(NSA_KV_HEADS):
            k = (head(p, hk - base + g) * c64 + head(p, hks - base + g) * s64).astype(k_ref.dtype)
            k_ref[0, g, :, lane0:lane0 + HEAD_DIM] = k
    for g in range(NSA_KV_HEADS):
        nks_ref[0, g, :, 0:ns] = oh_ref[...]
    for g in range(NSA_KV_HEADS):
        nvc_ref[0, g] = head(p, _H_VC - base + g).astype(nvc_ref.dtype)

    pt = _dot_nt(wt_ref[...], hn)
    for g in range(NSA_KV_HEADS):
        lo = g * HEAD_DIM
        nvs_ref[0, g, 0] = _ones_row_pad(pt[_T_VS + lo:_T_VS + lo + HEAD_DIM]).astype(nvs_ref.dtype)
        nvw_ref[0, g, 0] = _ones_row_pad(pt[_T_VW + lo:_T_VW + lo + HEAD_DIM]).astype(nvw_ref.dtype)
    gate_ref[0] = jax.nn.sigmoid(pt[_T_GATE:_T_GATE + GATE_ROWS] + gb_ref[...])

    p = proj(_H_SBQ, _N_HEAD_COLS)
    for h in range(SB_HEADS):
        sbq_ref[0, h] = (head(p, h) * scale).astype(sbq_ref.dtype)
        sbk_ref[0, h] = head(p, SB_HEADS + h).astype(sbk_ref.dtype)
        sbv_ref[0, h, 0] = pt[_T_SBV + h * HEAD_DIM:_T_SBV + (h + 1) * HEAD_DIM].astype(sbv_ref.dtype)


def _proj(x, g, w_ext, w_t, qn, wuq, wuqs, kvn, wuk, wuv, gb, tabs):
    b, s, _ = x.shape
    ts = min(TOKEN_CHUNK, s)
    cq, sq, ck, sk, c64, s64, onehot = tabs

    def full(a):
        return pl.BlockSpec(a.shape, lambda bi, i: (0,) * a.ndim)

    def tab(a):
        return pl.BlockSpec((ts, a.shape[1]), lambda bi, i: (i, 0))

    def heads(n, d):
        return (pl.BlockSpec((1, n, ts, d), lambda bi, i: (bi, 0, i, 0)),
                jax.ShapeDtypeStruct((b, n, s, d), _MXU))

    def values_t(n):
        return (pl.BlockSpec((1, n, 1, HEAD_DIM + ONES_PAD, ts), lambda bi, i: (bi, 0, i, 0, 0)),
                jax.ShapeDtypeStruct((b, n, s // ts, HEAD_DIM + ONES_PAD, ts), _MXU))

    kv = lambda: heads(NSA_KV_HEADS, HEAD_DIM)
    outs = [heads(MLA_HEADS, LANES), heads(MLA_HEADS, LANES), values_t(MLA_HEADS), heads(NSA_HEADS, HEAD_DIM),
            kv(), kv(), heads(NSA_KV_HEADS, onehot.shape[1] + HEAD_DIM), values_t(NSA_KV_HEADS),
            kv(), values_t(NSA_KV_HEADS),
            (pl.BlockSpec((1, GATE_ROWS, ts), lambda bi, i: (bi, 0, i)), jax.ShapeDtypeStruct((b, GATE_ROWS, s), _F32)),
            heads(SB_HEADS, HEAD_DIM), heads(SB_HEADS, HEAD_DIM),
            (pl.BlockSpec((1, SB_HEADS, 1, HEAD_DIM, ts), lambda bi, i: (bi, 0, i, 0, 0)),
             jax.ShapeDtypeStruct((b, SB_HEADS, s // ts, HEAD_DIM, ts), _MXU))]
    return pl.pallas_call(
        _proj_kernel,
        grid=(b, s // ts),
        in_specs=[pl.BlockSpec((1, ts, D_MODEL), lambda bi, i: (bi, i, 0)), full(g), full(w_ext), full(w_t), full(qn),
                  full(wuq), full(wuqs), full(kvn), full(wuk), full(wuv), full(gb),
                  tab(cq), tab(sq), tab(ck), tab(sk), tab(c64), tab(s64), tab(onehot)],
        out_specs=[o[0] for o in outs],
        out_shape=[o[1] for o in outs],
        compiler_params=_params("parallel", "parallel"),
        name="proj",
    )(x, g, w_ext, w_t, qn, wuq, wuqs, kvn, wuk, wuv, gb, cq, sq, ck, sk, c64, s64, onehot)


def _softmax_step(carry, s, v):
    m, l, acc = carry
    m_new = jnp.maximum(m, jnp.max(s, axis=-1, keepdims=True))
    alpha = jnp.exp(m - m_new)
    p = jnp.exp(s - m_new)
    l = alpha * l + jnp.sum(p, axis=-1, keepdims=True)
    acc = alpha * acc + _dot(p, v)
    return m_new, l, acc


def _softmax_init(rows, d):
    return (jnp.full((rows, 1), NEG_INF, _F32), jnp.zeros((rows, 1), _F32), jnp.zeros((rows, d), _F32))


def _ones_row_pad(vt):
    first = lax.broadcasted_iota(jnp.int32, (ONES_PAD, vt.shape[1]), 0) == 0
    return jnp.concatenate([vt, jnp.where(first, 1.0, 0.0).astype(vt.dtype)], axis=0)


def _softmax_step_t(carry, st, vt_chunks):
    m, acc = carry
    m_new = jnp.maximum(m, jnp.max(st, axis=0, keepdims=True))
    alpha = jnp.exp2(m - m_new)
    pt = jnp.exp2(st - m_new).astype(_MXU)
    n = st.shape[0] // len(vt_chunks)
    pv = sum(jnp.dot(vt, pt[c * n:(c + 1) * n], preferred_element_type=_F32) for c, vt in enumerate(vt_chunks))
    return m_new, alpha * acc + pv


def _softmax_init_t(d, cols):
    return (jnp.full((1, cols), NEG_INF, _F32), jnp.zeros((d + ONES_PAD, cols), _F32))


def _softmax_finish_t(carry, d):
    _, acc = carry
    return acc[:d] * (1.0 / acc[d:d + 1])


def _two_chain_sweep(n_full, qk, soft, init):
    qk(1, 0)

    def body(j, carry):
        c0, c1 = carry
        qk(0, j)
        c1 = soft(1, j, c1, False)
        qk(1, j + 1)
        c0 = soft(0, j, c0, False)
        return c0, c1

    def unrolled(i, carry):
        for u in range(SWEEP_UNROLL):
            carry = body(SWEEP_UNROLL * i + u, carry)
        return carry

    trips = n_full // SWEEP_UNROLL
    carry = lax.fori_loop(0, trips, unrolled, init)
    c0, c1 = lax.fori_loop(SWEEP_UNROLL * trips, n_full, body, carry)
    qk(0, n_full)
    c1 = soft(1, n_full, c1, True)
    c0 = soft(0, n_full, c0, True)
    return c0, c1


def _mla_kernel(q_ref, k_ref, vt_ref, o_ref, s0_ref, s1_ref, *, t, nsub):
    qi = pl.program_id(2)
    s_refs = (s0_ref, s1_ref)

    def qk(hh, j):
        off = pl.multiple_of(j * t, t)
        s_refs[hh][...] = _dot_nt(k_ref[0, hh, pl.ds(off, t), :], q_ref[0, hh])

    def soft(hh, j, carry, diag):
        st = s_refs[hh][...]
        if diag:
            key = lax.broadcasted_iota(jnp.int32, (t, t), 0)
            qry = lax.broadcasted_iota(jnp.int32, (t, t), 1)
            st = jnp.where(key <= qry, st, NEG_INF)
        return _softmax_step_t(carry, st, [vt_ref[0, hh, j * nsub + c] for c in range(nsub)])

    carry = _two_chain_sweep(qi, qk, soft, tuple(_softmax_init_t(MLA_V, t) for _ in range(2)))
    ot = jnp.concatenate([_softmax_finish_t(c, MLA_V) for c in carry], axis=0)
    o_ref[0] = ot.T.astype(o_ref.dtype)


def _mla_attention(q, k, vt):
    b, h, s, _ = q.shape
    tv = vt.shape[-1]
    dv = vt.shape[-2]
    t = min(512, s)
    assert h % 2 == 0 and 2 * MLA_V == LANES and t % tv == 0
    return pl.pallas_call(
        functools.partial(_mla_kernel, t=t, nsub=t // tv),
        grid=(b, h // 2, s // t),
        in_specs=[pl.BlockSpec((1, 2, t, LANES), lambda bi, hi, i: (bi, hi, i, 0)),
                  pl.BlockSpec((1, 2, s, LANES), lambda bi, hi, i: (bi, hi, 0, 0)),
                  pl.BlockSpec((1, 2, s // tv, dv, tv), lambda bi, hi, i: (bi, hi, 0, 0, 0))],
        out_specs=pl.BlockSpec((1, t, LANES), lambda bi, hi, i: (bi, i, hi)),
        out_shape=jax.ShapeDtypeStruct((b, s, h * MLA_V), _MXU),
        scratch_shapes=[pltpu.VMEM((t, t), _F32), pltpu.VMEM((t, t), _F32)],
        compiler_params=_params("parallel", "parallel", "arbitrary"),
        name="mla_attn",
    )(q, k, vt)


def _sb_kernel(q_ref, k_ref, vt_ref, u_ref, o_ref, z_ref, lb_ref, hi_ref, lo_ref, a_ref, *, t):
    qi = pl.program_id(2)
    u = u_ref[...]

    def step(j, carry, diag):
        off = pl.multiple_of(j * t, t)
        if diag:
            key = lax.broadcasted_iota(jnp.int32, (t, t), 0)
            qry = lax.broadcasted_iota(jnp.int32, (t, t), 1)
            strict = key < qry
        for hh in range(2):
            z_ref[hh] = _dot_nt(k_ref[0, hh, pl.ds(off, t), :], q_ref[0, hh])
        first_rem = []
        for hh in range(2):
            z = z_ref[hh]
            log_beta = jnp.minimum(z, 0.0) - jnp.log(1.0 + jnp.exp(-jnp.abs(z)))
            log_rem = log_beta - z
            if diag:
                log_rem = jnp.where(strict, log_rem, 0.0)
            hi = log_rem.astype(_MXU)
            lb_ref[hh] = log_beta
            hi_ref[hh] = hi
            lo_ref[hh] = (log_rem - hi.astype(_F32)).astype(_MXU)
            first_rem.append(log_rem[0:1, :])
        out = []
        for hh in range(2):
            rem, acc = carry[hh]
            suffix = (jnp.dot(u, hi_ref[hh], preferred_element_type=_F32)
                      + jnp.dot(u, lo_ref[hh], preferred_element_type=_F32))
            a = jnp.exp(lb_ref[hh] + suffix + rem)
            if diag:
                a = jnp.where(strict, a, 0.0)
            a_ref[hh] = a.astype(_MXU)
            out.append((rem + suffix[0:1, :] + first_rem[hh], acc))
        alive = jnp.max(jnp.maximum(out[0][0], out[1][0])) > F32_EXP_ZERO
        out = tuple((rem, acc + jnp.dot(vt_ref[0, hh, j], a_ref[hh], preferred_element_type=_F32))
                    for hh, (rem, acc) in enumerate(out))
        return alive, out

    init = tuple((jnp.zeros((1, t), _F32), jnp.zeros((HEAD_DIM, t), _F32)) for _ in range(2))
    alive, carry = step(qi, init, True)

    def earlier(c):
        return (c[0] - 1,) + step(c[0], c[2], False)

    _, _, carry = lax.while_loop(lambda c: jnp.logical_and(c[0] >= 0, c[1]), earlier, (qi - 1, alive, carry))
    o_ref[0] = jnp.concatenate([acc for _, acc in carry], axis=0).T.astype(o_ref.dtype)


def _sb_attention(q, k, vt):
    b, h, s, d = q.shape
    t = vt.shape[-1]
    assert h % 2 == 0 and 2 * d == LANES and s % t == 0
    idx = np.arange(t)
    u = jnp.asarray(idx[None, :] > idx[:, None], _MXU)
    return pl.pallas_call(
        functools.partial(_sb_kernel, t=t),
        grid=(b, h // 2, s // t),
        in_specs=[pl.BlockSpec((1, 2, t, d), lambda bi, hi, i: (bi, hi, i, 0)),
                  pl.BlockSpec((1, 2, s, d), lambda bi, hi, i: (bi, hi, 0, 0)),
                  pl.BlockSpec((1, 2, s // t, d, t), lambda bi, hi, i: (bi, hi, 0, 0, 0)),
                  pl.BlockSpec((t, t), lambda bi, hi, i: (0, 0))],
        out_specs=pl.BlockSpec((1, t, LANES), lambda bi, hi, i: (bi, i, hi)),
        out_shape=jax.ShapeDtypeStruct((b, s, h * d), _MXU),
        scratch_shapes=[pltpu.VMEM((2, t, t), _F32), pltpu.VMEM((2, t, t), _F32), pltpu.VMEM((2, t, t), _MXU),
                        pltpu.VMEM((2, t, t), _MXU), pltpu.VMEM((2, t, t), _MXU)],
        compiler_params=_params("parallel", "parallel", "arbitrary"),
        name="sb_attn",
    )(q, k, vt, u)


def _compress_kernel(xk_ref, xv_ref, w1k_ref, w2k_ref, pk_ref, w1v_ref, w2v_ref, pv_ref, ok_ref, ov_ref):
    def hidden(x_ref, w1_ref, p_ref):
        x = x_ref[0, 0]
        n = x.shape[0]
        first = jnp.dot(x, w1_ref[0], preferred_element_type=_F32)
        second = jnp.dot(x, w1_ref[1], preferred_element_type=_F32)
        pos = _dot(p_ref[0], w1_ref[0]) + _dot(p_ref[1], w1_ref[1])
        hid = first + pltpu.roll(second, n - 1, 0) + pos[0:1]
        return 0.5 * hid * (1.0 + jnp.tanh(math.sqrt(2.0 / math.pi) * (hid + 0.044715 * hid * hid * hid)))

    ok_ref[0, 0] = _dot(hidden(xk_ref, w1k_ref, pk_ref), w2k_ref[...]).astype(ok_ref.dtype)
    ov_ref[0, 0] = _dot_nt(w2v_ref[...], hidden(xv_ref, w1v_ref, pv_ref)).astype(ov_ref.dtype)


def _compress(xk, xv, w1k, w2k, pk, w1v, w2v, pv):
    b, g, s, d = xk.shape
    n = s // CMP_STRIDE
    xk = xk.reshape(b, g, n, CMP_STRIDE * d)
    xv = xv.reshape(b, g, n, CMP_STRIDE * d)

    def full(a):
        return pl.BlockSpec(a.shape, lambda bi, gi: (0,) * a.ndim)

    xspec = pl.BlockSpec((1, 1, n, CMP_STRIDE * d), lambda bi, gi: (bi, gi, 0, 0))
    return pl.pallas_call(
        _compress_kernel,
        grid=(b, g),
        in_specs=[xspec, xspec, full(w1k), full(w2k), full(pk), full(w1v), full(w2v), full(pv)],
        out_specs=[pl.BlockSpec((1, 1, n, d), lambda bi, gi: (bi, gi, 0, 0)),
                   pl.BlockSpec((1, 1, d, n), lambda bi, gi: (bi, gi, 0, 0))],
        out_shape=[jax.ShapeDtypeStruct((b, g, n, d), _MXU), jax.ShapeDtypeStruct((b, g, d, n), _MXU)],
        compiler_params=_params("parallel", "parallel"),
        name="nsa_compress",
    )(xk, xv, w1k, w2k, pk, w1v, w2v, pv)


def _group_queries(q_ref, g, tq):
    return q_ref[0, g * NSA_GROUP:(g + 1) * NSA_GROUP].reshape(NSA_GROUP * tq, q_ref.shape[-1])


def _gated_heads(ot, gt_ref, g, branch, tq):
    out = []
    for r in range(NSA_GROUP):
        row = NSA_BRANCHES * (g * NSA_GROUP + r) + branch
        out.append(ot[:, r * tq:(r + 1) * tq] * gt_ref[0, row:row + 1, :])
    return out


def _cmp_kernel(q_ref, kc_ref, vct_ref, ov_ref, gt_ref, o_ref, qa_ref, s0_ref, s1_ref, *, tq, n_top):
    q0 = pl.program_id(1) * tq
    ncp = kc_ref.shape[2]
    ns = ov_ref.shape[0]
    lanes = NSA_GROUP * tq
    s_refs = (s0_ref, s1_ref)
    for g in range(NSA_KV_HEADS):
        s_refs[g][...] = _dot_nt(kc_ref[0, g], _group_queries(q_ref, g, tq))
    qpos = q0 + (lax.broadcasted_iota(jnp.int32, (1, lanes), 1) & (tq - 1))
    cmp_end = lax.broadcasted_iota(jnp.int32, (ncp, 1), 0) * CMP_STRIDE + (CMP_LEN - 1)
    visible = cmp_end <= qpos
    cur = jnp.right_shift(q0 + lax.broadcasted_iota(jnp.int32, (1, tq), 1), int(math.log2(SEL_LEN)))
    blk = lax.broadcasted_iota(jnp.int32, (ns, 1), 0)
    forced = (blk == 0) | (blk == cur) | (blk == cur - 1)
    future = blk > cur
    blk_f = blk.astype(_F32)
    heads = []
    scores = []
    for g in range(NSA_KV_HEADS):
        st = jnp.where(visible, s_refs[g][...], NEG_INF)
        e = jnp.exp2(st - jnp.max(st, axis=0, keepdims=True))
        inv = jnp.where(qpos >= CMP_LEN - 1, 1.0 / jnp.sum(e, axis=0, keepdims=True), 0.0)
        pt = e * inv
        heads += _gated_heads(_dot(vct_ref[0, g], pt), gt_ref, g, 0, tq)
        p_sum = sum(pt[:, r * tq:(r + 1) * tq] for r in range(NSA_GROUP))
        score = _dot_split_rhs(ov_ref[...], p_sum)
        scores.append(jnp.where(forced, FORCE_SCORE, jnp.where(future, -1.0, score)))
    o_ref[0] = jnp.concatenate(heads, axis=0).T
    for _ in range(n_top):
        for g in range(NSA_KV_HEADS):
            top = jnp.max(scores[g], axis=0, keepdims=True)
            first = jnp.min(jnp.where(scores[g] == top, blk_f, float(ns)), axis=0, keepdims=True)
            scores[g] = jnp.where(blk_f == first, PICKED, scores[g])
    for g in range(NSA_KV_HEADS):
        sel_m1 = jnp.where(scores[g].T == PICKED, 0.0, -1.0).astype(qa_ref.dtype)
        for h in range(g * NSA_GROUP, (g + 1) * NSA_GROUP):
            qa_ref[0, h, :, 0:ns] = sel_m1
            qa_ref[0, h, :, ns:ns + HEAD_DIM] = q_ref[0, h]


def _cmp_select(q, kc, vct, gates_t):
    b, h, s, d = q.shape
    g = kc.shape[1]
    ncp = kc.shape[2]
    ns = s // SEL_LEN
    n_top = min(SEL_TOPK, ns)
    tq = min(256, s)
    assert tq & (tq - 1) == 0 and g == 2
    c0 = np.arange(ncp)[:, None] * CMP_STRIDE
    n0 = np.arange(ns)[None, :] * SEL_LEN
    overlap = jnp.asarray(((c0 < n0 + SEL_LEN) & (c0 + CMP_LEN > n0)).T, _MXU)
    return pl.pallas_call(
        functools.partial(_cmp_kernel, tq=tq, n_top=n_top),
        grid=(b, s // tq),
        in_specs=[pl.BlockSpec((1, h, tq, d), lambda bi, i: (bi, 0, i, 0)),
                  pl.BlockSpec((1, g, ncp, d), lambda bi, i: (bi, 0, 0, 0)),
                  pl.BlockSpec((1, g, d, ncp), lambda bi, i: (bi, 0, 0, 0)),
                  pl.BlockSpec((ns, ncp), lambda bi, i: (0, 0)),
                  pl.BlockSpec((1, GATE_ROWS, tq), lambda bi, i: (bi, 0, i))],
        out_specs=[pl.BlockSpec((1, tq, h * d), lambda bi, i: (bi, i, 0)),
                   pl.BlockSpec((1, h, tq, ns + d), lambda bi, i: (bi, 0, i, 0))],
        out_shape=[jax.ShapeDtypeStruct((b, s, h * d), _F32), jax.ShapeDtypeStruct((b, h, s, ns + d), _MXU)],
        scratch_shapes=[pltpu.VMEM((ncp, NSA_GROUP * tq), _F32) for _ in range(g)],
        compiler_params=_params("parallel", "arbitrary"),
        name="nsa_cmp_select",
    )(q, kc, vct, overlap, gates_t)


def _sel_kernel(q_ref, k_ref, vt_ref, gt_ref, o_ref, s0_ref, s1_ref, *, tq, tk, nsub):
    q0 = pl.program_id(1) * tq
    last = (q0 + tq - 1) // tk
    lanes = NSA_GROUP * tq
    s_refs = (s0_ref, s1_ref)

    def qk(g, j):
        off = pl.multiple_of(j * tk, tk)
        s_refs[g][...] = _dot_nt(k_ref[0, g, pl.ds(off, tk), :], _group_queries(q_ref, g, tq))

    def soft(g, j, carry, causal):
        st = s_refs[g][...]
        if causal:
            key = j * tk + lax.broadcasted_iota(jnp.int32, (tk, lanes), 0)
            qry = q0 + (lax.broadcasted_iota(jnp.int32, (tk, lanes), 1) & (tq - 1))
            st = jnp.where(key <= qry, st, NEG_INF)
        return _softmax_step_t(carry, st, [vt_ref[0, g, j * nsub + c] for c in range(nsub)])

    init = tuple(_softmax_init_t(HEAD_DIM, lanes) for _ in range(NSA_KV_HEADS))
    carry = _two_chain_sweep(last, qk, soft, init)
    heads = []
    for g in range(NSA_KV_HEADS):
        heads += _gated_heads(_softmax_finish_t(carry[g], HEAD_DIM), gt_ref, g, 1, tq)
    o_ref[0] = jnp.concatenate(heads, axis=0).T


def _sel_attention(q, k, vt, gates_t):
    b, h, s, da = q.shape
    g = k.shape[1]
    d = HEAD_DIM
    tv = vt.shape[-1]
    tq = min(256, s)
    tk = min(512, s)
    assert tq & (tq - 1) == 0 and s % tk == 0 and tk % tv == 0 and g == 2
    return pl.pallas_call(
        functools.partial(_sel_kernel, tq=tq, tk=tk, nsub=tk // tv),
        grid=(b, s // tq),
        in_specs=[pl.BlockSpec((1, h, tq, da), lambda bi, i: (bi, 0, i, 0)),
                  pl.BlockSpec((1, g, s, da), lambda bi, i: (bi, 0, 0, 0)),
                  pl.BlockSpec((1, g) + vt.shape[2:], lambda bi, i: (bi, 0, 0, 0, 0)),
                  pl.BlockSpec((1, GATE_ROWS, tq), lambda bi, i: (bi, 0, i))],
        out_specs=pl.BlockSpec((1, tq, h * d), lambda bi, i: (bi, i, 0)),
        out_shape=jax.ShapeDtypeStruct((b, s, h * d), _F32),
        scratch_shapes=[pltpu.VMEM((tk, NSA_GROUP * tq), _F32) for _ in range(g)],
        compiler_params=_params("parallel", "arbitrary"),
        name="nsa_selected",
    )(q, k, vt, gates_t)


def _win_kernel(q_ref, k_ref, vt_ref, gt_ref, o_ref, s0_ref, s1_ref, *, tq, span, tv):
    q0 = pl.program_id(1) * tq
    start = pl.multiple_of(jnp.maximum(q0 - WINDOW, 0), tq)
    first_chunk = start // tv
    lanes = NSA_GROUP * tq
    s_refs = (s0_ref, s1_ref)
    for g in range(NSA_KV_HEADS):
        s_refs[g][...] = _dot_nt(k_ref[0, g, pl.ds(start, span), :], _group_queries(q_ref, g, tq))
    key = start + lax.broadcasted_iota(jnp.int32, (span, lanes), 0)
    qry = q0 + (lax.broadcasted_iota(jnp.int32, (span, lanes), 1) & (tq - 1))
    heads = []
    for g in range(NSA_KV_HEADS):
        st = jnp.where(key <= qry, s_refs[g][...], NEG_INF)
        st = jnp.where(key > qry - WINDOW, st, NEG_INF)
        carry = _softmax_step_t(_softmax_init_t(HEAD_DIM, lanes), st,
                                [vt_ref[0, g, first_chunk + c] for c in range(span // tv)])
        heads += _gated_heads(_softmax_finish_t(carry, HEAD_DIM), gt_ref, g, 2, tq)
    o_ref[0] = jnp.concatenate(heads, axis=0).T


def _win_attention(q, k, vt, gates_t):
    b, h, s, d = q.shape
    g = k.shape[1]
    tv = vt.shape[-1]
    tq = min(256, s)
    span = WINDOW + tq
    assert tq & (tq - 1) == 0 and s >= span and tq % tv == 0 and WINDOW % tv == 0 and g == 2
    return pl.pallas_call(
        functools.partial(_win_kernel, tq=tq, span=span, tv=tv),
        grid=(b, s // tq),
        in_specs=[pl.BlockSpec((1, h, tq, d), lambda bi, i: (bi, 0, i, 0)),
                  pl.BlockSpec((1, g, s, d), lambda bi, i: (bi, 0, 0, 0)),
                  pl.BlockSpec((1, g) + vt.shape[2:], lambda bi, i: (bi, 0, 0, 0, 0)),
                  pl.BlockSpec((1, GATE_ROWS, tq), lambda bi, i: (bi, 0, i))],
        out_specs=pl.BlockSpec((1, tq, h * d), lambda bi, i: (bi, i, 0)),
        out_shape=jax.ShapeDtypeStruct((b, s, h * d), _F32),
        scratch_shapes=[pltpu.VMEM((span, NSA_GROUP * tq), _F32) for _ in range(g)],
        compiler_params=_params("parallel", "arbitrary"),
        name="nsa_window",
    )(q, k, vt, gates_t)


def _out_kernel(x_ref, mla_ref, cmp_ref, sel_ref, win_ref, sb_ref, w_ref, o_ref):
    def w_rows(first_head, n_heads):
        return w_ref[first_head * HEAD_DIM:(first_head + n_heads) * HEAD_DIM, :]

    acc = x_ref[0] + jnp.dot(mla_ref[0], w_rows(0, MLA_HEADS), preferred_element_type=_F32)
    nsa = cmp_ref[0] + sel_ref[0] + win_ref[0]
    acc = acc + _dot(nsa, w_rows(MLA_HEADS, NSA_HEADS))
    o_ref[0] = acc + jnp.dot(sb_ref[0], w_rows(MLA_HEADS + NSA_HEADS, SB_HEADS), preferred_element_type=_F32)


def _out_proj(x, o_mla, o_cmp, o_sel, o_win, o_sb, w_heads):
    b, s, _ = x.shape
    ts = min(512, s)

    def heads(a):
        return pl.BlockSpec((1, a.shape[1], ts, a.shape[3]), lambda bi, i: (bi, 0, i, 0))

    def rows(a):
        return pl.BlockSpec((1, ts, a.shape[2]), lambda bi, i: (bi, i, 0))

    xspec = pl.BlockSpec((1, ts, D_MODEL), lambda bi, i: (bi, i, 0))
    return pl.pallas_call(
        _out_kernel,
        grid=(b, s // ts),
        in_specs=[xspec, rows(o_mla), rows(o_cmp), rows(o_sel), rows(o_win), rows(o_sb),
                  pl.BlockSpec(w_heads.shape, lambda bi, i: (0, 0))],
        out_specs=xspec,
        out_shape=jax.ShapeDtypeStruct(x.shape, _F32),
        compiler_params=_params("parallel", "parallel"),
        name="out_proj",
    )(x, o_mla, o_cmp, o_sel, o_win, o_sb, w_heads)


def _gather_cols(w, idx):
    idx = np.asarray(idx)
    cols = jnp.take(w, jnp.asarray(np.maximum(idx, 0)), axis=1)
    return jnp.where(jnp.asarray(idx >= 0)[None, :], cols, 0.0).astype(_MXU)


def _swap_halves(rot):
    return (np.arange(rot) + rot // 2) % rot


def _w_in_index():
    idx = np.full((_N_HEAD_COLS * HEAD_DIM,), -1, np.int64)

    def put(col, src):
        src = np.asarray(src)
        idx[col:col + len(src)] = src

    def put_head(pos, src):
        put(pos * HEAD_DIM, src)

    put(_S_CQ * LANES, _O_CQ + np.arange(MLA_Q_LORA))
    put(_S_CKV * LANES, _O_CKV + np.arange(MLA_KV_LORA))
    put(_S_KR * LANES + MLA_NOPE, _O_KR + np.arange(MLA_ROPE))
    put(_S_KRS * LANES + MLA_NOPE, _O_KR + _swap_halves(MLA_ROPE))
    for h in range(NSA_HEADS):
        put_head(_H_NQ + h, _O_NQ + h * HEAD_DIM + np.arange(HEAD_DIM))
        put_head(_H_NQS + h, _O_NQ + h * HEAD_DIM + _swap_halves(PARTIAL_ROT))
    for hk, hks, ok in ((_H_KC, _H_KCS, _O_NKC), (_H_KS, _H_KSS, _O_NKS), (_H_KW, _H_KWS, _O_NKW)):
        for g in range(NSA_KV_HEADS):
            put_head(hk + g, ok + g * HEAD_DIM + np.arange(HEAD_DIM))
            put_head(hks + g, ok + g * HEAD_DIM + _swap_halves(PARTIAL_ROT))
    for g in range(NSA_KV_HEADS):
        put_head(_H_VC + g, _O_NVC + g * HEAD_DIM + np.arange(HEAD_DIM))
    for h in range(SB_HEADS):
        put_head(_H_SBQ + h, _O_SBQ + h * HEAD_DIM + np.arange(HEAD_DIM))
        put_head(_H_SBK + h, _O_SBK + h * HEAD_DIM + np.arange(HEAD_DIM))
    return idx


def _mla_up_index():
    qd = MLA_NOPE + MLA_ROPE
    kd = MLA_NOPE + MLA_V
    uq = np.full((MLA_HEADS * LANES,), -1, np.int64)
    uqs = uq.copy()
    uk = uq.copy()
    for h in range(MLA_HEADS):
        uq[h * LANES:h * LANES + qd] = h * qd + np.arange(qd)
        uqs[h * LANES + MLA_NOPE:h * LANES + qd] = h * qd + MLA_NOPE + _swap_halves(MLA_ROPE)
        uk[h * LANES:h * LANES + MLA_NOPE] = h * kd + np.arange(MLA_NOPE)
    return uq, uqs, uk


def _transposed_weights(w_in_l, gate_bias):
    width = NSA_KV_HEADS * HEAD_DIM
    gate_rows = jnp.pad(w_in_l[:, _O_GATE:_O_GATE + N_GATES], ((0, 0), (0, _T_SBV - _T_GATE - N_GATES)))
    rows = jnp.concatenate([w_in_l[:, _O_NVS:_O_NVS + width], w_in_l[:, _O_NVW:_O_NVW + width], gate_rows,
                            w_in_l[:, _O_SBV:_O_SBV + SB_HEADS * HEAD_DIM]], axis=1).T.astype(_MXU)
    bias = jnp.pad(gate_bias, (0, GATE_ROWS - N_GATES)).reshape(GATE_ROWS, 1)
    return rows, bias


def _rope_tables(s):
    pos = jnp.arange(s, dtype=_F32)

    def cs(rot):
        half = rot // 2
        inv_freq = ROPE_THETA ** (-jnp.arange(half, dtype=_F32) / half)
        ang = pos[:, None] * inv_freq[None, :]
        c, sn = jnp.cos(ang), jnp.sin(ang)
        return jnp.concatenate([c, c], axis=1), jnp.concatenate([-sn, sn], axis=1)

    c, sn = cs(MLA_ROPE)
    ones = jnp.ones((s, MLA_NOPE), _F32)
    zeros = jnp.zeros((s, MLA_NOPE), _F32)
    pad = jnp.zeros((s, LANES - MLA_NOPE - MLA_ROPE), _F32)
    ck = jnp.concatenate([ones, c, pad], axis=1)
    sk = jnp.concatenate([zeros, sn, pad], axis=1)
    q_scale = (MLA_NOPE + MLA_ROPE) ** -0.5 * LOG2_E
    c, sn = cs(PARTIAL_ROT)
    c64 = jnp.concatenate([c, jnp.ones((s, HEAD_DIM - PARTIAL_ROT), _F32)], axis=1)
    s64 = jnp.concatenate([sn, jnp.zeros((s, HEAD_DIM - PARTIAL_ROT), _F32)], axis=1)
    ns = s // SEL_LEN
    onehot = (np.arange(s)[:, None] // SEL_LEN == np.arange(ns)[None, :]) * -NEG_INF
    return ck * q_scale, sk * q_scale, ck, sk, c64, s64, jnp.asarray(onehot, _MXU)


def kernel(x, ffn1_norm, ffn1_w_gate, ffn1_w_up, ffn1_w_down, mix_norm, w_in, mla_q_norm, mla_w_uq, mla_kv_norm,
           mla_w_ukv, nsa_gate_bias, nsa_cmp_pos_k, nsa_cmp_w1_k, nsa_cmp_w2_k, nsa_cmp_pos_v, nsa_cmp_w1_v,
           nsa_cmp_w2_v, w_out, ffn2_norm, ffn2_w_gate, ffn2_w_up, ffn2_w_down, final_norm):
    b, s, d = x.shape
    depth = w_in.shape[0]
    tabs = _rope_tables(s)
    in_idx = _w_in_index()
    uq_idx, uqs_idx, uk_idx = _mla_up_index()
    half = CMP_LEN * HEAD_DIM // 2
    fg = final_norm.reshape(1, d)

    def cmp_weights(w1, w2, pos, transpose_out):
        pos = jnp.broadcast_to(pos.reshape(2, 1, half), (2, 8, half)).astype(_MXU)
        w2 = w2.T if transpose_out else w2
        return w1.reshape(2, half, CMP_HIDDEN).astype(_MXU), w2.astype(_MXU), pos

    for l in range(depth):
        x2d = _ffn(x.reshape(b * s, d), ffn1_norm[l].reshape(1, d), ffn1_w_gate[l].astype(_MXU),
                   ffn1_w_up[l].astype(_MXU), ffn1_w_down[l].astype(_MXU), fg, False)
        x = x2d.reshape(b, s, d)
        w_t, gate_bias = _transposed_weights(w_in[l], nsa_gate_bias[l])
        (mq, mk, mvt, nq, nkc, nvc, nks, nvst, nkw, nvwt, gates_t, sbq, sbk, sbv) = _proj(
            x, mix_norm[l].reshape(1, d), _gather_cols(w_in[l], in_idx), w_t,
            mla_q_norm[l].reshape(1, -1), _gather_cols(mla_w_uq[l], uq_idx), _gather_cols(mla_w_uq[l], uqs_idx),
            mla_kv_norm[l].reshape(1, -1), _gather_cols(mla_w_ukv[l], uk_idx),
            mla_w_ukv[l].reshape(MLA_KV_LORA, MLA_HEADS, 2, MLA_V)[:, :, 1].transpose(1, 2, 0).astype(_MXU),
            gate_bias, tabs)
        o_mla = _mla_attention(mq, mk, mvt)
        kc, vct = _compress(nkc, nvc, *cmp_weights(nsa_cmp_w1_k[l], nsa_cmp_w2_k[l], nsa_cmp_pos_k[l], False),
                            *cmp_weights(nsa_cmp_w1_v[l], nsa_cmp_w2_v[l], nsa_cmp_pos_v[l], True))
        o_cmp, q_sel = _cmp_select(nq, kc, vct, gates_t)
        o_sel = _sel_attention(q_sel, nks, nvst, gates_t)
        o_win = _win_attention(nq, nkw, nvwt, gates_t)
        o_sb = _sb_attention(sbq, sbk, sbv)
        x = _out_proj(x, o_mla, o_cmp, o_sel, o_win, o_sb, w_out[l].astype(_MXU))
        x2d = _ffn(x.reshape(b * s, d), ffn2_norm[l].reshape(1, d), ffn2_w_gate[l].astype(_MXU),
                   ffn2_w_up[l].astype(_MXU), ffn2_w_down[l].astype(_MXU), fg, l == depth - 1)
        x = x2d.reshape(b, s, d)
    return x
```

```python
import functools
import math

import numpy as np
import jax
import jax.numpy as jnp
from jax import lax
from jax.experimental import pallas as pl
from jax.experimental.pallas import tpu as pltpu

D_MODEL = 1024
HEAD_DIM = 64
MLA_HEADS = 6
MLA_NOPE = 64
MLA_ROPE = 32
MLA_V = 64
MLA_Q_LORA = 256
MLA_KV_LORA = 128
NSA_HEADS = 6
NSA_KV_HEADS = 2
NSA_GROUP = NSA_HEADS // NSA_KV_HEADS
NSA_BRANCHES = 3
CMP_LEN = 32
CMP_STRIDE = 16
CMP_HIDDEN = 128
SEL_LEN = 64
SEL_TOPK = 16
WINDOW = 512
SB_HEADS = 4
D_FF = 2816
ROPE_THETA = 500000.0
PARTIAL_ROT = HEAD_DIM // 4
EPS = 1e-6
NEG_INF = -1e30
FORCE_SCORE = 1e4
PICKED = -3e38
F32_EXP2_ZERO = -151.0
LOG2_E = math.log2(math.e)
N_GATES = NSA_HEADS * NSA_BRANCHES

LANES = 128
FFN_CHUNK = 256
SWEEP_UNROLL = 4
TOKEN_CHUNK = 256
ONES_PAD = 16
VMEM_LIMIT = 56 * 1024 * 1024

_MXU = jnp.bfloat16
_F32 = jnp.float32

_IN_WIDTHS = (MLA_Q_LORA, MLA_KV_LORA, MLA_ROPE, NSA_HEADS * HEAD_DIM) + (NSA_KV_HEADS * HEAD_DIM,) * 6 + (
    N_GATES, SB_HEADS * HEAD_DIM, SB_HEADS * HEAD_DIM, SB_HEADS * HEAD_DIM)
_IN_OFF = np.concatenate([[0], np.cumsum(_IN_WIDTHS)])
(_O_CQ, _O_CKV, _O_KR, _O_NQ, _O_NKC, _O_NVC, _O_NKS, _O_NVS, _O_NKW, _O_NVW, _O_GATE, _O_SBQ, _O_SBK,
 _O_SBV) = [int(v) for v in _IN_OFF[:-1]]

_S_CQ, _S_CKV, _S_KR, _S_KRS = 0, 2, 3, 4
_H_NQ, _H_NQS = 10, 16
_H_KC, _H_KCS, _H_VC = 22, 24, 26
_H_KS, _H_KSS = 28, 30
_H_KW, _H_KWS = 32, 34
_H_SBQ, _H_SBK = 36, 40
_N_HEAD_COLS = 44
_T_VS, _T_VW, _T_GATE = 0, NSA_KV_HEADS * HEAD_DIM, 2 * NSA_KV_HEADS * HEAD_DIM
GATE_ROWS = 24
_T_SBV = _T_GATE + 32
_T_ROWS = _T_SBV + SB_HEADS * HEAD_DIM


def _dot(a, b):
    return jnp.dot(a.astype(_MXU), b.astype(_MXU), preferred_element_type=_F32)


def _dot_nt(a, b):
    return lax.dot_general(a.astype(_MXU), b.astype(_MXU), (((1,), (1,)), ((), ())),
                           preferred_element_type=_F32)


def _dot_split(a, b):
    hi = a.astype(_MXU)
    lo = (a - hi.astype(_F32)).astype(_MXU)
    return (jnp.dot(hi, b, preferred_element_type=_F32) + jnp.dot(lo, b, preferred_element_type=_F32))


def _dot_split_rhs(a, b):
    hi = b.astype(_MXU)
    lo = (b - hi.astype(_F32)).astype(_MXU)
    return (jnp.dot(a, hi, preferred_element_type=_F32) + jnp.dot(a, lo, preferred_element_type=_F32))


def _rms(x, g):
    return x * lax.rsqrt(jnp.mean(x * x, axis=-1, keepdims=True) + EPS) * g


def _params(*sem):
    return pltpu.CompilerParams(dimension_semantics=sem, vmem_limit_bytes=VMEM_LIMIT)


def _ffn_kernel(x_ref, g_ref, wg_ref, wu_ref, wd_ref, fg_ref, o_ref, h_ref, acc_ref, act_ref, *, final_norm):
    j = pl.program_id(1)

    @pl.when(j == 0)
    def _():
        h_ref[...] = _rms(x_ref[...], g_ref[...]).astype(h_ref.dtype)
        acc_ref[...] = jnp.zeros_like(acc_ref)

    h = h_ref[...]
    tf = act_ref.shape[1]
    for c0 in range(0, tf, FFN_CHUNK):
        c1 = min(c0 + FFN_CHUNK, tf)
        gate = jnp.dot(h, wg_ref[:, c0:c1], preferred_element_type=_F32)
        up = jnp.dot(h, wu_ref[:, c0:c1], preferred_element_type=_F32)
        act_ref[:, c0:c1] = (gate * jax.nn.sigmoid(gate) * up).astype(act_ref.dtype)
    acc_ref[...] += jnp.dot(act_ref[...], wd_ref[...], preferred_element_type=_F32)

    @pl.when(j == pl.num_programs(1) - 1)
    def _():
        y = x_ref[...] + 0.5 * acc_ref[...]
        if final_norm:
            y = _rms(y, fg_ref[...])
        o_ref[...] = y


def _ffn(x2d, g, wg, wu, wd, fg, final_norm):
    rows = x2d.shape[0]
    tm = min(1024, rows)
    tf = D_FF // 2
    grid = (rows // tm, D_FF // tf)
    return pl.pallas_call(
        functools.partial(_ffn_kernel, final_norm=final_norm),
        grid=grid,
        in_specs=[
            pl.BlockSpec((tm, D_MODEL), lambda i, j: (i, 0)),
            pl.BlockSpec((1, D_MODEL), lambda i, j: (0, 0)),
            pl.BlockSpec((D_MODEL, tf), lambda i, j: (0, j)),
            pl.BlockSpec((D_MODEL, tf), lambda i, j: (0, j)),
            pl.BlockSpec((tf, D_MODEL), lambda i, j: (j, 0)),
            pl.BlockSpec((1, D_MODEL), lambda i, j: (0, 0)),
        ],
        out_specs=pl.BlockSpec((tm, D_MODEL), lambda i, j: (i, 0)),
        out_shape=jax.ShapeDtypeStruct((rows, D_MODEL), _F32),
        scratch_shapes=[pltpu.VMEM((tm, D_MODEL), _MXU), pltpu.VMEM((tm, D_MODEL), _F32), pltpu.VMEM((tm, tf), _MXU)],
        compiler_params=_params("parallel", "arbitrary"),
        name="ffn",
    )(x2d, g, wg, wu, wd, fg)


def _proj_kernel(x_ref, g_ref, w_ref, wt_ref, qn_ref, wuq_ref, wuqs_ref, kvn_ref, wuk_ref, wuv_ref, gb_ref,
                 cq_ref, sq_ref, ck_ref, sk_ref, c64_ref, s64_ref, oh_ref,
                 mq_ref, mk_ref, mv_ref, nq_ref, nkc_ref, nvc_ref, nks_ref, nvs_ref, nkw_ref, nvw_ref,
                 gate_ref, sbq_ref, sbk_ref, sbv_ref):
    hn = _rms(x_ref[0], g_ref[...]).astype(_MXU)

    def proj(h0, h1):
        return jnp.dot(hn, w_ref[:, h0 * HEAD_DIM:h1 * HEAD_DIM], preferred_element_type=_F32)

    def slot(p, s):
        return p[:, s * LANES:(s + 1) * LANES]

    def head(p, i):
        return p[:, i * HEAD_DIM:(i + 1) * HEAD_DIM]

    p = proj(0, _H_NQ)
    cq = _rms(p[:, :MLA_Q_LORA], qn_ref[...])
    ckv = _rms(slot(p, _S_CKV), kvn_ref[...])
    q = _dot(cq, wuq_ref[...])
    q_partner = _dot(cq, wuqs_ref[...])
    kpe = slot(p, _S_KR) * ck_ref[...] + slot(p, _S_KRS) * sk_ref[...]
    kn = _dot(ckv, wuk_ref[...])
    for h in range(MLA_HEADS):
        mq_ref[0, h] = (slot(q, h) * cq_ref[...] + slot(q_partner, h) * sq_ref[...]).astype(mq_ref.dtype)
        mk_ref[0, h] = (slot(kn, h) + kpe).astype(mk_ref.dtype)
        mv_ref[0, h, 0] = _ones_row_pad(_dot_nt(wuv_ref[h], ckv)).astype(mv_ref.dtype)

    c64 = c64_ref[...]
    s64 = s64_ref[...]
    scale = HEAD_DIM ** -0.5

    p = proj(_H_NQ, _H_KC)
    for h in range(NSA_HEADS):
        nq_ref[0, h] = ((head(p, h) * c64 + head(p, NSA_HEADS + h) * s64) * (scale * LOG2_E)).astype(nq_ref.dtype)

    p = proj(_H_KC, _H_SBQ)
    base = _H_KC
    ns = oh_ref.shape[-1]
    for k_ref, hk, hks, lane0 in ((nkc_ref, _H_KC, _H_KCS, 0), (nks_ref, _H_KS, _H_KSS, ns), (nkw_ref, _H_KW, _H_KWS, 0)):
        for g in range(NSA_KV_HEADS):
            k = (head(p, hk - base + g) * c64 + head(p, hks - base + g) * s64).astype(k_ref.dtype)
            k_ref[0, g, :, lane0:lane0 + HEAD_DIM] = k
    for g in range(NSA_KV_HEADS):
        nks_ref[0, g, :, 0:ns] = oh_ref[...]
    for g in range(NSA_KV_HEADS):
        nvc_ref[0, g] = head(p, _H_VC - base + g).astype(nvc_ref.dtype)

    pt = _dot_nt(wt_ref[...], hn)
    for g in range(NSA_KV_HEADS):
        lo = g * HEAD_DIM
        nvs_ref[0, g, 0] = _ones_row_pad(pt[_T_VS + lo:_T_VS + lo + HEAD_DIM]).astype(nvs_ref.dtype)
        nvw_ref[0, g, 0] = _ones_row_pad(pt[_T_VW + lo:_T_VW + lo + HEAD_DIM]).astype(nvw_ref.dtype)
    gate_ref[0] = jax.nn.sigmoid(pt[_T_GATE:_T_GATE + GATE_ROWS] + gb_ref[...])

    p = proj(_H_SBQ, _N_HEAD_COLS)
    for h in range(SB_HEADS):
        sbq_ref[0, h] = (head(p, h) * (scale * LOG2_E)).astype(sbq_ref.dtype)
        sbk_ref[0, h] = head(p, SB_HEADS + h).astype(sbk_ref.dtype)
        sbv_ref[0, h, 0] = pt[_T_SBV + h * HEAD_DIM:_T_SBV + (h + 1) * HEAD_DIM].astype(sbv_ref.dtype)


def _proj(x, g, w_ext, w_t, qn, wuq, wuqs, kvn, wuk, wuv, gb, tabs):
    b, s, _ = x.shape
    ts = min(TOKEN_CHUNK, s)
    cq, sq, ck, sk, c64, s64, onehot = tabs

    def full(a):
        return pl.BlockSpec(a.shape, lambda bi, i: (0,) * a.ndim)

    def tab(a):
        return pl.BlockSpec((ts, a.shape[1]), lambda bi, i: (i, 0))

    def heads(n, d):
        return (pl.BlockSpec((1, n, ts, d), lambda bi, i: (bi, 0, i, 0)),
                jax.ShapeDtypeStruct((b, n, s, d), _MXU))

    def values_t(n):
        return (pl.BlockSpec((1, n, 1, HEAD_DIM + ONES_PAD, ts), lambda bi, i: (bi, 0, i, 0, 0)),
                jax.ShapeDtypeStruct((b, n, s // ts, HEAD_DIM + ONES_PAD, ts), _MXU))

    kv = lambda: heads(NSA_KV_HEADS, HEAD_DIM)
    outs = [heads(MLA_HEADS, LANES), heads(MLA_HEADS, LANES), values_t(MLA_HEADS), heads(NSA_HEADS, HEAD_DIM),
            kv(), kv(), heads(NSA_KV_HEADS, onehot.shape[1] + HEAD_DIM), values_t(NSA_KV_HEADS),
            kv(), values_t(NSA_KV_HEADS),
            (pl.BlockSpec((1, GATE_ROWS, ts), lambda bi, i: (bi, 0, i)), jax.ShapeDtypeStruct((b, GATE_ROWS, s), _F32)),
            heads(SB_HEADS, HEAD_DIM), heads(SB_HEADS, HEAD_DIM),
            (pl.BlockSpec((1, SB_HEADS, 1, HEAD_DIM, ts), lambda bi, i: (bi, 0, i, 0, 0)),
             jax.ShapeDtypeStruct((b, SB_HEADS, s // ts, HEAD_DIM, ts), _MXU))]
    return pl.pallas_call(
        _proj_kernel,
        grid=(b, s // ts),
        in_specs=[pl.BlockSpec((1, ts, D_MODEL), lambda bi, i: (bi, i, 0)), full(g), full(w_ext), full(w_t), full(qn),
                  full(wuq), full(wuqs), full(kvn), full(wuk), full(wuv), full(gb),
                  tab(cq), tab(sq), tab(ck), tab(sk), tab(c64), tab(s64), tab(onehot)],
        out_specs=[o[0] for o in outs],
        out_shape=[o[1] for o in outs],
        compiler_params=_params("parallel", "parallel"),
        name="proj",
    )(x, g, w_ext, w_t, qn, wuq, wuqs, kvn, wuk, wuv, gb, cq, sq, ck, sk, c64, s64, onehot)


def _softmax_step(carry, s, v):
    m, l, acc = carry
    m_new = jnp.maximum(m, jnp.max(s, axis=-1, keepdims=True))
    alpha = jnp.exp(m - m_new)
    p = jnp.exp(s - m_new)
    l = alpha * l + jnp.sum(p, axis=-1, keepdims=True)
    acc = alpha * acc + _dot(p, v)
    return m_new, l, acc


def _softmax_init(rows, d):
    return (jnp.full((rows, 1), NEG_INF, _F32), jnp.zeros((rows, 1), _F32), jnp.zeros((rows, d), _F32))


def _ones_row_pad(vt):
    first = lax.broadcasted_iota(jnp.int32, (ONES_PAD, vt.shape[1]), 0) == 0
    return jnp.concatenate([vt, jnp.where(first, 1.0, 0.0).astype(vt.dtype)], axis=0)


def _softmax_step_t(carry, st, vt_chunks):
    m, acc = carry
    m_new = jnp.maximum(m, jnp.max(st, axis=0, keepdims=True))
    alpha = jnp.exp2(m - m_new)
    pt = jnp.exp2(st - m_new).astype(_MXU)
    n = st.shape[0] // len(vt_chunks)
    pv = sum(jnp.dot(vt, pt[c * n:(c + 1) * n], preferred_element_type=_F32) for c, vt in enumerate(vt_chunks))
    return m_new, alpha * acc + pv


def _softmax_init_t(d, cols):
    return (jnp.full((1, cols), NEG_INF, _F32), jnp.zeros((d + ONES_PAD, cols), _F32))


def _softmax_finish_t(carry, d):
    _, acc = carry
    return acc[:d] * (1.0 / acc[d:d + 1])


def _two_chain_sweep(n_full, qk, soft, init):
    qk(1, 0)

    def body(j, carry):
        c0, c1 = carry
        qk(0, j)
        c1 = soft(1, j, c1, False)
        qk(1, j + 1)
        c0 = soft(0, j, c0, False)
        return c0, c1

    def unrolled(i, carry):
        for u in range(SWEEP_UNROLL):
            carry = body(SWEEP_UNROLL * i + u, carry)
        return carry

    trips = n_full // SWEEP_UNROLL
    carry = lax.fori_loop(0, trips, unrolled, init)
    c0, c1 = lax.fori_loop(SWEEP_UNROLL * trips, n_full, body, carry)
    qk(0, n_full)
    c1 = soft(1, n_full, c1, True)
    c0 = soft(0, n_full, c0, True)
    return c0, c1


def _mla_kernel(q_ref, k_ref, vt_ref, o_ref, s0_ref, s1_ref, *, t, nsub):
    qi = pl.program_id(2)
    s_refs = (s0_ref, s1_ref)

    def qk(hh, j):
        off = pl.multiple_of(j * t, t)
        s_refs[hh][...] = _dot_nt(k_ref[0, hh, pl.ds(off, t), :], q_ref[0, hh])

    def soft(hh, j, carry, diag):
        st = s_refs[hh][...]
        if diag:
            key = lax.broadcasted_iota(jnp.int32, (t, t), 0)
            qry = lax.broadcasted_iota(jnp.int32, (t, t), 1)
            st = jnp.where(key <= qry, st, NEG_INF)
        return _softmax_step_t(carry, st, [vt_ref[0, hh, j * nsub + c] for c in range(nsub)])

    carry = _two_chain_sweep(qi, qk, soft, tuple(_softmax_init_t(MLA_V, t) for _ in range(2)))
    ot = jnp.concatenate([_softmax_finish_t(c, MLA_V) for c in carry], axis=0)
    o_ref[0] = ot.T.astype(o_ref.dtype)


def _mla_attention(q, k, vt):
    b, h, s, _ = q.shape
    tv = vt.shape[-1]
    dv = vt.shape[-2]
    t = min(512, s)
    assert h % 2 == 0 and 2 * MLA_V == LANES and t % tv == 0
    return pl.pallas_call(
        functools.partial(_mla_kernel, t=t, nsub=t // tv),
        grid=(b, h // 2, s // t),
        in_specs=[pl.BlockSpec((1, 2, t, LANES), lambda bi, hi, i: (bi, hi, i, 0)),
                  pl.BlockSpec((1, 2, s, LANES), lambda bi, hi, i: (bi, hi, 0, 0)),
                  pl.BlockSpec((1, 2, s // tv, dv, tv), lambda bi, hi, i: (bi, hi, 0, 0, 0))],
        out_specs=pl.BlockSpec((1, t, LANES), lambda bi, hi, i: (bi, i, hi)),
        out_shape=jax.ShapeDtypeStruct((b, s, h * MLA_V), _MXU),
        scratch_shapes=[pltpu.VMEM((t, t), _F32), pltpu.VMEM((t, t), _F32)],
        compiler_params=_params("parallel", "parallel", "arbitrary"),
        name="mla_attn",
    )(q, k, vt)


def _sb_kernel(q_ref, k_ref, vt_ref, u_ref, o_ref, z_ref, lb_ref, hi_ref, lo_ref, a_ref, *, t):
    qi = pl.program_id(2)
    u = u_ref[...]

    def step(j, carry, diag):
        off = pl.multiple_of(j * t, t)
        if diag:
            key = lax.broadcasted_iota(jnp.int32, (t, t), 0)
            qry = lax.broadcasted_iota(jnp.int32, (t, t), 1)
            strict = key < qry
        for hh in range(2):
            z_ref[hh] = _dot_nt(k_ref[0, hh, pl.ds(off, t), :], q_ref[0, hh])
        first_rem = []
        for hh in range(2):
            z = z_ref[hh]
            log_beta = jnp.minimum(z, 0.0) - jnp.log2(1.0 + jnp.exp2(-jnp.abs(z)))
            log_rem = log_beta - z
            if diag:
                log_rem = jnp.where(strict, log_rem, 0.0)
            hi = log_rem.astype(_MXU)
            lb_ref[hh] = log_beta
            hi_ref[hh] = hi
            lo_ref[hh] = (log_rem - hi.astype(_F32)).astype(_MXU)
            first_rem.append(log_rem[0:1, :])
        out = []
        for hh in range(2):
            rem, acc = carry[hh]
            suffix = (jnp.dot(u, hi_ref[hh], preferred_element_type=_F32)
                      + jnp.dot(u, lo_ref[hh], preferred_element_type=_F32))
            a = jnp.exp2(lb_ref[hh] + suffix + rem)
            if diag:
                a = jnp.where(strict, a, 0.0)
            a_ref[hh] = a.astype(_MXU)
            out.append((rem + suffix[0:1, :] + first_rem[hh], acc))
        alive = jnp.max(jnp.maximum(out[0][0], out[1][0])) > F32_EXP2_ZERO
        out = tuple((rem, acc + jnp.dot(vt_ref[0, hh, j], a_ref[hh], preferred_element_type=_F32))
                    for hh, (rem, acc) in enumerate(out))
        return alive, out

    init = tuple((jnp.zeros((1, t), _F32), jnp.zeros((HEAD_DIM, t), _F32)) for _ in range(2))
    alive, carry = step(qi, init, True)

    def earlier(c):
        return (c[0] - 1,) + step(c[0], c[2], False)

    _, _, carry = lax.while_loop(lambda c: jnp.logical_and(c[0] >= 0, c[1]), earlier, (qi - 1, alive, carry))
    o_ref[0] = jnp.concatenate([acc for _, acc in carry], axis=0).T.astype(o_ref.dtype)


def _sb_attention(q, k, vt):
    b, h, s, d = q.shape
    t = vt.shape[-1]
    assert h % 2 == 0 and 2 * d == LANES and s % t == 0
    idx = np.arange(t)
    u = jnp.asarray(idx[None, :] > idx[:, None], _MXU)
    return pl.pallas_call(
        functools.partial(_sb_kernel, t=t),
        grid=(b, h // 2, s // t),
        in_specs=[pl.BlockSpec((1, 2, t, d), lambda bi, hi, i: (bi, hi, i, 0)),
                  pl.BlockSpec((1, 2, s, d), lambda bi, hi, i: (bi, hi, 0, 0)),
                  pl.BlockSpec((1, 2, s // t, d, t), lambda bi, hi, i: (bi, hi, 0, 0, 0)),
                  pl.BlockSpec((t, t), lambda bi, hi, i: (0, 0))],
        out_specs=pl.BlockSpec((1, t, LANES), lambda bi, hi, i: (bi, i, hi)),
        out_shape=jax.ShapeDtypeStruct((b, s, h * d), _MXU),
        scratch_shapes=[pltpu.VMEM((2, t, t), _F32), pltpu.VMEM((2, t, t), _F32), pltpu.VMEM((2, t, t), _MXU),
                        pltpu.VMEM((2, t, t), _MXU), pltpu.VMEM((2, t, t), _MXU)],
        compiler_params=_params("parallel", "parallel", "arbitrary"),
        name="sb_attn",
    )(q, k, vt, u)


def _compress_kernel(xk_ref, xv_ref, w1k_ref, w2k_ref, pk_ref, w1v_ref, w2v_ref, pv_ref, ok_ref, ov_ref):
    def hidden(x_ref, w1_ref, p_ref):
        x = x_ref[0, 0]
        n = x.shape[0]
        first = jnp.dot(x, w1_ref[0], preferred_element_type=_F32)
        second = jnp.dot(x, w1_ref[1], preferred_element_type=_F32)
        pos = _dot(p_ref[0], w1_ref[0]) + _dot(p_ref[1], w1_ref[1])
        hid = first + pltpu.roll(second, n - 1, 0) + pos[0:1]
        return 0.5 * hid * (1.0 + jnp.tanh(math.sqrt(2.0 / math.pi) * (hid + 0.044715 * hid * hid * hid)))

    ok_ref[0, 0] = _dot(hidden(xk_ref, w1k_ref, pk_ref), w2k_ref[...]).astype(ok_ref.dtype)
    ov_ref[0, 0] = _dot_nt(w2v_ref[...], hidden(xv_ref, w1v_ref, pv_ref)).astype(ov_ref.dtype)


def _compress(xk, xv, w1k, w2k, pk, w1v, w2v, pv):
    b, g, s, d = xk.shape
    n = s // CMP_STRIDE
    xk = xk.reshape(b, g, n, CMP_STRIDE * d)
    xv = xv.reshape(b, g, n, CMP_STRIDE * d)

    def full(a):
        return pl.BlockSpec(a.shape, lambda bi, gi: (0,) * a.ndim)

    xspec = pl.BlockSpec((1, 1, n, CMP_STRIDE * d), lambda bi, gi: (bi, gi, 0, 0))
    return pl.pallas_call(
        _compress_kernel,
        grid=(b, g),
        in_specs=[xspec, xspec, full(w1k), full(w2k), full(pk), full(w1v), full(w2v), full(pv)],
        out_specs=[pl.BlockSpec((1, 1, n, d), lambda bi, gi: (bi, gi, 0, 0)),
                   pl.BlockSpec((1, 1, d, n), lambda bi, gi: (bi, gi, 0, 0))],
        out_shape=[jax.ShapeDtypeStruct((b, g, n, d), _MXU), jax.ShapeDtypeStruct((b, g, d, n), _MXU)],
        compiler_params=_params("parallel", "parallel"),
        name="nsa_compress",
    )(xk, xv, w1k, w2k, pk, w1v, w2v, pv)


def _group_queries(q_ref, g, tq):
    return q_ref[0, g * NSA_GROUP:(g + 1) * NSA_GROUP].reshape(NSA_GROUP * tq, q_ref.shape[-1])


def _gated_heads(ot, gt_ref, g, branch, tq):
    out = []
    for r in range(NSA_GROUP):
        row = NSA_BRANCHES * (g * NSA_GROUP + r) + branch
        out.append(ot[:, r * tq:(r + 1) * tq] * gt_ref[0, row:row + 1, :])
    return out


def _cmp_kernel(q_ref, kc_ref, vct_ref, ov_ref, gt_ref, o_ref, qa_ref, s0_ref, s1_ref, *, tq, n_top):
    q0 = pl.program_id(1) * tq
    ncp = kc_ref.shape[2]
    ns = ov_ref.shape[0]
    lanes = NSA_GROUP * tq
    s_refs = (s0_ref, s1_ref)
    for g in range(NSA_KV_HEADS):
        s_refs[g][...] = _dot_nt(kc_ref[0, g], _group_queries(q_ref, g, tq))
    qpos = q0 + (lax.broadcasted_iota(jnp.int32, (1, lanes), 1) & (tq - 1))
    cmp_end = lax.broadcasted_iota(jnp.int32, (ncp, 1), 0) * CMP_STRIDE + (CMP_LEN - 1)
    visible = cmp_end <= qpos
    cur = jnp.right_shift(q0 + lax.broadcasted_iota(jnp.int32, (1, tq), 1), int(math.log2(SEL_LEN)))
    blk = lax.broadcasted_iota(jnp.int32, (ns, 1), 0)
    forced = (blk == 0) | (blk == cur) | (blk == cur - 1)
    future = blk > cur
    blk_f = blk.astype(_F32)
    heads = []
    scores = []
    for g in range(NSA_KV_HEADS):
        st = jnp.where(visible, s_refs[g][...], NEG_INF)
        e = jnp.exp2(st - jnp.max(st, axis=0, keepdims=True))
        inv = jnp.where(qpos >= CMP_LEN - 1, 1.0 / jnp.sum(e, axis=0, keepdims=True), 0.0)
        pt = e * inv
        heads += _gated_heads(_dot(vct_ref[0, g], pt), gt_ref, g, 0, tq)
        p_sum = sum(pt[:, r * tq:(r + 1) * tq] for r in range(NSA_GROUP))
        score = _dot_split_rhs(ov_ref[...], p_sum)
        scores.append(jnp.where(forced, FORCE_SCORE, jnp.where(future, -1.0, score)))
    o_ref[0] = jnp.concatenate(heads, axis=0).T
    for _ in range(n_top):
        for g in range(NSA_KV_HEADS):
            top = jnp.max(scores[g], axis=0, keepdims=True)
            first = jnp.min(jnp.where(scores[g] == top, blk_f, float(ns)), axis=0, keepdims=True)
            scores[g] = jnp.where(blk_f == first, PICKED, scores[g])
    for g in range(NSA_KV_HEADS):
        sel_m1 = jnp.where(scores[g].T == PICKED, 0.0, -1.0).astype(qa_ref.dtype)
        for h in range(g * NSA_GROUP, (g + 1) * NSA_GROUP):
            qa_ref[0, h, :, 0:ns] = sel_m1
            qa_ref[0, h, :, ns:ns + HEAD_DIM] = q_ref[0, h]


def _cmp_select(q, kc, vct, gates_t):
    b, h, s, d = q.shape
    g = kc.shape[1]
    ncp = kc.shape[2]
    ns = s // SEL_LEN
    n_top = min(SEL_TOPK, ns)
    tq = min(256, s)
    assert tq & (tq - 1) == 0 and g == 2
    c0 = np.arange(ncp)[:, None] * CMP_STRIDE
    n0 = np.arange(ns)[None, :] * SEL_LEN
    overlap = jnp.asarray(((c0 < n0 + SEL_LEN) & (c0 + CMP_LEN > n0)).T, _MXU)
    return pl.pallas_call(
        functools.partial(_cmp_kernel, tq=tq, n_top=n_top),
        grid=(b, s // tq),
        in_specs=[pl.BlockSpec((1, h, tq, d), lambda bi, i: (bi, 0, i, 0)),
                  pl.BlockSpec((1, g, ncp, d), lambda bi, i: (bi, 0, 0, 0)),
                  pl.BlockSpec((1, g, d, ncp), lambda bi, i: (bi, 0, 0, 0)),
                  pl.BlockSpec((ns, ncp), lambda bi, i: (0, 0)),
                  pl.BlockSpec((1, GATE_ROWS, tq), lambda bi, i: (bi, 0, i))],
        out_specs=[pl.BlockSpec((1, tq, h * d), lambda bi, i: (bi, i, 0)),
                   pl.BlockSpec((1, h, tq, ns + d), lambda bi, i: (bi, 0, i, 0))],
        out_shape=[jax.ShapeDtypeStruct((b, s, h * d), _F32), jax.ShapeDtypeStruct((b, h, s, ns + d), _MXU)],
        scratch_shapes=[pltpu.VMEM((ncp, NSA_GROUP * tq), _F32) for _ in range(g)],
        compiler_params=_params("parallel", "arbitrary"),
        name="nsa_cmp_select",
    )(q, kc, vct, overlap, gates_t)


def _sel_kernel(q_ref, k_ref, vt_ref, gt_ref, o_ref, s0_ref, s1_ref, *, tq, tk, nsub):
    q0 = pl.program_id(1) * tq
    last = (q0 + tq - 1) // tk
    lanes = NSA_GROUP * tq
    s_refs = (s0_ref, s1_ref)

    def qk(g, j):
        off = pl.multiple_of(j * tk, tk)
        s_refs[g][...] = _dot_nt(k_ref[0, g, pl.ds(off, tk), :], _group_queries(q_ref, g, tq))

    def soft(g, j, carry, causal):
        st = s_refs[g][...]
        if causal:
            key = j * tk + lax.broadcasted_iota(jnp.int32, (tk, lanes), 0)
            qry = q0 + (lax.broadcasted_iota(jnp.int32, (tk, lanes), 1) & (tq - 1))
            st = jnp.where(key <= qry, st, NEG_INF)
        return _softmax_step_t(carry, st, [vt_ref[0, g, j * nsub + c] for c in range(nsub)])

    init = tuple(_softmax_init_t(HEAD_DIM, lanes) for _ in range(NSA_KV_HEADS))
    carry = _two_chain_sweep(last, qk, soft, init)
    heads = []
    for g in range(NSA_KV_HEADS):
        heads += _gated_heads(_softmax_finish_t(carry[g], HEAD_DIM), gt_ref, g, 1, tq)
    o_ref[0] = jnp.concatenate(heads, axis=0).T


def _sel_attention(q, k, vt, gates_t):
    b, h, s, da = q.shape
    g = k.shape[1]
    d = HEAD_DIM
    tv = vt.shape[-1]
    tq = min(256, s)
    tk = min(512, s)
    assert tq & (tq - 1) == 0 and s % tk == 0 and tk % tv == 0 and g == 2
    return pl.pallas_call(
        functools.partial(_sel_kernel, tq=tq, tk=tk, nsub=tk // tv),
        grid=(b, s // tq),
        in_specs=[pl.BlockSpec((1, h, tq, da), lambda bi, i: (bi, 0, i, 0)),
                  pl.BlockSpec((1, g, s, da), lambda bi, i: (bi, 0, 0, 0)),
                  pl.BlockSpec((1, g) + vt.shape[2:], lambda bi, i: (bi, 0, 0, 0, 0)),
                  pl.BlockSpec((1, GATE_ROWS, tq), lambda bi, i: (bi, 0, i))],
        out_specs=pl.BlockSpec((1, tq, h * d), lambda bi, i: (bi, i, 0)),
        out_shape=jax.ShapeDtypeStruct((b, s, h * d), _F32),
        scratch_shapes=[pltpu.VMEM((tk, NSA_GROUP * tq), _F32) for _ in range(g)],
        compiler_params=_params("parallel", "arbitrary"),
        name="nsa_selected",
    )(q, k, vt, gates_t)


def _win_kernel(q_ref, k_ref, vt_ref, gt_ref, o_ref, s0_ref, s1_ref, *, tq, span, tv):
    q0 = pl.program_id(1) * tq
    start = pl.multiple_of(jnp.maximum(q0 - WINDOW, 0), tq)
    first_chunk = start // tv
    lanes = NSA_GROUP * tq
    s_refs = (s0_ref, s1_ref)
    for g in range(NSA_KV_HEADS):
        s_refs[g][...] = _dot_nt(k_ref[0, g, pl.ds(start, span), :], _group_queries(q_ref, g, tq))
    key = start + lax.broadcasted_iota(jnp.int32, (span, lanes), 0)
    qry = q0 + (lax.broadcasted_iota(jnp.int32, (span, lanes), 1) & (tq - 1))
    heads = []
    for g in range(NSA_KV_HEADS):
        st = jnp.where(key <= qry, s_refs[g][...], NEG_INF)
        st = jnp.where(key > qry - WINDOW, st, NEG_INF)
        carry = _softmax_step_t(_softmax_init_t(HEAD_DIM, lanes), st,
                                [vt_ref[0, g, first_chunk + c] for c in range(span // tv)])
        heads += _gated_heads(_softmax_finish_t(carry, HEAD_DIM), gt_ref, g, 2, tq)
    o_ref[0] = jnp.concatenate(heads, axis=0).T


def _win_attention(q, k, vt, gates_t):
    b, h, s, d = q.shape
    g = k.shape[1]
    tv = vt.shape[-1]
    tq = min(256, s)
    span = WINDOW + tq
    assert tq & (tq - 1) == 0 and s >= span and tq % tv == 0 and WINDOW % tv == 0 and g == 2
    return pl.pallas_call(
        functools.partial(_win_kernel, tq=tq, span=span, tv=tv),
        grid=(b, s // tq),
        in_specs=[pl.BlockSpec((1, h, tq, d), lambda bi, i: (bi, 0, i, 0)),
                  pl.BlockSpec((1, g, s, d), lambda bi, i: (bi, 0, 0, 0)),
                  pl.BlockSpec((1, g) + vt.shape[2:], lambda bi, i: (bi, 0, 0, 0, 0)),
                  pl.BlockSpec((1, GATE_ROWS, tq), lambda bi, i: (bi, 0, i))],
        out_specs=pl.BlockSpec((1, tq, h * d), lambda bi, i: (bi, i, 0)),
        out_shape=jax.ShapeDtypeStruct((b, s, h * d), _F32),
        scratch_shapes=[pltpu.VMEM((span, NSA_GROUP * tq), _F32) for _ in range(g)],
        compiler_params=_params("parallel", "arbitrary"),
        name="nsa_window",
    )(q, k, vt, gates_t)


def _out_kernel(x_ref, mla_ref, cmp_ref, sel_ref, win_ref, sb_ref, w_ref, o_ref):
    def w_rows(first_head, n_heads):
        return w_ref[first_head * HEAD_DIM:(first_head + n_heads) * HEAD_DIM, :]

    acc = x_ref[0] + jnp.dot(mla_ref[0], w_rows(0, MLA_HEADS), preferred_element_type=_F32)
    nsa = cmp_ref[0] + sel_ref[0] + win_ref[0]
    acc = acc + _dot(nsa, w_rows(MLA_HEADS, NSA_HEADS))
    o_ref[0] = acc + jnp.dot(sb_ref[0], w_rows(MLA_HEADS + NSA_HEADS, SB_HEADS), preferred_element_type=_F32)


def _out_proj(x, o_mla, o_cmp, o_sel, o_win, o_sb, w_heads):
    b, s, _ = x.shape
    ts = min(512, s)

    def heads(a):
        return pl.BlockSpec((1, a.shape[1], ts, a.shape[3]), lambda bi, i: (bi, 0, i, 0))

    def rows(a):
        return pl.BlockSpec((1, ts, a.shape[2]), lambda bi, i: (bi, i, 0))

    xspec = pl.BlockSpec((1, ts, D_MODEL), lambda bi, i: (bi, i, 0))
    return pl.pallas_call(
        _out_kernel,
        grid=(b, s // ts),
        in_specs=[xspec, rows(o_mla), rows(o_cmp), rows(o_sel), rows(o_win), rows(o_sb),
                  pl.BlockSpec(w_heads.shape, lambda bi, i: (0, 0))],
        out_specs=xspec,
        out_shape=jax.ShapeDtypeStruct(x.shape, _F32),
        compiler_params=_params("parallel", "parallel"),
        name="out_proj",
    )(x, o_mla, o_cmp, o_sel, o_win, o_sb, w_heads)


def _gather_cols(w, idx):
    idx = np.asarray(idx)
    cols = jnp.take(w, jnp.asarray(np.maximum(idx, 0)), axis=1)
    return jnp.where(jnp.asarray(idx >= 0)[None, :], cols, 0.0).astype(_MXU)


def _swap_halves(rot):
    return (np.arange(rot) + rot // 2) % rot


def _w_in_index():
    idx = np.full((_N_HEAD_COLS * HEAD_DIM,), -1, np.int64)

    def put(col, src):
        src = np.asarray(src)
        idx[col:col + len(src)] = src

    def put_head(pos, src):
        put(pos * HEAD_DIM, src)

    put(_S_CQ * LANES, _O_CQ + np.arange(MLA_Q_LORA))
    put(_S_CKV * LANES, _O_CKV + np.arange(MLA_KV_LORA))
    put(_S_KR * LANES + MLA_NOPE, _O_KR + np.arange(MLA_ROPE))
    put(_S_KRS * LANES + MLA_NOPE, _O_KR + _swap_halves(MLA_ROPE))
    for h in range(NSA_HEADS):
        put_head(_H_NQ + h, _O_NQ + h * HEAD_DIM + np.arange(HEAD_DIM))
        put_head(_H_NQS + h, _O_NQ + h * HEAD_DIM + _swap_halves(PARTIAL_ROT))
    for hk, hks, ok in ((_H_KC, _H_KCS, _O_NKC), (_H_KS, _H_KSS, _O_NKS), (_H_KW, _H_KWS, _O_NKW)):
        for g in range(NSA_KV_HEADS):
            put_head(hk + g, ok + g * HEAD_DIM + np.arange(HEAD_DIM))
            put_head(hks + g, ok + g * HEAD_DIM + _swap_halves(PARTIAL_ROT))
    for g in range(NSA_KV_HEADS):
        put_head(_H_VC + g, _O_NVC + g * HEAD_DIM + np.arange(HEAD_DIM))
    for h in range(SB_HEADS):
        put_head(_H_SBQ + h, _O_SBQ + h * HEAD_DIM + np.arange(HEAD_DIM))
        put_head(_H_SBK + h, _O_SBK + h * HEAD_DIM + np.arange(HEAD_DIM))
    return idx


def _mla_up_index():
    qd = MLA_NOPE + MLA_ROPE
    kd = MLA_NOPE + MLA_V
    uq = np.full((MLA_HEADS * LANES,), -1, np.int64)
    uqs = uq.copy()
    uk = uq.copy()
    for h in range(MLA_HEADS):
        uq[h * LANES:h * LANES + qd] = h * qd + np.arange(qd)
        uqs[h * LANES + MLA_NOPE:h * LANES + qd] = h * qd + MLA_NOPE + _swap_halves(MLA_ROPE)
        uk[h * LANES:h * LANES + MLA_NOPE] = h * kd + np.arange(MLA_NOPE)
    return uq, uqs, uk


def _transposed_weights(w_in_l, gate_bias):
    width = NSA_KV_HEADS * HEAD_DIM
    gate_rows = jnp.pad(w_in_l[:, _O_GATE:_O_GATE + N_GATES], ((0, 0), (0, _T_SBV - _T_GATE - N_GATES)))
    rows = jnp.concatenate([w_in_l[:, _O_NVS:_O_NVS + width], w_in_l[:, _O_NVW:_O_NVW + width], gate_rows,
                            w_in_l[:, _O_SBV:_O_SBV + SB_HEADS * HEAD_DIM]], axis=1).T.astype(_MXU)
    bias = jnp.pad(gate_bias, (0, GATE_ROWS - N_GATES)).reshape(GATE_ROWS, 1)
    return rows, bias


def _rope_tables(s):
    pos = jnp.arange(s, dtype=_F32)

    def cs(rot):
        half = rot // 2
        inv_freq = ROPE_THETA ** (-jnp.arange(half, dtype=_F32) / half)
        ang = pos[:, None] * inv_freq[None, :]
        c, sn = jnp.cos(ang), jnp.sin(ang)
        return jnp.concatenate([c, c], axis=1), jnp.concatenate([-sn, sn], axis=1)

    c, sn = cs(MLA_ROPE)
    ones = jnp.ones((s, MLA_NOPE), _F32)
    zeros = jnp.zeros((s, MLA_NOPE), _F32)
    pad = jnp.zeros((s, LANES - MLA_NOPE - MLA_ROPE), _F32)
    ck = jnp.concatenate([ones, c, pad], axis=1)
    sk = jnp.concatenate([zeros, sn, pad], axis=1)
    q_scale = (MLA_NOPE + MLA_ROPE) ** -0.5 * LOG2_E
    c, sn = cs(PARTIAL_ROT)
    c64 = jnp.concatenate([c, jnp.ones((s, HEAD_DIM - PARTIAL_ROT), _F32)], axis=1)
    s64 = jnp.concatenate([sn, jnp.zeros((s, HEAD_DIM - PARTIAL_ROT), _F32)], axis=1)
    ns = s // SEL_LEN
    onehot = (np.arange(s)[:, None] // SEL_LEN == np.arange(ns)[None, :]) * -NEG_INF
    return ck * q_scale, sk * q_scale, ck, sk, c64, s64, jnp.asarray(onehot, _MXU)


def kernel(x, ffn1_norm, ffn1_w_gate, ffn1_w_up, ffn1_w_down, mix_norm, w_in, mla_q_norm, mla_w_uq, mla_kv_norm,
           mla_w_ukv, nsa_gate_bias, nsa_cmp_pos_k, nsa_cmp_w1_k, nsa_cmp_w2_k, nsa_cmp_pos_v, nsa_cmp_w1_v,
           nsa_cmp_w2_v, w_out, ffn2_norm, ffn2_w_gate, ffn2_w_up, ffn2_w_down, final_norm):
    b, s, d = x.shape
    depth = w_in.shape[0]
    tabs = _rope_tables(s)
    in_idx = _w_in_index()
    uq_idx, uqs_idx, uk_idx = _mla_up_index()
    half = CMP_LEN * HEAD_DIM // 2
    fg = final_norm.reshape(1, d)

    def cmp_weights(w1, w2, pos, transpose_out):
        pos = jnp.broadcast_to(pos.reshape(2, 1, half), (2, 8, half)).astype(_MXU)
        w2 = w2.T if transpose_out else w2
        return w1.reshape(2, half, CMP_HIDDEN).astype(_MXU), w2.astype(_MXU), pos

    for l in range(depth):
        x2d = _ffn(x.reshape(b * s, d), ffn1_norm[l].reshape(1, d), ffn1_w_gate[l].astype(_MXU),
                   ffn1_w_up[l].astype(_MXU), ffn1_w_down[l].astype(_MXU), fg, False)
        x = x2d.reshape(b, s, d)
        w_t, gate_bias = _transposed_weights(w_in[l], nsa_gate_bias[l])
        (mq, mk, mvt, nq, nkc, nvc, nks, nvst, nkw, nvwt, gates_t, sbq, sbk, sbv) = _proj(
            x, mix_norm[l].reshape(1, d), _gather_cols(w_in[l], in_idx), w_t,
            mla_q_norm[l].reshape(1, -1), _gather_cols(mla_w_uq[l], uq_idx), _gather_cols(mla_w_uq[l], uqs_idx),
            mla_kv_norm[l].reshape(1, -1), _gather_cols(mla_w_ukv[l], uk_idx),
            mla_w_ukv[l].reshape(MLA_KV_LORA, MLA_HEADS, 2, MLA_V)[:, :, 1].transpose(1, 2, 0).astype(_MXU),
            gate_bias, tabs)
        o_mla = _mla_attention(mq, mk, mvt)
        kc, vct = _compress(nkc, nvc, *cmp_weights(nsa_cmp_w1_k[l], nsa_cmp_w2_k[l], nsa_cmp_pos_k[l], False),
                            *cmp_weights(nsa_cmp_w1_v[l], nsa_cmp_w2_v[l], nsa_cmp_pos_v[l], True))
        o_cmp, q_sel = _cmp_select(nq, kc, vct, gates_t)
        o_sel = _sel_attention(q_sel, nks, nvst, gates_t)
        o_win = _win_attention(nq, nkw, nvwt, gates_t)
        o_sb = _sb_attention(sbq, sbk, sbv)
        x = _out_proj(x, o_mla, o_cmp, o_sel, o_win, o_sb, w_out[l].astype(_MXU))
        x2d = _ffn(x.reshape(b * s, d), ffn2_norm[l].reshape(1, d), ffn2_w_gate[l].astype(_MXU),
                   ffn2_w_up[l].astype(_MXU), ffn2_w_down[l].astype(_MXU), fg, l == depth - 1)
        x = x2d.reshape(b, s, d)
    return x
```

```python
import functools
import math

import numpy as np
import jax
import jax.numpy as jnp
from jax import lax
from jax.experimental import pallas as pl
from jax.experimental.pallas import tpu as pltpu

D_MODEL = 1024
HEAD_DIM = 64
MLA_HEADS = 6
MLA_NOPE = 64
MLA_ROPE = 32
MLA_V = 64
MLA_Q_LORA = 256
MLA_KV_LORA = 128
NSA_HEADS = 6
NSA_KV_HEADS = 2
NSA_GROUP = NSA_HEADS // NSA_KV_HEADS
NSA_BRANCHES = 3
CMP_LEN = 32
CMP_STRIDE = 16
CMP_HIDDEN = 128
SEL_LEN = 64
SEL_TOPK = 16
WINDOW = 512
SB_HEADS = 4
D_FF = 2816
ROPE_THETA = 500000.0
PARTIAL_ROT = HEAD_DIM // 4
EPS = 1e-6
NEG_INF = -1e30
M_FLOOR = 0.1 * NEG_INF
FORCE_SCORE = 1e4
PICKED = -3e38
F32_EXP2_ZERO = -151.0
LOG2_E = math.log2(math.e)
N_GATES = NSA_HEADS * NSA_BRANCHES

SUBLANES = 8
LANES = 128
FFN_CHUNK = 256
SWEEP_UNROLL = 4
TOKEN_CHUNK = 256
ONES_PAD = 16
VMEM_LIMIT = 56 * 1024 * 1024

_MXU = jnp.bfloat16
_F32 = jnp.float32

_IN_WIDTHS = (MLA_Q_LORA, MLA_KV_LORA, MLA_ROPE, NSA_HEADS * HEAD_DIM) + (NSA_KV_HEADS * HEAD_DIM,) * 6 + (
    N_GATES, SB_HEADS * HEAD_DIM, SB_HEADS * HEAD_DIM, SB_HEADS * HEAD_DIM)
_IN_OFF = np.concatenate([[0], np.cumsum(_IN_WIDTHS)])
(_O_CQ, _O_CKV, _O_KR, _O_NQ, _O_NKC, _O_NVC, _O_NKS, _O_NVS, _O_NKW, _O_NVW, _O_GATE, _O_SBQ, _O_SBK,
 _O_SBV) = [int(v) for v in _IN_OFF[:-1]]

_S_CQ, _S_CKV, _S_KR, _S_KRS = 0, 2, 3, 4
_H_NQ, _H_NQS = 10, 16
_H_KC, _H_KCS, _H_VC = 22, 24, 26
_H_KS, _H_KSS = 28, 30
_H_KW, _H_KWS = 32, 34
_H_SBQ, _H_SBK = 36, 40
_N_HEAD_COLS = 44
_T_VS, _T_VW, _T_GATE = 0, NSA_KV_HEADS * HEAD_DIM, 2 * NSA_KV_HEADS * HEAD_DIM
GATE_ROWS = 24
_T_SBV = _T_GATE + 32
_T_ROWS = _T_SBV + SB_HEADS * HEAD_DIM


def _dot(a, b):
    return jnp.dot(a.astype(_MXU), b.astype(_MXU), preferred_element_type=_F32)


def _dot_nt(a, b):
    return lax.dot_general(a.astype(_MXU), b.astype(_MXU), (((1,), (1,)), ((), ())),
                           preferred_element_type=_F32)


def _dot_split_rhs(a, b):
    hi = b.astype(_MXU)
    lo = (b - hi.astype(_F32)).astype(_MXU)
    return (jnp.dot(a, hi, preferred_element_type=_F32) + jnp.dot(a, lo, preferred_element_type=_F32))


def _rms(x, g):
    return x * lax.rsqrt(jnp.mean(x * x, axis=-1, keepdims=True) + EPS) * g


def _params(*sem):
    return pltpu.CompilerParams(dimension_semantics=sem, vmem_limit_bytes=VMEM_LIMIT)


def _ffn_kernel(x_ref, g_ref, wg_ref, wu_ref, wd_ref, fg_ref, o_ref, h_ref, acc_ref, act_ref, *, final_norm):
    j = pl.program_id(1)

    @pl.when(j == 0)
    def _():
        h_ref[...] = _rms(x_ref[...], g_ref[...]).astype(h_ref.dtype)
        acc_ref[...] = jnp.zeros_like(acc_ref)

    h = h_ref[...]
    tf = act_ref.shape[1]
    for c0 in range(0, tf, FFN_CHUNK):
        c1 = min(c0 + FFN_CHUNK, tf)
        gate = jnp.dot(h, wg_ref[:, c0:c1], preferred_element_type=_F32)
        up = jnp.dot(h, wu_ref[:, c0:c1], preferred_element_type=_F32)
        act_ref[:, c0:c1] = (gate * jax.nn.sigmoid(gate) * up).astype(act_ref.dtype)
    acc_ref[...] += jnp.dot(act_ref[...], wd_ref[...], preferred_element_type=_F32)

    @pl.when(j == pl.num_programs(1) - 1)
    def _():
        y = x_ref[...] + 0.5 * acc_ref[...]
        if final_norm:
            y = _rms(y, fg_ref[...])
        o_ref[...] = y


def _ffn(x2d, g, wg, wu, wd, fg, final_norm):
    rows = x2d.shape[0]
    tm = min(1024, rows)
    tf = D_FF // 2
    grid = (rows // tm, D_FF // tf)
    return pl.pallas_call(
        functools.partial(_ffn_kernel, final_norm=final_norm),
        grid=grid,
        in_specs=[
            pl.BlockSpec((tm, D_MODEL), lambda i, j: (i, 0)),
            pl.BlockSpec((1, D_MODEL), lambda i, j: (0, 0)),
            pl.BlockSpec((D_MODEL, tf), lambda i, j: (0, j)),
            pl.BlockSpec((D_MODEL, tf), lambda i, j: (0, j)),
            pl.BlockSpec((tf, D_MODEL), lambda i, j: (j, 0)),
            pl.BlockSpec((1, D_MODEL), lambda i, j: (0, 0)),
        ],
        out_specs=pl.BlockSpec((tm, D_MODEL), lambda i, j: (i, 0)),
        out_shape=jax.ShapeDtypeStruct((rows, D_MODEL), _F32),
        scratch_shapes=[pltpu.VMEM((tm, D_MODEL), _MXU), pltpu.VMEM((tm, D_MODEL), _F32), pltpu.VMEM((tm, tf), _MXU)],
        compiler_params=_params("parallel", "arbitrary"),
        name="ffn",
    )(x2d, g, wg, wu, wd, fg)


def _proj_kernel(x_ref, g_ref, w_ref, wt_ref, qn_ref, wuq_ref, wuqs_ref, kvn_ref, wuk_ref, wuv_ref, gb_ref,
                 cq_ref, sq_ref, ck_ref, sk_ref, c64_ref, s64_ref,
                 mq_ref, mk_ref, mv_ref, nq_ref, nkc_ref, nvc_ref, nks_ref, nvs_ref, nkw_ref, nvw_ref,
                 gate_ref, sbq_ref, sbk_ref, sbv_ref):
    hn = _rms(x_ref[0], g_ref[...]).astype(_MXU)

    def proj(h0, h1):
        return jnp.dot(hn, w_ref[:, h0 * HEAD_DIM:h1 * HEAD_DIM], preferred_element_type=_F32)

    def slot(p, s):
        return p[:, s * LANES:(s + 1) * LANES]

    def head(p, i):
        return p[:, i * HEAD_DIM:(i + 1) * HEAD_DIM]

    p = proj(0, _H_NQ)
    cq = _rms(p[:, :MLA_Q_LORA], qn_ref[...])
    ckv = _rms(slot(p, _S_CKV), kvn_ref[...])
    q = _dot(cq, wuq_ref[...])
    q_partner = _dot(cq, wuqs_ref[...])
    kpe = slot(p, _S_KR) * ck_ref[...] + slot(p, _S_KRS) * sk_ref[...]
    kn = _dot(ckv, wuk_ref[...])
    for h in range(MLA_HEADS):
        mq_ref[0, h] = (slot(q, h) * cq_ref[...] + slot(q_partner, h) * sq_ref[...]).astype(mq_ref.dtype)
        mk_ref[0, h] = (slot(kn, h) + kpe).astype(mk_ref.dtype)
        mv_ref[0, h, 0] = _ones_row_pad(_dot_nt(wuv_ref[h], ckv)).astype(mv_ref.dtype)

    c64 = c64_ref[...]
    s64 = s64_ref[...]
    scale = HEAD_DIM ** -0.5

    p = proj(_H_NQ, _H_KC)
    for h in range(NSA_HEADS):
        nq_ref[0, h] = ((head(p, h) * c64 + head(p, NSA_HEADS + h) * s64) * (scale * LOG2_E)).astype(nq_ref.dtype)

    p = proj(_H_KC, _H_SBQ)
    base = _H_KC
    for k_ref, hk, hks in ((nkc_ref, _H_KC, _H_KCS), (nks_ref, _H_KS, _H_KSS), (nkw_ref, _H_KW, _H_KWS)):
        for g in range(NSA_KV_HEADS):
            k_ref[0, g] = (head(p, hk - base + g) * c64 + head(p, hks - base + g) * s64).astype(k_ref.dtype)
    for g in range(NSA_KV_HEADS):
        nvc_ref[0, g] = head(p, _H_VC - base + g).astype(nvc_ref.dtype)

    pt = _dot_nt(wt_ref[...], hn)
    for g in range(NSA_KV_HEADS):
        lo = g * HEAD_DIM
        nvs_ref[0, g, 0] = _ones_row_pad(pt[_T_VS + lo:_T_VS + lo + HEAD_DIM]).astype(nvs_ref.dtype)
        nvw_ref[0, g, 0] = _ones_row_pad(pt[_T_VW + lo:_T_VW + lo + HEAD_DIM]).astype(nvw_ref.dtype)
    gate_ref[0] = jax.nn.sigmoid(pt[_T_GATE:_T_GATE + GATE_ROWS] + gb_ref[...])

    p = proj(_H_SBQ, _N_HEAD_COLS)
    for h in range(SB_HEADS):
        sbq_ref[0, h] = (head(p, h) * (scale * LOG2_E)).astype(sbq_ref.dtype)
        sbk_ref[0, h] = head(p, SB_HEADS + h).astype(sbk_ref.dtype)
        sbv_ref[0, h, 0] = pt[_T_SBV + h * HEAD_DIM:_T_SBV + (h + 1) * HEAD_DIM].astype(sbv_ref.dtype)


def _proj(x, g, w_ext, w_t, qn, wuq, wuqs, kvn, wuk, wuv, gb, tabs):
    b, s, _ = x.shape
    ts = min(TOKEN_CHUNK, s)
    cq, sq, ck, sk, c64, s64 = tabs

    def full(a):
        return pl.BlockSpec(a.shape, lambda bi, i: (0,) * a.ndim)

    def tab(a):
        return pl.BlockSpec((ts, a.shape[1]), lambda bi, i: (i, 0))

    def heads(n, d):
        return (pl.BlockSpec((1, n, ts, d), lambda bi, i: (bi, 0, i, 0)),
                jax.ShapeDtypeStruct((b, n, s, d), _MXU))

    def values_t(n):
        return (pl.BlockSpec((1, n, 1, HEAD_DIM + ONES_PAD, ts), lambda bi, i: (bi, 0, i, 0, 0)),
                jax.ShapeDtypeStruct((b, n, s // ts, HEAD_DIM + ONES_PAD, ts), _MXU))

    kv = lambda: heads(NSA_KV_HEADS, HEAD_DIM)
    outs = [heads(MLA_HEADS, LANES), heads(MLA_HEADS, LANES), values_t(MLA_HEADS), heads(NSA_HEADS, HEAD_DIM),
            kv(), kv(), kv(), values_t(NSA_KV_HEADS),
            kv(), values_t(NSA_KV_HEADS),
            (pl.BlockSpec((1, GATE_ROWS, ts), lambda bi, i: (bi, 0, i)), jax.ShapeDtypeStruct((b, GATE_ROWS, s), _F32)),
            heads(SB_HEADS, HEAD_DIM), heads(SB_HEADS, HEAD_DIM),
            (pl.BlockSpec((1, SB_HEADS, 1, HEAD_DIM, ts), lambda bi, i: (bi, 0, i, 0, 0)),
             jax.ShapeDtypeStruct((b, SB_HEADS, s // ts, HEAD_DIM, ts), _MXU))]
    return pl.pallas_call(
        _proj_kernel,
        grid=(b, s // ts),
        in_specs=[pl.BlockSpec((1, ts, D_MODEL), lambda bi, i: (bi, i, 0)), full(g), full(w_ext), full(w_t), full(qn),
                  full(wuq), full(wuqs), full(kvn), full(wuk), full(wuv), full(gb),
                  tab(cq), tab(sq), tab(ck), tab(sk), tab(c64), tab(s64)],
        out_specs=[o[0] for o in outs],
        out_shape=[o[1] for o in outs],
        compiler_params=_params("parallel", "parallel"),
        name="proj",
    )(x, g, w_ext, w_t, qn, wuq, wuqs, kvn, wuk, wuv, gb, cq, sq, ck, sk, c64, s64)


def _ones_row_pad(vt):
    first = lax.broadcasted_iota(jnp.int32, (ONES_PAD, vt.shape[1]), 0) == 0
    return jnp.concatenate([vt, jnp.where(first, 1.0, 0.0).astype(vt.dtype)], axis=0)


def _softmax_step_t(carry, st, vt_chunks, bias=None):
    m, acc = carry
    keys, lanes = st.shape
    if bias is None:
        m_new = jnp.maximum(m, jnp.max(st, axis=0, keepdims=True))
        pt = jnp.exp2(st - m_new)
    else:
        blocks = bias.shape[0]
        s3 = st.reshape(blocks, keys // blocks, lanes)
        block_max = jnp.max(s3.reshape(blocks, keys // blocks // SUBLANES, SUBLANES, lanes), axis=1)
        tile_max = jnp.max(block_max + bias[:, None, :], axis=0)
        m_new = jnp.maximum(m, jnp.max(tile_max, axis=0, keepdims=True))
        pt = jnp.exp2(s3 + (bias - m_new)[:, None, :]).reshape(keys, lanes)
    alpha = jnp.exp2(m - m_new)
    pt = pt.astype(_MXU)
    n = keys // len(vt_chunks)
    pv = sum(jnp.dot(vt, pt[c * n:(c + 1) * n], preferred_element_type=_F32) for c, vt in enumerate(vt_chunks))
    return m_new, alpha * acc + pv


def _softmax_init_t(d, cols):
    return (jnp.full((1, cols), M_FLOOR, _F32), jnp.zeros((d + ONES_PAD, cols), _F32))


def _softmax_finish_t(carry, d):
    _, acc = carry
    return acc[:d] * (1.0 / acc[d:d + 1])


def _two_chain_sweep(n_full, qk, soft, init):
    qk(1, 0)

    def body(j, carry):
        c0, c1 = carry
        qk(0, j)
        c1 = soft(1, j, c1, False)
        qk(1, j + 1)
        c0 = soft(0, j, c0, False)
        return c0, c1

    def unrolled(i, carry):
        for u in range(SWEEP_UNROLL):
            carry = body(SWEEP_UNROLL * i + u, carry)
        return carry

    trips = n_full // SWEEP_UNROLL
    carry = lax.fori_loop(0, trips, unrolled, init)
    c0, c1 = lax.fori_loop(SWEEP_UNROLL * trips, n_full, body, carry)
    qk(0, n_full)
    c1 = soft(1, n_full, c1, True)
    c0 = soft(0, n_full, c0, True)
    return c0, c1


def _mla_kernel(q_ref, k_ref, vt_ref, o_ref, s0_ref, s1_ref, *, t, nsub):
    qi = pl.program_id(2)
    s_refs = (s0_ref, s1_ref)

    def qk(hh, j):
        off = pl.multiple_of(j * t, t)
        s_refs[hh][...] = _dot_nt(k_ref[0, hh, pl.ds(off, t), :], q_ref[0, hh])

    def soft(hh, j, carry, diag):
        st = s_refs[hh][...]
        if diag:
            key = lax.broadcasted_iota(jnp.int32, (t, t), 0)
            qry = lax.broadcasted_iota(jnp.int32, (t, t), 1)
            st = jnp.where(key <= qry, st, NEG_INF)
        return _softmax_step_t(carry, st, [vt_ref[0, hh, j * nsub + c] for c in range(nsub)])

    carry = _two_chain_sweep(qi, qk, soft, tuple(_softmax_init_t(MLA_V, t) for _ in range(2)))
    ot = jnp.concatenate([_softmax_finish_t(c, MLA_V) for c in carry], axis=0)
    o_ref[0] = ot.T.astype(o_ref.dtype)


def _mla_attention(q, k, vt):
    b, h, s, _ = q.shape
    tv = vt.shape[-1]
    dv = vt.shape[-2]
    t = min(512, s)
    assert h % 2 == 0 and 2 * MLA_V == LANES and t % tv == 0
    return pl.pallas_call(
        functools.partial(_mla_kernel, t=t, nsub=t // tv),
        grid=(b, h // 2, s // t),
        in_specs=[pl.BlockSpec((1, 2, t, LANES), lambda bi, hi, i: (bi, hi, i, 0)),
                  pl.BlockSpec((1, 2, s, LANES), lambda bi, hi, i: (bi, hi, 0, 0)),
                  pl.BlockSpec((1, 2, s // tv, dv, tv), lambda bi, hi, i: (bi, hi, 0, 0, 0))],
        out_specs=pl.BlockSpec((1, t, LANES), lambda bi, hi, i: (bi, i, hi)),
        out_shape=jax.ShapeDtypeStruct((b, s, h * MLA_V), _MXU),
        scratch_shapes=[pltpu.VMEM((t, t), _F32), pltpu.VMEM((t, t), _F32)],
        compiler_params=_params("parallel", "parallel", "arbitrary"),
        name="mla_attn",
    )(q, k, vt)


def _sb_kernel(q_ref, k_ref, vt_ref, u_ref, o_ref, z_ref, lb_ref, hi_ref, lo_ref, a_ref, *, t):
    qi = pl.program_id(2)
    u = u_ref[...]

    def step(j, carry, diag):
        off = pl.multiple_of(j * t, t)
        if diag:
            key = lax.broadcasted_iota(jnp.int32, (t, t), 0)
            qry = lax.broadcasted_iota(jnp.int32, (t, t), 1)
            strict = key < qry
        for hh in range(2):
            z_ref[hh] = _dot_nt(k_ref[0, hh, pl.ds(off, t), :], q_ref[0, hh])
        first_rem = []
        for hh in range(2):
            z = z_ref[hh]
            log_beta = jnp.minimum(z, 0.0) - jnp.log2(1.0 + jnp.exp2(-jnp.abs(z)))
            log_rem = log_beta - z
            if diag:
                log_rem = jnp.where(strict, log_rem, 0.0)
            hi = log_rem.astype(_MXU)
            lb_ref[hh] = log_beta
            hi_ref[hh] = hi
            lo_ref[hh] = (log_rem - hi.astype(_F32)).astype(_MXU)
            first_rem.append(log_rem[0:1, :])
        out = []
        for hh in range(2):
            rem, acc = carry[hh]
            suffix = (jnp.dot(u, hi_ref[hh], preferred_element_type=_F32)
                      + jnp.dot(u, lo_ref[hh], preferred_element_type=_F32))
            a = jnp.exp2(lb_ref[hh] + suffix + rem)
            if diag:
                a = jnp.where(strict, a, 0.0)
            a_ref[hh] = a.astype(_MXU)
            out.append((rem + suffix[0:1, :] + first_rem[hh], acc))
        alive = jnp.max(jnp.maximum(out[0][0], out[1][0])) > F32_EXP2_ZERO
        out = tuple((rem, acc + jnp.dot(vt_ref[0, hh, j], a_ref[hh], preferred_element_type=_F32))
                    for hh, (rem, acc) in enumerate(out))
        return alive, out

    init = tuple((jnp.zeros((1, t), _F32), jnp.zeros((HEAD_DIM, t), _F32)) for _ in range(2))
    alive, carry = step(qi, init, True)

    def earlier(c):
        return (c[0] - 1,) + step(c[0], c[2], False)

    _, _, carry = lax.while_loop(lambda c: jnp.logical_and(c[0] >= 0, c[1]), earlier, (qi - 1, alive, carry))
    o_ref[0] = jnp.concatenate([acc for _, acc in carry], axis=0).T.astype(o_ref.dtype)


def _sb_attention(q, k, vt):
    b, h, s, d = q.shape
    t = vt.shape[-1]
    assert h % 2 == 0 and 2 * d == LANES and s % t == 0
    idx = np.arange(t)
    u = jnp.asarray(idx[None, :] > idx[:, None], _MXU)
    return pl.pallas_call(
        functools.partial(_sb_kernel, t=t),
        grid=(b, h // 2, s // t),
        in_specs=[pl.BlockSpec((1, 2, t, d), lambda bi, hi, i: (bi, hi, i, 0)),
                  pl.BlockSpec((1, 2, s, d), lambda bi, hi, i: (bi, hi, 0, 0)),
                  pl.BlockSpec((1, 2, s // t, d, t), lambda bi, hi, i: (bi, hi, 0, 0, 0)),
                  pl.BlockSpec((t, t), lambda bi, hi, i: (0, 0))],
        out_specs=pl.BlockSpec((1, t, LANES), lambda bi, hi, i: (bi, i, hi)),
        out_shape=jax.ShapeDtypeStruct((b, s, h * d), _MXU),
        scratch_shapes=[pltpu.VMEM((2, t, t), _F32), pltpu.VMEM((2, t, t), _F32), pltpu.VMEM((2, t, t), _MXU),
                        pltpu.VMEM((2, t, t), _MXU), pltpu.VMEM((2, t, t), _MXU)],
        compiler_params=_params("parallel", "parallel", "arbitrary"),
        name="sb_attn",
    )(q, k, vt, u)


def _compress_kernel(xk_ref, xv_ref, w1k_ref, w2k_ref, pk_ref, w1v_ref, w2v_ref, pv_ref, ok_ref, ov_ref):
    def hidden(x_ref, w1_ref, p_ref):
        x = x_ref[0, 0]
        n = x.shape[0]
        first = jnp.dot(x, w1_ref[0], preferred_element_type=_F32)
        second = jnp.dot(x, w1_ref[1], preferred_element_type=_F32)
        pos = _dot(p_ref[0], w1_ref[0]) + _dot(p_ref[1], w1_ref[1])
        hid = first + pltpu.roll(second, n - 1, 0) + pos[0:1]
        return 0.5 * hid * (1.0 + jnp.tanh(math.sqrt(2.0 / math.pi) * (hid + 0.044715 * hid * hid * hid)))

    ok_ref[0, 0] = _dot(hidden(xk_ref, w1k_ref, pk_ref), w2k_ref[...]).astype(ok_ref.dtype)
    ov_ref[0, 0] = _dot_nt(w2v_ref[...], hidden(xv_ref, w1v_ref, pv_ref)).astype(ov_ref.dtype)


def _compress(xk, xv, w1k, w2k, pk, w1v, w2v, pv):
    b, g, s, d = xk.shape
    n = s // CMP_STRIDE
    xk = xk.reshape(b, g, n, CMP_STRIDE * d)
    xv = xv.reshape(b, g, n, CMP_STRIDE * d)

    def full(a):
        return pl.BlockSpec(a.shape, lambda bi, gi: (0,) * a.ndim)

    xspec = pl.BlockSpec((1, 1, n, CMP_STRIDE * d), lambda bi, gi: (bi, gi, 0, 0))
    return pl.pallas_call(
        _compress_kernel,
        grid=(b, g),
        in_specs=[xspec, xspec, full(w1k), full(w2k), full(pk), full(w1v), full(w2v), full(pv)],
        out_specs=[pl.BlockSpec((1, 1, n, d), lambda bi, gi: (bi, gi, 0, 0)),
                   pl.BlockSpec((1, 1, d, n), lambda bi, gi: (bi, gi, 0, 0))],
        out_shape=[jax.ShapeDtypeStruct((b, g, n, d), _MXU), jax.ShapeDtypeStruct((b, g, d, n), _MXU)],
        compiler_params=_params("parallel", "parallel"),
        name="nsa_compress",
    )(xk, xv, w1k, w2k, pk, w1v, w2v, pv)


def _group_queries(q_ref, g, tq):
    return q_ref[0, g * NSA_GROUP:(g + 1) * NSA_GROUP].reshape(NSA_GROUP * tq, q_ref.shape[-1])


def _gated_heads(ot, gt_ref, g, branch, tq):
    out = []
    for r in range(NSA_GROUP):
        row = NSA_BRANCHES * (g * NSA_GROUP + r) + branch
        out.append(ot[:, r * tq:(r + 1) * tq] * gt_ref[0, row:row + 1, :])
    return out


def _cmp_kernel(q_ref, kc_ref, vct_ref, ov_ref, gt_ref, o_ref, bias_ref, s0_ref, s1_ref, *, tq, n_top):
    q0 = pl.program_id(1) * tq
    ncp = kc_ref.shape[2]
    ns = ov_ref.shape[0]
    lanes = NSA_GROUP * tq
    s_refs = (s0_ref, s1_ref)
    for g in range(NSA_KV_HEADS):
        s_refs[g][...] = _dot_nt(kc_ref[0, g], _group_queries(q_ref, g, tq))
    qpos = q0 + (lax.broadcasted_iota(jnp.int32, (1, lanes), 1) & (tq - 1))
    cmp_end = lax.broadcasted_iota(jnp.int32, (ncp, 1), 0) * CMP_STRIDE + (CMP_LEN - 1)
    visible = cmp_end <= qpos
    cur = jnp.right_shift(q0 + lax.broadcasted_iota(jnp.int32, (1, tq), 1), int(math.log2(SEL_LEN)))
    blk = lax.broadcasted_iota(jnp.int32, (ns, 1), 0)
    forced = (blk == 0) | (blk == cur) | (blk == cur - 1)
    future = blk > cur
    blk_f = blk.astype(_F32)
    heads = []
    scores = []
    for g in range(NSA_KV_HEADS):
        st = jnp.where(visible, s_refs[g][...], NEG_INF)
        e = jnp.exp2(st - jnp.max(st, axis=0, keepdims=True))
        inv = jnp.where(qpos >= CMP_LEN - 1, 1.0 / jnp.sum(e, axis=0, keepdims=True), 0.0)
        pt = e * inv
        heads += _gated_heads(_dot(vct_ref[0, g], pt), gt_ref, g, 0, tq)
        p_sum = sum(pt[:, r * tq:(r + 1) * tq] for r in range(NSA_GROUP))
        score = _dot_split_rhs(ov_ref[...], p_sum)
        scores.append(jnp.where(forced, FORCE_SCORE, jnp.where(future, -1.0, score)))
    o_ref[0] = jnp.concatenate(heads, axis=0).T
    bias = [jnp.full((ns, tq), NEG_INF, _F32) for _ in range(NSA_KV_HEADS)]
    for _ in range(n_top):
        for g in range(NSA_KV_HEADS):
            top = jnp.max(scores[g], axis=0, keepdims=True)
            first = jnp.min(jnp.where(scores[g] == top, blk_f, float(ns)), axis=0, keepdims=True)
            pick = blk_f == first
            bias[g] = jnp.where(pick, 0.0, bias[g])
            scores[g] = jnp.where(pick, PICKED, scores[g])
    for g in range(NSA_KV_HEADS):
        bias_ref[0, g] = bias[g]


def _cmp_select(q, kc, vct, gates_t):
    b, h, s, d = q.shape
    g = kc.shape[1]
    ncp = kc.shape[2]
    ns = s // SEL_LEN
    n_top = min(SEL_TOPK, ns)
    tq = min(256, s)
    assert tq & (tq - 1) == 0 and g == 2
    c0 = np.arange(ncp)[:, None] * CMP_STRIDE
    n0 = np.arange(ns)[None, :] * SEL_LEN
    overlap = jnp.asarray(((c0 < n0 + SEL_LEN) & (c0 + CMP_LEN > n0)).T, _MXU)
    return pl.pallas_call(
        functools.partial(_cmp_kernel, tq=tq, n_top=n_top),
        grid=(b, s // tq),
        in_specs=[pl.BlockSpec((1, h, tq, d), lambda bi, i: (bi, 0, i, 0)),
                  pl.BlockSpec((1, g, ncp, d), lambda bi, i: (bi, 0, 0, 0)),
                  pl.BlockSpec((1, g, d, ncp), lambda bi, i: (bi, 0, 0, 0)),
                  pl.BlockSpec((ns, ncp), lambda bi, i: (0, 0)),
                  pl.BlockSpec((1, GATE_ROWS, tq), lambda bi, i: (bi, 0, i))],
        out_specs=[pl.BlockSpec((1, tq, h * d), lambda bi, i: (bi, i, 0)),
                   pl.BlockSpec((1, g, ns, tq), lambda bi, i: (bi, 0, 0, i))],
        out_shape=[jax.ShapeDtypeStruct((b, s, h * d), _F32), jax.ShapeDtypeStruct((b, g, ns, s), _F32)],
        scratch_shapes=[pltpu.VMEM((ncp, NSA_GROUP * tq), _F32) for _ in range(g)],
        compiler_params=_params("parallel", "arbitrary"),
        name="nsa_cmp_select",
    )(q, kc, vct, overlap, gates_t)


def _sel_kernel(q_ref, k_ref, vt_ref, bias_ref, gt_ref, o_ref, s0_ref, s1_ref, *, tq, tk, nsub):
    q0 = pl.program_id(1) * tq
    last = (q0 + tq - 1) // tk
    lanes = NSA_GROUP * tq
    blocks = tk // SEL_LEN
    s_refs = (s0_ref, s1_ref)

    def qk(g, j):
        off = pl.multiple_of(j * tk, tk)
        s_refs[g][...] = _dot_nt(k_ref[0, g, pl.ds(off, tk), :], _group_queries(q_ref, g, tq))

    def soft(g, j, carry, causal):
        bias = bias_ref[0, g, pl.ds(pl.multiple_of(j * blocks, blocks), blocks), :]
        bias = jnp.concatenate([bias] * NSA_GROUP, axis=1)
        st = s_refs[g][...]
        if causal:
            key = j * tk + lax.broadcasted_iota(jnp.int32, (tk, lanes), 0)
            qry = q0 + (lax.broadcasted_iota(jnp.int32, (tk, lanes), 1) & (tq - 1))
            st = jnp.where(key <= qry, st, NEG_INF)
        return _softmax_step_t(carry, st, [vt_ref[0, g, j * nsub + c] for c in range(nsub)], bias)

    init = tuple(_softmax_init_t(HEAD_DIM, lanes) for _ in range(NSA_KV_HEADS))
    carry = _two_chain_sweep(last, qk, soft, init)
    heads = []
    for g in range(NSA_KV_HEADS):
        heads += _gated_heads(_softmax_finish_t(carry[g], HEAD_DIM), gt_ref, g, 1, tq)
    o_ref[0] = jnp.concatenate(heads, axis=0).T


def _sel_attention(q, k, vt, bias, gates_t):
    b, h, s, d = q.shape
    g = k.shape[1]
    ns = bias.shape[2]
    tv = vt.shape[-1]
    tq = min(256, s)
    tk = min(512, s)
    assert tq & (tq - 1) == 0 and s % tk == 0 and tk % tv == 0 and tk % SEL_LEN == 0 and g == 2
    return pl.pallas_call(
        functools.partial(_sel_kernel, tq=tq, tk=tk, nsub=tk // tv),
        grid=(b, s // tq),
        in_specs=[pl.BlockSpec((1, h, tq, d), lambda bi, i: (bi, 0, i, 0)),
                  pl.BlockSpec((1, g, s, d), lambda bi, i: (bi, 0, 0, 0)),
                  pl.BlockSpec((1, g) + vt.shape[2:], lambda bi, i: (bi, 0, 0, 0, 0)),
                  pl.BlockSpec((1, g, ns, tq), lambda bi, i: (bi, 0, 0, i)),
                  pl.BlockSpec((1, GATE_ROWS, tq), lambda bi, i: (bi, 0, i))],
        out_specs=pl.BlockSpec((1, tq, h * d), lambda bi, i: (bi, i, 0)),
        out_shape=jax.ShapeDtypeStruct((b, s, h * d), _F32),
        scratch_shapes=[pltpu.VMEM((tk, NSA_GROUP * tq), _F32) for _ in range(g)],
        compiler_params=_params("parallel", "arbitrary"),
        name="nsa_selected",
    )(q, k, vt, bias, gates_t)


def _win_kernel(q_ref, k_ref, vt_ref, gt_ref, o_ref, s0_ref, s1_ref, *, tq, span, tv):
    q0 = pl.program_id(1) * tq
    start = pl.multiple_of(jnp.maximum(q0 - WINDOW, 0), tq)
    first_chunk = start // tv
    lanes = NSA_GROUP * tq
    s_refs = (s0_ref, s1_ref)
    for g in range(NSA_KV_HEADS):
        s_refs[g][...] = _dot_nt(k_ref[0, g, pl.ds(start, span), :], _group_queries(q_ref, g, tq))
    key = start + lax.broadcasted_iota(jnp.int32, (span, lanes), 0)
    qry = q0 + (lax.broadcasted_iota(jnp.int32, (span, lanes), 1) & (tq - 1))
    heads = []
    for g in range(NSA_KV_HEADS):
        st = jnp.where(key <= qry, s_refs[g][...], NEG_INF)
        st = jnp.where(key > qry - WINDOW, st, NEG_INF)
        carry = _softmax_step_t(_softmax_init_t(HEAD_DIM, lanes), st,
                                [vt_ref[0, g, first_chunk + c] for c in range(span // tv)])
        heads += _gated_heads(_softmax_finish_t(carry, HEAD_DIM), gt_ref, g, 2, tq)
    o_ref[0] = jnp.concatenate(heads, axis=0).T


def _win_attention(q, k, vt, gates_t):
    b, h, s, d = q.shape
    g = k.shape[1]
    tv = vt.shape[-1]
    tq = min(256, s)
    span = WINDOW + tq
    assert tq & (tq - 1) == 0 and s >= span and tq % tv == 0 and WINDOW % tv == 0 and g == 2
    return pl.pallas_call(
        functools.partial(_win_kernel, tq=tq, span=span, tv=tv),
        grid=(b, s // tq),
        in_specs=[pl.BlockSpec((1, h, tq, d), lambda bi, i: (bi, 0, i, 0)),
                  pl.BlockSpec((1, g, s, d), lambda bi, i: (bi, 0, 0, 0)),
                  pl.BlockSpec((1, g) + vt.shape[2:], lambda bi, i: (bi, 0, 0, 0, 0)),
                  pl.BlockSpec((1, GATE_ROWS, tq), lambda bi, i: (bi, 0, i))],
        out_specs=pl.BlockSpec((1, tq, h * d), lambda bi, i: (bi, i, 0)),
        out_shape=jax.ShapeDtypeStruct((b, s, h * d), _F32),
        scratch_shapes=[pltpu.VMEM((span, NSA_GROUP * tq), _F32) for _ in range(g)],
        compiler_params=_params("parallel", "arbitrary"),
        name="nsa_window",
    )(q, k, vt, gates_t)


def _out_kernel(x_ref, mla_ref, cmp_ref, sel_ref, win_ref, sb_ref, w_ref, o_ref):
    def w_rows(first_head, n_heads):
        return w_ref[first_head * HEAD_DIM:(first_head + n_heads) * HEAD_DIM, :]

    acc = x_ref[0] + jnp.dot(mla_ref[0], w_rows(0, MLA_HEADS), preferred_element_type=_F32)
    nsa = cmp_ref[0] + sel_ref[0] + win_ref[0]
    acc = acc + _dot(nsa, w_rows(MLA_HEADS, NSA_HEADS))
    o_ref[0] = acc + jnp.dot(sb_ref[0], w_rows(MLA_HEADS + NSA_HEADS, SB_HEADS), preferred_element_type=_F32)


def _out_proj(x, o_mla, o_cmp, o_sel, o_win, o_sb, w_heads):
    b, s, _ = x.shape
    ts = min(512, s)

    def rows(a):
        return pl.BlockSpec((1, ts, a.shape[2]), lambda bi, i: (bi, i, 0))

    xspec = pl.BlockSpec((1, ts, D_MODEL), lambda bi, i: (bi, i, 0))
    return pl.pallas_call(
        _out_kernel,
        grid=(b, s // ts),
        in_specs=[xspec, rows(o_mla), rows(o_cmp), rows(o_sel), rows(o_win), rows(o_sb),
                  pl.BlockSpec(w_heads.shape, lambda bi, i: (0, 0))],
        out_specs=xspec,
        out_shape=jax.ShapeDtypeStruct(x.shape, _F32),
        compiler_params=_params("parallel", "parallel"),
        name="out_proj",
    )(x, o_mla, o_cmp, o_sel, o_win, o_sb, w_heads)


def _gather_cols(w, idx):
    idx = np.asarray(idx)
    cols = jnp.take(w, jnp.asarray(np.maximum(idx, 0)), axis=1)
    return jnp.where(jnp.asarray(idx >= 0)[None, :], cols, 0.0).astype(_MXU)


def _swap_halves(rot):
    return (np.arange(rot) + rot // 2) % rot


def _w_in_index():
    idx = np.full((_N_HEAD_COLS * HEAD_DIM,), -1, np.int64)

    def put(col, src):
        src = np.asarray(src)
        idx[col:col + len(src)] = src

    def put_head(pos, src):
        put(pos * HEAD_DIM, src)

    put(_S_CQ * LANES, _O_CQ + np.arange(MLA_Q_LORA))
    put(_S_CKV * LANES, _O_CKV + np.arange(MLA_KV_LORA))
    put(_S_KR * LANES + MLA_NOPE, _O_KR + np.arange(MLA_ROPE))
    put(_S_KRS * LANES + MLA_NOPE, _O_KR + _swap_halves(MLA_ROPE))
    for h in range(NSA_HEADS):
        put_head(_H_NQ + h, _O_NQ + h * HEAD_DIM + np.arange(HEAD_DIM))
        put_head(_H_NQS + h, _O_NQ + h * HEAD_DIM + _swap_halves(PARTIAL_ROT))
    for hk, hks, ok in ((_H_KC, _H_KCS, _O_NKC), (_H_KS, _H_KSS, _O_NKS), (_H_KW, _H_KWS, _O_NKW)):
        for g in range(NSA_KV_HEADS):
            put_head(hk + g, ok + g * HEAD_DIM + np.arange(HEAD_DIM))
            put_head(hks + g, ok + g * HEAD_DIM + _swap_halves(PARTIAL_ROT))
    for g in range(NSA_KV_HEADS):
        put_head(_H_VC + g, _O_NVC + g * HEAD_DIM + np.arange(HEAD_DIM))
    for h in range(SB_HEADS):
        put_head(_H_SBQ + h, _O_SBQ + h * HEAD_DIM + np.arange(HEAD_DIM))
        put_head(_H_SBK + h, _O_SBK + h * HEAD_DIM + np.arange(HEAD_DIM))
    return idx


def _mla_up_index():
    qd = MLA_NOPE + MLA_ROPE
    kd = MLA_NOPE + MLA_V
    uq = np.full((MLA_HEADS * LANES,), -1, np.int64)
    uqs = uq.copy()
    uk = uq.copy()
    for h in range(MLA_HEADS):
        uq[h * LANES:h * LANES + qd] = h * qd + np.arange(qd)
        uqs[h * LANES + MLA_NOPE:h * LANES + qd] = h * qd + MLA_NOPE + _swap_halves(MLA_ROPE)
        uk[h * LANES:h * LANES + MLA_NOPE] = h * kd + np.arange(MLA_NOPE)
    return uq, uqs, uk


def _transposed_weights(w_in_l, gate_bias):
    width = NSA_KV_HEADS * HEAD_DIM
    gate_rows = jnp.pad(w_in_l[:, _O_GATE:_O_GATE + N_GATES], ((0, 0), (0, _T_SBV - _T_GATE - N_GATES)))
    rows = jnp.concatenate([w_in_l[:, _O_NVS:_O_NVS + width], w_in_l[:, _O_NVW:_O_NVW + width], gate_rows,
                            w_in_l[:, _O_SBV:_O_SBV + SB_HEADS * HEAD_DIM]], axis=1).T.astype(_MXU)
    bias = jnp.pad(gate_bias, (0, GATE_ROWS - N_GATES)).reshape(GATE_ROWS, 1)
    return rows, bias


def _rope_tables(s):
    pos = jnp.arange(s, dtype=_F32)

    def cs(rot):
        half = rot // 2
        inv_freq = ROPE_THETA ** (-jnp.arange(half, dtype=_F32) / half)
        ang = pos[:, None] * inv_freq[None, :]
        c, sn = jnp.cos(ang), jnp.sin(ang)
        return jnp.concatenate([c, c], axis=1), jnp.concatenate([-sn, sn], axis=1)

    c, sn = cs(MLA_ROPE)
    ones = jnp.ones((s, MLA_NOPE), _F32)
    zeros = jnp.zeros((s, MLA_NOPE), _F32)
    pad = jnp.zeros((s, LANES - MLA_NOPE - MLA_ROPE), _F32)
    ck = jnp.concatenate([ones, c, pad], axis=1)
    sk = jnp.concatenate([zeros, sn, pad], axis=1)
    q_scale = (MLA_NOPE + MLA_ROPE) ** -0.5 * LOG2_E
    c, sn = cs(PARTIAL_ROT)
    c64 = jnp.concatenate([c, jnp.ones((s, HEAD_DIM - PARTIAL_ROT), _F32)], axis=1)
    s64 = jnp.concatenate([sn, jnp.zeros((s, HEAD_DIM - PARTIAL_ROT), _F32)], axis=1)
    return ck * q_scale, sk * q_scale, ck, sk, c64, s64


def kernel(x, ffn1_norm, ffn1_w_gate, ffn1_w_up, ffn1_w_down, mix_norm, w_in, mla_q_norm, mla_w_uq, mla_kv_norm,
           mla_w_ukv, nsa_gate_bias, nsa_cmp_pos_k, nsa_cmp_w1_k, nsa_cmp_w2_k, nsa_cmp_pos_v, nsa_cmp_w1_v,
           nsa_cmp_w2_v, w_out, ffn2_norm, ffn2_w_gate, ffn2_w_up, ffn2_w_down, final_norm):
    b, s, d = x.shape
    depth = w_in.shape[0]
    tabs = _rope_tables(s)
    in_idx = _w_in_index()
    uq_idx, uqs_idx, uk_idx = _mla_up_index()
    half = CMP_LEN * HEAD_DIM // 2
    fg = final_norm.reshape(1, d)

    def cmp_weights(w1, w2, pos, transpose_out):
        pos = jnp.broadcast_to(pos.reshape(2, 1, half), (2, 8, half)).astype(_MXU)
        w2 = w2.T if transpose_out else w2
        return w1.reshape(2, half, CMP_HIDDEN).astype(_MXU), w2.astype(_MXU), pos

    for l in range(depth):
        x2d = _ffn(x.reshape(b * s, d), ffn1_norm[l].reshape(1, d), ffn1_w_gate[l].astype(_MXU),
                   ffn1_w_up[l].astype(_MXU), ffn1_w_down[l].astype(_MXU), fg, False)
        x = x2d.reshape(b, s, d)
        w_t, gate_bias = _transposed_weights(w_in[l], nsa_gate_bias[l])
        (mq, mk, mvt, nq, nkc, nvc, nks, nvst, nkw, nvwt, gates_t, sbq, sbk, sbv) = _proj(
            x, mix_norm[l].reshape(1, d), _gather_cols(w_in[l], in_idx), w_t,
            mla_q_norm[l].reshape(1, -1), _gather_cols(mla_w_uq[l], uq_idx), _gather_cols(mla_w_uq[l], uqs_idx),
            mla_kv_norm[l].reshape(1, -1), _gather_cols(mla_w_ukv[l], uk_idx),
            mla_w_ukv[l].reshape(MLA_KV_LORA, MLA_HEADS, 2, MLA_V)[:, :, 1].transpose(1, 2, 0).astype(_MXU),
            gate_bias, tabs)
        o_mla = _mla_attention(mq, mk, mvt)
        kc, vct = _compress(nkc, nvc, *cmp_weights(nsa_cmp_w1_k[l], nsa_cmp_w2_k[l], nsa_cmp_pos_k[l], False),
                            *cmp_weights(nsa_cmp_w1_v[l], nsa_cmp_w2_v[l], nsa_cmp_pos_v[l], True))
        o_cmp, sel_bias = _cmp_select(nq, kc, vct, gates_t)
        o_sel = _sel_attention(nq, nks, nvst, sel_bias, gates_t)
        o_win = _win_attention(nq, nkw, nvwt, gates_t)
        o_sb = _sb_attention(sbq, sbk, sbv)
        x = _out_proj(x, o_mla, o_cmp, o_sel, o_win, o_sb, w_out[l].astype(_MXU))
        x2d = _ffn(x.reshape(b * s, d), ffn2_norm[l].reshape(1, d), ffn2_w_gate[l].astype(_MXU),
                   ffn2_w_up[l].astype(_MXU), ffn2_w_down[l].astype(_MXU), fg, l == depth - 1)
        x = x2d.reshape(b, s, d)
    return x
```

```python
import functools
import math

import numpy as np
import jax
import jax.numpy as jnp
from jax import lax
from jax.experimental import pallas as pl
from jax.experimental.pallas import tpu as pltpu

D_MODEL = 1024
HEAD_DIM = 64
MLA_HEADS = 6
MLA_NOPE = 64
MLA_ROPE = 32
MLA_V = 64
MLA_Q_LORA = 256
MLA_KV_LORA = 128
NSA_HEADS = 6
NSA_KV_HEADS = 2
NSA_GROUP = NSA_HEADS // NSA_KV_HEADS
NSA_BRANCHES = 3
CMP_LEN = 32
CMP_STRIDE = 16
CMP_HIDDEN = 128
SEL_LEN = 64
SEL_TOPK = 16
WINDOW = 512
SB_HEADS = 4
D_FF = 2816
ROPE_THETA = 500000.0
PARTIAL_ROT = HEAD_DIM // 4
EPS = 1e-6
NEG_INF = -1e30
M_FLOOR = 0.1 * NEG_INF
FORCE_SCORE = 1e4
PICKED = -3e38
F32_EXP2_ZERO = -151.0
LOG2_E = math.log2(math.e)
N_GATES = NSA_HEADS * NSA_BRANCHES

LANES = 128
FFN_CHUNK = 256
SWEEP_UNROLL = 4
TOKEN_CHUNK = 256
ONES_PAD = 16
VMEM_LIMIT = 56 * 1024 * 1024

_MXU = jnp.bfloat16
_F32 = jnp.float32

_IN_WIDTHS = (MLA_Q_LORA, MLA_KV_LORA, MLA_ROPE, NSA_HEADS * HEAD_DIM) + (NSA_KV_HEADS * HEAD_DIM,) * 6 + (
    N_GATES, SB_HEADS * HEAD_DIM, SB_HEADS * HEAD_DIM, SB_HEADS * HEAD_DIM)
_IN_OFF = np.concatenate([[0], np.cumsum(_IN_WIDTHS)])
(_O_CQ, _O_CKV, _O_KR, _O_NQ, _O_NKC, _O_NVC, _O_NKS, _O_NVS, _O_NKW, _O_NVW, _O_GATE, _O_SBQ, _O_SBK,
 _O_SBV) = [int(v) for v in _IN_OFF[:-1]]

_S_CQ, _S_CKV, _S_KR, _S_KRS = 0, 2, 3, 4
_H_NQ, _H_NQS = 10, 16
_H_KC, _H_KCS, _H_VC = 22, 24, 26
_H_KS, _H_KSS = 28, 30
_H_KW, _H_KWS = 32, 34
_H_SBQ, _H_SBK = 36, 40
_N_HEAD_COLS = 44
_T_VS, _T_VW, _T_GATE = 0, NSA_KV_HEADS * HEAD_DIM, 2 * NSA_KV_HEADS * HEAD_DIM
GATE_ROWS = 24
_T_SBV = _T_GATE + 32
_T_ROWS = _T_SBV + SB_HEADS * HEAD_DIM


def _dot(a, b):
    return jnp.dot(a.astype(_MXU), b.astype(_MXU), preferred_element_type=_F32)


def _dot_nt(a, b):
    return lax.dot_general(a.astype(_MXU), b.astype(_MXU), (((1,), (1,)), ((), ())),
                           preferred_element_type=_F32)


def _dot_split_rhs(a, b):
    hi = b.astype(_MXU)
    lo = (b - hi.astype(_F32)).astype(_MXU)
    return (jnp.dot(a, hi, preferred_element_type=_F32) + jnp.dot(a, lo, preferred_element_type=_F32))


def _rms(x, g):
    return x * lax.rsqrt(jnp.mean(x * x, axis=-1, keepdims=True) + EPS) * g


def _params(*sem):
    return pltpu.CompilerParams(dimension_semantics=sem, vmem_limit_bytes=VMEM_LIMIT)


def _ffn_kernel(x_ref, g_ref, wg_ref, wu_ref, wd_ref, fg_ref, o_ref, h_ref, acc_ref, act_ref, *, final_norm):
    j = pl.program_id(1)

    @pl.when(j == 0)
    def _():
        h_ref[...] = _rms(x_ref[...], g_ref[...]).astype(h_ref.dtype)
        acc_ref[...] = jnp.zeros_like(acc_ref)

    h = h_ref[...]
    tf = act_ref.shape[1]
    for c0 in range(0, tf, FFN_CHUNK):
        c1 = min(c0 + FFN_CHUNK, tf)
        gate = jnp.dot(h, wg_ref[:, c0:c1], preferred_element_type=_F32)
        up = jnp.dot(h, wu_ref[:, c0:c1], preferred_element_type=_F32)
        act_ref[:, c0:c1] = (gate * jax.nn.sigmoid(gate) * up).astype(act_ref.dtype)
    acc_ref[...] += jnp.dot(act_ref[...], wd_ref[...], preferred_element_type=_F32)

    @pl.when(j == pl.num_programs(1) - 1)
    def _():
        y = x_ref[...] + 0.5 * acc_ref[...]
        if final_norm:
            y = _rms(y, fg_ref[...])
        o_ref[...] = y


def _ffn(x2d, g, wg, wu, wd, fg, final_norm):
    rows = x2d.shape[0]
    tm = min(1024, rows)
    tf = D_FF // 2
    grid = (rows // tm, D_FF // tf)
    return pl.pallas_call(
        functools.partial(_ffn_kernel, final_norm=final_norm),
        grid=grid,
        in_specs=[
            pl.BlockSpec((tm, D_MODEL), lambda i, j: (i, 0)),
            pl.BlockSpec((1, D_MODEL), lambda i, j: (0, 0)),
            pl.BlockSpec((D_MODEL, tf), lambda i, j: (0, j)),
            pl.BlockSpec((D_MODEL, tf), lambda i, j: (0, j)),
            pl.BlockSpec((tf, D_MODEL), lambda i, j: (j, 0)),
            pl.BlockSpec((1, D_MODEL), lambda i, j: (0, 0)),
        ],
        out_specs=pl.BlockSpec((tm, D_MODEL), lambda i, j: (i, 0)),
        out_shape=jax.ShapeDtypeStruct((rows, D_MODEL), _F32),
        scratch_shapes=[pltpu.VMEM((tm, D_MODEL), _MXU), pltpu.VMEM((tm, D_MODEL), _F32), pltpu.VMEM((tm, tf), _MXU)],
        compiler_params=_params("parallel", "arbitrary"),
        name="ffn",
    )(x2d, g, wg, wu, wd, fg)


def _proj_kernel(x_ref, g_ref, w_ref, wt_ref, qn_ref, wuq_ref, wuqs_ref, kvn_ref, wuk_ref, wuv_ref, gb_ref,
                 cq_ref, sq_ref, ck_ref, sk_ref, c64_ref, s64_ref, oh_ref,
                 mq_ref, mk_ref, mv_ref, nq_ref, nkc_ref, nvc_ref, nks_ref, nvs_ref, nkw_ref, nvw_ref,
                 gate_ref, sbq_ref, sbk_ref, sbv_ref):
    hn = _rms(x_ref[0], g_ref[...]).astype(_MXU)

    def proj(h0, h1):
        return jnp.dot(hn, w_ref[:, h0 * HEAD_DIM:h1 * HEAD_DIM], preferred_element_type=_F32)

    def slot(p, s):
        return p[:, s * LANES:(s + 1) * LANES]

    def head(p, i):
        return p[:, i * HEAD_DIM:(i + 1) * HEAD_DIM]

    p = proj(0, _H_NQ)
    cq = _rms(p[:, :MLA_Q_LORA], qn_ref[...])
    ckv = _rms(slot(p, _S_CKV), kvn_ref[...])
    q = _dot(cq, wuq_ref[...])
    q_partner = _dot(cq, wuqs_ref[...])
    kpe = slot(p, _S_KR) * ck_ref[...] + slot(p, _S_KRS) * sk_ref[...]
    kn = _dot(ckv, wuk_ref[...])
    for h in range(MLA_HEADS):
        mq_ref[0, h] = (slot(q, h) * cq_ref[...] + slot(q_partner, h) * sq_ref[...]).astype(mq_ref.dtype)
        mk_ref[0, h] = (slot(kn, h) + kpe).astype(mk_ref.dtype)
        mv_ref[0, h, 0] = _ones_row_pad(_dot_nt(wuv_ref[h], ckv)).astype(mv_ref.dtype)

    c64 = c64_ref[...]
    s64 = s64_ref[...]
    scale = HEAD_DIM ** -0.5

    p = proj(_H_NQ, _H_KC)
    for h in range(NSA_HEADS):
        nq_ref[0, h] = ((head(p, h) * c64 + head(p, NSA_HEADS + h) * s64) * (scale * LOG2_E)).astype(nq_ref.dtype)

    p = proj(_H_KC, _H_SBQ)
    base = _H_KC
    ns = oh_ref.shape[-1]
    for k_ref, hk, hks, lane0 in ((nkc_ref, _H_KC, _H_KCS, 0), (nks_ref, _H_KS, _H_KSS, ns), (nkw_ref, _H_KW, _H_KWS, 0)):
        for g in range(NSA_KV_HEADS):
            k = (head(p, hk - base + g) * c64 + head(p, hks - base + g) * s64).astype(k_ref.dtype)
            k_ref[0, g, :, lane0:lane0 + HEAD_DIM] = k
    for g in range(NSA_KV_HEADS):
        nks_ref[0, g, :, 0:ns] = oh_ref[...]
    for g in range(NSA_KV_HEADS):
        nvc_ref[0, g] = head(p, _H_VC - base + g).astype(nvc_ref.dtype)

    pt = _dot_nt(wt_ref[...], hn)
    for g in range(NSA_KV_HEADS):
        lo = g * HEAD_DIM
        nvs_ref[0, g, 0] = _ones_row_pad(pt[_T_VS + lo:_T_VS + lo + HEAD_DIM]).astype(nvs_ref.dtype)
        nvw_ref[0, g, 0] = _ones_row_pad(pt[_T_VW + lo:_T_VW + lo + HEAD_DIM]).astype(nvw_ref.dtype)
    gate_ref[0] = jax.nn.sigmoid(pt[_T_GATE:_T_GATE + GATE_ROWS] + gb_ref[...])

    p = proj(_H_SBQ, _N_HEAD_COLS)
    for h in range(SB_HEADS):
        sbq_ref[0, h] = (head(p, h) * (scale * LOG2_E)).astype(sbq_ref.dtype)
        sbk_ref[0, h] = head(p, SB_HEADS + h).astype(sbk_ref.dtype)
        sbv_ref[0, h, 0] = pt[_T_SBV + h * HEAD_DIM:_T_SBV + (h + 1) * HEAD_DIM].astype(sbv_ref.dtype)


def _proj(x, g, w_ext, w_t, qn, wuq, wuqs, kvn, wuk, wuv, gb, tabs):
    b, s, _ = x.shape
    ts = min(TOKEN_CHUNK, s)
    cq, sq, ck, sk, c64, s64, onehot = tabs

    def full(a):
        return pl.BlockSpec(a.shape, lambda bi, i: (0,) * a.ndim)

    def tab(a):
        return pl.BlockSpec((ts, a.shape[1]), lambda bi, i: (i, 0))

    def heads(n, d):
        return (pl.BlockSpec((1, n, ts, d), lambda bi, i: (bi, 0, i, 0)),
                jax.ShapeDtypeStruct((b, n, s, d), _MXU))

    def values_t(n):
        return (pl.BlockSpec((1, n, 1, HEAD_DIM + ONES_PAD, ts), lambda bi, i: (bi, 0, i, 0, 0)),
                jax.ShapeDtypeStruct((b, n, s // ts, HEAD_DIM + ONES_PAD, ts), _MXU))

    kv = lambda: heads(NSA_KV_HEADS, HEAD_DIM)
    outs = [heads(MLA_HEADS, LANES), heads(MLA_HEADS, LANES), values_t(MLA_HEADS), heads(NSA_HEADS, HEAD_DIM),
            kv(), kv(), heads(NSA_KV_HEADS, onehot.shape[1] + HEAD_DIM), values_t(NSA_KV_HEADS),
            kv(), values_t(NSA_KV_HEADS),
            (pl.BlockSpec((1, GATE_ROWS, ts), lambda bi, i: (bi, 0, i)), jax.ShapeDtypeStruct((b, GATE_ROWS, s), _F32)),
            heads(SB_HEADS, HEAD_DIM), heads(SB_HEADS, HEAD_DIM),
            (pl.BlockSpec((1, SB_HEADS, 1, HEAD_DIM, ts), lambda bi, i: (bi, 0, i, 0, 0)),
             jax.ShapeDtypeStruct((b, SB_HEADS, s // ts, HEAD_DIM, ts), _MXU))]
    return pl.pallas_call(
        _proj_kernel,
        grid=(b, s // ts),
        in_specs=[pl.BlockSpec((1, ts, D_MODEL), lambda bi, i: (bi, i, 0)), full(g), full(w_ext), full(w_t), full(qn),
                  full(wuq), full(wuqs), full(kvn), full(wuk), full(wuv), full(gb),
                  tab(cq), tab(sq), tab(ck), tab(sk), tab(c64), tab(s64), tab(onehot)],
        out_specs=[o[0] for o in outs],
        out_shape=[o[1] for o in outs],
        compiler_params=_params("parallel", "parallel"),
        name="proj",
    )(x, g, w_ext, w_t, qn, wuq, wuqs, kvn, wuk, wuv, gb, cq, sq, ck, sk, c64, s64, onehot)


def _ones_row_pad(vt):
    first = lax.broadcasted_iota(jnp.int32, (ONES_PAD, vt.shape[1]), 0) == 0
    return jnp.concatenate([vt, jnp.where(first, 1.0, 0.0).astype(vt.dtype)], axis=0)


def _softmax_step_t(carry, st, vt_chunks):
    m, acc = carry
    m_new = jnp.maximum(m, jnp.max(st, axis=0, keepdims=True))
    alpha = jnp.exp2(m - m_new)
    pt = jnp.exp2(st - m_new).astype(_MXU)
    n = st.shape[0] // len(vt_chunks)
    pv = sum(jnp.dot(vt, pt[c * n:(c + 1) * n], preferred_element_type=_F32) for c, vt in enumerate(vt_chunks))
    return m_new, alpha * acc + pv


def _softmax_init_t(d, cols):
    return (jnp.full((1, cols), M_FLOOR, _F32), jnp.zeros((d + ONES_PAD, cols), _F32))


def _softmax_finish_t(carry, d):
    _, acc = carry
    return acc[:d] * (1.0 / acc[d:d + 1])


def _two_chain_sweep(n_full, qk, soft, init):
    qk(1, 0)

    def body(j, carry):
        c0, c1 = carry
        qk(0, j)
        c1 = soft(1, j, c1, False)
        qk(1, j + 1)
        c0 = soft(0, j, c0, False)
        return c0, c1

    def unrolled(i, carry):
        for u in range(SWEEP_UNROLL):
            carry = body(SWEEP_UNROLL * i + u, carry)
        return carry

    trips = n_full // SWEEP_UNROLL
    carry = lax.fori_loop(0, trips, unrolled, init)
    c0, c1 = lax.fori_loop(SWEEP_UNROLL * trips, n_full, body, carry)
    qk(0, n_full)
    c1 = soft(1, n_full, c1, True)
    c0 = soft(0, n_full, c0, True)
    return c0, c1


def _mla_kernel(q_ref, k_ref, vt_ref, o_ref, s0_ref, s1_ref, *, t, nsub):
    qi = pl.program_id(2)
    s_refs = (s0_ref, s1_ref)

    def qk(hh, j):
        off = pl.multiple_of(j * t, t)
        s_refs[hh][...] = _dot_nt(k_ref[0, hh, pl.ds(off, t), :], q_ref[0, hh])

    def soft(hh, j, carry, diag):
        st = s_refs[hh][...]
        if diag:
            key = lax.broadcasted_iota(jnp.int32, (t, t), 0)
            qry = lax.broadcasted_iota(jnp.int32, (t, t), 1)
            st = jnp.where(key <= qry, st, NEG_INF)
        return _softmax_step_t(carry, st, [vt_ref[0, hh, j * nsub + c] for c in range(nsub)])

    carry = _two_chain_sweep(qi, qk, soft, tuple(_softmax_init_t(MLA_V, t) for _ in range(2)))
    ot = jnp.concatenate([_softmax_finish_t(c, MLA_V) for c in carry], axis=0)
    o_ref[0] = ot.T.astype(o_ref.dtype)


def _mla_attention(q, k, vt):
    b, h, s, _ = q.shape
    tv = vt.shape[-1]
    dv = vt.shape[-2]
    t = min(512, s)
    assert h % 2 == 0 and 2 * MLA_V == LANES and t % tv == 0
    return pl.pallas_call(
        functools.partial(_mla_kernel, t=t, nsub=t // tv),
        grid=(b, h // 2, s // t),
        in_specs=[pl.BlockSpec((1, 2, t, LANES), lambda bi, hi, i: (bi, hi, i, 0)),
                  pl.BlockSpec((1, 2, s, LANES), lambda bi, hi, i: (bi, hi, 0, 0)),
                  pl.BlockSpec((1, 2, s // tv, dv, tv), lambda bi, hi, i: (bi, hi, 0, 0, 0))],
        out_specs=pl.BlockSpec((1, t, LANES), lambda bi, hi, i: (bi, i, hi)),
        out_shape=jax.ShapeDtypeStruct((b, s, h * MLA_V), _MXU),
        scratch_shapes=[pltpu.VMEM((t, t), _F32), pltpu.VMEM((t, t), _F32)],
        compiler_params=_params("parallel", "parallel", "arbitrary"),
        name="mla_attn",
    )(q, k, vt)


def _sb_kernel(q_ref, k_ref, vt_ref, u_ref, o_ref, z_ref, lb_ref, hi_ref, lo_ref, a_ref, *, t):
    qi = pl.program_id(2)
    u = u_ref[...]

    def step(j, carry, diag):
        off = pl.multiple_of(j * t, t)
        if diag:
            key = lax.broadcasted_iota(jnp.int32, (t, t), 0)
            qry = lax.broadcasted_iota(jnp.int32, (t, t), 1)
            strict = key < qry
        for hh in range(2):
            z_ref[hh] = _dot_nt(k_ref[0, hh, pl.ds(off, t), :], q_ref[0, hh])
        first_rem = []
        for hh in range(2):
            z = z_ref[hh]
            log_beta = jnp.minimum(z, 0.0) - jnp.log2(1.0 + jnp.exp2(-jnp.abs(z)))
            log_rem = log_beta - z
            if diag:
                log_rem = jnp.where(strict, log_rem, 0.0)
            hi = log_rem.astype(_MXU)
            lb_ref[hh] = log_beta
            hi_ref[hh] = hi
            lo_ref[hh] = (log_rem - hi.astype(_F32)).astype(_MXU)
            first_rem.append(log_rem[0:1, :])
        out = []
        for hh in range(2):
            rem, acc = carry[hh]
            suffix = (jnp.dot(u, hi_ref[hh], preferred_element_type=_F32)
                      + jnp.dot(u, lo_ref[hh], preferred_element_type=_F32))
            a = jnp.exp2(lb_ref[hh] + suffix + rem)
            if diag:
                a = jnp.where(strict, a, 0.0)
            a_ref[hh] = a.astype(_MXU)
            out.append((rem + suffix[0:1, :] + first_rem[hh], acc))
        alive = jnp.max(jnp.maximum(out[0][0], out[1][0])) > F32_EXP2_ZERO
        out = tuple((rem, acc + jnp.dot(vt_ref[0, hh, j], a_ref[hh], preferred_element_type=_F32))
                    for hh, (rem, acc) in enumerate(out))
        return alive, out

    init = tuple((jnp.zeros((1, t), _F32), jnp.zeros((HEAD_DIM, t), _F32)) for _ in range(2))
    alive, carry = step(qi, init, True)

    def earlier(c):
        return (c[0] - 1,) + step(c[0], c[2], False)

    _, _, carry = lax.while_loop(lambda c: jnp.logical_and(c[0] >= 0, c[1]), earlier, (qi - 1, alive, carry))
    o_ref[0] = jnp.concatenate([acc for _, acc in carry], axis=0).T.astype(o_ref.dtype)


def _sb_attention(q, k, vt):
    b, h, s, d = q.shape
    t = vt.shape[-1]
    assert h % 2 == 0 and 2 * d == LANES and s % t == 0
    idx = np.arange(t)
    u = jnp.asarray(idx[None, :] > idx[:, None], _MXU)
    return pl.pallas_call(
        functools.partial(_sb_kernel, t=t),
        grid=(b, h // 2, s // t),
        in_specs=[pl.BlockSpec((1, 2, t, d), lambda bi, hi, i: (bi, hi, i, 0)),
                  pl.BlockSpec((1, 2, s, d), lambda bi, hi, i: (bi, hi, 0, 0)),
                  pl.BlockSpec((1, 2, s // t, d, t), lambda bi, hi, i: (bi, hi, 0, 0, 0)),
                  pl.BlockSpec((t, t), lambda bi, hi, i: (0, 0))],
        out_specs=pl.BlockSpec((1, t, LANES), lambda bi, hi, i: (bi, i, hi)),
        out_shape=jax.ShapeDtypeStruct((b, s, h * d), _MXU),
        scratch_shapes=[pltpu.VMEM((2, t, t), _F32), pltpu.VMEM((2, t, t), _F32), pltpu.VMEM((2, t, t), _MXU),
                        pltpu.VMEM((2, t, t), _MXU), pltpu.VMEM((2, t, t), _MXU)],
        compiler_params=_params("parallel", "parallel", "arbitrary"),
        name="sb_attn",
    )(q, k, vt, u)


def _compress_kernel(xk_ref, xv_ref, w1k_ref, w2k_ref, pk_ref, w1v_ref, w2v_ref, pv_ref, ok_ref, ov_ref):
    def hidden(x_ref, w1_ref, p_ref):
        x = x_ref[0, 0]
        n = x.shape[0]
        first = jnp.dot(x, w1_ref[0], preferred_element_type=_F32)
        second = jnp.dot(x, w1_ref[1], preferred_element_type=_F32)
        pos = _dot(p_ref[0], w1_ref[0]) + _dot(p_ref[1], w1_ref[1])
        hid = first + pltpu.roll(second, n - 1, 0) + pos[0:1]
        return 0.5 * hid * (1.0 + jnp.tanh(math.sqrt(2.0 / math.pi) * (hid + 0.044715 * hid * hid * hid)))

    ok_ref[0, 0] = _dot(hidden(xk_ref, w1k_ref, pk_ref), w2k_ref[...]).astype(ok_ref.dtype)
    ov_ref[0, 0] = _dot_nt(w2v_ref[...], hidden(xv_ref, w1v_ref, pv_ref)).astype(ov_ref.dtype)


def _compress(xk, xv, w1k, w2k, pk, w1v, w2v, pv):
    b, g, s, d = xk.shape
    n = s // CMP_STRIDE
    xk = xk.reshape(b, g, n, CMP_STRIDE * d)
    xv = xv.reshape(b, g, n, CMP_STRIDE * d)

    def full(a):
        return pl.BlockSpec(a.shape, lambda bi, gi: (0,) * a.ndim)

    xspec = pl.BlockSpec((1, 1, n, CMP_STRIDE * d), lambda bi, gi: (bi, gi, 0, 0))
    return pl.pallas_call(
        _compress_kernel,
        grid=(b, g),
        in_specs=[xspec, xspec, full(w1k), full(w2k), full(pk), full(w1v), full(w2v), full(pv)],
        out_specs=[pl.BlockSpec((1, 1, n, d), lambda bi, gi: (bi, gi, 0, 0)),
                   pl.BlockSpec((1, 1, d, n), lambda bi, gi: (bi, gi, 0, 0))],
        out_shape=[jax.ShapeDtypeStruct((b, g, n, d), _MXU), jax.ShapeDtypeStruct((b, g, d, n), _MXU)],
        compiler_params=_params("parallel", "parallel"),
        name="nsa_compress",
    )(xk, xv, w1k, w2k, pk, w1v, w2v, pv)


def _group_queries(q_ref, g, tq):
    return q_ref[0, g * NSA_GROUP:(g + 1) * NSA_GROUP].reshape(NSA_GROUP * tq, q_ref.shape[-1])


def _gated_heads(ot, gt_ref, g, branch, tq):
    out = []
    for r in range(NSA_GROUP):
        row = NSA_BRANCHES * (g * NSA_GROUP + r) + branch
        out.append(ot[:, r * tq:(r + 1) * tq] * gt_ref[0, row:row + 1, :])
    return out


def _cmp_kernel(q_ref, kc_ref, vct_ref, ov_ref, gt_ref, o_ref, qa_ref, s0_ref, s1_ref, *, tq, n_top):
    q0 = pl.program_id(1) * tq
    ncp = kc_ref.shape[2]
    ns = ov_ref.shape[0]
    lanes = NSA_GROUP * tq
    s_refs = (s0_ref, s1_ref)
    for g in range(NSA_KV_HEADS):
        s_refs[g][...] = _dot_nt(kc_ref[0, g], _group_queries(q_ref, g, tq))
    qpos = q0 + (lax.broadcasted_iota(jnp.int32, (1, lanes), 1) & (tq - 1))
    cmp_end = lax.broadcasted_iota(jnp.int32, (ncp, 1), 0) * CMP_STRIDE + (CMP_LEN - 1)
    visible = cmp_end <= qpos
    cur = jnp.right_shift(q0 + lax.broadcasted_iota(jnp.int32, (1, tq), 1), int(math.log2(SEL_LEN)))
    blk = lax.broadcasted_iota(jnp.int32, (ns, 1), 0)
    forced = (blk == 0) | (blk == cur) | (blk == cur - 1)
    future = blk > cur
    blk_f = blk.astype(_F32)
    heads = []
    scores = []
    for g in range(NSA_KV_HEADS):
        st = jnp.where(visible, s_refs[g][...], NEG_INF)
        e = jnp.exp2(st - jnp.max(st, axis=0, keepdims=True))
        inv = jnp.where(qpos >= CMP_LEN - 1, 1.0 / jnp.sum(e, axis=0, keepdims=True), 0.0)
        pt = e * inv
        heads += _gated_heads(_dot(vct_ref[0, g], pt), gt_ref, g, 0, tq)
        p_sum = sum(pt[:, r * tq:(r + 1) * tq] for r in range(NSA_GROUP))
        score = _dot_split_rhs(ov_ref[...], p_sum)
        scores.append(jnp.where(forced, FORCE_SCORE, jnp.where(future, -1.0, score)))
    o_ref[0] = jnp.concatenate(heads, axis=0).T
    unselected = [jnp.full((ns, tq), -1.0, _F32) for _ in range(NSA_KV_HEADS)]
    for _ in range(n_top):
        for g in range(NSA_KV_HEADS):
            top = jnp.max(scores[g], axis=0, keepdims=True)
            first = jnp.min(jnp.where(scores[g] == top, blk_f, float(ns)), axis=0, keepdims=True)
            pick = blk_f == first
            unselected[g] = jnp.where(pick, 0.0, unselected[g])
            scores[g] = jnp.where(pick, PICKED, scores[g])
    for g in range(NSA_KV_HEADS):
        sel_m1 = unselected[g].T.astype(qa_ref.dtype)
        for h in range(g * NSA_GROUP, (g + 1) * NSA_GROUP):
            qa_ref[0, h, :, 0:ns] = sel_m1
            qa_ref[0, h, :, ns:ns + HEAD_DIM] = q_ref[0, h]


def _cmp_select(q, kc, vct, gates_t):
    b, h, s, d = q.shape
    g = kc.shape[1]
    ncp = kc.shape[2]
    ns = s // SEL_LEN
    n_top = min(SEL_TOPK, ns)
    tq = min(256, s)
    assert tq & (tq - 1) == 0 and g == 2
    c0 = np.arange(ncp)[:, None] * CMP_STRIDE
    n0 = np.arange(ns)[None, :] * SEL_LEN
    overlap = jnp.asarray(((c0 < n0 + SEL_LEN) & (c0 + CMP_LEN > n0)).T, _MXU)
    return pl.pallas_call(
        functools.partial(_cmp_kernel, tq=tq, n_top=n_top),
        grid=(b, s // tq),
        in_specs=[pl.BlockSpec((1, h, tq, d), lambda bi, i: (bi, 0, i, 0)),
                  pl.BlockSpec((1, g, ncp, d), lambda bi, i: (bi, 0, 0, 0)),
                  pl.BlockSpec((1, g, d, ncp), lambda bi, i: (bi, 0, 0, 0)),
                  pl.BlockSpec((ns, ncp), lambda bi, i: (0, 0)),
                  pl.BlockSpec((1, GATE_ROWS, tq), lambda bi, i: (bi, 0, i))],
        out_specs=[pl.BlockSpec((1, tq, h * d), lambda bi, i: (bi, i, 0)),
                   pl.BlockSpec((1, h, tq, ns + d), lambda bi, i: (bi, 0, i, 0))],
        out_shape=[jax.ShapeDtypeStruct((b, s, h * d), _F32), jax.ShapeDtypeStruct((b, h, s, ns + d), _MXU)],
        scratch_shapes=[pltpu.VMEM((ncp, NSA_GROUP * tq), _F32) for _ in range(g)],
        compiler_params=_params("parallel", "arbitrary"),
        name="nsa_cmp_select",
    )(q, kc, vct, overlap, gates_t)


def _sel_kernel(q_ref, k_ref, vt_ref, gt_ref, o_ref, s0_ref, s1_ref, *, tq, tk, nsub):
    q0 = pl.program_id(1) * tq
    last = (q0 + tq - 1) // tk
    lanes = NSA_GROUP * tq
    s_refs = (s0_ref, s1_ref)

    def qk(g, j):
        off = pl.multiple_of(j * tk, tk)
        s_refs[g][...] = _dot_nt(k_ref[0, g, pl.ds(off, tk), :], _group_queries(q_ref, g, tq))

    def soft(g, j, carry, causal):
        st = s_refs[g][...]
        if causal:
            key = j * tk + lax.broadcasted_iota(jnp.int32, (tk, lanes), 0)
            qry = q0 + (lax.broadcasted_iota(jnp.int32, (tk, lanes), 1) & (tq - 1))
            st = jnp.where(key <= qry, st, NEG_INF)
        return _softmax_step_t(carry, st, [vt_ref[0, g, j * nsub + c] for c in range(nsub)])

    init = tuple(_softmax_init_t(HEAD_DIM, lanes) for _ in range(NSA_KV_HEADS))
    carry = _two_chain_sweep(last, qk, soft, init)
    heads = []
    for g in range(NSA_KV_HEADS):
        heads += _gated_heads(_softmax_finish_t(carry[g], HEAD_DIM), gt_ref, g, 1, tq)
    o_ref[0] = jnp.concatenate(heads, axis=0).T


def _sel_attention(q, k, vt, gates_t):
    b, h, s, da = q.shape
    g = k.shape[1]
    d = HEAD_DIM
    tv = vt.shape[-1]
    tq = min(256, s)
    tk = min(512, s)
    assert tq & (tq - 1) == 0 and s % tk == 0 and tk % tv == 0 and g == 2
    return pl.pallas_call(
        functools.partial(_sel_kernel, tq=tq, tk=tk, nsub=tk // tv),
        grid=(b, s // tq),
        in_specs=[pl.BlockSpec((1, h, tq, da), lambda bi, i: (bi, 0, i, 0)),
                  pl.BlockSpec((1, g, s, da), lambda bi, i: (bi, 0, 0, 0)),
                  pl.BlockSpec((1, g) + vt.shape[2:], lambda bi, i: (bi, 0, 0, 0, 0)),
                  pl.BlockSpec((1, GATE_ROWS, tq), lambda bi, i: (bi, 0, i))],
        out_specs=pl.BlockSpec((1, tq, h * d), lambda bi, i: (bi, i, 0)),
        out_shape=jax.ShapeDtypeStruct((b, s, h * d), _F32),
        scratch_shapes=[pltpu.VMEM((tk, NSA_GROUP * tq), _F32) for _ in range(g)],
        compiler_params=_params("parallel", "arbitrary"),
        name="nsa_selected",
    )(q, k, vt, gates_t)


def _win_kernel(q_ref, k_ref, vt_ref, gt_ref, o_ref, s0_ref, s1_ref, *, tq, span, tv):
    q0 = pl.program_id(1) * tq
    start = pl.multiple_of(jnp.maximum(q0 - WINDOW, 0), tq)
    first_chunk = start // tv
    lanes = NSA_GROUP * tq
    s_refs = (s0_ref, s1_ref)
    for g in range(NSA_KV_HEADS):
        s_refs[g][...] = _dot_nt(k_ref[0, g, pl.ds(start, span), :], _group_queries(q_ref, g, tq))
    key = start + lax.broadcasted_iota(jnp.int32, (span, lanes), 0)
    qry = q0 + (lax.broadcasted_iota(jnp.int32, (span, lanes), 1) & (tq - 1))
    heads = []
    for g in range(NSA_KV_HEADS):
        st = jnp.where(key <= qry, s_refs[g][...], NEG_INF)
        st = jnp.where(key > qry - WINDOW, st, NEG_INF)
        carry = _softmax_step_t(_softmax_init_t(HEAD_DIM, lanes), st,
                                [vt_ref[0, g, first_chunk + c] for c in range(span // tv)])
        heads += _gated_heads(_softmax_finish_t(carry, HEAD_DIM), gt_ref, g, 2, tq)
    o_ref[0] = jnp.concatenate(heads, axis=0).T


def _win_attention(q, k, vt, gates_t):
    b, h, s, d = q.shape
    g = k.shape[1]
    tv = vt.shape[-1]
    tq = min(256, s)
    span = WINDOW + tq
    assert tq & (tq - 1) == 0 and s >= span and tq % tv == 0 and WINDOW % tv == 0 and g == 2
    return pl.pallas_call(
        functools.partial(_win_kernel, tq=tq, span=span, tv=tv),
        grid=(b, s // tq),
        in_specs=[pl.BlockSpec((1, h, tq, d), lambda bi, i: (bi, 0, i, 0)),
                  pl.BlockSpec((1, g, s, d), lambda bi, i: (bi, 0, 0, 0)),
                  pl.BlockSpec((1, g) + vt.shape[2:], lambda bi, i: (bi, 0, 0, 0, 0)),
                  pl.BlockSpec((1, GATE_ROWS, tq), lambda bi, i: (bi, 0, i))],
        out_specs=pl.BlockSpec((1, tq, h * d), lambda bi, i: (bi, i, 0)),
        out_shape=jax.ShapeDtypeStruct((b, s, h * d), _F32),
        scratch_shapes=[pltpu.VMEM((span, NSA_GROUP * tq), _F32) for _ in range(g)],
        compiler_params=_params("parallel", "arbitrary"),
        name="nsa_window",
    )(q, k, vt, gates_t)


def _out_kernel(x_ref, mla_ref, cmp_ref, sel_ref, win_ref, sb_ref, w_ref, o_ref):
    def w_rows(first_head, n_heads):
        return w_ref[first_head * HEAD_DIM:(first_head + n_heads) * HEAD_DIM, :]

    acc = x_ref[0] + jnp.dot(mla_ref[0], w_rows(0, MLA_HEADS), preferred_element_type=_F32)
    nsa = cmp_ref[0] + sel_ref[0] + win_ref[0]
    acc = acc + _dot(nsa, w_rows(MLA_HEADS, NSA_HEADS))
    o_ref[0] = acc + jnp.dot(sb_ref[0], w_rows(MLA_HEADS + NSA_HEADS, SB_HEADS), preferred_element_type=_F32)


def _out_proj(x, o_mla, o_cmp, o_sel, o_win, o_sb, w_heads):
    b, s, _ = x.shape
    ts = min(512, s)

    def rows(a):
        return pl.BlockSpec((1, ts, a.shape[2]), lambda bi, i: (bi, i, 0))

    xspec = pl.BlockSpec((1, ts, D_MODEL), lambda bi, i: (bi, i, 0))
    return pl.pallas_call(
        _out_kernel,
        grid=(b, s // ts),
        in_specs=[xspec, rows(o_mla), rows(o_cmp), rows(o_sel), rows(o_win), rows(o_sb),
                  pl.BlockSpec(w_heads.shape, lambda bi, i: (0, 0))],
        out_specs=xspec,
        out_shape=jax.ShapeDtypeStruct(x.shape, _F32),
        compiler_params=_params("parallel", "parallel"),
        name="out_proj",
    )(x, o_mla, o_cmp, o_sel, o_win, o_sb, w_heads)


def _gather_cols(w, idx):
    idx = np.asarray(idx)
    cols = jnp.take(w, jnp.asarray(np.maximum(idx, 0)), axis=1)
    return jnp.where(jnp.asarray(idx >= 0)[None, :], cols, 0.0).astype(_MXU)


def _swap_halves(rot):
    return (np.arange(rot) + rot // 2) % rot


def _w_in_index():
    idx = np.full((_N_HEAD_COLS * HEAD_DIM,), -1, np.int64)

    def put(col, src):
        src = np.asarray(src)
        idx[col:col + len(src)] = src

    def put_head(pos, src):
        put(pos * HEAD_DIM, src)

    put(_S_CQ * LANES, _O_CQ + np.arange(MLA_Q_LORA))
    put(_S_CKV * LANES, _O_CKV + np.arange(MLA_KV_LORA))
    put(_S_KR * LANES + MLA_NOPE, _O_KR + np.arange(MLA_ROPE))
    put(_S_KRS * LANES + MLA_NOPE, _O_KR + _swap_halves(MLA_ROPE))
    for h in range(NSA_HEADS):
        put_head(_H_NQ + h, _O_NQ + h * HEAD_DIM + np.arange(HEAD_DIM))
        put_head(_H_NQS + h, _O_NQ + h * HEAD_DIM + _swap_halves(PARTIAL_ROT))
    for hk, hks, ok in ((_H_KC, _H_KCS, _O_NKC), (_H_KS, _H_KSS, _O_NKS), (_H_KW, _H_KWS, _O_NKW)):
        for g in range(NSA_KV_HEADS):
            put_head(hk + g, ok + g * HEAD_DIM + np.arange(HEAD_DIM))
            put_head(hks + g, ok + g * HEAD_DIM + _swap_halves(PARTIAL_ROT))
    for g in range(NSA_KV_HEADS):
        put_head(_H_VC + g, _O_NVC + g * HEAD_DIM + np.arange(HEAD_DIM))
    for h in range(SB_HEADS):
        put_head(_H_SBQ + h, _O_SBQ + h * HEAD_DIM + np.arange(HEAD_DIM))
        put_head(_H_SBK + h, _O_SBK + h * HEAD_DIM + np.arange(HEAD_DIM))
    return idx


def _mla_up_index():
    qd = MLA_NOPE + MLA_ROPE
    kd = MLA_NOPE + MLA_V
    uq = np.full((MLA_HEADS * LANES,), -1, np.int64)
    uqs = uq.copy()
    uk = uq.copy()
    for h in range(MLA_HEADS):
        uq[h * LANES:h * LANES + qd] = h * qd + np.arange(qd)
        uqs[h * LANES + MLA_NOPE:h * LANES + qd] = h * qd + MLA_NOPE + _swap_halves(MLA_ROPE)
        uk[h * LANES:h * LANES + MLA_NOPE] = h * kd + np.arange(MLA_NOPE)
    return uq, uqs, uk


def _transposed_weights(w_in_l, gate_bias):
    width = NSA_KV_HEADS * HEAD_DIM
    gate_rows = jnp.pad(w_in_l[:, _O_GATE:_O_GATE + N_GATES], ((0, 0), (0, _T_SBV - _T_GATE - N_GATES)))
    rows = jnp.concatenate([w_in_l[:, _O_NVS:_O_NVS + width], w_in_l[:, _O_NVW:_O_NVW + width], gate_rows,
                            w_in_l[:, _O_SBV:_O_SBV + SB_HEADS * HEAD_DIM]], axis=1).T.astype(_MXU)
    bias = jnp.pad(gate_bias, (0, GATE_ROWS - N_GATES)).reshape(GATE_ROWS, 1)
    return rows, bias


def _rope_tables(s):
    pos = jnp.arange(s, dtype=_F32)

    def cs(rot):
        half = rot // 2
        inv_freq = ROPE_THETA ** (-jnp.arange(half, dtype=_F32) / half)
        ang = pos[:, None] * inv_freq[None, :]
        c, sn = jnp.cos(ang), jnp.sin(ang)
        return jnp.concatenate([c, c], axis=1), jnp.concatenate([-sn, sn], axis=1)

    c, sn = cs(MLA_ROPE)
    ones = jnp.ones((s, MLA_NOPE), _F32)
    zeros = jnp.zeros((s, MLA_NOPE), _F32)
    pad = jnp.zeros((s, LANES - MLA_NOPE - MLA_ROPE), _F32)
    ck = jnp.concatenate([ones, c, pad], axis=1)
    sk = jnp.concatenate([zeros, sn, pad], axis=1)
    q_scale = (MLA_NOPE + MLA_ROPE) ** -0.5 * LOG2_E
    c, sn = cs(PARTIAL_ROT)
    c64 = jnp.concatenate([c, jnp.ones((s, HEAD_DIM - PARTIAL_ROT), _F32)], axis=1)
    s64 = jnp.concatenate([sn, jnp.zeros((s, HEAD_DIM - PARTIAL_ROT), _F32)], axis=1)
    ns = s // SEL_LEN
    onehot = (np.arange(s)[:, None] // SEL_LEN == np.arange(ns)[None, :]) * -NEG_INF
    return ck * q_scale, sk * q_scale, ck, sk, c64, s64, jnp.asarray(onehot, _MXU)


def kernel(x, ffn1_norm, ffn1_w_gate, ffn1_w_up, ffn1_w_down, mix_norm, w_in, mla_q_norm, mla_w_uq, mla_kv_norm,
           mla_w_ukv, nsa_gate_bias, nsa_cmp_pos_k, nsa_cmp_w1_k, nsa_cmp_w2_k, nsa_cmp_pos_v, nsa_cmp_w1_v,
           nsa_cmp_w2_v, w_out, ffn2_norm, ffn2_w_gate, ffn2_w_up, ffn2_w_down, final_norm):
    b, s, d = x.shape
    depth = w_in.shape[0]
    tabs = _rope_tables(s)
    in_idx = _w_in_index()
    uq_idx, uqs_idx, uk_idx = _mla_up_index()
    half = CMP_LEN * HEAD_DIM // 2
    fg = final_norm.reshape(1, d)

    def cmp_weights(w1, w2, pos, transpose_out):
        pos = jnp.broadcast_to(pos.reshape(2, 1, half), (2, 8, half)).astype(_MXU)
        w2 = w2.T if transpose_out else w2
        return w1.reshape(2, half, CMP_HIDDEN).astype(_MXU), w2.astype(_MXU), pos

    for l in range(depth):
        x2d = _ffn(x.reshape(b * s, d), ffn1_norm[l].reshape(1, d), ffn1_w_gate[l].astype(_MXU),
                   ffn1_w_up[l].astype(_MXU), ffn1_w_down[l].astype(_MXU), fg, False)
        x = x2d.reshape(b, s, d)
        w_t, gate_bias = _transposed_weights(w_in[l], nsa_gate_bias[l])
        (mq, mk, mvt, nq, nkc, nvc, nks, nvst, nkw, nvwt, gates_t, sbq, sbk, sbv) = _proj(
            x, mix_norm[l].reshape(1, d), _gather_cols(w_in[l], in_idx), w_t,
            mla_q_norm[l].reshape(1, -1), _gather_cols(mla_w_uq[l], uq_idx), _gather_cols(mla_w_uq[l], uqs_idx),
            mla_kv_norm[l].reshape(1, -1), _gather_cols(mla_w_ukv[l], uk_idx),
            mla_w_ukv[l].reshape(MLA_KV_LORA, MLA_HEADS, 2, MLA_V)[:, :, 1].transpose(1, 2, 0).astype(_MXU),
            gate_bias, tabs)
        o_mla = _mla_attention(mq, mk, mvt)
        kc, vct = _compress(nkc, nvc, *cmp_weights(nsa_cmp_w1_k[l], nsa_cmp_w2_k[l], nsa_cmp_pos_k[l], False),
                            *cmp_weights(nsa_cmp_w1_v[l], nsa_cmp_w2_v[l], nsa_cmp_pos_v[l], True))
        o_cmp, q_sel = _cmp_select(nq, kc, vct, gates_t)
        o_sel = _sel_attention(q_sel, nks, nvst, gates_t)
        o_win = _win_attention(nq, nkw, nvwt, gates_t)
        o_sb = _sb_attention(sbq, sbk, sbv)
        x = _out_proj(x, o_mla, o_cmp, o_sel, o_win, o_sb, w_out[l].astype(_MXU))
        x2d = _ffn(x.reshape(b * s, d), ffn2_norm[l].reshape(1, d), ffn2_w_gate[l].astype(_MXU),
                   ffn2_w_up[l].astype(_MXU), ffn2_w_down[l].astype(_MXU), fg, l == depth - 1)
        x = x2d.reshape(b, s, d)
    return x
```

```python
import functools
import math

import numpy as np
import jax
import jax.numpy as jnp
from jax import lax
from jax.experimental import pallas as pl
from jax.experimental.pallas import tpu as pltpu

D_MODEL = 1024
HEAD_DIM = 64
MLA_HEADS = 6
MLA_NOPE = 64
MLA_ROPE = 32
MLA_V = 64
MLA_Q_LORA = 256
MLA_KV_LORA = 128
NSA_HEADS = 6
NSA_KV_HEADS = 2
NSA_GROUP = NSA_HEADS // NSA_KV_HEADS
NSA_BRANCHES = 3
CMP_LEN = 32
CMP_STRIDE = 16
CMP_HIDDEN = 128
SEL_LEN = 64
SEL_TOPK = 16
WINDOW = 512
SB_HEADS = 4
D_FF = 2816
ROPE_THETA = 500000.0
PARTIAL_ROT = HEAD_DIM // 4
EPS = 1e-6
NEG_INF = -1e30
M_FLOOR = 0.1 * NEG_INF
FORCE_SCORE = 1e4
PICKED = -3e38
F32_EXP2_ZERO = -151.0
LOG2_E = math.log2(math.e)
N_GATES = NSA_HEADS * NSA_BRANCHES

LANES = 128
FFN_CHUNK = 256
SWEEP_UNROLL = 4
TOKEN_CHUNK = 256
ONES_PAD = 16
VMEM_LIMIT = 56 * 1024 * 1024

_MXU = jnp.bfloat16
_F32 = jnp.float32

_IN_WIDTHS = (MLA_Q_LORA, MLA_KV_LORA, MLA_ROPE, NSA_HEADS * HEAD_DIM) + (NSA_KV_HEADS * HEAD_DIM,) * 6 + (
    N_GATES, SB_HEADS * HEAD_DIM, SB_HEADS * HEAD_DIM, SB_HEADS * HEAD_DIM)
_IN_OFF = np.concatenate([[0], np.cumsum(_IN_WIDTHS)])
(_O_CQ, _O_CKV, _O_KR, _O_NQ, _O_NKC, _O_NVC, _O_NKS, _O_NVS, _O_NKW, _O_NVW, _O_GATE, _O_SBQ, _O_SBK,
 _O_SBV) = [int(v) for v in _IN_OFF[:-1]]

_S_CQ, _S_CKV, _S_KR, _S_KRS = 0, 2, 3, 4
_H_NQ, _H_NQS = 10, 16
_H_KC, _H_KCS, _H_VC = 22, 24, 26
_H_KS, _H_KSS = 28, 30
_H_KW, _H_KWS = 32, 34
_H_SBQ, _H_SBK = 36, 40
_N_HEAD_COLS = 44
_T_VS, _T_VW, _T_GATE = 0, NSA_KV_HEADS * HEAD_DIM, 2 * NSA_KV_HEADS * HEAD_DIM
GATE_ROWS = 24
_T_SBV = _T_GATE + 32
_T_ROWS = _T_SBV + SB_HEADS * HEAD_DIM


def _dot(a, b):
    return jnp.dot(a.astype(_MXU), b.astype(_MXU), preferred_element_type=_F32)


def _dot_nt(a, b):
    return lax.dot_general(a.astype(_MXU), b.astype(_MXU), (((1,), (1,)), ((), ())),
                           preferred_element_type=_F32)


def _dot_split_rhs(a, b):
    hi = b.astype(_MXU)
    lo = (b - hi.astype(_F32)).astype(_MXU)
    return (jnp.dot(a, hi, preferred_element_type=_F32) + jnp.dot(a, lo, preferred_element_type=_F32))


def _rms(x, g):
    return x * lax.rsqrt(jnp.mean(x * x, axis=-1, keepdims=True) + EPS) * g


def _params(*sem):
    return pltpu.CompilerParams(dimension_semantics=sem, vmem_limit_bytes=VMEM_LIMIT)


def _ffn_kernel(x_ref, g_ref, wg_ref, wu_ref, wd_ref, fg_ref, o_ref, h_ref, acc_ref, act_ref, *, final_norm):
    j = pl.program_id(1)

    @pl.when(j == 0)
    def _():
        h_ref[...] = _rms(x_ref[...], g_ref[...]).astype(h_ref.dtype)
        acc_ref[...] = jnp.zeros_like(acc_ref)

    h = h_ref[...]
    tf = act_ref.shape[1]
    for c0 in range(0, tf, FFN_CHUNK):
        c1 = min(c0 + FFN_CHUNK, tf)
        gate = jnp.dot(h, wg_ref[:, c0:c1], preferred_element_type=_F32)
        up = jnp.dot(h, wu_ref[:, c0:c1], preferred_element_type=_F32)
        act_ref[:, c0:c1] = (gate * jax.nn.sigmoid(gate) * up).astype(act_ref.dtype)
    acc_ref[...] += jnp.dot(act_ref[...], wd_ref[...], preferred_element_type=_F32)

    @pl.when(j == pl.num_programs(1) - 1)
    def _():
        y = x_ref[...] + 0.5 * acc_ref[...]
        if final_norm:
            y = _rms(y, fg_ref[...])
        o_ref[...] = y


def _ffn(x2d, g, wg, wu, wd, layer, fg, final_norm):
    rows = x2d.shape[0]
    tm = min(1024, rows)
    tf = D_FF // 2
    grid = (rows // tm, D_FF // tf)
    return pl.pallas_call(
        functools.partial(_ffn_kernel, final_norm=final_norm),
        grid=grid,
        in_specs=[
            pl.BlockSpec((tm, D_MODEL), lambda i, j: (i, 0)),
            pl.BlockSpec((1, D_MODEL), lambda i, j: (0, 0)),
            pl.BlockSpec((None, D_MODEL, tf), lambda i, j: (layer, 0, j)),
            pl.BlockSpec((None, D_MODEL, tf), lambda i, j: (layer, 0, j)),
            pl.BlockSpec((None, tf, D_MODEL), lambda i, j: (layer, j, 0)),
            pl.BlockSpec((1, D_MODEL), lambda i, j: (0, 0)),
        ],
        out_specs=pl.BlockSpec((tm, D_MODEL), lambda i, j: (i, 0)),
        out_shape=jax.ShapeDtypeStruct((rows, D_MODEL), _F32),
        scratch_shapes=[pltpu.VMEM((tm, D_MODEL), _MXU), pltpu.VMEM((tm, D_MODEL), _F32), pltpu.VMEM((tm, tf), _MXU)],
        compiler_params=_params("parallel", "arbitrary"),
        name="ffn",
    )(x2d, g, wg, wu, wd, fg)


def _proj_kernel(x_ref, g_ref, w_ref, wt_ref, qn_ref, wuq_ref, wuqs_ref, kvn_ref, wuk_ref, wuv_ref, gb_ref,
                 cq_ref, sq_ref, ck_ref, sk_ref, c64_ref, s64_ref, oh_ref,
                 mq_ref, mk_ref, mv_ref, nq_ref, nkc_ref, nvc_ref, nks_ref, nvs_ref, nkw_ref, nvw_ref,
                 gate_ref, sbq_ref, sbk_ref, sbv_ref):
    hn = _rms(x_ref[0], g_ref[...]).astype(_MXU)

    def proj(h0, h1):
        return jnp.dot(hn, w_ref[:, h0 * HEAD_DIM:h1 * HEAD_DIM], preferred_element_type=_F32)

    def slot(p, s):
        return p[:, s * LANES:(s + 1) * LANES]

    def head(p, i):
        return p[:, i * HEAD_DIM:(i + 1) * HEAD_DIM]

    p = proj(0, _H_NQ)
    cq = _rms(p[:, :MLA_Q_LORA], qn_ref[...])
    ckv = _rms(slot(p, _S_CKV), kvn_ref[...])
    q = _dot(cq, wuq_ref[...])
    q_partner = _dot(cq, wuqs_ref[...])
    kpe = slot(p, _S_KR) * ck_ref[...] + slot(p, _S_KRS) * sk_ref[...]
    kn = _dot(ckv, wuk_ref[...])
    for h in range(MLA_HEADS):
        mq_ref[0, h] = (slot(q, h) * cq_ref[...] + slot(q_partner, h) * sq_ref[...]).astype(mq_ref.dtype)
        mk_ref[0, h] = (slot(kn, h) + kpe).astype(mk_ref.dtype)
        mv_ref[0, h, 0] = _ones_row_pad(_dot_nt(wuv_ref[h], ckv)).astype(mv_ref.dtype)

    c64 = c64_ref[...]
    s64 = s64_ref[...]
    scale = HEAD_DIM ** -0.5

    p = proj(_H_NQ, _H_KC)
    for h in range(NSA_HEADS):
        nq_ref[0, h] = ((head(p, h) * c64 + head(p, NSA_HEADS + h) * s64) * (scale * LOG2_E)).astype(nq_ref.dtype)

    p = proj(_H_KC, _H_SBQ)
    base = _H_KC
    ns = oh_ref.shape[-1]
    for k_ref, hk, hks, lane0 in ((nkc_ref, _H_KC, _H_KCS, 0), (nks_ref, _H_KS, _H_KSS, ns), (nkw_ref, _H_KW, _H_KWS, 0)):
        for g in range(NSA_KV_HEADS):
            k = (head(p, hk - base + g) * c64 + head(p, hks - base + g) * s64).astype(k_ref.dtype)
            k_ref[0, g, :, lane0:lane0 + HEAD_DIM] = k
    for g in range(NSA_KV_HEADS):
        nks_ref[0, g, :, 0:ns] = oh_ref[...]
    for g in range(NSA_KV_HEADS):
        nvc_ref[0, g] = head(p, _H_VC - base + g).astype(nvc_ref.dtype)

    pt = _dot_nt(wt_ref[...], hn)
    for g in range(NSA_KV_HEADS):
        lo = g * HEAD_DIM
        nvs_ref[0, g, 0] = _ones_row_pad(pt[_T_VS + lo:_T_VS + lo + HEAD_DIM]).astype(nvs_ref.dtype)
        nvw_ref[0, g, 0] = _ones_row_pad(pt[_T_VW + lo:_T_VW + lo + HEAD_DIM]).astype(nvw_ref.dtype)
    gate_ref[0] = jax.nn.sigmoid(pt[_T_GATE:_T_GATE + GATE_ROWS] + gb_ref[...])

    p = proj(_H_SBQ, _N_HEAD_COLS)
    for h in range(SB_HEADS):
        sbq_ref[0, h] = (head(p, h) * (scale * LOG2_E)).astype(sbq_ref.dtype)
        sbk_ref[0, h] = head(p, SB_HEADS + h).astype(sbk_ref.dtype)
        sbv_ref[0, h, 0] = pt[_T_SBV + h * HEAD_DIM:_T_SBV + (h + 1) * HEAD_DIM].astype(sbv_ref.dtype)


def _proj(x, g, w_ext, w_t, qn, wuq, wuqs, kvn, wuk, wuv, gb, tabs):
    b, s, _ = x.shape
    ts = min(TOKEN_CHUNK, s)
    cq, sq, ck, sk, c64, s64, onehot = tabs

    def full(a):
        return pl.BlockSpec(a.shape, lambda bi, i: (0,) * a.ndim)

    def tab(a):
        return pl.BlockSpec((ts, a.shape[1]), lambda bi, i: (i, 0))

    def heads(n, d):
        return (pl.BlockSpec((1, n, ts, d), lambda bi, i: (bi, 0, i, 0)),
                jax.ShapeDtypeStruct((b, n, s, d), _MXU))

    def values_t(n):
        return (pl.BlockSpec((1, n, 1, HEAD_DIM + ONES_PAD, ts), lambda bi, i: (bi, 0, i, 0, 0)),
                jax.ShapeDtypeStruct((b, n, s // ts, HEAD_DIM + ONES_PAD, ts), _MXU))

    kv = lambda: heads(NSA_KV_HEADS, HEAD_DIM)
    outs = [heads(MLA_HEADS, LANES), heads(MLA_HEADS, LANES), values_t(MLA_HEADS), heads(NSA_HEADS, HEAD_DIM),
            kv(), kv(), heads(NSA_KV_HEADS, onehot.shape[1] + HEAD_DIM), values_t(NSA_KV_HEADS),
            kv(), values_t(NSA_KV_HEADS),
            (pl.BlockSpec((1, GATE_ROWS, ts), lambda bi, i: (bi, 0, i)), jax.ShapeDtypeStruct((b, GATE_ROWS, s), _F32)),
            heads(SB_HEADS, HEAD_DIM), heads(SB_HEADS, HEAD_DIM),
            (pl.BlockSpec((1, SB_HEADS, 1, HEAD_DIM, ts), lambda bi, i: (bi, 0, i, 0, 0)),
             jax.ShapeDtypeStruct((b, SB_HEADS, s // ts, HEAD_DIM, ts), _MXU))]
    return pl.pallas_call(
        _proj_kernel,
        grid=(b, s // ts),
        in_specs=[pl.BlockSpec((1, ts, D_MODEL), lambda bi, i: (bi, i, 0)), full(g), full(w_ext), full(w_t), full(qn),
                  full(wuq), full(wuqs), full(kvn), full(wuk), full(wuv), full(gb),
                  tab(cq), tab(sq), tab(ck), tab(sk), tab(c64), tab(s64), tab(onehot)],
        out_specs=[o[0] for o in outs],
        out_shape=[o[1] for o in outs],
        compiler_params=_params("parallel", "parallel"),
        name="proj",
    )(x, g, w_ext, w_t, qn, wuq, wuqs, kvn, wuk, wuv, gb, cq, sq, ck, sk, c64, s64, onehot)


def _ones_row_pad(vt):
    first = lax.broadcasted_iota(jnp.int32, (ONES_PAD, vt.shape[1]), 0) == 0
    return jnp.concatenate([vt, jnp.where(first, 1.0, 0.0).astype(vt.dtype)], axis=0)


def _softmax_step_t(carry, st, vt_chunks):
    m, acc = carry
    m_new = jnp.maximum(m, jnp.max(st, axis=0, keepdims=True))
    alpha = jnp.exp2(m - m_new)
    pt = jnp.exp2(st - m_new).astype(_MXU)
    n = st.shape[0] // len(vt_chunks)
    pv = sum(jnp.dot(vt, pt[c * n:(c + 1) * n], preferred_element_type=_F32) for c, vt in enumerate(vt_chunks))
    return m_new, alpha * acc + pv


def _softmax_init_t(d, cols):
    return (jnp.full((1, cols), M_FLOOR, _F32), jnp.zeros((d + ONES_PAD, cols), _F32))


def _softmax_finish_t(carry, d):
    _, acc = carry
    return acc[:d] * (1.0 / acc[d:d + 1])


def _two_chain_sweep(n_full, qk, soft, init):
    def body(j, carry, diag=False):
        c0, c1 = carry
        qk(0, j)
        c1 = soft(1, j, c1, diag)
        qk(1, jnp.zeros_like(j) if diag else j + 1)
        c0 = soft(0, j, c0, diag)
        return c0, c1

    def unrolled(i, carry):
        for u in range(SWEEP_UNROLL):
            carry = body(SWEEP_UNROLL * i + u, carry)
        return carry

    qk(1, n_full)
    carry = body(n_full, init, True)
    trips = n_full // SWEEP_UNROLL
    carry = lax.fori_loop(0, trips, unrolled, carry)
    return lax.fori_loop(SWEEP_UNROLL * trips, n_full, body, carry)


def _mla_kernel(q_ref, k_ref, vt_ref, o_ref, s0_ref, s1_ref, *, t, nsub):
    qi = pl.program_id(2)
    s_refs = (s0_ref, s1_ref)

    def qk(hh, j):
        off = pl.multiple_of(j * t, t)
        s_refs[hh][...] = _dot_nt(k_ref[0, hh, pl.ds(off, t), :], q_ref[0, hh])

    def soft(hh, j, carry, diag):
        st = s_refs[hh][...]
        if diag:
            key = lax.broadcasted_iota(jnp.int32, (t, t), 0)
            qry = lax.broadcasted_iota(jnp.int32, (t, t), 1)
            st = jnp.where(key <= qry, st, NEG_INF)
        return _softmax_step_t(carry, st, [vt_ref[0, hh, j * nsub + c] for c in range(nsub)])

    carry = _two_chain_sweep(qi, qk, soft, tuple(_softmax_init_t(MLA_V, t) for _ in range(2)))
    ot = jnp.concatenate([_softmax_finish_t(c, MLA_V) for c in carry], axis=0)
    o_ref[0] = ot.T.astype(o_ref.dtype)


def _mla_attention(q, k, vt):
    b, h, s, _ = q.shape
    tv = vt.shape[-1]
    dv = vt.shape[-2]
    t = min(512, s)
    assert h % 2 == 0 and 2 * MLA_V == LANES and t % tv == 0
    return pl.pallas_call(
        functools.partial(_mla_kernel, t=t, nsub=t // tv),
        grid=(b, h // 2, s // t),
        in_specs=[pl.BlockSpec((1, 2, t, LANES), lambda bi, hi, i: (bi, hi, i, 0)),
                  pl.BlockSpec((1, 2, s, LANES), lambda bi, hi, i: (bi, hi, 0, 0)),
                  pl.BlockSpec((1, 2, s // tv, dv, tv), lambda bi, hi, i: (bi, hi, 0, 0, 0))],
        out_specs=pl.BlockSpec((1, t, LANES), lambda bi, hi, i: (bi, i, hi)),
        out_shape=jax.ShapeDtypeStruct((b, s, h * MLA_V), _MXU),
        scratch_shapes=[pltpu.VMEM((t, t), _F32), pltpu.VMEM((t, t), _F32)],
        compiler_params=_params("parallel", "parallel", "arbitrary"),
        name="mla_attn",
    )(q, k, vt)


def _sb_kernel(q_ref, k_ref, vt_ref, u_ref, o_ref, z_ref, lb_ref, hi_ref, lo_ref, a_ref, *, t):
    qi = pl.program_id(2)
    u = u_ref[...]

    def step(j, carry, diag):
        off = pl.multiple_of(j * t, t)
        if diag:
            key = lax.broadcasted_iota(jnp.int32, (t, t), 0)
            qry = lax.broadcasted_iota(jnp.int32, (t, t), 1)
            strict = key < qry
        for hh in range(2):
            z_ref[hh] = _dot_nt(k_ref[0, hh, pl.ds(off, t), :], q_ref[0, hh])
        first_rem = []
        for hh in range(2):
            z = z_ref[hh]
            log_beta = jnp.minimum(z, 0.0) - jnp.log2(1.0 + jnp.exp2(-jnp.abs(z)))
            log_rem = log_beta - z
            if diag:
                log_rem = jnp.where(strict, log_rem, 0.0)
            hi = log_rem.astype(_MXU)
            lb_ref[hh] = log_beta
            hi_ref[hh] = hi
            lo_ref[hh] = (log_rem - hi.astype(_F32)).astype(_MXU)
            first_rem.append(log_rem[0:1, :])
        out = []
        for hh in range(2):
            rem, acc = carry[hh]
            suffix = (jnp.dot(u, hi_ref[hh], preferred_element_type=_F32)
                      + jnp.dot(u, lo_ref[hh], preferred_element_type=_F32))
            a = jnp.exp2(lb_ref[hh] + suffix + rem)
            if diag:
                a = jnp.where(strict, a, 0.0)
            a_ref[hh] = a.astype(_MXU)
            out.append((rem + suffix[0:1, :] + first_rem[hh], acc))
        alive = jnp.max(jnp.maximum(out[0][0], out[1][0])) > F32_EXP2_ZERO
        out = tuple((rem, acc + jnp.dot(vt_ref[0, hh, j], a_ref[hh], preferred_element_type=_F32))
                    for hh, (rem, acc) in enumerate(out))
        return alive, out

    init = tuple((jnp.zeros((1, t), _F32), jnp.zeros((HEAD_DIM, t), _F32)) for _ in range(2))
    alive, carry = step(qi, init, True)

    def earlier(c):
        return (c[0] - 1,) + step(c[0], c[2], False)

    _, _, carry = lax.while_loop(lambda c: jnp.logical_and(c[0] >= 0, c[1]), earlier, (qi - 1, alive, carry))
    o_ref[0] = jnp.concatenate([acc for _, acc in carry], axis=0).T.astype(o_ref.dtype)


def _sb_attention(q, k, vt):
    b, h, s, d = q.shape
    t = vt.shape[-1]
    assert h % 2 == 0 and 2 * d == LANES and s % t == 0
    idx = np.arange(t)
    u = jnp.asarray(idx[None, :] > idx[:, None], _MXU)
    return pl.pallas_call(
        functools.partial(_sb_kernel, t=t),
        grid=(b, h // 2, s // t),
        in_specs=[pl.BlockSpec((1, 2, t, d), lambda bi, hi, i: (bi, hi, i, 0)),
                  pl.BlockSpec((1, 2, s, d), lambda bi, hi, i: (bi, hi, 0, 0)),
                  pl.BlockSpec((1, 2, s // t, d, t), lambda bi, hi, i: (bi, hi, 0, 0, 0)),
                  pl.BlockSpec((t, t), lambda bi, hi, i: (0, 0))],
        out_specs=pl.BlockSpec((1, t, LANES), lambda bi, hi, i: (bi, i, hi)),
        out_shape=jax.ShapeDtypeStruct((b, s, h * d), _MXU),
        scratch_shapes=[pltpu.VMEM((2, t, t), _F32), pltpu.VMEM((2, t, t), _F32), pltpu.VMEM((2, t, t), _MXU),
                        pltpu.VMEM((2, t, t), _MXU), pltpu.VMEM((2, t, t), _MXU)],
        compiler_params=_params("parallel", "parallel", "arbitrary"),
        name="sb_attn",
    )(q, k, vt, u)


def _compress_kernel(xk_ref, xv_ref, w1k_ref, w2k_ref, pk_ref, w1v_ref, w2v_ref, pv_ref, ok_ref, ov_ref):
    def hidden(x_ref, w1_ref, p_ref):
        x = x_ref[0, 0]
        n = x.shape[0]
        first = jnp.dot(x, w1_ref[0], preferred_element_type=_F32)
        second = jnp.dot(x, w1_ref[1], preferred_element_type=_F32)
        pos = _dot(p_ref[0], w1_ref[0]) + _dot(p_ref[1], w1_ref[1])
        hid = first + pltpu.roll(second, n - 1, 0) + pos[0:1]
        return 0.5 * hid * (1.0 + jnp.tanh(math.sqrt(2.0 / math.pi) * (hid + 0.044715 * hid * hid * hid)))

    ok_ref[0, 0] = _dot(hidden(xk_ref, w1k_ref, pk_ref), w2k_ref[...]).astype(ok_ref.dtype)
    ov_ref[0, 0] = _dot_nt(w2v_ref[...], hidden(xv_ref, w1v_ref, pv_ref)).astype(ov_ref.dtype)


def _compress(xk, xv, w1k, w2k, pk, w1v, w2v, pv):
    b, g, s, d = xk.shape
    n = s // CMP_STRIDE
    xk = xk.reshape(b, g, n, CMP_STRIDE * d)
    xv = xv.reshape(b, g, n, CMP_STRIDE * d)

    def full(a):
        return pl.BlockSpec(a.shape, lambda bi, gi: (0,) * a.ndim)

    xspec = pl.BlockSpec((1, 1, n, CMP_STRIDE * d), lambda bi, gi: (bi, gi, 0, 0))
    return pl.pallas_call(
        _compress_kernel,
        grid=(b, g),
        in_specs=[xspec, xspec, full(w1k), full(w2k), full(pk), full(w1v), full(w2v), full(pv)],
        out_specs=[pl.BlockSpec((1, 1, n, d), lambda bi, gi: (bi, gi, 0, 0)),
                   pl.BlockSpec((1, 1, d, n), lambda bi, gi: (bi, gi, 0, 0))],
        out_shape=[jax.ShapeDtypeStruct((b, g, n, d), _MXU), jax.ShapeDtypeStruct((b, g, d, n), _MXU)],
        compiler_params=_params("parallel", "parallel"),
        name="nsa_compress",
    )(xk, xv, w1k, w2k, pk, w1v, w2v, pv)


def _group_queries(q_ref, g, tq):
    return q_ref[0, g * NSA_GROUP:(g + 1) * NSA_GROUP].reshape(NSA_GROUP * tq, q_ref.shape[-1])


def _gated_heads(ot, gt_ref, g, branch, tq):
    out = []
    for r in range(NSA_GROUP):
        row = NSA_BRANCHES * (g * NSA_GROUP + r) + branch
        out.append(ot[:, r * tq:(r + 1) * tq] * gt_ref[0, row:row + 1, :])
    return out


def _cmp_kernel(q_ref, kc_ref, vct_ref, ov_ref, gt_ref, o_ref, qa_ref, s0_ref, s1_ref, *, tq, n_top):
    q0 = pl.program_id(1) * tq
    ncp = kc_ref.shape[2]
    ns = ov_ref.shape[0]
    lanes = NSA_GROUP * tq
    s_refs = (s0_ref, s1_ref)
    for g in range(NSA_KV_HEADS):
        s_refs[g][...] = _dot_nt(kc_ref[0, g], _group_queries(q_ref, g, tq))
    qpos = q0 + (lax.broadcasted_iota(jnp.int32, (1, lanes), 1) & (tq - 1))
    cmp_end = lax.broadcasted_iota(jnp.int32, (ncp, 1), 0) * CMP_STRIDE + (CMP_LEN - 1)
    visible = cmp_end <= qpos
    cur = jnp.right_shift(q0 + lax.broadcasted_iota(jnp.int32, (1, tq), 1), int(math.log2(SEL_LEN)))
    blk = lax.broadcasted_iota(jnp.int32, (ns, 1), 0)
    forced = (blk == 0) | (blk == cur) | (blk == cur - 1)
    future = blk > cur
    blk_f = blk.astype(_F32)
    heads = []
    scores = []
    for g in range(NSA_KV_HEADS):
        st = jnp.where(visible, s_refs[g][...], NEG_INF)
        e = jnp.exp2(st - jnp.max(st, axis=0, keepdims=True))
        inv = jnp.where(qpos >= CMP_LEN - 1, 1.0 / jnp.sum(e, axis=0, keepdims=True), 0.0)
        pt = e * inv
        heads += _gated_heads(_dot(vct_ref[0, g], pt), gt_ref, g, 0, tq)
        p_sum = sum(pt[:, r * tq:(r + 1) * tq] for r in range(NSA_GROUP))
        score = _dot_split_rhs(ov_ref[...], p_sum)
        scores.append(jnp.where(forced, FORCE_SCORE, jnp.where(future, -1.0, score)))
    o_ref[0] = jnp.concatenate(heads, axis=0).T
    unselected = [jnp.full((ns, tq), -1.0, _F32) for _ in range(NSA_KV_HEADS)]
    for _ in range(n_top):
        for g in range(NSA_KV_HEADS):
            top = jnp.max(scores[g], axis=0, keepdims=True)
            first = jnp.min(jnp.where(scores[g] == top, blk_f, float(ns)), axis=0, keepdims=True)
            pick = blk_f == first
            unselected[g] = jnp.where(pick, 0.0, unselected[g])
            scores[g] = jnp.where(pick, PICKED, scores[g])
    for g in range(NSA_KV_HEADS):
        sel_m1 = unselected[g].T.astype(qa_ref.dtype)
        for h in range(g * NSA_GROUP, (g + 1) * NSA_GROUP):
            qa_ref[0, h, :, 0:ns] = sel_m1
            qa_ref[0, h, :, ns:ns + HEAD_DIM] = q_ref[0, h]


def _cmp_select(q, kc, vct, gates_t):
    b, h, s, d = q.shape
    g = kc.shape[1]
    ncp = kc.shape[2]
    ns = s // SEL_LEN
    n_top = min(SEL_TOPK, ns)
    tq = min(256, s)
    assert tq & (tq - 1) == 0 and g == 2
    c0 = np.arange(ncp)[:, None] * CMP_STRIDE
    n0 = np.arange(ns)[None, :] * SEL_LEN
    overlap = jnp.asarray(((c0 < n0 + SEL_LEN) & (c0 + CMP_LEN > n0)).T, _MXU)
    return pl.pallas_call(
        functools.partial(_cmp_kernel, tq=tq, n_top=n_top),
        grid=(b, s // tq),
        in_specs=[pl.BlockSpec((1, h, tq, d), lambda bi, i: (bi, 0, i, 0)),
                  pl.BlockSpec((1, g, ncp, d), lambda bi, i: (bi, 0, 0, 0)),
                  pl.BlockSpec((1, g, d, ncp), lambda bi, i: (bi, 0, 0, 0)),
                  pl.BlockSpec((ns, ncp), lambda bi, i: (0, 0)),
                  pl.BlockSpec((1, GATE_ROWS, tq), lambda bi, i: (bi, 0, i))],
        out_specs=[pl.BlockSpec((1, tq, h * d), lambda bi, i: (bi, i, 0)),
                   pl.BlockSpec((1, h, tq, ns + d), lambda bi, i: (bi, 0, i, 0))],
        out_shape=[jax.ShapeDtypeStruct((b, s, h * d), _F32), jax.ShapeDtypeStruct((b, h, s, ns + d), _MXU)],
        scratch_shapes=[pltpu.VMEM((ncp, NSA_GROUP * tq), _F32) for _ in range(g)],
        compiler_params=_params("parallel", "arbitrary"),
        name="nsa_cmp_select",
    )(q, kc, vct, overlap, gates_t)


def _sel_kernel(q_ref, k_ref, vt_ref, gt_ref, o_ref, s0_ref, s1_ref, *, tq, tk, nsub):
    q0 = pl.program_id(1) * tq
    last = (q0 + tq - 1) // tk
    lanes = NSA_GROUP * tq
    s_refs = (s0_ref, s1_ref)

    def qk(g, j):
        off = pl.multiple_of(j * tk, tk)
        s_refs[g][...] = _dot_nt(k_ref[0, g, pl.ds(off, tk), :], _group_queries(q_ref, g, tq))

    def soft(g, j, carry, causal):
        st = s_refs[g][...]
        if causal:
            key = j * tk + lax.broadcasted_iota(jnp.int32, (tk, lanes), 0)
            qry = q0 + (lax.broadcasted_iota(jnp.int32, (tk, lanes), 1) & (tq - 1))
            st = jnp.where(key <= qry, st, NEG_INF)
        return _softmax_step_t(carry, st, [vt_ref[0, g, j * nsub + c] for c in range(nsub)])

    init = tuple(_softmax_init_t(HEAD_DIM, lanes) for _ in range(NSA_KV_HEADS))
    carry = _two_chain_sweep(last, qk, soft, init)
    heads = []
    for g in range(NSA_KV_HEADS):
        heads += _gated_heads(_softmax_finish_t(carry[g], HEAD_DIM), gt_ref, g, 1, tq)
    o_ref[0] = jnp.concatenate(heads, axis=0).T


def _sel_attention(q, k, vt, gates_t):
    b, h, s, da = q.shape
    g = k.shape[1]
    d = HEAD_DIM
    tv = vt.shape[-1]
    tq = min(256, s)
    tk = min(512, s)
    assert tq & (tq - 1) == 0 and s % tk == 0 and tk % tv == 0 and g == 2
    return pl.pallas_call(
        functools.partial(_sel_kernel, tq=tq, tk=tk, nsub=tk // tv),
        grid=(b, s // tq),
        in_specs=[pl.BlockSpec((1, h, tq, da), lambda bi, i: (bi, 0, i, 0)),
                  pl.BlockSpec((1, g, s, da), lambda bi, i: (bi, 0, 0, 0)),
                  pl.BlockSpec((1, g) + vt.shape[2:], lambda bi, i: (bi, 0, 0, 0, 0)),
                  pl.BlockSpec((1, GATE_ROWS, tq), lambda bi, i: (bi, 0, i))],
        out_specs=pl.BlockSpec((1, tq, h * d), lambda bi, i: (bi, i, 0)),
        out_shape=jax.ShapeDtypeStruct((b, s, h * d), _F32),
        scratch_shapes=[pltpu.VMEM((tk, NSA_GROUP * tq), _F32) for _ in range(g)],
        compiler_params=_params("parallel", "arbitrary"),
        name="nsa_selected",
    )(q, k, vt, gates_t)


def _win_kernel(q_ref, k_ref, vt_ref, gt_ref, o_ref, s0_ref, s1_ref, *, tq, span, tv):
    q0 = pl.program_id(1) * tq
    start = pl.multiple_of(jnp.maximum(q0 - WINDOW, 0), tq)
    first_chunk = start // tv
    lanes = NSA_GROUP * tq
    s_refs = (s0_ref, s1_ref)
    for g in range(NSA_KV_HEADS):
        s_refs[g][...] = _dot_nt(k_ref[0, g, pl.ds(start, span), :], _group_queries(q_ref, g, tq))
    key = start + lax.broadcasted_iota(jnp.int32, (span, lanes), 0)
    qry = q0 + (lax.broadcasted_iota(jnp.int32, (span, lanes), 1) & (tq - 1))
    heads = []
    for g in range(NSA_KV_HEADS):
        st = jnp.where(key <= qry, s_refs[g][...], NEG_INF)
        st = jnp.where(key > qry - WINDOW, st, NEG_INF)
        carry = _softmax_step_t(_softmax_init_t(HEAD_DIM, lanes), st,
                                [vt_ref[0, g, first_chunk + c] for c in range(span // tv)])
        heads += _gated_heads(_softmax_finish_t(carry, HEAD_DIM), gt_ref, g, 2, tq)
    o_ref[0] = jnp.concatenate(heads, axis=0).T


def _win_attention(q, k, vt, gates_t):
    b, h, s, d = q.shape
    g = k.shape[1]
    tv = vt.shape[-1]
    tq = min(256, s)
    span = WINDOW + tq
    assert tq & (tq - 1) == 0 and s >= span and tq % tv == 0 and WINDOW % tv == 0 and g == 2
    return pl.pallas_call(
        functools.partial(_win_kernel, tq=tq, span=span, tv=tv),
        grid=(b, s // tq),
        in_specs=[pl.BlockSpec((1, h, tq, d), lambda bi, i: (bi, 0, i, 0)),
                  pl.BlockSpec((1, g, s, d), lambda bi, i: (bi, 0, 0, 0)),
                  pl.BlockSpec((1, g) + vt.shape[2:], lambda bi, i: (bi, 0, 0, 0, 0)),
                  pl.BlockSpec((1, GATE_ROWS, tq), lambda bi, i: (bi, 0, i))],
        out_specs=pl.BlockSpec((1, tq, h * d), lambda bi, i: (bi, i, 0)),
        out_shape=jax.ShapeDtypeStruct((b, s, h * d), _F32),
        scratch_shapes=[pltpu.VMEM((span, NSA_GROUP * tq), _F32) for _ in range(g)],
        compiler_params=_params("parallel", "arbitrary"),
        name="nsa_window",
    )(q, k, vt, gates_t)


def _out_kernel(x_ref, mla_ref, cmp_ref, sel_ref, win_ref, sb_ref, w_ref, o_ref):
    def w_rows(first_head, n_heads):
        return w_ref[first_head * HEAD_DIM:(first_head + n_heads) * HEAD_DIM, :]

    acc = x_ref[0] + jnp.dot(mla_ref[0], w_rows(0, MLA_HEADS), preferred_element_type=_F32)
    nsa = cmp_ref[0] + sel_ref[0] + win_ref[0]
    acc = acc + _dot(nsa, w_rows(MLA_HEADS, NSA_HEADS))
    o_ref[0] = acc + jnp.dot(sb_ref[0], w_rows(MLA_HEADS + NSA_HEADS, SB_HEADS), preferred_element_type=_F32)


def _out_proj(x, o_mla, o_cmp, o_sel, o_win, o_sb, w_heads):
    b, s, _ = x.shape
    ts = min(512, s)

    def rows(a):
        return pl.BlockSpec((1, ts, a.shape[2]), lambda bi, i: (bi, i, 0))

    xspec = pl.BlockSpec((1, ts, D_MODEL), lambda bi, i: (bi, i, 0))
    return pl.pallas_call(
        _out_kernel,
        grid=(b, s // ts),
        in_specs=[xspec, rows(o_mla), rows(o_cmp), rows(o_sel), rows(o_win), rows(o_sb),
                  pl.BlockSpec(w_heads.shape, lambda bi, i: (0, 0))],
        out_specs=xspec,
        out_shape=jax.ShapeDtypeStruct(x.shape, _F32),
        compiler_params=_params("parallel", "parallel"),
        name="out_proj",
    )(x, o_mla, o_cmp, o_sel, o_win, o_sb, w_heads)


def _gather_cols(w, idx):
    idx = np.asarray(idx)
    cols = jnp.take(w, jnp.asarray(np.maximum(idx, 0)), axis=1)
    return jnp.where(jnp.asarray(idx >= 0)[None, :], cols, 0.0).astype(_MXU)


def _swap_halves(rot):
    return (np.arange(rot) + rot // 2) % rot


def _w_in_index():
    idx = np.full((_N_HEAD_COLS * HEAD_DIM,), -1, np.int64)

    def put(col, src):
        src = np.asarray(src)
        idx[col:col + len(src)] = src

    def put_head(pos, src):
        put(pos * HEAD_DIM, src)

    put(_S_CQ * LANES, _O_CQ + np.arange(MLA_Q_LORA))
    put(_S_CKV * LANES, _O_CKV + np.arange(MLA_KV_LORA))
    put(_S_KR * LANES + MLA_NOPE, _O_KR + np.arange(MLA_ROPE))
    put(_S_KRS * LANES + MLA_NOPE, _O_KR + _swap_halves(MLA_ROPE))
    for h in range(NSA_HEADS):
        put_head(_H_NQ + h, _O_NQ + h * HEAD_DIM + np.arange(HEAD_DIM))
        put_head(_H_NQS + h, _O_NQ + h * HEAD_DIM + _swap_halves(PARTIAL_ROT))
    for hk, hks, ok in ((_H_KC, _H_KCS, _O_NKC), (_H_KS, _H_KSS, _O_NKS), (_H_KW, _H_KWS, _O_NKW)):
        for g in range(NSA_KV_HEADS):
            put_head(hk + g, ok + g * HEAD_DIM + np.arange(HEAD_DIM))
            put_head(hks + g, ok + g * HEAD_DIM + _swap_halves(PARTIAL_ROT))
    for g in range(NSA_KV_HEADS):
        put_head(_H_VC + g, _O_NVC + g * HEAD_DIM + np.arange(HEAD_DIM))
    for h in range(SB_HEADS):
        put_head(_H_SBQ + h, _O_SBQ + h * HEAD_DIM + np.arange(HEAD_DIM))
        put_head(_H_SBK + h, _O_SBK + h * HEAD_DIM + np.arange(HEAD_DIM))
    return idx


def _mla_up_index():
    qd = MLA_NOPE + MLA_ROPE
    kd = MLA_NOPE + MLA_V
    uq = np.full((MLA_HEADS * LANES,), -1, np.int64)
    uqs = uq.copy()
    uk = uq.copy()
    for h in range(MLA_HEADS):
        uq[h * LANES:h * LANES + qd] = h * qd + np.arange(qd)
        uqs[h * LANES + MLA_NOPE:h * LANES + qd] = h * qd + MLA_NOPE + _swap_halves(MLA_ROPE)
        uk[h * LANES:h * LANES + MLA_NOPE] = h * kd + np.arange(MLA_NOPE)
    return uq, uqs, uk


def _transposed_weights(w_in_l, gate_bias):
    width = NSA_KV_HEADS * HEAD_DIM
    gate_rows = jnp.pad(w_in_l[:, _O_GATE:_O_GATE + N_GATES], ((0, 0), (0, _T_SBV - _T_GATE - N_GATES)))
    rows = jnp.concatenate([w_in_l[:, _O_NVS:_O_NVS + width], w_in_l[:, _O_NVW:_O_NVW + width], gate_rows,
                            w_in_l[:, _O_SBV:_O_SBV + SB_HEADS * HEAD_DIM]], axis=1).T.astype(_MXU)
    bias = jnp.pad(gate_bias, (0, GATE_ROWS - N_GATES)).reshape(GATE_ROWS, 1)
    return rows, bias


def _rope_tables(s):
    pos = jnp.arange(s, dtype=_F32)

    def cs(rot):
        half = rot // 2
        inv_freq = ROPE_THETA ** (-jnp.arange(half, dtype=_F32) / half)
        ang = pos[:, None] * inv_freq[None, :]
        c, sn = jnp.cos(ang), jnp.sin(ang)
        return jnp.concatenate([c, c], axis=1), jnp.concatenate([-sn, sn], axis=1)

    c, sn = cs(MLA_ROPE)
    ones = jnp.ones((s, MLA_NOPE), _F32)
    zeros = jnp.zeros((s, MLA_NOPE), _F32)
    pad = jnp.zeros((s, LANES - MLA_NOPE - MLA_ROPE), _F32)
    ck = jnp.concatenate([ones, c, pad], axis=1)
    sk = jnp.concatenate([zeros, sn, pad], axis=1)
    q_scale = (MLA_NOPE + MLA_ROPE) ** -0.5 * LOG2_E
    c, sn = cs(PARTIAL_ROT)
    c64 = jnp.concatenate([c, jnp.ones((s, HEAD_DIM - PARTIAL_ROT), _F32)], axis=1)
    s64 = jnp.concatenate([sn, jnp.zeros((s, HEAD_DIM - PARTIAL_ROT), _F32)], axis=1)
    ns = s // SEL_LEN
    onehot = (np.arange(s)[:, None] // SEL_LEN == np.arange(ns)[None, :]) * -NEG_INF
    return ck * q_scale, sk * q_scale, ck, sk, c64, s64, jnp.asarray(onehot, _MXU)


def kernel(x, ffn1_norm, ffn1_w_gate, ffn1_w_up, ffn1_w_down, mix_norm, w_in, mla_q_norm, mla_w_uq, mla_kv_norm,
           mla_w_ukv, nsa_gate_bias, nsa_cmp_pos_k, nsa_cmp_w1_k, nsa_cmp_w2_k, nsa_cmp_pos_v, nsa_cmp_w1_v,
           nsa_cmp_w2_v, w_out, ffn2_norm, ffn2_w_gate, ffn2_w_up, ffn2_w_down, final_norm):
    b, s, d = x.shape
    depth = w_in.shape[0]
    tabs = _rope_tables(s)
    in_idx = _w_in_index()
    uq_idx, uqs_idx, uk_idx = _mla_up_index()
    half = CMP_LEN * HEAD_DIM // 2
    fg = final_norm.reshape(1, d)

    def cmp_weights(w1, w2, pos, transpose_out):
        pos = jnp.broadcast_to(pos.reshape(2, 1, half), (2, 8, half)).astype(_MXU)
        w2 = w2.T if transpose_out else w2
        return w1.reshape(2, half, CMP_HIDDEN).astype(_MXU), w2.astype(_MXU), pos

    ffn1 = [w.astype(_MXU) for w in (ffn1_w_gate, ffn1_w_up, ffn1_w_down)]
    ffn2 = [w.astype(_MXU) for w in (ffn2_w_gate, ffn2_w_up, ffn2_w_down)]
    for l in range(depth):
        x2d = _ffn(x.reshape(b * s, d), ffn1_norm[l].reshape(1, d), *ffn1, l, fg, False)
        x = x2d.reshape(b, s, d)
        w_t, gate_bias = _transposed_weights(w_in[l], nsa_gate_bias[l])
        (mq, mk, mvt, nq, nkc, nvc, nks, nvst, nkw, nvwt, gates_t, sbq, sbk, sbv) = _proj(
            x, mix_norm[l].reshape(1, d), _gather_cols(w_in[l], in_idx), w_t,
            mla_q_norm[l].reshape(1, -1), _gather_cols(mla_w_uq[l], uq_idx), _gather_cols(mla_w_uq[l], uqs_idx),
            mla_kv_norm[l].reshape(1, -1), _gather_cols(mla_w_ukv[l], uk_idx),
            mla_w_ukv[l].reshape(MLA_KV_LORA, MLA_HEADS, 2, MLA_V)[:, :, 1].transpose(1, 2, 0).astype(_MXU),
            gate_bias, tabs)
        o_mla = _mla_attention(mq, mk, mvt)
        kc, vct = _compress(nkc, nvc, *cmp_weights(nsa_cmp_w1_k[l], nsa_cmp_w2_k[l], nsa_cmp_pos_k[l], False),
                            *cmp_weights(nsa_cmp_w1_v[l], nsa_cmp_w2_v[l], nsa_cmp_pos_v[l], True))
        o_cmp, q_sel = _cmp_select(nq, kc, vct, gates_t)
        o_sel = _sel_attention(q_sel, nks, nvst, gates_t)
        o_win = _win_attention(nq, nkw, nvwt, gates_t)
        o_sb = _sb_attention(sbq, sbk, sbv)
        x = _out_proj(x, o_mla, o_cmp, o_sel, o_win, o_sb, w_out[l].astype(_MXU))
        x2d = _ffn(x.reshape(b * s, d), ffn2_norm[l].reshape(1, d), *ffn2, l, fg, l == depth - 1)
        x = x2d.reshape(b, s, d)
    return x
```

```python
import functools
import math

import numpy as np
import jax
import jax.numpy as jnp
from jax import lax
from jax.experimental import pallas as pl
from jax.experimental.pallas import tpu as pltpu

D_MODEL = 1024
HEAD_DIM = 64
MLA_HEADS = 6
MLA_NOPE = 64
MLA_ROPE = 32
MLA_V = 64
MLA_Q_LORA = 256
MLA_KV_LORA = 128
NSA_HEADS = 6
NSA_KV_HEADS = 2
NSA_GROUP = NSA_HEADS // NSA_KV_HEADS
NSA_BRANCHES = 3
CMP_LEN = 32
CMP_STRIDE = 16
CMP_HIDDEN = 128
SEL_LEN = 64
SEL_TOPK = 16
WINDOW = 512
SB_HEADS = 4
D_FF = 2816
ROPE_THETA = 500000.0
PARTIAL_ROT = HEAD_DIM // 4
EPS = 1e-6
NEG_INF = -1e30
M_FLOOR = 0.1 * NEG_INF
FORCE_SCORE = 1e4
PICKED = -3e38
F32_EXP2_ZERO = -151.0
LOG2_E = math.log2(math.e)
N_GATES = NSA_HEADS * NSA_BRANCHES

LANES = 128
FFN_CHUNK = 256
SWEEP_UNROLL = 4
TOKEN_CHUNK = 256
ONES_PAD = 16
VMEM_LIMIT = 56 * 1024 * 1024

_MXU = jnp.bfloat16
_F32 = jnp.float32

_IN_WIDTHS = (MLA_Q_LORA, MLA_KV_LORA, MLA_ROPE, NSA_HEADS * HEAD_DIM) + (NSA_KV_HEADS * HEAD_DIM,) * 6 + (
    N_GATES, SB_HEADS * HEAD_DIM, SB_HEADS * HEAD_DIM, SB_HEADS * HEAD_DIM)
_IN_OFF = np.concatenate([[0], np.cumsum(_IN_WIDTHS)])
(_O_CQ, _O_CKV, _O_KR, _O_NQ, _O_NKC, _O_NVC, _O_NKS, _O_NVS, _O_NKW, _O_NVW, _O_GATE, _O_SBQ, _O_SBK,
 _O_SBV) = [int(v) for v in _IN_OFF[:-1]]

_S_CQ, _S_CKV, _S_KR, _S_KRS = 0, 2, 3, 4
_H_NQ, _H_NQS = 10, 16
_H_KC, _H_KCS, _H_VC = 22, 24, 26
_H_KS, _H_KSS = 28, 30
_H_KW, _H_KWS = 32, 34
_H_SBQ, _H_SBK = 36, 40
_N_HEAD_COLS = 44
_T_VS, _T_VW, _T_GATE = 0, NSA_KV_HEADS * HEAD_DIM, 2 * NSA_KV_HEADS * HEAD_DIM
GATE_ROWS = 24
_T_SBV = _T_GATE + 32
_T_ROWS = _T_SBV + SB_HEADS * HEAD_DIM


def _dot(a, b):
    return jnp.dot(a.astype(_MXU), b.astype(_MXU), preferred_element_type=_F32)


def _dot_nt(a, b):
    return lax.dot_general(a.astype(_MXU), b.astype(_MXU), (((1,), (1,)), ((), ())),
                           preferred_element_type=_F32)


def _dot_split_rhs(a, b):
    hi = b.astype(_MXU)
    lo = (b - hi.astype(_F32)).astype(_MXU)
    return (jnp.dot(a, hi, preferred_element_type=_F32) + jnp.dot(a, lo, preferred_element_type=_F32))


def _rms(x, g):
    return x * lax.rsqrt(jnp.mean(x * x, axis=-1, keepdims=True) + EPS) * g


def _params(*sem):
    return pltpu.CompilerParams(dimension_semantics=sem, vmem_limit_bytes=VMEM_LIMIT)


def _layer_spec(a, layer):
    return pl.BlockSpec((None,) + a.shape[1:], lambda *_: (layer,) + (0,) * (a.ndim - 1))


def _ffn_kernel(x_ref, g_ref, wg_ref, wu_ref, wd_ref, fg_ref, o_ref, h_ref, acc_ref, act_ref, *, final_norm):
    j = pl.program_id(1)

    @pl.when(j == 0)
    def _():
        h_ref[...] = _rms(x_ref[...], g_ref[...]).astype(h_ref.dtype)
        acc_ref[...] = jnp.zeros_like(acc_ref)

    h = h_ref[...]
    tf = act_ref.shape[1]
    for c0 in range(0, tf, FFN_CHUNK):
        c1 = min(c0 + FFN_CHUNK, tf)
        gate = jnp.dot(h, wg_ref[:, c0:c1], preferred_element_type=_F32)
        up = jnp.dot(h, wu_ref[:, c0:c1], preferred_element_type=_F32)
        act_ref[:, c0:c1] = (gate * jax.nn.sigmoid(gate) * up).astype(act_ref.dtype)
    acc_ref[...] += jnp.dot(act_ref[...], wd_ref[...], preferred_element_type=_F32)

    @pl.when(j == pl.num_programs(1) - 1)
    def _():
        y = x_ref[...] + 0.5 * acc_ref[...]
        if final_norm:
            y = _rms(y, fg_ref[...])
        o_ref[...] = y


def _ffn(x2d, g, wg, wu, wd, layer, fg, final_norm):
    rows = x2d.shape[0]
    tm = min(1024, rows)
    tf = D_FF // 2
    grid = (rows // tm, D_FF // tf)
    return pl.pallas_call(
        functools.partial(_ffn_kernel, final_norm=final_norm),
        grid=grid,
        in_specs=[
            pl.BlockSpec((tm, D_MODEL), lambda i, j: (i, 0)),
            _layer_spec(g, layer),
            pl.BlockSpec((None, D_MODEL, tf), lambda i, j: (layer, 0, j)),
            pl.BlockSpec((None, D_MODEL, tf), lambda i, j: (layer, 0, j)),
            pl.BlockSpec((None, tf, D_MODEL), lambda i, j: (layer, j, 0)),
            pl.BlockSpec((1, D_MODEL), lambda i, j: (0, 0)),
        ],
        out_specs=pl.BlockSpec((tm, D_MODEL), lambda i, j: (i, 0)),
        out_shape=jax.ShapeDtypeStruct((rows, D_MODEL), _F32),
        scratch_shapes=[pltpu.VMEM((tm, D_MODEL), _MXU), pltpu.VMEM((tm, D_MODEL), _F32), pltpu.VMEM((tm, tf), _MXU)],
        compiler_params=_params("parallel", "arbitrary"),
        name="ffn",
    )(x2d, g, wg, wu, wd, fg)


def _proj_kernel(x_ref, g_ref, w_ref, wt_ref, qn_ref, wuq_ref, wuqs_ref, kvn_ref, wuk_ref, wuv_ref, gb_ref,
                 cq_ref, sq_ref, ck_ref, sk_ref, c64_ref, s64_ref, oh_ref,
                 mq_ref, mk_ref, mv_ref, nq_ref, nkc_ref, nvc_ref, nks_ref, nvs_ref, nkw_ref, nvw_ref,
                 gate_ref, sbq_ref, sbk_ref, sbv_ref):
    hn = _rms(x_ref[0], g_ref[...]).astype(_MXU)

    def proj(h0, h1):
        return jnp.dot(hn, w_ref[:, h0 * HEAD_DIM:h1 * HEAD_DIM], preferred_element_type=_F32)

    def slot(p, s):
        return p[:, s * LANES:(s + 1) * LANES]

    def head(p, i):
        return p[:, i * HEAD_DIM:(i + 1) * HEAD_DIM]

    p = proj(0, _H_NQ)
    cq = _rms(p[:, :MLA_Q_LORA], qn_ref[...])
    ckv = _rms(slot(p, _S_CKV), kvn_ref[...])
    q = _dot(cq, wuq_ref[...])
    q_partner = _dot(cq, wuqs_ref[...])
    kpe = slot(p, _S_KR) * ck_ref[...] + slot(p, _S_KRS) * sk_ref[...]
    kn = _dot(ckv, wuk_ref[...])
    for h in range(MLA_HEADS):
        mq_ref[0, h] = (slot(q, h) * cq_ref[...] + slot(q_partner, h) * sq_ref[...]).astype(mq_ref.dtype)
        mk_ref[0, h] = (slot(kn, h) + kpe).astype(mk_ref.dtype)
        mv_ref[0, h, 0] = _ones_row_pad(_dot_nt(wuv_ref[h], ckv)).astype(mv_ref.dtype)

    c64 = c64_ref[...]
    s64 = s64_ref[...]
    scale = HEAD_DIM ** -0.5

    p = proj(_H_NQ, _H_KC)
    for h in range(NSA_HEADS):
        nq_ref[0, h] = ((head(p, h) * c64 + head(p, NSA_HEADS + h) * s64) * (scale * LOG2_E)).astype(nq_ref.dtype)

    p = proj(_H_KC, _H_SBQ)
    base = _H_KC
    ns = oh_ref.shape[-1]
    for k_ref, hk, hks, lane0 in ((nkc_ref, _H_KC, _H_KCS, 0), (nks_ref, _H_KS, _H_KSS, ns), (nkw_ref, _H_KW, _H_KWS, 0)):
        for g in range(NSA_KV_HEADS):
            k = (head(p, hk - base + g) * c64 + head(p, hks - base + g) * s64).astype(k_ref.dtype)
            k_ref[0, g, :, lane0:lane0 + HEAD_DIM] = k
    for g in range(NSA_KV_HEADS):
        nks_ref[0, g, :, 0:ns] = oh_ref[...]
    for g in range(NSA_KV_HEADS):
        nvc_ref[0, g] = head(p, _H_VC - base + g).astype(nvc_ref.dtype)

    pt = _dot_nt(wt_ref[...], hn)
    for g in range(NSA_KV_HEADS):
        lo = g * HEAD_DIM
        nvs_ref[0, g, 0] = _ones_row_pad(pt[_T_VS + lo:_T_VS + lo + HEAD_DIM]).astype(nvs_ref.dtype)
        nvw_ref[0, g, 0] = _ones_row_pad(pt[_T_VW + lo:_T_VW + lo + HEAD_DIM]).astype(nvw_ref.dtype)
    gate_ref[0] = jax.nn.sigmoid(pt[_T_GATE:_T_GATE + GATE_ROWS] + gb_ref[...])

    p = proj(_H_SBQ, _N_HEAD_COLS)
    for h in range(SB_HEADS):
        sbq_ref[0, h] = (head(p, h) * (scale * LOG2_E)).astype(sbq_ref.dtype)
        sbk_ref[0, h] = head(p, SB_HEADS + h).astype(sbk_ref.dtype)
        sbv_ref[0, h, 0] = pt[_T_SBV + h * HEAD_DIM:_T_SBV + (h + 1) * HEAD_DIM].astype(sbv_ref.dtype)


def _proj(x, layer, g, w_ext, w_t, qn, wuq, wuqs, kvn, wuk, wuv, gb, tabs):
    b, s, _ = x.shape
    ts = min(TOKEN_CHUNK, s)
    cq, sq, ck, sk, c64, s64, onehot = tabs
    full = functools.partial(_layer_spec, layer=layer)

    def tab(a):
        return pl.BlockSpec((ts, a.shape[1]), lambda bi, i: (i, 0))

    def heads(n, d):
        return (pl.BlockSpec((1, n, ts, d), lambda bi, i: (bi, 0, i, 0)),
                jax.ShapeDtypeStruct((b, n, s, d), _MXU))

    def values_t(n):
        return (pl.BlockSpec((1, n, 1, HEAD_DIM + ONES_PAD, ts), lambda bi, i: (bi, 0, i, 0, 0)),
                jax.ShapeDtypeStruct((b, n, s // ts, HEAD_DIM + ONES_PAD, ts), _MXU))

    kv = lambda: heads(NSA_KV_HEADS, HEAD_DIM)
    outs = [heads(MLA_HEADS, LANES), heads(MLA_HEADS, LANES), values_t(MLA_HEADS), heads(NSA_HEADS, HEAD_DIM),
            kv(), kv(), heads(NSA_KV_HEADS, onehot.shape[1] + HEAD_DIM), values_t(NSA_KV_HEADS),
            kv(), values_t(NSA_KV_HEADS),
            (pl.BlockSpec((1, GATE_ROWS, ts), lambda bi, i: (bi, 0, i)), jax.ShapeDtypeStruct((b, GATE_ROWS, s), _F32)),
            heads(SB_HEADS, HEAD_DIM), heads(SB_HEADS, HEAD_DIM),
            (pl.BlockSpec((1, SB_HEADS, 1, HEAD_DIM, ts), lambda bi, i: (bi, 0, i, 0, 0)),
             jax.ShapeDtypeStruct((b, SB_HEADS, s // ts, HEAD_DIM, ts), _MXU))]
    return pl.pallas_call(
        _proj_kernel,
        grid=(b, s // ts),
        in_specs=[pl.BlockSpec((1, ts, D_MODEL), lambda bi, i: (bi, i, 0)), full(g), full(w_ext), full(w_t), full(qn),
                  full(wuq), full(wuqs), full(kvn), full(wuk), full(wuv), full(gb),
                  tab(cq), tab(sq), tab(ck), tab(sk), tab(c64), tab(s64), tab(onehot)],
        out_specs=[o[0] for o in outs],
        out_shape=[o[1] for o in outs],
        compiler_params=_params("parallel", "parallel"),
        name="proj",
    )(x, g, w_ext, w_t, qn, wuq, wuqs, kvn, wuk, wuv, gb, cq, sq, ck, sk, c64, s64, onehot)


def _ones_row_pad(vt):
    first = lax.broadcasted_iota(jnp.int32, (ONES_PAD, vt.shape[1]), 0) == 0
    return jnp.concatenate([vt, jnp.where(first, 1.0, 0.0).astype(vt.dtype)], axis=0)


def _softmax_step_t(carry, st, vt_chunks):
    m, acc = carry
    m_new = jnp.maximum(m, jnp.max(st, axis=0, keepdims=True))
    alpha = jnp.exp2(m - m_new)
    pt = jnp.exp2(st - m_new).astype(_MXU)
    n = st.shape[0] // len(vt_chunks)
    pv = sum(jnp.dot(vt, pt[c * n:(c + 1) * n], preferred_element_type=_F32) for c, vt in enumerate(vt_chunks))
    return m_new, alpha * acc + pv


def _softmax_init_t(d, cols):
    return (jnp.full((1, cols), M_FLOOR, _F32), jnp.zeros((d + ONES_PAD, cols), _F32))


def _softmax_finish_t(carry, d):
    _, acc = carry
    return acc[:d] * (1.0 / acc[d:d + 1])


def _two_chain_sweep(n_full, qk, soft, init):
    def body(j, carry, diag=False):
        c0, c1 = carry
        qk(0, j)
        c1 = soft(1, j, c1, diag)
        qk(1, jnp.zeros_like(j) if diag else j + 1)
        c0 = soft(0, j, c0, diag)
        return c0, c1

    def unrolled(i, carry):
        for u in range(SWEEP_UNROLL):
            carry = body(SWEEP_UNROLL * i + u, carry)
        return carry

    qk(1, n_full)
    carry = body(n_full, init, True)
    trips = n_full // SWEEP_UNROLL
    carry = lax.fori_loop(0, trips, unrolled, carry)
    return lax.fori_loop(SWEEP_UNROLL * trips, n_full, body, carry)


def _mla_kernel(q_ref, k_ref, vt_ref, o_ref, s0_ref, s1_ref, *, t, nsub):
    qi = pl.program_id(2)
    s_refs = (s0_ref, s1_ref)

    def qk(hh, j):
        off = pl.multiple_of(j * t, t)
        s_refs[hh][...] = _dot_nt(k_ref[0, hh, pl.ds(off, t), :], q_ref[0, hh])

    def soft(hh, j, carry, diag):
        st = s_refs[hh][...]
        if diag:
            key = lax.broadcasted_iota(jnp.int32, (t, t), 0)
            qry = lax.broadcasted_iota(jnp.int32, (t, t), 1)
            st = jnp.where(key <= qry, st, NEG_INF)
        return _softmax_step_t(carry, st, [vt_ref[0, hh, j * nsub + c] for c in range(nsub)])

    carry = _two_chain_sweep(qi, qk, soft, tuple(_softmax_init_t(MLA_V, t) for _ in range(2)))
    ot = jnp.concatenate([_softmax_finish_t(c, MLA_V) for c in carry], axis=0)
    o_ref[0] = ot.T.astype(o_ref.dtype)


def _mla_attention(q, k, vt):
    b, h, s, _ = q.shape
    tv = vt.shape[-1]
    dv = vt.shape[-2]
    t = min(512, s)
    assert h % 2 == 0 and 2 * MLA_V == LANES and t % tv == 0
    return pl.pallas_call(
        functools.partial(_mla_kernel, t=t, nsub=t // tv),
        grid=(b, h // 2, s // t),
        in_specs=[pl.BlockSpec((1, 2, t, LANES), lambda bi, hi, i: (bi, hi, i, 0)),
                  pl.BlockSpec((1, 2, s, LANES), lambda bi, hi, i: (bi, hi, 0, 0)),
                  pl.BlockSpec((1, 2, s // tv, dv, tv), lambda bi, hi, i: (bi, hi, 0, 0, 0))],
        out_specs=pl.BlockSpec((1, t, LANES), lambda bi, hi, i: (bi, i, hi)),
        out_shape=jax.ShapeDtypeStruct((b, s, h * MLA_V), _MXU),
        scratch_shapes=[pltpu.VMEM((t, t), _F32), pltpu.VMEM((t, t), _F32)],
        compiler_params=_params("parallel", "parallel", "arbitrary"),
        name="mla_attn",
    )(q, k, vt)


def _sb_kernel(q_ref, k_ref, vt_ref, u_ref, o_ref, z_ref, lb_ref, hi_ref, lo_ref, a_ref, *, t):
    qi = pl.program_id(2)
    u = u_ref[...]

    def step(j, carry, diag):
        off = pl.multiple_of(j * t, t)
        if diag:
            key = lax.broadcasted_iota(jnp.int32, (t, t), 0)
            qry = lax.broadcasted_iota(jnp.int32, (t, t), 1)
            strict = key < qry
        for hh in range(2):
            z_ref[hh] = _dot_nt(k_ref[0, hh, pl.ds(off, t), :], q_ref[0, hh])
        first_rem = []
        for hh in range(2):
            z = z_ref[hh]
            log_beta = jnp.minimum(z, 0.0) - jnp.log2(1.0 + jnp.exp2(-jnp.abs(z)))
            log_rem = log_beta - z
            if diag:
                log_rem = jnp.where(strict, log_rem, 0.0)
            hi = log_rem.astype(_MXU)
            lb_ref[hh] = log_beta
            hi_ref[hh] = hi
            lo_ref[hh] = (log_rem - hi.astype(_F32)).astype(_MXU)
            first_rem.append(log_rem[0:1, :])
        out = []
        for hh in range(2):
            rem, acc = carry[hh]
            suffix = (jnp.dot(u, hi_ref[hh], preferred_element_type=_F32)
                      + jnp.dot(u, lo_ref[hh], preferred_element_type=_F32))
            a = jnp.exp2(lb_ref[hh] + suffix + rem)
            if diag:
                a = jnp.where(strict, a, 0.0)
            a_ref[hh] = a.astype(_MXU)
            out.append((rem + suffix[0:1, :] + first_rem[hh], acc))
        alive = jnp.max(jnp.maximum(out[0][0], out[1][0])) > F32_EXP2_ZERO
        out = tuple((rem, acc + jnp.dot(vt_ref[0, hh, j], a_ref[hh], preferred_element_type=_F32))
                    for hh, (rem, acc) in enumerate(out))
        return alive, out

    init = tuple((jnp.zeros((1, t), _F32), jnp.zeros((HEAD_DIM, t), _F32)) for _ in range(2))
    alive, carry = step(qi, init, True)

    def earlier(c):
        return (c[0] - 1,) + step(c[0], c[2], False)

    _, _, carry = lax.while_loop(lambda c: jnp.logical_and(c[0] >= 0, c[1]), earlier, (qi - 1, alive, carry))
    o_ref[0] = jnp.concatenate([acc for _, acc in carry], axis=0).T.astype(o_ref.dtype)


def _sb_attention(q, k, vt):
    b, h, s, d = q.shape
    t = vt.shape[-1]
    assert h % 2 == 0 and 2 * d == LANES and s % t == 0
    idx = np.arange(t)
    u = jnp.asarray(idx[None, :] > idx[:, None], _MXU)
    return pl.pallas_call(
        functools.partial(_sb_kernel, t=t),
        grid=(b, h // 2, s // t),
        in_specs=[pl.BlockSpec((1, 2, t, d), lambda bi, hi, i: (bi, hi, i, 0)),
                  pl.BlockSpec((1, 2, s, d), lambda bi, hi, i: (bi, hi, 0, 0)),
                  pl.BlockSpec((1, 2, s // t, d, t), lambda bi, hi, i: (bi, hi, 0, 0, 0)),
                  pl.BlockSpec((t, t), lambda bi, hi, i: (0, 0))],
        out_specs=pl.BlockSpec((1, t, LANES), lambda bi, hi, i: (bi, i, hi)),
        out_shape=jax.ShapeDtypeStruct((b, s, h * d), _MXU),
        scratch_shapes=[pltpu.VMEM((2, t, t), _F32), pltpu.VMEM((2, t, t), _F32), pltpu.VMEM((2, t, t), _MXU),
                        pltpu.VMEM((2, t, t), _MXU), pltpu.VMEM((2, t, t), _MXU)],
        compiler_params=_params("parallel", "parallel", "arbitrary"),
        name="sb_attn",
    )(q, k, vt, u)


def _compress_kernel(xk_ref, xv_ref, w1k_ref, w2k_ref, pk_ref, w1v_ref, w2v_ref, pv_ref, ok_ref, ov_ref):
    def hidden(x_ref, w1_ref, p_ref):
        x = x_ref[0, 0]
        n = x.shape[0]
        first = jnp.dot(x, w1_ref[0], preferred_element_type=_F32)
        second = jnp.dot(x, w1_ref[1], preferred_element_type=_F32)
        pos = _dot(p_ref[0], w1_ref[0]) + _dot(p_ref[1], w1_ref[1])
        hid = first + pltpu.roll(second, n - 1, 0) + pos[0:1]
        return 0.5 * hid * (1.0 + jnp.tanh(math.sqrt(2.0 / math.pi) * (hid + 0.044715 * hid * hid * hid)))

    ok_ref[0, 0] = _dot(hidden(xk_ref, w1k_ref, pk_ref), w2k_ref[...]).astype(ok_ref.dtype)
    ov_ref[0, 0] = _dot_nt(w2v_ref[...], hidden(xv_ref, w1v_ref, pv_ref)).astype(ov_ref.dtype)


def _compress(xk, xv, layer, w1k, w2k, pk, w1v, w2v, pv):
    b, g, s, d = xk.shape
    n = s // CMP_STRIDE
    xk = xk.reshape(b, g, n, CMP_STRIDE * d)
    xv = xv.reshape(b, g, n, CMP_STRIDE * d)
    full = functools.partial(_layer_spec, layer=layer)

    xspec = pl.BlockSpec((1, 1, n, CMP_STRIDE * d), lambda bi, gi: (bi, gi, 0, 0))
    return pl.pallas_call(
        _compress_kernel,
        grid=(b, g),
        in_specs=[xspec, xspec, full(w1k), full(w2k), full(pk), full(w1v), full(w2v), full(pv)],
        out_specs=[pl.BlockSpec((1, 1, n, d), lambda bi, gi: (bi, gi, 0, 0)),
                   pl.BlockSpec((1, 1, d, n), lambda bi, gi: (bi, gi, 0, 0))],
        out_shape=[jax.ShapeDtypeStruct((b, g, n, d), _MXU), jax.ShapeDtypeStruct((b, g, d, n), _MXU)],
        compiler_params=_params("parallel", "parallel"),
        name="nsa_compress",
    )(xk, xv, w1k, w2k, pk, w1v, w2v, pv)


def _group_queries(q_ref, g, tq):
    return q_ref[0, g * NSA_GROUP:(g + 1) * NSA_GROUP].reshape(NSA_GROUP * tq, q_ref.shape[-1])


def _gated_heads(ot, gt_ref, g, branch, tq):
    out = []
    for r in range(NSA_GROUP):
        row = NSA_BRANCHES * (g * NSA_GROUP + r) + branch
        out.append(ot[:, r * tq:(r + 1) * tq] * gt_ref[0, row:row + 1, :])
    return out


def _cmp_kernel(q_ref, kc_ref, vct_ref, ov_ref, gt_ref, o_ref, qa_ref, s0_ref, s1_ref, *, tq, n_top):
    q0 = pl.program_id(1) * tq
    ncp = kc_ref.shape[2]
    ns = ov_ref.shape[0]
    lanes = NSA_GROUP * tq
    s_refs = (s0_ref, s1_ref)
    for g in range(NSA_KV_HEADS):
        s_refs[g][...] = _dot_nt(kc_ref[0, g], _group_queries(q_ref, g, tq))
    qpos = q0 + (lax.broadcasted_iota(jnp.int32, (1, lanes), 1) & (tq - 1))
    cmp_end = lax.broadcasted_iota(jnp.int32, (ncp, 1), 0) * CMP_STRIDE + (CMP_LEN - 1)
    visible = cmp_end <= qpos
    cur = jnp.right_shift(q0 + lax.broadcasted_iota(jnp.int32, (1, tq), 1), int(math.log2(SEL_LEN)))
    blk = lax.broadcasted_iota(jnp.int32, (ns, 1), 0)
    forced = (blk == 0) | (blk == cur) | (blk == cur - 1)
    future = blk > cur
    blk_f = blk.astype(_F32)
    heads = []
    scores = []
    for g in range(NSA_KV_HEADS):
        st = jnp.where(visible, s_refs[g][...], NEG_INF)
        e = jnp.exp2(st - jnp.max(st, axis=0, keepdims=True))
        inv = jnp.where(qpos >= CMP_LEN - 1, 1.0 / jnp.sum(e, axis=0, keepdims=True), 0.0)
        pt = e * inv
        heads += _gated_heads(_dot(vct_ref[0, g], pt), gt_ref, g, 0, tq)
        p_sum = sum(pt[:, r * tq:(r + 1) * tq] for r in range(NSA_GROUP))
        score = _dot_split_rhs(ov_ref[...], p_sum)
        scores.append(jnp.where(forced, FORCE_SCORE, jnp.where(future, -1.0, score)))
    o_ref[0] = jnp.concatenate(heads, axis=0).T
    unselected = [jnp.full((ns, tq), -1.0, _F32) for _ in range(NSA_KV_HEADS)]
    for _ in range(n_top):
        for g in range(NSA_KV_HEADS):
            top = jnp.max(scores[g], axis=0, keepdims=True)
            first = jnp.min(jnp.where(scores[g] == top, blk_f, float(ns)), axis=0, keepdims=True)
            pick = blk_f == first
            unselected[g] = jnp.where(pick, 0.0, unselected[g])
            scores[g] = jnp.where(pick, PICKED, scores[g])
    for g in range(NSA_KV_HEADS):
        sel_m1 = unselected[g].T.astype(qa_ref.dtype)
        for h in range(g * NSA_GROUP, (g + 1) * NSA_GROUP):
            qa_ref[0, h, :, 0:ns] = sel_m1
            qa_ref[0, h, :, ns:ns + HEAD_DIM] = q_ref[0, h]


def _cmp_select(q, kc, vct, gates_t):
    b, h, s, d = q.shape
    g = kc.shape[1]
    ncp = kc.shape[2]
    ns = s // SEL_LEN
    n_top = min(SEL_TOPK, ns)
    tq = min(256, s)
    assert tq & (tq - 1) == 0 and g == 2
    c0 = np.arange(ncp)[:, None] * CMP_STRIDE
    n0 = np.arange(ns)[None, :] * SEL_LEN
    overlap = jnp.asarray(((c0 < n0 + SEL_LEN) & (c0 + CMP_LEN > n0)).T, _MXU)
    return pl.pallas_call(
        functools.partial(_cmp_kernel, tq=tq, n_top=n_top),
        grid=(b, s // tq),
        in_specs=[pl.BlockSpec((1, h, tq, d), lambda bi, i: (bi, 0, i, 0)),
                  pl.BlockSpec((1, g, ncp, d), lambda bi, i: (bi, 0, 0, 0)),
                  pl.BlockSpec((1, g, d, ncp), lambda bi, i: (bi, 0, 0, 0)),
                  pl.BlockSpec((ns, ncp), lambda bi, i: (0, 0)),
                  pl.BlockSpec((1, GATE_ROWS, tq), lambda bi, i: (bi, 0, i))],
        out_specs=[pl.BlockSpec((1, tq, h * d), lambda bi, i: (bi, i, 0)),
                   pl.BlockSpec((1, h, tq, ns + d), lambda bi, i: (bi, 0, i, 0))],
        out_shape=[jax.ShapeDtypeStruct((b, s, h * d), _F32), jax.ShapeDtypeStruct((b, h, s, ns + d), _MXU)],
        scratch_shapes=[pltpu.VMEM((ncp, NSA_GROUP * tq), _F32) for _ in range(g)],
        compiler_params=_params("parallel", "arbitrary"),
        name="nsa_cmp_select",
    )(q, kc, vct, overlap, gates_t)


def _sel_kernel(q_ref, k_ref, vt_ref, gt_ref, o_ref, s0_ref, s1_ref, *, tq, tk, nsub):
    q0 = pl.program_id(1) * tq
    last = (q0 + tq - 1) // tk
    lanes = NSA_GROUP * tq
    s_refs = (s0_ref, s1_ref)

    def qk(g, j):
        off = pl.multiple_of(j * tk, tk)
        s_refs[g][...] = _dot_nt(k_ref[0, g, pl.ds(off, tk), :], _group_queries(q_ref, g, tq))

    def soft(g, j, carry, causal):
        st = s_refs[g][...]
        if causal:
            key = j * tk + lax.broadcasted_iota(jnp.int32, (tk, lanes), 0)
            qry = q0 + (lax.broadcasted_iota(jnp.int32, (tk, lanes), 1) & (tq - 1))
            st = jnp.where(key <= qry, st, NEG_INF)
        return _softmax_step_t(carry, st, [vt_ref[0, g, j * nsub + c] for c in range(nsub)])

    init = tuple(_softmax_init_t(HEAD_DIM, lanes) for _ in range(NSA_KV_HEADS))
    carry = _two_chain_sweep(last, qk, soft, init)
    heads = []
    for g in range(NSA_KV_HEADS):
        heads += _gated_heads(_softmax_finish_t(carry[g], HEAD_DIM), gt_ref, g, 1, tq)
    o_ref[0] = jnp.concatenate(heads, axis=0).T


def _sel_attention(q, k, vt, gates_t):
    b, h, s, da = q.shape
    g = k.shape[1]
    d = HEAD_DIM
    tv = vt.shape[-1]
    tq = min(256, s)
    tk = min(512, s)
    assert tq & (tq - 1) == 0 and s % tk == 0 and tk % tv == 0 and g == 2
    return pl.pallas_call(
        functools.partial(_sel_kernel, tq=tq, tk=tk, nsub=tk // tv),
        grid=(b, s // tq),
        in_specs=[pl.BlockSpec((1, h, tq, da), lambda bi, i: (bi, 0, i, 0)),
                  pl.BlockSpec((1, g, s, da), lambda bi, i: (bi, 0, 0, 0)),
                  pl.BlockSpec((1, g) + vt.shape[2:], lambda bi, i: (bi, 0, 0, 0, 0)),
                  pl.BlockSpec((1, GATE_ROWS, tq), lambda bi, i: (bi, 0, i))],
        out_specs=pl.BlockSpec((1, tq, h * d), lambda bi, i: (bi, i, 0)),
        out_shape=jax.ShapeDtypeStruct((b, s, h * d), _F32),
        scratch_shapes=[pltpu.VMEM((tk, NSA_GROUP * tq), _F32) for _ in range(g)],
        compiler_params=_params("parallel", "arbitrary"),
        name="nsa_selected",
    )(q, k, vt, gates_t)


def _win_kernel(q_ref, k_ref, vt_ref, gt_ref, o_ref, s0_ref, s1_ref, *, tq, span, tv):
    q0 = pl.program_id(1) * tq
    start = pl.multiple_of(jnp.maximum(q0 - WINDOW, 0), tq)
    first_chunk = start // tv
    lanes = NSA_GROUP * tq
    s_refs = (s0_ref, s1_ref)
    for g in range(NSA_KV_HEADS):
        s_refs[g][...] = _dot_nt(k_ref[0, g, pl.ds(start, span), :], _group_queries(q_ref, g, tq))
    key = start + lax.broadcasted_iota(jnp.int32, (span, lanes), 0)
    qry = q0 + (lax.broadcasted_iota(jnp.int32, (span, lanes), 1) & (tq - 1))
    heads = []
    for g in range(NSA_KV_HEADS):
        st = jnp.where(key <= qry, s_refs[g][...], NEG_INF)
        st = jnp.where(key > qry - WINDOW, st, NEG_INF)
        carry = _softmax_step_t(_softmax_init_t(HEAD_DIM, lanes), st,
                                [vt_ref[0, g, first_chunk + c] for c in range(span // tv)])
        heads += _gated_heads(_softmax_finish_t(carry, HEAD_DIM), gt_ref, g, 2, tq)
    o_ref[0] = jnp.concatenate(heads, axis=0).T


def _win_attention(q, k, vt, gates_t):
    b, h, s, d = q.shape
    g = k.shape[1]
    tv = vt.shape[-1]
    tq = min(256, s)
    span = WINDOW + tq
    assert tq & (tq - 1) == 0 and s >= span and tq % tv == 0 and WINDOW % tv == 0 and g == 2
    return pl.pallas_call(
        functools.partial(_win_kernel, tq=tq, span=span, tv=tv),
        grid=(b, s // tq),
        in_specs=[pl.BlockSpec((1, h, tq, d), lambda bi, i: (bi, 0, i, 0)),
                  pl.BlockSpec((1, g, s, d), lambda bi, i: (bi, 0, 0, 0)),
                  pl.BlockSpec((1, g) + vt.shape[2:], lambda bi, i: (bi, 0, 0, 0, 0)),
                  pl.BlockSpec((1, GATE_ROWS, tq), lambda bi, i: (bi, 0, i))],
        out_specs=pl.BlockSpec((1, tq, h * d), lambda bi, i: (bi, i, 0)),
        out_shape=jax.ShapeDtypeStruct((b, s, h * d), _F32),
        scratch_shapes=[pltpu.VMEM((span, NSA_GROUP * tq), _F32) for _ in range(g)],
        compiler_params=_params("parallel", "arbitrary"),
        name="nsa_window",
    )(q, k, vt, gates_t)


def _out_kernel(x_ref, mla_ref, cmp_ref, sel_ref, win_ref, sb_ref, w_ref, o_ref):
    def w_rows(first_head, n_heads):
        return w_ref[first_head * HEAD_DIM:(first_head + n_heads) * HEAD_DIM, :]

    acc = x_ref[0] + jnp.dot(mla_ref[0], w_rows(0, MLA_HEADS), preferred_element_type=_F32)
    nsa = cmp_ref[0] + sel_ref[0] + win_ref[0]
    acc = acc + _dot(nsa, w_rows(MLA_HEADS, NSA_HEADS))
    o_ref[0] = acc + jnp.dot(sb_ref[0], w_rows(MLA_HEADS + NSA_HEADS, SB_HEADS), preferred_element_type=_F32)


def _out_proj(x, o_mla, o_cmp, o_sel, o_win, o_sb, layer, w_heads):
    b, s, _ = x.shape
    ts = min(512, s)

    def rows(a):
        return pl.BlockSpec((1, ts, a.shape[2]), lambda bi, i: (bi, i, 0))

    xspec = pl.BlockSpec((1, ts, D_MODEL), lambda bi, i: (bi, i, 0))
    return pl.pallas_call(
        _out_kernel,
        grid=(b, s // ts),
        in_specs=[xspec, rows(o_mla), rows(o_cmp), rows(o_sel), rows(o_win), rows(o_sb),
                  _layer_spec(w_heads, layer)],
        out_specs=xspec,
        out_shape=jax.ShapeDtypeStruct(x.shape, _F32),
        compiler_params=_params("parallel", "parallel"),
        name="out_proj",
    )(x, o_mla, o_cmp, o_sel, o_win, o_sb, w_heads)


def _gather_cols(w, idx):
    idx = np.asarray(idx)
    cols = jnp.take(w, jnp.asarray(np.maximum(idx, 0)), axis=-1)
    return jnp.where(jnp.asarray(idx >= 0), cols, 0.0).astype(_MXU)


def _swap_halves(rot):
    return (np.arange(rot) + rot // 2) % rot


def _w_in_index():
    idx = np.full((_N_HEAD_COLS * HEAD_DIM,), -1, np.int64)

    def put(col, src):
        src = np.asarray(src)
        idx[col:col + len(src)] = src

    def put_head(pos, src):
        put(pos * HEAD_DIM, src)

    put(_S_CQ * LANES, _O_CQ + np.arange(MLA_Q_LORA))
    put(_S_CKV * LANES, _O_CKV + np.arange(MLA_KV_LORA))
    put(_S_KR * LANES + MLA_NOPE, _O_KR + np.arange(MLA_ROPE))
    put(_S_KRS * LANES + MLA_NOPE, _O_KR + _swap_halves(MLA_ROPE))
    for h in range(NSA_HEADS):
        put_head(_H_NQ + h, _O_NQ + h * HEAD_DIM + np.arange(HEAD_DIM))
        put_head(_H_NQS + h, _O_NQ + h * HEAD_DIM + _swap_halves(PARTIAL_ROT))
    for hk, hks, ok in ((_H_KC, _H_KCS, _O_NKC), (_H_KS, _H_KSS, _O_NKS), (_H_KW, _H_KWS, _O_NKW)):
        for g in range(NSA_KV_HEADS):
            put_head(hk + g, ok + g * HEAD_DIM + np.arange(HEAD_DIM))
            put_head(hks + g, ok + g * HEAD_DIM + _swap_halves(PARTIAL_ROT))
    for g in range(NSA_KV_HEADS):
        put_head(_H_VC + g, _O_NVC + g * HEAD_DIM + np.arange(HEAD_DIM))
    for h in range(SB_HEADS):
        put_head(_H_SBQ + h, _O_SBQ + h * HEAD_DIM + np.arange(HEAD_DIM))
        put_head(_H_SBK + h, _O_SBK + h * HEAD_DIM + np.arange(HEAD_DIM))
    return idx


def _mla_up_index():
    qd = MLA_NOPE + MLA_ROPE
    kd = MLA_NOPE + MLA_V
    uq = np.full((MLA_HEADS * LANES,), -1, np.int64)
    uqs = uq.copy()
    uk = uq.copy()
    for h in range(MLA_HEADS):
        uq[h * LANES:h * LANES + qd] = h * qd + np.arange(qd)
        uqs[h * LANES + MLA_NOPE:h * LANES + qd] = h * qd + MLA_NOPE + _swap_halves(MLA_ROPE)
        uk[h * LANES:h * LANES + MLA_NOPE] = h * kd + np.arange(MLA_NOPE)
    return uq, uqs, uk


def _transposed_weights(w_in, gate_bias):
    width = NSA_KV_HEADS * HEAD_DIM
    gate_rows = jnp.pad(w_in[..., _O_GATE:_O_GATE + N_GATES], ((0, 0), (0, 0), (0, _T_SBV - _T_GATE - N_GATES)))
    rows = jnp.concatenate([w_in[..., _O_NVS:_O_NVS + width], w_in[..., _O_NVW:_O_NVW + width], gate_rows,
                            w_in[..., _O_SBV:_O_SBV + SB_HEADS * HEAD_DIM]], axis=-1)
    bias = jnp.pad(gate_bias, ((0, 0), (0, GATE_ROWS - N_GATES)))[..., None]
    return jnp.swapaxes(rows, -1, -2).astype(_MXU), bias


def _rope_tables(s):
    pos = jnp.arange(s, dtype=_F32)

    def cs(rot):
        half = rot // 2
        inv_freq = ROPE_THETA ** (-jnp.arange(half, dtype=_F32) / half)
        ang = pos[:, None] * inv_freq[None, :]
        c, sn = jnp.cos(ang), jnp.sin(ang)
        return jnp.concatenate([c, c], axis=1), jnp.concatenate([-sn, sn], axis=1)

    c, sn = cs(MLA_ROPE)
    ones = jnp.ones((s, MLA_NOPE), _F32)
    zeros = jnp.zeros((s, MLA_NOPE), _F32)
    pad = jnp.zeros((s, LANES - MLA_NOPE - MLA_ROPE), _F32)
    ck = jnp.concatenate([ones, c, pad], axis=1)
    sk = jnp.concatenate([zeros, sn, pad], axis=1)
    q_scale = (MLA_NOPE + MLA_ROPE) ** -0.5 * LOG2_E
    c, sn = cs(PARTIAL_ROT)
    c64 = jnp.concatenate([c, jnp.ones((s, HEAD_DIM - PARTIAL_ROT), _F32)], axis=1)
    s64 = jnp.concatenate([sn, jnp.zeros((s, HEAD_DIM - PARTIAL_ROT), _F32)], axis=1)
    ns = s // SEL_LEN
    onehot = (np.arange(s)[:, None] // SEL_LEN == np.arange(ns)[None, :]) * -NEG_INF
    return ck * q_scale, sk * q_scale, ck, sk, c64, s64, jnp.asarray(onehot, _MXU)


def kernel(x, ffn1_norm, ffn1_w_gate, ffn1_w_up, ffn1_w_down, mix_norm, w_in, mla_q_norm, mla_w_uq, mla_kv_norm,
           mla_w_ukv, nsa_gate_bias, nsa_cmp_pos_k, nsa_cmp_w1_k, nsa_cmp_w2_k, nsa_cmp_pos_v, nsa_cmp_w1_v,
           nsa_cmp_w2_v, w_out, ffn2_norm, ffn2_w_gate, ffn2_w_up, ffn2_w_down, final_norm):
    b, s, d = x.shape
    depth = w_in.shape[0]
    tabs = _rope_tables(s)
    in_idx = _w_in_index()
    uq_idx, uqs_idx, uk_idx = _mla_up_index()
    half = CMP_LEN * HEAD_DIM // 2
    fg = final_norm.reshape(1, d)

    def row(p):
        return p[:, None, :]

    def cmp_weights(w1, w2, pos, transpose_out):
        pos = jnp.broadcast_to(pos.reshape(depth, 2, 1, half), (depth, 2, 8, half)).astype(_MXU)
        w2 = jnp.swapaxes(w2, -1, -2) if transpose_out else w2
        return w1.reshape(depth, 2, half, CMP_HIDDEN).astype(_MXU), w2.astype(_MXU), pos

    ffn1 = [w.astype(_MXU) for w in (ffn1_w_gate, ffn1_w_up, ffn1_w_down)]
    ffn2 = [w.astype(_MXU) for w in (ffn2_w_gate, ffn2_w_up, ffn2_w_down)]
    w_t, gate_bias = _transposed_weights(w_in, nsa_gate_bias)
    wuv_t = mla_w_ukv.reshape(depth, MLA_KV_LORA, MLA_HEADS, 2, MLA_V)[:, :, :, 1].transpose(0, 2, 3, 1).astype(_MXU)
    proj_params = (row(mix_norm), _gather_cols(w_in, in_idx), w_t,
                   row(mla_q_norm), _gather_cols(mla_w_uq, uq_idx), _gather_cols(mla_w_uq, uqs_idx),
                   row(mla_kv_norm), _gather_cols(mla_w_ukv, uk_idx), wuv_t, gate_bias)
    cmp_params = (cmp_weights(nsa_cmp_w1_k, nsa_cmp_w2_k, nsa_cmp_pos_k, False)
                  + cmp_weights(nsa_cmp_w1_v, nsa_cmp_w2_v, nsa_cmp_pos_v, True))
    w_out = w_out.astype(_MXU)
    ffn1_norm, ffn2_norm = row(ffn1_norm), row(ffn2_norm)

    for l in range(depth):
        x2d = _ffn(x.reshape(b * s, d), ffn1_norm, *ffn1, l, fg, False)
        x = x2d.reshape(b, s, d)
        (mq, mk, mvt, nq, nkc, nvc, nks, nvst, nkw, nvwt, gates_t, sbq, sbk, sbv) = _proj(x, l, *proj_params, tabs)
        o_mla = _mla_attention(mq, mk, mvt)
        kc, vct = _compress(nkc, nvc, l, *cmp_params)
        o_cmp, q_sel = _cmp_select(nq, kc, vct, gates_t)
        o_sel = _sel_attention(q_sel, nks, nvst, gates_t)
        o_win = _win_attention(nq, nkw, nvwt, gates_t)
        o_sb = _sb_attention(sbq, sbk, sbv)
        x = _out_proj(x, o_mla, o_cmp, o_sel, o_win, o_sb, l, w_out)
        x2d = _ffn(x.reshape(b * s, d), ffn2_norm, *ffn2, l, fg, l == depth - 1)
        x = x2d.reshape(b, s, d)
    return x
```

```python
import functools
import math

import numpy as np
import jax
import jax.numpy as jnp
from jax import lax
from jax.experimental import pallas as pl
from jax.experimental.pallas import tpu as pltpu

D_MODEL = 1024
HEAD_DIM = 64
MLA_HEADS = 6
MLA_NOPE = 64
MLA_ROPE = 32
MLA_V = 64
MLA_Q_LORA = 256
MLA_KV_LORA = 128
NSA_HEADS = 6
NSA_KV_HEADS = 2
NSA_GROUP = NSA_HEADS // NSA_KV_HEADS
NSA_BRANCHES = 3
CMP_LEN = 32
CMP_STRIDE = 16
CMP_HIDDEN = 128
SEL_LEN = 64
SEL_TOPK = 16
WINDOW = 512
SB_HEADS = 4
D_FF = 2816
ROPE_THETA = 500000.0
PARTIAL_ROT = HEAD_DIM // 4
EPS = 1e-6
NEG_INF = -1e30
M_FLOOR = 0.1 * NEG_INF
FORCE_SCORE = 1e4
PICKED = -3e38
F32_EXP2_ZERO = -151.0
LOG2_E = math.log2(math.e)
N_GATES = NSA_HEADS * NSA_BRANCHES

LANES = 128
FFN_CHUNK = 256
SWEEP_UNROLL = 4
TOKEN_CHUNK = 256
ONES_PAD = 16
VMEM_LIMIT = 56 * 1024 * 1024

_MXU = jnp.bfloat16
_F32 = jnp.float32

_IN_WIDTHS = (MLA_Q_LORA, MLA_KV_LORA, MLA_ROPE, NSA_HEADS * HEAD_DIM) + (NSA_KV_HEADS * HEAD_DIM,) * 6 + (
    N_GATES, SB_HEADS * HEAD_DIM, SB_HEADS * HEAD_DIM, SB_HEADS * HEAD_DIM)
_IN_OFF = np.concatenate([[0], np.cumsum(_IN_WIDTHS)])
(_O_CQ, _O_CKV, _O_KR, _O_NQ, _O_NKC, _O_NVC, _O_NKS, _O_NVS, _O_NKW, _O_NVW, _O_GATE, _O_SBQ, _O_SBK,
 _O_SBV) = [int(v) for v in _IN_OFF[:-1]]

_S_CQ, _S_CKV, _S_KR, _S_KRS = 0, 2, 3, 4
_H_NQ, _H_NQS = 10, 16
_H_KC, _H_KCS, _H_VC = 22, 24, 26
_H_KS, _H_KSS = 28, 30
_H_KW, _H_KWS = 32, 34
_H_SBQ, _H_SBK = 36, 40
_N_HEAD_COLS = 44
_T_VS, _T_VW, _T_GATE = 0, NSA_KV_HEADS * HEAD_DIM, 2 * NSA_KV_HEADS * HEAD_DIM
GATE_ROWS = 24
_T_SBV = _T_GATE + 32
_T_ROWS = _T_SBV + SB_HEADS * HEAD_DIM


def _dot(a, b):
    return jnp.dot(a.astype(_MXU), b.astype(_MXU), preferred_element_type=_F32)


def _dot_nt(a, b):
    return lax.dot_general(a.astype(_MXU), b.astype(_MXU), (((1,), (1,)), ((), ())),
                           preferred_element_type=_F32)


def _dot_split_rhs(a, b):
    hi = b.astype(_MXU)
    lo = (b - hi.astype(_F32)).astype(_MXU)
    return (jnp.dot(a, hi, preferred_element_type=_F32) + jnp.dot(a, lo, preferred_element_type=_F32))


def _rms(x, g):
    return x * lax.rsqrt(jnp.mean(x * x, axis=-1, keepdims=True) + EPS) * g


def _params(*sem):
    return pltpu.CompilerParams(dimension_semantics=sem, vmem_limit_bytes=VMEM_LIMIT)


def _layer_spec(a, layer):
    return pl.BlockSpec((None,) + a.shape[1:], lambda *_: (layer,) + (0,) * (a.ndim - 1))


def _ffn_kernel(x_ref, g_ref, wg_ref, wu_ref, wd_ref, fg_ref, o_ref, h_ref, acc_ref, act_ref, *, final_norm):
    j = pl.program_id(1)

    @pl.when(j == 0)
    def _():
        h_ref[...] = _rms(x_ref[...], g_ref[...]).astype(h_ref.dtype)
        acc_ref[...] = jnp.zeros_like(acc_ref)

    h = h_ref[...]
    tf = act_ref.shape[1]
    for c0 in range(0, tf, FFN_CHUNK):
        c1 = min(c0 + FFN_CHUNK, tf)
        gate = jnp.dot(h, wg_ref[:, c0:c1], preferred_element_type=_F32)
        up = jnp.dot(h, wu_ref[:, c0:c1], preferred_element_type=_F32)
        act_ref[:, c0:c1] = (gate * jax.nn.sigmoid(gate) * up).astype(act_ref.dtype)
    acc_ref[...] += jnp.dot(act_ref[...], wd_ref[...], preferred_element_type=_F32)

    @pl.when(j == pl.num_programs(1) - 1)
    def _():
        y = x_ref[...] + 0.5 * acc_ref[...]
        if final_norm:
            y = _rms(y, fg_ref[...])
        o_ref[...] = y


def _ffn(x2d, g, wg, wu, wd, layer, fg, final_norm):
    rows = x2d.shape[0]
    tm = min(1024, rows)
    tf = D_FF // 2
    grid = (rows // tm, D_FF // tf)
    return pl.pallas_call(
        functools.partial(_ffn_kernel, final_norm=final_norm),
        grid=grid,
        in_specs=[
            pl.BlockSpec((tm, D_MODEL), lambda i, j: (i, 0)),
            _layer_spec(g, layer),
            pl.BlockSpec((None, D_MODEL, tf), lambda i, j: (layer, 0, j)),
            pl.BlockSpec((None, D_MODEL, tf), lambda i, j: (layer, 0, j)),
            pl.BlockSpec((None, tf, D_MODEL), lambda i, j: (layer, j, 0)),
            pl.BlockSpec((1, D_MODEL), lambda i, j: (0, 0)),
        ],
        out_specs=pl.BlockSpec((tm, D_MODEL), lambda i, j: (i, 0)),
        out_shape=jax.ShapeDtypeStruct((rows, D_MODEL), _F32),
        scratch_shapes=[pltpu.VMEM((tm, D_MODEL), _MXU), pltpu.VMEM((tm, D_MODEL), _F32), pltpu.VMEM((tm, tf), _MXU)],
        compiler_params=_params("parallel", "arbitrary"),
        name="ffn",
    )(x2d, g, wg, wu, wd, fg)


def _proj_kernel(x_ref, g_ref, w_ref, wt_ref, qn_ref, wuq_ref, wuqs_ref, kvn_ref, wuk_ref, wuv_ref, gb_ref,
                 cq_ref, sq_ref, ck_ref, sk_ref, c64_ref, s64_ref, oh_ref,
                 mq_ref, mk_ref, mv_ref, nq_ref, nkc_ref, nvc_ref, nks_ref, nvs_ref, nkw_ref, nvw_ref,
                 gate_ref, sbq_ref, sbk_ref, sbv_ref, stage_ref):
    hn = _rms(x_ref[0], g_ref[...]).astype(_MXU)

    def proj(h0, h1):
        return jnp.dot(hn, w_ref[:, h0 * HEAD_DIM:h1 * HEAD_DIM], preferred_element_type=_F32)

    def slot(p, s):
        return p[:, s * LANES:(s + 1) * LANES]

    def head(p, i):
        return p[:, i * HEAD_DIM:(i + 1) * HEAD_DIM]

    p = proj(0, _H_NQ)
    cq = _rms(p[:, :MLA_Q_LORA], qn_ref[...])
    ckv = _rms(slot(p, _S_CKV), kvn_ref[...])
    q = _dot(cq, wuq_ref[...])
    q_partner = _dot(cq, wuqs_ref[...])
    kpe = slot(p, _S_KR) * ck_ref[...] + slot(p, _S_KRS) * sk_ref[...]
    kn = _dot(ckv, wuk_ref[...])
    for h in range(MLA_HEADS):
        mq_ref[0, h] = (slot(q, h) * cq_ref[...] + slot(q_partner, h) * sq_ref[...]).astype(mq_ref.dtype)
        mk_ref[0, h] = (slot(kn, h) + kpe).astype(mk_ref.dtype)
        mv_ref[0, h, 0] = _ones_row_pad(_dot_nt(wuv_ref[h], ckv)).astype(mv_ref.dtype)

    c64 = c64_ref[...]
    s64 = s64_ref[...]
    scale = HEAD_DIM ** -0.5

    p = proj(_H_NQ, _H_KC)
    for h in range(NSA_HEADS):
        nq_ref[0, h] = ((head(p, h) * c64 + head(p, NSA_HEADS + h) * s64) * (scale * LOG2_E)).astype(nq_ref.dtype)

    p = proj(_H_KC, _H_SBQ)
    base = _H_KC
    ns = oh_ref.shape[-1]

    def roped(hk, hks, g):
        return head(p, hk - base + g) * c64 + head(p, hks - base + g) * s64

    def write_chunked(o_ref, g, val):
        stage_ref[...] = val
        for t in range(CMP_STRIDE):
            piece = stage_ref[pl.ds(t, val.shape[0] // CMP_STRIDE, stride=CMP_STRIDE), :]
            o_ref[0, g, :, t * HEAD_DIM:(t + 1) * HEAD_DIM] = piece.astype(o_ref.dtype)

    for g in range(NSA_KV_HEADS):
        write_chunked(nkc_ref, g, roped(_H_KC, _H_KCS, g))
        write_chunked(nvc_ref, g, head(p, _H_VC - base + g))
        nks_ref[0, g, :, 0:ns] = oh_ref[...]
        nks_ref[0, g, :, ns:ns + HEAD_DIM] = roped(_H_KS, _H_KSS, g).astype(nks_ref.dtype)
        nkw_ref[0, g] = roped(_H_KW, _H_KWS, g).astype(nkw_ref.dtype)

    pt = _dot_nt(wt_ref[...], hn)
    for g in range(NSA_KV_HEADS):
        lo = g * HEAD_DIM
        nvs_ref[0, g, 0] = _ones_row_pad(pt[_T_VS + lo:_T_VS + lo + HEAD_DIM]).astype(nvs_ref.dtype)
        nvw_ref[0, g, 0] = _ones_row_pad(pt[_T_VW + lo:_T_VW + lo + HEAD_DIM]).astype(nvw_ref.dtype)
    gate_ref[0] = jax.nn.sigmoid(pt[_T_GATE:_T_GATE + GATE_ROWS] + gb_ref[...])

    p = proj(_H_SBQ, _N_HEAD_COLS)
    for h in range(SB_HEADS):
        sbq_ref[0, h] = (head(p, h) * (scale * LOG2_E)).astype(sbq_ref.dtype)
        sbk_ref[0, h] = head(p, SB_HEADS + h).astype(sbk_ref.dtype)
        sbv_ref[0, h, 0] = pt[_T_SBV + h * HEAD_DIM:_T_SBV + (h + 1) * HEAD_DIM].astype(sbv_ref.dtype)


def _proj(x, layer, g, w_ext, w_t, qn, wuq, wuqs, kvn, wuk, wuv, gb, tabs):
    b, s, _ = x.shape
    ts = min(TOKEN_CHUNK, s)
    cq, sq, ck, sk, c64, s64, onehot = tabs
    full = functools.partial(_layer_spec, layer=layer)

    def tab(a):
        return pl.BlockSpec((ts, a.shape[1]), lambda bi, i: (i, 0))

    def heads(n, d):
        return (pl.BlockSpec((1, n, ts, d), lambda bi, i: (bi, 0, i, 0)),
                jax.ShapeDtypeStruct((b, n, s, d), _MXU))

    def values_t(n):
        return (pl.BlockSpec((1, n, 1, HEAD_DIM + ONES_PAD, ts), lambda bi, i: (bi, 0, i, 0, 0)),
                jax.ShapeDtypeStruct((b, n, s // ts, HEAD_DIM + ONES_PAD, ts), _MXU))

    def chunked():
        return (pl.BlockSpec((1, NSA_KV_HEADS, ts // CMP_STRIDE, CMP_STRIDE * HEAD_DIM), lambda bi, i: (bi, 0, i, 0)),
                jax.ShapeDtypeStruct((b, NSA_KV_HEADS, s // CMP_STRIDE, CMP_STRIDE * HEAD_DIM), _MXU))

    outs = [heads(MLA_HEADS, LANES), heads(MLA_HEADS, LANES), values_t(MLA_HEADS), heads(NSA_HEADS, HEAD_DIM),
            chunked(), chunked(), heads(NSA_KV_HEADS, onehot.shape[1] + HEAD_DIM), values_t(NSA_KV_HEADS),
            heads(NSA_KV_HEADS, HEAD_DIM), values_t(NSA_KV_HEADS),
            (pl.BlockSpec((1, GATE_ROWS, ts), lambda bi, i: (bi, 0, i)), jax.ShapeDtypeStruct((b, GATE_ROWS, s), _F32)),
            heads(SB_HEADS, HEAD_DIM), heads(SB_HEADS, HEAD_DIM),
            (pl.BlockSpec((1, SB_HEADS, 1, HEAD_DIM, ts), lambda bi, i: (bi, 0, i, 0, 0)),
             jax.ShapeDtypeStruct((b, SB_HEADS, s // ts, HEAD_DIM, ts), _MXU))]
    return pl.pallas_call(
        _proj_kernel,
        grid=(b, s // ts),
        in_specs=[pl.BlockSpec((1, ts, D_MODEL), lambda bi, i: (bi, i, 0)), full(g), full(w_ext), full(w_t), full(qn),
                  full(wuq), full(wuqs), full(kvn), full(wuk), full(wuv), full(gb),
                  tab(cq), tab(sq), tab(ck), tab(sk), tab(c64), tab(s64), tab(onehot)],
        out_specs=[o[0] for o in outs],
        out_shape=[o[1] for o in outs],
        scratch_shapes=[pltpu.VMEM((ts, HEAD_DIM), _F32)],
        compiler_params=_params("parallel", "parallel"),
        name="proj",
    )(x, g, w_ext, w_t, qn, wuq, wuqs, kvn, wuk, wuv, gb, cq, sq, ck, sk, c64, s64, onehot)


def _ones_row_pad(vt):
    first = lax.broadcasted_iota(jnp.int32, (ONES_PAD, vt.shape[1]), 0) == 0
    return jnp.concatenate([vt, jnp.where(first, 1.0, 0.0).astype(vt.dtype)], axis=0)


def _softmax_step_t(carry, st, vt_chunks):
    m, acc = carry
    m_new = jnp.maximum(m, jnp.max(st, axis=0, keepdims=True))
    alpha = jnp.exp2(m - m_new)
    pt = jnp.exp2(st - m_new).astype(_MXU)
    n = st.shape[0] // len(vt_chunks)
    pv = sum(jnp.dot(vt, pt[c * n:(c + 1) * n], preferred_element_type=_F32) for c, vt in enumerate(vt_chunks))
    return m_new, alpha * acc + pv


def _softmax_init_t(d, cols):
    return (jnp.full((1, cols), M_FLOOR, _F32), jnp.zeros((d + ONES_PAD, cols), _F32))


def _softmax_finish_t(carry, d):
    _, acc = carry
    return acc[:d] * (1.0 / acc[d:d + 1])


def _two_chain_sweep(n_full, qk, soft, init):
    def body(j, carry, diag=False):
        c0, c1 = carry
        qk(0, j)
        c1 = soft(1, j, c1, diag)
        qk(1, jnp.zeros_like(j) if diag else j + 1)
        c0 = soft(0, j, c0, diag)
        return c0, c1

    def unrolled(i, carry):
        for u in range(SWEEP_UNROLL):
            carry = body(SWEEP_UNROLL * i + u, carry)
        return carry

    qk(1, n_full)
    carry = body(n_full, init, True)
    trips = n_full // SWEEP_UNROLL
    carry = lax.fori_loop(0, trips, unrolled, carry)
    return lax.fori_loop(SWEEP_UNROLL * trips, n_full, body, carry)


def _mla_kernel(q_ref, k_ref, vt_ref, o_ref, s0_ref, s1_ref, *, t, nsub):
    qi = pl.program_id(2)
    s_refs = (s0_ref, s1_ref)

    def qk(hh, j):
        off = pl.multiple_of(j * t, t)
        s_refs[hh][...] = _dot_nt(k_ref[0, hh, pl.ds(off, t), :], q_ref[0, hh])

    def soft(hh, j, carry, diag):
        st = s_refs[hh][...]
        if diag:
            key = lax.broadcasted_iota(jnp.int32, (t, t), 0)
            qry = lax.broadcasted_iota(jnp.int32, (t, t), 1)
            st = jnp.where(key <= qry, st, NEG_INF)
        return _softmax_step_t(carry, st, [vt_ref[0, hh, j * nsub + c] for c in range(nsub)])

    carry = _two_chain_sweep(qi, qk, soft, tuple(_softmax_init_t(MLA_V, t) for _ in range(2)))
    ot = jnp.concatenate([_softmax_finish_t(c, MLA_V) for c in carry], axis=0)
    o_ref[0] = ot.T.astype(o_ref.dtype)


def _mla_attention(q, k, vt):
    b, h, s, _ = q.shape
    tv = vt.shape[-1]
    dv = vt.shape[-2]
    t = min(512, s)
    assert h % 2 == 0 and 2 * MLA_V == LANES and t % tv == 0
    return pl.pallas_call(
        functools.partial(_mla_kernel, t=t, nsub=t // tv),
        grid=(b, h // 2, s // t),
        in_specs=[pl.BlockSpec((1, 2, t, LANES), lambda bi, hi, i: (bi, hi, i, 0)),
                  pl.BlockSpec((1, 2, s, LANES), lambda bi, hi, i: (bi, hi, 0, 0)),
                  pl.BlockSpec((1, 2, s // tv, dv, tv), lambda bi, hi, i: (bi, hi, 0, 0, 0))],
        out_specs=pl.BlockSpec((1, t, LANES), lambda bi, hi, i: (bi, i, hi)),
        out_shape=jax.ShapeDtypeStruct((b, s, h * MLA_V), _MXU),
        scratch_shapes=[pltpu.VMEM((t, t), _F32), pltpu.VMEM((t, t), _F32)],
        compiler_params=_params("parallel", "parallel", "arbitrary"),
        name="mla_attn",
    )(q, k, vt)


def _sb_kernel(q_ref, k_ref, vt_ref, u_ref, o_ref, z_ref, lb_ref, hi_ref, lo_ref, a_ref, *, t):
    qi = pl.program_id(2)
    u = u_ref[...]

    def step(j, carry, diag):
        off = pl.multiple_of(j * t, t)
        if diag:
            key = lax.broadcasted_iota(jnp.int32, (t, t), 0)
            qry = lax.broadcasted_iota(jnp.int32, (t, t), 1)
            strict = key < qry
        for hh in range(2):
            z_ref[hh] = _dot_nt(k_ref[0, hh, pl.ds(off, t), :], q_ref[0, hh])
        first_rem = []
        for hh in range(2):
            z = z_ref[hh]
            log_beta = jnp.minimum(z, 0.0) - jnp.log2(1.0 + jnp.exp2(-jnp.abs(z)))
            log_rem = log_beta - z
            if diag:
                log_rem = jnp.where(strict, log_rem, 0.0)
            hi = log_rem.astype(_MXU)
            lb_ref[hh] = log_beta
            hi_ref[hh] = hi
            lo_ref[hh] = (log_rem - hi.astype(_F32)).astype(_MXU)
            first_rem.append(log_rem[0:1, :])
        out = []
        for hh in range(2):
            rem, acc = carry[hh]
            suffix = (jnp.dot(u, hi_ref[hh], preferred_element_type=_F32)
                      + jnp.dot(u, lo_ref[hh], preferred_element_type=_F32))
            a = jnp.exp2(lb_ref[hh] + suffix + rem)
            if diag:
                a = jnp.where(strict, a, 0.0)
            a_ref[hh] = a.astype(_MXU)
            out.append((rem + suffix[0:1, :] + first_rem[hh], acc))
        alive = jnp.max(jnp.maximum(out[0][0], out[1][0])) > F32_EXP2_ZERO
        out = tuple((rem, acc + jnp.dot(vt_ref[0, hh, j], a_ref[hh], preferred_element_type=_F32))
                    for hh, (rem, acc) in enumerate(out))
        return alive, out

    init = tuple((jnp.zeros((1, t), _F32), jnp.zeros((HEAD_DIM, t), _F32)) for _ in range(2))
    alive, carry = step(qi, init, True)

    def earlier(c):
        return (c[0] - 1,) + step(c[0], c[2], False)

    _, _, carry = lax.while_loop(lambda c: jnp.logical_and(c[0] >= 0, c[1]), earlier, (qi - 1, alive, carry))
    o_ref[0] = jnp.concatenate([acc for _, acc in carry], axis=0).T.astype(o_ref.dtype)


def _sb_attention(q, k, vt):
    b, h, s, d = q.shape
    t = vt.shape[-1]
    assert h % 2 == 0 and 2 * d == LANES and s % t == 0
    idx = np.arange(t)
    u = jnp.asarray(idx[None, :] > idx[:, None], _MXU)
    return pl.pallas_call(
        functools.partial(_sb_kernel, t=t),
        grid=(b, h // 2, s // t),
        in_specs=[pl.BlockSpec((1, 2, t, d), lambda bi, hi, i: (bi, hi, i, 0)),
                  pl.BlockSpec((1, 2, s, d), lambda bi, hi, i: (bi, hi, 0, 0)),
                  pl.BlockSpec((1, 2, s // t, d, t), lambda bi, hi, i: (bi, hi, 0, 0, 0)),
                  pl.BlockSpec((t, t), lambda bi, hi, i: (0, 0))],
        out_specs=pl.BlockSpec((1, t, LANES), lambda bi, hi, i: (bi, i, hi)),
        out_shape=jax.ShapeDtypeStruct((b, s, h * d), _MXU),
        scratch_shapes=[pltpu.VMEM((2, t, t), _F32), pltpu.VMEM((2, t, t), _F32), pltpu.VMEM((2, t, t), _MXU),
                        pltpu.VMEM((2, t, t), _MXU), pltpu.VMEM((2, t, t), _MXU)],
        compiler_params=_params("parallel", "parallel", "arbitrary"),
        name="sb_attn",
    )(q, k, vt, u)


def _compress_kernel(xk_ref, xv_ref, w1k_ref, w2k_ref, pk_ref, w1v_ref, w2v_ref, pv_ref, ok_ref, ov_ref):
    def hidden(x_ref, w1_ref, p_ref):
        x = x_ref[0, 0]
        n = x.shape[0]
        first = jnp.dot(x, w1_ref[0], preferred_element_type=_F32)
        second = jnp.dot(x, w1_ref[1], preferred_element_type=_F32)
        pos = _dot(p_ref[0], w1_ref[0]) + _dot(p_ref[1], w1_ref[1])
        hid = first + pltpu.roll(second, n - 1, 0) + pos[0:1]
        return 0.5 * hid * (1.0 + jnp.tanh(math.sqrt(2.0 / math.pi) * (hid + 0.044715 * hid * hid * hid)))

    ok_ref[0, 0] = _dot(hidden(xk_ref, w1k_ref, pk_ref), w2k_ref[...]).astype(ok_ref.dtype)
    ov_ref[0, 0] = _dot_nt(w2v_ref[...], hidden(xv_ref, w1v_ref, pv_ref)).astype(ov_ref.dtype)


def _compress(xk, xv, layer, w1k, w2k, pk, w1v, w2v, pv):
    b, g, n, _ = xk.shape
    d = HEAD_DIM
    full = functools.partial(_layer_spec, layer=layer)

    xspec = pl.BlockSpec((1, 1, n, CMP_STRIDE * d), lambda bi, gi: (bi, gi, 0, 0))
    return pl.pallas_call(
        _compress_kernel,
        grid=(b, g),
        in_specs=[xspec, xspec, full(w1k), full(w2k), full(pk), full(w1v), full(w2v), full(pv)],
        out_specs=[pl.BlockSpec((1, 1, n, d), lambda bi, gi: (bi, gi, 0, 0)),
                   pl.BlockSpec((1, 1, d, n), lambda bi, gi: (bi, gi, 0, 0))],
        out_shape=[jax.ShapeDtypeStruct((b, g, n, d), _MXU), jax.ShapeDtypeStruct((b, g, d, n), _MXU)],
        compiler_params=_params("parallel", "parallel"),
        name="nsa_compress",
    )(xk, xv, w1k, w2k, pk, w1v, w2v, pv)


def _group_queries(q_ref, g, tq):
    return q_ref[0, g * NSA_GROUP:(g + 1) * NSA_GROUP].reshape(NSA_GROUP * tq, q_ref.shape[-1])


def _gated_heads(ot, gt_ref, g, branch, tq):
    out = []
    for r in range(NSA_GROUP):
        row = NSA_BRANCHES * (g * NSA_GROUP + r) + branch
        out.append(ot[:, r * tq:(r + 1) * tq] * gt_ref[0, row:row + 1, :])
    return out


def _cmp_kernel(q_ref, kc_ref, vct_ref, ov_ref, gt_ref, o_ref, qa_ref, s0_ref, s1_ref, *, tq, n_top):
    q0 = pl.program_id(1) * tq
    ncp = kc_ref.shape[2]
    ns = ov_ref.shape[0]
    lanes = NSA_GROUP * tq
    s_refs = (s0_ref, s1_ref)
    for g in range(NSA_KV_HEADS):
        s_refs[g][...] = _dot_nt(kc_ref[0, g], _group_queries(q_ref, g, tq))
    qpos = q0 + (lax.broadcasted_iota(jnp.int32, (1, lanes), 1) & (tq - 1))
    cmp_end = lax.broadcasted_iota(jnp.int32, (ncp, 1), 0) * CMP_STRIDE + (CMP_LEN - 1)
    visible = cmp_end <= qpos
    cur = jnp.right_shift(q0 + lax.broadcasted_iota(jnp.int32, (1, tq), 1), int(math.log2(SEL_LEN)))
    blk = lax.broadcasted_iota(jnp.int32, (ns, 1), 0)
    forced = (blk == 0) | (blk == cur) | (blk == cur - 1)
    future = blk > cur
    blk_f = blk.astype(_F32)
    heads = []
    scores = []
    for g in range(NSA_KV_HEADS):
        st = jnp.where(visible, s_refs[g][...], NEG_INF)
        e = jnp.exp2(st - jnp.max(st, axis=0, keepdims=True))
        inv = jnp.where(qpos >= CMP_LEN - 1, 1.0 / jnp.sum(e, axis=0, keepdims=True), 0.0)
        pt = e * inv
        heads += _gated_heads(_dot(vct_ref[0, g], pt), gt_ref, g, 0, tq)
        p_sum = sum(pt[:, r * tq:(r + 1) * tq] for r in range(NSA_GROUP))
        score = _dot_split_rhs(ov_ref[...], p_sum)
        scores.append(jnp.where(forced, FORCE_SCORE, jnp.where(future, -1.0, score)))
    o_ref[0] = jnp.concatenate(heads, axis=0).T
    unselected = [jnp.full((ns, tq), -1.0, _F32) for _ in range(NSA_KV_HEADS)]
    for _ in range(n_top):
        for g in range(NSA_KV_HEADS):
            top = jnp.max(scores[g], axis=0, keepdims=True)
            first = jnp.min(jnp.where(scores[g] == top, blk_f, float(ns)), axis=0, keepdims=True)
            pick = blk_f == first
            unselected[g] = jnp.where(pick, 0.0, unselected[g])
            scores[g] = jnp.where(pick, PICKED, scores[g])
    for g in range(NSA_KV_HEADS):
        sel_m1 = unselected[g].T.astype(qa_ref.dtype)
        for h in range(g * NSA_GROUP, (g + 1) * NSA_GROUP):
            qa_ref[0, h, :, 0:ns] = sel_m1
            qa_ref[0, h, :, ns:ns + HEAD_DIM] = q_ref[0, h]


def _cmp_select(q, kc, vct, gates_t):
    b, h, s, d = q.shape
    g = kc.shape[1]
    ncp = kc.shape[2]
    ns = s // SEL_LEN
    n_top = min(SEL_TOPK, ns)
    tq = min(256, s)
    assert tq & (tq - 1) == 0 and g == 2
    c0 = np.arange(ncp)[:, None] * CMP_STRIDE
    n0 = np.arange(ns)[None, :] * SEL_LEN
    overlap = jnp.asarray(((c0 < n0 + SEL_LEN) & (c0 + CMP_LEN > n0)).T, _MXU)
    return pl.pallas_call(
        functools.partial(_cmp_kernel, tq=tq, n_top=n_top),
        grid=(b, s // tq),
        in_specs=[pl.BlockSpec((1, h, tq, d), lambda bi, i: (bi, 0, i, 0)),
                  pl.BlockSpec((1, g, ncp, d), lambda bi, i: (bi, 0, 0, 0)),
                  pl.BlockSpec((1, g, d, ncp), lambda bi, i: (bi, 0, 0, 0)),
                  pl.BlockSpec((ns, ncp), lambda bi, i: (0, 0)),
                  pl.BlockSpec((1, GATE_ROWS, tq), lambda bi, i: (bi, 0, i))],
        out_specs=[pl.BlockSpec((1, tq, h * d), lambda bi, i: (bi, i, 0)),
                   pl.BlockSpec((1, h, tq, ns + d), lambda bi, i: (bi, 0, i, 0))],
        out_shape=[jax.ShapeDtypeStruct((b, s, h * d), _F32), jax.ShapeDtypeStruct((b, h, s, ns + d), _MXU)],
        scratch_shapes=[pltpu.VMEM((ncp, NSA_GROUP * tq), _F32) for _ in range(g)],
        compiler_params=_params("parallel", "arbitrary"),
        name="nsa_cmp_select",
    )(q, kc, vct, overlap, gates_t)


def _sel_kernel(q_ref, k_ref, vt_ref, gt_ref, o_ref, s0_ref, s1_ref, *, tq, tk, nsub):
    q0 = pl.program_id(1) * tq
    last = (q0 + tq - 1) // tk
    lanes = NSA_GROUP * tq
    s_refs = (s0_ref, s1_ref)

    def qk(g, j):
        off = pl.multiple_of(j * tk, tk)
        s_refs[g][...] = _dot_nt(k_ref[0, g, pl.ds(off, tk), :], _group_queries(q_ref, g, tq))

    def soft(g, j, carry, causal):
        st = s_refs[g][...]
        if causal:
            key = j * tk + lax.broadcasted_iota(jnp.int32, (tk, lanes), 0)
            qry = q0 + (lax.broadcasted_iota(jnp.int32, (tk, lanes), 1) & (tq - 1))
            st = jnp.where(key <= qry, st, NEG_INF)
        return _softmax_step_t(carry, st, [vt_ref[0, g, j * nsub + c] for c in range(nsub)])

    init = tuple(_softmax_init_t(HEAD_DIM, lanes) for _ in range(NSA_KV_HEADS))
    carry = _two_chain_sweep(last, qk, soft, init)
    heads = []
    for g in range(NSA_KV_HEADS):
        heads += _gated_heads(_softmax_finish_t(carry[g], HEAD_DIM), gt_ref, g, 1, tq)
    o_ref[0] = jnp.concatenate(heads, axis=0).T


def _sel_attention(q, k, vt, gates_t):
    b, h, s, da = q.shape
    g = k.shape[1]
    d = HEAD_DIM
    tv = vt.shape[-1]
    tq = min(256, s)
    tk = min(512, s)
    assert tq & (tq - 1) == 0 and s % tk == 0 and tk % tv == 0 and g == 2
    return pl.pallas_call(
        functools.partial(_sel_kernel, tq=tq, tk=tk, nsub=tk // tv),
        grid=(b, s // tq),
        in_specs=[pl.BlockSpec((1, h, tq, da), lambda bi, i: (bi, 0, i, 0)),
                  pl.BlockSpec((1, g, s, da), lambda bi, i: (bi, 0, 0, 0)),
                  pl.BlockSpec((1, g) + vt.shape[2:], lambda bi, i: (bi, 0, 0, 0, 0)),
                  pl.BlockSpec((1, GATE_ROWS, tq), lambda bi, i: (bi, 0, i))],
        out_specs=pl.BlockSpec((1, tq, h * d), lambda bi, i: (bi, i, 0)),
        out_shape=jax.ShapeDtypeStruct((b, s, h * d), _F32),
        scratch_shapes=[pltpu.VMEM((tk, NSA_GROUP * tq), _F32) for _ in range(g)],
        compiler_params=_params("parallel", "arbitrary"),
        name="nsa_selected",
    )(q, k, vt, gates_t)


def _win_kernel(q_ref, k_ref, vt_ref, gt_ref, o_ref, s0_ref, s1_ref, *, tq, span, tv):
    q0 = pl.program_id(1) * tq
    start = pl.multiple_of(jnp.maximum(q0 - WINDOW, 0), tq)
    first_chunk = start // tv
    lanes = NSA_GROUP * tq
    s_refs = (s0_ref, s1_ref)
    for g in range(NSA_KV_HEADS):
        s_refs[g][...] = _dot_nt(k_ref[0, g, pl.ds(start, span), :], _group_queries(q_ref, g, tq))
    key = start + lax.broadcasted_iota(jnp.int32, (span, lanes), 0)
    qry = q0 + (lax.broadcasted_iota(jnp.int32, (span, lanes), 1) & (tq - 1))
    heads = []
    for g in range(NSA_KV_HEADS):
        st = jnp.where(key <= qry, s_refs[g][...], NEG_INF)
        st = jnp.where(key > qry - WINDOW, st, NEG_INF)
        carry = _softmax_step_t(_softmax_init_t(HEAD_DIM, lanes), st,
                                [vt_ref[0, g, first_chunk + c] for c in range(span // tv)])
        heads += _gated_heads(_softmax_finish_t(carry, HEAD_DIM), gt_ref, g, 2, tq)
    o_ref[0] = jnp.concatenate(heads, axis=0).T


def _win_attention(q, k, vt, gates_t):
    b, h, s, d = q.shape
    g = k.shape[1]
    tv = vt.shape[-1]
    tq = min(256, s)
    span = WINDOW + tq
    assert tq & (tq - 1) == 0 and s >= span and tq % tv == 0 and WINDOW % tv == 0 and g == 2
    return pl.pallas_call(
        functools.partial(_win_kernel, tq=tq, span=span, tv=tv),
        grid=(b, s // tq),
        in_specs=[pl.BlockSpec((1, h, tq, d), lambda bi, i: (bi, 0, i, 0)),
                  pl.BlockSpec((1, g, s, d), lambda bi, i: (bi, 0, 0, 0)),
                  pl.BlockSpec((1, g) + vt.shape[2:], lambda bi, i: (bi, 0, 0, 0, 0)),
                  pl.BlockSpec((1, GATE_ROWS, tq), lambda bi, i: (bi, 0, i))],
        out_specs=pl.BlockSpec((1, tq, h * d), lambda bi, i: (bi, i, 0)),
        out_shape=jax.ShapeDtypeStruct((b, s, h * d), _F32),
        scratch_shapes=[pltpu.VMEM((span, NSA_GROUP * tq), _F32) for _ in range(g)],
        compiler_params=_params("parallel", "arbitrary"),
        name="nsa_window",
    )(q, k, vt, gates_t)


def _out_kernel(x_ref, mla_ref, cmp_ref, sel_ref, win_ref, sb_ref, w_ref, o_ref):
    def w_rows(first_head, n_heads):
        return w_ref[first_head * HEAD_DIM:(first_head + n_heads) * HEAD_DIM, :]

    acc = x_ref[0] + jnp.dot(mla_ref[0], w_rows(0, MLA_HEADS), preferred_element_type=_F32)
    nsa = cmp_ref[0] + sel_ref[0] + win_ref[0]
    acc = acc + _dot(nsa, w_rows(MLA_HEADS, NSA_HEADS))
    o_ref[0] = acc + jnp.dot(sb_ref[0], w_rows(MLA_HEADS + NSA_HEADS, SB_HEADS), preferred_element_type=_F32)


def _out_proj(x, o_mla, o_cmp, o_sel, o_win, o_sb, layer, w_heads):
    b, s, _ = x.shape
    ts = min(512, s)

    def rows(a):
        return pl.BlockSpec((1, ts, a.shape[2]), lambda bi, i: (bi, i, 0))

    xspec = pl.BlockSpec((1, ts, D_MODEL), lambda bi, i: (bi, i, 0))
    return pl.pallas_call(
        _out_kernel,
        grid=(b, s // ts),
        in_specs=[xspec, rows(o_mla), rows(o_cmp), rows(o_sel), rows(o_win), rows(o_sb),
                  _layer_spec(w_heads, layer)],
        out_specs=xspec,
        out_shape=jax.ShapeDtypeStruct(x.shape, _F32),
        compiler_params=_params("parallel", "parallel"),
        name="out_proj",
    )(x, o_mla, o_cmp, o_sel, o_win, o_sb, w_heads)


def _gather_cols(w, idx):
    idx = np.asarray(idx)
    cuts = [0] + [i for i in range(1, len(idx)) if idx[i] != idx[i - 1] + (idx[i - 1] >= 0)] + [len(idx)]
    pieces = []
    for a, b in zip(cuts[:-1], cuts[1:]):
        if idx[a] < 0:
            pieces.append(jnp.zeros(w.shape[:-1] + (b - a,), _MXU))
        else:
            pieces.append(w[..., int(idx[a]):int(idx[a]) + b - a].astype(_MXU))
    return jnp.concatenate(pieces, axis=-1)


def _swap_halves(rot):
    return (np.arange(rot) + rot // 2) % rot


def _w_in_index():
    idx = np.full((_N_HEAD_COLS * HEAD_DIM,), -1, np.int64)

    def put(col, src):
        src = np.asarray(src)
        idx[col:col + len(src)] = src

    def put_head(pos, src):
        put(pos * HEAD_DIM, src)

    put(_S_CQ * LANES, _O_CQ + np.arange(MLA_Q_LORA))
    put(_S_CKV * LANES, _O_CKV + np.arange(MLA_KV_LORA))
    put(_S_KR * LANES + MLA_NOPE, _O_KR + np.arange(MLA_ROPE))
    put(_S_KRS * LANES + MLA_NOPE, _O_KR + _swap_halves(MLA_ROPE))
    for h in range(NSA_HEADS):
        put_head(_H_NQ + h, _O_NQ + h * HEAD_DIM + np.arange(HEAD_DIM))
        put_head(_H_NQS + h, _O_NQ + h * HEAD_DIM + _swap_halves(PARTIAL_ROT))
    for hk, hks, ok in ((_H_KC, _H_KCS, _O_NKC), (_H_KS, _H_KSS, _O_NKS), (_H_KW, _H_KWS, _O_NKW)):
        for g in range(NSA_KV_HEADS):
            put_head(hk + g, ok + g * HEAD_DIM + np.arange(HEAD_DIM))
            put_head(hks + g, ok + g * HEAD_DIM + _swap_halves(PARTIAL_ROT))
    for g in range(NSA_KV_HEADS):
        put_head(_H_VC + g, _O_NVC + g * HEAD_DIM + np.arange(HEAD_DIM))
    for h in range(SB_HEADS):
        put_head(_H_SBQ + h, _O_SBQ + h * HEAD_DIM + np.arange(HEAD_DIM))
        put_head(_H_SBK + h, _O_SBK + h * HEAD_DIM + np.arange(HEAD_DIM))
    return idx


def _mla_up_index():
    qd = MLA_NOPE + MLA_ROPE
    kd = MLA_NOPE + MLA_V
    uq = np.full((MLA_HEADS * LANES,), -1, np.int64)
    uqs = uq.copy()
    uk = uq.copy()
    for h in range(MLA_HEADS):
        uq[h * LANES:h * LANES + qd] = h * qd + np.arange(qd)
        uqs[h * LANES + MLA_NOPE:h * LANES + qd] = h * qd + MLA_NOPE + _swap_halves(MLA_ROPE)
        uk[h * LANES:h * LANES + MLA_NOPE] = h * kd + np.arange(MLA_NOPE)
    return uq, uqs, uk


def _transposed_weights(w_in, gate_bias):
    width = NSA_KV_HEADS * HEAD_DIM
    gate_rows = jnp.pad(w_in[..., _O_GATE:_O_GATE + N_GATES], ((0, 0), (0, 0), (0, _T_SBV - _T_GATE - N_GATES)))
    rows = jnp.concatenate([w_in[..., _O_NVS:_O_NVS + width], w_in[..., _O_NVW:_O_NVW + width], gate_rows,
                            w_in[..., _O_SBV:_O_SBV + SB_HEADS * HEAD_DIM]], axis=-1)
    bias = jnp.pad(gate_bias, ((0, 0), (0, GATE_ROWS - N_GATES)))[..., None]
    return jnp.swapaxes(rows, -1, -2).astype(_MXU), bias


def _rope_tables(s):
    pos = jnp.arange(s, dtype=_F32)

    def cs(rot):
        half = rot // 2
        inv_freq = ROPE_THETA ** (-jnp.arange(half, dtype=_F32) / half)
        ang = pos[:, None] * inv_freq[None, :]
        c, sn = jnp.cos(ang), jnp.sin(ang)
        return jnp.concatenate([c, c], axis=1), jnp.concatenate([-sn, sn], axis=1)

    c, sn = cs(MLA_ROPE)
    ones = jnp.ones((s, MLA_NOPE), _F32)
    zeros = jnp.zeros((s, MLA_NOPE), _F32)
    pad = jnp.zeros((s, LANES - MLA_NOPE - MLA_ROPE), _F32)
    ck = jnp.concatenate([ones, c, pad], axis=1)
    sk = jnp.concatenate([zeros, sn, pad], axis=1)
    q_scale = (MLA_NOPE + MLA_ROPE) ** -0.5 * LOG2_E
    c, sn = cs(PARTIAL_ROT)
    c64 = jnp.concatenate([c, jnp.ones((s, HEAD_DIM - PARTIAL_ROT), _F32)], axis=1)
    s64 = jnp.concatenate([sn, jnp.zeros((s, HEAD_DIM - PARTIAL_ROT), _F32)], axis=1)
    ns = s // SEL_LEN
    onehot = (np.arange(s)[:, None] // SEL_LEN == np.arange(ns)[None, :]) * -NEG_INF
    return ck * q_scale, sk * q_scale, ck, sk, c64, s64, jnp.asarray(onehot, _MXU)


def kernel(x, ffn1_norm, ffn1_w_gate, ffn1_w_up, ffn1_w_down, mix_norm, w_in, mla_q_norm, mla_w_uq, mla_kv_norm,
           mla_w_ukv, nsa_gate_bias, nsa_cmp_pos_k, nsa_cmp_w1_k, nsa_cmp_w2_k, nsa_cmp_pos_v, nsa_cmp_w1_v,
           nsa_cmp_w2_v, w_out, ffn2_norm, ffn2_w_gate, ffn2_w_up, ffn2_w_down, final_norm):
    b, s, d = x.shape
    depth = w_in.shape[0]
    tabs = _rope_tables(s)
    in_idx = _w_in_index()
    uq_idx, uqs_idx, uk_idx = _mla_up_index()
    half = CMP_LEN * HEAD_DIM // 2
    fg = final_norm.reshape(1, d)

    def row(p):
        return p[:, None, :]

    def cmp_weights(w1, w2, pos, transpose_out):
        pos = jnp.broadcast_to(pos.reshape(depth, 2, 1, half), (depth, 2, 8, half)).astype(_MXU)
        w2 = jnp.swapaxes(w2, -1, -2) if transpose_out else w2
        return w1.reshape(depth, 2, half, CMP_HIDDEN).astype(_MXU), w2.astype(_MXU), pos

    ffn1 = [w.astype(_MXU) for w in (ffn1_w_gate, ffn1_w_up, ffn1_w_down)]
    ffn2 = [w.astype(_MXU) for w in (ffn2_w_gate, ffn2_w_up, ffn2_w_down)]
    w_t, gate_bias = _transposed_weights(w_in, nsa_gate_bias)
    wuv_t = mla_w_ukv.reshape(depth, MLA_KV_LORA, MLA_HEADS, 2, MLA_V)[:, :, :, 1].transpose(0, 2, 3, 1).astype(_MXU)
    proj_params = (row(mix_norm), _gather_cols(w_in, in_idx), w_t,
                   row(mla_q_norm), _gather_cols(mla_w_uq, uq_idx), _gather_cols(mla_w_uq, uqs_idx),
                   row(mla_kv_norm), _gather_cols(mla_w_ukv, uk_idx), wuv_t, gate_bias)
    cmp_params = (cmp_weights(nsa_cmp_w1_k, nsa_cmp_w2_k, nsa_cmp_pos_k, False)
                  + cmp_weights(nsa_cmp_w1_v, nsa_cmp_w2_v, nsa_cmp_pos_v, True))
    w_out = w_out.astype(_MXU)
    ffn1_norm, ffn2_norm = row(ffn1_norm), row(ffn2_norm)

    for l in range(depth):
        x2d = _ffn(x.reshape(b * s, d), ffn1_norm, *ffn1, l, fg, False)
        x = x2d.reshape(b, s, d)
        (mq, mk, mvt, nq, nkc, nvc, nks, nvst, nkw, nvwt, gates_t, sbq, sbk, sbv) = _proj(x, l, *proj_params, tabs)
        o_mla = _mla_attention(mq, mk, mvt)
        kc, vct = _compress(nkc, nvc, l, *cmp_params)
        o_cmp, q_sel = _cmp_select(nq, kc, vct, gates_t)
        o_sel = _sel_attention(q_sel, nks, nvst, gates_t)
        o_win = _win_attention(nq, nkw, nvwt, gates_t)
        o_sb = _sb_attention(sbq, sbk, sbv)
        x = _out_proj(x, o_mla, o_cmp, o_sel, o_win, o_sb, l, w_out)
        x2d = _ffn(x.reshape(b * s, d), ffn2_norm, *ffn2, l, fg, l == depth - 1)
        x = x2d.reshape(b, s, d)
    return x
```

```python
import functools
import math

import numpy as np
import jax
import jax.numpy as jnp
from jax import lax
from jax.experimental import pallas as pl
from jax.experimental.pallas import tpu as pltpu

D_MODEL = 1024
HEAD_DIM = 64
MLA_HEADS = 6
MLA_NOPE = 64
MLA_ROPE = 32
MLA_V = 64
MLA_Q_LORA = 256
MLA_KV_LORA = 128
NSA_HEADS = 6
NSA_KV_HEADS = 2
NSA_GROUP = NSA_HEADS // NSA_KV_HEADS
NSA_BRANCHES = 3
CMP_LEN = 32
CMP_STRIDE = 16
CMP_HIDDEN = 128
SEL_LEN = 64
SEL_TOPK = 16
WINDOW = 512
SB_HEADS = 4
D_FF = 2816
ROPE_THETA = 500000.0
PARTIAL_ROT = HEAD_DIM // 4
EPS = 1e-6
NEG_INF = -1e30
M_FLOOR = 0.1 * NEG_INF
FORCE_SCORE = 1e4
PICKED = -3e38
F32_EXP2_ZERO = -151.0
LOG2_E = math.log2(math.e)
N_GATES = NSA_HEADS * NSA_BRANCHES

LANES = 128
FFN_CHUNK = 256
SWEEP_UNROLL = 4
TOKEN_CHUNK = 256
ONES_PAD = 16
VMEM_LIMIT = 56 * 1024 * 1024

_MXU = jnp.bfloat16
_F32 = jnp.float32

_IN_WIDTHS = (MLA_Q_LORA, MLA_KV_LORA, MLA_ROPE, NSA_HEADS * HEAD_DIM) + (NSA_KV_HEADS * HEAD_DIM,) * 6 + (
    N_GATES, SB_HEADS * HEAD_DIM, SB_HEADS * HEAD_DIM, SB_HEADS * HEAD_DIM)
_IN_OFF = np.concatenate([[0], np.cumsum(_IN_WIDTHS)])
(_O_CQ, _O_CKV, _O_KR, _O_NQ, _O_NKC, _O_NVC, _O_NKS, _O_NVS, _O_NKW, _O_NVW, _O_GATE, _O_SBQ, _O_SBK,
 _O_SBV) = [int(v) for v in _IN_OFF[:-1]]

_S_CQ, _S_CKV, _S_KR, _S_KRS = 0, 2, 3, 4
_H_NQ, _H_NQS = 10, 16
_H_KC, _H_KCS, _H_VC = 22, 24, 26
_H_KS, _H_KSS = 28, 30
_H_KW, _H_KWS = 32, 34
_H_SBQ, _H_SBK = 36, 40
_N_HEAD_COLS = 44
_T_VS, _T_VW, _T_GATE = 0, NSA_KV_HEADS * HEAD_DIM, 2 * NSA_KV_HEADS * HEAD_DIM
GATE_ROWS = 24
_T_SBV = _T_GATE + 32
_T_ROWS = _T_SBV + SB_HEADS * HEAD_DIM


def _dot(a, b):
    return jnp.dot(a.astype(_MXU), b.astype(_MXU), preferred_element_type=_F32)


def _dot_nt(a, b):
    return lax.dot_general(a.astype(_MXU), b.astype(_MXU), (((1,), (1,)), ((), ())),
                           preferred_element_type=_F32)


def _dot_split_rhs(a, b):
    hi = b.astype(_MXU)
    lo = (b - hi.astype(_F32)).astype(_MXU)
    return (jnp.dot(a, hi, preferred_element_type=_F32) + jnp.dot(a, lo, preferred_element_type=_F32))


def _rms(x, g):
    return x * lax.rsqrt(jnp.mean(x * x, axis=-1, keepdims=True) + EPS) * g


def _params(*sem):
    return pltpu.CompilerParams(dimension_semantics=sem, vmem_limit_bytes=VMEM_LIMIT)


def _layer_spec(a, layer):
    return pl.BlockSpec((None,) + a.shape[1:], lambda *_: (layer,) + (0,) * (a.ndim - 1))


def _ffn_kernel(x_ref, g_ref, wg_ref, wu_ref, wd_ref, fg_ref, o_ref, h_ref, acc_ref, act_ref, *, final_norm):
    j = pl.program_id(1)

    @pl.when(j == 0)
    def _():
        h_ref[...] = _rms(x_ref[...], g_ref[...]).astype(h_ref.dtype)
        acc_ref[...] = jnp.zeros_like(acc_ref)

    h = h_ref[...]
    tf = act_ref.shape[1]
    for c0 in range(0, tf, FFN_CHUNK):
        c1 = min(c0 + FFN_CHUNK, tf)
        gate = jnp.dot(h, wg_ref[:, c0:c1], preferred_element_type=_F32)
        up = jnp.dot(h, wu_ref[:, c0:c1], preferred_element_type=_F32)
        act_ref[:, c0:c1] = (gate * jax.nn.sigmoid(gate) * up).astype(act_ref.dtype)
    acc_ref[...] += jnp.dot(act_ref[...], wd_ref[...], preferred_element_type=_F32)

    @pl.when(j == pl.num_programs(1) - 1)
    def _():
        y = x_ref[...] + 0.5 * acc_ref[...]
        if final_norm:
            y = _rms(y, fg_ref[...])
        o_ref[...] = y


def _ffn(x2d, g, wg, wu, wd, layer, fg, final_norm):
    rows = x2d.shape[0]
    tm = min(1024, rows)
    tf = D_FF // 2
    grid = (rows // tm, D_FF // tf)
    return pl.pallas_call(
        functools.partial(_ffn_kernel, final_norm=final_norm),
        grid=grid,
        in_specs=[
            pl.BlockSpec((tm, D_MODEL), lambda i, j: (i, 0)),
            _layer_spec(g, layer),
            pl.BlockSpec((None, D_MODEL, tf), lambda i, j: (layer, 0, j)),
            pl.BlockSpec((None, D_MODEL, tf), lambda i, j: (layer, 0, j)),
            pl.BlockSpec((None, tf, D_MODEL), lambda i, j: (layer, j, 0)),
            pl.BlockSpec((1, D_MODEL), lambda i, j: (0, 0)),
        ],
        out_specs=pl.BlockSpec((tm, D_MODEL), lambda i, j: (i, 0)),
        out_shape=jax.ShapeDtypeStruct((rows, D_MODEL), _F32),
        scratch_shapes=[pltpu.VMEM((tm, D_MODEL), _MXU), pltpu.VMEM((tm, D_MODEL), _F32), pltpu.VMEM((tm, tf), _MXU)],
        compiler_params=_params("parallel", "arbitrary"),
        name="ffn",
    )(x2d, g, wg, wu, wd, fg)


def _proj_kernel(x_ref, g_ref, w_ref, wt_ref, qn_ref, wuq_ref, wuqs_ref, kvn_ref, wuk_ref, wuv_ref, gb_ref,
                 cq_ref, sq_ref, ck_ref, sk_ref, c64_ref, s64_ref, oh_ref,
                 mq_ref, mk_ref, mv_ref, nq_ref, nkc_ref, nvc_ref, nks_ref, nvs_ref, nkw_ref, nvw_ref,
                 gate_ref, sbq_ref, sbk_ref, sbv_ref, stage_ref):
    hn = _rms(x_ref[0], g_ref[...]).astype(_MXU)

    def proj(h0, h1):
        return jnp.dot(hn, w_ref[:, h0 * HEAD_DIM:h1 * HEAD_DIM], preferred_element_type=_F32)

    def slot(p, s):
        return p[:, s * LANES:(s + 1) * LANES]

    def head(p, i):
        return p[:, i * HEAD_DIM:(i + 1) * HEAD_DIM]

    p = proj(0, _H_NQ)
    cq = _rms(p[:, :MLA_Q_LORA], qn_ref[...])
    ckv = _rms(slot(p, _S_CKV), kvn_ref[...])
    q = _dot(cq, wuq_ref[...])
    q_partner = _dot(cq, wuqs_ref[...])
    kpe = slot(p, _S_KR) * ck_ref[...] + slot(p, _S_KRS) * sk_ref[...]
    kn = _dot(ckv, wuk_ref[...])
    for h in range(MLA_HEADS):
        mq_ref[0, h] = (slot(q, h) * cq_ref[...] + slot(q_partner, h) * sq_ref[...]).astype(mq_ref.dtype)
        mk_ref[0, h] = (slot(kn, h) + kpe).astype(mk_ref.dtype)
        mv_ref[0, h, 0] = _ones_row_pad(_dot_nt(wuv_ref[h], ckv)).astype(mv_ref.dtype)

    c64 = c64_ref[...]
    s64 = s64_ref[...]
    scale = HEAD_DIM ** -0.5

    p = proj(_H_NQ, _H_KC)
    for h in range(NSA_HEADS):
        nq_ref[0, h] = ((head(p, h) * c64 + head(p, NSA_HEADS + h) * s64) * (scale * LOG2_E)).astype(nq_ref.dtype)

    p = proj(_H_KC, _H_SBQ)
    base = _H_KC
    ns = oh_ref.shape[-1]

    def roped(hk, hks, g):
        return head(p, hk - base + g) * c64 + head(p, hks - base + g) * s64

    def write_chunked(o_ref, g, val):
        stage_ref[...] = val
        for t in range(CMP_STRIDE):
            piece = stage_ref[pl.ds(t, val.shape[0] // CMP_STRIDE, stride=CMP_STRIDE), :]
            o_ref[0, g, :, t * HEAD_DIM:(t + 1) * HEAD_DIM] = piece.astype(o_ref.dtype)

    for g in range(NSA_KV_HEADS):
        write_chunked(nkc_ref, g, roped(_H_KC, _H_KCS, g))
        write_chunked(nvc_ref, g, head(p, _H_VC - base + g))
        nks_ref[0, g, :, 0:ns] = oh_ref[...]
        nks_ref[0, g, :, ns:ns + HEAD_DIM] = roped(_H_KS, _H_KSS, g).astype(nks_ref.dtype)
        nkw_ref[0, g] = roped(_H_KW, _H_KWS, g).astype(nkw_ref.dtype)

    pt = _dot_nt(wt_ref[...], hn)
    for g in range(NSA_KV_HEADS):
        lo = g * HEAD_DIM
        nvs_ref[0, g, 0] = _ones_row_pad(pt[_T_VS + lo:_T_VS + lo + HEAD_DIM]).astype(nvs_ref.dtype)
        nvw_ref[0, g, 0] = _ones_row_pad(pt[_T_VW + lo:_T_VW + lo + HEAD_DIM]).astype(nvw_ref.dtype)
    gate_ref[0] = jax.nn.sigmoid(pt[_T_GATE:_T_GATE + GATE_ROWS] + gb_ref[...])

    p = proj(_H_SBQ, _N_HEAD_COLS)
    for h in range(SB_HEADS):
        sbq_ref[0, h] = (head(p, h) * (scale * LOG2_E)).astype(sbq_ref.dtype)
        sbk_ref[0, h] = head(p, SB_HEADS + h).astype(sbk_ref.dtype)
        sbv_ref[0, h, 0] = pt[_T_SBV + h * HEAD_DIM:_T_SBV + (h + 1) * HEAD_DIM].astype(sbv_ref.dtype)


def _proj(x, layer, g, w_ext, w_t, qn, wuq, wuqs, kvn, wuk, wuv, gb, tabs):
    b, s, _ = x.shape
    ts = min(TOKEN_CHUNK, s)
    cq, sq, ck, sk, c64, s64, onehot = tabs
    full = functools.partial(_layer_spec, layer=layer)

    def tab(a):
        return pl.BlockSpec((ts, a.shape[1]), lambda bi, i: (i, 0))

    def heads(n, d):
        return (pl.BlockSpec((1, n, ts, d), lambda bi, i: (bi, 0, i, 0)),
                jax.ShapeDtypeStruct((b, n, s, d), _MXU))

    def values_t(n):
        return (pl.BlockSpec((1, n, 1, HEAD_DIM + ONES_PAD, ts), lambda bi, i: (bi, 0, i, 0, 0)),
                jax.ShapeDtypeStruct((b, n, s // ts, HEAD_DIM + ONES_PAD, ts), _MXU))

    def chunked():
        return (pl.BlockSpec((1, NSA_KV_HEADS, ts // CMP_STRIDE, CMP_STRIDE * HEAD_DIM), lambda bi, i: (bi, 0, i, 0)),
                jax.ShapeDtypeStruct((b, NSA_KV_HEADS, s // CMP_STRIDE, CMP_STRIDE * HEAD_DIM), _MXU))

    outs = [heads(MLA_HEADS, LANES), heads(MLA_HEADS, LANES), values_t(MLA_HEADS), heads(NSA_HEADS, HEAD_DIM),
            chunked(), chunked(), heads(NSA_KV_HEADS, onehot.shape[1] + HEAD_DIM), values_t(NSA_KV_HEADS),
            heads(NSA_KV_HEADS, HEAD_DIM), values_t(NSA_KV_HEADS),
            (pl.BlockSpec((1, GATE_ROWS, ts), lambda bi, i: (bi, 0, i)), jax.ShapeDtypeStruct((b, GATE_ROWS, s), _F32)),
            heads(SB_HEADS, HEAD_DIM), heads(SB_HEADS, HEAD_DIM),
            (pl.BlockSpec((1, SB_HEADS, 1, HEAD_DIM, ts), lambda bi, i: (bi, 0, i, 0, 0)),
             jax.ShapeDtypeStruct((b, SB_HEADS, s // ts, HEAD_DIM, ts), _MXU))]
    return pl.pallas_call(
        _proj_kernel,
        grid=(b, s // ts),
        in_specs=[pl.BlockSpec((1, ts, D_MODEL), lambda bi, i: (bi, i, 0)), full(g), full(w_ext), full(w_t), full(qn),
                  full(wuq), full(wuqs), full(kvn), full(wuk), full(wuv), full(gb),
                  tab(cq), tab(sq), tab(ck), tab(sk), tab(c64), tab(s64), tab(onehot)],
        out_specs=[o[0] for o in outs],
        out_shape=[o[1] for o in outs],
        scratch_shapes=[pltpu.VMEM((ts, HEAD_DIM), _F32)],
        compiler_params=_params("parallel", "parallel"),
        name="proj",
    )(x, g, w_ext, w_t, qn, wuq, wuqs, kvn, wuk, wuv, gb, cq, sq, ck, sk, c64, s64, onehot)


def _ones_row_pad(vt):
    first = lax.broadcasted_iota(jnp.int32, (ONES_PAD, vt.shape[1]), 0) == 0
    return jnp.concatenate([vt, jnp.where(first, 1.0, 0.0).astype(vt.dtype)], axis=0)


def _softmax_step_t(carry, st, vt_chunks):
    m, acc = carry
    m_new = jnp.maximum(m, jnp.max(st, axis=0, keepdims=True))
    alpha = jnp.exp2(m - m_new)
    pt = jnp.exp2(st - m_new).astype(_MXU)
    n = st.shape[0] // len(vt_chunks)
    pv = sum(jnp.dot(vt, pt[c * n:(c + 1) * n], preferred_element_type=_F32) for c, vt in enumerate(vt_chunks))
    return m_new, alpha * acc + pv


def _softmax_init_t(d, cols):
    return (jnp.full((1, cols), M_FLOOR, _F32), jnp.zeros((d + ONES_PAD, cols), _F32))


def _softmax_finish_t(carry, d):
    _, acc = carry
    return acc[:d] * (1.0 / acc[d:d + 1])


def _two_chain_sweep(n_full, qk, soft, init):
    def body(j, carry, diag=False):
        c0, c1 = carry
        qk(0, j)
        c1 = soft(1, j, c1, diag)
        qk(1, jnp.zeros_like(j) if diag else j + 1)
        c0 = soft(0, j, c0, diag)
        return c0, c1

    def unrolled(i, carry):
        for u in range(SWEEP_UNROLL):
            carry = body(SWEEP_UNROLL * i + u, carry)
        return carry

    qk(1, n_full)
    carry = body(n_full, init, True)
    trips = n_full // SWEEP_UNROLL
    carry = lax.fori_loop(0, trips, unrolled, carry)
    return lax.fori_loop(SWEEP_UNROLL * trips, n_full, body, carry)


def _mla_kernel(q_ref, k_ref, vt_ref, o_ref, s0_ref, s1_ref, *, tq, tk, nsub):
    q0 = pl.program_id(2) * tq
    last = (q0 + tq - 1) // tk
    s_refs = (s0_ref, s1_ref)

    def qk(hh, j):
        off = pl.multiple_of(j * tk, tk)
        s_refs[hh][...] = _dot_nt(k_ref[0, hh, pl.ds(off, tk), :], q_ref[0, hh])

    def soft(hh, j, carry, diag):
        st = s_refs[hh][...]
        if diag:
            key = j * tk + lax.broadcasted_iota(jnp.int32, (tk, tq), 0)
            qry = q0 + lax.broadcasted_iota(jnp.int32, (tk, tq), 1)
            st = jnp.where(key <= qry, st, NEG_INF)
        return _softmax_step_t(carry, st, [vt_ref[0, hh, j * nsub + c] for c in range(nsub)])

    carry = _two_chain_sweep(last, qk, soft, tuple(_softmax_init_t(MLA_V, tq) for _ in range(2)))
    ot = jnp.concatenate([_softmax_finish_t(c, MLA_V) for c in carry], axis=0)
    o_ref[0] = ot.T.astype(o_ref.dtype)


def _mla_attention(q, k, vt):
    b, h, s, _ = q.shape
    tv = vt.shape[-1]
    dv = vt.shape[-2]
    tq = tk = min(512, s)
    assert h % 2 == 0 and 2 * MLA_V == LANES and tk % tv == 0 and s % tk == 0 and tk % tq == 0
    return pl.pallas_call(
        functools.partial(_mla_kernel, tq=tq, tk=tk, nsub=tk // tv),
        grid=(b, h // 2, s // tq),
        in_specs=[pl.BlockSpec((1, 2, tq, LANES), lambda bi, hi, i: (bi, hi, i, 0)),
                  pl.BlockSpec((1, 2, s, LANES), lambda bi, hi, i: (bi, hi, 0, 0)),
                  pl.BlockSpec((1, 2, s // tv, dv, tv), lambda bi, hi, i: (bi, hi, 0, 0, 0))],
        out_specs=pl.BlockSpec((1, tq, LANES), lambda bi, hi, i: (bi, i, hi)),
        out_shape=jax.ShapeDtypeStruct((b, s, h * MLA_V), _MXU),
        scratch_shapes=[pltpu.VMEM((tk, tq), _F32), pltpu.VMEM((tk, tq), _F32)],
        compiler_params=_params("parallel", "parallel", "arbitrary"),
        name="mla_attn",
    )(q, k, vt)


def _sb_kernel(q_ref, k_ref, vt_ref, u_ref, o_ref, *scratch_refs, t):
    qi = pl.program_id(2)
    u = u_ref[...]
    scratch = (scratch_refs[:6], scratch_refs[6:])

    def step(j, carry, diag):
        off = pl.multiple_of(j * t, t)
        if diag:
            key = lax.broadcasted_iota(jnp.int32, (t, t), 0)
            qry = lax.broadcasted_iota(jnp.int32, (t, t), 1)
            strict = key < qry
        def logits(hh):
            z_ref, _, _, _, _, _ = scratch[hh]
            z_ref[...] = _dot_nt(k_ref[0, hh, pl.ds(off, t), :], q_ref[0, hh])

        def log_terms(hh):
            z_ref, lb_ref, hi_ref, lo_ref, _, _ = scratch[hh]
            z = z_ref[...]
            log_beta = jnp.minimum(z, 0.0) - jnp.log2(1.0 + jnp.exp2(-jnp.abs(z)))
            log_rem = log_beta - z
            if diag:
                log_rem = jnp.where(strict, log_rem, 0.0)
            hi = log_rem.astype(_MXU)
            lb_ref[...] = log_beta
            hi_ref[...] = hi
            lo_ref[...] = (log_rem - hi.astype(_F32)).astype(_MXU)
            return log_rem[0:1, :]

        def suffix_sums(hh):
            _, _, hi_ref, lo_ref, sfx_ref, _ = scratch[hh]
            sfx_ref[...] = (jnp.dot(u, hi_ref[...], preferred_element_type=_F32)
                            + jnp.dot(u, lo_ref[...], preferred_element_type=_F32))

        def weights(hh, first_rem):
            _, lb_ref, _, _, sfx_ref, a_ref = scratch[hh]
            rem = carry[hh][0]
            suffix = sfx_ref[...]
            a = jnp.exp2(lb_ref[...] + suffix + rem)
            if diag:
                a = jnp.where(strict, a, 0.0)
            a_ref[...] = a.astype(_MXU)
            return rem + suffix[0:1, :] + first_rem

        def values(hh):
            a_ref = scratch[hh][5]
            return carry[hh][1] + jnp.dot(vt_ref[0, hh, j], a_ref[...], preferred_element_type=_F32)

        logits(0)
        logits(1)
        first0 = log_terms(0)
        suffix_sums(0)
        first1 = log_terms(1)
        rem0 = weights(0, first0)
        suffix_sums(1)
        acc0 = values(0)
        rem1 = weights(1, first1)
        alive = jnp.max(jnp.maximum(rem0, rem1)) > F32_EXP2_ZERO
        acc1 = values(1)
        return alive, ((rem0, acc0), (rem1, acc1))

    init = tuple((jnp.zeros((1, t), _F32), jnp.zeros((HEAD_DIM, t), _F32)) for _ in range(2))
    alive, carry = step(qi, init, True)

    def earlier(c):
        return (c[0] - 1,) + step(c[0], c[2], False)

    _, _, carry = lax.while_loop(lambda c: jnp.logical_and(c[0] >= 0, c[1]), earlier, (qi - 1, alive, carry))
    o_ref[0] = jnp.concatenate([acc for _, acc in carry], axis=0).T.astype(o_ref.dtype)


def _sb_attention(q, k, vt):
    b, h, s, d = q.shape
    t = vt.shape[-1]
    assert h % 2 == 0 and 2 * d == LANES and s % t == 0
    idx = np.arange(t)
    u = jnp.asarray(idx[None, :] > idx[:, None], _MXU)
    return pl.pallas_call(
        functools.partial(_sb_kernel, t=t),
        grid=(b, h // 2, s // t),
        in_specs=[pl.BlockSpec((1, 2, t, d), lambda bi, hi, i: (bi, hi, i, 0)),
                  pl.BlockSpec((1, 2, s, d), lambda bi, hi, i: (bi, hi, 0, 0)),
                  pl.BlockSpec((1, 2, s // t, d, t), lambda bi, hi, i: (bi, hi, 0, 0, 0)),
                  pl.BlockSpec((t, t), lambda bi, hi, i: (0, 0))],
        out_specs=pl.BlockSpec((1, t, LANES), lambda bi, hi, i: (bi, i, hi)),
        out_shape=jax.ShapeDtypeStruct((b, s, h * d), _MXU),
        scratch_shapes=[pltpu.VMEM((t, t), dt) for _ in range(2) for dt in (_F32, _F32, _MXU, _MXU, _F32, _MXU)],
        compiler_params=_params("parallel", "parallel", "arbitrary"),
        name="sb_attn",
    )(q, k, vt, u)


def _compress_kernel(xk_ref, xv_ref, w1k_ref, w2k_ref, pk_ref, w1v_ref, w2v_ref, pv_ref, ok_ref, ov_ref):
    def hidden(x_ref, w1_ref, p_ref):
        x = x_ref[0, 0]
        n = x.shape[0]
        first = jnp.dot(x, w1_ref[0], preferred_element_type=_F32)
        second = jnp.dot(x, w1_ref[1], preferred_element_type=_F32)
        pos = _dot(p_ref[0], w1_ref[0]) + _dot(p_ref[1], w1_ref[1])
        hid = first + pltpu.roll(second, n - 1, 0) + pos[0:1]
        return 0.5 * hid * (1.0 + jnp.tanh(math.sqrt(2.0 / math.pi) * (hid + 0.044715 * hid * hid * hid)))

    ok_ref[0, 0] = _dot(hidden(xk_ref, w1k_ref, pk_ref), w2k_ref[...]).astype(ok_ref.dtype)
    ov_ref[0, 0] = _dot_nt(w2v_ref[...], hidden(xv_ref, w1v_ref, pv_ref)).astype(ov_ref.dtype)


def _compress(xk, xv, layer, w1k, w2k, pk, w1v, w2v, pv):
    b, g, n, _ = xk.shape
    d = HEAD_DIM
    full = functools.partial(_layer_spec, layer=layer)

    xspec = pl.BlockSpec((1, 1, n, CMP_STRIDE * d), lambda bi, gi: (bi, gi, 0, 0))
    return pl.pallas_call(
        _compress_kernel,
        grid=(b, g),
        in_specs=[xspec, xspec, full(w1k), full(w2k), full(pk), full(w1v), full(w2v), full(pv)],
        out_specs=[pl.BlockSpec((1, 1, n, d), lambda bi, gi: (bi, gi, 0, 0)),
                   pl.BlockSpec((1, 1, d, n), lambda bi, gi: (bi, gi, 0, 0))],
        out_shape=[jax.ShapeDtypeStruct((b, g, n, d), _MXU), jax.ShapeDtypeStruct((b, g, d, n), _MXU)],
        compiler_params=_params("parallel", "parallel"),
        name="nsa_compress",
    )(xk, xv, w1k, w2k, pk, w1v, w2v, pv)


def _group_queries(q_ref, g, tq):
    return q_ref[0, g * NSA_GROUP:(g + 1) * NSA_GROUP].reshape(NSA_GROUP * tq, q_ref.shape[-1])


def _gated_heads(ot, gt_ref, g, branch, tq):
    out = []
    for r in range(NSA_GROUP):
        row = NSA_BRANCHES * (g * NSA_GROUP + r) + branch
        out.append(ot[:, r * tq:(r + 1) * tq] * gt_ref[0, row:row + 1, :])
    return out


def _cmp_kernel(q_ref, kc_ref, vct_ref, ov_ref, gt_ref, o_ref, qa_ref, s0_ref, s1_ref, *, tq, n_top):
    q0 = pl.program_id(1) * tq
    ncp = kc_ref.shape[2]
    ns = ov_ref.shape[0]
    lanes = NSA_GROUP * tq
    s_refs = (s0_ref, s1_ref)
    for g in range(NSA_KV_HEADS):
        s_refs[g][...] = _dot_nt(kc_ref[0, g], _group_queries(q_ref, g, tq))
    qpos = q0 + (lax.broadcasted_iota(jnp.int32, (1, lanes), 1) & (tq - 1))
    cmp_end = lax.broadcasted_iota(jnp.int32, (ncp, 1), 0) * CMP_STRIDE + (CMP_LEN - 1)
    visible = cmp_end <= qpos
    cur = jnp.right_shift(q0 + lax.broadcasted_iota(jnp.int32, (1, tq), 1), int(math.log2(SEL_LEN)))
    blk = lax.broadcasted_iota(jnp.int32, (ns, 1), 0)
    forced = (blk == 0) | (blk == cur) | (blk == cur - 1)
    future = blk > cur
    blk_f = blk.astype(_F32)
    heads = []
    scores = []
    for g in range(NSA_KV_HEADS):
        st = jnp.where(visible, s_refs[g][...], NEG_INF)
        e = jnp.exp2(st - jnp.max(st, axis=0, keepdims=True))
        inv = jnp.where(qpos >= CMP_LEN - 1, 1.0 / jnp.sum(e, axis=0, keepdims=True), 0.0)
        pt = e * inv
        heads += _gated_heads(_dot(vct_ref[0, g], pt), gt_ref, g, 0, tq)
        p_sum = sum(pt[:, r * tq:(r + 1) * tq] for r in range(NSA_GROUP))
        score = _dot_split_rhs(ov_ref[...], p_sum)
        scores.append(jnp.where(forced, FORCE_SCORE, jnp.where(future, -1.0, score)))
    o_ref[0] = jnp.concatenate(heads, axis=0).T.astype(o_ref.dtype)
    unselected = [jnp.full((ns, tq), -1.0, _F32) for _ in range(NSA_KV_HEADS)]
    for _ in range(n_top):
        for g in range(NSA_KV_HEADS):
            top = jnp.max(scores[g], axis=0, keepdims=True)
            first = jnp.min(jnp.where(scores[g] == top, blk_f, float(ns)), axis=0, keepdims=True)
            pick = blk_f == first
            unselected[g] = jnp.where(pick, 0.0, unselected[g])
            scores[g] = jnp.where(pick, PICKED, scores[g])
    for g in range(NSA_KV_HEADS):
        sel_m1 = unselected[g].T.astype(qa_ref.dtype)
        for h in range(g * NSA_GROUP, (g + 1) * NSA_GROUP):
            qa_ref[0, h, :, 0:ns] = sel_m1
            qa_ref[0, h, :, ns:ns + HEAD_DIM] = q_ref[0, h]


def _cmp_select(q, kc, vct, gates_t):
    b, h, s, d = q.shape
    g = kc.shape[1]
    ncp = kc.shape[2]
    ns = s // SEL_LEN
    n_top = min(SEL_TOPK, ns)
    tq = min(256, s)
    assert tq & (tq - 1) == 0 and g == 2
    c0 = np.arange(ncp)[:, None] * CMP_STRIDE
    n0 = np.arange(ns)[None, :] * SEL_LEN
    overlap = jnp.asarray(((c0 < n0 + SEL_LEN) & (c0 + CMP_LEN > n0)).T, _MXU)
    return pl.pallas_call(
        functools.partial(_cmp_kernel, tq=tq, n_top=n_top),
        grid=(b, s // tq),
        in_specs=[pl.BlockSpec((1, h, tq, d), lambda bi, i: (bi, 0, i, 0)),
                  pl.BlockSpec((1, g, ncp, d), lambda bi, i: (bi, 0, 0, 0)),
                  pl.BlockSpec((1, g, d, ncp), lambda bi, i: (bi, 0, 0, 0)),
                  pl.BlockSpec((ns, ncp), lambda bi, i: (0, 0)),
                  pl.BlockSpec((1, GATE_ROWS, tq), lambda bi, i: (bi, 0, i))],
        out_specs=[pl.BlockSpec((1, tq, h * d), lambda bi, i: (bi, i, 0)),
                   pl.BlockSpec((1, h, tq, ns + d), lambda bi, i: (bi, 0, i, 0))],
        out_shape=[jax.ShapeDtypeStruct((b, s, h * d), _MXU), jax.ShapeDtypeStruct((b, h, s, ns + d), _MXU)],
        scratch_shapes=[pltpu.VMEM((ncp, NSA_GROUP * tq), _F32) for _ in range(g)],
        compiler_params=_params("parallel", "arbitrary"),
        name="nsa_cmp_select",
    )(q, kc, vct, overlap, gates_t)


def _sel_kernel(q_ref, k_ref, vt_ref, gt_ref, o_ref, s0_ref, s1_ref, *, tq, tk, nsub):
    q0 = pl.program_id(1) * tq
    last = (q0 + tq - 1) // tk
    lanes = NSA_GROUP * tq
    s_refs = (s0_ref, s1_ref)

    def qk(g, j):
        off = pl.multiple_of(j * tk, tk)
        s_refs[g][...] = _dot_nt(k_ref[0, g, pl.ds(off, tk), :], _group_queries(q_ref, g, tq))

    def soft(g, j, carry, causal):
        st = s_refs[g][...]
        if causal:
            key = j * tk + lax.broadcasted_iota(jnp.int32, (tk, lanes), 0)
            qry = q0 + (lax.broadcasted_iota(jnp.int32, (tk, lanes), 1) & (tq - 1))
            st = jnp.where(key <= qry, st, NEG_INF)
        return _softmax_step_t(carry, st, [vt_ref[0, g, j * nsub + c] for c in range(nsub)])

    init = tuple(_softmax_init_t(HEAD_DIM, lanes) for _ in range(NSA_KV_HEADS))
    carry = _two_chain_sweep(last, qk, soft, init)
    heads = []
    for g in range(NSA_KV_HEADS):
        heads += _gated_heads(_softmax_finish_t(carry[g], HEAD_DIM), gt_ref, g, 1, tq)
    o_ref[0] = jnp.concatenate(heads, axis=0).T.astype(o_ref.dtype)


def _sel_attention(q, k, vt, gates_t):
    b, h, s, da = q.shape
    g = k.shape[1]
    d = HEAD_DIM
    tv = vt.shape[-1]
    tq = min(256, s)
    tk = min(512, s)
    assert tq & (tq - 1) == 0 and s % tk == 0 and tk % tv == 0 and g == 2
    return pl.pallas_call(
        functools.partial(_sel_kernel, tq=tq, tk=tk, nsub=tk // tv),
        grid=(b, s // tq),
        in_specs=[pl.BlockSpec((1, h, tq, da), lambda bi, i: (bi, 0, i, 0)),
                  pl.BlockSpec((1, g, s, da), lambda bi, i: (bi, 0, 0, 0)),
                  pl.BlockSpec((1, g) + vt.shape[2:], lambda bi, i: (bi, 0, 0, 0, 0)),
                  pl.BlockSpec((1, GATE_ROWS, tq), lambda bi, i: (bi, 0, i))],
        out_specs=pl.BlockSpec((1, tq, h * d), lambda bi, i: (bi, i, 0)),
        out_shape=jax.ShapeDtypeStruct((b, s, h * d), _MXU),
        scratch_shapes=[pltpu.VMEM((tk, NSA_GROUP * tq), _F32) for _ in range(g)],
        compiler_params=_params("parallel", "arbitrary"),
        name="nsa_selected",
    )(q, k, vt, gates_t)


def _win_kernel(q_ref, k_ref, vt_ref, gt_ref, o_ref, s0_ref, s1_ref, *, tq, span, tv):
    q0 = pl.program_id(1) * tq
    start = pl.multiple_of(jnp.maximum(q0 - WINDOW, 0), tq)
    first_chunk = start // tv
    lanes = NSA_GROUP * tq
    s_refs = (s0_ref, s1_ref)
    for g in range(NSA_KV_HEADS):
        s_refs[g][...] = _dot_nt(k_ref[0, g, pl.ds(start, span), :], _group_queries(q_ref, g, tq))
    key = start + lax.broadcasted_iota(jnp.int32, (span, lanes), 0)
    qry = q0 + (lax.broadcasted_iota(jnp.int32, (span, lanes), 1) & (tq - 1))
    heads = []
    for g in range(NSA_KV_HEADS):
        st = jnp.where(key <= qry, s_refs[g][...], NEG_INF)
        st = jnp.where(key > qry - WINDOW, st, NEG_INF)
        carry = _softmax_step_t(_softmax_init_t(HEAD_DIM, lanes), st,
                                [vt_ref[0, g, first_chunk + c] for c in range(span // tv)])
        heads += _gated_heads(_softmax_finish_t(carry, HEAD_DIM), gt_ref, g, 2, tq)
    o_ref[0] = jnp.concatenate(heads, axis=0).T.astype(o_ref.dtype)


def _win_attention(q, k, vt, gates_t):
    b, h, s, d = q.shape
    g = k.shape[1]
    tv = vt.shape[-1]
    tq = min(256, s)
    span = WINDOW + tq
    assert tq & (tq - 1) == 0 and s >= span and tq % tv == 0 and WINDOW % tv == 0 and g == 2
    return pl.pallas_call(
        functools.partial(_win_kernel, tq=tq, span=span, tv=tv),
        grid=(b, s // tq),
        in_specs=[pl.BlockSpec((1, h, tq, d), lambda bi, i: (bi, 0, i, 0)),
                  pl.BlockSpec((1, g, s, d), lambda bi, i: (bi, 0, 0, 0)),
                  pl.BlockSpec((1, g) + vt.shape[2:], lambda bi, i: (bi, 0, 0, 0, 0)),
                  pl.BlockSpec((1, GATE_ROWS, tq), lambda bi, i: (bi, 0, i))],
        out_specs=pl.BlockSpec((1, tq, h * d), lambda bi, i: (bi, i, 0)),
        out_shape=jax.ShapeDtypeStruct((b, s, h * d), _MXU),
        scratch_shapes=[pltpu.VMEM((span, NSA_GROUP * tq), _F32) for _ in range(g)],
        compiler_params=_params("parallel", "arbitrary"),
        name="nsa_window",
    )(q, k, vt, gates_t)


def _out_kernel(x_ref, mla_ref, cmp_ref, sel_ref, win_ref, sb_ref, w_ref, o_ref):
    def w_rows(first_head, n_heads):
        return w_ref[first_head * HEAD_DIM:(first_head + n_heads) * HEAD_DIM, :]

    acc = x_ref[0] + jnp.dot(mla_ref[0], w_rows(0, MLA_HEADS), preferred_element_type=_F32)
    nsa = cmp_ref[0].astype(_F32) + sel_ref[0].astype(_F32) + win_ref[0].astype(_F32)
    acc = acc + _dot(nsa, w_rows(MLA_HEADS, NSA_HEADS))
    o_ref[0] = acc + jnp.dot(sb_ref[0], w_rows(MLA_HEADS + NSA_HEADS, SB_HEADS), preferred_element_type=_F32)


def _out_proj(x, o_mla, o_cmp, o_sel, o_win, o_sb, layer, w_heads):
    b, s, _ = x.shape
    ts = min(512, s)

    def rows(a):
        return pl.BlockSpec((1, ts, a.shape[2]), lambda bi, i: (bi, i, 0))

    xspec = pl.BlockSpec((1, ts, D_MODEL), lambda bi, i: (bi, i, 0))
    return pl.pallas_call(
        _out_kernel,
        grid=(b, s // ts),
        in_specs=[xspec, rows(o_mla), rows(o_cmp), rows(o_sel), rows(o_win), rows(o_sb),
                  _layer_spec(w_heads, layer)],
        out_specs=xspec,
        out_shape=jax.ShapeDtypeStruct(x.shape, _F32),
        compiler_params=_params("parallel", "parallel"),
        name="out_proj",
    )(x, o_mla, o_cmp, o_sel, o_win, o_sb, w_heads)


def _gather_cols(w, idx):
    idx = np.asarray(idx)
    cuts = [0] + [i for i in range(1, len(idx)) if idx[i] != idx[i - 1] + (idx[i - 1] >= 0)] + [len(idx)]
    pieces = []
    for a, b in zip(cuts[:-1], cuts[1:]):
        if idx[a] < 0:
            pieces.append(jnp.zeros(w.shape[:-1] + (b - a,), _MXU))
        else:
            pieces.append(w[..., int(idx[a]):int(idx[a]) + b - a].astype(_MXU))
    return jnp.concatenate(pieces, axis=-1)


def _swap_halves(rot):
    return (np.arange(rot) + rot // 2) % rot


def _w_in_index():
    idx = np.full((_N_HEAD_COLS * HEAD_DIM,), -1, np.int64)

    def put(col, src):
        src = np.asarray(src)
        idx[col:col + len(src)] = src

    def put_head(pos, src):
        put(pos * HEAD_DIM, src)

    put(_S_CQ * LANES, _O_CQ + np.arange(MLA_Q_LORA))
    put(_S_CKV * LANES, _O_CKV + np.arange(MLA_KV_LORA))
    put(_S_KR * LANES + MLA_NOPE, _O_KR + np.arange(MLA_ROPE))
    put(_S_KRS * LANES + MLA_NOPE, _O_KR + _swap_halves(MLA_ROPE))
    for h in range(NSA_HEADS):
        put_head(_H_NQ + h, _O_NQ + h * HEAD_DIM + np.arange(HEAD_DIM))
        put_head(_H_NQS + h, _O_NQ + h * HEAD_DIM + _swap_halves(PARTIAL_ROT))
    for hk, hks, ok in ((_H_KC, _H_KCS, _O_NKC), (_H_KS, _H_KSS, _O_NKS), (_H_KW, _H_KWS, _O_NKW)):
        for g in range(NSA_KV_HEADS):
            put_head(hk + g, ok + g * HEAD_DIM + np.arange(HEAD_DIM))
            put_head(hks + g, ok + g * HEAD_DIM + _swap_halves(PARTIAL_ROT))
    for g in range(NSA_KV_HEADS):
        put_head(_H_VC + g, _O_NVC + g * HEAD_DIM + np.arange(HEAD_DIM))
    for h in range(SB_HEADS):
        put_head(_H_SBQ + h, _O_SBQ + h * HEAD_DIM + np.arange(HEAD_DIM))
        put_head(_H_SBK + h, _O_SBK + h * HEAD_DIM + np.arange(HEAD_DIM))
    return idx


def _mla_up_index():
    qd = MLA_NOPE + MLA_ROPE
    kd = MLA_NOPE + MLA_V
    uq = np.full((MLA_HEADS * LANES,), -1, np.int64)
    uqs = uq.copy()
    uk = uq.copy()
    for h in range(MLA_HEADS):
        uq[h * LANES:h * LANES + qd] = h * qd + np.arange(qd)
        uqs[h * LANES + MLA_NOPE:h * LANES + qd] = h * qd + MLA_NOPE + _swap_halves(MLA_ROPE)
        uk[h * LANES:h * LANES + MLA_NOPE] = h * kd + np.arange(MLA_NOPE)
    return uq, uqs, uk


def _transposed_weights(w_in, gate_bias):
    width = NSA_KV_HEADS * HEAD_DIM
    gate_rows = jnp.pad(w_in[..., _O_GATE:_O_GATE + N_GATES], ((0, 0), (0, 0), (0, _T_SBV - _T_GATE - N_GATES)))
    rows = jnp.concatenate([w_in[..., _O_NVS:_O_NVS + width], w_in[..., _O_NVW:_O_NVW + width], gate_rows,
                            w_in[..., _O_SBV:_O_SBV + SB_HEADS * HEAD_DIM]], axis=-1)
    bias = jnp.pad(gate_bias, ((0, 0), (0, GATE_ROWS - N_GATES)))[..., None]
    return jnp.swapaxes(rows, -1, -2).astype(_MXU), bias


def _rope_tables(s):
    pos = jnp.arange(s, dtype=_F32)

    def cs(rot):
        half = rot // 2
        inv_freq = ROPE_THETA ** (-jnp.arange(half, dtype=_F32) / half)
        ang = pos[:, None] * inv_freq[None, :]
        c, sn = jnp.cos(ang), jnp.sin(ang)
        return jnp.concatenate([c, c], axis=1), jnp.concatenate([-sn, sn], axis=1)

    c, sn = cs(MLA_ROPE)
    ones = jnp.ones((s, MLA_NOPE), _F32)
    zeros = jnp.zeros((s, MLA_NOPE), _F32)
    pad = jnp.zeros((s, LANES - MLA_NOPE - MLA_ROPE), _F32)
    ck = jnp.concatenate([ones, c, pad], axis=1)
    sk = jnp.concatenate([zeros, sn, pad], axis=1)
    q_scale = (MLA_NOPE + MLA_ROPE) ** -0.5 * LOG2_E
    c, sn = cs(PARTIAL_ROT)
    c64 = jnp.concatenate([c, jnp.ones((s, HEAD_DIM - PARTIAL_ROT), _F32)], axis=1)
    s64 = jnp.concatenate([sn, jnp.zeros((s, HEAD_DIM - PARTIAL_ROT), _F32)], axis=1)
    ns = s // SEL_LEN
    onehot = (np.arange(s)[:, None] // SEL_LEN == np.arange(ns)[None, :]) * -NEG_INF
    return ck * q_scale, sk * q_scale, ck, sk, c64, s64, jnp.asarray(onehot, _MXU)


def kernel(x, ffn1_norm, ffn1_w_gate, ffn1_w_up, ffn1_w_down, mix_norm, w_in, mla_q_norm, mla_w_uq, mla_kv_norm,
           mla_w_ukv, nsa_gate_bias, nsa_cmp_pos_k, nsa_cmp_w1_k, nsa_cmp_w2_k, nsa_cmp_pos_v, nsa_cmp_w1_v,
           nsa_cmp_w2_v, w_out, ffn2_norm, ffn2_w_gate, ffn2_w_up, ffn2_w_down, final_norm):
    b, s, d = x.shape
    depth = w_in.shape[0]
    tabs = _rope_tables(s)
    in_idx = _w_in_index()
    uq_idx, uqs_idx, uk_idx = _mla_up_index()
    half = CMP_LEN * HEAD_DIM // 2
    fg = final_norm.reshape(1, d)

    def row(p):
        return p[:, None, :]

    def cmp_weights(w1, w2, pos, transpose_out):
        pos = jnp.broadcast_to(pos.reshape(depth, 2, 1, half), (depth, 2, 8, half)).astype(_MXU)
        w2 = jnp.swapaxes(w2, -1, -2) if transpose_out else w2
        return w1.reshape(depth, 2, half, CMP_HIDDEN).astype(_MXU), w2.astype(_MXU), pos

    ffn1 = [w.astype(_MXU) for w in (ffn1_w_gate, ffn1_w_up, ffn1_w_down)]
    ffn2 = [w.astype(_MXU) for w in (ffn2_w_gate, ffn2_w_up, ffn2_w_down)]
    w_t, gate_bias = _transposed_weights(w_in, nsa_gate_bias)
    wuv_t = mla_w_ukv.reshape(depth, MLA_KV_LORA, MLA_HEADS, 2, MLA_V)[:, :, :, 1].transpose(0, 2, 3, 1).astype(_MXU)
    proj_params = (row(mix_norm), _gather_cols(w_in, in_idx), w_t,
                   row(mla_q_norm), _gather_cols(mla_w_uq, uq_idx), _gather_cols(mla_w_uq, uqs_idx),
                   row(mla_kv_norm), _gather_cols(mla_w_ukv, uk_idx), wuv_t, gate_bias)
    cmp_params = (cmp_weights(nsa_cmp_w1_k, nsa_cmp_w2_k, nsa_cmp_pos_k, False)
                  + cmp_weights(nsa_cmp_w1_v, nsa_cmp_w2_v, nsa_cmp_pos_v, True))
    w_out = w_out.astype(_MXU)
    ffn1_norm, ffn2_norm = row(ffn1_norm), row(ffn2_norm)

    for l in range(depth):
        x2d = _ffn(x.reshape(b * s, d), ffn1_norm, *ffn1, l, fg, False)
        x = x2d.reshape(b, s, d)
        (mq, mk, mvt, nq, nkc, nvc, nks, nvst, nkw, nvwt, gates_t, sbq, sbk, sbv) = _proj(x, l, *proj_params, tabs)
        o_mla = _mla_attention(mq, mk, mvt)
        kc, vct = _compress(nkc, nvc, l, *cmp_params)
        o_cmp, q_sel = _cmp_select(nq, kc, vct, gates_t)
        o_sel = _sel_attention(q_sel, nks, nvst, gates_t)
        o_win = _win_attention(nq, nkw, nvwt, gates_t)
        o_sb = _sb_attention(sbq, sbk, sbv)
        x = _out_proj(x, o_mla, o_cmp, o_sel, o_win, o_sb, l, w_out)
        x2d = _ffn(x.reshape(b * s, d), ffn2_norm, *ffn2, l, fg, l == depth - 1)
        x = x2d.reshape(b, s, d)
    return x
```

```python
import functools
import math

import numpy as np
import jax
import jax.numpy as jnp
from jax import lax
from jax.experimental import pallas as pl
from jax.experimental.pallas import tpu as pltpu

D_MODEL = 1024
HEAD_DIM = 64
MLA_HEADS = 6
MLA_NOPE = 64
MLA_ROPE = 32
MLA_V = 64
MLA_Q_LORA = 256
MLA_KV_LORA = 128
NSA_HEADS = 6
NSA_KV_HEADS = 2
NSA_GROUP = NSA_HEADS // NSA_KV_HEADS
NSA_BRANCHES = 3
CMP_LEN = 32
CMP_STRIDE = 16
CMP_HIDDEN = 128
SEL_LEN = 64
SEL_TOPK = 16
WINDOW = 512
SB_HEADS = 4
D_FF = 2816
ROPE_THETA = 500000.0
PARTIAL_ROT = HEAD_DIM // 4
EPS = 1e-6
NEG_INF = -1e30
M_FLOOR = 0.1 * NEG_INF
FORCE_SCORE = 1e4
PICKED = -3e38
F32_EXP2_ZERO = -151.0
LOG2_E = math.log2(math.e)
N_GATES = NSA_HEADS * NSA_BRANCHES

LANES = 128
FFN_CHUNK = 256
SWEEP_UNROLL = 4
TOKEN_CHUNK = 256
ONES_PAD = 16
VMEM_LIMIT = 56 * 1024 * 1024

_MXU = jnp.bfloat16
_F32 = jnp.float32

_IN_WIDTHS = (MLA_Q_LORA, MLA_KV_LORA, MLA_ROPE, NSA_HEADS * HEAD_DIM) + (NSA_KV_HEADS * HEAD_DIM,) * 6 + (
    N_GATES, SB_HEADS * HEAD_DIM, SB_HEADS * HEAD_DIM, SB_HEADS * HEAD_DIM)
_IN_OFF = np.concatenate([[0], np.cumsum(_IN_WIDTHS)])
(_O_CQ, _O_CKV, _O_KR, _O_NQ, _O_NKC, _O_NVC, _O_NKS, _O_NVS, _O_NKW, _O_NVW, _O_GATE, _O_SBQ, _O_SBK,
 _O_SBV) = [int(v) for v in _IN_OFF[:-1]]

_S_CQ, _S_CKV, _S_KR, _S_KRS = 0, 2, 3, 4
_H_NQ, _H_NQS = 10, 16
_H_KC, _H_KCS, _H_VC = 22, 24, 26
_H_KS, _H_KSS = 28, 30
_H_KW, _H_KWS = 32, 34
_H_SBQ, _H_SBK = 36, 40
_N_HEAD_COLS = 44
_T_VS, _T_VW, _T_GATE = 0, NSA_KV_HEADS * HEAD_DIM, 2 * NSA_KV_HEADS * HEAD_DIM
GATE_ROWS = 24
_T_SBV = _T_GATE + 32
_T_ROWS = _T_SBV + SB_HEADS * HEAD_DIM


def _dot(a, b):
    return jnp.dot(a.astype(_MXU), b.astype(_MXU), preferred_element_type=_F32)


def _dot_nt(a, b):
    return lax.dot_general(a.astype(_MXU), b.astype(_MXU), (((1,), (1,)), ((), ())),
                           preferred_element_type=_F32)


def _dot_split_rhs(a, b):
    hi = b.astype(_MXU)
    lo = (b - hi.astype(_F32)).astype(_MXU)
    return (jnp.dot(a, hi, preferred_element_type=_F32) + jnp.dot(a, lo, preferred_element_type=_F32))


def _rms(x, g):
    return x * lax.rsqrt(jnp.mean(x * x, axis=-1, keepdims=True) + EPS) * g


def _params(*sem):
    return pltpu.CompilerParams(dimension_semantics=sem, vmem_limit_bytes=VMEM_LIMIT)


def _layer_spec(a, layer):
    return pl.BlockSpec((None,) + a.shape[1:], lambda *_: (layer,) + (0,) * (a.ndim - 1))


def _ffn_kernel(x_ref, g_ref, wg_ref, wu_ref, wd_ref, fg_ref, o_ref, h_ref, acc_ref, act_ref, *, final_norm):
    j = pl.program_id(1)

    @pl.when(j == 0)
    def _():
        h_ref[...] = _rms(x_ref[...], g_ref[...]).astype(h_ref.dtype)
        acc_ref[...] = jnp.zeros_like(acc_ref)

    h = h_ref[...]
    tf = act_ref.shape[1]
    for c0 in range(0, tf, FFN_CHUNK):
        c1 = min(c0 + FFN_CHUNK, tf)
        gate = jnp.dot(h, wg_ref[:, c0:c1], preferred_element_type=_F32)
        up = jnp.dot(h, wu_ref[:, c0:c1], preferred_element_type=_F32)
        act_ref[:, c0:c1] = (gate * jax.nn.sigmoid(gate) * up).astype(act_ref.dtype)
    acc_ref[...] += jnp.dot(act_ref[...], wd_ref[...], preferred_element_type=_F32)

    @pl.when(j == pl.num_programs(1) - 1)
    def _():
        y = x_ref[...] + 0.5 * acc_ref[...]
        if final_norm:
            y = _rms(y, fg_ref[...])
        o_ref[...] = y


def _ffn(x2d, g, wg, wu, wd, layer, fg, final_norm):
    rows = x2d.shape[0]
    tm = min(1024, rows)
    tf = D_FF // 2
    grid = (rows // tm, D_FF // tf)
    return pl.pallas_call(
        functools.partial(_ffn_kernel, final_norm=final_norm),
        grid=grid,
        in_specs=[
            pl.BlockSpec((tm, D_MODEL), lambda i, j: (i, 0)),
            _layer_spec(g, layer),
            pl.BlockSpec((None, D_MODEL, tf), lambda i, j: (layer, 0, j)),
            pl.BlockSpec((None, D_MODEL, tf), lambda i, j: (layer, 0, j)),
            pl.BlockSpec((None, tf, D_MODEL), lambda i, j: (layer, j, 0)),
            pl.BlockSpec((1, D_MODEL), lambda i, j: (0, 0)),
        ],
        out_specs=pl.BlockSpec((tm, D_MODEL), lambda i, j: (i, 0)),
        out_shape=jax.ShapeDtypeStruct((rows, D_MODEL), _F32),
        scratch_shapes=[pltpu.VMEM((tm, D_MODEL), _MXU), pltpu.VMEM((tm, D_MODEL), _F32), pltpu.VMEM((tm, tf), _MXU)],
        compiler_params=_params("parallel", "arbitrary"),
        name="ffn",
    )(x2d, g, wg, wu, wd, fg)


def _proj_kernel(x_ref, g_ref, w_ref, wt_ref, qn_ref, wuq_ref, wuqs_ref, kvn_ref, wuk_ref, wuv_ref, gb_ref,
                 cq_ref, sq_ref, ck_ref, sk_ref, c64_ref, s64_ref, oh_ref,
                 mq_ref, mk_ref, mv_ref, nq_ref, nkc_ref, nvc_ref, nks_ref, nvs_ref, nkw_ref, nvw_ref,
                 gate_ref, sbq_ref, sbk_ref, sbv_ref, stage_ref):
    hn = _rms(x_ref[0], g_ref[...]).astype(_MXU)

    def proj(h0, h1):
        return jnp.dot(hn, w_ref[:, h0 * HEAD_DIM:h1 * HEAD_DIM], preferred_element_type=_F32)

    def slot(p, s):
        return p[:, s * LANES:(s + 1) * LANES]

    def head(p, i):
        return p[:, i * HEAD_DIM:(i + 1) * HEAD_DIM]

    p = proj(0, _H_NQ)
    cq = _rms(p[:, :MLA_Q_LORA], qn_ref[...])
    ckv = _rms(slot(p, _S_CKV), kvn_ref[...])
    q = _dot(cq, wuq_ref[...])
    q_partner = _dot(cq, wuqs_ref[...])
    kpe = slot(p, _S_KR) * ck_ref[...] + slot(p, _S_KRS) * sk_ref[...]
    kn = _dot(ckv, wuk_ref[...])
    for h in range(MLA_HEADS):
        mq_ref[0, h] = (slot(q, h) * cq_ref[...] + slot(q_partner, h) * sq_ref[...]).astype(mq_ref.dtype)
        mk_ref[0, h] = (slot(kn, h) + kpe).astype(mk_ref.dtype)
        mv_ref[0, h, 0] = _ones_row_pad(_dot_nt(wuv_ref[h], ckv)).astype(mv_ref.dtype)

    c64 = c64_ref[...]
    s64 = s64_ref[...]
    scale = HEAD_DIM ** -0.5

    p = proj(_H_NQ, _H_KC)
    for h in range(NSA_HEADS):
        nq_ref[0, h] = ((head(p, h) * c64 + head(p, NSA_HEADS + h) * s64) * (scale * LOG2_E)).astype(nq_ref.dtype)

    p = proj(_H_KC, _H_SBQ)
    base = _H_KC
    ns = oh_ref.shape[-1]

    def roped(hk, hks, g):
        return head(p, hk - base + g) * c64 + head(p, hks - base + g) * s64

    def write_chunked(o_ref, g, val):
        stage_ref[...] = val
        for t in range(CMP_STRIDE):
            piece = stage_ref[pl.ds(t, val.shape[0] // CMP_STRIDE, stride=CMP_STRIDE), :]
            o_ref[0, g, :, t * HEAD_DIM:(t + 1) * HEAD_DIM] = piece.astype(o_ref.dtype)

    for g in range(NSA_KV_HEADS):
        write_chunked(nkc_ref, g, roped(_H_KC, _H_KCS, g))
        write_chunked(nvc_ref, g, head(p, _H_VC - base + g))
        nks_ref[0, g, :, 0:ns] = oh_ref[...]
        nks_ref[0, g, :, ns:ns + HEAD_DIM] = roped(_H_KS, _H_KSS, g).astype(nks_ref.dtype)
        nkw_ref[0, g] = roped(_H_KW, _H_KWS, g).astype(nkw_ref.dtype)

    pt = _dot_nt(wt_ref[...], hn)
    for g in range(NSA_KV_HEADS):
        lo = g * HEAD_DIM
        nvs_ref[0, g, 0] = _ones_row_pad(pt[_T_VS + lo:_T_VS + lo + HEAD_DIM]).astype(nvs_ref.dtype)
        nvw_ref[0, g, 0] = _ones_row_pad(pt[_T_VW + lo:_T_VW + lo + HEAD_DIM]).astype(nvw_ref.dtype)
    gate_ref[0] = jax.nn.sigmoid(pt[_T_GATE:_T_GATE + GATE_ROWS] + gb_ref[...])

    p = proj(_H_SBQ, _N_HEAD_COLS)
    for h in range(SB_HEADS):
        sbq_ref[0, h] = (head(p, h) * (scale * LOG2_E)).astype(sbq_ref.dtype)
        sbk_ref[0, h] = head(p, SB_HEADS + h).astype(sbk_ref.dtype)
        sbv_ref[0, h, 0] = pt[_T_SBV + h * HEAD_DIM:_T_SBV + (h + 1) * HEAD_DIM].astype(sbv_ref.dtype)


def _proj(x, layer, g, w_ext, w_t, qn, wuq, wuqs, kvn, wuk, wuv, gb, tabs):
    b, s, _ = x.shape
    ts = min(TOKEN_CHUNK, s)
    cq, sq, ck, sk, c64, s64, onehot = tabs
    full = functools.partial(_layer_spec, layer=layer)

    def tab(a):
        return pl.BlockSpec((ts, a.shape[1]), lambda bi, i: (i, 0))

    def heads(n, d):
        return (pl.BlockSpec((1, n, ts, d), lambda bi, i: (bi, 0, i, 0)),
                jax.ShapeDtypeStruct((b, n, s, d), _MXU))

    def values_t(n):
        return (pl.BlockSpec((1, n, 1, HEAD_DIM + ONES_PAD, ts), lambda bi, i: (bi, 0, i, 0, 0)),
                jax.ShapeDtypeStruct((b, n, s // ts, HEAD_DIM + ONES_PAD, ts), _MXU))

    def chunked():
        return (pl.BlockSpec((1, NSA_KV_HEADS, ts // CMP_STRIDE, CMP_STRIDE * HEAD_DIM), lambda bi, i: (bi, 0, i, 0)),
                jax.ShapeDtypeStruct((b, NSA_KV_HEADS, s // CMP_STRIDE, CMP_STRIDE * HEAD_DIM), _MXU))

    outs = [heads(MLA_HEADS, LANES), heads(MLA_HEADS, LANES), values_t(MLA_HEADS), heads(NSA_HEADS, HEAD_DIM),
            chunked(), chunked(), heads(NSA_KV_HEADS, onehot.shape[1] + HEAD_DIM), values_t(NSA_KV_HEADS),
            heads(NSA_KV_HEADS, HEAD_DIM), values_t(NSA_KV_HEADS),
            (pl.BlockSpec((1, GATE_ROWS, ts), lambda bi, i: (bi, 0, i)), jax.ShapeDtypeStruct((b, GATE_ROWS, s), _F32)),
            heads(SB_HEADS, HEAD_DIM), heads(SB_HEADS, HEAD_DIM),
            (pl.BlockSpec((1, SB_HEADS, 1, HEAD_DIM, ts), lambda bi, i: (bi, 0, i, 0, 0)),
             jax.ShapeDtypeStruct((b, SB_HEADS, s // ts, HEAD_DIM, ts), _MXU))]
    return pl.pallas_call(
        _proj_kernel,
        grid=(b, s // ts),
        in_specs=[pl.BlockSpec((1, ts, D_MODEL), lambda bi, i: (bi, i, 0)), full(g), full(w_ext), full(w_t), full(qn),
                  full(wuq), full(wuqs), full(kvn), full(wuk), full(wuv), full(gb),
                  tab(cq), tab(sq), tab(ck), tab(sk), tab(c64), tab(s64), tab(onehot)],
        out_specs=[o[0] for o in outs],
        out_shape=[o[1] for o in outs],
        scratch_shapes=[pltpu.VMEM((ts, HEAD_DIM), _F32)],
        compiler_params=_params("parallel", "parallel"),
        name="proj",
    )(x, g, w_ext, w_t, qn, wuq, wuqs, kvn, wuk, wuv, gb, cq, sq, ck, sk, c64, s64, onehot)


def _ones_row_pad(vt):
    first = lax.broadcasted_iota(jnp.int32, (ONES_PAD, vt.shape[1]), 0) == 0
    return jnp.concatenate([vt, jnp.where(first, 1.0, 0.0).astype(vt.dtype)], axis=0)


def _softmax_step_t(carry, st, vt_chunks):
    m, acc = carry
    m_new = jnp.maximum(m, jnp.max(st, axis=0, keepdims=True))
    alpha = jnp.exp2(m - m_new)
    pt = jnp.exp2(st - m_new).astype(_MXU)
    n = st.shape[0] // len(vt_chunks)
    pv = sum(jnp.dot(vt, pt[c * n:(c + 1) * n], preferred_element_type=_F32) for c, vt in enumerate(vt_chunks))
    return m_new, alpha * acc + pv


def _softmax_init_t(d, cols):
    return (jnp.full((1, cols), M_FLOOR, _F32), jnp.zeros((d + ONES_PAD, cols), _F32))


def _softmax_finish_t(carry, d):
    _, acc = carry
    return acc[:d] * (1.0 / acc[d:d + 1])


def _two_chain_sweep(n_full, qk, soft, init):
    def body(j, carry, diag=False):
        c0, c1 = carry
        qk(0, j)
        c1 = soft(1, j, c1, diag)
        qk(1, jnp.zeros_like(j) if diag else j + 1)
        c0 = soft(0, j, c0, diag)
        return c0, c1

    def unrolled(i, carry):
        for u in range(SWEEP_UNROLL):
            carry = body(SWEEP_UNROLL * i + u, carry)
        return carry

    qk(1, n_full)
    carry = body(n_full, init, True)
    trips = n_full // SWEEP_UNROLL
    carry = lax.fori_loop(0, trips, unrolled, carry)
    return lax.fori_loop(SWEEP_UNROLL * trips, n_full, body, carry)


def _mla_kernel(q_ref, k_ref, vt_ref, o_ref, s0_ref, s1_ref, *, t, nsub):
    qi = pl.program_id(2)
    s_refs = (s0_ref, s1_ref)

    def qk(hh, j):
        off = pl.multiple_of(j * t, t)
        s_refs[hh][...] = _dot_nt(k_ref[0, hh, pl.ds(off, t), :], q_ref[0, hh])

    def soft(hh, j, carry, diag):
        st = s_refs[hh][...]
        if diag:
            key = lax.broadcasted_iota(jnp.int32, (t, t), 0)
            qry = lax.broadcasted_iota(jnp.int32, (t, t), 1)
            st = jnp.where(key <= qry, st, NEG_INF)
        return _softmax_step_t(carry, st, [vt_ref[0, hh, j * nsub + c] for c in range(nsub)])

    carry = _two_chain_sweep(qi, qk, soft, tuple(_softmax_init_t(MLA_V, t) for _ in range(2)))
    ot = jnp.concatenate([_softmax_finish_t(c, MLA_V) for c in carry], axis=0)
    o_ref[0] = ot.T.astype(o_ref.dtype)


def _mla_attention(q, k, vt):
    b, h, s, _ = q.shape
    tv = vt.shape[-1]
    dv = vt.shape[-2]
    t = min(512, s)
    assert h % 2 == 0 and 2 * MLA_V == LANES and t % tv == 0 and s % t == 0
    return pl.pallas_call(
        functools.partial(_mla_kernel, t=t, nsub=t // tv),
        grid=(b, h // 2, s // t),
        in_specs=[pl.BlockSpec((1, 2, t, LANES), lambda bi, hi, i: (bi, hi, i, 0)),
                  pl.BlockSpec((1, 2, s, LANES), lambda bi, hi, i: (bi, hi, 0, 0)),
                  pl.BlockSpec((1, 2, s // tv, dv, tv), lambda bi, hi, i: (bi, hi, 0, 0, 0))],
        out_specs=pl.BlockSpec((1, t, LANES), lambda bi, hi, i: (bi, i, hi)),
        out_shape=jax.ShapeDtypeStruct((b, s, h * MLA_V), _MXU),
        scratch_shapes=[pltpu.VMEM((t, t), _F32), pltpu.VMEM((t, t), _F32)],
        compiler_params=_params("parallel", "parallel", "arbitrary"),
        name="mla_attn",
    )(q, k, vt)


def _sb_kernel(q_ref, k_ref, vt_ref, u_ref, o_ref, *scratch_refs, t):
    qi = pl.program_id(2)
    u = u_ref[...]
    scratch = (scratch_refs[:6], scratch_refs[6:])

    def step(j, carry, diag):
        off = pl.multiple_of(j * t, t)
        if diag:
            key = lax.broadcasted_iota(jnp.int32, (t, t), 0)
            qry = lax.broadcasted_iota(jnp.int32, (t, t), 1)
            strict = key < qry
        def logits(hh):
            z_ref, _, _, _, _, _ = scratch[hh]
            z_ref[...] = _dot_nt(k_ref[0, hh, pl.ds(off, t), :], q_ref[0, hh])

        def log_terms(hh):
            z_ref, lb_ref, hi_ref, lo_ref, _, _ = scratch[hh]
            z = z_ref[...]
            log_beta = jnp.minimum(z, 0.0) - jnp.log2(1.0 + jnp.exp2(-jnp.abs(z)))
            log_rem = log_beta - z
            if diag:
                log_rem = jnp.where(strict, log_rem, 0.0)
            hi = log_rem.astype(_MXU)
            lb_ref[...] = log_beta
            hi_ref[...] = hi
            lo_ref[...] = (log_rem - hi.astype(_F32)).astype(_MXU)
            return log_rem[0:1, :]

        def suffix_sums(hh):
            _, _, hi_ref, lo_ref, sfx_ref, _ = scratch[hh]
            sfx_ref[...] = (jnp.dot(u, hi_ref[...], preferred_element_type=_F32)
                            + jnp.dot(u, lo_ref[...], preferred_element_type=_F32))

        def weights(hh, first_rem):
            _, lb_ref, _, _, sfx_ref, a_ref = scratch[hh]
            rem = carry[hh][0]
            suffix = sfx_ref[...]
            a = jnp.exp2(lb_ref[...] + suffix + rem)
            if diag:
                a = jnp.where(strict, a, 0.0)
            a_ref[...] = a.astype(_MXU)
            return rem + suffix[0:1, :] + first_rem

        def values(hh):
            a_ref = scratch[hh][5]
            return carry[hh][1] + jnp.dot(vt_ref[0, hh, j], a_ref[...], preferred_element_type=_F32)

        logits(0)
        logits(1)
        first0 = log_terms(0)
        suffix_sums(0)
        first1 = log_terms(1)
        rem0 = weights(0, first0)
        suffix_sums(1)
        acc0 = values(0)
        rem1 = weights(1, first1)
        alive = jnp.max(jnp.maximum(rem0, rem1)) > F32_EXP2_ZERO
        acc1 = values(1)
        return alive, ((rem0, acc0), (rem1, acc1))

    init = tuple((jnp.zeros((1, t), _F32), jnp.zeros((HEAD_DIM, t), _F32)) for _ in range(2))
    alive, carry = step(qi, init, True)

    def earlier(c):
        return (c[0] - 1,) + step(c[0], c[2], False)

    _, _, carry = lax.while_loop(lambda c: jnp.logical_and(c[0] >= 0, c[1]), earlier, (qi - 1, alive, carry))
    o_ref[0] = jnp.concatenate([acc for _, acc in carry], axis=0).T.astype(o_ref.dtype)


def _sb_attention(q, k, vt):
    b, h, s, d = q.shape
    t = vt.shape[-1]
    assert h % 2 == 0 and 2 * d == LANES and s % t == 0
    idx = np.arange(t)
    u = jnp.asarray(idx[None, :] > idx[:, None], _MXU)
    return pl.pallas_call(
        functools.partial(_sb_kernel, t=t),
        grid=(b, h // 2, s // t),
        in_specs=[pl.BlockSpec((1, 2, t, d), lambda bi, hi, i: (bi, hi, i, 0)),
                  pl.BlockSpec((1, 2, s, d), lambda bi, hi, i: (bi, hi, 0, 0)),
                  pl.BlockSpec((1, 2, s // t, d, t), lambda bi, hi, i: (bi, hi, 0, 0, 0)),
                  pl.BlockSpec((t, t), lambda bi, hi, i: (0, 0))],
        out_specs=pl.BlockSpec((1, t, LANES), lambda bi, hi, i: (bi, i, hi)),
        out_shape=jax.ShapeDtypeStruct((b, s, h * d), _MXU),
        scratch_shapes=[pltpu.VMEM((t, t), dt) for _ in range(2) for dt in (_F32, _F32, _MXU, _MXU, _F32, _MXU)],
        compiler_params=_params("parallel", "parallel", "arbitrary"),
        name="sb_attn",
    )(q, k, vt, u)


def _compress_kernel(xk_ref, xv_ref, w1k_ref, w2k_ref, pk_ref, w1v_ref, w2v_ref, pv_ref, ok_ref, ov_ref):
    def hidden(x_ref, w1_ref, p_ref):
        x = x_ref[0, 0]
        n = x.shape[0]
        first = jnp.dot(x, w1_ref[0], preferred_element_type=_F32)
        second = jnp.dot(x, w1_ref[1], preferred_element_type=_F32)
        pos = _dot(p_ref[0], w1_ref[0]) + _dot(p_ref[1], w1_ref[1])
        hid = first + pltpu.roll(second, n - 1, 0) + pos[0:1]
        return 0.5 * hid * (1.0 + jnp.tanh(math.sqrt(2.0 / math.pi) * (hid + 0.044715 * hid * hid * hid)))

    ok_ref[0, 0] = _dot(hidden(xk_ref, w1k_ref, pk_ref), w2k_ref[...]).astype(ok_ref.dtype)
    ov_ref[0, 0] = _dot_nt(w2v_ref[...], hidden(xv_ref, w1v_ref, pv_ref)).astype(ov_ref.dtype)


def _compress(xk, xv, layer, w1k, w2k, pk, w1v, w2v, pv):
    b, g, n, _ = xk.shape
    d = HEAD_DIM
    full = functools.partial(_layer_spec, layer=layer)

    xspec = pl.BlockSpec((1, 1, n, CMP_STRIDE * d), lambda bi, gi: (bi, gi, 0, 0))
    return pl.pallas_call(
        _compress_kernel,
        grid=(b, g),
        in_specs=[xspec, xspec, full(w1k), full(w2k), full(pk), full(w1v), full(w2v), full(pv)],
        out_specs=[pl.BlockSpec((1, 1, n, d), lambda bi, gi: (bi, gi, 0, 0)),
                   pl.BlockSpec((1, 1, d, n), lambda bi, gi: (bi, gi, 0, 0))],
        out_shape=[jax.ShapeDtypeStruct((b, g, n, d), _MXU), jax.ShapeDtypeStruct((b, g, d, n), _MXU)],
        compiler_params=_params("parallel", "parallel"),
        name="nsa_compress",
    )(xk, xv, w1k, w2k, pk, w1v, w2v, pv)


def _group_queries(q_ref, g, tq):
    return q_ref[0, g * NSA_GROUP:(g + 1) * NSA_GROUP].reshape(NSA_GROUP * tq, q_ref.shape[-1])


def _gated_heads(ot, gt_ref, g, branch, tq):
    out = []
    for r in range(NSA_GROUP):
        row = NSA_BRANCHES * (g * NSA_GROUP + r) + branch
        out.append(ot[:, r * tq:(r + 1) * tq] * gt_ref[0, row:row + 1, :])
    return out


def _cmp_kernel(q_ref, kc_ref, vct_ref, ov_ref, gt_ref, o_ref, qa_ref, s0_ref, s1_ref, *, tq, n_top):
    q0 = pl.program_id(1) * tq
    ncp = kc_ref.shape[2]
    ns = ov_ref.shape[0]
    lanes = NSA_GROUP * tq
    s_refs = (s0_ref, s1_ref)
    for g in range(NSA_KV_HEADS):
        s_refs[g][...] = _dot_nt(kc_ref[0, g], _group_queries(q_ref, g, tq))
    qpos = q0 + (lax.broadcasted_iota(jnp.int32, (1, lanes), 1) & (tq - 1))
    cmp_end = lax.broadcasted_iota(jnp.int32, (ncp, 1), 0) * CMP_STRIDE + (CMP_LEN - 1)
    visible = cmp_end <= qpos
    cur = jnp.right_shift(q0 + lax.broadcasted_iota(jnp.int32, (1, tq), 1), int(math.log2(SEL_LEN)))
    blk = lax.broadcasted_iota(jnp.int32, (ns, 1), 0)
    forced = (blk == 0) | (blk == cur) | (blk == cur - 1)
    future = blk > cur
    blk_f = blk.astype(_F32)
    heads = []
    scores = []
    for g in range(NSA_KV_HEADS):
        st = jnp.where(visible, s_refs[g][...], NEG_INF)
        e = jnp.exp2(st - jnp.max(st, axis=0, keepdims=True))
        inv = jnp.where(qpos >= CMP_LEN - 1, 1.0 / jnp.sum(e, axis=0, keepdims=True), 0.0)
        pt = e * inv
        heads += _gated_heads(_dot(vct_ref[0, g], pt), gt_ref, g, 0, tq)
        p_sum = sum(pt[:, r * tq:(r + 1) * tq] for r in range(NSA_GROUP))
        score = _dot_split_rhs(ov_ref[...], p_sum)
        scores.append(jnp.where(forced, FORCE_SCORE, jnp.where(future, -1.0, score)))
    o_ref[0] = jnp.concatenate(heads, axis=0).T.astype(o_ref.dtype)
    unselected = [jnp.full((ns, tq), -1.0, _F32) for _ in range(NSA_KV_HEADS)]
    for _ in range(n_top):
        for g in range(NSA_KV_HEADS):
            top = jnp.max(scores[g], axis=0, keepdims=True)
            first = jnp.min(jnp.where(scores[g] == top, blk_f, float(ns)), axis=0, keepdims=True)
            pick = blk_f == first
            unselected[g] = jnp.where(pick, 0.0, unselected[g])
            scores[g] = jnp.where(pick, PICKED, scores[g])
    for g in range(NSA_KV_HEADS):
        sel_m1 = unselected[g].T.astype(qa_ref.dtype)
        for h in range(g * NSA_GROUP, (g + 1) * NSA_GROUP):
            qa_ref[0, h, :, 0:ns] = sel_m1
            qa_ref[0, h, :, ns:ns + HEAD_DIM] = q_ref[0, h]


def _cmp_select(q, kc, vct, gates_t):
    b, h, s, d = q.shape
    g = kc.shape[1]
    ncp = kc.shape[2]
    ns = s // SEL_LEN
    n_top = min(SEL_TOPK, ns)
    tq = min(256, s)
    assert tq & (tq - 1) == 0 and g == 2
    c0 = np.arange(ncp)[:, None] * CMP_STRIDE
    n0 = np.arange(ns)[None, :] * SEL_LEN
    overlap = jnp.asarray(((c0 < n0 + SEL_LEN) & (c0 + CMP_LEN > n0)).T, _MXU)
    return pl.pallas_call(
        functools.partial(_cmp_kernel, tq=tq, n_top=n_top),
        grid=(b, s // tq),
        in_specs=[pl.BlockSpec((1, h, tq, d), lambda bi, i: (bi, 0, i, 0)),
                  pl.BlockSpec((1, g, ncp, d), lambda bi, i: (bi, 0, 0, 0)),
                  pl.BlockSpec((1, g, d, ncp), lambda bi, i: (bi, 0, 0, 0)),
                  pl.BlockSpec((ns, ncp), lambda bi, i: (0, 0)),
                  pl.BlockSpec((1, GATE_ROWS, tq), lambda bi, i: (bi, 0, i))],
        out_specs=[pl.BlockSpec((1, tq, h * d), lambda bi, i: (bi, i, 0)),
                   pl.BlockSpec((1, h, tq, ns + d), lambda bi, i: (bi, 0, i, 0))],
        out_shape=[jax.ShapeDtypeStruct((b, s, h * d), _MXU), jax.ShapeDtypeStruct((b, h, s, ns + d), _MXU)],
        scratch_shapes=[pltpu.VMEM((ncp, NSA_GROUP * tq), _F32) for _ in range(g)],
        compiler_params=_params("parallel", "arbitrary"),
        name="nsa_cmp_select",
    )(q, kc, vct, overlap, gates_t)


def _key_minus_query(keys, tq):
    return jnp.asarray(np.arange(keys)[:, None] - np.arange(NSA_GROUP * tq)[None, :] % tq, jnp.int32)


def _sel_kernel(q_ref, k_ref, vt_ref, gt_ref, rel_ref, o_ref, s0_ref, s1_ref, *, tq, tk, nsub):
    q0 = pl.program_id(1) * tq
    last = (q0 + tq - 1) // tk
    lanes = NSA_GROUP * tq
    s_refs = (s0_ref, s1_ref)

    def qk(g, j):
        off = pl.multiple_of(j * tk, tk)
        s_refs[g][...] = _dot_nt(k_ref[0, g, pl.ds(off, tk), :], _group_queries(q_ref, g, tq))

    def soft(g, j, carry, causal):
        st = s_refs[g][...]
        if causal:
            st = jnp.where(rel_ref[...] <= q0 - j * tk, st, NEG_INF)
        return _softmax_step_t(carry, st, [vt_ref[0, g, j * nsub + c] for c in range(nsub)])

    init = tuple(_softmax_init_t(HEAD_DIM, lanes) for _ in range(NSA_KV_HEADS))
    carry = _two_chain_sweep(last, qk, soft, init)
    heads = []
    for g in range(NSA_KV_HEADS):
        heads += _gated_heads(_softmax_finish_t(carry[g], HEAD_DIM), gt_ref, g, 1, tq)
    o_ref[0] = jnp.concatenate(heads, axis=0).T.astype(o_ref.dtype)


def _sel_attention(q, k, vt, gates_t):
    b, h, s, da = q.shape
    g = k.shape[1]
    d = HEAD_DIM
    tv = vt.shape[-1]
    tq = min(256, s)
    tk = min(512, s)
    assert tq & (tq - 1) == 0 and s % tk == 0 and tk % tv == 0 and g == 2
    return pl.pallas_call(
        functools.partial(_sel_kernel, tq=tq, tk=tk, nsub=tk // tv),
        grid=(b, s // tq),
        in_specs=[pl.BlockSpec((1, h, tq, da), lambda bi, i: (bi, 0, i, 0)),
                  pl.BlockSpec((1, g, s, da), lambda bi, i: (bi, 0, 0, 0)),
                  pl.BlockSpec((1, g) + vt.shape[2:], lambda bi, i: (bi, 0, 0, 0, 0)),
                  pl.BlockSpec((1, GATE_ROWS, tq), lambda bi, i: (bi, 0, i)),
                  pl.BlockSpec((tk, NSA_GROUP * tq), lambda bi, i: (0, 0))],
        out_specs=pl.BlockSpec((1, tq, h * d), lambda bi, i: (bi, i, 0)),
        out_shape=jax.ShapeDtypeStruct((b, s, h * d), _MXU),
        scratch_shapes=[pltpu.VMEM((tk, NSA_GROUP * tq), _F32) for _ in range(g)],
        compiler_params=_params("parallel", "arbitrary"),
        name="nsa_selected",
    )(q, k, vt, gates_t, _key_minus_query(tk, tq))


def _win_kernel(q_ref, k_ref, vt_ref, gt_ref, rel_ref, o_ref, s0_ref, s1_ref, *, tq, span, tv):
    q0 = pl.program_id(1) * tq
    start = pl.multiple_of(jnp.maximum(q0 - WINDOW, 0), tq)
    first_chunk = start // tv
    lanes = NSA_GROUP * tq
    s_refs = (s0_ref, s1_ref)
    for g in range(NSA_KV_HEADS):
        s_refs[g][...] = _dot_nt(k_ref[0, g, pl.ds(start, span), :], _group_queries(q_ref, g, tq))
    rel = rel_ref[...]
    offset = q0 - start
    heads = []
    for g in range(NSA_KV_HEADS):
        st = jnp.where(rel <= offset, s_refs[g][...], NEG_INF)
        st = jnp.where(rel > offset - WINDOW, st, NEG_INF)
        carry = _softmax_step_t(_softmax_init_t(HEAD_DIM, lanes), st,
                                [vt_ref[0, g, first_chunk + c] for c in range(span // tv)])
        heads += _gated_heads(_softmax_finish_t(carry, HEAD_DIM), gt_ref, g, 2, tq)
    o_ref[0] = jnp.concatenate(heads, axis=0).T.astype(o_ref.dtype)


def _win_attention(q, k, vt, gates_t):
    b, h, s, d = q.shape
    g = k.shape[1]
    tv = vt.shape[-1]
    tq = min(256, s)
    span = WINDOW + tq
    assert tq & (tq - 1) == 0 and s >= span and tq % tv == 0 and WINDOW % tv == 0 and g == 2
    return pl.pallas_call(
        functools.partial(_win_kernel, tq=tq, span=span, tv=tv),
        grid=(b, s // tq),
        in_specs=[pl.BlockSpec((1, h, tq, d), lambda bi, i: (bi, 0, i, 0)),
                  pl.BlockSpec((1, g, s, d), lambda bi, i: (bi, 0, 0, 0)),
                  pl.BlockSpec((1, g) + vt.shape[2:], lambda bi, i: (bi, 0, 0, 0, 0)),
                  pl.BlockSpec((1, GATE_ROWS, tq), lambda bi, i: (bi, 0, i)),
                  pl.BlockSpec((span, NSA_GROUP * tq), lambda bi, i: (0, 0))],
        out_specs=pl.BlockSpec((1, tq, h * d), lambda bi, i: (bi, i, 0)),
        out_shape=jax.ShapeDtypeStruct((b, s, h * d), _MXU),
        scratch_shapes=[pltpu.VMEM((span, NSA_GROUP * tq), _F32) for _ in range(g)],
        compiler_params=_params("parallel", "arbitrary"),
        name="nsa_window",
    )(q, k, vt, gates_t, _key_minus_query(span, tq))


def _out_kernel(x_ref, mla_ref, cmp_ref, sel_ref, win_ref, sb_ref, w_ref, o_ref):
    def w_rows(first_head, n_heads):
        return w_ref[first_head * HEAD_DIM:(first_head + n_heads) * HEAD_DIM, :]

    acc = x_ref[0] + jnp.dot(mla_ref[0], w_rows(0, MLA_HEADS), preferred_element_type=_F32)
    nsa = cmp_ref[0].astype(_F32) + sel_ref[0].astype(_F32) + win_ref[0].astype(_F32)
    acc = acc + _dot(nsa, w_rows(MLA_HEADS, NSA_HEADS))
    o_ref[0] = acc + jnp.dot(sb_ref[0], w_rows(MLA_HEADS + NSA_HEADS, SB_HEADS), preferred_element_type=_F32)


def _out_proj(x, o_mla, o_cmp, o_sel, o_win, o_sb, layer, w_heads):
    b, s, _ = x.shape
    ts = min(512, s)

    def rows(a):
        return pl.BlockSpec((1, ts, a.shape[2]), lambda bi, i: (bi, i, 0))

    xspec = pl.BlockSpec((1, ts, D_MODEL), lambda bi, i: (bi, i, 0))
    return pl.pallas_call(
        _out_kernel,
        grid=(b, s // ts),
        in_specs=[xspec, rows(o_mla), rows(o_cmp), rows(o_sel), rows(o_win), rows(o_sb),
                  _layer_spec(w_heads, layer)],
        out_specs=xspec,
        out_shape=jax.ShapeDtypeStruct(x.shape, _F32),
        compiler_params=_params("parallel", "parallel"),
        name="out_proj",
    )(x, o_mla, o_cmp, o_sel, o_win, o_sb, w_heads)


def _gather_cols(w, idx):
    idx = np.asarray(idx)
    cuts = [0] + [i for i in range(1, len(idx)) if idx[i] != idx[i - 1] + (idx[i - 1] >= 0)] + [len(idx)]
    pieces = []
    for a, b in zip(cuts[:-1], cuts[1:]):
        if idx[a] < 0:
            pieces.append(jnp.zeros(w.shape[:-1] + (b - a,), _MXU))
        else:
            pieces.append(w[..., int(idx[a]):int(idx[a]) + b - a].astype(_MXU))
    return jnp.concatenate(pieces, axis=-1)


def _swap_halves(rot):
    return (np.arange(rot) + rot // 2) % rot


def _w_in_index():
    idx = np.full((_N_HEAD_COLS * HEAD_DIM,), -1, np.int64)

    def put(col, src):
        src = np.asarray(src)
        idx[col:col + len(src)] = src

    def put_head(pos, src):
        put(pos * HEAD_DIM, src)

    put(_S_CQ * LANES, _O_CQ + np.arange(MLA_Q_LORA))
    put(_S_CKV * LANES, _O_CKV + np.arange(MLA_KV_LORA))
    put(_S_KR * LANES + MLA_NOPE, _O_KR + np.arange(MLA_ROPE))
    put(_S_KRS * LANES + MLA_NOPE, _O_KR + _swap_halves(MLA_ROPE))
    for h in range(NSA_HEADS):
        put_head(_H_NQ + h, _O_NQ + h * HEAD_DIM + np.arange(HEAD_DIM))
        put_head(_H_NQS + h, _O_NQ + h * HEAD_DIM + _swap_halves(PARTIAL_ROT))
    for hk, hks, ok in ((_H_KC, _H_KCS, _O_NKC), (_H_KS, _H_KSS, _O_NKS), (_H_KW, _H_KWS, _O_NKW)):
        for g in range(NSA_KV_HEADS):
            put_head(hk + g, ok + g * HEAD_DIM + np.arange(HEAD_DIM))
            put_head(hks + g, ok + g * HEAD_DIM + _swap_halves(PARTIAL_ROT))
    for g in range(NSA_KV_HEADS):
        put_head(_H_VC + g, _O_NVC + g * HEAD_DIM + np.arange(HEAD_DIM))
    for h in range(SB_HEADS):
        put_head(_H_SBQ + h, _O_SBQ + h * HEAD_DIM + np.arange(HEAD_DIM))
        put_head(_H_SBK + h, _O_SBK + h * HEAD_DIM + np.arange(HEAD_DIM))
    return idx


def _mla_up_index():
    qd = MLA_NOPE + MLA_ROPE
    kd = MLA_NOPE + MLA_V
    uq = np.full((MLA_HEADS * LANES,), -1, np.int64)
    uqs = uq.copy()
    uk = uq.copy()
    for h in range(MLA_HEADS):
        uq[h * LANES:h * LANES + qd] = h * qd + np.arange(qd)
        uqs[h * LANES + MLA_NOPE:h * LANES + qd] = h * qd + MLA_NOPE + _swap_halves(MLA_ROPE)
        uk[h * LANES:h * LANES + MLA_NOPE] = h * kd + np.arange(MLA_NOPE)
    return uq, uqs, uk


def _transposed_weights(w_in, gate_bias):
    width = NSA_KV_HEADS * HEAD_DIM
    gate_rows = jnp.pad(w_in[..., _O_GATE:_O_GATE + N_GATES], ((0, 0), (0, 0), (0, _T_SBV - _T_GATE - N_GATES)))
    rows = jnp.concatenate([w_in[..., _O_NVS:_O_NVS + width], w_in[..., _O_NVW:_O_NVW + width], gate_rows,
                            w_in[..., _O_SBV:_O_SBV + SB_HEADS * HEAD_DIM]], axis=-1)
    bias = jnp.pad(gate_bias, ((0, 0), (0, GATE_ROWS - N_GATES)))[..., None]
    return jnp.swapaxes(rows, -1, -2).astype(_MXU), bias


def _rope_tables(s):
    pos = jnp.arange(s, dtype=_F32)

    def cs(rot):
        half = rot // 2
        inv_freq = ROPE_THETA ** (-jnp.arange(half, dtype=_F32) / half)
        ang = pos[:, None] * inv_freq[None, :]
        c, sn = jnp.cos(ang), jnp.sin(ang)
        return jnp.concatenate([c, c], axis=1), jnp.concatenate([-sn, sn], axis=1)

    c, sn = cs(MLA_ROPE)
    ones = jnp.ones((s, MLA_NOPE), _F32)
    zeros = jnp.zeros((s, MLA_NOPE), _F32)
    pad = jnp.zeros((s, LANES - MLA_NOPE - MLA_ROPE), _F32)
    ck = jnp.concatenate([ones, c, pad], axis=1)
    sk = jnp.concatenate([zeros, sn, pad], axis=1)
    q_scale = (MLA_NOPE + MLA_ROPE) ** -0.5 * LOG2_E
    c, sn = cs(PARTIAL_ROT)
    c64 = jnp.concatenate([c, jnp.ones((s, HEAD_DIM - PARTIAL_ROT), _F32)], axis=1)
    s64 = jnp.concatenate([sn, jnp.zeros((s, HEAD_DIM - PARTIAL_ROT), _F32)], axis=1)
    ns = s // SEL_LEN
    onehot = (np.arange(s)[:, None] // SEL_LEN == np.arange(ns)[None, :]) * -NEG_INF
    return ck * q_scale, sk * q_scale, ck, sk, c64, s64, jnp.asarray(onehot, _MXU)


def kernel(x, ffn1_norm, ffn1_w_gate, ffn1_w_up, ffn1_w_down, mix_norm, w_in, mla_q_norm, mla_w_uq, mla_kv_norm,
           mla_w_ukv, nsa_gate_bias, nsa_cmp_pos_k, nsa_cmp_w1_k, nsa_cmp_w2_k, nsa_cmp_pos_v, nsa_cmp_w1_v,
           nsa_cmp_w2_v, w_out, ffn2_norm, ffn2_w_gate, ffn2_w_up, ffn2_w_down, final_norm):
    b, s, d = x.shape
    depth = w_in.shape[0]
    tabs = _rope_tables(s)
    in_idx = _w_in_index()
    uq_idx, uqs_idx, uk_idx = _mla_up_index()
    half = CMP_LEN * HEAD_DIM // 2
    fg = final_norm.reshape(1, d)

    def row(p):
        return p[:, None, :]

    def cmp_weights(w1, w2, pos, transpose_out):
        pos = jnp.broadcast_to(pos.reshape(depth, 2, 1, half), (depth, 2, 8, half)).astype(_MXU)
        w2 = jnp.swapaxes(w2, -1, -2) if transpose_out else w2
        return w1.reshape(depth, 2, half, CMP_HIDDEN).astype(_MXU), w2.astype(_MXU), pos

    ffn1 = [w.astype(_MXU) for w in (ffn1_w_gate, ffn1_w_up, ffn1_w_down)]
    ffn2 = [w.astype(_MXU) for w in (ffn2_w_gate, ffn2_w_up, ffn2_w_down)]
    w_t, gate_bias = _transposed_weights(w_in, nsa_gate_bias)
    wuv_t = mla_w_ukv.reshape(depth, MLA_KV_LORA, MLA_HEADS, 2, MLA_V)[:, :, :, 1].transpose(0, 2, 3, 1).astype(_MXU)
    proj_params = (row(mix_norm), _gather_cols(w_in, in_idx), w_t,
                   row(mla_q_norm), _gather_cols(mla_w_uq, uq_idx), _gather_cols(mla_w_uq, uqs_idx),
                   row(mla_kv_norm), _gather_cols(mla_w_ukv, uk_idx), wuv_t, gate_bias)
    cmp_params = (cmp_weights(nsa_cmp_w1_k, nsa_cmp_w2_k, nsa_cmp_pos_k, False)
                  + cmp_weights(nsa_cmp_w1_v, nsa_cmp_w2_v, nsa_cmp_pos_v, True))
    w_out = w_out.astype(_MXU)
    ffn1_norm, ffn2_norm = row(ffn1_norm), row(ffn2_norm)

    for l in range(depth):
        x2d = _ffn(x.reshape(b * s, d), ffn1_norm, *ffn1, l, fg, False)
        x = x2d.reshape(b, s, d)
        (mq, mk, mvt, nq, nkc, nvc, nks, nvst, nkw, nvwt, gates_t, sbq, sbk, sbv) = _proj(x, l, *proj_params, tabs)
        o_mla = _mla_attention(mq, mk, mvt)
        kc, vct = _compress(nkc, nvc, l, *cmp_params)
        o_cmp, q_sel = _cmp_select(nq, kc, vct, gates_t)
        o_sel = _sel_attention(q_sel, nks, nvst, gates_t)
        o_win = _win_attention(nq, nkw, nvwt, gates_t)
        o_sb = _sb_attention(sbq, sbk, sbv)
        x = _out_proj(x, o_mla, o_cmp, o_sel, o_win, o_sb, l, w_out)
        x2d = _ffn(x.reshape(b * s, d), ffn2_norm, *ffn2, l, fg, l == depth - 1)
        x = x2d.reshape(b, s, d)
    return x
```

```python
import functools
import math

import numpy as np
import jax
import jax.numpy as jnp
from jax import lax
from jax.experimental import pallas as pl
from jax.experimental.pallas import tpu as pltpu

D_MODEL = 1024
HEAD_DIM = 64
MLA_HEADS = 6
MLA_NOPE = 64
MLA_ROPE = 32
MLA_V = 64
MLA_Q_LORA = 256
MLA_KV_LORA = 128
NSA_HEADS = 6
NSA_KV_HEADS = 2
NSA_GROUP = NSA_HEADS // NSA_KV_HEADS
NSA_BRANCHES = 3
CMP_LEN = 32
CMP_STRIDE = 16
CMP_HIDDEN = 128
SEL_LEN = 64
SEL_TOPK = 16
WINDOW = 512
SB_HEADS = 4
D_FF = 2816
ROPE_THETA = 500000.0
PARTIAL_ROT = HEAD_DIM // 4
EPS = 1e-6
NEG_INF = -1e30
M_FLOOR = 0.1 * NEG_INF
FORCE_SCORE = 1e4
PICKED = -3e38
F32_EXP2_ZERO = -151.0
LOG2_E = math.log2(math.e)
N_GATES = NSA_HEADS * NSA_BRANCHES

LANES = 128
FFN_CHUNK = 256
SWEEP_UNROLL = 4
SB_KEY_BLOCK = 256
TOKEN_CHUNK = 256
ONES_PAD = 16
VMEM_LIMIT = 56 * 1024 * 1024

_MXU = jnp.bfloat16
_F32 = jnp.float32

_IN_WIDTHS = (MLA_Q_LORA, MLA_KV_LORA, MLA_ROPE, NSA_HEADS * HEAD_DIM) + (NSA_KV_HEADS * HEAD_DIM,) * 6 + (
    N_GATES, SB_HEADS * HEAD_DIM, SB_HEADS * HEAD_DIM, SB_HEADS * HEAD_DIM)
_IN_OFF = np.concatenate([[0], np.cumsum(_IN_WIDTHS)])
(_O_CQ, _O_CKV, _O_KR, _O_NQ, _O_NKC, _O_NVC, _O_NKS, _O_NVS, _O_NKW, _O_NVW, _O_GATE, _O_SBQ, _O_SBK,
 _O_SBV) = [int(v) for v in _IN_OFF[:-1]]

_S_CQ, _S_CKV, _S_KR, _S_KRS = 0, 2, 3, 4
_H_NQ, _H_NQS = 10, 16
_H_KC, _H_KCS, _H_VC = 22, 24, 26
_H_KS, _H_KSS = 28, 30
_H_KW, _H_KWS = 32, 34
_H_SBQ, _H_SBK = 36, 40
_N_HEAD_COLS = 44
_T_VS, _T_VW, _T_GATE = 0, NSA_KV_HEADS * HEAD_DIM, 2 * NSA_KV_HEADS * HEAD_DIM
GATE_ROWS = 24
_T_SBV = _T_GATE + 32
_T_ROWS = _T_SBV + SB_HEADS * HEAD_DIM


def _dot(a, b):
    return jnp.dot(a.astype(_MXU), b.astype(_MXU), preferred_element_type=_F32)


def _dot_nt(a, b):
    return lax.dot_general(a.astype(_MXU), b.astype(_MXU), (((1,), (1,)), ((), ())),
                           preferred_element_type=_F32)


def _dot_split_rhs(a, b):
    hi = b.astype(_MXU)
    lo = (b - hi.astype(_F32)).astype(_MXU)
    return (jnp.dot(a, hi, preferred_element_type=_F32) + jnp.dot(a, lo, preferred_element_type=_F32))


def _rms(x, g):
    return x * lax.rsqrt(jnp.mean(x * x, axis=-1, keepdims=True) + EPS) * g


def _params(*sem):
    return pltpu.CompilerParams(dimension_semantics=sem, vmem_limit_bytes=VMEM_LIMIT)


def _layer_spec(a, layer):
    return pl.BlockSpec((None,) + a.shape[1:], lambda *_: (layer,) + (0,) * (a.ndim - 1))


def _ffn_kernel(x_ref, g_ref, wg_ref, wu_ref, wd_ref, fg_ref, o_ref, h_ref, acc_ref, act_ref, *, final_norm):
    j = pl.program_id(1)

    @pl.when(j == 0)
    def _():
        h_ref[...] = _rms(x_ref[...], g_ref[...]).astype(h_ref.dtype)
        acc_ref[...] = jnp.zeros_like(acc_ref)

    h = h_ref[...]
    tf = act_ref.shape[1]
    for c0 in range(0, tf, FFN_CHUNK):
        c1 = min(c0 + FFN_CHUNK, tf)
        gate = jnp.dot(h, wg_ref[:, c0:c1], preferred_element_type=_F32)
        up = jnp.dot(h, wu_ref[:, c0:c1], preferred_element_type=_F32)
        act_ref[:, c0:c1] = (gate * jax.nn.sigmoid(gate) * up).astype(act_ref.dtype)
    acc_ref[...] += jnp.dot(act_ref[...], wd_ref[...], preferred_element_type=_F32)

    @pl.when(j == pl.num_programs(1) - 1)
    def _():
        y = x_ref[...] + 0.5 * acc_ref[...]
        if final_norm:
            y = _rms(y, fg_ref[...])
        o_ref[...] = y


def _ffn(x2d, g, wg, wu, wd, layer, fg, final_norm):
    rows = x2d.shape[0]
    tm = min(1024, rows)
    tf = D_FF // 2
    grid = (rows // tm, D_FF // tf)
    return pl.pallas_call(
        functools.partial(_ffn_kernel, final_norm=final_norm),
        grid=grid,
        in_specs=[
            pl.BlockSpec((tm, D_MODEL), lambda i, j: (i, 0)),
            _layer_spec(g, layer),
            pl.BlockSpec((None, D_MODEL, tf), lambda i, j: (layer, 0, j)),
            pl.BlockSpec((None, D_MODEL, tf), lambda i, j: (layer, 0, j)),
            pl.BlockSpec((None, tf, D_MODEL), lambda i, j: (layer, j, 0)),
            pl.BlockSpec((1, D_MODEL), lambda i, j: (0, 0)),
        ],
        out_specs=pl.BlockSpec((tm, D_MODEL), lambda i, j: (i, 0)),
        out_shape=jax.ShapeDtypeStruct((rows, D_MODEL), _F32),
        scratch_shapes=[pltpu.VMEM((tm, D_MODEL), _MXU), pltpu.VMEM((tm, D_MODEL), _F32), pltpu.VMEM((tm, tf), _MXU)],
        compiler_params=_params("parallel", "arbitrary"),
        name="ffn",
    )(x2d, g, wg, wu, wd, fg)


def _proj_kernel(x_ref, g_ref, w_ref, wt_ref, qn_ref, wuq_ref, wuqs_ref, kvn_ref, wuk_ref, wuv_ref, gb_ref,
                 cq_ref, sq_ref, ck_ref, sk_ref, c64_ref, s64_ref, oh_ref,
                 mq_ref, mk_ref, mv_ref, nq_ref, nkc_ref, nvc_ref, nks_ref, nvs_ref, nkw_ref, nvw_ref,
                 gate_ref, sbq_ref, sbk_ref, sbv_ref, stage_ref):
    hn = _rms(x_ref[0], g_ref[...]).astype(_MXU)

    def proj(h0, h1):
        return jnp.dot(hn, w_ref[:, h0 * HEAD_DIM:h1 * HEAD_DIM], preferred_element_type=_F32)

    def slot(p, s):
        return p[:, s * LANES:(s + 1) * LANES]

    def head(p, i):
        return p[:, i * HEAD_DIM:(i + 1) * HEAD_DIM]

    p = proj(0, _H_NQ)
    cq = _rms(p[:, :MLA_Q_LORA], qn_ref[...])
    ckv = _rms(slot(p, _S_CKV), kvn_ref[...])
    q = _dot(cq, wuq_ref[...])
    q_partner = _dot(cq, wuqs_ref[...])
    kpe = slot(p, _S_KR) * ck_ref[...] + slot(p, _S_KRS) * sk_ref[...]
    kn = _dot(ckv, wuk_ref[...])
    for h in range(MLA_HEADS):
        mq_ref[0, h] = (slot(q, h) * cq_ref[...] + slot(q_partner, h) * sq_ref[...]).astype(mq_ref.dtype)
        mk_ref[0, h] = (slot(kn, h) + kpe).astype(mk_ref.dtype)
        mv_ref[0, h, 0] = _ones_row_pad(_dot_nt(wuv_ref[h], ckv)).astype(mv_ref.dtype)

    c64 = c64_ref[...]
    s64 = s64_ref[...]
    scale = HEAD_DIM ** -0.5

    p = proj(_H_NQ, _H_KC)
    for h in range(NSA_HEADS):
        nq_ref[0, h] = ((head(p, h) * c64 + head(p, NSA_HEADS + h) * s64) * (scale * LOG2_E)).astype(nq_ref.dtype)

    p = proj(_H_KC, _H_SBQ)
    base = _H_KC
    ns = oh_ref.shape[-1]

    def roped(hk, hks, g):
        return head(p, hk - base + g) * c64 + head(p, hks - base + g) * s64

    def write_chunked(o_ref, g, val):
        stage_ref[...] = val
        for t in range(CMP_STRIDE):
            piece = stage_ref[pl.ds(t, val.shape[0] // CMP_STRIDE, stride=CMP_STRIDE), :]
            o_ref[0, g, :, t * HEAD_DIM:(t + 1) * HEAD_DIM] = piece.astype(o_ref.dtype)

    for g in range(NSA_KV_HEADS):
        write_chunked(nkc_ref, g, roped(_H_KC, _H_KCS, g))
        write_chunked(nvc_ref, g, head(p, _H_VC - base + g))
        nks_ref[0, g, :, 0:ns] = oh_ref[...]
        nks_ref[0, g, :, ns:ns + HEAD_DIM] = roped(_H_KS, _H_KSS, g).astype(nks_ref.dtype)
        nkw_ref[0, g] = roped(_H_KW, _H_KWS, g).astype(nkw_ref.dtype)

    pt = _dot_nt(wt_ref[...], hn)
    for g in range(NSA_KV_HEADS):
        lo = g * HEAD_DIM
        nvs_ref[0, g, 0] = _ones_row_pad(pt[_T_VS + lo:_T_VS + lo + HEAD_DIM]).astype(nvs_ref.dtype)
        nvw_ref[0, g, 0] = _ones_row_pad(pt[_T_VW + lo:_T_VW + lo + HEAD_DIM]).astype(nvw_ref.dtype)
    gate_ref[0] = jax.nn.sigmoid(pt[_T_GATE:_T_GATE + GATE_ROWS] + gb_ref[...])

    p = proj(_H_SBQ, _N_HEAD_COLS)
    for h in range(SB_HEADS):
        sbq_ref[0, h] = (head(p, h) * (scale * LOG2_E)).astype(sbq_ref.dtype)
        sbk_ref[0, h] = head(p, SB_HEADS + h).astype(sbk_ref.dtype)
        for c in range(sbv_ref.shape[2]):
            sbv_ref[0, h, c] = pt[_T_SBV + h * HEAD_DIM:_T_SBV + (h + 1) * HEAD_DIM,
                                  c * SB_KEY_BLOCK:(c + 1) * SB_KEY_BLOCK].astype(sbv_ref.dtype)


def _proj(x, layer, g, w_ext, w_t, qn, wuq, wuqs, kvn, wuk, wuv, gb, tabs):
    b, s, _ = x.shape
    ts = min(TOKEN_CHUNK, s)
    cq, sq, ck, sk, c64, s64, onehot = tabs
    full = functools.partial(_layer_spec, layer=layer)

    def tab(a):
        return pl.BlockSpec((ts, a.shape[1]), lambda bi, i: (i, 0))

    def heads(n, d):
        return (pl.BlockSpec((1, n, ts, d), lambda bi, i: (bi, 0, i, 0)),
                jax.ShapeDtypeStruct((b, n, s, d), _MXU))

    def values_t(n):
        return (pl.BlockSpec((1, n, 1, HEAD_DIM + ONES_PAD, ts), lambda bi, i: (bi, 0, i, 0, 0)),
                jax.ShapeDtypeStruct((b, n, s // ts, HEAD_DIM + ONES_PAD, ts), _MXU))

    def chunked():
        return (pl.BlockSpec((1, NSA_KV_HEADS, ts // CMP_STRIDE, CMP_STRIDE * HEAD_DIM), lambda bi, i: (bi, 0, i, 0)),
                jax.ShapeDtypeStruct((b, NSA_KV_HEADS, s // CMP_STRIDE, CMP_STRIDE * HEAD_DIM), _MXU))

    outs = [heads(MLA_HEADS, LANES), heads(MLA_HEADS, LANES), values_t(MLA_HEADS), heads(NSA_HEADS, HEAD_DIM),
            chunked(), chunked(), heads(NSA_KV_HEADS, onehot.shape[1] + HEAD_DIM), values_t(NSA_KV_HEADS),
            heads(NSA_KV_HEADS, HEAD_DIM), values_t(NSA_KV_HEADS),
            (pl.BlockSpec((1, GATE_ROWS, ts), lambda bi, i: (bi, 0, i)), jax.ShapeDtypeStruct((b, GATE_ROWS, s), _F32)),
            heads(SB_HEADS, HEAD_DIM), heads(SB_HEADS, HEAD_DIM),
            (pl.BlockSpec((1, SB_HEADS, ts // SB_KEY_BLOCK, HEAD_DIM, SB_KEY_BLOCK), lambda bi, i: (bi, 0, i, 0, 0)),
             jax.ShapeDtypeStruct((b, SB_HEADS, s // SB_KEY_BLOCK, HEAD_DIM, SB_KEY_BLOCK), _MXU))]
    return pl.pallas_call(
        _proj_kernel,
        grid=(b, s // ts),
        in_specs=[pl.BlockSpec((1, ts, D_MODEL), lambda bi, i: (bi, i, 0)), full(g), full(w_ext), full(w_t), full(qn),
                  full(wuq), full(wuqs), full(kvn), full(wuk), full(wuv), full(gb),
                  tab(cq), tab(sq), tab(ck), tab(sk), tab(c64), tab(s64), tab(onehot)],
        out_specs=[o[0] for o in outs],
        out_shape=[o[1] for o in outs],
        scratch_shapes=[pltpu.VMEM((ts, HEAD_DIM), _F32)],
        compiler_params=_params("parallel", "parallel"),
        name="proj",
    )(x, g, w_ext, w_t, qn, wuq, wuqs, kvn, wuk, wuv, gb, cq, sq, ck, sk, c64, s64, onehot)


def _ones_row_pad(vt):
    first = lax.broadcasted_iota(jnp.int32, (ONES_PAD, vt.shape[1]), 0) == 0
    return jnp.concatenate([vt, jnp.where(first, 1.0, 0.0).astype(vt.dtype)], axis=0)


def _softmax_step_t(carry, st, vt_chunks):
    m, acc = carry
    m_new = jnp.maximum(m, jnp.max(st, axis=0, keepdims=True))
    alpha = jnp.exp2(m - m_new)
    pt = jnp.exp2(st - m_new).astype(_MXU)
    n = st.shape[0] // len(vt_chunks)
    pv = sum(jnp.dot(vt, pt[c * n:(c + 1) * n], preferred_element_type=_F32) for c, vt in enumerate(vt_chunks))
    return m_new, alpha * acc + pv


def _softmax_init_t(d, cols):
    return (jnp.full((1, cols), M_FLOOR, _F32), jnp.zeros((d + ONES_PAD, cols), _F32))


def _softmax_finish_t(carry, d):
    _, acc = carry
    return acc[:d] * (1.0 / acc[d:d + 1])


def _two_chain_sweep(n_full, qk, soft, init):
    def body(j, carry, diag=False):
        c0, c1 = carry
        qk(0, j)
        c1 = soft(1, j, c1, diag)
        qk(1, jnp.zeros_like(j) if diag else j + 1)
        c0 = soft(0, j, c0, diag)
        return c0, c1

    def unrolled(i, carry):
        for u in range(SWEEP_UNROLL):
            carry = body(SWEEP_UNROLL * i + u, carry)
        return carry

    qk(1, n_full)
    carry = body(n_full, init, True)
    trips = n_full // SWEEP_UNROLL
    carry = lax.fori_loop(0, trips, unrolled, carry)
    return lax.fori_loop(SWEEP_UNROLL * trips, n_full, body, carry)


def _mla_kernel(q_ref, k_ref, vt_ref, o_ref, s0_ref, s1_ref, *, t, nsub):
    qi = pl.program_id(2)
    s_refs = (s0_ref, s1_ref)

    def qk(hh, j):
        off = pl.multiple_of(j * t, t)
        s_refs[hh][...] = _dot_nt(k_ref[0, hh, pl.ds(off, t), :], q_ref[0, hh])

    def soft(hh, j, carry, diag):
        st = s_refs[hh][...]
        if diag:
            key = lax.broadcasted_iota(jnp.int32, (t, t), 0)
            qry = lax.broadcasted_iota(jnp.int32, (t, t), 1)
            st = jnp.where(key <= qry, st, NEG_INF)
        return _softmax_step_t(carry, st, [vt_ref[0, hh, j * nsub + c] for c in range(nsub)])

    carry = _two_chain_sweep(qi, qk, soft, tuple(_softmax_init_t(MLA_V, t) for _ in range(2)))
    ot = jnp.concatenate([_softmax_finish_t(c, MLA_V) for c in carry], axis=0)
    o_ref[0] = ot.T.astype(o_ref.dtype)


def _mla_attention(q, k, vt):
    b, h, s, _ = q.shape
    tv = vt.shape[-1]
    dv = vt.shape[-2]
    t = min(512, s)
    assert h % 2 == 0 and 2 * MLA_V == LANES and t % tv == 0 and s % t == 0
    return pl.pallas_call(
        functools.partial(_mla_kernel, t=t, nsub=t // tv),
        grid=(b, h // 2, s // t),
        in_specs=[pl.BlockSpec((1, 2, t, LANES), lambda bi, hi, i: (bi, hi, i, 0)),
                  pl.BlockSpec((1, 2, s, LANES), lambda bi, hi, i: (bi, hi, 0, 0)),
                  pl.BlockSpec((1, 2, s // tv, dv, tv), lambda bi, hi, i: (bi, hi, 0, 0, 0))],
        out_specs=pl.BlockSpec((1, t, LANES), lambda bi, hi, i: (bi, i, hi)),
        out_shape=jax.ShapeDtypeStruct((b, s, h * MLA_V), _MXU),
        scratch_shapes=[pltpu.VMEM((t, t), _F32), pltpu.VMEM((t, t), _F32)],
        compiler_params=_params("parallel", "parallel", "arbitrary"),
        name="mla_attn",
    )(q, k, vt)


def _sb_kernel(q_ref, k_ref, vt_ref, u_ref, o_ref, *scratch_refs, tq, tk):
    qi = pl.program_id(2)
    u = u_ref[...]
    scratch = (scratch_refs[:6], scratch_refs[6:])
    per_tile = tq // tk

    def step(j, carry, key_offset=None):
        diag = key_offset is not None
        off = pl.multiple_of(j * tk, tk)
        if diag:
            key = key_offset + lax.broadcasted_iota(jnp.int32, (tk, tq), 0)
            qry = lax.broadcasted_iota(jnp.int32, (tk, tq), 1)
            strict = key < qry

        def logits(hh):
            z_ref, _, _, _, _, _ = scratch[hh]
            z_ref[...] = _dot_nt(k_ref[0, hh, pl.ds(off, tk), :], q_ref[0, hh])

        def log_terms(hh):
            z_ref, lb_ref, hi_ref, lo_ref, _, _ = scratch[hh]
            z = z_ref[...]
            log_beta = jnp.minimum(z, 0.0) - jnp.log2(1.0 + jnp.exp2(-jnp.abs(z)))
            log_rem = log_beta - z
            if diag:
                log_rem = jnp.where(strict, log_rem, 0.0)
            hi = log_rem.astype(_MXU)
            lb_ref[...] = log_beta
            hi_ref[...] = hi
            lo_ref[...] = (log_rem - hi.astype(_F32)).astype(_MXU)
            return log_rem[0:1, :]

        def suffix_sums(hh):
            _, _, hi_ref, lo_ref, sfx_ref, _ = scratch[hh]
            sfx_ref[...] = (jnp.dot(u, hi_ref[...], preferred_element_type=_F32)
                            + jnp.dot(u, lo_ref[...], preferred_element_type=_F32))

        def weights(hh, first_rem):
            _, lb_ref, _, _, sfx_ref, a_ref = scratch[hh]
            rem = carry[hh][0]
            suffix = sfx_ref[...]
            a = jnp.exp2(lb_ref[...] + suffix + rem)
            if diag:
                a = jnp.where(strict, a, 0.0)
            a_ref[...] = a.astype(_MXU)
            return rem + suffix[0:1, :] + first_rem

        def values(hh):
            a_ref = scratch[hh][5]
            return carry[hh][1] + jnp.dot(vt_ref[0, hh, j], a_ref[...], preferred_element_type=_F32)

        logits(0)
        logits(1)
        first0 = log_terms(0)
        suffix_sums(0)
        first1 = log_terms(1)
        rem0 = weights(0, first0)
        suffix_sums(1)
        acc0 = values(0)
        rem1 = weights(1, first1)
        alive = jnp.max(jnp.maximum(rem0, rem1)) > F32_EXP2_ZERO
        acc1 = values(1)
        return alive, ((rem0, acc0), (rem1, acc1))

    carry = tuple((jnp.zeros((1, tq), _F32), jnp.zeros((HEAD_DIM, tq), _F32)) for _ in range(2))
    first = qi * per_tile
    for i in reversed(range(per_tile)):
        _, carry = step(first + i, carry, key_offset=i * tk)
    has_past = first > 0
    carry = tuple((jnp.where(has_past, rem, NEG_INF), acc) for rem, acc in carry)
    alive, carry = step(jnp.maximum(first - 1, 0), carry)

    def earlier(c):
        return (c[0] - 1,) + step(c[0], c[2])

    _, _, carry = lax.while_loop(lambda c: jnp.logical_and(c[0] >= 0, c[1]), earlier, (first - 2, alive, carry))
    o_ref[0] = jnp.concatenate([acc for _, acc in carry], axis=0).T.astype(o_ref.dtype)


def _sb_attention(q, k, vt):
    b, h, s, d = q.shape
    tk = vt.shape[-1]
    tq = tk
    assert h % 2 == 0 and 2 * d == LANES and s % tq == 0 and tq % tk == 0
    idx = np.arange(tk)
    u = jnp.asarray(idx[None, :] > idx[:, None], _MXU)
    return pl.pallas_call(
        functools.partial(_sb_kernel, tq=tq, tk=tk),
        grid=(b, h // 2, s // tq),
        in_specs=[pl.BlockSpec((1, 2, tq, d), lambda bi, hi, i: (bi, hi, i, 0)),
                  pl.BlockSpec((1, 2, s, d), lambda bi, hi, i: (bi, hi, 0, 0)),
                  pl.BlockSpec((1, 2, s // tk, d, tk), lambda bi, hi, i: (bi, hi, 0, 0, 0)),
                  pl.BlockSpec((tk, tk), lambda bi, hi, i: (0, 0))],
        out_specs=pl.BlockSpec((1, tq, LANES), lambda bi, hi, i: (bi, i, hi)),
        out_shape=jax.ShapeDtypeStruct((b, s, h * d), _MXU),
        scratch_shapes=[pltpu.VMEM((tk, tq), dt) for _ in range(2) for dt in (_F32, _F32, _MXU, _MXU, _F32, _MXU)],
        compiler_params=_params("parallel", "parallel", "arbitrary"),
        name="sb_attn",
    )(q, k, vt, u)


def _compress_kernel(xk_ref, xv_ref, w1k_ref, w2k_ref, pk_ref, w1v_ref, w2v_ref, pv_ref, ok_ref, ov_ref):
    def hidden(x_ref, w1_ref, p_ref):
        x = x_ref[0, 0]
        n = x.shape[0]
        first = jnp.dot(x, w1_ref[0], preferred_element_type=_F32)
        second = jnp.dot(x, w1_ref[1], preferred_element_type=_F32)
        pos = _dot(p_ref[0], w1_ref[0]) + _dot(p_ref[1], w1_ref[1])
        hid = first + pltpu.roll(second, n - 1, 0) + pos[0:1]
        return 0.5 * hid * (1.0 + jnp.tanh(math.sqrt(2.0 / math.pi) * (hid + 0.044715 * hid * hid * hid)))

    ok_ref[0, 0] = _dot(hidden(xk_ref, w1k_ref, pk_ref), w2k_ref[...]).astype(ok_ref.dtype)
    ov_ref[0, 0] = _dot_nt(w2v_ref[...], hidden(xv_ref, w1v_ref, pv_ref)).astype(ov_ref.dtype)


def _compress(xk, xv, layer, w1k, w2k, pk, w1v, w2v, pv):
    b, g, n, _ = xk.shape
    d = HEAD_DIM
    full = functools.partial(_layer_spec, layer=layer)

    xspec = pl.BlockSpec((1, 1, n, CMP_STRIDE * d), lambda bi, gi: (bi, gi, 0, 0))
    return pl.pallas_call(
        _compress_kernel,
        grid=(b, g),
        in_specs=[xspec, xspec, full(w1k), full(w2k), full(pk), full(w1v), full(w2v), full(pv)],
        out_specs=[pl.BlockSpec((1, 1, n, d), lambda bi, gi: (bi, gi, 0, 0)),
                   pl.BlockSpec((1, 1, d, n), lambda bi, gi: (bi, gi, 0, 0))],
        out_shape=[jax.ShapeDtypeStruct((b, g, n, d), _MXU), jax.ShapeDtypeStruct((b, g, d, n), _MXU)],
        compiler_params=_params("parallel", "parallel"),
        name="nsa_compress",
    )(xk, xv, w1k, w2k, pk, w1v, w2v, pv)


def _group_queries(q_ref, g, tq):
    return q_ref[0, g * NSA_GROUP:(g + 1) * NSA_GROUP].reshape(NSA_GROUP * tq, q_ref.shape[-1])


def _gated_heads(ot, gt_ref, g, branch, tq):
    out = []
    for r in range(NSA_GROUP):
        row = NSA_BRANCHES * (g * NSA_GROUP + r) + branch
        out.append(ot[:, r * tq:(r + 1) * tq] * gt_ref[0, row:row + 1, :])
    return out


def _cmp_kernel(q_ref, kc_ref, vct_ref, ov_ref, gt_ref, o_ref, qa_ref, s0_ref, s1_ref, *, tq, n_top):
    q0 = pl.program_id(1) * tq
    ncp = kc_ref.shape[2]
    ns = ov_ref.shape[0]
    lanes = NSA_GROUP * tq
    s_refs = (s0_ref, s1_ref)
    for g in range(NSA_KV_HEADS):
        s_refs[g][...] = _dot_nt(kc_ref[0, g], _group_queries(q_ref, g, tq))
    qpos = q0 + (lax.broadcasted_iota(jnp.int32, (1, lanes), 1) & (tq - 1))
    cmp_end = lax.broadcasted_iota(jnp.int32, (ncp, 1), 0) * CMP_STRIDE + (CMP_LEN - 1)
    visible = cmp_end <= qpos
    cur = jnp.right_shift(q0 + lax.broadcasted_iota(jnp.int32, (1, tq), 1), int(math.log2(SEL_LEN)))
    blk = lax.broadcasted_iota(jnp.int32, (ns, 1), 0)
    forced = (blk == 0) | (blk == cur) | (blk == cur - 1)
    future = blk > cur
    blk_f = blk.astype(_F32)
    heads = []
    scores = []
    for g in range(NSA_KV_HEADS):
        st = jnp.where(visible, s_refs[g][...], NEG_INF)
        e = jnp.exp2(st - jnp.max(st, axis=0, keepdims=True))
        inv = jnp.where(qpos >= CMP_LEN - 1, 1.0 / jnp.sum(e, axis=0, keepdims=True), 0.0)
        pt = e * inv
        heads += _gated_heads(_dot(vct_ref[0, g], pt), gt_ref, g, 0, tq)
        p_sum = sum(pt[:, r * tq:(r + 1) * tq] for r in range(NSA_GROUP))
        score = _dot_split_rhs(ov_ref[...], p_sum)
        scores.append(jnp.where(forced, FORCE_SCORE, jnp.where(future, -1.0, score)))
    o_ref[0] = jnp.concatenate(heads, axis=0).T.astype(o_ref.dtype)
    unselected = [jnp.full((ns, tq), -1.0, _F32) for _ in range(NSA_KV_HEADS)]
    for _ in range(n_top):
        for g in range(NSA_KV_HEADS):
            top = jnp.max(scores[g], axis=0, keepdims=True)
            first = jnp.min(jnp.where(scores[g] == top, blk_f, float(ns)), axis=0, keepdims=True)
            pick = blk_f == first
            unselected[g] = jnp.where(pick, 0.0, unselected[g])
            scores[g] = jnp.where(pick, PICKED, scores[g])
    for g in range(NSA_KV_HEADS):
        sel_m1 = unselected[g].T.astype(qa_ref.dtype)
        for h in range(g * NSA_GROUP, (g + 1) * NSA_GROUP):
            qa_ref[0, h, :, 0:ns] = sel_m1
            qa_ref[0, h, :, ns:ns + HEAD_DIM] = q_ref[0, h]


def _cmp_select(q, kc, vct, gates_t):
    b, h, s, d = q.shape
    g = kc.shape[1]
    ncp = kc.shape[2]
    ns = s // SEL_LEN
    n_top = min(SEL_TOPK, ns)
    tq = min(256, s)
    assert tq & (tq - 1) == 0 and g == 2
    c0 = np.arange(ncp)[:, None] * CMP_STRIDE
    n0 = np.arange(ns)[None, :] * SEL_LEN
    overlap = jnp.asarray(((c0 < n0 + SEL_LEN) & (c0 + CMP_LEN > n0)).T, _MXU)
    return pl.pallas_call(
        functools.partial(_cmp_kernel, tq=tq, n_top=n_top),
        grid=(b, s // tq),
        in_specs=[pl.BlockSpec((1, h, tq, d), lambda bi, i: (bi, 0, i, 0)),
                  pl.BlockSpec((1, g, ncp, d), lambda bi, i: (bi, 0, 0, 0)),
                  pl.BlockSpec((1, g, d, ncp), lambda bi, i: (bi, 0, 0, 0)),
                  pl.BlockSpec((ns, ncp), lambda bi, i: (0, 0)),
                  pl.BlockSpec((1, GATE_ROWS, tq), lambda bi, i: (bi, 0, i))],
        out_specs=[pl.BlockSpec((1, tq, h * d), lambda bi, i: (bi, i, 0)),
                   pl.BlockSpec((1, h, tq, ns + d), lambda bi, i: (bi, 0, i, 0))],
        out_shape=[jax.ShapeDtypeStruct((b, s, h * d), _MXU), jax.ShapeDtypeStruct((b, h, s, ns + d), _MXU)],
        scratch_shapes=[pltpu.VMEM((ncp, NSA_GROUP * tq), _F32) for _ in range(g)],
        compiler_params=_params("parallel", "arbitrary"),
        name="nsa_cmp_select",
    )(q, kc, vct, overlap, gates_t)


def _key_minus_query(keys, tq):
    return jnp.asarray(np.arange(keys)[:, None] - np.arange(NSA_GROUP * tq)[None, :] % tq, jnp.int32)


def _sel_kernel(q_ref, k_ref, vt_ref, gt_ref, rel_ref, o_ref, s0_ref, s1_ref, *, tq, tk, nsub):
    q0 = pl.program_id(1) * tq
    last = (q0 + tq - 1) // tk
    lanes = NSA_GROUP * tq
    s_refs = (s0_ref, s1_ref)

    def qk(g, j):
        off = pl.multiple_of(j * tk, tk)
        s_refs[g][...] = _dot_nt(k_ref[0, g, pl.ds(off, tk), :], _group_queries(q_ref, g, tq))

    def soft(g, j, carry, causal):
        st = s_refs[g][...]
        if causal:
            st = jnp.where(rel_ref[...] <= q0 - j * tk, st, NEG_INF)
        return _softmax_step_t(carry, st, [vt_ref[0, g, j * nsub + c] for c in range(nsub)])

    init = tuple(_softmax_init_t(HEAD_DIM, lanes) for _ in range(NSA_KV_HEADS))
    carry = _two_chain_sweep(last, qk, soft, init)
    heads = []
    for g in range(NSA_KV_HEADS):
        heads += _gated_heads(_softmax_finish_t(carry[g], HEAD_DIM), gt_ref, g, 1, tq)
    o_ref[0] = jnp.concatenate(heads, axis=0).T.astype(o_ref.dtype)


def _sel_attention(q, k, vt, gates_t):
    b, h, s, da = q.shape
    g = k.shape[1]
    d = HEAD_DIM
    tv = vt.shape[-1]
    tq = min(256, s)
    tk = min(512, s)
    assert tq & (tq - 1) == 0 and s % tk == 0 and tk % tv == 0 and g == 2
    return pl.pallas_call(
        functools.partial(_sel_kernel, tq=tq, tk=tk, nsub=tk // tv),
        grid=(b, s // tq),
        in_specs=[pl.BlockSpec((1, h, tq, da), lambda bi, i: (bi, 0, i, 0)),
                  pl.BlockSpec((1, g, s, da), lambda bi, i: (bi, 0, 0, 0)),
                  pl.BlockSpec((1, g) + vt.shape[2:], lambda bi, i: (bi, 0, 0, 0, 0)),
                  pl.BlockSpec((1, GATE_ROWS, tq), lambda bi, i: (bi, 0, i)),
                  pl.BlockSpec((tk, NSA_GROUP * tq), lambda bi, i: (0, 0))],
        out_specs=pl.BlockSpec((1, tq, h * d), lambda bi, i: (bi, i, 0)),
        out_shape=jax.ShapeDtypeStruct((b, s, h * d), _MXU),
        scratch_shapes=[pltpu.VMEM((tk, NSA_GROUP * tq), _F32) for _ in range(g)],
        compiler_params=_params("parallel", "arbitrary"),
        name="nsa_selected",
    )(q, k, vt, gates_t, _key_minus_query(tk, tq))


def _win_kernel(q_ref, k_ref, vt_ref, gt_ref, rel_ref, o_ref, s0_ref, s1_ref, *, tq, span, tv):
    q0 = pl.program_id(1) * tq
    start = pl.multiple_of(jnp.maximum(q0 - WINDOW, 0), tq)
    first_chunk = start // tv
    lanes = NSA_GROUP * tq
    s_refs = (s0_ref, s1_ref)
    for g in range(NSA_KV_HEADS):
        s_refs[g][...] = _dot_nt(k_ref[0, g, pl.ds(start, span), :], _group_queries(q_ref, g, tq))
    rel = rel_ref[...]
    offset = q0 - start
    heads = []
    for g in range(NSA_KV_HEADS):
        st = jnp.where(rel <= offset, s_refs[g][...], NEG_INF)
        st = jnp.where(rel > offset - WINDOW, st, NEG_INF)
        carry = _softmax_step_t(_softmax_init_t(HEAD_DIM, lanes), st,
                                [vt_ref[0, g, first_chunk + c] for c in range(span // tv)])
        heads += _gated_heads(_softmax_finish_t(carry, HEAD_DIM), gt_ref, g, 2, tq)
    o_ref[0] = jnp.concatenate(heads, axis=0).T.astype(o_ref.dtype)


def _win_attention(q, k, vt, gates_t):
    b, h, s, d = q.shape
    g = k.shape[1]
    tv = vt.shape[-1]
    tq = min(256, s)
    span = WINDOW + tq
    assert tq & (tq - 1) == 0 and s >= span and tq % tv == 0 and WINDOW % tv == 0 and g == 2
    return pl.pallas_call(
        functools.partial(_win_kernel, tq=tq, span=span, tv=tv),
        grid=(b, s // tq),
        in_specs=[pl.BlockSpec((1, h, tq, d), lambda bi, i: (bi, 0, i, 0)),
                  pl.BlockSpec((1, g, s, d), lambda bi, i: (bi, 0, 0, 0)),
                  pl.BlockSpec((1, g) + vt.shape[2:], lambda bi, i: (bi, 0, 0, 0, 0)),
                  pl.BlockSpec((1, GATE_ROWS, tq), lambda bi, i: (bi, 0, i)),
                  pl.BlockSpec((span, NSA_GROUP * tq), lambda bi, i: (0, 0))],
        out_specs=pl.BlockSpec((1, tq, h * d), lambda bi, i: (bi, i, 0)),
        out_shape=jax.ShapeDtypeStruct((b, s, h * d), _MXU),
        scratch_shapes=[pltpu.VMEM((span, NSA_GROUP * tq), _F32) for _ in range(g)],
        compiler_params=_params("parallel", "arbitrary"),
        name="nsa_window",
    )(q, k, vt, gates_t, _key_minus_query(span, tq))


def _out_kernel(x_ref, mla_ref, cmp_ref, sel_ref, win_ref, sb_ref, w_ref, o_ref):
    def w_rows(first_head, n_heads):
        return w_ref[first_head * HEAD_DIM:(first_head + n_heads) * HEAD_DIM, :]

    acc = x_ref[0] + jnp.dot(mla_ref[0], w_rows(0, MLA_HEADS), preferred_element_type=_F32)
    nsa = cmp_ref[0].astype(_F32) + sel_ref[0].astype(_F32) + win_ref[0].astype(_F32)
    acc = acc + _dot(nsa, w_rows(MLA_HEADS, NSA_HEADS))
    o_ref[0] = acc + jnp.dot(sb_ref[0], w_rows(MLA_HEADS + NSA_HEADS, SB_HEADS), preferred_element_type=_F32)


def _out_proj(x, o_mla, o_cmp, o_sel, o_win, o_sb, layer, w_heads):
    b, s, _ = x.shape
    ts = min(512, s)

    def rows(a):
        return pl.BlockSpec((1, ts, a.shape[2]), lambda bi, i: (bi, i, 0))

    xspec = pl.BlockSpec((1, ts, D_MODEL), lambda bi, i: (bi, i, 0))
    return pl.pallas_call(
        _out_kernel,
        grid=(b, s // ts),
        in_specs=[xspec, rows(o_mla), rows(o_cmp), rows(o_sel), rows(o_win), rows(o_sb),
                  _layer_spec(w_heads, layer)],
        out_specs=xspec,
        out_shape=jax.ShapeDtypeStruct(x.shape, _F32),
        compiler_params=_params("parallel", "parallel"),
        name="out_proj",
    )(x, o_mla, o_cmp, o_sel, o_win, o_sb, w_heads)


def _gather_cols(w, idx):
    idx = np.asarray(idx)
    cuts = [0] + [i for i in range(1, len(idx)) if idx[i] != idx[i - 1] + (idx[i - 1] >= 0)] + [len(idx)]
    pieces = []
    for a, b in zip(cuts[:-1], cuts[1:]):
        if idx[a] < 0:
            pieces.append(jnp.zeros(w.shape[:-1] + (b - a,), _MXU))
        else:
            pieces.append(w[..., int(idx[a]):int(idx[a]) + b - a].astype(_MXU))
    return jnp.concatenate(pieces, axis=-1)


def _swap_halves(rot):
    return (np.arange(rot) + rot // 2) % rot


def _w_in_index():
    idx = np.full((_N_HEAD_COLS * HEAD_DIM,), -1, np.int64)

    def put(col, src):
        src = np.asarray(src)
        idx[col:col + len(src)] = src

    def put_head(pos, src):
        put(pos * HEAD_DIM, src)

    put(_S_CQ * LANES, _O_CQ + np.arange(MLA_Q_LORA))
    put(_S_CKV * LANES, _O_CKV + np.arange(MLA_KV_LORA))
    put(_S_KR * LANES + MLA_NOPE, _O_KR + np.arange(MLA_ROPE))
    put(_S_KRS * LANES + MLA_NOPE, _O_KR + _swap_halves(MLA_ROPE))
    for h in range(NSA_HEADS):
        put_head(_H_NQ + h, _O_NQ + h * HEAD_DIM + np.arange(HEAD_DIM))
        put_head(_H_NQS + h, _O_NQ + h * HEAD_DIM + _swap_halves(PARTIAL_ROT))
    for hk, hks, ok in ((_H_KC, _H_KCS, _O_NKC), (_H_KS, _H_KSS, _O_NKS), (_H_KW, _H_KWS, _O_NKW)):
        for g in range(NSA_KV_HEADS):
            put_head(hk + g, ok + g * HEAD_DIM + np.arange(HEAD_DIM))
            put_head(hks + g, ok + g * HEAD_DIM + _swap_halves(PARTIAL_ROT))
    for g in range(NSA_KV_HEADS):
        put_head(_H_VC + g, _O_NVC + g * HEAD_DIM + np.arange(HEAD_DIM))
    for h in range(SB_HEADS):
        put_head(_H_SBQ + h, _O_SBQ + h * HEAD_DIM + np.arange(HEAD_DIM))
        put_head(_H_SBK + h, _O_SBK + h * HEAD_DIM + np.arange(HEAD_DIM))
    return idx


def _mla_up_index():
    qd = MLA_NOPE + MLA_ROPE
    kd = MLA_NOPE + MLA_V
    uq = np.full((MLA_HEADS * LANES,), -1, np.int64)
    uqs = uq.copy()
    uk = uq.copy()
    for h in range(MLA_HEADS):
        uq[h * LANES:h * LANES + qd] = h * qd + np.arange(qd)
        uqs[h * LANES + MLA_NOPE:h * LANES + qd] = h * qd + MLA_NOPE + _swap_halves(MLA_ROPE)
        uk[h * LANES:h * LANES + MLA_NOPE] = h * kd + np.arange(MLA_NOPE)
    return uq, uqs, uk


def _transposed_weights(w_in, gate_bias):
    width = NSA_KV_HEADS * HEAD_DIM
    gate_rows = jnp.pad(w_in[..., _O_GATE:_O_GATE + N_GATES], ((0, 0), (0, 0), (0, _T_SBV - _T_GATE - N_GATES)))
    rows = jnp.concatenate([w_in[..., _O_NVS:_O_NVS + width], w_in[..., _O_NVW:_O_NVW + width], gate_rows,
                            w_in[..., _O_SBV:_O_SBV + SB_HEADS * HEAD_DIM]], axis=-1)
    bias = jnp.pad(gate_bias, ((0, 0), (0, GATE_ROWS - N_GATES)))[..., None]
    return jnp.swapaxes(rows, -1, -2).astype(_MXU), bias


def _rope_tables(s):
    pos = jnp.arange(s, dtype=_F32)

    def cs(rot):
        half = rot // 2
        inv_freq = ROPE_THETA ** (-jnp.arange(half, dtype=_F32) / half)
        ang = pos[:, None] * inv_freq[None, :]
        c, sn = jnp.cos(ang), jnp.sin(ang)
        return jnp.concatenate([c, c], axis=1), jnp.concatenate([-sn, sn], axis=1)

    c, sn = cs(MLA_ROPE)
    ones = jnp.ones((s, MLA_NOPE), _F32)
    zeros = jnp.zeros((s, MLA_NOPE), _F32)
    pad = jnp.zeros((s, LANES - MLA_NOPE - MLA_ROPE), _F32)
    ck = jnp.concatenate([ones, c, pad], axis=1)
    sk = jnp.concatenate([zeros, sn, pad], axis=1)
    q_scale = (MLA_NOPE + MLA_ROPE) ** -0.5 * LOG2_E
    c, sn = cs(PARTIAL_ROT)
    c64 = jnp.concatenate([c, jnp.ones((s, HEAD_DIM - PARTIAL_ROT), _F32)], axis=1)
    s64 = jnp.concatenate([sn, jnp.zeros((s, HEAD_DIM - PARTIAL_ROT), _F32)], axis=1)
    ns = s // SEL_LEN
    onehot = (np.arange(s)[:, None] // SEL_LEN == np.arange(ns)[None, :]) * -NEG_INF
    return ck * q_scale, sk * q_scale, ck, sk, c64, s64, jnp.asarray(onehot, _MXU)


def kernel(x, ffn1_norm, ffn1_w_gate, ffn1_w_up, ffn1_w_down, mix_norm, w_in, mla_q_norm, mla_w_uq, mla_kv_norm,
           mla_w_ukv, nsa_gate_bias, nsa_cmp_pos_k, nsa_cmp_w1_k, nsa_cmp_w2_k, nsa_cmp_pos_v, nsa_cmp_w1_v,
           nsa_cmp_w2_v, w_out, ffn2_norm, ffn2_w_gate, ffn2_w_up, ffn2_w_down, final_norm):
    b, s, d = x.shape
    depth = w_in.shape[0]
    tabs = _rope_tables(s)
    in_idx = _w_in_index()
    uq_idx, uqs_idx, uk_idx = _mla_up_index()
    half = CMP_LEN * HEAD_DIM // 2
    fg = final_norm.reshape(1, d)

    def row(p):
        return p[:, None, :]

    def cmp_weights(w1, w2, pos, transpose_out):
        pos = jnp.broadcast_to(pos.reshape(depth, 2, 1, half), (depth, 2, 8, half)).astype(_MXU)
        w2 = jnp.swapaxes(w2, -1, -2) if transpose_out else w2
        return w1.reshape(depth, 2, half, CMP_HIDDEN).astype(_MXU), w2.astype(_MXU), pos

    ffn1 = [w.astype(_MXU) for w in (ffn1_w_gate, ffn1_w_up, ffn1_w_down)]
    ffn2 = [w.astype(_MXU) for w in (ffn2_w_gate, ffn2_w_up, ffn2_w_down)]
    w_t, gate_bias = _transposed_weights(w_in, nsa_gate_bias)
    wuv_t = mla_w_ukv.reshape(depth, MLA_KV_LORA, MLA_HEADS, 2, MLA_V)[:, :, :, 1].transpose(0, 2, 3, 1).astype(_MXU)
    proj_params = (row(mix_norm), _gather_cols(w_in, in_idx), w_t,
                   row(mla_q_norm), _gather_cols(mla_w_uq, uq_idx), _gather_cols(mla_w_uq, uqs_idx),
                   row(mla_kv_norm), _gather_cols(mla_w_ukv, uk_idx), wuv_t, gate_bias)
    cmp_params = (cmp_weights(nsa_cmp_w1_k, nsa_cmp_w2_k, nsa_cmp_pos_k, False)
                  + cmp_weights(nsa_cmp_w1_v, nsa_cmp_w2_v, nsa_cmp_pos_v, True))
    w_out = w_out.astype(_MXU)
    ffn1_norm, ffn2_norm = row(ffn1_norm), row(ffn2_norm)

    for l in range(depth):
        x2d = _ffn(x.reshape(b * s, d), ffn1_norm, *ffn1, l, fg, False)
        x = x2d.reshape(b, s, d)
        (mq, mk, mvt, nq, nkc, nvc, nks, nvst, nkw, nvwt, gates_t, sbq, sbk, sbv) = _proj(x, l, *proj_params, tabs)
        o_mla = _mla_attention(mq, mk, mvt)
        kc, vct = _compress(nkc, nvc, l, *cmp_params)
        o_cmp, q_sel = _cmp_select(nq, kc, vct, gates_t)
        o_sel = _sel_attention(q_sel, nks, nvst, gates_t)
        o_win = _win_attention(nq, nkw, nvwt, gates_t)
        o_sb = _sb_attention(sbq, sbk, sbv)
        x = _out_proj(x, o_mla, o_cmp, o_sel, o_win, o_sb, l, w_out)
        x2d = _ffn(x.reshape(b * s, d), ffn2_norm, *ffn2, l, fg, l == depth - 1)
        x = x2d.reshape(b, s, d)
    return x
```

```python
import functools
import math

import numpy as np
import jax
import jax.numpy as jnp
from jax import lax
from jax.experimental import pallas as pl
from jax.experimental.pallas import tpu as pltpu

D_MODEL = 1024
HEAD_DIM = 64
MLA_HEADS = 6
MLA_NOPE = 64
MLA_ROPE = 32
MLA_V = 64
MLA_Q_LORA = 256
MLA_KV_LORA = 128
NSA_HEADS = 6
NSA_KV_HEADS = 2
NSA_GROUP = NSA_HEADS // NSA_KV_HEADS
NSA_BRANCHES = 3
CMP_LEN = 32
CMP_STRIDE = 16
CMP_HIDDEN = 128
SEL_LEN = 64
SEL_TOPK = 16
WINDOW = 512
SB_HEADS = 4
D_FF = 2816
ROPE_THETA = 500000.0
PARTIAL_ROT = HEAD_DIM // 4
EPS = 1e-6
NEG_INF = -1e30
M_FLOOR = 0.1 * NEG_INF
FORCE_SCORE = 1e4
PICKED = -3e38
F32_EXP2_ZERO = -151.0
LOG2_E = math.log2(math.e)
N_GATES = NSA_HEADS * NSA_BRANCHES

LANES = 128
FFN_CHUNK = 256
SWEEP_UNROLL = 4
SB_KEY_BLOCK = 256
TOKEN_CHUNK = 256
ONES_PAD = 16
VMEM_LIMIT = 56 * 1024 * 1024

_MXU = jnp.bfloat16
_F32 = jnp.float32

_IN_WIDTHS = (MLA_Q_LORA, MLA_KV_LORA, MLA_ROPE, NSA_HEADS * HEAD_DIM) + (NSA_KV_HEADS * HEAD_DIM,) * 6 + (
    N_GATES, SB_HEADS * HEAD_DIM, SB_HEADS * HEAD_DIM, SB_HEADS * HEAD_DIM)
_IN_OFF = np.concatenate([[0], np.cumsum(_IN_WIDTHS)])
(_O_CQ, _O_CKV, _O_KR, _O_NQ, _O_NKC, _O_NVC, _O_NKS, _O_NVS, _O_NKW, _O_NVW, _O_GATE, _O_SBQ, _O_SBK,
 _O_SBV) = [int(v) for v in _IN_OFF[:-1]]

_S_CQ, _S_CKV, _S_KR, _S_KRS = 0, 2, 3, 4
_H_NQ, _H_NQS = 10, 16
_H_KC, _H_KCS, _H_VC = 22, 24, 26
_H_KS, _H_KSS = 28, 30
_H_KW, _H_KWS = 32, 34
_H_SBQ, _H_SBK = 36, 40
_N_HEAD_COLS = 44
_T_VS, _T_VW, _T_GATE = 0, NSA_KV_HEADS * HEAD_DIM, 2 * NSA_KV_HEADS * HEAD_DIM
GATE_ROWS = 24
_T_SBV = _T_GATE + 32
_T_ROWS = _T_SBV + SB_HEADS * HEAD_DIM


def _dot(a, b):
    return jnp.dot(a.astype(_MXU), b.astype(_MXU), preferred_element_type=_F32)


def _dot_nt(a, b):
    return lax.dot_general(a.astype(_MXU), b.astype(_MXU), (((1,), (1,)), ((), ())),
                           preferred_element_type=_F32)


def _dot_split_rhs(a, b):
    hi = b.astype(_MXU)
    lo = (b - hi.astype(_F32)).astype(_MXU)
    return (jnp.dot(a, hi, preferred_element_type=_F32) + jnp.dot(a, lo, preferred_element_type=_F32))


def _rms(x, g):
    return x * lax.rsqrt(jnp.mean(x * x, axis=-1, keepdims=True) + EPS) * g


def _params(*sem):
    return pltpu.CompilerParams(dimension_semantics=sem, vmem_limit_bytes=VMEM_LIMIT)


def _layer_spec(a, layer):
    return pl.BlockSpec((None,) + a.shape[1:], lambda *_: (layer,) + (0,) * (a.ndim - 1))


def _ffn_kernel(x_ref, g_ref, wg_ref, wu_ref, wd_ref, fg_ref, o_ref, h_ref, acc_ref, act_ref, *, final_norm):
    j = pl.program_id(1)

    @pl.when(j == 0)
    def _():
        h_ref[...] = _rms(x_ref[...], g_ref[...]).astype(h_ref.dtype)
        acc_ref[...] = jnp.zeros_like(acc_ref)

    h = h_ref[...]
    tf = act_ref.shape[1]
    for c0 in range(0, tf, FFN_CHUNK):
        c1 = min(c0 + FFN_CHUNK, tf)
        gate = jnp.dot(h, wg_ref[:, c0:c1], preferred_element_type=_F32)
        up = jnp.dot(h, wu_ref[:, c0:c1], preferred_element_type=_F32)
        act_ref[:, c0:c1] = (gate * jax.nn.sigmoid(gate) * up).astype(act_ref.dtype)
    acc_ref[...] += jnp.dot(act_ref[...], wd_ref[...], preferred_element_type=_F32)

    @pl.when(j == pl.num_programs(1) - 1)
    def _():
        y = x_ref[...] + 0.5 * acc_ref[...]
        if final_norm:
            y = _rms(y, fg_ref[...])
        o_ref[...] = y


def _ffn(x2d, g, wg, wu, wd, layer, fg, final_norm):
    rows = x2d.shape[0]
    tm = min(1024, rows)
    tf = D_FF // 2
    grid = (rows // tm, D_FF // tf)
    return pl.pallas_call(
        functools.partial(_ffn_kernel, final_norm=final_norm),
        grid=grid,
        in_specs=[
            pl.BlockSpec((tm, D_MODEL), lambda i, j: (i, 0)),
            _layer_spec(g, layer),
            pl.BlockSpec((None, D_MODEL, tf), lambda i, j: (layer, 0, j)),
            pl.BlockSpec((None, D_MODEL, tf), lambda i, j: (layer, 0, j)),
            pl.BlockSpec((None, tf, D_MODEL), lambda i, j: (layer, j, 0)),
            pl.BlockSpec((1, D_MODEL), lambda i, j: (0, 0)),
        ],
        out_specs=pl.BlockSpec((tm, D_MODEL), lambda i, j: (i, 0)),
        out_shape=jax.ShapeDtypeStruct((rows, D_MODEL), _F32),
        scratch_shapes=[pltpu.VMEM((tm, D_MODEL), _MXU), pltpu.VMEM((tm, D_MODEL), _F32), pltpu.VMEM((tm, tf), _MXU)],
        compiler_params=_params("parallel", "arbitrary"),
        name="ffn",
    )(x2d, g, wg, wu, wd, fg)


def _proj_kernel(x_ref, g_ref, w_ref, wt_ref, qn_ref, wuq_ref, wuqs_ref, kvn_ref, wuk_ref, wuv_ref, gb_ref,
                 cq_ref, sq_ref, ck_ref, sk_ref, c64_ref, s64_ref, oh_ref,
                 mq_ref, mk_ref, mv_ref, nq_ref, nkc_ref, nvc_ref, nks_ref, nvs_ref, nkw_ref, nvw_ref,
                 gate_ref, sbq_ref, sbk_ref, sbv_ref, stage_ref):
    hn = _rms(x_ref[0], g_ref[...]).astype(_MXU)

    def proj(h0, h1):
        return jnp.dot(hn, w_ref[:, h0 * HEAD_DIM:h1 * HEAD_DIM], preferred_element_type=_F32)

    def slot(p, s):
        return p[:, s * LANES:(s + 1) * LANES]

    def head(p, i):
        return p[:, i * HEAD_DIM:(i + 1) * HEAD_DIM]

    p = proj(0, _H_NQ)
    cq = _rms(p[:, :MLA_Q_LORA], qn_ref[...])
    ckv = _rms(slot(p, _S_CKV), kvn_ref[...])
    q = _dot(cq, wuq_ref[...])
    q_partner = _dot(cq, wuqs_ref[...])
    kpe = slot(p, _S_KR) * ck_ref[...] + slot(p, _S_KRS) * sk_ref[...]
    kn = _dot(ckv, wuk_ref[...])
    for h in range(MLA_HEADS):
        mq_ref[0, h] = (slot(q, h) * cq_ref[...] + slot(q_partner, h) * sq_ref[...]).astype(mq_ref.dtype)
        mk_ref[0, h] = (slot(kn, h) + kpe).astype(mk_ref.dtype)
        mv_ref[0, h, 0] = _ones_row_pad(_dot_nt(wuv_ref[h], ckv)).astype(mv_ref.dtype)

    c64 = c64_ref[...]
    s64 = s64_ref[...]
    scale = HEAD_DIM ** -0.5

    p = proj(_H_NQ, _H_KC)
    for h in range(NSA_HEADS):
        nq_ref[0, h] = ((head(p, h) * c64 + head(p, NSA_HEADS + h) * s64) * (scale * LOG2_E)).astype(nq_ref.dtype)

    p = proj(_H_KC, _H_SBQ)
    base = _H_KC
    ns = oh_ref.shape[-1]

    def roped(hk, hks, g):
        return head(p, hk - base + g) * c64 + head(p, hks - base + g) * s64

    def write_chunked(o_ref, g, val):
        stage_ref[...] = val
        for t in range(CMP_STRIDE):
            piece = stage_ref[pl.ds(t, val.shape[0] // CMP_STRIDE, stride=CMP_STRIDE), :]
            o_ref[0, g, :, t * HEAD_DIM:(t + 1) * HEAD_DIM] = piece.astype(o_ref.dtype)

    for g in range(NSA_KV_HEADS):
        write_chunked(nkc_ref, g, roped(_H_KC, _H_KCS, g))
        write_chunked(nvc_ref, g, head(p, _H_VC - base + g))
        nks_ref[0, g, :, 0:ns] = oh_ref[...]
        nks_ref[0, g, :, ns:ns + HEAD_DIM] = roped(_H_KS, _H_KSS, g).astype(nks_ref.dtype)
        nkw_ref[0, g] = roped(_H_KW, _H_KWS, g).astype(nkw_ref.dtype)

    pt = _dot_nt(wt_ref[...], hn)
    for g in range(NSA_KV_HEADS):
        lo = g * HEAD_DIM
        nvs_ref[0, g, 0] = _ones_row_pad(pt[_T_VS + lo:_T_VS + lo + HEAD_DIM]).astype(nvs_ref.dtype)
        nvw_ref[0, g, 0] = _ones_row_pad(pt[_T_VW + lo:_T_VW + lo + HEAD_DIM]).astype(nvw_ref.dtype)
    gate_ref[0] = jax.nn.sigmoid(pt[_T_GATE:_T_GATE + GATE_ROWS] + gb_ref[...])

    p = proj(_H_SBQ, _N_HEAD_COLS)
    for h in range(SB_HEADS):
        sbq_ref[0, h] = (head(p, h) * (scale * LOG2_E)).astype(sbq_ref.dtype)
        sbk_ref[0, h] = head(p, SB_HEADS + h).astype(sbk_ref.dtype)
        for c in range(sbv_ref.shape[2]):
            sbv_ref[0, h, c] = pt[_T_SBV + h * HEAD_DIM:_T_SBV + (h + 1) * HEAD_DIM,
                                  c * SB_KEY_BLOCK:(c + 1) * SB_KEY_BLOCK].astype(sbv_ref.dtype)


def _proj(x, layer, g, w_ext, w_t, qn, wuq, wuqs, kvn, wuk, wuv, gb, tabs):
    b, s, _ = x.shape
    ts = min(TOKEN_CHUNK, s)
    cq, sq, ck, sk, c64, s64, onehot = tabs
    full = functools.partial(_layer_spec, layer=layer)

    def tab(a):
        return pl.BlockSpec((ts, a.shape[1]), lambda bi, i: (i, 0))

    def heads(n, d):
        return (pl.BlockSpec((1, n, ts, d), lambda bi, i: (bi, 0, i, 0)),
                jax.ShapeDtypeStruct((b, n, s, d), _MXU))

    def values_t(n):
        return (pl.BlockSpec((1, n, 1, HEAD_DIM + ONES_PAD, ts), lambda bi, i: (bi, 0, i, 0, 0)),
                jax.ShapeDtypeStruct((b, n, s // ts, HEAD_DIM + ONES_PAD, ts), _MXU))

    def chunked():
        return (pl.BlockSpec((1, NSA_KV_HEADS, ts // CMP_STRIDE, CMP_STRIDE * HEAD_DIM), lambda bi, i: (bi, 0, i, 0)),
                jax.ShapeDtypeStruct((b, NSA_KV_HEADS, s // CMP_STRIDE, CMP_STRIDE * HEAD_DIM), _MXU))

    outs = [heads(MLA_HEADS, LANES), heads(MLA_HEADS, LANES), values_t(MLA_HEADS), heads(NSA_HEADS, HEAD_DIM),
            chunked(), chunked(), heads(NSA_KV_HEADS, onehot.shape[1] + HEAD_DIM), values_t(NSA_KV_HEADS),
            heads(NSA_KV_HEADS, HEAD_DIM), values_t(NSA_KV_HEADS),
            (pl.BlockSpec((1, GATE_ROWS, ts), lambda bi, i: (bi, 0, i)), jax.ShapeDtypeStruct((b, GATE_ROWS, s), _F32)),
            heads(SB_HEADS, HEAD_DIM), heads(SB_HEADS, HEAD_DIM),
            (pl.BlockSpec((1, SB_HEADS, ts // SB_KEY_BLOCK, HEAD_DIM, SB_KEY_BLOCK), lambda bi, i: (bi, 0, i, 0, 0)),
             jax.ShapeDtypeStruct((b, SB_HEADS, s // SB_KEY_BLOCK, HEAD_DIM, SB_KEY_BLOCK), _MXU))]
    return pl.pallas_call(
        _proj_kernel,
        grid=(b, s // ts),
        in_specs=[pl.BlockSpec((1, ts, D_MODEL), lambda bi, i: (bi, i, 0)), full(g), full(w_ext), full(w_t), full(qn),
                  full(wuq), full(wuqs), full(kvn), full(wuk), full(wuv), full(gb),
                  tab(cq), tab(sq), tab(ck), tab(sk), tab(c64), tab(s64), tab(onehot)],
        out_specs=[o[0] for o in outs],
        out_shape=[o[1] for o in outs],
        scratch_shapes=[pltpu.VMEM((ts, HEAD_DIM), _F32)],
        compiler_params=_params("parallel", "parallel"),
        name="proj",
    )(x, g, w_ext, w_t, qn, wuq, wuqs, kvn, wuk, wuv, gb, cq, sq, ck, sk, c64, s64, onehot)


def _ones_row_pad(vt):
    first = lax.broadcasted_iota(jnp.int32, (ONES_PAD, vt.shape[1]), 0) == 0
    return jnp.concatenate([vt, jnp.where(first, 1.0, 0.0).astype(vt.dtype)], axis=0)


def _softmax_step_t(carry, st, vt_chunks):
    m, acc = carry
    m_new = jnp.maximum(m, jnp.max(st, axis=0, keepdims=True))
    alpha = jnp.exp2(m - m_new)
    pt = jnp.exp2(st - m_new).astype(_MXU)
    n = st.shape[0] // len(vt_chunks)
    pv = sum(jnp.dot(vt, pt[c * n:(c + 1) * n], preferred_element_type=_F32) for c, vt in enumerate(vt_chunks))
    return m_new, alpha * acc + pv


def _softmax_init_t(d, cols):
    return (jnp.full((1, cols), M_FLOOR, _F32), jnp.zeros((d + ONES_PAD, cols), _F32))


def _softmax_finish_t(carry, d):
    _, acc = carry
    return acc[:d] * (1.0 / acc[d:d + 1])


def _two_chain_sweep(n_full, qk, soft, init):
    def body(j, carry, diag=False):
        c0, c1 = carry
        qk(0, j)
        c1 = soft(1, j, c1, diag)
        qk(1, jnp.zeros_like(j) if diag else j + 1)
        c0 = soft(0, j, c0, diag)
        return c0, c1

    def unrolled(i, carry):
        for u in range(SWEEP_UNROLL):
            carry = body(SWEEP_UNROLL * i + u, carry)
        return carry

    qk(1, n_full)
    carry = body(n_full, init, True)
    trips = n_full // SWEEP_UNROLL
    carry = lax.fori_loop(0, trips, unrolled, carry)
    return lax.fori_loop(SWEEP_UNROLL * trips, n_full, body, carry)


def _mla_kernel(q_ref, k_ref, vt_ref, o_ref, s0_ref, s1_ref, *, t, nsub):
    qi = pl.program_id(2)
    s_refs = (s0_ref, s1_ref)

    def qk(hh, j):
        off = pl.multiple_of(j * t, t)
        s_refs[hh][...] = _dot_nt(k_ref[0, hh, pl.ds(off, t), :], q_ref[0, hh])

    def soft(hh, j, carry, diag):
        st = s_refs[hh][...]
        if diag:
            key = lax.broadcasted_iota(jnp.int32, (t, t), 0)
            qry = lax.broadcasted_iota(jnp.int32, (t, t), 1)
            st = jnp.where(key <= qry, st, NEG_INF)
        return _softmax_step_t(carry, st, [vt_ref[0, hh, j * nsub + c] for c in range(nsub)])

    carry = _two_chain_sweep(qi, qk, soft, tuple(_softmax_init_t(MLA_V, t) for _ in range(2)))
    ot = jnp.concatenate([_softmax_finish_t(c, MLA_V) for c in carry], axis=0)
    o_ref[0] = ot.T.astype(o_ref.dtype)


def _mla_attention(q, k, vt):
    b, h, s, _ = q.shape
    tv = vt.shape[-1]
    dv = vt.shape[-2]
    t = min(512, s)
    assert h % 2 == 0 and 2 * MLA_V == LANES and t % tv == 0 and s % t == 0
    return pl.pallas_call(
        functools.partial(_mla_kernel, t=t, nsub=t // tv),
        grid=(b, h // 2, s // t),
        in_specs=[pl.BlockSpec((1, 2, t, LANES), lambda bi, hi, i: (bi, hi, i, 0)),
                  pl.BlockSpec((1, 2, s, LANES), lambda bi, hi, i: (bi, hi, 0, 0)),
                  pl.BlockSpec((1, 2, s // tv, dv, tv), lambda bi, hi, i: (bi, hi, 0, 0, 0))],
        out_specs=pl.BlockSpec((1, t, LANES), lambda bi, hi, i: (bi, i, hi)),
        out_shape=jax.ShapeDtypeStruct((b, s, h * MLA_V), _MXU),
        scratch_shapes=[pltpu.VMEM((t, t), _F32), pltpu.VMEM((t, t), _F32)],
        compiler_params=_params("parallel", "parallel", "arbitrary"),
        name="mla_attn",
    )(q, k, vt)


def _sb_kernel(q_ref, k_ref, vt_ref, u_ref, o_ref, *scratch_refs, tq, tk):
    qi = pl.program_id(1)
    u = u_ref[...]
    n_heads = q_ref.shape[1]
    scratch = [scratch_refs[6 * hh:6 * (hh + 1)] for hh in range(n_heads)]
    per_tile = tq // tk

    def step(j, carry, key_offset=None):
        diag = key_offset is not None
        off = pl.multiple_of(j * tk, tk)
        if diag:
            key = key_offset + lax.broadcasted_iota(jnp.int32, (tk, tq), 0)
            qry = lax.broadcasted_iota(jnp.int32, (tk, tq), 1)
            strict = key < qry

        def logits(hh):
            z_ref, _, _, _, _, _ = scratch[hh]
            z_ref[...] = _dot_nt(k_ref[0, hh, pl.ds(off, tk), :], q_ref[0, hh])

        def log_terms(hh):
            z_ref, lb_ref, hi_ref, lo_ref, _, _ = scratch[hh]
            z = z_ref[...]
            log_beta = jnp.minimum(z, 0.0) - jnp.log2(1.0 + jnp.exp2(-jnp.abs(z)))
            log_rem = log_beta - z
            if diag:
                log_rem = jnp.where(strict, log_rem, 0.0)
            hi = log_rem.astype(_MXU)
            lb_ref[...] = log_beta
            hi_ref[...] = hi
            lo_ref[...] = (log_rem - hi.astype(_F32)).astype(_MXU)
            return log_rem[0:1, :]

        def suffix_sums(hh):
            _, _, hi_ref, lo_ref, sfx_ref, _ = scratch[hh]
            sfx_ref[...] = (jnp.dot(u, hi_ref[...], preferred_element_type=_F32)
                            + jnp.dot(u, lo_ref[...], preferred_element_type=_F32))

        def weights(hh, first_rem):
            _, lb_ref, _, _, sfx_ref, a_ref = scratch[hh]
            rem = carry[hh][0]
            suffix = sfx_ref[...]
            a = jnp.exp2(lb_ref[...] + suffix + rem)
            if diag:
                a = jnp.where(strict, a, 0.0)
            a_ref[...] = a.astype(_MXU)
            return rem + suffix[0:1, :] + first_rem

        def values(hh):
            a_ref = scratch[hh][5]
            return carry[hh][1] + jnp.dot(vt_ref[0, hh, j], a_ref[...], preferred_element_type=_F32)

        heads = range(n_heads)
        for hh in heads:
            logits(hh)
        first = []
        for hh in heads:
            first.append(log_terms(hh))
            suffix_sums(hh)
        rems = [weights(hh, first[hh]) for hh in heads]
        alive = jnp.max(functools.reduce(jnp.maximum, rems)) > F32_EXP2_ZERO
        return alive, tuple((rems[hh], values(hh)) for hh in heads)

    carry = tuple((jnp.zeros((1, tq), _F32), jnp.zeros((HEAD_DIM, tq), _F32)) for _ in range(n_heads))
    first = qi * per_tile
    for i in reversed(range(per_tile)):
        _, carry = step(first + i, carry, key_offset=i * tk)
    has_past = first > 0
    carry = tuple((jnp.where(has_past, rem, NEG_INF), acc) for rem, acc in carry)
    alive, carry = step(jnp.maximum(first - 1, 0), carry)

    def earlier(c):
        return (c[0] - 1,) + step(c[0], c[2])

    _, _, carry = lax.while_loop(lambda c: jnp.logical_and(c[0] >= 0, c[1]), earlier, (first - 2, alive, carry))
    o_ref[0] = jnp.concatenate([acc for _, acc in carry], axis=0).T.astype(o_ref.dtype)


def _sb_attention(q, k, vt):
    b, h, s, d = q.shape
    tk = vt.shape[-1]
    tq = tk
    assert (h * d) % LANES == 0 and s % tq == 0 and tq % tk == 0
    idx = np.arange(tk)
    u = jnp.asarray(idx[None, :] > idx[:, None], _MXU)
    return pl.pallas_call(
        functools.partial(_sb_kernel, tq=tq, tk=tk),
        grid=(b, s // tq),
        in_specs=[pl.BlockSpec((1, h, tq, d), lambda bi, i: (bi, 0, i, 0)),
                  pl.BlockSpec((1, h, s, d), lambda bi, i: (bi, 0, 0, 0)),
                  pl.BlockSpec((1, h, s // tk, d, tk), lambda bi, i: (bi, 0, 0, 0, 0)),
                  pl.BlockSpec((tk, tk), lambda bi, i: (0, 0))],
        out_specs=pl.BlockSpec((1, tq, h * d), lambda bi, i: (bi, i, 0)),
        out_shape=jax.ShapeDtypeStruct((b, s, h * d), _MXU),
        scratch_shapes=[pltpu.VMEM((tk, tq), dt) for _ in range(h) for dt in (_F32, _F32, _MXU, _MXU, _F32, _MXU)],
        compiler_params=_params("parallel", "arbitrary"),
        name="sb_attn",
    )(q, k, vt, u)


def _compress_kernel(xk_ref, xv_ref, w1k_ref, w2k_ref, pk_ref, w1v_ref, w2v_ref, pv_ref, ok_ref, ov_ref):
    def hidden(x_ref, w1_ref, p_ref):
        x = x_ref[0, 0]
        n = x.shape[0]
        first = jnp.dot(x, w1_ref[0], preferred_element_type=_F32)
        second = jnp.dot(x, w1_ref[1], preferred_element_type=_F32)
        pos = _dot(p_ref[0], w1_ref[0]) + _dot(p_ref[1], w1_ref[1])
        hid = first + pltpu.roll(second, n - 1, 0) + pos[0:1]
        return 0.5 * hid * (1.0 + jnp.tanh(math.sqrt(2.0 / math.pi) * (hid + 0.044715 * hid * hid * hid)))

    ok_ref[0, 0] = _dot(hidden(xk_ref, w1k_ref, pk_ref), w2k_ref[...]).astype(ok_ref.dtype)
    ov_ref[0, 0] = _dot_nt(w2v_ref[...], hidden(xv_ref, w1v_ref, pv_ref)).astype(ov_ref.dtype)


def _compress(xk, xv, layer, w1k, w2k, pk, w1v, w2v, pv):
    b, g, n, _ = xk.shape
    d = HEAD_DIM
    full = functools.partial(_layer_spec, layer=layer)

    xspec = pl.BlockSpec((1, 1, n, CMP_STRIDE * d), lambda bi, gi: (bi, gi, 0, 0))
    return pl.pallas_call(
        _compress_kernel,
        grid=(b, g),
        in_specs=[xspec, xspec, full(w1k), full(w2k), full(pk), full(w1v), full(w2v), full(pv)],
        out_specs=[pl.BlockSpec((1, 1, n, d), lambda bi, gi: (bi, gi, 0, 0)),
                   pl.BlockSpec((1, 1, d, n), lambda bi, gi: (bi, gi, 0, 0))],
        out_shape=[jax.ShapeDtypeStruct((b, g, n, d), _MXU), jax.ShapeDtypeStruct((b, g, d, n), _MXU)],
        compiler_params=_params("parallel", "parallel"),
        name="nsa_compress",
    )(xk, xv, w1k, w2k, pk, w1v, w2v, pv)


def _group_queries(q_ref, g, tq):
    return q_ref[0, g * NSA_GROUP:(g + 1) * NSA_GROUP].reshape(NSA_GROUP * tq, q_ref.shape[-1])


def _gated_heads(ot, gt_ref, g, branch, tq):
    out = []
    for r in range(NSA_GROUP):
        row = NSA_BRANCHES * (g * NSA_GROUP + r) + branch
        out.append(ot[:, r * tq:(r + 1) * tq] * gt_ref[0, row:row + 1, :])
    return out


def _cmp_kernel(q_ref, kc_ref, vct_ref, ov_ref, gt_ref, o_ref, qa_ref, s0_ref, s1_ref, *, tq, n_top):
    q0 = pl.program_id(1) * tq
    ncp = kc_ref.shape[2]
    ns = ov_ref.shape[0]
    lanes = NSA_GROUP * tq
    s_refs = (s0_ref, s1_ref)
    for g in range(NSA_KV_HEADS):
        s_refs[g][...] = _dot_nt(kc_ref[0, g], _group_queries(q_ref, g, tq))
    qpos = q0 + (lax.broadcasted_iota(jnp.int32, (1, lanes), 1) & (tq - 1))
    cmp_end = lax.broadcasted_iota(jnp.int32, (ncp, 1), 0) * CMP_STRIDE + (CMP_LEN - 1)
    visible = cmp_end <= qpos
    cur = jnp.right_shift(q0 + lax.broadcasted_iota(jnp.int32, (1, tq), 1), int(math.log2(SEL_LEN)))
    blk = lax.broadcasted_iota(jnp.int32, (ns, 1), 0)
    forced = (blk == 0) | (blk == cur) | (blk == cur - 1)
    future = blk > cur
    blk_f = blk.astype(_F32)
    heads = []
    scores = []
    for g in range(NSA_KV_HEADS):
        st = jnp.where(visible, s_refs[g][...], NEG_INF)
        e = jnp.exp2(st - jnp.max(st, axis=0, keepdims=True))
        inv = jnp.where(qpos >= CMP_LEN - 1, 1.0 / jnp.sum(e, axis=0, keepdims=True), 0.0)
        pt = e * inv
        heads += _gated_heads(_dot(vct_ref[0, g], pt), gt_ref, g, 0, tq)
        p_sum = sum(pt[:, r * tq:(r + 1) * tq] for r in range(NSA_GROUP))
        score = _dot_split_rhs(ov_ref[...], p_sum)
        scores.append(jnp.where(forced, FORCE_SCORE, jnp.where(future, -1.0, score)))
    o_ref[0] = jnp.concatenate(heads, axis=0).T.astype(o_ref.dtype)
    unselected = [jnp.full((ns, tq), -1.0, _F32) for _ in range(NSA_KV_HEADS)]
    for _ in range(n_top):
        for g in range(NSA_KV_HEADS):
            top = jnp.max(scores[g], axis=0, keepdims=True)
            first = jnp.min(jnp.where(scores[g] == top, blk_f, float(ns)), axis=0, keepdims=True)
            pick = blk_f == first
            unselected[g] = jnp.where(pick, 0.0, unselected[g])
            scores[g] = jnp.where(pick, PICKED, scores[g])
    for g in range(NSA_KV_HEADS):
        sel_m1 = unselected[g].T.astype(qa_ref.dtype)
        for h in range(g * NSA_GROUP, (g + 1) * NSA_GROUP):
            qa_ref[0, h, :, 0:ns] = sel_m1
            qa_ref[0, h, :, ns:ns + HEAD_DIM] = q_ref[0, h]


def _cmp_select(q, kc, vct, gates_t):
    b, h, s, d = q.shape
    g = kc.shape[1]
    ncp = kc.shape[2]
    ns = s // SEL_LEN
    n_top = min(SEL_TOPK, ns)
    tq = min(256, s)
    assert tq & (tq - 1) == 0 and g == 2
    c0 = np.arange(ncp)[:, None] * CMP_STRIDE
    n0 = np.arange(ns)[None, :] * SEL_LEN
    overlap = jnp.asarray(((c0 < n0 + SEL_LEN) & (c0 + CMP_LEN > n0)).T, _MXU)
    return pl.pallas_call(
        functools.partial(_cmp_kernel, tq=tq, n_top=n_top),
        grid=(b, s // tq),
        in_specs=[pl.BlockSpec((1, h, tq, d), lambda bi, i: (bi, 0, i, 0)),
                  pl.BlockSpec((1, g, ncp, d), lambda bi, i: (bi, 0, 0, 0)),
                  pl.BlockSpec((1, g, d, ncp), lambda bi, i: (bi, 0, 0, 0)),
                  pl.BlockSpec((ns, ncp), lambda bi, i: (0, 0)),
                  pl.BlockSpec((1, GATE_ROWS, tq), lambda bi, i: (bi, 0, i))],
        out_specs=[pl.BlockSpec((1, tq, h * d), lambda bi, i: (bi, i, 0)),
                   pl.BlockSpec((1, h, tq, ns + d), lambda bi, i: (bi, 0, i, 0))],
        out_shape=[jax.ShapeDtypeStruct((b, s, h * d), _MXU), jax.ShapeDtypeStruct((b, h, s, ns + d), _MXU)],
        scratch_shapes=[pltpu.VMEM((ncp, NSA_GROUP * tq), _F32) for _ in range(g)],
        compiler_params=_params("parallel", "arbitrary"),
        name="nsa_cmp_select",
    )(q, kc, vct, overlap, gates_t)


def _key_minus_query(keys, tq):
    return jnp.asarray(np.arange(keys)[:, None] - np.arange(NSA_GROUP * tq)[None, :] % tq, jnp.int32)


def _sel_kernel(q_ref, k_ref, vt_ref, gt_ref, rel_ref, o_ref, s0_ref, s1_ref, *, tq, tk, nsub):
    q0 = pl.program_id(1) * tq
    last = (q0 + tq - 1) // tk
    lanes = NSA_GROUP * tq
    s_refs = (s0_ref, s1_ref)

    def qk(g, j):
        off = pl.multiple_of(j * tk, tk)
        s_refs[g][...] = _dot_nt(k_ref[0, g, pl.ds(off, tk), :], _group_queries(q_ref, g, tq))

    def soft(g, j, carry, causal):
        st = s_refs[g][...]
        if causal:
            st = jnp.where(rel_ref[...] <= q0 - j * tk, st, NEG_INF)
        return _softmax_step_t(carry, st, [vt_ref[0, g, j * nsub + c] for c in range(nsub)])

    init = tuple(_softmax_init_t(HEAD_DIM, lanes) for _ in range(NSA_KV_HEADS))
    carry = _two_chain_sweep(last, qk, soft, init)
    heads = []
    for g in range(NSA_KV_HEADS):
        heads += _gated_heads(_softmax_finish_t(carry[g], HEAD_DIM), gt_ref, g, 1, tq)
    o_ref[0] = jnp.concatenate(heads, axis=0).T.astype(o_ref.dtype)


def _sel_attention(q, k, vt, gates_t):
    b, h, s, da = q.shape
    g = k.shape[1]
    d = HEAD_DIM
    tv = vt.shape[-1]
    tq = min(256, s)
    tk = min(512, s)
    assert tq & (tq - 1) == 0 and s % tk == 0 and tk % tv == 0 and g == 2
    return pl.pallas_call(
        functools.partial(_sel_kernel, tq=tq, tk=tk, nsub=tk // tv),
        grid=(b, s // tq),
        in_specs=[pl.BlockSpec((1, h, tq, da), lambda bi, i: (bi, 0, i, 0)),
                  pl.BlockSpec((1, g, s, da), lambda bi, i: (bi, 0, 0, 0)),
                  pl.BlockSpec((1, g) + vt.shape[2:], lambda bi, i: (bi, 0, 0, 0, 0)),
                  pl.BlockSpec((1, GATE_ROWS, tq), lambda bi, i: (bi, 0, i)),
                  pl.BlockSpec((tk, NSA_GROUP * tq), lambda bi, i: (0, 0))],
        out_specs=pl.BlockSpec((1, tq, h * d), lambda bi, i: (bi, i, 0)),
        out_shape=jax.ShapeDtypeStruct((b, s, h * d), _MXU),
        scratch_shapes=[pltpu.VMEM((tk, NSA_GROUP * tq), _F32) for _ in range(g)],
        compiler_params=_params("parallel", "arbitrary"),
        name="nsa_selected",
    )(q, k, vt, gates_t, _key_minus_query(tk, tq))


def _win_kernel(q_ref, k_ref, vt_ref, gt_ref, rel_ref, o_ref, s0_ref, s1_ref, *, tq, span, tv):
    q0 = pl.program_id(1) * tq
    start = pl.multiple_of(jnp.maximum(q0 - WINDOW, 0), tq)
    first_chunk = start // tv
    lanes = NSA_GROUP * tq
    s_refs = (s0_ref, s1_ref)
    for g in range(NSA_KV_HEADS):
        s_refs[g][...] = _dot_nt(k_ref[0, g, pl.ds(start, span), :], _group_queries(q_ref, g, tq))
    rel = rel_ref[...]
    offset = q0 - start
    heads = []
    for g in range(NSA_KV_HEADS):
        st = jnp.where(rel <= offset, s_refs[g][...], NEG_INF)
        st = jnp.where(rel > offset - WINDOW, st, NEG_INF)
        carry = _softmax_step_t(_softmax_init_t(HEAD_DIM, lanes), st,
                                [vt_ref[0, g, first_chunk + c] for c in range(span // tv)])
        heads += _gated_heads(_softmax_finish_t(carry, HEAD_DIM), gt_ref, g, 2, tq)
    o_ref[0] = jnp.concatenate(heads, axis=0).T.astype(o_ref.dtype)


def _win_attention(q, k, vt, gates_t):
    b, h, s, d = q.shape
    g = k.shape[1]
    tv = vt.shape[-1]
    tq = min(256, s)
    span = WINDOW + tq
    assert tq & (tq - 1) == 0 and s >= span and tq % tv == 0 and WINDOW % tv == 0 and g == 2
    return pl.pallas_call(
        functools.partial(_win_kernel, tq=tq, span=span, tv=tv),
        grid=(b, s // tq),
        in_specs=[pl.BlockSpec((1, h, tq, d), lambda bi, i: (bi, 0, i, 0)),
                  pl.BlockSpec((1, g, s, d), lambda bi, i: (bi, 0, 0, 0)),
                  pl.BlockSpec((1, g) + vt.shape[2:], lambda bi, i: (bi, 0, 0, 0, 0)),
                  pl.BlockSpec((1, GATE_ROWS, tq), lambda bi, i: (bi, 0, i)),
                  pl.BlockSpec((span, NSA_GROUP * tq), lambda bi, i: (0, 0))],
        out_specs=pl.BlockSpec((1, tq, h * d), lambda bi, i: (bi, i, 0)),
        out_shape=jax.ShapeDtypeStruct((b, s, h * d), _MXU),
        scratch_shapes=[pltpu.VMEM((span, NSA_GROUP * tq), _F32) for _ in range(g)],
        compiler_params=_params("parallel", "arbitrary"),
        name="nsa_window",
    )(q, k, vt, gates_t, _key_minus_query(span, tq))


def _out_kernel(x_ref, mla_ref, cmp_ref, sel_ref, win_ref, sb_ref, w_ref, o_ref):
    def w_rows(first_head, n_heads):
        return w_ref[first_head * HEAD_DIM:(first_head + n_heads) * HEAD_DIM, :]

    acc = x_ref[0] + jnp.dot(mla_ref[0], w_rows(0, MLA_HEADS), preferred_element_type=_F32)
    nsa = cmp_ref[0].astype(_F32) + sel_ref[0].astype(_F32) + win_ref[0].astype(_F32)
    acc = acc + _dot(nsa, w_rows(MLA_HEADS, NSA_HEADS))
    o_ref[0] = acc + jnp.dot(sb_ref[0], w_rows(MLA_HEADS + NSA_HEADS, SB_HEADS), preferred_element_type=_F32)


def _out_proj(x, o_mla, o_cmp, o_sel, o_win, o_sb, layer, w_heads):
    b, s, _ = x.shape
    ts = min(512, s)

    def rows(a):
        return pl.BlockSpec((1, ts, a.shape[2]), lambda bi, i: (bi, i, 0))

    xspec = pl.BlockSpec((1, ts, D_MODEL), lambda bi, i: (bi, i, 0))
    return pl.pallas_call(
        _out_kernel,
        grid=(b, s // ts),
        in_specs=[xspec, rows(o_mla), rows(o_cmp), rows(o_sel), rows(o_win), rows(o_sb),
                  _layer_spec(w_heads, layer)],
        out_specs=xspec,
        out_shape=jax.ShapeDtypeStruct(x.shape, _F32),
        compiler_params=_params("parallel", "parallel"),
        name="out_proj",
    )(x, o_mla, o_cmp, o_sel, o_win, o_sb, w_heads)


def _gather_cols(w, idx):
    idx = np.asarray(idx)
    cuts = [0] + [i for i in range(1, len(idx)) if idx[i] != idx[i - 1] + (idx[i - 1] >= 0)] + [len(idx)]
    pieces = []
    for a, b in zip(cuts[:-1], cuts[1:]):
        if idx[a] < 0:
            pieces.append(jnp.zeros(w.shape[:-1] + (b - a,), _MXU))
        else:
            pieces.append(w[..., int(idx[a]):int(idx[a]) + b - a].astype(_MXU))
    return jnp.concatenate(pieces, axis=-1)


def _swap_halves(rot):
    return (np.arange(rot) + rot // 2) % rot


def _w_in_index():
    idx = np.full((_N_HEAD_COLS * HEAD_DIM,), -1, np.int64)

    def put(col, src):
        src = np.asarray(src)
        idx[col:col + len(src)] = src

    def put_head(pos, src):
        put(pos * HEAD_DIM, src)

    put(_S_CQ * LANES, _O_CQ + np.arange(MLA_Q_LORA))
    put(_S_CKV * LANES, _O_CKV + np.arange(MLA_KV_LORA))
    put(_S_KR * LANES + MLA_NOPE, _O_KR + np.arange(MLA_ROPE))
    put(_S_KRS * LANES + MLA_NOPE, _O_KR + _swap_halves(MLA_ROPE))
    for h in range(NSA_HEADS):
        put_head(_H_NQ + h, _O_NQ + h * HEAD_DIM + np.arange(HEAD_DIM))
        put_head(_H_NQS + h, _O_NQ + h * HEAD_DIM + _swap_halves(PARTIAL_ROT))
    for hk, hks, ok in ((_H_KC, _H_KCS, _O_NKC), (_H_KS, _H_KSS, _O_NKS), (_H_KW, _H_KWS, _O_NKW)):
        for g in range(NSA_KV_HEADS):
            put_head(hk + g, ok + g * HEAD_DIM + np.arange(HEAD_DIM))
            put_head(hks + g, ok + g * HEAD_DIM + _swap_halves(PARTIAL_ROT))
    for g in range(NSA_KV_HEADS):
        put_head(_H_VC + g, _O_NVC + g * HEAD_DIM + np.arange(HEAD_DIM))
    for h in range(SB_HEADS):
        put_head(_H_SBQ + h, _O_SBQ + h * HEAD_DIM + np.arange(HEAD_DIM))
        put_head(_H_SBK + h, _O_SBK + h * HEAD_DIM + np.arange(HEAD_DIM))
    return idx


def _mla_up_index():
    qd = MLA_NOPE + MLA_ROPE
    kd = MLA_NOPE + MLA_V
    uq = np.full((MLA_HEADS * LANES,), -1, np.int64)
    uqs = uq.copy()
    uk = uq.copy()
    for h in range(MLA_HEADS):
        uq[h * LANES:h * LANES + qd] = h * qd + np.arange(qd)
        uqs[h * LANES + MLA_NOPE:h * LANES + qd] = h * qd + MLA_NOPE + _swap_halves(MLA_ROPE)
        uk[h * LANES:h * LANES + MLA_NOPE] = h * kd + np.arange(MLA_NOPE)
    return uq, uqs, uk


def _transposed_weights(w_in, gate_bias):
    width = NSA_KV_HEADS * HEAD_DIM
    gate_rows = jnp.pad(w_in[..., _O_GATE:_O_GATE + N_GATES], ((0, 0), (0, 0), (0, _T_SBV - _T_GATE - N_GATES)))
    rows = jnp.concatenate([w_in[..., _O_NVS:_O_NVS + width], w_in[..., _O_NVW:_O_NVW + width], gate_rows,
                            w_in[..., _O_SBV:_O_SBV + SB_HEADS * HEAD_DIM]], axis=-1)
    bias = jnp.pad(gate_bias, ((0, 0), (0, GATE_ROWS - N_GATES)))[..., None]
    return jnp.swapaxes(rows, -1, -2).astype(_MXU), bias


def _rope_tables(s):
    pos = jnp.arange(s, dtype=_F32)

    def cs(rot):
        half = rot // 2
        inv_freq = ROPE_THETA ** (-jnp.arange(half, dtype=_F32) / half)
        ang = pos[:, None] * inv_freq[None, :]
        c, sn = jnp.cos(ang), jnp.sin(ang)
        return jnp.concatenate([c, c], axis=1), jnp.concatenate([-sn, sn], axis=1)

    c, sn = cs(MLA_ROPE)
    ones = jnp.ones((s, MLA_NOPE), _F32)
    zeros = jnp.zeros((s, MLA_NOPE), _F32)
    pad = jnp.zeros((s, LANES - MLA_NOPE - MLA_ROPE), _F32)
    ck = jnp.concatenate([ones, c, pad], axis=1)
    sk = jnp.concatenate([zeros, sn, pad], axis=1)
    q_scale = (MLA_NOPE + MLA_ROPE) ** -0.5 * LOG2_E
    c, sn = cs(PARTIAL_ROT)
    c64 = jnp.concatenate([c, jnp.ones((s, HEAD_DIM - PARTIAL_ROT), _F32)], axis=1)
    s64 = jnp.concatenate([sn, jnp.zeros((s, HEAD_DIM - PARTIAL_ROT), _F32)], axis=1)
    ns = s // SEL_LEN
    onehot = (np.arange(s)[:, None] // SEL_LEN == np.arange(ns)[None, :]) * -NEG_INF
    return ck * q_scale, sk * q_scale, ck, sk, c64, s64, jnp.asarray(onehot, _MXU)


def kernel(x, ffn1_norm, ffn1_w_gate, ffn1_w_up, ffn1_w_down, mix_norm, w_in, mla_q_norm, mla_w_uq, mla_kv_norm,
           mla_w_ukv, nsa_gate_bias, nsa_cmp_pos_k, nsa_cmp_w1_k, nsa_cmp_w2_k, nsa_cmp_pos_v, nsa_cmp_w1_v,
           nsa_cmp_w2_v, w_out, ffn2_norm, ffn2_w_gate, ffn2_w_up, ffn2_w_down, final_norm):
    b, s, d = x.shape
    depth = w_in.shape[0]
    tabs = _rope_tables(s)
    in_idx = _w_in_index()
    uq_idx, uqs_idx, uk_idx = _mla_up_index()
    half = CMP_LEN * HEAD_DIM // 2
    fg = final_norm.reshape(1, d)

    def row(p):
        return p[:, None, :]

    def cmp_weights(w1, w2, pos, transpose_out):
        pos = jnp.broadcast_to(pos.reshape(depth, 2, 1, half), (depth, 2, 8, half)).astype(_MXU)
        w2 = jnp.swapaxes(w2, -1, -2) if transpose_out else w2
        return w1.reshape(depth, 2, half, CMP_HIDDEN).astype(_MXU), w2.astype(_MXU), pos

    ffn1 = [w.astype(_MXU) for w in (ffn1_w_gate, ffn1_w_up, ffn1_w_down)]
    ffn2 = [w.astype(_MXU) for w in (ffn2_w_gate, ffn2_w_up, ffn2_w_down)]
    w_t, gate_bias = _transposed_weights(w_in, nsa_gate_bias)
    wuv_t = mla_w_ukv.reshape(depth, MLA_KV_LORA, MLA_HEADS, 2, MLA_V)[:, :, :, 1].transpose(0, 2, 3, 1).astype(_MXU)
    proj_params = (row(mix_norm), _gather_cols(w_in, in_idx), w_t,
                   row(mla_q_norm), _gather_cols(mla_w_uq, uq_idx), _gather_cols(mla_w_uq, uqs_idx),
                   row(mla_kv_norm), _gather_cols(mla_w_ukv, uk_idx), wuv_t, gate_bias)
    cmp_params = (cmp_weights(nsa_cmp_w1_k, nsa_cmp_w2_k, nsa_cmp_pos_k, False)
                  + cmp_weights(nsa_cmp_w1_v, nsa_cmp_w2_v, nsa_cmp_pos_v, True))
    w_out = w_out.astype(_MXU)
    ffn1_norm, ffn2_norm = row(ffn1_norm), row(ffn2_norm)

    for l in range(depth):
        x2d = _ffn(x.reshape(b * s, d), ffn1_norm, *ffn1, l, fg, False)
        x = x2d.reshape(b, s, d)
        (mq, mk, mvt, nq, nkc, nvc, nks, nvst, nkw, nvwt, gates_t, sbq, sbk, sbv) = _proj(x, l, *proj_params, tabs)
        o_mla = _mla_attention(mq, mk, mvt)
        kc, vct = _compress(nkc, nvc, l, *cmp_params)
        o_cmp, q_sel = _cmp_select(nq, kc, vct, gates_t)
        o_sel = _sel_attention(q_sel, nks, nvst, gates_t)
        o_win = _win_attention(nq, nkw, nvwt, gates_t)
        o_sb = _sb_attention(sbq, sbk, sbv)
        x = _out_proj(x, o_mla, o_cmp, o_sel, o_win, o_sb, l, w_out)
        x2d = _ffn(x.reshape(b * s, d), ffn2_norm, *ffn2, l, fg, l == depth - 1)
        x = x2d.reshape(b, s, d)
    return x
```

```python
import functools
import math

import numpy as np
import jax
import jax.numpy as jnp
from jax import lax
from jax.experimental import pallas as pl
from jax.experimental.pallas import tpu as pltpu

D_MODEL = 1024
HEAD_DIM = 64
MLA_HEADS = 6
MLA_NOPE = 64
MLA_ROPE = 32
MLA_V = 64
MLA_Q_LORA = 256
MLA_KV_LORA = 128
NSA_HEADS = 6
NSA_KV_HEADS = 2
NSA_GROUP = NSA_HEADS // NSA_KV_HEADS
NSA_BRANCHES = 3
CMP_LEN = 32
CMP_STRIDE = 16
CMP_HIDDEN = 128
SEL_LEN = 64
SEL_TOPK = 16
WINDOW = 512
SB_HEADS = 4
D_FF = 2816
ROPE_THETA = 500000.0
PARTIAL_ROT = HEAD_DIM // 4
EPS = 1e-6
NEG_INF = -1e30
M_FLOOR = 0.1 * NEG_INF
FORCE_SCORE = 1e4
PICKED = -3e38
F32_EXP2_ZERO = -151.0
LOG2_E = math.log2(math.e)
N_GATES = NSA_HEADS * NSA_BRANCHES

LANES = 128
FFN_CHUNK = 256
SWEEP_UNROLL = 4
SB_KEY_BLOCK = 256
TOKEN_CHUNK = 256
ONES_PAD = 16
VMEM_LIMIT = 56 * 1024 * 1024

_MXU = jnp.bfloat16
_F32 = jnp.float32

_IN_WIDTHS = (MLA_Q_LORA, MLA_KV_LORA, MLA_ROPE, NSA_HEADS * HEAD_DIM) + (NSA_KV_HEADS * HEAD_DIM,) * 6 + (
    N_GATES, SB_HEADS * HEAD_DIM, SB_HEADS * HEAD_DIM, SB_HEADS * HEAD_DIM)
_IN_OFF = np.concatenate([[0], np.cumsum(_IN_WIDTHS)])
(_O_CQ, _O_CKV, _O_KR, _O_NQ, _O_NKC, _O_NVC, _O_NKS, _O_NVS, _O_NKW, _O_NVW, _O_GATE, _O_SBQ, _O_SBK,
 _O_SBV) = [int(v) for v in _IN_OFF[:-1]]

_S_CQ, _S_CKV, _S_KR, _S_KRS = 0, 2, 3, 4
_H_NQ, _H_NQS = 10, 16
_H_KC, _H_KCS, _H_VC = 22, 24, 26
_H_KS, _H_KSS = 28, 30
_H_KW, _H_KWS = 32, 34
_H_SBQ, _H_SBK = 36, 40
_N_HEAD_COLS = 44
_T_VS, _T_VW, _T_GATE = 0, NSA_KV_HEADS * HEAD_DIM, 2 * NSA_KV_HEADS * HEAD_DIM
GATE_ROWS = 24
_T_SBV = _T_GATE + 32
_T_ROWS = _T_SBV + SB_HEADS * HEAD_DIM


def _dot(a, b):
    return jnp.dot(a.astype(_MXU), b.astype(_MXU), preferred_element_type=_F32)


def _dot_nt(a, b):
    return lax.dot_general(a.astype(_MXU), b.astype(_MXU), (((1,), (1,)), ((), ())),
                           preferred_element_type=_F32)


def _dot_split_rhs(a, b):
    hi = b.astype(_MXU)
    lo = (b - hi.astype(_F32)).astype(_MXU)
    return (jnp.dot(a, hi, preferred_element_type=_F32) + jnp.dot(a, lo, preferred_element_type=_F32))


def _rms(x, g):
    return x * lax.rsqrt(jnp.mean(x * x, axis=-1, keepdims=True) + EPS) * g


def _params(*sem):
    return pltpu.CompilerParams(dimension_semantics=sem, vmem_limit_bytes=VMEM_LIMIT)


def _layer_spec(a, layer):
    return pl.BlockSpec((None,) + a.shape[1:], lambda *_: (layer,) + (0,) * (a.ndim - 1))


def _ffn_kernel(x_ref, g_ref, wg_ref, wu_ref, wd_ref, fg_ref, o_ref, h_ref, acc_ref, act_ref, *, final_norm):
    j = pl.program_id(1)

    @pl.when(j == 0)
    def _():
        h_ref[...] = _rms(x_ref[...], g_ref[...]).astype(h_ref.dtype)
        acc_ref[...] = jnp.zeros_like(acc_ref)

    h = h_ref[...]
    tf = act_ref.shape[1]
    for c0 in range(0, tf, FFN_CHUNK):
        c1 = min(c0 + FFN_CHUNK, tf)
        gate = jnp.dot(h, wg_ref[:, c0:c1], preferred_element_type=_F32)
        up = jnp.dot(h, wu_ref[:, c0:c1], preferred_element_type=_F32)
        act_ref[:, c0:c1] = (gate * jax.nn.sigmoid(gate) * up).astype(act_ref.dtype)
    acc_ref[...] += jnp.dot(act_ref[...], wd_ref[...], preferred_element_type=_F32)

    @pl.when(j == pl.num_programs(1) - 1)
    def _():
        y = x_ref[...] + 0.5 * acc_ref[...]
        if final_norm:
            y = _rms(y, fg_ref[...])
        o_ref[...] = y


def _ffn(x2d, g, wg, wu, wd, layer, fg, final_norm):
    rows = x2d.shape[0]
    tm = min(1024, rows)
    tf = D_FF // 2
    grid = (rows // tm, D_FF // tf)
    return pl.pallas_call(
        functools.partial(_ffn_kernel, final_norm=final_norm),
        grid=grid,
        in_specs=[
            pl.BlockSpec((tm, D_MODEL), lambda i, j: (i, 0)),
            _layer_spec(g, layer),
            pl.BlockSpec((None, D_MODEL, tf), lambda i, j: (layer, 0, j)),
            pl.BlockSpec((None, D_MODEL, tf), lambda i, j: (layer, 0, j)),
            pl.BlockSpec((None, tf, D_MODEL), lambda i, j: (layer, j, 0)),
            pl.BlockSpec((1, D_MODEL), lambda i, j: (0, 0)),
        ],
        out_specs=pl.BlockSpec((tm, D_MODEL), lambda i, j: (i, 0)),
        out_shape=jax.ShapeDtypeStruct((rows, D_MODEL), _F32),
        scratch_shapes=[pltpu.VMEM((tm, D_MODEL), _MXU), pltpu.VMEM((tm, D_MODEL), _F32), pltpu.VMEM((tm, tf), _MXU)],
        compiler_params=_params("parallel", "arbitrary"),
        name="ffn",
    )(x2d, g, wg, wu, wd, fg)


def _proj_kernel(x_ref, g_ref, w_ref, wt_ref, qn_ref, wuq_ref, wuqs_ref, kvn_ref, wuk_ref, wuv_ref, gb_ref,
                 cq_ref, sq_ref, ck_ref, sk_ref, c64_ref, s64_ref, oh_ref,
                 mq_ref, mk_ref, mv_ref, nq_ref, nkc_ref, nvc_ref, nks_ref, nvs_ref, nkw_ref, nvw_ref,
                 gate_ref, sbq_ref, sbk_ref, sbv_ref, stage_ref):
    hn = _rms(x_ref[0], g_ref[...]).astype(_MXU)

    def proj(h0, h1):
        return jnp.dot(hn, w_ref[:, h0 * HEAD_DIM:h1 * HEAD_DIM], preferred_element_type=_F32)

    def slot(p, s):
        return p[:, s * LANES:(s + 1) * LANES]

    def head(p, i):
        return p[:, i * HEAD_DIM:(i + 1) * HEAD_DIM]

    p = proj(0, _H_NQ)
    cq = _rms(p[:, :MLA_Q_LORA], qn_ref[...])
    ckv = _rms(slot(p, _S_CKV), kvn_ref[...])
    q = _dot(cq, wuq_ref[...])
    q_partner = _dot(cq, wuqs_ref[...])
    kpe = slot(p, _S_KR) * ck_ref[...] + slot(p, _S_KRS) * sk_ref[...]
    kn = _dot(ckv, wuk_ref[...])
    for h in range(MLA_HEADS):
        mq_ref[0, h] = (slot(q, h) * cq_ref[...] + slot(q_partner, h) * sq_ref[...]).astype(mq_ref.dtype)
        mk_ref[0, h] = (slot(kn, h) + kpe).astype(mk_ref.dtype)
        mv_ref[0, h, 0] = _ones_row_pad(_dot_nt(wuv_ref[h], ckv)).astype(mv_ref.dtype)

    c64 = c64_ref[...]
    s64 = s64_ref[...]
    scale = HEAD_DIM ** -0.5

    p = proj(_H_NQ, _H_KC)
    for h in range(NSA_HEADS):
        nq_ref[0, h] = ((head(p, h) * c64 + head(p, NSA_HEADS + h) * s64) * (scale * LOG2_E)).astype(nq_ref.dtype)

    p = proj(_H_KC, _H_SBQ)
    base = _H_KC
    ns = oh_ref.shape[-1]

    def roped(hk, hks, g):
        return head(p, hk - base + g) * c64 + head(p, hks - base + g) * s64

    def write_chunked(o_ref, g, val):
        stage_ref[...] = val
        for t in range(CMP_STRIDE):
            piece = stage_ref[pl.ds(t, val.shape[0] // CMP_STRIDE, stride=CMP_STRIDE), :]
            o_ref[0, g, :, t * HEAD_DIM:(t + 1) * HEAD_DIM] = piece.astype(o_ref.dtype)

    for g in range(NSA_KV_HEADS):
        write_chunked(nkc_ref, g, roped(_H_KC, _H_KCS, g))
        write_chunked(nvc_ref, g, head(p, _H_VC - base + g))
        nks_ref[0, g, :, 0:ns] = oh_ref[...]
        nks_ref[0, g, :, ns:ns + HEAD_DIM] = roped(_H_KS, _H_KSS, g).astype(nks_ref.dtype)
        nkw_ref[0, g] = roped(_H_KW, _H_KWS, g).astype(nkw_ref.dtype)

    pt = _dot_nt(wt_ref[...], hn)
    for g in range(NSA_KV_HEADS):
        lo = g * HEAD_DIM
        nvs_ref[0, g, 0] = _ones_row_pad(pt[_T_VS + lo:_T_VS + lo + HEAD_DIM]).astype(nvs_ref.dtype)
        nvw_ref[0, g, 0] = _ones_row_pad(pt[_T_VW + lo:_T_VW + lo + HEAD_DIM]).astype(nvw_ref.dtype)
    gate_ref[0] = jax.nn.sigmoid(pt[_T_GATE:_T_GATE + GATE_ROWS] + gb_ref[...])

    p = proj(_H_SBQ, _N_HEAD_COLS)
    for h in range(SB_HEADS):
        sbq_ref[0, h] = (head(p, h) * (scale * LOG2_E)).astype(sbq_ref.dtype)
        sbk_ref[0, h] = head(p, SB_HEADS + h).astype(sbk_ref.dtype)
        for c in range(sbv_ref.shape[2]):
            sbv_ref[0, h, c] = pt[_T_SBV + h * HEAD_DIM:_T_SBV + (h + 1) * HEAD_DIM,
                                  c * SB_KEY_BLOCK:(c + 1) * SB_KEY_BLOCK].astype(sbv_ref.dtype)


def _proj(x, layer, g, w_ext, w_t, qn, wuq, wuqs, kvn, wuk, wuv, gb, tabs):
    b, s, _ = x.shape
    ts = min(TOKEN_CHUNK, s)
    cq, sq, ck, sk, c64, s64, onehot = tabs
    full = functools.partial(_layer_spec, layer=layer)

    def tab(a):
        return pl.BlockSpec((ts, a.shape[1]), lambda bi, i: (i, 0))

    def heads(n, d):
        return (pl.BlockSpec((1, n, ts, d), lambda bi, i: (bi, 0, i, 0)),
                jax.ShapeDtypeStruct((b, n, s, d), _MXU))

    def values_t(n):
        return (pl.BlockSpec((1, n, 1, HEAD_DIM + ONES_PAD, ts), lambda bi, i: (bi, 0, i, 0, 0)),
                jax.ShapeDtypeStruct((b, n, s // ts, HEAD_DIM + ONES_PAD, ts), _MXU))

    def chunked():
        return (pl.BlockSpec((1, NSA_KV_HEADS, ts // CMP_STRIDE, CMP_STRIDE * HEAD_DIM), lambda bi, i: (bi, 0, i, 0)),
                jax.ShapeDtypeStruct((b, NSA_KV_HEADS, s // CMP_STRIDE, CMP_STRIDE * HEAD_DIM), _MXU))

    outs = [heads(MLA_HEADS, LANES), heads(MLA_HEADS, LANES), values_t(MLA_HEADS), heads(NSA_HEADS, HEAD_DIM),
            chunked(), chunked(), heads(NSA_KV_HEADS, onehot.shape[1] + HEAD_DIM), values_t(NSA_KV_HEADS),
            heads(NSA_KV_HEADS, HEAD_DIM), values_t(NSA_KV_HEADS),
            (pl.BlockSpec((1, GATE_ROWS, ts), lambda bi, i: (bi, 0, i)), jax.ShapeDtypeStruct((b, GATE_ROWS, s), _F32)),
            heads(SB_HEADS, HEAD_DIM), heads(SB_HEADS, HEAD_DIM),
            (pl.BlockSpec((1, SB_HEADS, ts // SB_KEY_BLOCK, HEAD_DIM, SB_KEY_BLOCK), lambda bi, i: (bi, 0, i, 0, 0)),
             jax.ShapeDtypeStruct((b, SB_HEADS, s // SB_KEY_BLOCK, HEAD_DIM, SB_KEY_BLOCK), _MXU))]
    return pl.pallas_call(
        _proj_kernel,
        grid=(b, s // ts),
        in_specs=[pl.BlockSpec((1, ts, D_MODEL), lambda bi, i: (bi, i, 0)), full(g), full(w_ext), full(w_t), full(qn),
                  full(wuq), full(wuqs), full(kvn), full(wuk), full(wuv), full(gb),
                  tab(cq), tab(sq), tab(ck), tab(sk), tab(c64), tab(s64), tab(onehot)],
        out_specs=[o[0] for o in outs],
        out_shape=[o[1] for o in outs],
        scratch_shapes=[pltpu.VMEM((ts, HEAD_DIM), _F32)],
        compiler_params=_params("parallel", "parallel"),
        name="proj",
    )(x, g, w_ext, w_t, qn, wuq, wuqs, kvn, wuk, wuv, gb, cq, sq, ck, sk, c64, s64, onehot)


def _ones_row_pad(vt):
    first = lax.broadcasted_iota(jnp.int32, (ONES_PAD, vt.shape[1]), 0) == 0
    return jnp.concatenate([vt, jnp.where(first, 1.0, 0.0).astype(vt.dtype)], axis=0)


def _softmax_step_t(carry, st, vt_chunks):
    m, acc = carry
    m_new = jnp.maximum(m, jnp.max(st, axis=0, keepdims=True))
    alpha = jnp.exp2(m - m_new)
    pt = jnp.exp2(st - m_new).astype(_MXU)
    n = st.shape[0] // len(vt_chunks)
    pv = sum(jnp.dot(vt, pt[c * n:(c + 1) * n], preferred_element_type=_F32) for c, vt in enumerate(vt_chunks))
    return m_new, alpha * acc + pv


def _softmax_init_t(d, cols):
    return (jnp.full((1, cols), M_FLOOR, _F32), jnp.zeros((d + ONES_PAD, cols), _F32))


def _softmax_finish_t(carry, d):
    _, acc = carry
    return acc[:d] * (1.0 / acc[d:d + 1])


def _two_chain_sweep(n_full, qk, soft, init):
    def body(j, carry, diag=False):
        c0, c1 = carry
        qk(0, j)
        c1 = soft(1, j, c1, diag)
        qk(1, jnp.zeros_like(j) if diag else j + 1)
        c0 = soft(0, j, c0, diag)
        return c0, c1

    def unrolled(i, carry):
        for u in range(SWEEP_UNROLL):
            carry = body(SWEEP_UNROLL * i + u, carry)
        return carry

    qk(1, n_full)
    carry = body(n_full, init, True)
    trips = n_full // SWEEP_UNROLL
    carry = lax.fori_loop(0, trips, unrolled, carry)
    return lax.fori_loop(SWEEP_UNROLL * trips, n_full, body, carry)


def _mla_kernel(q_ref, k_ref, vt_ref, o_ref, s0_ref, s1_ref, *, t, nsub):
    qi = pl.program_id(2)
    s_refs = (s0_ref, s1_ref)

    def qk(hh, j):
        off = pl.multiple_of(j * t, t)
        s_refs[hh][...] = _dot_nt(k_ref[0, hh, pl.ds(off, t), :], q_ref[0, hh])

    def soft(hh, j, carry, diag):
        st = s_refs[hh][...]
        if diag:
            key = lax.broadcasted_iota(jnp.int32, (t, t), 0)
            qry = lax.broadcasted_iota(jnp.int32, (t, t), 1)
            st = jnp.where(key <= qry, st, NEG_INF)
        return _softmax_step_t(carry, st, [vt_ref[0, hh, j * nsub + c] for c in range(nsub)])

    carry = _two_chain_sweep(qi, qk, soft, tuple(_softmax_init_t(MLA_V, t) for _ in range(2)))
    ot = jnp.concatenate([_softmax_finish_t(c, MLA_V) for c in carry], axis=0)
    o_ref[0] = ot.T.astype(o_ref.dtype)


def _mla_attention(q, k, vt):
    b, h, s, _ = q.shape
    tv = vt.shape[-1]
    dv = vt.shape[-2]
    t = min(512, s)
    assert h % 2 == 0 and 2 * MLA_V == LANES and t % tv == 0 and s % t == 0
    return pl.pallas_call(
        functools.partial(_mla_kernel, t=t, nsub=t // tv),
        grid=(b, h // 2, s // t),
        in_specs=[pl.BlockSpec((1, 2, t, LANES), lambda bi, hi, i: (bi, hi, i, 0)),
                  pl.BlockSpec((1, 2, s, LANES), lambda bi, hi, i: (bi, hi, 0, 0)),
                  pl.BlockSpec((1, 2, s // tv, dv, tv), lambda bi, hi, i: (bi, hi, 0, 0, 0))],
        out_specs=pl.BlockSpec((1, t, LANES), lambda bi, hi, i: (bi, i, hi)),
        out_shape=jax.ShapeDtypeStruct((b, s, h * MLA_V), _MXU),
        scratch_shapes=[pltpu.VMEM((t, t), _F32), pltpu.VMEM((t, t), _F32)],
        compiler_params=_params("parallel", "parallel", "arbitrary"),
        name="mla_attn",
    )(q, k, vt)


def _sb_kernel(q_ref, k_ref, vt_ref, u_ref, o_ref, *scratch_refs, tq, tk):
    qi = pl.program_id(1)
    u = u_ref[...]
    n_heads = q_ref.shape[1]
    scratch = [scratch_refs[6 * hh:6 * (hh + 1)] for hh in range(n_heads)]
    per_tile = tq // tk

    def step(j, carry, key_offset=None):
        diag = key_offset is not None
        off = pl.multiple_of(j * tk, tk)
        if diag:
            key = key_offset + lax.broadcasted_iota(jnp.int32, (tk, tq), 0)
            qry = lax.broadcasted_iota(jnp.int32, (tk, tq), 1)
            strict = key < qry

        def logits(hh):
            z_ref, _, _, _, _, _ = scratch[hh]
            z_ref[...] = _dot_nt(k_ref[0, hh, pl.ds(off, tk), :], q_ref[0, hh])

        def log_terms(hh):
            z_ref, lb_ref, hi_ref, lo_ref, _, _ = scratch[hh]
            z = z_ref[...]
            log_beta = jnp.minimum(z, 0.0) - jnp.log2(1.0 + jnp.exp2(-jnp.abs(z)))
            log_rem = log_beta - z
            if diag:
                log_rem = jnp.where(strict, log_rem, 0.0)
            hi = log_rem.astype(_MXU)
            lb_ref[...] = log_beta
            hi_ref[...] = hi
            lo_ref[...] = (log_rem - hi.astype(_F32)).astype(_MXU)
            return log_rem[0:1, :]

        def suffix_sums(hh):
            _, _, hi_ref, lo_ref, sfx_ref, _ = scratch[hh]
            sfx_ref[...] = (jnp.dot(u, hi_ref[...], preferred_element_type=_F32)
                            + jnp.dot(u, lo_ref[...], preferred_element_type=_F32))

        def weights(hh, first_rem):
            _, lb_ref, _, _, sfx_ref, a_ref = scratch[hh]
            rem = carry[hh][0]
            suffix = sfx_ref[...]
            a = jnp.exp2(lb_ref[...] + suffix + rem)
            if diag:
                a = jnp.where(strict, a, 0.0)
            a_ref[...] = a.astype(_MXU)
            return rem + suffix[0:1, :] + first_rem

        def values(hh):
            a_ref = scratch[hh][5]
            return carry[hh][1] + jnp.dot(vt_ref[0, hh, j], a_ref[...], preferred_element_type=_F32)

        heads = range(n_heads)
        for hh in heads:
            logits(hh)
        first = []
        for hh in heads:
            first.append(log_terms(hh))
            suffix_sums(hh)
        rems = [weights(hh, first[hh]) for hh in heads]
        alive = jnp.max(functools.reduce(jnp.maximum, rems)) > F32_EXP2_ZERO
        return alive, tuple((rems[hh], values(hh)) for hh in heads)

    carry = tuple((jnp.zeros((1, tq), _F32), jnp.zeros((HEAD_DIM, tq), _F32)) for _ in range(n_heads))
    first = qi * per_tile
    for i in reversed(range(per_tile)):
        _, carry = step(first + i, carry, key_offset=i * tk)
    has_past = first > 0
    carry = tuple((jnp.where(has_past, rem, NEG_INF), acc) for rem, acc in carry)
    alive, carry = step(jnp.maximum(first - 1, 0), carry)

    def earlier(c):
        return (c[0] - 1,) + step(c[0], c[2])

    _, _, carry = lax.while_loop(lambda c: jnp.logical_and(c[0] >= 0, c[1]), earlier, (first - 2, alive, carry))
    o_ref[0] = jnp.concatenate([acc for _, acc in carry], axis=0).T.astype(o_ref.dtype)


def _sb_attention(q, k, vt):
    b, h, s, d = q.shape
    tk = vt.shape[-1]
    tq = tk
    assert (h * d) % LANES == 0 and s % tq == 0 and tq % tk == 0
    idx = np.arange(tk)
    u = jnp.asarray(idx[None, :] > idx[:, None], _MXU)
    return pl.pallas_call(
        functools.partial(_sb_kernel, tq=tq, tk=tk),
        grid=(b, s // tq),
        in_specs=[pl.BlockSpec((1, h, tq, d), lambda bi, i: (bi, 0, i, 0)),
                  pl.BlockSpec((1, h, s, d), lambda bi, i: (bi, 0, 0, 0)),
                  pl.BlockSpec((1, h, s // tk, d, tk), lambda bi, i: (bi, 0, 0, 0, 0)),
                  pl.BlockSpec((tk, tk), lambda bi, i: (0, 0))],
        out_specs=pl.BlockSpec((1, tq, h * d), lambda bi, i: (bi, i, 0)),
        out_shape=jax.ShapeDtypeStruct((b, s, h * d), _MXU),
        scratch_shapes=[pltpu.VMEM((tk, tq), dt) for _ in range(h) for dt in (_F32, _F32, _MXU, _MXU, _F32, _MXU)],
        compiler_params=_params("parallel", "arbitrary"),
        name="sb_attn",
    )(q, k, vt, u)


def _compress_kernel(xk_ref, xv_ref, w1k_ref, w2k_ref, pk_ref, w1v_ref, w2v_ref, pv_ref, ok_ref, ov_ref):
    def hidden(x_ref, w1_ref, p_ref):
        x = x_ref[0, 0]
        n = x.shape[0]
        first = jnp.dot(x, w1_ref[0], preferred_element_type=_F32)
        second = jnp.dot(x, w1_ref[1], preferred_element_type=_F32)
        pos = _dot(p_ref[0], w1_ref[0]) + _dot(p_ref[1], w1_ref[1])
        hid = first + pltpu.roll(second, n - 1, 0) + pos[0:1]
        return 0.5 * hid * (1.0 + jnp.tanh(math.sqrt(2.0 / math.pi) * (hid + 0.044715 * hid * hid * hid)))

    ok_ref[0, 0] = _dot(hidden(xk_ref, w1k_ref, pk_ref), w2k_ref[...]).astype(ok_ref.dtype)
    ov_ref[0, 0] = _dot_nt(w2v_ref[...], hidden(xv_ref, w1v_ref, pv_ref)).astype(ov_ref.dtype)


def _compress(xk, xv, layer, w1k, w2k, pk, w1v, w2v, pv):
    b, g, n, _ = xk.shape
    d = HEAD_DIM
    full = functools.partial(_layer_spec, layer=layer)

    xspec = pl.BlockSpec((1, 1, n, CMP_STRIDE * d), lambda bi, gi: (bi, gi, 0, 0))
    return pl.pallas_call(
        _compress_kernel,
        grid=(b, g),
        in_specs=[xspec, xspec, full(w1k), full(w2k), full(pk), full(w1v), full(w2v), full(pv)],
        out_specs=[pl.BlockSpec((1, 1, n, d), lambda bi, gi: (bi, gi, 0, 0)),
                   pl.BlockSpec((1, 1, d, n), lambda bi, gi: (bi, gi, 0, 0))],
        out_shape=[jax.ShapeDtypeStruct((b, g, n, d), _MXU), jax.ShapeDtypeStruct((b, g, d, n), _MXU)],
        compiler_params=_params("parallel", "parallel"),
        name="nsa_compress",
    )(xk, xv, w1k, w2k, pk, w1v, w2v, pv)


def _group_queries(q_ref, g, tq):
    return q_ref[0, g * NSA_GROUP:(g + 1) * NSA_GROUP].reshape(NSA_GROUP * tq, q_ref.shape[-1])


def _gated_heads(ot, gt_ref, g, branch, tq):
    out = []
    for r in range(NSA_GROUP):
        row = NSA_BRANCHES * (g * NSA_GROUP + r) + branch
        out.append(ot[:, r * tq:(r + 1) * tq] * gt_ref[0, row:row + 1, :])
    return out


def _cmp_kernel(q_ref, kc_ref, vct_ref, ov_ref, gt_ref, o_ref, qa_ref, s0_ref, s1_ref, *, tq, n_top):
    q0 = pl.program_id(1) * tq
    ncp = kc_ref.shape[2]
    ns = ov_ref.shape[0]
    lanes = NSA_GROUP * tq
    s_refs = (s0_ref, s1_ref)
    for g in range(NSA_KV_HEADS):
        s_refs[g][...] = _dot_nt(kc_ref[0, g], _group_queries(q_ref, g, tq))
    qpos = q0 + (lax.broadcasted_iota(jnp.int32, (1, lanes), 1) & (tq - 1))
    cmp_end = lax.broadcasted_iota(jnp.int32, (ncp, 1), 0) * CMP_STRIDE + (CMP_LEN - 1)
    visible = cmp_end <= qpos
    cur = jnp.right_shift(q0 + lax.broadcasted_iota(jnp.int32, (1, tq), 1), int(math.log2(SEL_LEN)))
    blk = lax.broadcasted_iota(jnp.int32, (ns, 1), 0)
    forced = (blk == 0) | (blk == cur) | (blk == cur - 1)
    future = blk > cur
    blk_f = blk.astype(_F32)
    heads = []
    scores = []
    for g in range(NSA_KV_HEADS):
        st = jnp.where(visible, s_refs[g][...], NEG_INF)
        e = jnp.exp2(st - jnp.max(st, axis=0, keepdims=True))
        inv = jnp.where(qpos >= CMP_LEN - 1, 1.0 / jnp.sum(e, axis=0, keepdims=True), 0.0)
        pt = e * inv
        heads += _gated_heads(_dot(vct_ref[0, g], pt), gt_ref, g, 0, tq)
        p_sum = sum(pt[:, r * tq:(r + 1) * tq] for r in range(NSA_GROUP))
        score = _dot_split_rhs(ov_ref[...], p_sum)
        scores.append(jnp.where(forced, FORCE_SCORE, jnp.where(future, -1.0, score)))
    o_ref[0] = jnp.concatenate(heads, axis=0).T.astype(o_ref.dtype)
    def select(rows):
        sc = [s[:rows] for s in scores]
        idx = blk_f[:rows]
        for _ in range(n_top):
            for g in range(NSA_KV_HEADS):
                top = jnp.max(sc[g], axis=0, keepdims=True)
                first = jnp.min(jnp.where(sc[g] == top, idx, float(ns)), axis=0, keepdims=True)
                sc[g] = jnp.where(idx == first, PICKED, sc[g])
        for g in range(NSA_KV_HEADS):
            sel_m1 = jnp.where(sc[g] < 0.5 * PICKED, 0.0, -1.0)
            if rows < ns:
                sel_m1 = jnp.concatenate([sel_m1, jnp.full((ns - rows, tq), -1.0, _F32)], axis=0)
            sel_m1 = sel_m1.T.astype(qa_ref.dtype)
            for h in range(g * NSA_GROUP, (g + 1) * NSA_GROUP):
                qa_ref[0, h, :, 0:ns] = sel_m1
                qa_ref[0, h, :, ns:ns + HEAD_DIM] = q_ref[0, h]

    visible_blocks = (q0 + tq) // SEL_LEN
    lower = 0
    for rows in sorted({min(ns, 32), min(ns, 64), ns}):
        in_range = visible_blocks > lower
        if rows < ns:
            in_range = jnp.logical_and(in_range, visible_blocks <= rows)
        pl.when(in_range)(functools.partial(select, rows))
        lower = rows


def _cmp_select(q, kc, vct, gates_t):
    b, h, s, d = q.shape
    g = kc.shape[1]
    ncp = kc.shape[2]
    ns = s // SEL_LEN
    n_top = min(SEL_TOPK, ns)
    tq = min(256, s)
    assert tq & (tq - 1) == 0 and g == 2
    c0 = np.arange(ncp)[:, None] * CMP_STRIDE
    n0 = np.arange(ns)[None, :] * SEL_LEN
    overlap = jnp.asarray(((c0 < n0 + SEL_LEN) & (c0 + CMP_LEN > n0)).T, _MXU)
    return pl.pallas_call(
        functools.partial(_cmp_kernel, tq=tq, n_top=n_top),
        grid=(b, s // tq),
        in_specs=[pl.BlockSpec((1, h, tq, d), lambda bi, i: (bi, 0, i, 0)),
                  pl.BlockSpec((1, g, ncp, d), lambda bi, i: (bi, 0, 0, 0)),
                  pl.BlockSpec((1, g, d, ncp), lambda bi, i: (bi, 0, 0, 0)),
                  pl.BlockSpec((ns, ncp), lambda bi, i: (0, 0)),
                  pl.BlockSpec((1, GATE_ROWS, tq), lambda bi, i: (bi, 0, i))],
        out_specs=[pl.BlockSpec((1, tq, h * d), lambda bi, i: (bi, i, 0)),
                   pl.BlockSpec((1, h, tq, ns + d), lambda bi, i: (bi, 0, i, 0))],
        out_shape=[jax.ShapeDtypeStruct((b, s, h * d), _MXU), jax.ShapeDtypeStruct((b, h, s, ns + d), _MXU)],
        scratch_shapes=[pltpu.VMEM((ncp, NSA_GROUP * tq), _F32) for _ in range(g)],
        compiler_params=_params("parallel", "arbitrary"),
        name="nsa_cmp_select",
    )(q, kc, vct, overlap, gates_t)


def _key_minus_query(keys, tq):
    return jnp.asarray(np.arange(keys)[:, None] - np.arange(NSA_GROUP * tq)[None, :] % tq, jnp.int32)


def _sel_kernel(q_ref, k_ref, vt_ref, gt_ref, rel_ref, o_ref, s0_ref, s1_ref, *, tq, tk, nsub):
    q0 = pl.program_id(1) * tq
    last = (q0 + tq - 1) // tk
    lanes = NSA_GROUP * tq
    s_refs = (s0_ref, s1_ref)

    def qk(g, j):
        off = pl.multiple_of(j * tk, tk)
        s_refs[g][...] = _dot_nt(k_ref[0, g, pl.ds(off, tk), :], _group_queries(q_ref, g, tq))

    def soft(g, j, carry, causal):
        st = s_refs[g][...]
        if causal:
            st = jnp.where(rel_ref[...] <= q0 - j * tk, st, NEG_INF)
        return _softmax_step_t(carry, st, [vt_ref[0, g, j * nsub + c] for c in range(nsub)])

    init = tuple(_softmax_init_t(HEAD_DIM, lanes) for _ in range(NSA_KV_HEADS))
    carry = _two_chain_sweep(last, qk, soft, init)
    heads = []
    for g in range(NSA_KV_HEADS):
        heads += _gated_heads(_softmax_finish_t(carry[g], HEAD_DIM), gt_ref, g, 1, tq)
    o_ref[0] = jnp.concatenate(heads, axis=0).T.astype(o_ref.dtype)


def _sel_attention(q, k, vt, gates_t):
    b, h, s, da = q.shape
    g = k.shape[1]
    d = HEAD_DIM
    tv = vt.shape[-1]
    tq = min(256, s)
    tk = min(512, s)
    assert tq & (tq - 1) == 0 and s % tk == 0 and tk % tv == 0 and g == 2
    return pl.pallas_call(
        functools.partial(_sel_kernel, tq=tq, tk=tk, nsub=tk // tv),
        grid=(b, s // tq),
        in_specs=[pl.BlockSpec((1, h, tq, da), lambda bi, i: (bi, 0, i, 0)),
                  pl.BlockSpec((1, g, s, da), lambda bi, i: (bi, 0, 0, 0)),
                  pl.BlockSpec((1, g) + vt.shape[2:], lambda bi, i: (bi, 0, 0, 0, 0)),
                  pl.BlockSpec((1, GATE_ROWS, tq), lambda bi, i: (bi, 0, i)),
                  pl.BlockSpec((tk, NSA_GROUP * tq), lambda bi, i: (0, 0))],
        out_specs=pl.BlockSpec((1, tq, h * d), lambda bi, i: (bi, i, 0)),
        out_shape=jax.ShapeDtypeStruct((b, s, h * d), _MXU),
        scratch_shapes=[pltpu.VMEM((tk, NSA_GROUP * tq), _F32) for _ in range(g)],
        compiler_params=_params("parallel", "arbitrary"),
        name="nsa_selected",
    )(q, k, vt, gates_t, _key_minus_query(tk, tq))


def _win_kernel(q_ref, k_ref, vt_ref, gt_ref, rel_ref, o_ref, s0_ref, s1_ref, *, tq, span, tv):
    q0 = pl.program_id(1) * tq
    start = pl.multiple_of(jnp.maximum(q0 - WINDOW, 0), tq)
    first_chunk = start // tv
    lanes = NSA_GROUP * tq
    s_refs = (s0_ref, s1_ref)
    for g in range(NSA_KV_HEADS):
        s_refs[g][...] = _dot_nt(k_ref[0, g, pl.ds(start, span), :], _group_queries(q_ref, g, tq))
    rel = rel_ref[...]
    offset = q0 - start
    heads = []
    for g in range(NSA_KV_HEADS):
        st = jnp.where(rel <= offset, s_refs[g][...], NEG_INF)
        st = jnp.where(rel > offset - WINDOW, st, NEG_INF)
        carry = _softmax_step_t(_softmax_init_t(HEAD_DIM, lanes), st,
                                [vt_ref[0, g, first_chunk + c] for c in range(span // tv)])
        heads += _gated_heads(_softmax_finish_t(carry, HEAD_DIM), gt_ref, g, 2, tq)
    o_ref[0] = jnp.concatenate(heads, axis=0).T.astype(o_ref.dtype)


def _win_attention(q, k, vt, gates_t):
    b, h, s, d = q.shape
    g = k.shape[1]
    tv = vt.shape[-1]
    tq = min(256, s)
    span = WINDOW + tq
    assert tq & (tq - 1) == 0 and s >= span and tq % tv == 0 and WINDOW % tv == 0 and g == 2
    return pl.pallas_call(
        functools.partial(_win_kernel, tq=tq, span=span, tv=tv),
        grid=(b, s // tq),
        in_specs=[pl.BlockSpec((1, h, tq, d), lambda bi, i: (bi, 0, i, 0)),
                  pl.BlockSpec((1, g, s, d), lambda bi, i: (bi, 0, 0, 0)),
                  pl.BlockSpec((1, g) + vt.shape[2:], lambda bi, i: (bi, 0, 0, 0, 0)),
                  pl.BlockSpec((1, GATE_ROWS, tq), lambda bi, i: (bi, 0, i)),
                  pl.BlockSpec((span, NSA_GROUP * tq), lambda bi, i: (0, 0))],
        out_specs=pl.BlockSpec((1, tq, h * d), lambda bi, i: (bi, i, 0)),
        out_shape=jax.ShapeDtypeStruct((b, s, h * d), _MXU),
        scratch_shapes=[pltpu.VMEM((span, NSA_GROUP * tq), _F32) for _ in range(g)],
        compiler_params=_params("parallel", "arbitrary"),
        name="nsa_window",
    )(q, k, vt, gates_t, _key_minus_query(span, tq))


def _out_kernel(x_ref, mla_ref, cmp_ref, sel_ref, win_ref, sb_ref, w_ref, o_ref):
    def w_rows(first_head, n_heads):
        return w_ref[first_head * HEAD_DIM:(first_head + n_heads) * HEAD_DIM, :]

    acc = x_ref[0] + jnp.dot(mla_ref[0], w_rows(0, MLA_HEADS), preferred_element_type=_F32)
    nsa = cmp_ref[0].astype(_F32) + sel_ref[0].astype(_F32) + win_ref[0].astype(_F32)
    acc = acc + _dot(nsa, w_rows(MLA_HEADS, NSA_HEADS))
    o_ref[0] = acc + jnp.dot(sb_ref[0], w_rows(MLA_HEADS + NSA_HEADS, SB_HEADS), preferred_element_type=_F32)


def _out_proj(x, o_mla, o_cmp, o_sel, o_win, o_sb, layer, w_heads):
    b, s, _ = x.shape
    ts = min(512, s)

    def rows(a):
        return pl.BlockSpec((1, ts, a.shape[2]), lambda bi, i: (bi, i, 0))

    xspec = pl.BlockSpec((1, ts, D_MODEL), lambda bi, i: (bi, i, 0))
    return pl.pallas_call(
        _out_kernel,
        grid=(b, s // ts),
        in_specs=[xspec, rows(o_mla), rows(o_cmp), rows(o_sel), rows(o_win), rows(o_sb),
                  _layer_spec(w_heads, layer)],
        out_specs=xspec,
        out_shape=jax.ShapeDtypeStruct(x.shape, _F32),
        compiler_params=_params("parallel", "parallel"),
        name="out_proj",
    )(x, o_mla, o_cmp, o_sel, o_win, o_sb, w_heads)


def _gather_cols(w, idx):
    idx = np.asarray(idx)
    cuts = [0] + [i for i in range(1, len(idx)) if idx[i] != idx[i - 1] + (idx[i - 1] >= 0)] + [len(idx)]
    pieces = []
    for a, b in zip(cuts[:-1], cuts[1:]):
        if idx[a] < 0:
            pieces.append(jnp.zeros(w.shape[:-1] + (b - a,), _MXU))
        else:
            pieces.append(w[..., int(idx[a]):int(idx[a]) + b - a].astype(_MXU))
    return jnp.concatenate(pieces, axis=-1)


def _swap_halves(rot):
    return (np.arange(rot) + rot // 2) % rot


def _w_in_index():
    idx = np.full((_N_HEAD_COLS * HEAD_DIM,), -1, np.int64)

    def put(col, src):
        src = np.asarray(src)
        idx[col:col + len(src)] = src

    def put_head(pos, src):
        put(pos * HEAD_DIM, src)

    put(_S_CQ * LANES, _O_CQ + np.arange(MLA_Q_LORA))
    put(_S_CKV * LANES, _O_CKV + np.arange(MLA_KV_LORA))
    put(_S_KR * LANES + MLA_NOPE, _O_KR + np.arange(MLA_ROPE))
    put(_S_KRS * LANES + MLA_NOPE, _O_KR + _swap_halves(MLA_ROPE))
    for h in range(NSA_HEADS):
        put_head(_H_NQ + h, _O_NQ + h * HEAD_DIM + np.arange(HEAD_DIM))
        put_head(_H_NQS + h, _O_NQ + h * HEAD_DIM + _swap_halves(PARTIAL_ROT))
    for hk, hks, ok in ((_H_KC, _H_KCS, _O_NKC), (_H_KS, _H_KSS, _O_NKS), (_H_KW, _H_KWS, _O_NKW)):
        for g in range(NSA_KV_HEADS):
            put_head(hk + g, ok + g * HEAD_DIM + np.arange(HEAD_DIM))
            put_head(hks + g, ok + g * HEAD_DIM + _swap_halves(PARTIAL_ROT))
    for g in range(NSA_KV_HEADS):
        put_head(_H_VC + g, _O_NVC + g * HEAD_DIM + np.arange(HEAD_DIM))
    for h in range(SB_HEADS):
        put_head(_H_SBQ + h, _O_SBQ + h * HEAD_DIM + np.arange(HEAD_DIM))
        put_head(_H_SBK + h, _O_SBK + h * HEAD_DIM + np.arange(HEAD_DIM))
    return idx


def _mla_up_index():
    qd = MLA_NOPE + MLA_ROPE
    kd = MLA_NOPE + MLA_V
    uq = np.full((MLA_HEADS * LANES,), -1, np.int64)
    uqs = uq.copy()
    uk = uq.copy()
    for h in range(MLA_HEADS):
        uq[h * LANES:h * LANES + qd] = h * qd + np.arange(qd)
        uqs[h * LANES + MLA_NOPE:h * LANES + qd] = h * qd + MLA_NOPE + _swap_halves(MLA_ROPE)
        uk[h * LANES:h * LANES + MLA_NOPE] = h * kd + np.arange(MLA_NOPE)
    return uq, uqs, uk


def _transposed_weights(w_in, gate_bias):
    width = NSA_KV_HEADS * HEAD_DIM
    gate_rows = jnp.pad(w_in[..., _O_GATE:_O_GATE + N_GATES], ((0, 0), (0, 0), (0, _T_SBV - _T_GATE - N_GATES)))
    rows = jnp.concatenate([w_in[..., _O_NVS:_O_NVS + width], w_in[..., _O_NVW:_O_NVW + width], gate_rows,
                            w_in[..., _O_SBV:_O_SBV + SB_HEADS * HEAD_DIM]], axis=-1)
    bias = jnp.pad(gate_bias, ((0, 0), (0, GATE_ROWS - N_GATES)))[..., None]
    return jnp.swapaxes(rows, -1, -2).astype(_MXU), bias


def _rope_tables(s):
    pos = jnp.arange(s, dtype=_F32)

    def cs(rot):
        half = rot // 2
        inv_freq = ROPE_THETA ** (-jnp.arange(half, dtype=_F32) / half)
        ang = pos[:, None] * inv_freq[None, :]
        c, sn = jnp.cos(ang), jnp.sin(ang)
        return jnp.concatenate([c, c], axis=1), jnp.concatenate([-sn, sn], axis=1)

    c, sn = cs(MLA_ROPE)
    ones = jnp.ones((s, MLA_NOPE), _F32)
    zeros = jnp.zeros((s, MLA_NOPE), _F32)
    pad = jnp.zeros((s, LANES - MLA_NOPE - MLA_ROPE), _F32)
    ck = jnp.concatenate([ones, c, pad], axis=1)
    sk = jnp.concatenate([zeros, sn, pad], axis=1)
    q_scale = (MLA_NOPE + MLA_ROPE) ** -0.5 * LOG2_E
    c, sn = cs(PARTIAL_ROT)
    c64 = jnp.concatenate([c, jnp.ones((s, HEAD_DIM - PARTIAL_ROT), _F32)], axis=1)
    s64 = jnp.concatenate([sn, jnp.zeros((s, HEAD_DIM - PARTIAL_ROT), _F32)], axis=1)
    ns = s // SEL_LEN
    onehot = (np.arange(s)[:, None] // SEL_LEN == np.arange(ns)[None, :]) * -NEG_INF
    return ck * q_scale, sk * q_scale, ck, sk, c64, s64, jnp.asarray(onehot, _MXU)


def kernel(x, ffn1_norm, ffn1_w_gate, ffn1_w_up, ffn1_w_down, mix_norm, w_in, mla_q_norm, mla_w_uq, mla_kv_norm,
           mla_w_ukv, nsa_gate_bias, nsa_cmp_pos_k, nsa_cmp_w1_k, nsa_cmp_w2_k, nsa_cmp_pos_v, nsa_cmp_w1_v,
           nsa_cmp_w2_v, w_out, ffn2_norm, ffn2_w_gate, ffn2_w_up, ffn2_w_down, final_norm):
    b, s, d = x.shape
    depth = w_in.shape[0]
    tabs = _rope_tables(s)
    in_idx = _w_in_index()
    uq_idx, uqs_idx, uk_idx = _mla_up_index()
    half = CMP_LEN * HEAD_DIM // 2
    fg = final_norm.reshape(1, d)

    def row(p):
        return p[:, None, :]

    def cmp_weights(w1, w2, pos, transpose_out):
        pos = jnp.broadcast_to(pos.reshape(depth, 2, 1, half), (depth, 2, 8, half)).astype(_MXU)
        w2 = jnp.swapaxes(w2, -1, -2) if transpose_out else w2
        return w1.reshape(depth, 2, half, CMP_HIDDEN).astype(_MXU), w2.astype(_MXU), pos

    ffn1 = [w.astype(_MXU) for w in (ffn1_w_gate, ffn1_w_up, ffn1_w_down)]
    ffn2 = [w.astype(_MXU) for w in (ffn2_w_gate, ffn2_w_up, ffn2_w_down)]
    w_t, gate_bias = _transposed_weights(w_in, nsa_gate_bias)
    wuv_t = mla_w_ukv.reshape(depth, MLA_KV_LORA, MLA_HEADS, 2, MLA_V)[:, :, :, 1].transpose(0, 2, 3, 1).astype(_MXU)
    proj_params = (row(mix_norm), _gather_cols(w_in, in_idx), w_t,
                   row(mla_q_norm), _gather_cols(mla_w_uq, uq_idx), _gather_cols(mla_w_uq, uqs_idx),
                   row(mla_kv_norm), _gather_cols(mla_w_ukv, uk_idx), wuv_t, gate_bias)
    cmp_params = (cmp_weights(nsa_cmp_w1_k, nsa_cmp_w2_k, nsa_cmp_pos_k, False)
                  + cmp_weights(nsa_cmp_w1_v, nsa_cmp_w2_v, nsa_cmp_pos_v, True))
    w_out = w_out.astype(_MXU)
    ffn1_norm, ffn2_norm = row(ffn1_norm), row(ffn2_norm)

    for l in range(depth):
        x2d = _ffn(x.reshape(b * s, d), ffn1_norm, *ffn1, l, fg, False)
        x = x2d.reshape(b, s, d)
        (mq, mk, mvt, nq, nkc, nvc, nks, nvst, nkw, nvwt, gates_t, sbq, sbk, sbv) = _proj(x, l, *proj_params, tabs)
        o_mla = _mla_attention(mq, mk, mvt)
        kc, vct = _compress(nkc, nvc, l, *cmp_params)
        o_cmp, q_sel = _cmp_select(nq, kc, vct, gates_t)
        o_sel = _sel_attention(q_sel, nks, nvst, gates_t)
        o_win = _win_attention(nq, nkw, nvwt, gates_t)
        o_sb = _sb_attention(sbq, sbk, sbv)
        x = _out_proj(x, o_mla, o_cmp, o_sel, o_win, o_sb, l, w_out)
        x2d = _ffn(x.reshape(b * s, d), ffn2_norm, *ffn2, l, fg, l == depth - 1)
        x = x2d.reshape(b, s, d)
    return x
```

```python
import functools
import math

import numpy as np
import jax
import jax.numpy as jnp
from jax import lax
from jax.experimental import pallas as pl
from jax.experimental.pallas import tpu as pltpu

D_MODEL = 1024
HEAD_DIM = 64
MLA_HEADS = 6
MLA_NOPE = 64
MLA_ROPE = 32
MLA_V = 64
MLA_Q_LORA = 256
MLA_KV_LORA = 128
NSA_HEADS = 6
NSA_KV_HEADS = 2
NSA_GROUP = NSA_HEADS // NSA_KV_HEADS
NSA_BRANCHES = 3
CMP_LEN = 32
CMP_STRIDE = 16
CMP_HIDDEN = 128
SEL_LEN = 64
SEL_TOPK = 16
WINDOW = 512
SB_HEADS = 4
D_FF = 2816
ROPE_THETA = 500000.0
PARTIAL_ROT = HEAD_DIM // 4
EPS = 1e-6
NEG_INF = -1e30
M_FLOOR = 0.1 * NEG_INF
FORCE_SCORE = 1e4
PICKED = -3e38
F32_EXP2_ZERO = -151.0
LOG2_E = math.log2(math.e)
N_GATES = NSA_HEADS * NSA_BRANCHES

LANES = 128
FFN_CHUNK = 256
SWEEP_UNROLL = 4
SB_KEY_BLOCK = 256
TOKEN_CHUNK = 256
ONES_PAD = 16
VMEM_LIMIT = 56 * 1024 * 1024

_MXU = jnp.bfloat16
_F32 = jnp.float32

_IN_WIDTHS = (MLA_Q_LORA, MLA_KV_LORA, MLA_ROPE, NSA_HEADS * HEAD_DIM) + (NSA_KV_HEADS * HEAD_DIM,) * 6 + (
    N_GATES, SB_HEADS * HEAD_DIM, SB_HEADS * HEAD_DIM, SB_HEADS * HEAD_DIM)
_IN_OFF = np.concatenate([[0], np.cumsum(_IN_WIDTHS)])
(_O_CQ, _O_CKV, _O_KR, _O_NQ, _O_NKC, _O_NVC, _O_NKS, _O_NVS, _O_NKW, _O_NVW, _O_GATE, _O_SBQ, _O_SBK,
 _O_SBV) = [int(v) for v in _IN_OFF[:-1]]

_S_CQ, _S_CKV, _S_KR, _S_KRS = 0, 2, 3, 4
_H_NQ, _H_NQS = 10, 16
_H_KC, _H_KCS, _H_VC = 22, 24, 26
_H_KS, _H_KSS = 28, 30
_H_KW, _H_KWS = 32, 34
_H_SBQ, _H_SBK = 36, 40
_N_HEAD_COLS = 44
_T_VS, _T_VW, _T_GATE = 0, NSA_KV_HEADS * HEAD_DIM, 2 * NSA_KV_HEADS * HEAD_DIM
GATE_ROWS = 24
_T_SBV = _T_GATE + 32
_T_ROWS = _T_SBV + SB_HEADS * HEAD_DIM


def _dot(a, b):
    return jnp.dot(a.astype(_MXU), b.astype(_MXU), preferred_element_type=_F32)


def _dot_nt(a, b):
    return lax.dot_general(a.astype(_MXU), b.astype(_MXU), (((1,), (1,)), ((), ())),
                           preferred_element_type=_F32)


def _dot_split_rhs(a, b):
    hi = b.astype(_MXU)
    lo = (b - hi.astype(_F32)).astype(_MXU)
    return (jnp.dot(a, hi, preferred_element_type=_F32) + jnp.dot(a, lo, preferred_element_type=_F32))


def _rms(x, g):
    return x * lax.rsqrt(jnp.mean(x * x, axis=-1, keepdims=True) + EPS) * g


def _params(*sem):
    return pltpu.CompilerParams(dimension_semantics=sem, vmem_limit_bytes=VMEM_LIMIT)


def _layer_spec(a, layer):
    return pl.BlockSpec((None,) + a.shape[1:], lambda *_: (layer,) + (0,) * (a.ndim - 1))


def _ffn_kernel(x_ref, g_ref, wg_ref, wu_ref, wd_ref, fg_ref, o_ref, h_ref, acc_ref, act_ref, *, final_norm):
    j = pl.program_id(1)

    @pl.when(j == 0)
    def _():
        h_ref[...] = _rms(x_ref[...], g_ref[...]).astype(h_ref.dtype)
        acc_ref[...] = jnp.zeros_like(acc_ref)

    h = h_ref[...]
    tf = act_ref.shape[1]
    for c0 in range(0, tf, FFN_CHUNK):
        c1 = min(c0 + FFN_CHUNK, tf)
        gate = jnp.dot(h, wg_ref[:, c0:c1], preferred_element_type=_F32)
        up = jnp.dot(h, wu_ref[:, c0:c1], preferred_element_type=_F32)
        act_ref[:, c0:c1] = (gate * jax.nn.sigmoid(gate) * up).astype(act_ref.dtype)
    acc_ref[...] += jnp.dot(act_ref[...], wd_ref[...], preferred_element_type=_F32)

    @pl.when(j == pl.num_programs(1) - 1)
    def _():
        y = x_ref[...] + 0.5 * acc_ref[...]
        if final_norm:
            y = _rms(y, fg_ref[...])
        o_ref[...] = y


def _ffn(x2d, g, wg, wu, wd, layer, fg, final_norm):
    rows = x2d.shape[0]
    tm = min(1024, rows)
    tf = D_FF // 2
    grid = (rows // tm, D_FF // tf)
    return pl.pallas_call(
        functools.partial(_ffn_kernel, final_norm=final_norm),
        grid=grid,
        in_specs=[
            pl.BlockSpec((tm, D_MODEL), lambda i, j: (i, 0)),
            _layer_spec(g, layer),
            pl.BlockSpec((None, D_MODEL, tf), lambda i, j: (layer, 0, j)),
            pl.BlockSpec((None, D_MODEL, tf), lambda i, j: (layer, 0, j)),
            pl.BlockSpec((None, tf, D_MODEL), lambda i, j: (layer, j, 0)),
            pl.BlockSpec((1, D_MODEL), lambda i, j: (0, 0)),
        ],
        out_specs=pl.BlockSpec((tm, D_MODEL), lambda i, j: (i, 0)),
        out_shape=jax.ShapeDtypeStruct((rows, D_MODEL), _F32),
        scratch_shapes=[pltpu.VMEM((tm, D_MODEL), _MXU), pltpu.VMEM((tm, D_MODEL), _F32), pltpu.VMEM((tm, tf), _MXU)],
        compiler_params=_params("parallel", "arbitrary"),
        name="ffn",
    )(x2d, g, wg, wu, wd, fg)


def _proj_kernel(x_ref, g_ref, w_ref, wt_ref, qn_ref, wuq_ref, wuqs_ref, kvn_ref, wuk_ref, wuv_ref, gb_ref,
                 cq_ref, sq_ref, ck_ref, sk_ref, c64_ref, s64_ref, oh_ref,
                 mq_ref, mk_ref, mv_ref, nq_ref, nkc_ref, nvc_ref, nks_ref, nvs_ref, nkw_ref, nvw_ref,
                 gate_ref, sbq_ref, sbk_ref, sbv_ref, stage_ref):
    hn = _rms(x_ref[0], g_ref[...]).astype(_MXU)

    def proj(h0, h1):
        return jnp.dot(hn, w_ref[:, h0 * HEAD_DIM:h1 * HEAD_DIM], preferred_element_type=_F32)

    def slot(p, s):
        return p[:, s * LANES:(s + 1) * LANES]

    def head(p, i):
        return p[:, i * HEAD_DIM:(i + 1) * HEAD_DIM]

    p = proj(0, _H_NQ)
    cq = _rms(p[:, :MLA_Q_LORA], qn_ref[...])
    ckv = _rms(slot(p, _S_CKV), kvn_ref[...])
    q = _dot(cq, wuq_ref[...])
    q_partner = _dot(cq, wuqs_ref[...])
    kpe = slot(p, _S_KR) * ck_ref[...] + slot(p, _S_KRS) * sk_ref[...]
    kn = _dot(ckv, wuk_ref[...])
    for h in range(MLA_HEADS):
        mq_ref[0, h] = (slot(q, h) * cq_ref[...] + slot(q_partner, h) * sq_ref[...]).astype(mq_ref.dtype)
        mk_ref[0, h] = (slot(kn, h) + kpe).astype(mk_ref.dtype)
        mv_ref[0, h, 0] = _ones_row_pad(_dot_nt(wuv_ref[h], ckv)).astype(mv_ref.dtype)

    c64 = c64_ref[...]
    s64 = s64_ref[...]
    scale = HEAD_DIM ** -0.5

    p = proj(_H_NQ, _H_KC)
    for h in range(NSA_HEADS):
        nq_ref[0, h] = ((head(p, h) * c64 + head(p, NSA_HEADS + h) * s64) * (scale * LOG2_E)).astype(nq_ref.dtype)

    p = proj(_H_KC, _H_SBQ)
    base = _H_KC
    ns = oh_ref.shape[-1]

    def roped(hk, hks, g):
        return head(p, hk - base + g) * c64 + head(p, hks - base + g) * s64

    def write_chunked(o_ref, g, val):
        stage_ref[...] = val
        for t in range(CMP_STRIDE):
            piece = stage_ref[pl.ds(t, val.shape[0] // CMP_STRIDE, stride=CMP_STRIDE), :]
            o_ref[0, g, :, t * HEAD_DIM:(t + 1) * HEAD_DIM] = piece.astype(o_ref.dtype)

    for g in range(NSA_KV_HEADS):
        write_chunked(nkc_ref, g, roped(_H_KC, _H_KCS, g))
        write_chunked(nvc_ref, g, head(p, _H_VC - base + g))
        nks_ref[0, g, :, 0:ns] = oh_ref[...]
        nks_ref[0, g, :, ns:ns + HEAD_DIM] = roped(_H_KS, _H_KSS, g).astype(nks_ref.dtype)
        nkw_ref[0, g] = roped(_H_KW, _H_KWS, g).astype(nkw_ref.dtype)

    pt = _dot_nt(wt_ref[...], hn)
    for g in range(NSA_KV_HEADS):
        lo = g * HEAD_DIM
        nvs_ref[0, g, 0] = _ones_row_pad(pt[_T_VS + lo:_T_VS + lo + HEAD_DIM]).astype(nvs_ref.dtype)
        nvw_ref[0, g, 0] = _ones_row_pad(pt[_T_VW + lo:_T_VW + lo + HEAD_DIM]).astype(nvw_ref.dtype)
    gate_ref[0] = jax.nn.sigmoid(pt[_T_GATE:_T_GATE + GATE_ROWS] + gb_ref[...])

    p = proj(_H_SBQ, _N_HEAD_COLS)
    for h in range(SB_HEADS):
        sbq_ref[0, h] = (head(p, h) * (scale * LOG2_E)).astype(sbq_ref.dtype)
        sbk_ref[0, h] = head(p, SB_HEADS + h).astype(sbk_ref.dtype)
        for c in range(sbv_ref.shape[2]):
            sbv_ref[0, h, c] = pt[_T_SBV + h * HEAD_DIM:_T_SBV + (h + 1) * HEAD_DIM,
                                  c * SB_KEY_BLOCK:(c + 1) * SB_KEY_BLOCK].astype(sbv_ref.dtype)


def _proj(x, layer, g, w_ext, w_t, qn, wuq, wuqs, kvn, wuk, wuv, gb, tabs):
    b, s, _ = x.shape
    ts = min(TOKEN_CHUNK, s)
    cq, sq, ck, sk, c64, s64, onehot = tabs
    full = functools.partial(_layer_spec, layer=layer)

    def tab(a):
        return pl.BlockSpec((ts, a.shape[1]), lambda bi, i: (i, 0))

    def heads(n, d):
        return (pl.BlockSpec((1, n, ts, d), lambda bi, i: (bi, 0, i, 0)),
                jax.ShapeDtypeStruct((b, n, s, d), _MXU))

    def values_t(n):
        return (pl.BlockSpec((1, n, 1, HEAD_DIM + ONES_PAD, ts), lambda bi, i: (bi, 0, i, 0, 0)),
                jax.ShapeDtypeStruct((b, n, s // ts, HEAD_DIM + ONES_PAD, ts), _MXU))

    def chunked():
        return (pl.BlockSpec((1, NSA_KV_HEADS, ts // CMP_STRIDE, CMP_STRIDE * HEAD_DIM), lambda bi, i: (bi, 0, i, 0)),
                jax.ShapeDtypeStruct((b, NSA_KV_HEADS, s // CMP_STRIDE, CMP_STRIDE * HEAD_DIM), _MXU))

    outs = [heads(MLA_HEADS, LANES), heads(MLA_HEADS, LANES), values_t(MLA_HEADS), heads(NSA_HEADS, HEAD_DIM),
            chunked(), chunked(), heads(NSA_KV_HEADS, onehot.shape[1] + HEAD_DIM), values_t(NSA_KV_HEADS),
            heads(NSA_KV_HEADS, HEAD_DIM), values_t(NSA_KV_HEADS),
            (pl.BlockSpec((1, GATE_ROWS, ts), lambda bi, i: (bi, 0, i)), jax.ShapeDtypeStruct((b, GATE_ROWS, s), _F32)),
            heads(SB_HEADS, HEAD_DIM), heads(SB_HEADS, HEAD_DIM),
            (pl.BlockSpec((1, SB_HEADS, ts // SB_KEY_BLOCK, HEAD_DIM, SB_KEY_BLOCK), lambda bi, i: (bi, 0, i, 0, 0)),
             jax.ShapeDtypeStruct((b, SB_HEADS, s // SB_KEY_BLOCK, HEAD_DIM, SB_KEY_BLOCK), _MXU))]
    return pl.pallas_call(
        _proj_kernel,
        grid=(b, s // ts),
        in_specs=[pl.BlockSpec((1, ts, D_MODEL), lambda bi, i: (bi, i, 0)), full(g), full(w_ext), full(w_t), full(qn),
                  full(wuq), full(wuqs), full(kvn), full(wuk), full(wuv), full(gb),
                  tab(cq), tab(sq), tab(ck), tab(sk), tab(c64), tab(s64), tab(onehot)],
        out_specs=[o[0] for o in outs],
        out_shape=[o[1] for o in outs],
        scratch_shapes=[pltpu.VMEM((ts, HEAD_DIM), _F32)],
        compiler_params=_params("parallel", "parallel"),
        name="proj",
    )(x, g, w_ext, w_t, qn, wuq, wuqs, kvn, wuk, wuv, gb, cq, sq, ck, sk, c64, s64, onehot)


def _ones_row_pad(vt):
    first = lax.broadcasted_iota(jnp.int32, (ONES_PAD, vt.shape[1]), 0) == 0
    return jnp.concatenate([vt, jnp.where(first, 1.0, 0.0).astype(vt.dtype)], axis=0)


def _softmax_step_t(carry, st, vt_chunks):
    m, acc = carry
    m_new = jnp.maximum(m, jnp.max(st, axis=0, keepdims=True))
    alpha = jnp.exp2(m - m_new)
    pt = jnp.exp2(st - m_new).astype(_MXU)
    n = st.shape[0] // len(vt_chunks)
    pv = sum(jnp.dot(vt, pt[c * n:(c + 1) * n], preferred_element_type=_F32) for c, vt in enumerate(vt_chunks))
    return m_new, alpha * acc + pv


def _softmax_init_t(d, cols):
    return (jnp.full((1, cols), M_FLOOR, _F32), jnp.zeros((d + ONES_PAD, cols), _F32))


def _softmax_finish_t(carry, d):
    _, acc = carry
    return acc[:d] * (1.0 / acc[d:d + 1])


def _two_chain_sweep(n_full, qk, soft, init):
    def body(j, carry, diag=False):
        c0, c1 = carry
        qk(0, j)
        c1 = soft(1, j, c1, diag)
        qk(1, jnp.zeros_like(j) if diag else j + 1)
        c0 = soft(0, j, c0, diag)
        return c0, c1

    def unrolled(i, carry):
        for u in range(SWEEP_UNROLL):
            carry = body(SWEEP_UNROLL * i + u, carry)
        return carry

    qk(1, n_full)
    carry = body(n_full, init, True)
    trips = n_full // SWEEP_UNROLL
    carry = lax.fori_loop(0, trips, unrolled, carry)
    return lax.fori_loop(SWEEP_UNROLL * trips, n_full, body, carry)


def _mla_kernel(q_ref, k_ref, vt_ref, o_ref, s0_ref, s1_ref, *, t, nsub):
    qi = pl.program_id(2)
    s_refs = (s0_ref, s1_ref)

    def qk(hh, j):
        off = pl.multiple_of(j * t, t)
        s_refs[hh][...] = _dot_nt(k_ref[0, hh, pl.ds(off, t), :], q_ref[0, hh])

    def soft(hh, j, carry, diag):
        st = s_refs[hh][...]
        if diag:
            key = lax.broadcasted_iota(jnp.int32, (t, t), 0)
            qry = lax.broadcasted_iota(jnp.int32, (t, t), 1)
            st = jnp.where(key <= qry, st, NEG_INF)
        return _softmax_step_t(carry, st, [vt_ref[0, hh, j * nsub + c] for c in range(nsub)])

    carry = _two_chain_sweep(qi, qk, soft, tuple(_softmax_init_t(MLA_V, t) for _ in range(2)))
    ot = jnp.concatenate([_softmax_finish_t(c, MLA_V) for c in carry], axis=0)
    o_ref[0] = ot.T.astype(o_ref.dtype)


def _mla_attention(q, k, vt):
    b, h, s, _ = q.shape
    tv = vt.shape[-1]
    dv = vt.shape[-2]
    t = min(512, s)
    assert h % 2 == 0 and 2 * MLA_V == LANES and t % tv == 0 and s % t == 0
    return pl.pallas_call(
        functools.partial(_mla_kernel, t=t, nsub=t // tv),
        grid=(b, h // 2, s // t),
        in_specs=[pl.BlockSpec((1, 2, t, LANES), lambda bi, hi, i: (bi, hi, i, 0)),
                  pl.BlockSpec((1, 2, s, LANES), lambda bi, hi, i: (bi, hi, 0, 0)),
                  pl.BlockSpec((1, 2, s // tv, dv, tv), lambda bi, hi, i: (bi, hi, 0, 0, 0))],
        out_specs=pl.BlockSpec((1, t, LANES), lambda bi, hi, i: (bi, i, hi)),
        out_shape=jax.ShapeDtypeStruct((b, s, h * MLA_V), _MXU),
        scratch_shapes=[pltpu.VMEM((t, t), _F32), pltpu.VMEM((t, t), _F32)],
        compiler_params=_params("parallel", "parallel", "arbitrary"),
        name="mla_attn",
    )(q, k, vt)


def _sb_kernel(q_ref, k_ref, vt_ref, u_ref, o_ref, *scratch_refs, tq, tk):
    qi = pl.program_id(1)
    u = u_ref[...]
    n_heads = q_ref.shape[1]
    scratch = [scratch_refs[6 * hh:6 * (hh + 1)] for hh in range(n_heads)]
    per_tile = tq // tk

    def step(j, carry, key_offset=None):
        diag = key_offset is not None
        off = pl.multiple_of(j * tk, tk)
        if diag:
            key = key_offset + lax.broadcasted_iota(jnp.int32, (tk, tq), 0)
            qry = lax.broadcasted_iota(jnp.int32, (tk, tq), 1)
            strict = key < qry

        def logits(hh):
            z_ref, _, _, _, _, _ = scratch[hh]
            z_ref[...] = _dot_nt(k_ref[0, hh, pl.ds(off, tk), :], q_ref[0, hh])

        def log_terms(hh):
            z_ref, lb_ref, hi_ref, lo_ref, _, _ = scratch[hh]
            z = z_ref[...]
            log_beta = jnp.minimum(z, 0.0) - jnp.log2(1.0 + jnp.exp2(-jnp.abs(z)))
            log_rem = log_beta - z
            if diag:
                log_rem = jnp.where(strict, log_rem, 0.0)
            hi = log_rem.astype(_MXU)
            lb_ref[...] = log_beta
            hi_ref[...] = hi
            lo_ref[...] = (log_rem - hi.astype(_F32)).astype(_MXU)
            return log_rem[0:1, :]

        def suffix_sums(hh):
            _, _, hi_ref, lo_ref, sfx_ref, _ = scratch[hh]
            sfx_ref[...] = (jnp.dot(u, hi_ref[...], preferred_element_type=_F32)
                            + jnp.dot(u, lo_ref[...], preferred_element_type=_F32))

        def weights(hh, first_rem):
            _, lb_ref, _, _, sfx_ref, a_ref = scratch[hh]
            rem = carry[hh][0]
            suffix = sfx_ref[...]
            a = jnp.exp2(lb_ref[...] + suffix + rem)
            if diag:
                a = jnp.where(strict, a, 0.0)
            a_ref[...] = a.astype(_MXU)
            return rem + suffix[0:1, :] + first_rem

        def values(hh):
            a_ref = scratch[hh][5]
            return carry[hh][1] + jnp.dot(vt_ref[0, hh, j], a_ref[...], preferred_element_type=_F32)

        heads = range(n_heads)
        for hh in heads:
            logits(hh)
        first = []
        for hh in heads:
            first.append(log_terms(hh))
            suffix_sums(hh)
        rems = [weights(hh, first[hh]) for hh in heads]
        alive = jnp.max(functools.reduce(jnp.maximum, rems)) > F32_EXP2_ZERO
        return alive, tuple((rems[hh], values(hh)) for hh in heads)

    carry = tuple((jnp.zeros((1, tq), _F32), jnp.zeros((HEAD_DIM, tq), _F32)) for _ in range(n_heads))
    first = qi * per_tile
    for i in reversed(range(per_tile)):
        _, carry = step(first + i, carry, key_offset=i * tk)
    has_past = first > 0
    carry = tuple((jnp.where(has_past, rem, NEG_INF), acc) for rem, acc in carry)
    alive, carry = step(jnp.maximum(first - 1, 0), carry)

    def earlier(c):
        return (c[0] - 1,) + step(c[0], c[2])

    _, _, carry = lax.while_loop(lambda c: jnp.logical_and(c[0] >= 0, c[1]), earlier, (first - 2, alive, carry))
    o_ref[0] = jnp.concatenate([acc for _, acc in carry], axis=0).T.astype(o_ref.dtype)


def _sb_attention(q, k, vt):
    b, h, s, d = q.shape
    tk = vt.shape[-1]
    tq = tk
    assert (h * d) % LANES == 0 and s % tq == 0 and tq % tk == 0
    idx = np.arange(tk)
    u = jnp.asarray(idx[None, :] > idx[:, None], _MXU)
    return pl.pallas_call(
        functools.partial(_sb_kernel, tq=tq, tk=tk),
        grid=(b, s // tq),
        in_specs=[pl.BlockSpec((1, h, tq, d), lambda bi, i: (bi, 0, i, 0)),
                  pl.BlockSpec((1, h, s, d), lambda bi, i: (bi, 0, 0, 0)),
                  pl.BlockSpec((1, h, s // tk, d, tk), lambda bi, i: (bi, 0, 0, 0, 0)),
                  pl.BlockSpec((tk, tk), lambda bi, i: (0, 0))],
        out_specs=pl.BlockSpec((1, tq, h * d), lambda bi, i: (bi, i, 0)),
        out_shape=jax.ShapeDtypeStruct((b, s, h * d), _MXU),
        scratch_shapes=[pltpu.VMEM((tk, tq), dt) for _ in range(h) for dt in (_F32, _F32, _MXU, _MXU, _F32, _MXU)],
        compiler_params=_params("parallel", "arbitrary"),
        name="sb_attn",
    )(q, k, vt, u)


def _compress_kernel(xk_ref, xv_ref, w1k_ref, w2k_ref, pk_ref, w1v_ref, w2v_ref, pv_ref, ok_ref, ov_ref):
    def hidden(x_ref, w1_ref, p_ref):
        x = x_ref[0, 0]
        n = x.shape[0]
        first = jnp.dot(x, w1_ref[0], preferred_element_type=_F32)
        second = jnp.dot(x, w1_ref[1], preferred_element_type=_F32)
        pos = _dot(p_ref[0], w1_ref[0]) + _dot(p_ref[1], w1_ref[1])
        hid = first + pltpu.roll(second, n - 1, 0) + pos[0:1]
        return 0.5 * hid * (1.0 + jnp.tanh(math.sqrt(2.0 / math.pi) * (hid + 0.044715 * hid * hid * hid)))

    ok_ref[0, 0] = _dot(hidden(xk_ref, w1k_ref, pk_ref), w2k_ref[...]).astype(ok_ref.dtype)
    ov_ref[0, 0] = _dot_nt(w2v_ref[...], hidden(xv_ref, w1v_ref, pv_ref)).astype(ov_ref.dtype)


def _compress(xk, xv, layer, w1k, w2k, pk, w1v, w2v, pv):
    b, g, n, _ = xk.shape
    d = HEAD_DIM
    full = functools.partial(_layer_spec, layer=layer)

    xspec = pl.BlockSpec((1, 1, n, CMP_STRIDE * d), lambda bi, gi: (bi, gi, 0, 0))
    return pl.pallas_call(
        _compress_kernel,
        grid=(b, g),
        in_specs=[xspec, xspec, full(w1k), full(w2k), full(pk), full(w1v), full(w2v), full(pv)],
        out_specs=[pl.BlockSpec((1, 1, n, d), lambda bi, gi: (bi, gi, 0, 0)),
                   pl.BlockSpec((1, 1, d, n), lambda bi, gi: (bi, gi, 0, 0))],
        out_shape=[jax.ShapeDtypeStruct((b, g, n, d), _MXU), jax.ShapeDtypeStruct((b, g, d, n), _MXU)],
        compiler_params=_params("parallel", "parallel"),
        name="nsa_compress",
    )(xk, xv, w1k, w2k, pk, w1v, w2v, pv)


def _group_queries(q_ref, g, tq):
    return q_ref[0, g * NSA_GROUP:(g + 1) * NSA_GROUP].reshape(NSA_GROUP * tq, q_ref.shape[-1])


def _gated_heads(ot, gt_ref, g, branch, tq):
    out = []
    for r in range(NSA_GROUP):
        row = NSA_BRANCHES * (g * NSA_GROUP + r) + branch
        out.append(ot[:, r * tq:(r + 1) * tq] * gt_ref[0, row:row + 1, :])
    return out


def _cmp_kernel(q_ref, kc_ref, vct_ref, ov_ref, gt_ref, o_ref, qa_ref, s0_ref, s1_ref, *, tq, n_top):
    q0 = pl.program_id(1) * tq
    ncp = kc_ref.shape[2]
    ns = ov_ref.shape[0]
    lanes = NSA_GROUP * tq
    s_refs = (s0_ref, s1_ref)
    for g in range(NSA_KV_HEADS):
        s_refs[g][...] = _dot_nt(kc_ref[0, g], _group_queries(q_ref, g, tq))
    qpos = q0 + (lax.broadcasted_iota(jnp.int32, (1, lanes), 1) & (tq - 1))
    cmp_end = lax.broadcasted_iota(jnp.int32, (ncp, 1), 0) * CMP_STRIDE + (CMP_LEN - 1)
    visible = cmp_end <= qpos
    cur = jnp.right_shift(q0 + lax.broadcasted_iota(jnp.int32, (1, tq), 1), int(math.log2(SEL_LEN)))
    blk = lax.broadcasted_iota(jnp.int32, (ns, 1), 0)
    forced = (blk == 0) | (blk == cur) | (blk == cur - 1)
    future = blk > cur
    blk_f = blk.astype(_F32)
    heads = []
    scores = []
    for g in range(NSA_KV_HEADS):
        st = jnp.where(visible, s_refs[g][...], NEG_INF)
        e = jnp.exp2(st - jnp.max(st, axis=0, keepdims=True))
        inv = jnp.where(qpos >= CMP_LEN - 1, 1.0 / jnp.sum(e, axis=0, keepdims=True), 0.0)
        pt = e * inv
        heads += _gated_heads(_dot(vct_ref[0, g], pt), gt_ref, g, 0, tq)
        p_sum = sum(pt[:, r * tq:(r + 1) * tq] for r in range(NSA_GROUP))
        score = _dot_split_rhs(ov_ref[...], p_sum)
        scores.append(jnp.where(forced, FORCE_SCORE, jnp.where(future, -1.0, score)))
    o_ref[0] = jnp.concatenate(heads, axis=0).T.astype(o_ref.dtype)
    def select(rows):
        sc = [s[:rows] for s in scores]
        idx = blk_f[:rows]
        for _ in range(n_top):
            for g in range(NSA_KV_HEADS):
                top = jnp.max(sc[g], axis=0, keepdims=True)
                first = jnp.min(jnp.where(sc[g] == top, idx, float(ns)), axis=0, keepdims=True)
                sc[g] = jnp.where(idx == first, PICKED, sc[g])
        for g in range(NSA_KV_HEADS):
            sel_m1 = jnp.where(sc[g] < 0.5 * PICKED, 0.0, -1.0)
            if rows < ns:
                sel_m1 = jnp.concatenate([sel_m1, jnp.full((ns - rows, tq), -1.0, _F32)], axis=0)
            sel_m1 = sel_m1.T.astype(qa_ref.dtype)
            for h in range(g * NSA_GROUP, (g + 1) * NSA_GROUP):
                qa_ref[0, h, :, 0:ns] = sel_m1
                qa_ref[0, h, :, ns:ns + HEAD_DIM] = q_ref[0, h]

    visible_blocks = (q0 + tq) // SEL_LEN
    lower = 0
    for rows in sorted({min(ns, 32), min(ns, 64), ns}):
        in_range = visible_blocks > lower
        if rows < ns:
            in_range = jnp.logical_and(in_range, visible_blocks <= rows)
        pl.when(in_range)(functools.partial(select, rows))
        lower = rows


def _cmp_select(q, kc, vct, gates_t):
    b, h, s, d = q.shape
    g = kc.shape[1]
    ncp = kc.shape[2]
    ns = s // SEL_LEN
    n_top = min(SEL_TOPK, ns)
    tq = min(256, s)
    assert tq & (tq - 1) == 0 and g == 2
    c0 = np.arange(ncp)[:, None] * CMP_STRIDE
    n0 = np.arange(ns)[None, :] * SEL_LEN
    overlap = jnp.asarray(((c0 < n0 + SEL_LEN) & (c0 + CMP_LEN > n0)).T, _MXU)
    return pl.pallas_call(
        functools.partial(_cmp_kernel, tq=tq, n_top=n_top),
        grid=(b, s // tq),
        in_specs=[pl.BlockSpec((1, h, tq, d), lambda bi, i: (bi, 0, i, 0)),
                  pl.BlockSpec((1, g, ncp, d), lambda bi, i: (bi, 0, 0, 0)),
                  pl.BlockSpec((1, g, d, ncp), lambda bi, i: (bi, 0, 0, 0)),
                  pl.BlockSpec((ns, ncp), lambda bi, i: (0, 0)),
                  pl.BlockSpec((1, GATE_ROWS, tq), lambda bi, i: (bi, 0, i))],
        out_specs=[pl.BlockSpec((1, tq, h * d), lambda bi, i: (bi, i, 0)),
                   pl.BlockSpec((1, h, tq, ns + d), lambda bi, i: (bi, 0, i, 0))],
        out_shape=[jax.ShapeDtypeStruct((b, s, h * d), _MXU), jax.ShapeDtypeStruct((b, h, s, ns + d), _MXU)],
        scratch_shapes=[pltpu.VMEM((ncp, NSA_GROUP * tq), _F32) for _ in range(g)],
        compiler_params=_params("parallel", "arbitrary"),
        name="nsa_cmp_select",
    )(q, kc, vct, overlap, gates_t)


def _key_minus_query(keys, tq):
    return jnp.asarray(np.arange(keys)[:, None] - np.arange(NSA_GROUP * tq)[None, :] % tq, jnp.int32)


def _sel_kernel(q_ref, k_ref, vt_ref, gt_ref, rel_ref, o_ref, s0_ref, s1_ref, *, tq, tk, nsub):
    q0 = pl.program_id(1) * tq
    last = (q0 + tq - 1) // tk
    lanes = NSA_GROUP * tq
    s_refs = (s0_ref, s1_ref)

    def qk(g, j):
        off = pl.multiple_of(j * tk, tk)
        s_refs[g][...] = _dot_nt(k_ref[0, g, pl.ds(off, tk), :], _group_queries(q_ref, g, tq))

    def soft(g, j, carry, causal):
        st = s_refs[g][...]
        if causal:
            st = jnp.where(rel_ref[...] <= q0 - j * tk, st, NEG_INF)
        return _softmax_step_t(carry, st, [vt_ref[0, g, j * nsub + c] for c in range(nsub)])

    init = tuple(_softmax_init_t(HEAD_DIM, lanes) for _ in range(NSA_KV_HEADS))
    carry = _two_chain_sweep(last, qk, soft, init)
    heads = []
    for g in range(NSA_KV_HEADS):
        heads += _gated_heads(_softmax_finish_t(carry[g], HEAD_DIM), gt_ref, g, 1, tq)
    o_ref[0] = jnp.concatenate(heads, axis=0).T.astype(o_ref.dtype)


def _sel_attention(q, k, vt, gates_t):
    b, h, s, da = q.shape
    g = k.shape[1]
    d = HEAD_DIM
    tv = vt.shape[-1]
    tq = min(256, s)
    tk = min(512, s)
    assert tq & (tq - 1) == 0 and s % tk == 0 and tk % tv == 0 and g == 2
    return pl.pallas_call(
        functools.partial(_sel_kernel, tq=tq, tk=tk, nsub=tk // tv),
        grid=(b, s // tq),
        in_specs=[pl.BlockSpec((1, h, tq, da), lambda bi, i: (bi, 0, i, 0)),
                  pl.BlockSpec((1, g, s, da), lambda bi, i: (bi, 0, 0, 0)),
                  pl.BlockSpec((1, g) + vt.shape[2:], lambda bi, i: (bi, 0, 0, 0, 0)),
                  pl.BlockSpec((1, GATE_ROWS, tq), lambda bi, i: (bi, 0, i)),
                  pl.BlockSpec((tk, NSA_GROUP * tq), lambda bi, i: (0, 0))],
        out_specs=pl.BlockSpec((1, tq, h * d), lambda bi, i: (bi, i, 0)),
        out_shape=jax.ShapeDtypeStruct((b, s, h * d), _MXU),
        scratch_shapes=[pltpu.VMEM((tk, NSA_GROUP * tq), _F32) for _ in range(g)],
        compiler_params=_params("parallel", "arbitrary"),
        name="nsa_selected",
    )(q, k, vt, gates_t, _key_minus_query(tk, tq))


def _win_kernel(q_ref, k_ref, vt_ref, gt_ref, rel_ref, o_ref, s0_ref, s1_ref, *, tq, span, tv):
    q0 = pl.program_id(1) * tq
    start = pl.multiple_of(jnp.maximum(q0 - WINDOW, 0), tq)
    first_chunk = start // tv
    lanes = NSA_GROUP * tq
    s_refs = (s0_ref, s1_ref)
    for g in range(NSA_KV_HEADS):
        s_refs[g][...] = _dot_nt(k_ref[0, g, pl.ds(start, span), :], _group_queries(q_ref, g, tq))
    rel = rel_ref[...]
    offset = q0 - start
    heads = []
    for g in range(NSA_KV_HEADS):
        st = jnp.where(rel <= offset, s_refs[g][...], NEG_INF)
        st = jnp.where(rel > offset - WINDOW, st, NEG_INF)
        carry = _softmax_step_t(_softmax_init_t(HEAD_DIM, lanes), st,
                                [vt_ref[0, g, first_chunk + c] for c in range(span // tv)])
        heads += _gated_heads(_softmax_finish_t(carry, HEAD_DIM), gt_ref, g, 2, tq)
    o_ref[0] = jnp.concatenate(heads, axis=0).T.astype(o_ref.dtype)


def _win_attention(q, k, vt, gates_t):
    b, h, s, d = q.shape
    g = k.shape[1]
    tv = vt.shape[-1]
    tq = min(256, s)
    span = WINDOW + tq
    assert tq & (tq - 1) == 0 and s >= span and tq % tv == 0 and WINDOW % tv == 0 and g == 2
    return pl.pallas_call(
        functools.partial(_win_kernel, tq=tq, span=span, tv=tv),
        grid=(b, s // tq),
        in_specs=[pl.BlockSpec((1, h, tq, d), lambda bi, i: (bi, 0, i, 0)),
                  pl.BlockSpec((1, g, s, d), lambda bi, i: (bi, 0, 0, 0)),
                  pl.BlockSpec((1, g) + vt.shape[2:], lambda bi, i: (bi, 0, 0, 0, 0)),
                  pl.BlockSpec((1, GATE_ROWS, tq), lambda bi, i: (bi, 0, i)),
                  pl.BlockSpec((span, NSA_GROUP * tq), lambda bi, i: (0, 0))],
        out_specs=pl.BlockSpec((1, tq, h * d), lambda bi, i: (bi, i, 0)),
        out_shape=jax.ShapeDtypeStruct((b, s, h * d), _MXU),
        scratch_shapes=[pltpu.VMEM((span, NSA_GROUP * tq), _F32) for _ in range(g)],
        compiler_params=_params("parallel", "arbitrary"),
        name="nsa_window",
    )(q, k, vt, gates_t, _key_minus_query(span, tq))


def _out_kernel(x_ref, mla_ref, cmp_ref, sel_ref, win_ref, sb_ref, w_ref, o_ref):
    def w_rows(first_head, n_heads):
        return w_ref[first_head * HEAD_DIM:(first_head + n_heads) * HEAD_DIM, :]

    acc = x_ref[0] + jnp.dot(mla_ref[0], w_rows(0, MLA_HEADS), preferred_element_type=_F32)
    nsa = cmp_ref[0].astype(_F32) + sel_ref[0].astype(_F32) + win_ref[0].astype(_F32)
    acc = acc + _dot(nsa, w_rows(MLA_HEADS, NSA_HEADS))
    o_ref[0] = acc + jnp.dot(sb_ref[0], w_rows(MLA_HEADS + NSA_HEADS, SB_HEADS), preferred_element_type=_F32)


def _out_proj(x, o_mla, o_cmp, o_sel, o_win, o_sb, layer, w_heads):
    b, s, _ = x.shape
    ts = min(512, s)

    def rows(a):
        return pl.BlockSpec((1, ts, a.shape[2]), lambda bi, i: (bi, i, 0))

    xspec = pl.BlockSpec((1, ts, D_MODEL), lambda bi, i: (bi, i, 0))
    return pl.pallas_call(
        _out_kernel,
        grid=(b, s // ts),
        in_specs=[xspec, rows(o_mla), rows(o_cmp), rows(o_sel), rows(o_win), rows(o_sb),
                  _layer_spec(w_heads, layer)],
        out_specs=xspec,
        out_shape=jax.ShapeDtypeStruct(x.shape, _F32),
        compiler_params=_params("parallel", "parallel"),
        name="out_proj",
    )(x, o_mla, o_cmp, o_sel, o_win, o_sb, w_heads)


def _gather_cols(w, idx):
    idx = np.asarray(idx)
    cuts = [0] + [i for i in range(1, len(idx)) if idx[i] != idx[i - 1] + (idx[i - 1] >= 0)] + [len(idx)]
    pieces = []
    for a, b in zip(cuts[:-1], cuts[1:]):
        if idx[a] < 0:
            pieces.append(jnp.zeros(w.shape[:-1] + (b - a,), _MXU))
        else:
            pieces.append(w[..., int(idx[a]):int(idx[a]) + b - a].astype(_MXU))
    return jnp.concatenate(pieces, axis=-1)


def _swap_halves(rot):
    return (np.arange(rot) + rot // 2) % rot


def _w_in_index():
    idx = np.full((_N_HEAD_COLS * HEAD_DIM,), -1, np.int64)

    def put(col, src):
        src = np.asarray(src)
        idx[col:col + len(src)] = src

    def put_head(pos, src):
        put(pos * HEAD_DIM, src)

    put(_S_CQ * LANES, _O_CQ + np.arange(MLA_Q_LORA))
    put(_S_CKV * LANES, _O_CKV + np.arange(MLA_KV_LORA))
    put(_S_KR * LANES + MLA_NOPE, _O_KR + np.arange(MLA_ROPE))
    put(_S_KRS * LANES + MLA_NOPE, _O_KR + _swap_halves(MLA_ROPE))
    for h in range(NSA_HEADS):
        put_head(_H_NQ + h, _O_NQ + h * HEAD_DIM + np.arange(HEAD_DIM))
        put_head(_H_NQS + h, _O_NQ + h * HEAD_DIM + _swap_halves(PARTIAL_ROT))
    for hk, hks, ok in ((_H_KC, _H_KCS, _O_NKC), (_H_KS, _H_KSS, _O_NKS), (_H_KW, _H_KWS, _O_NKW)):
        for g in range(NSA_KV_HEADS):
            put_head(hk + g, ok + g * HEAD_DIM + np.arange(HEAD_DIM))
            put_head(hks + g, ok + g * HEAD_DIM + _swap_halves(PARTIAL_ROT))
    for g in range(NSA_KV_HEADS):
        put_head(_H_VC + g, _O_NVC + g * HEAD_DIM + np.arange(HEAD_DIM))
    for h in range(SB_HEADS):
        put_head(_H_SBQ + h, _O_SBQ + h * HEAD_DIM + np.arange(HEAD_DIM))
        put_head(_H_SBK + h, _O_SBK + h * HEAD_DIM + np.arange(HEAD_DIM))
    return idx


def _mla_up_index():
    qd = MLA_NOPE + MLA_ROPE
    kd = MLA_NOPE + MLA_V
    uq = np.full((MLA_HEADS * LANES,), -1, np.int64)
    uqs = uq.copy()
    uk = uq.copy()
    for h in range(MLA_HEADS):
        uq[h * LANES:h * LANES + qd] = h * qd + np.arange(qd)
        uqs[h * LANES + MLA_NOPE:h * LANES + qd] = h * qd + MLA_NOPE + _swap_halves(MLA_ROPE)
        uk[h * LANES:h * LANES + MLA_NOPE] = h * kd + np.arange(MLA_NOPE)
    return uq, uqs, uk


def _transposed_weights(w_in, gate_bias):
    width = NSA_KV_HEADS * HEAD_DIM
    gate_rows = jnp.pad(w_in[..., _O_GATE:_O_GATE + N_GATES], ((0, 0), (0, 0), (0, _T_SBV - _T_GATE - N_GATES)))
    rows = jnp.concatenate([w_in[..., _O_NVS:_O_NVS + width], w_in[..., _O_NVW:_O_NVW + width], gate_rows,
                            w_in[..., _O_SBV:_O_SBV + SB_HEADS * HEAD_DIM]], axis=-1)
    bias = jnp.pad(gate_bias, ((0, 0), (0, GATE_ROWS - N_GATES)))[..., None]
    return jnp.swapaxes(rows, -1, -2).astype(_MXU), bias


def _rope_tables(s):
    pos = np.arange(s, dtype=np.float32)

    def cs(rot):
        half = rot // 2
        inv_freq = np.float32(ROPE_THETA) ** (-np.arange(half, dtype=np.float32) / np.float32(half))
        ang = (pos[:, None] * inv_freq[None, :].astype(np.float32)).astype(np.float32)
        c, sn = np.cos(ang.astype(np.float64)), np.sin(ang.astype(np.float64))
        return np.concatenate([c, c], axis=1), np.concatenate([-sn, sn], axis=1)

    c, sn = cs(MLA_ROPE)
    pad = np.zeros((s, LANES - MLA_NOPE - MLA_ROPE))
    ck = np.concatenate([np.ones((s, MLA_NOPE)), c, pad], axis=1)
    sk = np.concatenate([np.zeros((s, MLA_NOPE)), sn, pad], axis=1)
    q_scale = (MLA_NOPE + MLA_ROPE) ** -0.5 * LOG2_E
    c, sn = cs(PARTIAL_ROT)
    c64 = np.concatenate([c, np.ones((s, HEAD_DIM - PARTIAL_ROT))], axis=1)
    s64 = np.concatenate([sn, np.zeros((s, HEAD_DIM - PARTIAL_ROT))], axis=1)
    ns = s // SEL_LEN
    onehot = (np.arange(s)[:, None] // SEL_LEN == np.arange(ns)[None, :]) * -NEG_INF
    tables = [jnp.asarray(t, _F32) for t in (ck * q_scale, sk * q_scale, ck, sk, c64, s64)]
    return tables + [jnp.asarray(onehot, _MXU)]


def kernel(x, ffn1_norm, ffn1_w_gate, ffn1_w_up, ffn1_w_down, mix_norm, w_in, mla_q_norm, mla_w_uq, mla_kv_norm,
           mla_w_ukv, nsa_gate_bias, nsa_cmp_pos_k, nsa_cmp_w1_k, nsa_cmp_w2_k, nsa_cmp_pos_v, nsa_cmp_w1_v,
           nsa_cmp_w2_v, w_out, ffn2_norm, ffn2_w_gate, ffn2_w_up, ffn2_w_down, final_norm):
    b, s, d = x.shape
    depth = w_in.shape[0]
    tabs = _rope_tables(s)
    in_idx = _w_in_index()
    uq_idx, uqs_idx, uk_idx = _mla_up_index()
    half = CMP_LEN * HEAD_DIM // 2
    fg = final_norm.reshape(1, d)

    def row(p):
        return p[:, None, :]

    def cmp_weights(w1, w2, pos, transpose_out):
        pos = jnp.broadcast_to(pos.reshape(depth, 2, 1, half), (depth, 2, 8, half)).astype(_MXU)
        w2 = jnp.swapaxes(w2, -1, -2) if transpose_out else w2
        return w1.reshape(depth, 2, half, CMP_HIDDEN).astype(_MXU), w2.astype(_MXU), pos

    ffn1 = [w.astype(_MXU) for w in (ffn1_w_gate, ffn1_w_up, ffn1_w_down)]
    ffn2 = [w.astype(_MXU) for w in (ffn2_w_gate, ffn2_w_up, ffn2_w_down)]
    w_t, gate_bias = _transposed_weights(w_in, nsa_gate_bias)
    wuv_t = mla_w_ukv.reshape(depth, MLA_KV_LORA, MLA_HEADS, 2, MLA_V)[:, :, :, 1].transpose(0, 2, 3, 1).astype(_MXU)
    proj_params = (row(mix_norm), _gather_cols(w_in, in_idx), w_t,
                   row(mla_q_norm), _gather_cols(mla_w_uq, uq_idx), _gather_cols(mla_w_uq, uqs_idx),
                   row(mla_kv_norm), _gather_cols(mla_w_ukv, uk_idx), wuv_t, gate_bias)
    cmp_params = (cmp_weights(nsa_cmp_w1_k, nsa_cmp_w2_k, nsa_cmp_pos_k, False)
                  + cmp_weights(nsa_cmp_w1_v, nsa_cmp_w2_v, nsa_cmp_pos_v, True))
    w_out = w_out.astype(_MXU)
    ffn1_norm, ffn2_norm = row(ffn1_norm), row(ffn2_norm)

    for l in range(depth):
        x2d = _ffn(x.reshape(b * s, d), ffn1_norm, *ffn1, l, fg, False)
        x = x2d.reshape(b, s, d)
        (mq, mk, mvt, nq, nkc, nvc, nks, nvst, nkw, nvwt, gates_t, sbq, sbk, sbv) = _proj(x, l, *proj_params, tabs)
        o_mla = _mla_attention(mq, mk, mvt)
        kc, vct = _compress(nkc, nvc, l, *cmp_params)
        o_cmp, q_sel = _cmp_select(nq, kc, vct, gates_t)
        o_sel = _sel_attention(q_sel, nks, nvst, gates_t)
        o_win = _win_attention(nq, nkw, nvwt, gates_t)
        o_sb = _sb_attention(sbq, sbk, sbv)
        x = _out_proj(x, o_mla, o_cmp, o_sel, o_win, o_sb, l, w_out)
        x2d = _ffn(x.reshape(b * s, d), ffn2_norm, *ffn2, l, fg, l == depth - 1)
        x = x2d.reshape(b, s, d)
    return x
```

```python
import functools
import math

import numpy as np
import jax
import jax.numpy as jnp
from jax import lax
from jax.experimental import pallas as pl
from jax.experimental.pallas import tpu as pltpu

D_MODEL = 1024
HEAD_DIM = 64
MLA_HEADS = 6
MLA_NOPE = 64
MLA_ROPE = 32
MLA_V = 64
MLA_Q_LORA = 256
MLA_KV_LORA = 128
NSA_HEADS = 6
NSA_KV_HEADS = 2
NSA_GROUP = NSA_HEADS // NSA_KV_HEADS
NSA_BRANCHES = 3
CMP_LEN = 32
CMP_STRIDE = 16
CMP_HIDDEN = 128
SEL_LEN = 64
SEL_TOPK = 16
WINDOW = 512
SB_HEADS = 4
D_FF = 2816
ROPE_THETA = 500000.0
PARTIAL_ROT = HEAD_DIM // 4
EPS = 1e-6
NEG_INF = -1e30
M_FLOOR = 0.1 * NEG_INF
FORCE_SCORE = 1e4
PICKED = -3e38
F32_EXP2_ZERO = -151.0
LOG2_E = math.log2(math.e)
N_GATES = NSA_HEADS * NSA_BRANCHES

LANES = 128
FFN_CHUNK = 256
SWEEP_UNROLL = 4
SB_KEY_BLOCK = 256
TOKEN_CHUNK = 256
ONES_PAD = 16
VMEM_LIMIT = 56 * 1024 * 1024

_MXU = jnp.bfloat16
_F32 = jnp.float32

_IN_WIDTHS = (MLA_Q_LORA, MLA_KV_LORA, MLA_ROPE, NSA_HEADS * HEAD_DIM) + (NSA_KV_HEADS * HEAD_DIM,) * 6 + (
    N_GATES, SB_HEADS * HEAD_DIM, SB_HEADS * HEAD_DIM, SB_HEADS * HEAD_DIM)
_IN_OFF = np.concatenate([[0], np.cumsum(_IN_WIDTHS)])
(_O_CQ, _O_CKV, _O_KR, _O_NQ, _O_NKC, _O_NVC, _O_NKS, _O_NVS, _O_NKW, _O_NVW, _O_GATE, _O_SBQ, _O_SBK,
 _O_SBV) = [int(v) for v in _IN_OFF[:-1]]

_S_CQ, _S_CKV, _S_KR, _S_KRS = 0, 2, 3, 4
_H_NQ, _H_NQS = 10, 16
_H_KC, _H_KCS, _H_VC = 22, 24, 26
_H_KS, _H_KSS = 28, 30
_H_KW, _H_KWS = 32, 34
_H_SBQ, _H_SBK = 36, 40
_N_HEAD_COLS = 44
_T_VS, _T_VW, _T_GATE = 0, NSA_KV_HEADS * HEAD_DIM, 2 * NSA_KV_HEADS * HEAD_DIM
GATE_ROWS = 24
_T_SBV = _T_GATE + 32
_T_ROWS = _T_SBV + SB_HEADS * HEAD_DIM


def _dot(a, b):
    return jnp.dot(a.astype(_MXU), b.astype(_MXU), preferred_element_type=_F32)


def _dot_nt(a, b):
    return lax.dot_general(a.astype(_MXU), b.astype(_MXU), (((1,), (1,)), ((), ())),
                           preferred_element_type=_F32)


def _dot_split_rhs(a, b):
    hi = b.astype(_MXU)
    lo = (b - hi.astype(_F32)).astype(_MXU)
    return (jnp.dot(a, hi, preferred_element_type=_F32) + jnp.dot(a, lo, preferred_element_type=_F32))


def _rms(x, g):
    return x * lax.rsqrt(jnp.mean(x * x, axis=-1, keepdims=True) + EPS) * g


def _params(*sem):
    return pltpu.CompilerParams(dimension_semantics=sem, vmem_limit_bytes=VMEM_LIMIT)


def _layer_spec(a, layer):
    return pl.BlockSpec((None,) + a.shape[1:], lambda *_: (layer,) + (0,) * (a.ndim - 1))


def _ffn_kernel(x_ref, g_ref, wg_ref, wu_ref, wd_ref, fg_ref, o_ref, h_ref, acc_ref, act_ref, *, final_norm):
    j = pl.program_id(1)

    @pl.when(j == 0)
    def _():
        h_ref[...] = _rms(x_ref[...], g_ref[...]).astype(h_ref.dtype)
        acc_ref[...] = jnp.zeros_like(acc_ref)

    h = h_ref[...]
    tf = act_ref.shape[1]
    for c0 in range(0, tf, FFN_CHUNK):
        c1 = min(c0 + FFN_CHUNK, tf)
        gate = jnp.dot(h, wg_ref[:, c0:c1], preferred_element_type=_F32)
        up = jnp.dot(h, wu_ref[:, c0:c1], preferred_element_type=_F32)
        act_ref[:, c0:c1] = (gate * jax.nn.sigmoid(gate) * up).astype(act_ref.dtype)
    acc_ref[...] += jnp.dot(act_ref[...], wd_ref[...], preferred_element_type=_F32)

    @pl.when(j == pl.num_programs(1) - 1)
    def _():
        y = x_ref[...] + 0.5 * acc_ref[...]
        if final_norm:
            y = _rms(y, fg_ref[...])
        o_ref[...] = y


def _ffn(x2d, g, wg, wu, wd, layer, fg, final_norm):
    rows = x2d.shape[0]
    tm = min(1024, rows)
    tf = D_FF // 2
    grid = (rows // tm, D_FF // tf)
    return pl.pallas_call(
        functools.partial(_ffn_kernel, final_norm=final_norm),
        grid=grid,
        in_specs=[
            pl.BlockSpec((tm, D_MODEL), lambda i, j: (i, 0)),
            _layer_spec(g, layer),
            pl.BlockSpec((None, D_MODEL, tf), lambda i, j: (layer, 0, j)),
            pl.BlockSpec((None, D_MODEL, tf), lambda i, j: (layer, 0, j)),
            pl.BlockSpec((None, tf, D_MODEL), lambda i, j: (layer, j, 0)),
            pl.BlockSpec((1, D_MODEL), lambda i, j: (0, 0)),
        ],
        out_specs=pl.BlockSpec((tm, D_MODEL), lambda i, j: (i, 0)),
        out_shape=jax.ShapeDtypeStruct((rows, D_MODEL), _F32),
        scratch_shapes=[pltpu.VMEM((tm, D_MODEL), _MXU), pltpu.VMEM((tm, D_MODEL), _F32), pltpu.VMEM((tm, tf), _MXU)],
        compiler_params=_params("parallel", "arbitrary"),
        name="ffn",
    )(x2d, g, wg, wu, wd, fg)


def _proj_kernel(x_ref, g_ref, w_ref, wt_ref, qn_ref, wuq_ref, wuqs_ref, kvn_ref, wuk_ref, wuv_ref, gb_ref,
                 cq_ref, sq_ref, ck_ref, sk_ref, c64_ref, s64_ref, oh_ref,
                 mq_ref, mk_ref, mv_ref, nq_ref, nkc_ref, nvc_ref, nks_ref, nvs_ref, nkw_ref, nvw_ref,
                 gate_ref, sbq_ref, sbk_ref, sbv_ref, stage_ref):
    hn = _rms(x_ref[0], g_ref[...]).astype(_MXU)

    def proj(h0, h1):
        return jnp.dot(hn, w_ref[:, h0 * HEAD_DIM:h1 * HEAD_DIM], preferred_element_type=_F32)

    def slot(p, s):
        return p[:, s * LANES:(s + 1) * LANES]

    def head(p, i):
        return p[:, i * HEAD_DIM:(i + 1) * HEAD_DIM]

    p = proj(0, _H_NQ)
    cq = _rms(p[:, :MLA_Q_LORA], qn_ref[...])
    ckv = _rms(slot(p, _S_CKV), kvn_ref[...])
    q = _dot(cq, wuq_ref[...])
    q_partner = _dot(cq, wuqs_ref[...])
    kpe = slot(p, _S_KR) * ck_ref[...] + slot(p, _S_KRS) * sk_ref[...]
    kn = _dot(ckv, wuk_ref[...])
    for h in range(MLA_HEADS):
        mq_ref[0, h] = (slot(q, h) * cq_ref[...] + slot(q_partner, h) * sq_ref[...]).astype(mq_ref.dtype)
        mk_ref[0, h] = (slot(kn, h) + kpe).astype(mk_ref.dtype)
        mv_ref[0, h, 0] = _ones_row_pad(_dot_nt(wuv_ref[h], ckv)).astype(mv_ref.dtype)

    c64 = c64_ref[...]
    s64 = s64_ref[...]
    scale = HEAD_DIM ** -0.5

    p = proj(_H_NQ, _H_KC)
    for h in range(NSA_HEADS):
        nq_ref[0, h] = ((head(p, h) * c64 + head(p, NSA_HEADS + h) * s64) * (scale * LOG2_E)).astype(nq_ref.dtype)

    p = proj(_H_KC, _H_SBQ)
    base = _H_KC
    ns = oh_ref.shape[-1]

    def roped(hk, hks, g):
        return head(p, hk - base + g) * c64 + head(p, hks - base + g) * s64

    def write_chunked(o_ref, g, val):
        stage_ref[...] = val
        for t in range(CMP_STRIDE):
            piece = stage_ref[pl.ds(t, val.shape[0] // CMP_STRIDE, stride=CMP_STRIDE), :]
            o_ref[0, g, :, t * HEAD_DIM:(t + 1) * HEAD_DIM] = piece.astype(o_ref.dtype)

    for g in range(NSA_KV_HEADS):
        write_chunked(nkc_ref, g, roped(_H_KC, _H_KCS, g))
        write_chunked(nvc_ref, g, head(p, _H_VC - base + g))
        nks_ref[0, g, :, 0:ns] = oh_ref[...]
        nks_ref[0, g, :, ns:ns + HEAD_DIM] = roped(_H_KS, _H_KSS, g).astype(nks_ref.dtype)
        nkw_ref[0, g] = roped(_H_KW, _H_KWS, g).astype(nkw_ref.dtype)

    pt = _dot_nt(wt_ref[...], hn)
    for g in range(NSA_KV_HEADS):
        lo = g * HEAD_DIM
        nvs_ref[0, g, 0] = _ones_row_pad(pt[_T_VS + lo:_T_VS + lo + HEAD_DIM]).astype(nvs_ref.dtype)
        nvw_ref[0, g, 0] = _ones_row_pad(pt[_T_VW + lo:_T_VW + lo + HEAD_DIM]).astype(nvw_ref.dtype)
    gate_ref[0] = jax.nn.sigmoid(pt[_T_GATE:_T_GATE + GATE_ROWS] + gb_ref[...])

    p = proj(_H_SBQ, _N_HEAD_COLS)
    for h in range(SB_HEADS):
        sbq_ref[0, h] = (head(p, h) * (scale * LOG2_E)).astype(sbq_ref.dtype)
        sbk_ref[0, h] = head(p, SB_HEADS + h).astype(sbk_ref.dtype)
        for c in range(sbv_ref.shape[2]):
            sbv_ref[0, h, c] = pt[_T_SBV + h * HEAD_DIM:_T_SBV + (h + 1) * HEAD_DIM,
                                  c * SB_KEY_BLOCK:(c + 1) * SB_KEY_BLOCK].astype(sbv_ref.dtype)


def _proj(x, layer, g, w_ext, w_t, qn, wuq, wuqs, kvn, wuk, wuv, gb, tabs):
    b, s, _ = x.shape
    ts = min(TOKEN_CHUNK, s)
    cq, sq, ck, sk, c64, s64, onehot = tabs
    full = functools.partial(_layer_spec, layer=layer)

    def tab(a):
        return pl.BlockSpec((ts, a.shape[1]), lambda bi, i: (i, 0))

    def heads(n, d):
        return (pl.BlockSpec((1, n, ts, d), lambda bi, i: (bi, 0, i, 0)),
                jax.ShapeDtypeStruct((b, n, s, d), _MXU))

    def values_t(n):
        return (pl.BlockSpec((1, n, 1, HEAD_DIM + ONES_PAD, ts), lambda bi, i: (bi, 0, i, 0, 0)),
                jax.ShapeDtypeStruct((b, n, s // ts, HEAD_DIM + ONES_PAD, ts), _MXU))

    def chunked():
        return (pl.BlockSpec((1, NSA_KV_HEADS, ts // CMP_STRIDE, CMP_STRIDE * HEAD_DIM), lambda bi, i: (bi, 0, i, 0)),
                jax.ShapeDtypeStruct((b, NSA_KV_HEADS, s // CMP_STRIDE, CMP_STRIDE * HEAD_DIM), _MXU))

    outs = [heads(MLA_HEADS, LANES), heads(MLA_HEADS, LANES), values_t(MLA_HEADS), heads(NSA_HEADS, HEAD_DIM),
            chunked(), chunked(), heads(NSA_KV_HEADS, onehot.shape[1] + HEAD_DIM), values_t(NSA_KV_HEADS),
            heads(NSA_KV_HEADS, HEAD_DIM), values_t(NSA_KV_HEADS),
            (pl.BlockSpec((1, GATE_ROWS, ts), lambda bi, i: (bi, 0, i)), jax.ShapeDtypeStruct((b, GATE_ROWS, s), _F32)),
            heads(SB_HEADS, HEAD_DIM), heads(SB_HEADS, HEAD_DIM),
            (pl.BlockSpec((1, SB_HEADS, ts // SB_KEY_BLOCK, HEAD_DIM, SB_KEY_BLOCK), lambda bi, i: (bi, 0, i, 0, 0)),
             jax.ShapeDtypeStruct((b, SB_HEADS, s // SB_KEY_BLOCK, HEAD_DIM, SB_KEY_BLOCK), _MXU))]
    return pl.pallas_call(
        _proj_kernel,
        grid=(b, s // ts),
        in_specs=[pl.BlockSpec((1, ts, D_MODEL), lambda bi, i: (bi, i, 0)), full(g), full(w_ext), full(w_t), full(qn),
                  full(wuq), full(wuqs), full(kvn), full(wuk), full(wuv), full(gb),
                  tab(cq), tab(sq), tab(ck), tab(sk), tab(c64), tab(s64), tab(onehot)],
        out_specs=[o[0] for o in outs],
        out_shape=[o[1] for o in outs],
        scratch_shapes=[pltpu.VMEM((ts, HEAD_DIM), _F32)],
        compiler_params=_params("parallel", "parallel"),
        name="proj",
    )(x, g, w_ext, w_t, qn, wuq, wuqs, kvn, wuk, wuv, gb, cq, sq, ck, sk, c64, s64, onehot)


def _ones_row_pad(vt):
    first = lax.broadcasted_iota(jnp.int32, (ONES_PAD, vt.shape[1]), 0) == 0
    return jnp.concatenate([vt, jnp.where(first, 1.0, 0.0).astype(vt.dtype)], axis=0)


def _softmax_step_t(carry, st, vt_chunks):
    m, acc = carry
    m_new = jnp.maximum(m, jnp.max(st, axis=0, keepdims=True))
    alpha = jnp.exp2(m - m_new)
    pt = jnp.exp2(st - m_new).astype(_MXU)
    n = st.shape[0] // len(vt_chunks)
    pv = sum(jnp.dot(vt, pt[c * n:(c + 1) * n], preferred_element_type=_F32) for c, vt in enumerate(vt_chunks))
    return m_new, alpha * acc + pv


def _softmax_init_t(d, cols):
    return (jnp.full((1, cols), M_FLOOR, _F32), jnp.zeros((d + ONES_PAD, cols), _F32))


def _softmax_finish_t(carry, d):
    _, acc = carry
    return acc[:d] * (1.0 / acc[d:d + 1])


def _two_chain_sweep(n_full, qk, soft, init):
    def body(j, carry, diag=False):
        c0, c1 = carry
        qk(0, j)
        c1 = soft(1, j, c1, diag)
        qk(1, jnp.zeros_like(j) if diag else j + 1)
        c0 = soft(0, j, c0, diag)
        return c0, c1

    def unrolled(i, carry):
        for u in range(SWEEP_UNROLL):
            carry = body(SWEEP_UNROLL * i + u, carry)
        return carry

    qk(1, n_full)
    carry = body(n_full, init, True)
    trips = n_full // SWEEP_UNROLL
    carry = lax.fori_loop(0, trips, unrolled, carry)
    return lax.fori_loop(SWEEP_UNROLL * trips, n_full, body, carry)


def _mla_kernel(q_ref, k_ref, vt_ref, o_ref, s0_ref, s1_ref, *, t, nsub):
    qi = pl.program_id(2)
    s_refs = (s0_ref, s1_ref)

    def qk(hh, j):
        off = pl.multiple_of(j * t, t)
        s_refs[hh][...] = _dot_nt(k_ref[0, hh, pl.ds(off, t), :], q_ref[0, hh])

    def soft(hh, j, carry, diag):
        st = s_refs[hh][...]
        if diag:
            key = lax.broadcasted_iota(jnp.int32, (t, t), 0)
            qry = lax.broadcasted_iota(jnp.int32, (t, t), 1)
            st = jnp.where(key <= qry, st, NEG_INF)
        return _softmax_step_t(carry, st, [vt_ref[0, hh, j * nsub + c] for c in range(nsub)])

    carry = _two_chain_sweep(qi, qk, soft, tuple(_softmax_init_t(MLA_V, t) for _ in range(2)))
    ot = jnp.concatenate([_softmax_finish_t(c, MLA_V) for c in carry], axis=0)
    o_ref[0] = ot.T.astype(o_ref.dtype)


def _mla_attention(q, k, vt):
    b, h, s, _ = q.shape
    tv = vt.shape[-1]
    dv = vt.shape[-2]
    t = min(512, s)
    assert h % 2 == 0 and 2 * MLA_V == LANES and t % tv == 0 and s % t == 0
    return pl.pallas_call(
        functools.partial(_mla_kernel, t=t, nsub=t // tv),
        grid=(b, h // 2, s // t),
        in_specs=[pl.BlockSpec((1, 2, t, LANES), lambda bi, hi, i: (bi, hi, i, 0)),
                  pl.BlockSpec((1, 2, s, LANES), lambda bi, hi, i: (bi, hi, 0, 0)),
                  pl.BlockSpec((1, 2, s // tv, dv, tv), lambda bi, hi, i: (bi, hi, 0, 0, 0))],
        out_specs=pl.BlockSpec((1, t, LANES), lambda bi, hi, i: (bi, i, hi)),
        out_shape=jax.ShapeDtypeStruct((b, s, h * MLA_V), _MXU),
        scratch_shapes=[pltpu.VMEM((t, t), _F32), pltpu.VMEM((t, t), _F32)],
        compiler_params=_params("parallel", "parallel", "arbitrary"),
        name="mla_attn",
    )(q, k, vt)


def _sb_kernel(q_ref, k_ref, vt_ref, u_ref, o_ref, *scratch_refs, tq, tk):
    qi = pl.program_id(1)
    u = u_ref[...]
    n_heads = q_ref.shape[1]
    scratch = [scratch_refs[6 * hh:6 * (hh + 1)] for hh in range(n_heads)]
    per_tile = tq // tk

    def step(j, carry, key_offset=None):
        diag = key_offset is not None
        off = pl.multiple_of(j * tk, tk)
        if diag:
            key = key_offset + lax.broadcasted_iota(jnp.int32, (tk, tq), 0)
            qry = lax.broadcasted_iota(jnp.int32, (tk, tq), 1)
            strict = key < qry

        def logits(hh):
            z_ref, _, _, _, _, _ = scratch[hh]
            z_ref[...] = _dot_nt(k_ref[0, hh, pl.ds(off, tk), :], q_ref[0, hh])

        def log_terms(hh):
            z_ref, lb_ref, hi_ref, lo_ref, _, _ = scratch[hh]
            z = z_ref[...]
            log_beta = jnp.minimum(z, 0.0) - jnp.log2(1.0 + jnp.exp2(-jnp.abs(z)))
            log_rem = log_beta - z
            if diag:
                log_rem = jnp.where(strict, log_rem, 0.0)
            hi = log_rem.astype(_MXU)
            lb_ref[...] = log_beta
            hi_ref[...] = hi
            lo_ref[...] = (log_rem - hi.astype(_F32)).astype(_MXU)
            return log_rem[0:1, :]

        def suffix_sums(hh):
            _, _, hi_ref, lo_ref, sfx_ref, _ = scratch[hh]
            sfx_ref[...] = (jnp.dot(u, hi_ref[...], preferred_element_type=_F32)
                            + jnp.dot(u, lo_ref[...], preferred_element_type=_F32))

        def weights(hh, first_rem):
            _, lb_ref, _, _, sfx_ref, a_ref = scratch[hh]
            rem = carry[hh][0]
            suffix = sfx_ref[...]
            a = jnp.exp2(lb_ref[...] + suffix + rem)
            if diag:
                a = jnp.where(strict, a, 0.0)
            a_ref[...] = a.astype(_MXU)
            return rem + suffix[0:1, :] + first_rem

        def values(hh):
            a_ref = scratch[hh][5]
            return carry[hh][1] + jnp.dot(vt_ref[0, hh, j], a_ref[...], preferred_element_type=_F32)

        heads = range(n_heads)
        for hh in heads:
            logits(hh)
        first = []
        for hh in heads:
            first.append(log_terms(hh))
            suffix_sums(hh)
        rems = [weights(hh, first[hh]) for hh in heads]
        alive = jnp.max(functools.reduce(jnp.maximum, rems)) > F32_EXP2_ZERO
        return alive, tuple((rems[hh], values(hh)) for hh in heads)

    carry = tuple((jnp.zeros((1, tq), _F32), jnp.zeros((HEAD_DIM, tq), _F32)) for _ in range(n_heads))
    first = qi * per_tile
    for i in reversed(range(per_tile)):
        _, carry = step(first + i, carry, key_offset=i * tk)
    has_past = first > 0
    carry = tuple((jnp.where(has_past, rem, NEG_INF), acc) for rem, acc in carry)
    alive, carry = step(jnp.maximum(first - 1, 0), carry)

    def earlier(c):
        return (c[0] - 1,) + step(c[0], c[2])

    _, _, carry = lax.while_loop(lambda c: jnp.logical_and(c[0] >= 0, c[1]), earlier, (first - 2, alive, carry))
    o_ref[0] = jnp.concatenate([acc for _, acc in carry], axis=0).T.astype(o_ref.dtype)


def _sb_attention(q, k, vt):
    b, h, s, d = q.shape
    tk = vt.shape[-1]
    tq = tk
    assert (h * d) % LANES == 0 and s % tq == 0 and tq % tk == 0
    idx = np.arange(tk)
    u = jnp.asarray(idx[None, :] > idx[:, None], _MXU)
    return pl.pallas_call(
        functools.partial(_sb_kernel, tq=tq, tk=tk),
        grid=(b, s // tq),
        in_specs=[pl.BlockSpec((1, h, tq, d), lambda bi, i: (bi, 0, i, 0)),
                  pl.BlockSpec((1, h, s, d), lambda bi, i: (bi, 0, 0, 0)),
                  pl.BlockSpec((1, h, s // tk, d, tk), lambda bi, i: (bi, 0, 0, 0, 0)),
                  pl.BlockSpec((tk, tk), lambda bi, i: (0, 0))],
        out_specs=pl.BlockSpec((1, tq, h * d), lambda bi, i: (bi, i, 0)),
        out_shape=jax.ShapeDtypeStruct((b, s, h * d), _MXU),
        scratch_shapes=[pltpu.VMEM((tk, tq), dt) for _ in range(h) for dt in (_F32, _F32, _MXU, _MXU, _F32, _MXU)],
        compiler_params=_params("parallel", "arbitrary"),
        name="sb_attn",
    )(q, k, vt, u)


def _compress_kernel(xk_ref, xv_ref, w1k_ref, w2k_ref, pk_ref, w1v_ref, w2v_ref, pv_ref, ok_ref, ov_ref):
    def hidden(x_ref, w1_ref, p_ref):
        x = x_ref[0, 0]
        n = x.shape[0]
        first = jnp.dot(x, w1_ref[0], preferred_element_type=_F32)
        second = jnp.dot(x, w1_ref[1], preferred_element_type=_F32)
        pos = _dot(p_ref[0], w1_ref[0]) + _dot(p_ref[1], w1_ref[1])
        hid = first + pltpu.roll(second, n - 1, 0) + pos[0:1]
        return 0.5 * hid * (1.0 + jnp.tanh(math.sqrt(2.0 / math.pi) * (hid + 0.044715 * hid * hid * hid)))

    ok_ref[0, 0] = _dot(hidden(xk_ref, w1k_ref, pk_ref), w2k_ref[...]).astype(ok_ref.dtype)
    ov_ref[0, 0] = _dot_nt(w2v_ref[...], hidden(xv_ref, w1v_ref, pv_ref)).astype(ov_ref.dtype)


def _compress(xk, xv, layer, w1k, w2k, pk, w1v, w2v, pv):
    b, g, n, _ = xk.shape
    d = HEAD_DIM
    full = functools.partial(_layer_spec, layer=layer)

    xspec = pl.BlockSpec((1, 1, n, CMP_STRIDE * d), lambda bi, gi: (bi, gi, 0, 0))
    return pl.pallas_call(
        _compress_kernel,
        grid=(b, g),
        in_specs=[xspec, xspec, full(w1k), full(w2k), full(pk), full(w1v), full(w2v), full(pv)],
        out_specs=[pl.BlockSpec((1, 1, n, d), lambda bi, gi: (bi, gi, 0, 0)),
                   pl.BlockSpec((1, 1, d, n), lambda bi, gi: (bi, gi, 0, 0))],
        out_shape=[jax.ShapeDtypeStruct((b, g, n, d), _MXU), jax.ShapeDtypeStruct((b, g, d, n), _MXU)],
        compiler_params=_params("parallel", "parallel"),
        name="nsa_compress",
    )(xk, xv, w1k, w2k, pk, w1v, w2v, pv)


def _group_queries(q_ref, g, tq):
    return q_ref[0, g * NSA_GROUP:(g + 1) * NSA_GROUP].reshape(NSA_GROUP * tq, q_ref.shape[-1])


def _gated_heads(ot, gt_ref, g, branch, tq):
    out = []
    for r in range(NSA_GROUP):
        row = NSA_BRANCHES * (g * NSA_GROUP + r) + branch
        out.append(ot[:, r * tq:(r + 1) * tq] * gt_ref[0, row:row + 1, :])
    return out


def _cmp_kernel(q_ref, kc_ref, vct_ref, ov_ref, gt_ref, o_ref, qa_ref, s0_ref, s1_ref, *, tq, n_top):
    q0 = pl.program_id(1) * tq
    ncp = kc_ref.shape[2]
    ns = ov_ref.shape[0]
    lanes = NSA_GROUP * tq
    s_refs = (s0_ref, s1_ref)
    for g in range(NSA_KV_HEADS):
        s_refs[g][...] = _dot_nt(kc_ref[0, g], _group_queries(q_ref, g, tq))
    qpos = q0 + (lax.broadcasted_iota(jnp.int32, (1, lanes), 1) & (tq - 1))
    cmp_end = lax.broadcasted_iota(jnp.int32, (ncp, 1), 0) * CMP_STRIDE + (CMP_LEN - 1)
    visible = cmp_end <= qpos
    cur = jnp.right_shift(q0 + lax.broadcasted_iota(jnp.int32, (1, tq), 1), int(math.log2(SEL_LEN)))
    blk = lax.broadcasted_iota(jnp.int32, (ns, 1), 0)
    forced = (blk == 0) | (blk == cur) | (blk == cur - 1)
    future = blk > cur
    blk_f = blk.astype(_F32)
    heads = []
    scores = []
    for g in range(NSA_KV_HEADS):
        st = jnp.where(visible, s_refs[g][...], NEG_INF)
        e = jnp.exp2(st - jnp.max(st, axis=0, keepdims=True))
        inv = jnp.where(qpos >= CMP_LEN - 1, 1.0 / jnp.sum(e, axis=0, keepdims=True), 0.0)
        pt = e * inv
        heads += _gated_heads(_dot(vct_ref[0, g], pt), gt_ref, g, 0, tq)
        p_sum = sum(pt[:, r * tq:(r + 1) * tq] for r in range(NSA_GROUP))
        score = _dot_split_rhs(ov_ref[...], p_sum)
        scores.append(jnp.where(forced, FORCE_SCORE, jnp.where(future, -1.0, score)))
    o_ref[0] = jnp.concatenate(heads, axis=0).T.astype(o_ref.dtype)
    def select(rows):
        sc = [s[:rows] for s in scores]
        idx = blk_f[:rows]
        for _ in range(n_top):
            for g in range(NSA_KV_HEADS):
                top = jnp.max(sc[g], axis=0, keepdims=True)
                first = jnp.min(jnp.where(sc[g] == top, idx, float(ns)), axis=0, keepdims=True)
                sc[g] = jnp.where(idx == first, PICKED, sc[g])
        for g in range(NSA_KV_HEADS):
            sel_m1 = jnp.where(sc[g] < 0.5 * PICKED, 0.0, -1.0)
            if rows < ns:
                sel_m1 = jnp.concatenate([sel_m1, jnp.full((ns - rows, tq), -1.0, _F32)], axis=0)
            sel_m1 = sel_m1.T.astype(qa_ref.dtype)
            for h in range(g * NSA_GROUP, (g + 1) * NSA_GROUP):
                qa_ref[0, h, :, 0:ns] = sel_m1
                qa_ref[0, h, :, ns:ns + HEAD_DIM] = q_ref[0, h]

    visible_blocks = (q0 + tq) // SEL_LEN
    lower = 0
    for rows in sorted({min(ns, 32), min(ns, 64), ns}):
        in_range = visible_blocks > lower
        if rows < ns:
            in_range = jnp.logical_and(in_range, visible_blocks <= rows)
        pl.when(in_range)(functools.partial(select, rows))
        lower = rows


def _cmp_select(q, kc, vct, gates_t):
    b, h, s, d = q.shape
    g = kc.shape[1]
    ncp = kc.shape[2]
    ns = s // SEL_LEN
    n_top = min(SEL_TOPK, ns)
    tq = min(256, s)
    assert tq & (tq - 1) == 0 and g == 2
    c0 = np.arange(ncp)[:, None] * CMP_STRIDE
    n0 = np.arange(ns)[None, :] * SEL_LEN
    overlap = jnp.asarray(((c0 < n0 + SEL_LEN) & (c0 + CMP_LEN > n0)).T, _MXU)
    return pl.pallas_call(
        functools.partial(_cmp_kernel, tq=tq, n_top=n_top),
        grid=(b, s // tq),
        in_specs=[pl.BlockSpec((1, h, tq, d), lambda bi, i: (bi, 0, i, 0)),
                  pl.BlockSpec((1, g, ncp, d), lambda bi, i: (bi, 0, 0, 0)),
                  pl.BlockSpec((1, g, d, ncp), lambda bi, i: (bi, 0, 0, 0)),
                  pl.BlockSpec((ns, ncp), lambda bi, i: (0, 0)),
                  pl.BlockSpec((1, GATE_ROWS, tq), lambda bi, i: (bi, 0, i))],
        out_specs=[pl.BlockSpec((1, tq, h * d), lambda bi, i: (bi, i, 0)),
                   pl.BlockSpec((1, h, tq, ns + d), lambda bi, i: (bi, 0, i, 0))],
        out_shape=[jax.ShapeDtypeStruct((b, s, h * d), _MXU), jax.ShapeDtypeStruct((b, h, s, ns + d), _MXU)],
        scratch_shapes=[pltpu.VMEM((ncp, NSA_GROUP * tq), _F32) for _ in range(g)],
        compiler_params=_params("parallel", "arbitrary"),
        name="nsa_cmp_select",
    )(q, kc, vct, overlap, gates_t)


def _key_minus_query(keys, tq):
    return jnp.asarray(np.arange(keys)[:, None] - np.arange(NSA_GROUP * tq)[None, :] % tq, jnp.int32)


def _sel_kernel(q_ref, k_ref, vt_ref, gt_ref, rel_ref, o_ref, s0_ref, s1_ref, *, tq, tk, nsub):
    q0 = pl.program_id(1) * tq
    last = (q0 + tq - 1) // tk
    lanes = NSA_GROUP * tq
    s_refs = (s0_ref, s1_ref)

    def qk(g, j):
        off = pl.multiple_of(j * tk, tk)
        s_refs[g][...] = _dot_nt(k_ref[0, g, pl.ds(off, tk), :], _group_queries(q_ref, g, tq))

    def soft(g, j, carry, causal):
        st = s_refs[g][...]
        if causal:
            st = jnp.where(rel_ref[...] <= q0 - j * tk, st, NEG_INF)
        return _softmax_step_t(carry, st, [vt_ref[0, g, j * nsub + c] for c in range(nsub)])

    init = tuple(_softmax_init_t(HEAD_DIM, lanes) for _ in range(NSA_KV_HEADS))
    carry = _two_chain_sweep(last, qk, soft, init)
    heads = []
    for g in range(NSA_KV_HEADS):
        heads += _gated_heads(_softmax_finish_t(carry[g], HEAD_DIM), gt_ref, g, 1, tq)
    o_ref[0] = jnp.concatenate(heads, axis=0).T.astype(o_ref.dtype)


def _sel_attention(q, k, vt, gates_t):
    b, h, s, da = q.shape
    g = k.shape[1]
    d = HEAD_DIM
    tv = vt.shape[-1]
    tq = min(256, s)
    tk = min(512, s)
    assert tq & (tq - 1) == 0 and s % tk == 0 and tk % tv == 0 and g == 2
    return pl.pallas_call(
        functools.partial(_sel_kernel, tq=tq, tk=tk, nsub=tk // tv),
        grid=(b, s // tq),
        in_specs=[pl.BlockSpec((1, h, tq, da), lambda bi, i: (bi, 0, i, 0)),
                  pl.BlockSpec((1, g, s, da), lambda bi, i: (bi, 0, 0, 0)),
                  pl.BlockSpec((1, g) + vt.shape[2:], lambda bi, i: (bi, 0, 0, 0, 0)),
                  pl.BlockSpec((1, GATE_ROWS, tq), lambda bi, i: (bi, 0, i)),
                  pl.BlockSpec((tk, NSA_GROUP * tq), lambda bi, i: (0, 0))],
        out_specs=pl.BlockSpec((1, tq, h * d), lambda bi, i: (bi, i, 0)),
        out_shape=jax.ShapeDtypeStruct((b, s, h * d), _MXU),
        scratch_shapes=[pltpu.VMEM((tk, NSA_GROUP * tq), _F32) for _ in range(g)],
        compiler_params=_params("parallel", "arbitrary"),
        name="nsa_selected",
    )(q, k, vt, gates_t, _key_minus_query(tk, tq))


def _win_kernel(q_ref, k_ref, vt_ref, gt_ref, rel_ref, o_ref, *s_refs, tq, subs, span, tv):
    lanes = NSA_GROUP * tq
    rel = rel_ref[...]
    chains = [(sub, g) for sub in range(subs) for g in range(NSA_KV_HEADS)]

    def origin(sub):
        q0 = (pl.program_id(1) * subs + sub) * tq
        return q0, pl.multiple_of(jnp.maximum(q0 - WINDOW, 0), tq)

    for c, (sub, g) in enumerate(chains):
        _, start = origin(sub)
        q = q_ref[0, g * NSA_GROUP:(g + 1) * NSA_GROUP, sub * tq:(sub + 1) * tq].reshape(lanes, q_ref.shape[-1])
        s_refs[c][...] = _dot_nt(k_ref[0, g, pl.ds(start, span), :], q)
    for sub in range(subs):
        q0, start = origin(sub)
        offset = q0 - start
        heads = []
        for g in range(NSA_KV_HEADS):
            st = jnp.where(rel <= offset, s_refs[chains.index((sub, g))][...], NEG_INF)
            st = jnp.where(rel > offset - WINDOW, st, NEG_INF)
            carry = _softmax_step_t(_softmax_init_t(HEAD_DIM, lanes), st,
                                    [vt_ref[0, g, start // tv + c] for c in range(span // tv)])
            ot = _softmax_finish_t(carry, HEAD_DIM)
            for r in range(NSA_GROUP):
                row = NSA_BRANCHES * (g * NSA_GROUP + r) + 2
                heads.append(ot[:, r * tq:(r + 1) * tq] * gt_ref[0, row:row + 1, sub * tq:(sub + 1) * tq])
        o_ref[0, sub * tq:(sub + 1) * tq, :] = jnp.concatenate(heads, axis=0).T.astype(o_ref.dtype)


def _win_attention(q, k, vt, gates_t):
    b, h, s, d = q.shape
    g = k.shape[1]
    tv = vt.shape[-1]
    tq = min(256, s)
    subs = 2 if s % (2 * tq) == 0 else 1
    span = WINDOW + tq
    assert tq & (tq - 1) == 0 and s >= span and tq % tv == 0 and WINDOW % tv == 0 and g == NSA_KV_HEADS
    return pl.pallas_call(
        functools.partial(_win_kernel, tq=tq, subs=subs, span=span, tv=tv),
        grid=(b, s // (subs * tq)),
        in_specs=[pl.BlockSpec((1, h, subs * tq, d), lambda bi, i: (bi, 0, i, 0)),
                  pl.BlockSpec((1, g, s, d), lambda bi, i: (bi, 0, 0, 0)),
                  pl.BlockSpec((1, g) + vt.shape[2:], lambda bi, i: (bi, 0, 0, 0, 0)),
                  pl.BlockSpec((1, GATE_ROWS, subs * tq), lambda bi, i: (bi, 0, i)),
                  pl.BlockSpec((span, NSA_GROUP * tq), lambda bi, i: (0, 0))],
        out_specs=pl.BlockSpec((1, subs * tq, h * d), lambda bi, i: (bi, i, 0)),
        out_shape=jax.ShapeDtypeStruct((b, s, h * d), _MXU),
        scratch_shapes=[pltpu.VMEM((span, NSA_GROUP * tq), _F32) for _ in range(subs * g)],
        compiler_params=_params("parallel", "arbitrary"),
        name="nsa_window",
    )(q, k, vt, gates_t, _key_minus_query(span, tq))


def _out_kernel(x_ref, mla_ref, cmp_ref, sel_ref, win_ref, sb_ref, w_ref, o_ref):
    def w_rows(first_head, n_heads):
        return w_ref[first_head * HEAD_DIM:(first_head + n_heads) * HEAD_DIM, :]

    acc = x_ref[0] + jnp.dot(mla_ref[0], w_rows(0, MLA_HEADS), preferred_element_type=_F32)
    nsa = cmp_ref[0].astype(_F32) + sel_ref[0].astype(_F32) + win_ref[0].astype(_F32)
    acc = acc + _dot(nsa, w_rows(MLA_HEADS, NSA_HEADS))
    o_ref[0] = acc + jnp.dot(sb_ref[0], w_rows(MLA_HEADS + NSA_HEADS, SB_HEADS), preferred_element_type=_F32)


def _out_proj(x, o_mla, o_cmp, o_sel, o_win, o_sb, layer, w_heads):
    b, s, _ = x.shape
    ts = min(512, s)

    def rows(a):
        return pl.BlockSpec((1, ts, a.shape[2]), lambda bi, i: (bi, i, 0))

    xspec = pl.BlockSpec((1, ts, D_MODEL), lambda bi, i: (bi, i, 0))
    return pl.pallas_call(
        _out_kernel,
        grid=(b, s // ts),
        in_specs=[xspec, rows(o_mla), rows(o_cmp), rows(o_sel), rows(o_win), rows(o_sb),
                  _layer_spec(w_heads, layer)],
        out_specs=xspec,
        out_shape=jax.ShapeDtypeStruct(x.shape, _F32),
        compiler_params=_params("parallel", "parallel"),
        name="out_proj",
    )(x, o_mla, o_cmp, o_sel, o_win, o_sb, w_heads)


def _gather_cols(w, idx):
    idx = np.asarray(idx)
    cuts = [0] + [i for i in range(1, len(idx)) if idx[i] != idx[i - 1] + (idx[i - 1] >= 0)] + [len(idx)]
    pieces = []
    for a, b in zip(cuts[:-1], cuts[1:]):
        if idx[a] < 0:
            pieces.append(jnp.zeros(w.shape[:-1] + (b - a,), _MXU))
        else:
            pieces.append(w[..., int(idx[a]):int(idx[a]) + b - a].astype(_MXU))
    return jnp.concatenate(pieces, axis=-1)


def _swap_halves(rot):
    return (np.arange(rot) + rot // 2) % rot


def _w_in_index():
    idx = np.full((_N_HEAD_COLS * HEAD_DIM,), -1, np.int64)

    def put(col, src):
        src = np.asarray(src)
        idx[col:col + len(src)] = src

    def put_head(pos, src):
        put(pos * HEAD_DIM, src)

    put(_S_CQ * LANES, _O_CQ + np.arange(MLA_Q_LORA))
    put(_S_CKV * LANES, _O_CKV + np.arange(MLA_KV_LORA))
    put(_S_KR * LANES + MLA_NOPE, _O_KR + np.arange(MLA_ROPE))
    put(_S_KRS * LANES + MLA_NOPE, _O_KR + _swap_halves(MLA_ROPE))
    for h in range(NSA_HEADS):
        put_head(_H_NQ + h, _O_NQ + h * HEAD_DIM + np.arange(HEAD_DIM))
        put_head(_H_NQS + h, _O_NQ + h * HEAD_DIM + _swap_halves(PARTIAL_ROT))
    for hk, hks, ok in ((_H_KC, _H_KCS, _O_NKC), (_H_KS, _H_KSS, _O_NKS), (_H_KW, _H_KWS, _O_NKW)):
        for g in range(NSA_KV_HEADS):
            put_head(hk + g, ok + g * HEAD_DIM + np.arange(HEAD_DIM))
            put_head(hks + g, ok + g * HEAD_DIM + _swap_halves(PARTIAL_ROT))
    for g in range(NSA_KV_HEADS):
        put_head(_H_VC + g, _O_NVC + g * HEAD_DIM + np.arange(HEAD_DIM))
    for h in range(SB_HEADS):
        put_head(_H_SBQ + h, _O_SBQ + h * HEAD_DIM + np.arange(HEAD_DIM))
        put_head(_H_SBK + h, _O_SBK + h * HEAD_DIM + np.arange(HEAD_DIM))
    return idx


def _mla_up_index():
    qd = MLA_NOPE + MLA_ROPE
    kd = MLA_NOPE + MLA_V
    uq = np.full((MLA_HEADS * LANES,), -1, np.int64)
    uqs = uq.copy()
    uk = uq.copy()
    for h in range(MLA_HEADS):
        uq[h * LANES:h * LANES + qd] = h * qd + np.arange(qd)
        uqs[h * LANES + MLA_NOPE:h * LANES + qd] = h * qd + MLA_NOPE + _swap_halves(MLA_ROPE)
        uk[h * LANES:h * LANES + MLA_NOPE] = h * kd + np.arange(MLA_NOPE)
    return uq, uqs, uk


def _transposed_weights(w_in, gate_bias):
    width = NSA_KV_HEADS * HEAD_DIM
    gate_rows = jnp.pad(w_in[..., _O_GATE:_O_GATE + N_GATES], ((0, 0), (0, 0), (0, _T_SBV - _T_GATE - N_GATES)))
    rows = jnp.concatenate([w_in[..., _O_NVS:_O_NVS + width], w_in[..., _O_NVW:_O_NVW + width], gate_rows,
                            w_in[..., _O_SBV:_O_SBV + SB_HEADS * HEAD_DIM]], axis=-1)
    bias = jnp.pad(gate_bias, ((0, 0), (0, GATE_ROWS - N_GATES)))[..., None]
    return jnp.swapaxes(rows, -1, -2).astype(_MXU), bias


def _rope_tables(s):
    pos = np.arange(s, dtype=np.float32)

    def cs(rot):
        half = rot // 2
        inv_freq = np.float32(ROPE_THETA) ** (-np.arange(half, dtype=np.float32) / np.float32(half))
        ang = (pos[:, None] * inv_freq[None, :].astype(np.float32)).astype(np.float32)
        c, sn = np.cos(ang.astype(np.float64)), np.sin(ang.astype(np.float64))
        return np.concatenate([c, c], axis=1), np.concatenate([-sn, sn], axis=1)

    c, sn = cs(MLA_ROPE)
    pad = np.zeros((s, LANES - MLA_NOPE - MLA_ROPE))
    ck = np.concatenate([np.ones((s, MLA_NOPE)), c, pad], axis=1)
    sk = np.concatenate([np.zeros((s, MLA_NOPE)), sn, pad], axis=1)
    q_scale = (MLA_NOPE + MLA_ROPE) ** -0.5 * LOG2_E
    c, sn = cs(PARTIAL_ROT)
    c64 = np.concatenate([c, np.ones((s, HEAD_DIM - PARTIAL_ROT))], axis=1)
    s64 = np.concatenate([sn, np.zeros((s, HEAD_DIM - PARTIAL_ROT))], axis=1)
    ns = s // SEL_LEN
    onehot = (np.arange(s)[:, None] // SEL_LEN == np.arange(ns)[None, :]) * -NEG_INF
    tables = [jnp.asarray(t, _F32) for t in (ck * q_scale, sk * q_scale, ck, sk, c64, s64)]
    return tables + [jnp.asarray(onehot, _MXU)]


def kernel(x, ffn1_norm, ffn1_w_gate, ffn1_w_up, ffn1_w_down, mix_norm, w_in, mla_q_norm, mla_w_uq, mla_kv_norm,
           mla_w_ukv, nsa_gate_bias, nsa_cmp_pos_k, nsa_cmp_w1_k, nsa_cmp_w2_k, nsa_cmp_pos_v, nsa_cmp_w1_v,
           nsa_cmp_w2_v, w_out, ffn2_norm, ffn2_w_gate, ffn2_w_up, ffn2_w_down, final_norm):
    b, s, d = x.shape
    depth = w_in.shape[0]
    tabs = _rope_tables(s)
    in_idx = _w_in_index()
    uq_idx, uqs_idx, uk_idx = _mla_up_index()
    half = CMP_LEN * HEAD_DIM // 2
    fg = final_norm.reshape(1, d)

    def row(p):
        return p[:, None, :]

    def cmp_weights(w1, w2, pos, transpose_out):
        pos = jnp.broadcast_to(pos.reshape(depth, 2, 1, half), (depth, 2, 8, half)).astype(_MXU)
        w2 = jnp.swapaxes(w2, -1, -2) if transpose_out else w2
        return w1.reshape(depth, 2, half, CMP_HIDDEN).astype(_MXU), w2.astype(_MXU), pos

    ffn1 = [w.astype(_MXU) for w in (ffn1_w_gate, ffn1_w_up, ffn1_w_down)]
    ffn2 = [w.astype(_MXU) for w in (ffn2_w_gate, ffn2_w_up, ffn2_w_down)]
    w_t, gate_bias = _transposed_weights(w_in, nsa_gate_bias)
    wuv_t = mla_w_ukv.reshape(depth, MLA_KV_LORA, MLA_HEADS, 2, MLA_V)[:, :, :, 1].transpose(0, 2, 3, 1).astype(_MXU)
    proj_params = (row(mix_norm), _gather_cols(w_in, in_idx), w_t,
                   row(mla_q_norm), _gather_cols(mla_w_uq, uq_idx), _gather_cols(mla_w_uq, uqs_idx),
                   row(mla_kv_norm), _gather_cols(mla_w_ukv, uk_idx), wuv_t, gate_bias)
    cmp_params = (cmp_weights(nsa_cmp_w1_k, nsa_cmp_w2_k, nsa_cmp_pos_k, False)
                  + cmp_weights(nsa_cmp_w1_v, nsa_cmp_w2_v, nsa_cmp_pos_v, True))
    w_out = w_out.astype(_MXU)
    ffn1_norm, ffn2_norm = row(ffn1_norm), row(ffn2_norm)

    for l in range(depth):
        x2d = _ffn(x.reshape(b * s, d), ffn1_norm, *ffn1, l, fg, False)
        x = x2d.reshape(b, s, d)
        (mq, mk, mvt, nq, nkc, nvc, nks, nvst, nkw, nvwt, gates_t, sbq, sbk, sbv) = _proj(x, l, *proj_params, tabs)
        o_mla = _mla_attention(mq, mk, mvt)
        kc, vct = _compress(nkc, nvc, l, *cmp_params)
        o_cmp, q_sel = _cmp_select(nq, kc, vct, gates_t)
        o_sel = _sel_attention(q_sel, nks, nvst, gates_t)
        o_win = _win_attention(nq, nkw, nvwt, gates_t)
        o_sb = _sb_attention(sbq, sbk, sbv)
        x = _out_proj(x, o_mla, o_cmp, o_sel, o_win, o_sb, l, w_out)
        x2d = _ffn(x.reshape(b * s, d), ffn2_norm, *ffn2, l, fg, l == depth - 1)
        x = x2d.reshape(b, s, d)
    return x
```

```python
import functools
import math

import numpy as np
import jax
import jax.numpy as jnp
from jax import lax
from jax.experimental import pallas as pl
from jax.experimental.pallas import tpu as pltpu

D_MODEL = 1024
HEAD_DIM = 64
MLA_HEADS = 6
MLA_NOPE = 64
MLA_ROPE = 32
MLA_V = 64
MLA_Q_LORA = 256
MLA_KV_LORA = 128
NSA_HEADS = 6
NSA_KV_HEADS = 2
NSA_GROUP = NSA_HEADS // NSA_KV_HEADS
NSA_BRANCHES = 3
CMP_LEN = 32
CMP_STRIDE = 16
CMP_HIDDEN = 128
SEL_LEN = 64
SEL_TOPK = 16
WINDOW = 512
SB_HEADS = 4
D_FF = 2816
ROPE_THETA = 500000.0
PARTIAL_ROT = HEAD_DIM // 4
EPS = 1e-6
NEG_INF = -1e30
M_FLOOR = 0.1 * NEG_INF
FORCE_SCORE = 1e4
PICKED = -3e38
F32_EXP2_ZERO = -151.0
LOG2_E = math.log2(math.e)
N_GATES = NSA_HEADS * NSA_BRANCHES

LANES = 128
FFN_CHUNK = 256
SWEEP_UNROLL = 4
SB_KEY_BLOCK = 256
TOKEN_CHUNK = 256
ONES_PAD = 16
VMEM_LIMIT = 56 * 1024 * 1024

_MXU = jnp.bfloat16
_F32 = jnp.float32

_IN_WIDTHS = (MLA_Q_LORA, MLA_KV_LORA, MLA_ROPE, NSA_HEADS * HEAD_DIM) + (NSA_KV_HEADS * HEAD_DIM,) * 6 + (
    N_GATES, SB_HEADS * HEAD_DIM, SB_HEADS * HEAD_DIM, SB_HEADS * HEAD_DIM)
_IN_OFF = np.concatenate([[0], np.cumsum(_IN_WIDTHS)])
(_O_CQ, _O_CKV, _O_KR, _O_NQ, _O_NKC, _O_NVC, _O_NKS, _O_NVS, _O_NKW, _O_NVW, _O_GATE, _O_SBQ, _O_SBK,
 _O_SBV) = [int(v) for v in _IN_OFF[:-1]]

_S_CQ, _S_CKV, _S_KR, _S_KRS = 0, 2, 3, 4
_H_NQ, _H_NQS = 10, 16
_H_KC, _H_KCS, _H_VC = 22, 24, 26
_H_KS, _H_KSS = 28, 30
_H_KW, _H_KWS = 32, 34
_H_SBQ, _H_SBK = 36, 40
_N_HEAD_COLS = 44
_T_VS, _T_VW, _T_GATE = 0, NSA_KV_HEADS * HEAD_DIM, 2 * NSA_KV_HEADS * HEAD_DIM
GATE_ROWS = 24
_T_SBV = _T_GATE + 32
_T_ROWS = _T_SBV + SB_HEADS * HEAD_DIM


def _dot(a, b):
    return jnp.dot(a.astype(_MXU), b.astype(_MXU), preferred_element_type=_F32)


def _dot_nt(a, b):
    return lax.dot_general(a.astype(_MXU), b.astype(_MXU), (((1,), (1,)), ((), ())),
                           preferred_element_type=_F32)


def _dot_split_rhs(a, b):
    hi = b.astype(_MXU)
    lo = (b - hi.astype(_F32)).astype(_MXU)
    return (jnp.dot(a, hi, preferred_element_type=_F32) + jnp.dot(a, lo, preferred_element_type=_F32))


def _rms(x, g):
    return x * lax.rsqrt(jnp.mean(x * x, axis=-1, keepdims=True) + EPS) * g


def _params(*sem):
    return pltpu.CompilerParams(dimension_semantics=sem, vmem_limit_bytes=VMEM_LIMIT)


def _layer_spec(a, layer):
    return pl.BlockSpec((None,) + a.shape[1:], lambda *_: (layer,) + (0,) * (a.ndim - 1))


def _ffn_kernel(x_ref, g_ref, wg_ref, wu_ref, wd_ref, fg_ref, o_ref, h_ref, acc_ref, act_ref, *, final_norm):
    j = pl.program_id(1)

    @pl.when(j == 0)
    def _():
        h_ref[...] = _rms(x_ref[...], g_ref[...]).astype(h_ref.dtype)
        acc_ref[...] = jnp.zeros_like(acc_ref)

    h = h_ref[...]
    tf = act_ref.shape[1]
    for c0 in range(0, tf, FFN_CHUNK):
        c1 = min(c0 + FFN_CHUNK, tf)
        gate = jnp.dot(h, wg_ref[:, c0:c1], preferred_element_type=_F32)
        up = jnp.dot(h, wu_ref[:, c0:c1], preferred_element_type=_F32)
        act_ref[:, c0:c1] = (gate * jax.nn.sigmoid(gate) * up).astype(act_ref.dtype)
    acc_ref[...] += jnp.dot(act_ref[...], wd_ref[...], preferred_element_type=_F32)

    @pl.when(j == pl.num_programs(1) - 1)
    def _():
        y = x_ref[...] + 0.5 * acc_ref[...]
        if final_norm:
            y = _rms(y, fg_ref[...])
        o_ref[...] = y


def _ffn(x2d, g, wg, wu, wd, layer, fg, final_norm):
    rows = x2d.shape[0]
    tm = min(1024, rows)
    tf = D_FF // 2
    grid = (rows // tm, D_FF // tf)
    return pl.pallas_call(
        functools.partial(_ffn_kernel, final_norm=final_norm),
        grid=grid,
        in_specs=[
            pl.BlockSpec((tm, D_MODEL), lambda i, j: (i, 0)),
            _layer_spec(g, layer),
            pl.BlockSpec((None, D_MODEL, tf), lambda i, j: (layer, 0, j)),
            pl.BlockSpec((None, D_MODEL, tf), lambda i, j: (layer, 0, j)),
            pl.BlockSpec((None, tf, D_MODEL), lambda i, j: (layer, j, 0)),
            pl.BlockSpec((1, D_MODEL), lambda i, j: (0, 0)),
        ],
        out_specs=pl.BlockSpec((tm, D_MODEL), lambda i, j: (i, 0)),
        out_shape=jax.ShapeDtypeStruct((rows, D_MODEL), _F32),
        scratch_shapes=[pltpu.VMEM((tm, D_MODEL), _MXU), pltpu.VMEM((tm, D_MODEL), _F32), pltpu.VMEM((tm, tf), _MXU)],
        compiler_params=_params("parallel", "arbitrary"),
        name="ffn",
    )(x2d, g, wg, wu, wd, fg)


def _proj_kernel(x_ref, g_ref, w_ref, wt_ref, qn_ref, wuq_ref, wuqs_ref, kvn_ref, wuk_ref, wuv_ref, gb_ref,
                 cq_ref, sq_ref, ck_ref, sk_ref, c64_ref, s64_ref, oh_ref,
                 mq_ref, mk_ref, mv_ref, nq_ref, nkc_ref, nvc_ref, nks_ref, nvs_ref, nkw_ref, nvw_ref,
                 gate_ref, sbq_ref, sbk_ref, sbv_ref, stage_ref):
    hn = _rms(x_ref[0], g_ref[...]).astype(_MXU)

    def proj(h0, h1):
        return jnp.dot(hn, w_ref[:, h0 * HEAD_DIM:h1 * HEAD_DIM], preferred_element_type=_F32)

    def slot(p, s):
        return p[:, s * LANES:(s + 1) * LANES]

    def head(p, i):
        return p[:, i * HEAD_DIM:(i + 1) * HEAD_DIM]

    p = proj(0, _H_NQ)
    cq = _rms(p[:, :MLA_Q_LORA], qn_ref[...])
    ckv = _rms(slot(p, _S_CKV), kvn_ref[...])
    q = _dot(cq, wuq_ref[...])
    q_partner = _dot(cq, wuqs_ref[...])
    kpe = slot(p, _S_KR) * ck_ref[...] + slot(p, _S_KRS) * sk_ref[...]
    kn = _dot(ckv, wuk_ref[...])
    for h in range(MLA_HEADS):
        mq_ref[0, h] = (slot(q, h) * cq_ref[...] + slot(q_partner, h) * sq_ref[...]).astype(mq_ref.dtype)
        mk_ref[0, h] = (slot(kn, h) + kpe).astype(mk_ref.dtype)
        mv_ref[0, h, 0] = _ones_row_pad(_dot_nt(wuv_ref[h], ckv)).astype(mv_ref.dtype)

    c64 = c64_ref[...]
    s64 = s64_ref[...]
    scale = HEAD_DIM ** -0.5

    p = proj(_H_NQ, _H_KC)
    for h in range(NSA_HEADS):
        nq_ref[0, h] = ((head(p, h) * c64 + head(p, NSA_HEADS + h) * s64) * (scale * LOG2_E)).astype(nq_ref.dtype)

    p = proj(_H_KC, _H_SBQ)
    base = _H_KC
    ns = oh_ref.shape[-1]

    def roped(hk, hks, g):
        return head(p, hk - base + g) * c64 + head(p, hks - base + g) * s64

    def write_chunked(o_ref, g, val):
        stage_ref[...] = val
        for t in range(CMP_STRIDE):
            piece = stage_ref[pl.ds(t, val.shape[0] // CMP_STRIDE, stride=CMP_STRIDE), :]
            o_ref[0, g, :, t * HEAD_DIM:(t + 1) * HEAD_DIM] = piece.astype(o_ref.dtype)

    for g in range(NSA_KV_HEADS):
        write_chunked(nkc_ref, g, roped(_H_KC, _H_KCS, g))
        write_chunked(nvc_ref, g, head(p, _H_VC - base + g))
        nks_ref[0, g, :, 0:ns] = oh_ref[...]
        nks_ref[0, g, :, ns:ns + HEAD_DIM] = roped(_H_KS, _H_KSS, g).astype(nks_ref.dtype)
        nkw_ref[0, g] = roped(_H_KW, _H_KWS, g).astype(nkw_ref.dtype)

    pt = _dot_nt(wt_ref[...], hn)
    for g in range(NSA_KV_HEADS):
        lo = g * HEAD_DIM
        nvs_ref[0, g, 0] = _ones_row_pad(pt[_T_VS + lo:_T_VS + lo + HEAD_DIM]).astype(nvs_ref.dtype)
        nvw_ref[0, g, 0] = _ones_row_pad(pt[_T_VW + lo:_T_VW + lo + HEAD_DIM]).astype(nvw_ref.dtype)
    gate_ref[0] = jax.nn.sigmoid(pt[_T_GATE:_T_GATE + GATE_ROWS] + gb_ref[...])

    p = proj(_H_SBQ, _N_HEAD_COLS)
    for h in range(SB_HEADS):
        sbq_ref[0, h] = (head(p, h) * (scale * LOG2_E)).astype(sbq_ref.dtype)
        sbk_ref[0, h] = head(p, SB_HEADS + h).astype(sbk_ref.dtype)
        for c in range(sbv_ref.shape[2]):
            sbv_ref[0, h, c] = pt[_T_SBV + h * HEAD_DIM:_T_SBV + (h + 1) * HEAD_DIM,
                                  c * SB_KEY_BLOCK:(c + 1) * SB_KEY_BLOCK].astype(sbv_ref.dtype)


def _proj(x, layer, g, w_ext, w_t, qn, wuq, wuqs, kvn, wuk, wuv, gb, tabs):
    b, s, _ = x.shape
    ts = min(TOKEN_CHUNK, s)
    cq, sq, ck, sk, c64, s64, onehot = tabs
    full = functools.partial(_layer_spec, layer=layer)

    def tab(a):
        return pl.BlockSpec((ts, a.shape[1]), lambda bi, i: (i, 0))

    def heads(n, d):
        return (pl.BlockSpec((1, n, ts, d), lambda bi, i: (bi, 0, i, 0)),
                jax.ShapeDtypeStruct((b, n, s, d), _MXU))

    def values_t(n):
        return (pl.BlockSpec((1, n, 1, HEAD_DIM + ONES_PAD, ts), lambda bi, i: (bi, 0, i, 0, 0)),
                jax.ShapeDtypeStruct((b, n, s // ts, HEAD_DIM + ONES_PAD, ts), _MXU))

    def chunked():
        return (pl.BlockSpec((1, NSA_KV_HEADS, ts // CMP_STRIDE, CMP_STRIDE * HEAD_DIM), lambda bi, i: (bi, 0, i, 0)),
                jax.ShapeDtypeStruct((b, NSA_KV_HEADS, s // CMP_STRIDE, CMP_STRIDE * HEAD_DIM), _MXU))

    outs = [heads(MLA_HEADS, LANES), heads(MLA_HEADS, LANES), values_t(MLA_HEADS), heads(NSA_HEADS, HEAD_DIM),
            chunked(), chunked(), heads(NSA_KV_HEADS, onehot.shape[1] + HEAD_DIM), values_t(NSA_KV_HEADS),
            heads(NSA_KV_HEADS, HEAD_DIM), values_t(NSA_KV_HEADS),
            (pl.BlockSpec((1, GATE_ROWS, ts), lambda bi, i: (bi, 0, i)), jax.ShapeDtypeStruct((b, GATE_ROWS, s), _F32)),
            heads(SB_HEADS, HEAD_DIM), heads(SB_HEADS, HEAD_DIM),
            (pl.BlockSpec((1, SB_HEADS, ts // SB_KEY_BLOCK, HEAD_DIM, SB_KEY_BLOCK), lambda bi, i: (bi, 0, i, 0, 0)),
             jax.ShapeDtypeStruct((b, SB_HEADS, s // SB_KEY_BLOCK, HEAD_DIM, SB_KEY_BLOCK), _MXU))]
    return pl.pallas_call(
        _proj_kernel,
        grid=(b, s // ts),
        in_specs=[pl.BlockSpec((1, ts, D_MODEL), lambda bi, i: (bi, i, 0)), full(g), full(w_ext), full(w_t), full(qn),
                  full(wuq), full(wuqs), full(kvn), full(wuk), full(wuv), full(gb),
                  tab(cq), tab(sq), tab(ck), tab(sk), tab(c64), tab(s64), tab(onehot)],
        out_specs=[o[0] for o in outs],
        out_shape=[o[1] for o in outs],
        scratch_shapes=[pltpu.VMEM((ts, HEAD_DIM), _F32)],
        compiler_params=_params("parallel", "parallel"),
        name="proj",
    )(x, g, w_ext, w_t, qn, wuq, wuqs, kvn, wuk, wuv, gb, cq, sq, ck, sk, c64, s64, onehot)


def _ones_row_pad(vt):
    first = lax.broadcasted_iota(jnp.int32, (ONES_PAD, vt.shape[1]), 0) == 0
    return jnp.concatenate([vt, jnp.where(first, 1.0, 0.0).astype(vt.dtype)], axis=0)


def _softmax_step_t(carry, st, vt_chunks):
    m, acc = carry
    m_new = jnp.maximum(m, jnp.max(st, axis=0, keepdims=True))
    alpha = jnp.exp2(m - m_new)
    pt = jnp.exp2(st - m_new).astype(_MXU)
    n = st.shape[0] // len(vt_chunks)
    pv = sum(jnp.dot(vt, pt[c * n:(c + 1) * n], preferred_element_type=_F32) for c, vt in enumerate(vt_chunks))
    return m_new, alpha * acc + pv


def _softmax_init_t(d, cols):
    return (jnp.full((1, cols), M_FLOOR, _F32), jnp.zeros((d + ONES_PAD, cols), _F32))


def _softmax_finish_t(carry, d):
    _, acc = carry
    return acc[:d] * (1.0 / acc[d:d + 1])


def _two_chain_sweep(n_full, qk, soft, init):
    def body(j, carry, diag=False):
        c0, c1 = carry
        qk(0, j)
        c1 = soft(1, j, c1, diag)
        qk(1, jnp.zeros_like(j) if diag else j + 1)
        c0 = soft(0, j, c0, diag)
        return c0, c1

    def unrolled(i, carry):
        for u in range(SWEEP_UNROLL):
            carry = body(SWEEP_UNROLL * i + u, carry)
        return carry

    qk(1, n_full)
    carry = body(n_full, init, True)
    trips = n_full // SWEEP_UNROLL
    carry = lax.fori_loop(0, trips, unrolled, carry)
    return lax.fori_loop(SWEEP_UNROLL * trips, n_full, body, carry)


def _mla_kernel(q_ref, k_ref, vt_ref, o_ref, s0_ref, s1_ref, *, t, nsub):
    qi = pl.program_id(2)
    s_refs = (s0_ref, s1_ref)

    def qk(hh, j):
        off = pl.multiple_of(j * t, t)
        s_refs[hh][...] = _dot_nt(k_ref[0, hh, pl.ds(off, t), :], q_ref[0, hh])

    def soft(hh, j, carry, diag):
        st = s_refs[hh][...]
        if diag:
            key = lax.broadcasted_iota(jnp.int32, (t, t), 0)
            qry = lax.broadcasted_iota(jnp.int32, (t, t), 1)
            st = jnp.where(key <= qry, st, NEG_INF)
        return _softmax_step_t(carry, st, [vt_ref[0, hh, j * nsub + c] for c in range(nsub)])

    carry = _two_chain_sweep(qi, qk, soft, tuple(_softmax_init_t(MLA_V, t) for _ in range(2)))
    ot = jnp.concatenate([_softmax_finish_t(c, MLA_V) for c in carry], axis=0)
    o_ref[0] = ot.T.astype(o_ref.dtype)


def _mla_attention(q, k, vt):
    b, h, s, _ = q.shape
    tv = vt.shape[-1]
    dv = vt.shape[-2]
    t = min(512, s)
    assert h % 2 == 0 and 2 * MLA_V == LANES and t % tv == 0 and s % t == 0
    return pl.pallas_call(
        functools.partial(_mla_kernel, t=t, nsub=t // tv),
        grid=(b, h // 2, s // t),
        in_specs=[pl.BlockSpec((1, 2, t, LANES), lambda bi, hi, i: (bi, hi, i, 0)),
                  pl.BlockSpec((1, 2, s, LANES), lambda bi, hi, i: (bi, hi, 0, 0)),
                  pl.BlockSpec((1, 2, s // tv, dv, tv), lambda bi, hi, i: (bi, hi, 0, 0, 0))],
        out_specs=pl.BlockSpec((1, t, LANES), lambda bi, hi, i: (bi, i, hi)),
        out_shape=jax.ShapeDtypeStruct((b, s, h * MLA_V), _MXU),
        scratch_shapes=[pltpu.VMEM((t, t), _F32), pltpu.VMEM((t, t), _F32)],
        compiler_params=_params("parallel", "parallel", "arbitrary"),
        name="mla_attn",
    )(q, k, vt)


def _sb_kernel(q_ref, k_ref, vt_ref, u_ref, o_ref, *scratch_refs, tq, tk):
    qi = pl.program_id(1)
    u = u_ref[...]
    n_heads = q_ref.shape[1]
    scratch = [scratch_refs[6 * hh:6 * (hh + 1)] for hh in range(n_heads)]
    per_tile = tq // tk

    def step(j, carry, key_offset=None):
        diag = key_offset is not None
        off = pl.multiple_of(j * tk, tk)
        if diag:
            key = key_offset + lax.broadcasted_iota(jnp.int32, (tk, tq), 0)
            qry = lax.broadcasted_iota(jnp.int32, (tk, tq), 1)
            strict = key < qry

        def logits(hh):
            z_ref, _, _, _, _, _ = scratch[hh]
            z_ref[...] = _dot_nt(k_ref[0, hh, pl.ds(off, tk), :], q_ref[0, hh])

        def log_terms(hh):
            z_ref, lb_ref, hi_ref, lo_ref, _, _ = scratch[hh]
            z = z_ref[...]
            log_beta = jnp.minimum(z, 0.0) - jnp.log2(1.0 + jnp.exp2(-jnp.abs(z)))
            log_rem = log_beta - z
            if diag:
                log_rem = jnp.where(strict, log_rem, 0.0)
            hi = log_rem.astype(_MXU)
            lb_ref[...] = log_beta
            hi_ref[...] = hi
            lo_ref[...] = (log_rem - hi.astype(_F32)).astype(_MXU)
            return log_rem[0:1, :]

        def suffix_sums(hh):
            _, _, hi_ref, lo_ref, sfx_ref, _ = scratch[hh]
            sfx_ref[...] = (jnp.dot(u, hi_ref[...], preferred_element_type=_F32)
                            + jnp.dot(u, lo_ref[...], preferred_element_type=_F32))

        def weights(hh, first_rem):
            _, lb_ref, _, _, sfx_ref, a_ref = scratch[hh]
            rem = carry[hh][0]
            suffix = sfx_ref[...]
            a = jnp.exp2(lb_ref[...] + suffix + rem)
            if diag:
                a = jnp.where(strict, a, 0.0)
            a_ref[...] = a.astype(_MXU)
            return rem + suffix[0:1, :] + first_rem

        def values(hh):
            a_ref = scratch[hh][5]
            return carry[hh][1] + jnp.dot(vt_ref[0, hh, j], a_ref[...], preferred_element_type=_F32)

        heads = range(n_heads)
        for hh in heads:
            logits(hh)
        first = []
        for hh in heads:
            first.append(log_terms(hh))
            suffix_sums(hh)
        rems = [weights(hh, first[hh]) for hh in heads]
        alive = jnp.max(functools.reduce(jnp.maximum, rems)) > F32_EXP2_ZERO
        return alive, tuple((rems[hh], values(hh)) for hh in heads)

    carry = tuple((jnp.zeros((1, tq), _F32), jnp.zeros((HEAD_DIM, tq), _F32)) for _ in range(n_heads))
    first = qi * per_tile
    for i in reversed(range(per_tile)):
        _, carry = step(first + i, carry, key_offset=i * tk)
    has_past = first > 0
    carry = tuple((jnp.where(has_past, rem, NEG_INF), acc) for rem, acc in carry)
    alive, carry = step(jnp.maximum(first - 1, 0), carry)

    def earlier(c):
        return (c[0] - 1,) + step(c[0], c[2])

    _, _, carry = lax.while_loop(lambda c: jnp.logical_and(c[0] >= 0, c[1]), earlier, (first - 2, alive, carry))
    o_ref[0] = jnp.concatenate([acc for _, acc in carry], axis=0).T.astype(o_ref.dtype)


def _sb_attention(q, k, vt):
    b, h, s, d = q.shape
    tk = vt.shape[-1]
    tq = tk
    assert (h * d) % LANES == 0 and s % tq == 0 and tq % tk == 0
    idx = np.arange(tk)
    u = jnp.asarray(idx[None, :] > idx[:, None], _MXU)
    return pl.pallas_call(
        functools.partial(_sb_kernel, tq=tq, tk=tk),
        grid=(b, s // tq),
        in_specs=[pl.BlockSpec((1, h, tq, d), lambda bi, i: (bi, 0, i, 0)),
                  pl.BlockSpec((1, h, s, d), lambda bi, i: (bi, 0, 0, 0)),
                  pl.BlockSpec((1, h, s // tk, d, tk), lambda bi, i: (bi, 0, 0, 0, 0)),
                  pl.BlockSpec((tk, tk), lambda bi, i: (0, 0))],
        out_specs=pl.BlockSpec((1, tq, h * d), lambda bi, i: (bi, i, 0)),
        out_shape=jax.ShapeDtypeStruct((b, s, h * d), _MXU),
        scratch_shapes=[pltpu.VMEM((tk, tq), dt) for _ in range(h) for dt in (_F32, _F32, _MXU, _MXU, _F32, _MXU)],
        compiler_params=_params("parallel", "arbitrary"),
        name="sb_attn",
    )(q, k, vt, u)


def _compress_kernel(xk_ref, xv_ref, w1k_ref, w2k_ref, pk_ref, w1v_ref, w2v_ref, pv_ref, ok_ref, ov_ref):
    def hidden(x_ref, w1_ref, p_ref):
        x = x_ref[0, 0]
        n = x.shape[0]
        first = jnp.dot(x, w1_ref[0], preferred_element_type=_F32)
        second = jnp.dot(x, w1_ref[1], preferred_element_type=_F32)
        pos = _dot(p_ref[0], w1_ref[0]) + _dot(p_ref[1], w1_ref[1])
        hid = first + pltpu.roll(second, n - 1, 0) + pos[0:1]
        return 0.5 * hid * (1.0 + jnp.tanh(math.sqrt(2.0 / math.pi) * (hid + 0.044715 * hid * hid * hid)))

    ok_ref[0, 0] = _dot(hidden(xk_ref, w1k_ref, pk_ref), w2k_ref[...]).astype(ok_ref.dtype)
    ov_ref[0, 0] = _dot_nt(w2v_ref[...], hidden(xv_ref, w1v_ref, pv_ref)).astype(ov_ref.dtype)


def _compress(xk, xv, layer, w1k, w2k, pk, w1v, w2v, pv):
    b, g, n, _ = xk.shape
    d = HEAD_DIM
    full = functools.partial(_layer_spec, layer=layer)

    xspec = pl.BlockSpec((1, 1, n, CMP_STRIDE * d), lambda bi, gi: (bi, gi, 0, 0))
    return pl.pallas_call(
        _compress_kernel,
        grid=(b, g),
        in_specs=[xspec, xspec, full(w1k), full(w2k), full(pk), full(w1v), full(w2v), full(pv)],
        out_specs=[pl.BlockSpec((1, 1, n, d), lambda bi, gi: (bi, gi, 0, 0)),
                   pl.BlockSpec((1, 1, d, n), lambda bi, gi: (bi, gi, 0, 0))],
        out_shape=[jax.ShapeDtypeStruct((b, g, n, d), _MXU), jax.ShapeDtypeStruct((b, g, d, n), _MXU)],
        compiler_params=_params("parallel", "parallel"),
        name="nsa_compress",
    )(xk, xv, w1k, w2k, pk, w1v, w2v, pv)


def _group_queries(q_ref, g, tq):
    return q_ref[0, g * NSA_GROUP:(g + 1) * NSA_GROUP].reshape(NSA_GROUP * tq, q_ref.shape[-1])


def _gated_heads(ot, gt_ref, g, branch, tq):
    out = []
    for r in range(NSA_GROUP):
        row = NSA_BRANCHES * (g * NSA_GROUP + r) + branch
        out.append(ot[:, r * tq:(r + 1) * tq] * gt_ref[0, row:row + 1, :])
    return out


def _cmp_kernel(q_ref, kc_ref, vct_ref, ov_ref, gt_ref, o_ref, qa_ref, s0_ref, s1_ref, *, tq, n_top):
    q0 = pl.program_id(1) * tq
    ncp = kc_ref.shape[2]
    ns = ov_ref.shape[0]
    lanes = NSA_GROUP * tq
    s_refs = (s0_ref, s1_ref)
    for g in range(NSA_KV_HEADS):
        s_refs[g][...] = _dot_nt(kc_ref[0, g], _group_queries(q_ref, g, tq))
    qpos = q0 + (lax.broadcasted_iota(jnp.int32, (1, lanes), 1) & (tq - 1))
    cmp_end = lax.broadcasted_iota(jnp.int32, (ncp, 1), 0) * CMP_STRIDE + (CMP_LEN - 1)
    visible = cmp_end <= qpos
    cur = jnp.right_shift(q0 + lax.broadcasted_iota(jnp.int32, (1, tq), 1), int(math.log2(SEL_LEN)))
    blk = lax.broadcasted_iota(jnp.int32, (ns, 1), 0)
    forced = (blk == 0) | (blk == cur) | (blk == cur - 1)
    future = blk > cur
    blk_f = blk.astype(_F32)
    heads = []
    scores = []
    for g in range(NSA_KV_HEADS):
        st = jnp.where(visible, s_refs[g][...], NEG_INF)
        e = jnp.exp2(st - jnp.max(st, axis=0, keepdims=True))
        inv = jnp.where(qpos >= CMP_LEN - 1, 1.0 / jnp.sum(e, axis=0, keepdims=True), 0.0)
        pt = e * inv
        heads += _gated_heads(_dot(vct_ref[0, g], pt), gt_ref, g, 0, tq)
        p_sum = sum(pt[:, r * tq:(r + 1) * tq] for r in range(NSA_GROUP))
        score = _dot_split_rhs(ov_ref[...], p_sum)
        scores.append(jnp.where(forced, FORCE_SCORE, jnp.where(future, -1.0, score)))
    o_ref[0] = jnp.concatenate(heads, axis=0).T.astype(o_ref.dtype)
    def select(rows):
        sc = [s[:rows] for s in scores]
        idx = blk_f[:rows]
        for _ in range(n_top):
            for g in range(NSA_KV_HEADS):
                top = jnp.max(sc[g], axis=0, keepdims=True)
                first = jnp.min(jnp.where(sc[g] == top, idx, float(ns)), axis=0, keepdims=True)
                sc[g] = jnp.where(idx == first, PICKED, sc[g])
        for g in range(NSA_KV_HEADS):
            sel_m1 = jnp.where(sc[g] < 0.5 * PICKED, 0.0, -1.0)
            if rows < ns:
                sel_m1 = jnp.concatenate([sel_m1, jnp.full((ns - rows, tq), -1.0, _F32)], axis=0)
            sel_m1 = sel_m1.T.astype(qa_ref.dtype)
            for h in range(g * NSA_GROUP, (g + 1) * NSA_GROUP):
                qa_ref[0, h, :, 0:ns] = sel_m1
                qa_ref[0, h, :, ns:ns + HEAD_DIM] = q_ref[0, h]

    visible_blocks = (q0 + tq) // SEL_LEN
    lower = 0
    for rows in sorted({min(ns, 32), min(ns, 64), ns}):
        in_range = visible_blocks > lower
        if rows < ns:
            in_range = jnp.logical_and(in_range, visible_blocks <= rows)
        pl.when(in_range)(functools.partial(select, rows))
        lower = rows


def _cmp_select(q, kc, vct, gates_t):
    b, h, s, d = q.shape
    g = kc.shape[1]
    ncp = kc.shape[2]
    ns = s // SEL_LEN
    n_top = min(SEL_TOPK, ns)
    tq = min(256, s)
    assert tq & (tq - 1) == 0 and g == 2
    c0 = np.arange(ncp)[:, None] * CMP_STRIDE
    n0 = np.arange(ns)[None, :] * SEL_LEN
    overlap = jnp.asarray(((c0 < n0 + SEL_LEN) & (c0 + CMP_LEN > n0)).T, _MXU)
    return pl.pallas_call(
        functools.partial(_cmp_kernel, tq=tq, n_top=n_top),
        grid=(b, s // tq),
        in_specs=[pl.BlockSpec((1, h, tq, d), lambda bi, i: (bi, 0, i, 0)),
                  pl.BlockSpec((1, g, ncp, d), lambda bi, i: (bi, 0, 0, 0)),
                  pl.BlockSpec((1, g, d, ncp), lambda bi, i: (bi, 0, 0, 0)),
                  pl.BlockSpec((ns, ncp), lambda bi, i: (0, 0)),
                  pl.BlockSpec((1, GATE_ROWS, tq), lambda bi, i: (bi, 0, i))],
        out_specs=[pl.BlockSpec((1, tq, h * d), lambda bi, i: (bi, i, 0)),
                   pl.BlockSpec((1, h, tq, ns + d), lambda bi, i: (bi, 0, i, 0))],
        out_shape=[jax.ShapeDtypeStruct((b, s, h * d), _MXU), jax.ShapeDtypeStruct((b, h, s, ns + d), _MXU)],
        scratch_shapes=[pltpu.VMEM((ncp, NSA_GROUP * tq), _F32) for _ in range(g)],
        compiler_params=_params("parallel", "arbitrary"),
        name="nsa_cmp_select",
    )(q, kc, vct, overlap, gates_t)


def _key_minus_query(keys, tq):
    return jnp.asarray(np.arange(keys)[:, None] - np.arange(NSA_GROUP * tq)[None, :] % tq, jnp.int32)


def _sel_kernel(q_ref, k_ref, vt_ref, gt_ref, rel_ref, o_ref, s0_ref, s1_ref, *, tq, tk, nsub):
    q0 = pl.program_id(1) * tq
    last = (q0 + tq - 1) // tk
    lanes = NSA_GROUP * tq
    s_refs = (s0_ref, s1_ref)

    def qk(g, j):
        off = pl.multiple_of(j * tk, tk)
        s_refs[g][...] = _dot_nt(k_ref[0, g, pl.ds(off, tk), :], _group_queries(q_ref, g, tq))

    def soft(g, j, carry, causal):
        st = s_refs[g][...]
        if causal:
            st = jnp.where(rel_ref[...] <= q0 - j * tk, st, NEG_INF)
        return _softmax_step_t(carry, st, [vt_ref[0, g, j * nsub + c] for c in range(nsub)])

    init = tuple(_softmax_init_t(HEAD_DIM, lanes) for _ in range(NSA_KV_HEADS))
    carry = _two_chain_sweep(last, qk, soft, init)
    heads = []
    for g in range(NSA_KV_HEADS):
        heads += _gated_heads(_softmax_finish_t(carry[g], HEAD_DIM), gt_ref, g, 1, tq)
    o_ref[0] = jnp.concatenate(heads, axis=0).T.astype(o_ref.dtype)


def _sel_attention(q, k, vt, gates_t):
    b, h, s, da = q.shape
    g = k.shape[1]
    d = HEAD_DIM
    tv = vt.shape[-1]
    tq = min(256, s)
    tk = min(512, s)
    assert tq & (tq - 1) == 0 and s % tk == 0 and tk % tv == 0 and g == 2
    return pl.pallas_call(
        functools.partial(_sel_kernel, tq=tq, tk=tk, nsub=tk // tv),
        grid=(b, s // tq),
        in_specs=[pl.BlockSpec((1, h, tq, da), lambda bi, i: (bi, 0, i, 0)),
                  pl.BlockSpec((1, g, s, da), lambda bi, i: (bi, 0, 0, 0)),
                  pl.BlockSpec((1, g) + vt.shape[2:], lambda bi, i: (bi, 0, 0, 0, 0)),
                  pl.BlockSpec((1, GATE_ROWS, tq), lambda bi, i: (bi, 0, i)),
                  pl.BlockSpec((tk, NSA_GROUP * tq), lambda bi, i: (0, 0))],
        out_specs=pl.BlockSpec((1, tq, h * d), lambda bi, i: (bi, i, 0)),
        out_shape=jax.ShapeDtypeStruct((b, s, h * d), _MXU),
        scratch_shapes=[pltpu.VMEM((tk, NSA_GROUP * tq), _F32) for _ in range(g)],
        compiler_params=_params("parallel", "arbitrary"),
        name="nsa_selected",
    )(q, k, vt, gates_t, _key_minus_query(tk, tq))


def _win_kernel(q_ref, k_ref, vt_ref, gt_ref, rel_ref, o_ref, *s_refs, tq, subs, span, tv):
    lanes = NSA_GROUP * tq
    rel = rel_ref[...]
    chains = [(sub, g) for sub in range(subs) for g in range(NSA_KV_HEADS)]

    def origin(sub):
        q0 = (pl.program_id(1) * subs + sub) * tq
        return q0, pl.multiple_of(jnp.maximum(q0 - WINDOW, 0), tq)

    for c, (sub, g) in enumerate(chains):
        _, start = origin(sub)
        q = q_ref[0, g * NSA_GROUP:(g + 1) * NSA_GROUP, sub * tq:(sub + 1) * tq].reshape(lanes, q_ref.shape[-1])
        s_refs[c][...] = _dot_nt(k_ref[0, g, pl.ds(start, span), :], q)
    for sub in range(subs):
        q0, start = origin(sub)
        offset = q0 - start
        heads = []
        for g in range(NSA_KV_HEADS):
            st = jnp.where(rel <= offset, s_refs[chains.index((sub, g))][...], NEG_INF)
            st = jnp.where(rel > offset - WINDOW, st, NEG_INF)
            carry = _softmax_step_t(_softmax_init_t(HEAD_DIM, lanes), st,
                                    [vt_ref[0, g, start // tv + c] for c in range(span // tv)])
            ot = _softmax_finish_t(carry, HEAD_DIM)
            for r in range(NSA_GROUP):
                row = NSA_BRANCHES * (g * NSA_GROUP + r) + 2
                heads.append(ot[:, r * tq:(r + 1) * tq] * gt_ref[0, row:row + 1, sub * tq:(sub + 1) * tq])
        o_ref[0, sub * tq:(sub + 1) * tq, :] = jnp.concatenate(heads, axis=0).T.astype(o_ref.dtype)


def _win_attention(q, k, vt, gates_t):
    b, h, s, d = q.shape
    g = k.shape[1]
    tv = vt.shape[-1]
    tq = min(256, s)
    subs = 2 if s % (2 * tq) == 0 else 1
    span = WINDOW + tq
    assert tq & (tq - 1) == 0 and s >= span and tq % tv == 0 and WINDOW % tv == 0 and g == NSA_KV_HEADS
    return pl.pallas_call(
        functools.partial(_win_kernel, tq=tq, subs=subs, span=span, tv=tv),
        grid=(b, s // (subs * tq)),
        in_specs=[pl.BlockSpec((1, h, subs * tq, d), lambda bi, i: (bi, 0, i, 0)),
                  pl.BlockSpec((1, g, s, d), lambda bi, i: (bi, 0, 0, 0)),
                  pl.BlockSpec((1, g) + vt.shape[2:], lambda bi, i: (bi, 0, 0, 0, 0)),
                  pl.BlockSpec((1, GATE_ROWS, subs * tq), lambda bi, i: (bi, 0, i)),
                  pl.BlockSpec((span, NSA_GROUP * tq), lambda bi, i: (0, 0))],
        out_specs=pl.BlockSpec((1, subs * tq, h * d), lambda bi, i: (bi, i, 0)),
        out_shape=jax.ShapeDtypeStruct((b, s, h * d), _MXU),
        scratch_shapes=[pltpu.VMEM((span, NSA_GROUP * tq), _F32) for _ in range(subs * g)],
        compiler_params=_params("parallel", "arbitrary"),
        name="nsa_window",
    )(q, k, vt, gates_t, _key_minus_query(span, tq))


def _out_kernel(x_ref, mla_ref, cmp_ref, sel_ref, win_ref, sb_ref, w_ref, o_ref):
    def w_rows(first_head, n_heads):
        return w_ref[first_head * HEAD_DIM:(first_head + n_heads) * HEAD_DIM, :]

    acc = x_ref[0] + jnp.dot(mla_ref[0], w_rows(0, MLA_HEADS), preferred_element_type=_F32)
    nsa = cmp_ref[0].astype(_F32) + sel_ref[0].astype(_F32) + win_ref[0].astype(_F32)
    acc = acc + _dot(nsa, w_rows(MLA_HEADS, NSA_HEADS))
    o_ref[0] = acc + jnp.dot(sb_ref[0], w_rows(MLA_HEADS + NSA_HEADS, SB_HEADS), preferred_element_type=_F32)


def _out_proj(x, o_mla, o_cmp, o_sel, o_win, o_sb, layer, w_heads):
    b, s, _ = x.shape
    ts = min(512, s)

    def rows(a):
        return pl.BlockSpec((1, ts, a.shape[2]), lambda bi, i: (bi, i, 0))

    xspec = pl.BlockSpec((1, ts, D_MODEL), lambda bi, i: (bi, i, 0))
    return pl.pallas_call(
        _out_kernel,
        grid=(b, s // ts),
        in_specs=[xspec, rows(o_mla), rows(o_cmp), rows(o_sel), rows(o_win), rows(o_sb),
                  _layer_spec(w_heads, layer)],
        out_specs=xspec,
        out_shape=jax.ShapeDtypeStruct(x.shape, _F32),
        compiler_params=_params("parallel", "parallel"),
        name="out_proj",
    )(x, o_mla, o_cmp, o_sel, o_win, o_sb, w_heads)


def _gather_cols(w, idx):
    idx = np.asarray(idx)
    cuts = [0] + [i for i in range(1, len(idx)) if idx[i] != idx[i - 1] + (idx[i - 1] >= 0)] + [len(idx)]
    pieces = []
    for a, b in zip(cuts[:-1], cuts[1:]):
        if idx[a] < 0:
            pieces.append(jnp.zeros(w.shape[:-1] + (b - a,), _MXU))
        else:
            pieces.append(w[..., int(idx[a]):int(idx[a]) + b - a].astype(_MXU))
    return jnp.concatenate(pieces, axis=-1)


def _swap_halves(rot):
    return (np.arange(rot) + rot // 2) % rot


def _w_in_index():
    idx = np.full((_N_HEAD_COLS * HEAD_DIM,), -1, np.int64)

    def put(col, src):
        src = np.asarray(src)
        idx[col:col + len(src)] = src

    def put_head(pos, src):
        put(pos * HEAD_DIM, src)

    put(_S_CQ * LANES, _O_CQ + np.arange(MLA_Q_LORA))
    put(_S_CKV * LANES, _O_CKV + np.arange(MLA_KV_LORA))
    put(_S_KR * LANES + MLA_NOPE, _O_KR + np.arange(MLA_ROPE))
    put(_S_KRS * LANES + MLA_NOPE, _O_KR + _swap_halves(MLA_ROPE))
    for h in range(NSA_HEADS):
        put_head(_H_NQ + h, _O_NQ + h * HEAD_DIM + np.arange(HEAD_DIM))
        put_head(_H_NQS + h, _O_NQ + h * HEAD_DIM + _swap_halves(PARTIAL_ROT))
    for hk, hks, ok in ((_H_KC, _H_KCS, _O_NKC), (_H_KS, _H_KSS, _O_NKS), (_H_KW, _H_KWS, _O_NKW)):
        for g in range(NSA_KV_HEADS):
            put_head(hk + g, ok + g * HEAD_DIM + np.arange(HEAD_DIM))
            put_head(hks + g, ok + g * HEAD_DIM + _swap_halves(PARTIAL_ROT))
    for g in range(NSA_KV_HEADS):
        put_head(_H_VC + g, _O_NVC + g * HEAD_DIM + np.arange(HEAD_DIM))
    for h in range(SB_HEADS):
        put_head(_H_SBQ + h, _O_SBQ + h * HEAD_DIM + np.arange(HEAD_DIM))
        put_head(_H_SBK + h, _O_SBK + h * HEAD_DIM + np.arange(HEAD_DIM))
    return idx


def _mla_up_index():
    qd = MLA_NOPE + MLA_ROPE
    kd = MLA_NOPE + MLA_V
    uq = np.full((MLA_HEADS * LANES,), -1, np.int64)
    uqs = uq.copy()
    uk = uq.copy()
    for h in range(MLA_HEADS):
        uq[h * LANES:h * LANES + qd] = h * qd + np.arange(qd)
        uqs[h * LANES + MLA_NOPE:h * LANES + qd] = h * qd + MLA_NOPE + _swap_halves(MLA_ROPE)
        uk[h * LANES:h * LANES + MLA_NOPE] = h * kd + np.arange(MLA_NOPE)
    return uq, uqs, uk


def _transposed_weights(w_in, gate_bias):
    width = NSA_KV_HEADS * HEAD_DIM
    gate_rows = jnp.pad(w_in[..., _O_GATE:_O_GATE + N_GATES], ((0, 0), (0, 0), (0, _T_SBV - _T_GATE - N_GATES)))
    rows = jnp.concatenate([w_in[..., _O_NVS:_O_NVS + width], w_in[..., _O_NVW:_O_NVW + width], gate_rows,
                            w_in[..., _O_SBV:_O_SBV + SB_HEADS * HEAD_DIM]], axis=-1)
    bias = jnp.pad(gate_bias, ((0, 0), (0, GATE_ROWS - N_GATES)))[..., None]
    return jnp.swapaxes(rows, -1, -2).astype(_MXU), bias


def _rope_tables(s):
    pos = np.arange(s, dtype=np.float64)

    def cs(rot):
        half = rot // 2
        ang = pos[:, None] * (ROPE_THETA ** (-np.arange(half, dtype=np.float64) / half))[None, :]
        c, sn = np.cos(ang), np.sin(ang)
        return np.concatenate([c, c], axis=1), np.concatenate([-sn, sn], axis=1)

    c, sn = cs(MLA_ROPE)
    pad = np.zeros((s, LANES - MLA_NOPE - MLA_ROPE))
    ck = np.concatenate([np.ones((s, MLA_NOPE)), c, pad], axis=1)
    sk = np.concatenate([np.zeros((s, MLA_NOPE)), sn, pad], axis=1)
    q_scale = (MLA_NOPE + MLA_ROPE) ** -0.5 * LOG2_E
    c, sn = cs(PARTIAL_ROT)
    c64 = np.concatenate([c, np.ones((s, HEAD_DIM - PARTIAL_ROT))], axis=1)
    s64 = np.concatenate([sn, np.zeros((s, HEAD_DIM - PARTIAL_ROT))], axis=1)
    ns = s // SEL_LEN
    onehot = (np.arange(s)[:, None] // SEL_LEN == np.arange(ns)[None, :]) * -NEG_INF
    tables = [jnp.asarray(t, _F32) for t in (ck * q_scale, sk * q_scale, ck, sk, c64, s64)]
    return tables + [jnp.asarray(onehot, _MXU)]


def kernel(x, ffn1_norm, ffn1_w_gate, ffn1_w_up, ffn1_w_down, mix_norm, w_in, mla_q_norm, mla_w_uq, mla_kv_norm,
           mla_w_ukv, nsa_gate_bias, nsa_cmp_pos_k, nsa_cmp_w1_k, nsa_cmp_w2_k, nsa_cmp_pos_v, nsa_cmp_w1_v,
           nsa_cmp_w2_v, w_out, ffn2_norm, ffn2_w_gate, ffn2_w_up, ffn2_w_down, final_norm):
    b, s, d = x.shape
    depth = w_in.shape[0]
    tabs = _rope_tables(s)
    in_idx = _w_in_index()
    uq_idx, uqs_idx, uk_idx = _mla_up_index()
    half = CMP_LEN * HEAD_DIM // 2
    fg = final_norm.reshape(1, d)

    def row(p):
        return p[:, None, :]

    def cmp_weights(w1, w2, pos, transpose_out):
        pos = jnp.broadcast_to(pos.reshape(depth, 2, 1, half), (depth, 2, 8, half)).astype(_MXU)
        w2 = jnp.swapaxes(w2, -1, -2) if transpose_out else w2
        return w1.reshape(depth, 2, half, CMP_HIDDEN).astype(_MXU), w2.astype(_MXU), pos

    ffn1 = [w.astype(_MXU) for w in (ffn1_w_gate, ffn1_w_up, ffn1_w_down)]
    ffn2 = [w.astype(_MXU) for w in (ffn2_w_gate, ffn2_w_up, ffn2_w_down)]
    w_t, gate_bias = _transposed_weights(w_in, nsa_gate_bias)
    wuv_t = mla_w_ukv.reshape(depth, MLA_KV_LORA, MLA_HEADS, 2, MLA_V)[:, :, :, 1].transpose(0, 2, 3, 1).astype(_MXU)
    proj_params = (row(mix_norm), _gather_cols(w_in, in_idx), w_t,
                   row(mla_q_norm), _gather_cols(mla_w_uq, uq_idx), _gather_cols(mla_w_uq, uqs_idx),
                   row(mla_kv_norm), _gather_cols(mla_w_ukv, uk_idx), wuv_t, gate_bias)
    cmp_params = (cmp_weights(nsa_cmp_w1_k, nsa_cmp_w2_k, nsa_cmp_pos_k, False)
                  + cmp_weights(nsa_cmp_w1_v, nsa_cmp_w2_v, nsa_cmp_pos_v, True))
    w_out = w_out.astype(_MXU)
    ffn1_norm, ffn2_norm = row(ffn1_norm), row(ffn2_norm)

    for l in range(depth):
        x2d = _ffn(x.reshape(b * s, d), ffn1_norm, *ffn1, l, fg, False)
        x = x2d.reshape(b, s, d)
        (mq, mk, mvt, nq, nkc, nvc, nks, nvst, nkw, nvwt, gates_t, sbq, sbk, sbv) = _proj(x, l, *proj_params, tabs)
        o_mla = _mla_attention(mq, mk, mvt)
        kc, vct = _compress(nkc, nvc, l, *cmp_params)
        o_cmp, q_sel = _cmp_select(nq, kc, vct, gates_t)
        o_sel = _sel_attention(q_sel, nks, nvst, gates_t)
        o_win = _win_attention(nq, nkw, nvwt, gates_t)
        o_sb = _sb_attention(sbq, sbk, sbv)
        x = _out_proj(x, o_mla, o_cmp, o_sel, o_win, o_sb, l, w_out)
        x2d = _ffn(x.reshape(b * s, d), ffn2_norm, *ffn2, l, fg, l == depth - 1)
        x = x2d.reshape(b, s, d)
    return x
```

```python
import functools
import math

import numpy as np
import jax
import jax.numpy as jnp
from jax import lax
from jax.experimental import pallas as pl
from jax.experimental.pallas import tpu as pltpu

D_MODEL = 1024
HEAD_DIM = 64
MLA_HEADS = 6
MLA_NOPE = 64
MLA_ROPE = 32
MLA_V = 64
MLA_Q_LORA = 256
MLA_KV_LORA = 128
NSA_HEADS = 6
NSA_KV_HEADS = 2
NSA_GROUP = NSA_HEADS // NSA_KV_HEADS
NSA_BRANCHES = 3
CMP_LEN = 32
CMP_STRIDE = 16
CMP_HIDDEN = 128
SEL_LEN = 64
SEL_TOPK = 16
WINDOW = 512
SB_HEADS = 4
D_FF = 2816
ROPE_THETA = 500000.0
PARTIAL_ROT = HEAD_DIM // 4
EPS = 1e-6
NEG_INF = -1e30
M_FLOOR = 0.1 * NEG_INF
FORCE_SCORE = 1e4
PICKED = -3e38
F32_EXP2_ZERO = -151.0
LOG2_E = math.log2(math.e)
N_GATES = NSA_HEADS * NSA_BRANCHES

LANES = 128
FFN_CHUNK = 256
SWEEP_UNROLL = 4
TOPK_ROW_STEPS = (32, 64)
SB_KEY_BLOCK = 256
PROJ_ROWS = 512
TOKEN_CHUNK = 256
ONES_PAD = 16
VMEM_LIMIT = 56 * 1024 * 1024

_MXU = jnp.bfloat16
_F32 = jnp.float32

_IN_WIDTHS = (MLA_Q_LORA, MLA_KV_LORA, MLA_ROPE, NSA_HEADS * HEAD_DIM) + (NSA_KV_HEADS * HEAD_DIM,) * 6 + (
    N_GATES, SB_HEADS * HEAD_DIM, SB_HEADS * HEAD_DIM, SB_HEADS * HEAD_DIM)
_IN_OFF = np.concatenate([[0], np.cumsum(_IN_WIDTHS)])
(_O_CQ, _O_CKV, _O_KR, _O_NQ, _O_NKC, _O_NVC, _O_NKS, _O_NVS, _O_NKW, _O_NVW, _O_GATE, _O_SBQ, _O_SBK,
 _O_SBV) = [int(v) for v in _IN_OFF[:-1]]

_S_CQ, _S_CKV, _S_KR, _S_KRS = 0, 2, 3, 4
_H_NQ, _H_NQS = 10, 16
_H_KC, _H_KCS, _H_VC = 22, 24, 26
_H_KS, _H_KSS = 28, 30
_H_KW, _H_KWS = 32, 34
_H_SBQ, _H_SBK = 36, 40
_N_HEAD_COLS = 44
_T_VS, _T_VW, _T_GATE = 0, NSA_KV_HEADS * HEAD_DIM, 2 * NSA_KV_HEADS * HEAD_DIM
GATE_ROWS = 24
_T_SBV = _T_GATE + 2 * ONES_PAD
_T_ROWS = _T_SBV + SB_HEADS * HEAD_DIM


def _dot(a, b):
    return jnp.dot(a.astype(_MXU), b.astype(_MXU), preferred_element_type=_F32)


def _dot_nt(a, b):
    return lax.dot_general(a.astype(_MXU), b.astype(_MXU), (((1,), (1,)), ((), ())),
                           preferred_element_type=_F32)


def _dot_split_rhs(a, b):
    hi = b.astype(_MXU)
    lo = (b - hi.astype(_F32)).astype(_MXU)
    return (jnp.dot(a, hi, preferred_element_type=_F32) + jnp.dot(a, lo, preferred_element_type=_F32))


def _rms(x, g):
    return x * lax.rsqrt(jnp.mean(x * x, axis=-1, keepdims=True) + EPS) * g


def _params(*sem):
    return pltpu.CompilerParams(dimension_semantics=sem, vmem_limit_bytes=VMEM_LIMIT)


def _layer_spec(a, layer):
    return pl.BlockSpec((None,) + a.shape[1:], lambda *_: (layer,) + (0,) * (a.ndim - 1))


def _ffn_kernel(x_ref, g_ref, wg_ref, wu_ref, wd_ref, fg_ref, o_ref, h_ref, acc_ref, act_ref, *, final_norm):
    j = pl.program_id(1)

    @pl.when(j == 0)
    def _():
        h_ref[...] = _rms(x_ref[...], g_ref[...]).astype(h_ref.dtype)
        acc_ref[...] = jnp.zeros_like(acc_ref)

    h = h_ref[...]
    tf = act_ref.shape[1]
    for c0 in range(0, tf, FFN_CHUNK):
        c1 = min(c0 + FFN_CHUNK, tf)
        gate = jnp.dot(h, wg_ref[:, c0:c1], preferred_element_type=_F32)
        up = jnp.dot(h, wu_ref[:, c0:c1], preferred_element_type=_F32)
        act_ref[:, c0:c1] = (gate * jax.nn.sigmoid(gate) * up).astype(act_ref.dtype)
    acc_ref[...] += jnp.dot(act_ref[...], wd_ref[...], preferred_element_type=_F32)

    @pl.when(j == pl.num_programs(1) - 1)
    def _():
        y = x_ref[...] + 0.5 * acc_ref[...]
        if final_norm:
            y = _rms(y, fg_ref[...])
        o_ref[...] = y


def _ffn(x2d, g, wg, wu, wd, layer, fg, final_norm):
    rows = x2d.shape[0]
    tm = min(1024, rows)
    tf = D_FF // 2
    grid = (rows // tm, D_FF // tf)
    return pl.pallas_call(
        functools.partial(_ffn_kernel, final_norm=final_norm),
        grid=grid,
        in_specs=[
            pl.BlockSpec((tm, D_MODEL), lambda i, j: (i, 0)),
            _layer_spec(g, layer),
            pl.BlockSpec((None, D_MODEL, tf), lambda i, j: (layer, 0, j)),
            pl.BlockSpec((None, D_MODEL, tf), lambda i, j: (layer, 0, j)),
            pl.BlockSpec((None, tf, D_MODEL), lambda i, j: (layer, j, 0)),
            pl.BlockSpec((1, D_MODEL), lambda i, j: (0, 0)),
        ],
        out_specs=pl.BlockSpec((tm, D_MODEL), lambda i, j: (i, 0)),
        out_shape=jax.ShapeDtypeStruct((rows, D_MODEL), _F32),
        scratch_shapes=[pltpu.VMEM((tm, D_MODEL), _MXU), pltpu.VMEM((tm, D_MODEL), _F32), pltpu.VMEM((tm, tf), _MXU)],
        compiler_params=_params("parallel", "arbitrary"),
        name="ffn",
    )(x2d, g, wg, wu, wd, fg)


def _proj_kernel(x_ref, g_ref, w_ref, wt_ref, qn_ref, wuq_ref, wuqs_ref, kvn_ref, wuk_ref, wuv_ref, gb_ref,
                 cq_ref, sq_ref, ck_ref, sk_ref, c64_ref, s64_ref, oh_ref,
                 mq_ref, mk_ref, mv_ref, nq_ref, nkc_ref, nvc_ref, nks_ref, nvs_ref, nkw_ref, nvw_ref,
                 gate_ref, sbq_ref, sbk_ref, sbv_ref, stage_ref):
    hn = _rms(x_ref[0], g_ref[...]).astype(_MXU)

    def proj(h0, h1):
        return jnp.dot(hn, w_ref[:, h0 * HEAD_DIM:h1 * HEAD_DIM], preferred_element_type=_F32)

    def slot(p, s):
        return p[:, s * LANES:(s + 1) * LANES]

    def head(p, i):
        return p[:, i * HEAD_DIM:(i + 1) * HEAD_DIM]

    def write_value_chunks(o_ref, h, vt):
        width = o_ref.shape[-1]
        for c in range(o_ref.shape[2]):
            o_ref[0, h, c] = vt[:, c * width:(c + 1) * width].astype(o_ref.dtype)

    p = proj(0, _H_NQ)
    cq = _rms(p[:, :MLA_Q_LORA], qn_ref[...])
    ckv = _rms(slot(p, _S_CKV), kvn_ref[...])
    q = _dot(cq, wuq_ref[...])
    q_partner = _dot(cq, wuqs_ref[...])
    kpe = slot(p, _S_KR) * ck_ref[...] + slot(p, _S_KRS) * sk_ref[...]
    kn = _dot(ckv, wuk_ref[...])
    for h in range(MLA_HEADS):
        mq_ref[0, h] = (slot(q, h) * cq_ref[...] + slot(q_partner, h) * sq_ref[...]).astype(mq_ref.dtype)
        mk_ref[0, h] = (slot(kn, h) + kpe).astype(mk_ref.dtype)
        write_value_chunks(mv_ref, h, _ones_row_pad(_dot_nt(wuv_ref[h], ckv)))

    c64 = c64_ref[...]
    s64 = s64_ref[...]
    scale = HEAD_DIM ** -0.5

    p = proj(_H_NQ, _H_KC)
    for h in range(NSA_HEADS):
        nq_ref[0, h] = ((head(p, h) * c64 + head(p, NSA_HEADS + h) * s64) * (scale * LOG2_E)).astype(nq_ref.dtype)

    p = proj(_H_KC, _H_SBQ)
    base = _H_KC
    ns = oh_ref.shape[-1]

    def roped(hk, hks, g):
        return head(p, hk - base + g) * c64 + head(p, hks - base + g) * s64

    def write_chunked(o_ref, g, val):
        stage_ref[...] = val
        for t in range(CMP_STRIDE):
            piece = stage_ref[pl.ds(t, val.shape[0] // CMP_STRIDE, stride=CMP_STRIDE), :]
            o_ref[0, g, :, t * HEAD_DIM:(t + 1) * HEAD_DIM] = piece.astype(o_ref.dtype)

    for g in range(NSA_KV_HEADS):
        write_chunked(nkc_ref, g, roped(_H_KC, _H_KCS, g))
        write_chunked(nvc_ref, g, head(p, _H_VC - base + g))
        nks_ref[0, g, :, 0:ns] = oh_ref[...]
        nks_ref[0, g, :, ns:ns + HEAD_DIM] = roped(_H_KS, _H_KSS, g).astype(nks_ref.dtype)
        nkw_ref[0, g] = roped(_H_KW, _H_KWS, g).astype(nkw_ref.dtype)

    pt = _dot_nt(wt_ref[...], hn)
    for g in range(NSA_KV_HEADS):
        lo = g * HEAD_DIM
        write_value_chunks(nvs_ref, g, _ones_row_pad(pt[_T_VS + lo:_T_VS + lo + HEAD_DIM]))
        write_value_chunks(nvw_ref, g, _ones_row_pad(pt[_T_VW + lo:_T_VW + lo + HEAD_DIM]))
    gate_ref[0] = jax.nn.sigmoid(pt[_T_GATE:_T_GATE + GATE_ROWS] + gb_ref[...])

    p = proj(_H_SBQ, _N_HEAD_COLS)
    for h in range(SB_HEADS):
        sbq_ref[0, h] = (head(p, h) * (scale * LOG2_E)).astype(sbq_ref.dtype)
        sbk_ref[0, h] = head(p, SB_HEADS + h).astype(sbk_ref.dtype)
        write_value_chunks(sbv_ref, h, pt[_T_SBV + h * HEAD_DIM:_T_SBV + (h + 1) * HEAD_DIM])


def _proj(x, layer, g, w_ext, w_t, qn, wuq, wuqs, kvn, wuk, wuv, gb, tabs):
    b, s, _ = x.shape
    ts = min(PROJ_ROWS, s)
    tv = min(TOKEN_CHUNK, ts)
    cq, sq, ck, sk, c64, s64, onehot = tabs
    full = functools.partial(_layer_spec, layer=layer)

    def tab(a):
        return pl.BlockSpec((ts, a.shape[1]), lambda bi, i: (i, 0))

    def heads(n, d):
        return (pl.BlockSpec((1, n, ts, d), lambda bi, i: (bi, 0, i, 0)),
                jax.ShapeDtypeStruct((b, n, s, d), _MXU))

    def values_t(n):
        return (pl.BlockSpec((1, n, ts // tv, HEAD_DIM + ONES_PAD, tv), lambda bi, i: (bi, 0, i, 0, 0)),
                jax.ShapeDtypeStruct((b, n, s // tv, HEAD_DIM + ONES_PAD, tv), _MXU))

    def chunked():
        return (pl.BlockSpec((1, NSA_KV_HEADS, ts // CMP_STRIDE, CMP_STRIDE * HEAD_DIM), lambda bi, i: (bi, 0, i, 0)),
                jax.ShapeDtypeStruct((b, NSA_KV_HEADS, s // CMP_STRIDE, CMP_STRIDE * HEAD_DIM), _MXU))

    outs = [heads(MLA_HEADS, LANES), heads(MLA_HEADS, LANES), values_t(MLA_HEADS), heads(NSA_HEADS, HEAD_DIM),
            chunked(), chunked(), heads(NSA_KV_HEADS, onehot.shape[1] + HEAD_DIM), values_t(NSA_KV_HEADS),
            heads(NSA_KV_HEADS, HEAD_DIM), values_t(NSA_KV_HEADS),
            (pl.BlockSpec((1, GATE_ROWS, ts), lambda bi, i: (bi, 0, i)), jax.ShapeDtypeStruct((b, GATE_ROWS, s), _F32)),
            heads(SB_HEADS, HEAD_DIM), heads(SB_HEADS, HEAD_DIM),
            (pl.BlockSpec((1, SB_HEADS, ts // SB_KEY_BLOCK, HEAD_DIM, SB_KEY_BLOCK), lambda bi, i: (bi, 0, i, 0, 0)),
             jax.ShapeDtypeStruct((b, SB_HEADS, s // SB_KEY_BLOCK, HEAD_DIM, SB_KEY_BLOCK), _MXU))]
    return pl.pallas_call(
        _proj_kernel,
        grid=(b, s // ts),
        in_specs=[pl.BlockSpec((1, ts, D_MODEL), lambda bi, i: (bi, i, 0)), full(g), full(w_ext), full(w_t), full(qn),
                  full(wuq), full(wuqs), full(kvn), full(wuk), full(wuv), full(gb),
                  tab(cq), tab(sq), tab(ck), tab(sk), tab(c64), tab(s64), tab(onehot)],
        out_specs=[o[0] for o in outs],
        out_shape=[o[1] for o in outs],
        scratch_shapes=[pltpu.VMEM((ts, HEAD_DIM), _F32)],
        compiler_params=_params("parallel", "parallel"),
        name="proj",
    )(x, g, w_ext, w_t, qn, wuq, wuqs, kvn, wuk, wuv, gb, cq, sq, ck, sk, c64, s64, onehot)


def _ones_row_pad(vt):
    first = lax.broadcasted_iota(jnp.int32, (ONES_PAD, vt.shape[1]), 0) == 0
    return jnp.concatenate([vt, jnp.where(first, 1.0, 0.0).astype(vt.dtype)], axis=0)


def _softmax_step_t(carry, st, vt_chunks):
    m, acc = carry
    m_new = jnp.maximum(m, jnp.max(st, axis=0, keepdims=True))
    alpha = jnp.exp2(m - m_new)
    pt = jnp.exp2(st - m_new).astype(_MXU)
    n = st.shape[0] // len(vt_chunks)
    pv = sum(jnp.dot(vt, pt[c * n:(c + 1) * n], preferred_element_type=_F32) for c, vt in enumerate(vt_chunks))
    return m_new, alpha * acc + pv


def _softmax_init_t(d, cols):
    return (jnp.full((1, cols), M_FLOOR, _F32), jnp.zeros((d + ONES_PAD, cols), _F32))


def _softmax_finish_t(carry, d):
    _, acc = carry
    return acc[:d] * (1.0 / acc[d:d + 1])


def _two_chain_sweep(n_full, qk, soft, init):
    def body(j, carry, diag=False):
        c0, c1 = carry
        qk(0, j)
        c1 = soft(1, j, c1, diag)
        qk(1, jnp.zeros_like(j) if diag else j + 1)
        c0 = soft(0, j, c0, diag)
        return c0, c1

    def unrolled(i, carry):
        for u in range(SWEEP_UNROLL):
            carry = body(SWEEP_UNROLL * i + u, carry)
        return carry

    qk(1, n_full)
    carry = body(n_full, init, True)
    trips = n_full // SWEEP_UNROLL
    carry = lax.fori_loop(0, trips, unrolled, carry)
    return lax.fori_loop(SWEEP_UNROLL * trips, n_full, body, carry)


def _mla_kernel(q_ref, k_ref, vt_ref, o_ref, s0_ref, s1_ref, *, t, nsub):
    qi = pl.program_id(2)
    s_refs = (s0_ref, s1_ref)

    def qk(hh, j):
        off = pl.multiple_of(j * t, t)
        s_refs[hh][...] = _dot_nt(k_ref[0, hh, pl.ds(off, t), :], q_ref[0, hh])

    def soft(hh, j, carry, diag):
        st = s_refs[hh][...]
        if diag:
            key = lax.broadcasted_iota(jnp.int32, (t, t), 0)
            qry = lax.broadcasted_iota(jnp.int32, (t, t), 1)
            st = jnp.where(key <= qry, st, NEG_INF)
        return _softmax_step_t(carry, st, [vt_ref[0, hh, j * nsub + c] for c in range(nsub)])

    carry = _two_chain_sweep(qi, qk, soft, tuple(_softmax_init_t(MLA_V, t) for _ in range(2)))
    ot = jnp.concatenate([_softmax_finish_t(c, MLA_V) for c in carry], axis=0)
    o_ref[0] = ot.T.astype(o_ref.dtype)


def _mla_attention(q, k, vt):
    b, h, s, _ = q.shape
    tv = vt.shape[-1]
    dv = vt.shape[-2]
    t = min(512, s)
    assert h % 2 == 0 and 2 * MLA_V == LANES and t % tv == 0 and s % t == 0
    return pl.pallas_call(
        functools.partial(_mla_kernel, t=t, nsub=t // tv),
        grid=(b, h // 2, s // t),
        in_specs=[pl.BlockSpec((1, 2, t, LANES), lambda bi, hi, i: (bi, hi, i, 0)),
                  pl.BlockSpec((1, 2, s, LANES), lambda bi, hi, i: (bi, hi, 0, 0)),
                  pl.BlockSpec((1, 2, s // tv, dv, tv), lambda bi, hi, i: (bi, hi, 0, 0, 0))],
        out_specs=pl.BlockSpec((1, t, LANES), lambda bi, hi, i: (bi, i, hi)),
        out_shape=jax.ShapeDtypeStruct((b, s, h * MLA_V), _MXU),
        scratch_shapes=[pltpu.VMEM((t, t), _F32), pltpu.VMEM((t, t), _F32)],
        compiler_params=_params("parallel", "parallel", "arbitrary"),
        name="mla_attn",
    )(q, k, vt)


def _sb_kernel(q_ref, k_ref, vt_ref, u_ref, o_ref, *scratch_refs, tq, tk):
    qi = pl.program_id(1)
    u = u_ref[...]
    n_heads = q_ref.shape[1]
    scratch = [scratch_refs[6 * hh:6 * (hh + 1)] for hh in range(n_heads)]
    per_tile = tq // tk

    def step(j, carry, key_offset=None):
        diag = key_offset is not None
        off = pl.multiple_of(j * tk, tk)
        if diag:
            key = key_offset + lax.broadcasted_iota(jnp.int32, (tk, tq), 0)
            qry = lax.broadcasted_iota(jnp.int32, (tk, tq), 1)
            strict = key < qry

        def logits(hh):
            z_ref, _, _, _, _, _ = scratch[hh]
            z_ref[...] = _dot_nt(k_ref[0, hh, pl.ds(off, tk), :], q_ref[0, hh])

        def log_terms(hh):
            z_ref, lb_ref, hi_ref, lo_ref, _, _ = scratch[hh]
            z = z_ref[...]
            log_beta = jnp.minimum(z, 0.0) - jnp.log2(1.0 + jnp.exp2(-jnp.abs(z)))
            log_rem = log_beta - z
            if diag:
                log_rem = jnp.where(strict, log_rem, 0.0)
            hi = log_rem.astype(_MXU)
            lb_ref[...] = log_beta
            hi_ref[...] = hi
            lo_ref[...] = (log_rem - hi.astype(_F32)).astype(_MXU)
            return log_rem[0:1, :]

        def suffix_sums(hh):
            _, _, hi_ref, lo_ref, sfx_ref, _ = scratch[hh]
            sfx_ref[...] = (jnp.dot(u, hi_ref[...], preferred_element_type=_F32)
                            + jnp.dot(u, lo_ref[...], preferred_element_type=_F32))

        def weights(hh, first_rem):
            _, lb_ref, _, _, sfx_ref, a_ref = scratch[hh]
            rem = carry[hh][0]
            suffix = sfx_ref[...]
            a = jnp.exp2(lb_ref[...] + suffix + rem)
            if diag:
                a = jnp.where(strict, a, 0.0)
            a_ref[...] = a.astype(_MXU)
            return rem + suffix[0:1, :] + first_rem

        def values(hh):
            a_ref = scratch[hh][5]
            return carry[hh][1] + jnp.dot(vt_ref[0, hh, j], a_ref[...], preferred_element_type=_F32)

        heads = range(n_heads)
        for hh in heads:
            logits(hh)
        first = []
        for hh in heads:
            first.append(log_terms(hh))
            suffix_sums(hh)
        rems = [weights(hh, first[hh]) for hh in heads]
        alive = jnp.max(functools.reduce(jnp.maximum, rems)) > F32_EXP2_ZERO
        return alive, tuple((rems[hh], values(hh)) for hh in heads)

    carry = tuple((jnp.zeros((1, tq), _F32), jnp.zeros((HEAD_DIM, tq), _F32)) for _ in range(n_heads))
    first = qi * per_tile
    for i in reversed(range(per_tile)):
        _, carry = step(first + i, carry, key_offset=i * tk)
    has_past = first > 0
    carry = tuple((jnp.where(has_past, rem, NEG_INF), acc) for rem, acc in carry)
    alive, carry = step(jnp.maximum(first - 1, 0), carry)

    def earlier(c):
        return (c[0] - 1,) + step(c[0], c[2])

    _, _, carry = lax.while_loop(lambda c: jnp.logical_and(c[0] >= 0, c[1]), earlier, (first - 2, alive, carry))
    o_ref[0] = jnp.concatenate([acc for _, acc in carry], axis=0).T.astype(o_ref.dtype)


def _sb_attention(q, k, vt):
    b, h, s, d = q.shape
    tk = vt.shape[-1]
    tq = tk
    assert (h * d) % LANES == 0 and s % tq == 0 and tq % tk == 0
    idx = np.arange(tk)
    u = jnp.asarray(idx[None, :] > idx[:, None], _MXU)
    return pl.pallas_call(
        functools.partial(_sb_kernel, tq=tq, tk=tk),
        grid=(b, s // tq),
        in_specs=[pl.BlockSpec((1, h, tq, d), lambda bi, i: (bi, 0, i, 0)),
                  pl.BlockSpec((1, h, s, d), lambda bi, i: (bi, 0, 0, 0)),
                  pl.BlockSpec((1, h, s // tk, d, tk), lambda bi, i: (bi, 0, 0, 0, 0)),
                  pl.BlockSpec((tk, tk), lambda bi, i: (0, 0))],
        out_specs=pl.BlockSpec((1, tq, h * d), lambda bi, i: (bi, i, 0)),
        out_shape=jax.ShapeDtypeStruct((b, s, h * d), _MXU),
        scratch_shapes=[pltpu.VMEM((tk, tq), dt) for _ in range(h) for dt in (_F32, _F32, _MXU, _MXU, _F32, _MXU)],
        compiler_params=_params("parallel", "arbitrary"),
        name="sb_attn",
    )(q, k, vt, u)


def _compress_kernel(xk_ref, xv_ref, w1k_ref, w2k_ref, pk_ref, w1v_ref, w2v_ref, pv_ref, ok_ref, ov_ref):
    def hidden(x_ref, w1_ref, p_ref):
        x = x_ref[0, 0]
        n = x.shape[0]
        first = jnp.dot(x, w1_ref[0], preferred_element_type=_F32)
        second = jnp.dot(x, w1_ref[1], preferred_element_type=_F32)
        pos = _dot(p_ref[0], w1_ref[0]) + _dot(p_ref[1], w1_ref[1])
        hid = first + pltpu.roll(second, n - 1, 0) + pos[0:1]
        return 0.5 * hid * (1.0 + jnp.tanh(math.sqrt(2.0 / math.pi) * (hid + 0.044715 * hid * hid * hid)))

    ok_ref[0, 0] = _dot(hidden(xk_ref, w1k_ref, pk_ref), w2k_ref[...]).astype(ok_ref.dtype)
    ov_ref[0, 0] = _dot_nt(w2v_ref[...], hidden(xv_ref, w1v_ref, pv_ref)).astype(ov_ref.dtype)


def _compress(xk, xv, layer, w1k, w2k, pk, w1v, w2v, pv):
    b, g, n, _ = xk.shape
    d = HEAD_DIM
    full = functools.partial(_layer_spec, layer=layer)

    xspec = pl.BlockSpec((1, 1, n, CMP_STRIDE * d), lambda bi, gi: (bi, gi, 0, 0))
    return pl.pallas_call(
        _compress_kernel,
        grid=(b, g),
        in_specs=[xspec, xspec, full(w1k), full(w2k), full(pk), full(w1v), full(w2v), full(pv)],
        out_specs=[pl.BlockSpec((1, 1, n, d), lambda bi, gi: (bi, gi, 0, 0)),
                   pl.BlockSpec((1, 1, d, n), lambda bi, gi: (bi, gi, 0, 0))],
        out_shape=[jax.ShapeDtypeStruct((b, g, n, d), _MXU), jax.ShapeDtypeStruct((b, g, d, n), _MXU)],
        compiler_params=_params("parallel", "parallel"),
        name="nsa_compress",
    )(xk, xv, w1k, w2k, pk, w1v, w2v, pv)


def _group_queries(q_ref, g, tq):
    return q_ref[0, g * NSA_GROUP:(g + 1) * NSA_GROUP].reshape(NSA_GROUP * tq, q_ref.shape[-1])


def _gated_heads(ot, gt_ref, g, branch, tq):
    out = []
    for r in range(NSA_GROUP):
        row = NSA_BRANCHES * (g * NSA_GROUP + r) + branch
        out.append(ot[:, r * tq:(r + 1) * tq] * gt_ref[0, row:row + 1, :])
    return out


def _cmp_kernel(q_ref, kc_ref, vct_ref, ov_ref, gt_ref, o_ref, qa_ref, s0_ref, s1_ref, *, tq, n_top):
    q0 = pl.program_id(1) * tq
    ncp = kc_ref.shape[2]
    ns = ov_ref.shape[0]
    lanes = NSA_GROUP * tq
    s_refs = (s0_ref, s1_ref)
    for g in range(NSA_KV_HEADS):
        s_refs[g][...] = _dot_nt(kc_ref[0, g], _group_queries(q_ref, g, tq))
    qpos = q0 + (lax.broadcasted_iota(jnp.int32, (1, lanes), 1) & (tq - 1))
    cmp_end = lax.broadcasted_iota(jnp.int32, (ncp, 1), 0) * CMP_STRIDE + (CMP_LEN - 1)
    visible = cmp_end <= qpos
    cur = jnp.right_shift(q0 + lax.broadcasted_iota(jnp.int32, (1, tq), 1), int(math.log2(SEL_LEN)))
    blk = lax.broadcasted_iota(jnp.int32, (ns, 1), 0)
    forced = (blk == 0) | (blk == cur) | (blk == cur - 1)
    future = blk > cur
    blk_f = blk.astype(_F32)
    heads = []
    scores = []
    for g in range(NSA_KV_HEADS):
        st = jnp.where(visible, s_refs[g][...], NEG_INF)
        e = jnp.exp2(st - jnp.max(st, axis=0, keepdims=True))
        inv = jnp.where(qpos >= CMP_LEN - 1, 1.0 / jnp.sum(e, axis=0, keepdims=True), 0.0)
        pt = e * inv
        heads += _gated_heads(_dot(vct_ref[0, g], pt), gt_ref, g, 0, tq)
        p_sum = sum(pt[:, r * tq:(r + 1) * tq] for r in range(NSA_GROUP))
        score = _dot_split_rhs(ov_ref[...], p_sum)
        scores.append(jnp.where(forced, FORCE_SCORE, jnp.where(future, -1.0, score)))
    o_ref[0] = jnp.concatenate(heads, axis=0).T.astype(o_ref.dtype)
    def select(rows):
        sc = [s[:rows] for s in scores]
        idx = blk_f[:rows]
        for _ in range(n_top):
            for g in range(NSA_KV_HEADS):
                top = jnp.max(sc[g], axis=0, keepdims=True)
                first = jnp.min(jnp.where(sc[g] == top, idx, float(ns)), axis=0, keepdims=True)
                sc[g] = jnp.where(idx == first, PICKED, sc[g])
        for g in range(NSA_KV_HEADS):
            sel_m1 = jnp.where(sc[g] < 0.5 * PICKED, 0.0, -1.0)
            if rows < ns:
                sel_m1 = jnp.concatenate([sel_m1, jnp.full((ns - rows, tq), -1.0, _F32)], axis=0)
            sel_m1 = sel_m1.T.astype(qa_ref.dtype)
            for h in range(g * NSA_GROUP, (g + 1) * NSA_GROUP):
                qa_ref[0, h, :, 0:ns] = sel_m1
                qa_ref[0, h, :, ns:ns + HEAD_DIM] = q_ref[0, h]

    visible_blocks = (q0 + tq) // SEL_LEN
    lower = 0
    for rows in sorted({min(ns, r) for r in TOPK_ROW_STEPS} | {ns}):
        in_range = visible_blocks > lower
        if rows < ns:
            in_range = jnp.logical_and(in_range, visible_blocks <= rows)
        pl.when(in_range)(functools.partial(select, rows))
        lower = rows


def _cmp_select(q, kc, vct, gates_t):
    b, h, s, d = q.shape
    g = kc.shape[1]
    ncp = kc.shape[2]
    ns = s // SEL_LEN
    n_top = min(SEL_TOPK, ns)
    tq = min(256, s)
    assert tq & (tq - 1) == 0 and g == 2
    c0 = np.arange(ncp)[:, None] * CMP_STRIDE
    n0 = np.arange(ns)[None, :] * SEL_LEN
    overlap = jnp.asarray(((c0 < n0 + SEL_LEN) & (c0 + CMP_LEN > n0)).T, _MXU)
    return pl.pallas_call(
        functools.partial(_cmp_kernel, tq=tq, n_top=n_top),
        grid=(b, s // tq),
        in_specs=[pl.BlockSpec((1, h, tq, d), lambda bi, i: (bi, 0, i, 0)),
                  pl.BlockSpec((1, g, ncp, d), lambda bi, i: (bi, 0, 0, 0)),
                  pl.BlockSpec((1, g, d, ncp), lambda bi, i: (bi, 0, 0, 0)),
                  pl.BlockSpec((ns, ncp), lambda bi, i: (0, 0)),
                  pl.BlockSpec((1, GATE_ROWS, tq), lambda bi, i: (bi, 0, i))],
        out_specs=[pl.BlockSpec((1, tq, h * d), lambda bi, i: (bi, i, 0)),
                   pl.BlockSpec((1, h, tq, ns + d), lambda bi, i: (bi, 0, i, 0))],
        out_shape=[jax.ShapeDtypeStruct((b, s, h * d), _MXU), jax.ShapeDtypeStruct((b, h, s, ns + d), _MXU)],
        scratch_shapes=[pltpu.VMEM((ncp, NSA_GROUP * tq), _F32) for _ in range(g)],
        compiler_params=_params("parallel", "arbitrary"),
        name="nsa_cmp_select",
    )(q, kc, vct, overlap, gates_t)


def _key_minus_query(keys, tq):
    return jnp.asarray(np.arange(keys)[:, None] - np.arange(NSA_GROUP * tq)[None, :] % tq, jnp.int32)


def _sel_kernel(q_ref, k_ref, vt_ref, gt_ref, rel_ref, o_ref, s0_ref, s1_ref, *, tq, tk, nsub):
    q0 = pl.program_id(1) * tq
    last = (q0 + tq - 1) // tk
    lanes = NSA_GROUP * tq
    s_refs = (s0_ref, s1_ref)

    def qk(g, j):
        off = pl.multiple_of(j * tk, tk)
        s_refs[g][...] = _dot_nt(k_ref[0, g, pl.ds(off, tk), :], _group_queries(q_ref, g, tq))

    def soft(g, j, carry, causal):
        st = s_refs[g][...]
        if causal:
            st = jnp.where(rel_ref[...] <= q0 - j * tk, st, NEG_INF)
        return _softmax_step_t(carry, st, [vt_ref[0, g, j * nsub + c] for c in range(nsub)])

    init = tuple(_softmax_init_t(HEAD_DIM, lanes) for _ in range(NSA_KV_HEADS))
    carry = _two_chain_sweep(last, qk, soft, init)
    heads = []
    for g in range(NSA_KV_HEADS):
        heads += _gated_heads(_softmax_finish_t(carry[g], HEAD_DIM), gt_ref, g, 1, tq)
    o_ref[0] = jnp.concatenate(heads, axis=0).T.astype(o_ref.dtype)


def _sel_attention(q, k, vt, gates_t):
    b, h, s, da = q.shape
    g = k.shape[1]
    d = HEAD_DIM
    tv = vt.shape[-1]
    tq = min(256, s)
    tk = min(512, s)
    assert tq & (tq - 1) == 0 and s % tk == 0 and tk % tv == 0 and g == 2
    return pl.pallas_call(
        functools.partial(_sel_kernel, tq=tq, tk=tk, nsub=tk // tv),
        grid=(b, s // tq),
        in_specs=[pl.BlockSpec((1, h, tq, da), lambda bi, i: (bi, 0, i, 0)),
                  pl.BlockSpec((1, g, s, da), lambda bi, i: (bi, 0, 0, 0)),
                  pl.BlockSpec((1, g) + vt.shape[2:], lambda bi, i: (bi, 0, 0, 0, 0)),
                  pl.BlockSpec((1, GATE_ROWS, tq), lambda bi, i: (bi, 0, i)),
                  pl.BlockSpec((tk, NSA_GROUP * tq), lambda bi, i: (0, 0))],
        out_specs=pl.BlockSpec((1, tq, h * d), lambda bi, i: (bi, i, 0)),
        out_shape=jax.ShapeDtypeStruct((b, s, h * d), _MXU),
        scratch_shapes=[pltpu.VMEM((tk, NSA_GROUP * tq), _F32) for _ in range(g)],
        compiler_params=_params("parallel", "arbitrary"),
        name="nsa_selected",
    )(q, k, vt, gates_t, _key_minus_query(tk, tq))


def _win_kernel(q_ref, k_ref, vt_ref, gt_ref, rel_ref, o_ref, *s_refs, tq, subs, span, tv):
    lanes = NSA_GROUP * tq
    rel = rel_ref[...]
    chains = [(sub, g) for sub in range(subs) for g in range(NSA_KV_HEADS)]

    def origin(sub):
        q0 = (pl.program_id(1) * subs + sub) * tq
        return q0, pl.multiple_of(jnp.maximum(q0 - WINDOW, 0), tq)

    for c, (sub, g) in enumerate(chains):
        _, start = origin(sub)
        q = q_ref[0, g * NSA_GROUP:(g + 1) * NSA_GROUP, sub * tq:(sub + 1) * tq].reshape(lanes, q_ref.shape[-1])
        s_refs[c][...] = _dot_nt(k_ref[0, g, pl.ds(start, span), :], q)
    for sub in range(subs):
        q0, start = origin(sub)
        offset = q0 - start
        heads = []
        for g in range(NSA_KV_HEADS):
            st = jnp.where(rel <= offset, s_refs[chains.index((sub, g))][...], NEG_INF)
            st = jnp.where(rel > offset - WINDOW, st, NEG_INF)
            carry = _softmax_step_t(_softmax_init_t(HEAD_DIM, lanes), st,
                                    [vt_ref[0, g, start // tv + c] for c in range(span // tv)])
            ot = _softmax_finish_t(carry, HEAD_DIM)
            for r in range(NSA_GROUP):
                row = NSA_BRANCHES * (g * NSA_GROUP + r) + 2
                heads.append(ot[:, r * tq:(r + 1) * tq] * gt_ref[0, row:row + 1, sub * tq:(sub + 1) * tq])
        o_ref[0, sub * tq:(sub + 1) * tq, :] = jnp.concatenate(heads, axis=0).T.astype(o_ref.dtype)


def _win_attention(q, k, vt, gates_t):
    b, h, s, d = q.shape
    g = k.shape[1]
    tv = vt.shape[-1]
    tq = min(256, s)
    subs = 2 if s % (2 * tq) == 0 else 1
    span = WINDOW + tq
    assert tq & (tq - 1) == 0 and s >= span and tq % tv == 0 and WINDOW % tv == 0 and g == NSA_KV_HEADS
    return pl.pallas_call(
        functools.partial(_win_kernel, tq=tq, subs=subs, span=span, tv=tv),
        grid=(b, s // (subs * tq)),
        in_specs=[pl.BlockSpec((1, h, subs * tq, d), lambda bi, i: (bi, 0, i, 0)),
                  pl.BlockSpec((1, g, s, d), lambda bi, i: (bi, 0, 0, 0)),
                  pl.BlockSpec((1, g) + vt.shape[2:], lambda bi, i: (bi, 0, 0, 0, 0)),
                  pl.BlockSpec((1, GATE_ROWS, subs * tq), lambda bi, i: (bi, 0, i)),
                  pl.BlockSpec((span, NSA_GROUP * tq), lambda bi, i: (0, 0))],
        out_specs=pl.BlockSpec((1, subs * tq, h * d), lambda bi, i: (bi, i, 0)),
        out_shape=jax.ShapeDtypeStruct((b, s, h * d), _MXU),
        scratch_shapes=[pltpu.VMEM((span, NSA_GROUP * tq), _F32) for _ in range(subs * g)],
        compiler_params=_params("parallel", "arbitrary"),
        name="nsa_window",
    )(q, k, vt, gates_t, _key_minus_query(span, tq))


def _out_kernel(x_ref, mla_ref, cmp_ref, sel_ref, win_ref, sb_ref, w_ref, o_ref):
    def w_rows(first_head, n_heads):
        return w_ref[first_head * HEAD_DIM:(first_head + n_heads) * HEAD_DIM, :]

    acc = x_ref[0] + jnp.dot(mla_ref[0], w_rows(0, MLA_HEADS), preferred_element_type=_F32)
    nsa = cmp_ref[0].astype(_F32) + sel_ref[0].astype(_F32) + win_ref[0].astype(_F32)
    acc = acc + _dot(nsa, w_rows(MLA_HEADS, NSA_HEADS))
    o_ref[0] = acc + jnp.dot(sb_ref[0], w_rows(MLA_HEADS + NSA_HEADS, SB_HEADS), preferred_element_type=_F32)


def _out_proj(x, o_mla, o_cmp, o_sel, o_win, o_sb, layer, w_heads):
    b, s, _ = x.shape
    ts = min(512, s)

    def rows(a):
        return pl.BlockSpec((1, ts, a.shape[2]), lambda bi, i: (bi, i, 0))

    xspec = pl.BlockSpec((1, ts, D_MODEL), lambda bi, i: (bi, i, 0))
    return pl.pallas_call(
        _out_kernel,
        grid=(b, s // ts),
        in_specs=[xspec, rows(o_mla), rows(o_cmp), rows(o_sel), rows(o_win), rows(o_sb),
                  _layer_spec(w_heads, layer)],
        out_specs=xspec,
        out_shape=jax.ShapeDtypeStruct(x.shape, _F32),
        compiler_params=_params("parallel", "parallel"),
        name="out_proj",
    )(x, o_mla, o_cmp, o_sel, o_win, o_sb, w_heads)


def _gather_cols(w, idx):
    idx = np.asarray(idx)
    cuts = [0] + [i for i in range(1, len(idx)) if idx[i] != idx[i - 1] + (idx[i - 1] >= 0)] + [len(idx)]
    pieces = []
    for a, b in zip(cuts[:-1], cuts[1:]):
        if idx[a] < 0:
            pieces.append(jnp.zeros(w.shape[:-1] + (b - a,), _MXU))
        else:
            pieces.append(w[..., int(idx[a]):int(idx[a]) + b - a].astype(_MXU))
    return jnp.concatenate(pieces, axis=-1)


def _swap_halves(rot):
    return (np.arange(rot) + rot // 2) % rot


def _w_in_index():
    idx = np.full((_N_HEAD_COLS * HEAD_DIM,), -1, np.int64)

    def put(col, src):
        src = np.asarray(src)
        idx[col:col + len(src)] = src

    def put_head(pos, src):
        put(pos * HEAD_DIM, src)

    put(_S_CQ * LANES, _O_CQ + np.arange(MLA_Q_LORA))
    put(_S_CKV * LANES, _O_CKV + np.arange(MLA_KV_LORA))
    put(_S_KR * LANES + MLA_NOPE, _O_KR + np.arange(MLA_ROPE))
    put(_S_KRS * LANES + MLA_NOPE, _O_KR + _swap_halves(MLA_ROPE))
    for h in range(NSA_HEADS):
        put_head(_H_NQ + h, _O_NQ + h * HEAD_DIM + np.arange(HEAD_DIM))
        put_head(_H_NQS + h, _O_NQ + h * HEAD_DIM + _swap_halves(PARTIAL_ROT))
    for hk, hks, ok in ((_H_KC, _H_KCS, _O_NKC), (_H_KS, _H_KSS, _O_NKS), (_H_KW, _H_KWS, _O_NKW)):
        for g in range(NSA_KV_HEADS):
            put_head(hk + g, ok + g * HEAD_DIM + np.arange(HEAD_DIM))
            put_head(hks + g, ok + g * HEAD_DIM + _swap_halves(PARTIAL_ROT))
    for g in range(NSA_KV_HEADS):
        put_head(_H_VC + g, _O_NVC + g * HEAD_DIM + np.arange(HEAD_DIM))
    for h in range(SB_HEADS):
        put_head(_H_SBQ + h, _O_SBQ + h * HEAD_DIM + np.arange(HEAD_DIM))
        put_head(_H_SBK + h, _O_SBK + h * HEAD_DIM + np.arange(HEAD_DIM))
    return idx


def _mla_up_index():
    qd = MLA_NOPE + MLA_ROPE
    kd = MLA_NOPE + MLA_V
    uq = np.full((MLA_HEADS * LANES,), -1, np.int64)
    uqs = uq.copy()
    uk = uq.copy()
    for h in range(MLA_HEADS):
        uq[h * LANES:h * LANES + qd] = h * qd + np.arange(qd)
        uqs[h * LANES + MLA_NOPE:h * LANES + qd] = h * qd + MLA_NOPE + _swap_halves(MLA_ROPE)
        uk[h * LANES:h * LANES + MLA_NOPE] = h * kd + np.arange(MLA_NOPE)
    return uq, uqs, uk


def _transposed_weights(w_in, gate_bias):
    width = NSA_KV_HEADS * HEAD_DIM
    gate_rows = jnp.pad(w_in[..., _O_GATE:_O_GATE + N_GATES], ((0, 0), (0, 0), (0, _T_SBV - _T_GATE - N_GATES)))
    rows = jnp.concatenate([w_in[..., _O_NVS:_O_NVS + width], w_in[..., _O_NVW:_O_NVW + width], gate_rows,
                            w_in[..., _O_SBV:_O_SBV + SB_HEADS * HEAD_DIM]], axis=-1)
    bias = jnp.pad(gate_bias, ((0, 0), (0, GATE_ROWS - N_GATES)))[..., None]
    return jnp.swapaxes(rows, -1, -2).astype(_MXU), bias


def _rope_tables(s):
    pos = np.arange(s, dtype=np.float64)

    def cs(rot):
        half = rot // 2
        ang = pos[:, None] * (ROPE_THETA ** (-np.arange(half, dtype=np.float64) / half))[None, :]
        c, sn = np.cos(ang), np.sin(ang)
        return np.concatenate([c, c], axis=1), np.concatenate([-sn, sn], axis=1)

    c, sn = cs(MLA_ROPE)
    pad = np.zeros((s, LANES - MLA_NOPE - MLA_ROPE))
    ck = np.concatenate([np.ones((s, MLA_NOPE)), c, pad], axis=1)
    sk = np.concatenate([np.zeros((s, MLA_NOPE)), sn, pad], axis=1)
    q_scale = (MLA_NOPE + MLA_ROPE) ** -0.5 * LOG2_E
    c, sn = cs(PARTIAL_ROT)
    c64 = np.concatenate([c, np.ones((s, HEAD_DIM - PARTIAL_ROT))], axis=1)
    s64 = np.concatenate([sn, np.zeros((s, HEAD_DIM - PARTIAL_ROT))], axis=1)
    ns = s // SEL_LEN
    onehot = (np.arange(s)[:, None] // SEL_LEN == np.arange(ns)[None, :]) * -NEG_INF
    tables = [jnp.asarray(t, _F32) for t in (ck * q_scale, sk * q_scale, ck, sk, c64, s64)]
    return tables + [jnp.asarray(onehot, _MXU)]


def kernel(x, ffn1_norm, ffn1_w_gate, ffn1_w_up, ffn1_w_down, mix_norm, w_in, mla_q_norm, mla_w_uq, mla_kv_norm,
           mla_w_ukv, nsa_gate_bias, nsa_cmp_pos_k, nsa_cmp_w1_k, nsa_cmp_w2_k, nsa_cmp_pos_v, nsa_cmp_w1_v,
           nsa_cmp_w2_v, w_out, ffn2_norm, ffn2_w_gate, ffn2_w_up, ffn2_w_down, final_norm):
    b, s, d = x.shape
    depth = w_in.shape[0]
    tabs = _rope_tables(s)
    in_idx = _w_in_index()
    uq_idx, uqs_idx, uk_idx = _mla_up_index()
    half = CMP_LEN * HEAD_DIM // 2
    fg = final_norm.reshape(1, d)

    def row(p):
        return p[:, None, :]

    def cmp_weights(w1, w2, pos, transpose_out):
        pos = jnp.broadcast_to(pos.reshape(depth, 2, 1, half), (depth, 2, 8, half)).astype(_MXU)
        w2 = jnp.swapaxes(w2, -1, -2) if transpose_out else w2
        return w1.reshape(depth, 2, half, CMP_HIDDEN).astype(_MXU), w2.astype(_MXU), pos

    ffn1 = [w.astype(_MXU) for w in (ffn1_w_gate, ffn1_w_up, ffn1_w_down)]
    ffn2 = [w.astype(_MXU) for w in (ffn2_w_gate, ffn2_w_up, ffn2_w_down)]
    w_t, gate_bias = _transposed_weights(w_in, nsa_gate_bias)
    wuv_t = mla_w_ukv.reshape(depth, MLA_KV_LORA, MLA_HEADS, 2, MLA_V)[:, :, :, 1].transpose(0, 2, 3, 1).astype(_MXU)
    proj_params = (row(mix_norm), _gather_cols(w_in, in_idx), w_t,
                   row(mla_q_norm), _gather_cols(mla_w_uq, uq_idx), _gather_cols(mla_w_uq, uqs_idx),
                   row(mla_kv_norm), _gather_cols(mla_w_ukv, uk_idx), wuv_t, gate_bias)
    cmp_params = (cmp_weights(nsa_cmp_w1_k, nsa_cmp_w2_k, nsa_cmp_pos_k, False)
                  + cmp_weights(nsa_cmp_w1_v, nsa_cmp_w2_v, nsa_cmp_pos_v, True))
    w_out = w_out.astype(_MXU)
    ffn1_norm, ffn2_norm = row(ffn1_norm), row(ffn2_norm)

    for l in range(depth):
        x2d = _ffn(x.reshape(b * s, d), ffn1_norm, *ffn1, l, fg, False)
        x = x2d.reshape(b, s, d)
        (mq, mk, mvt, nq, nkc, nvc, nks, nvst, nkw, nvwt, gates_t, sbq, sbk, sbv) = _proj(x, l, *proj_params, tabs)
        o_mla = _mla_attention(mq, mk, mvt)
        kc, vct = _compress(nkc, nvc, l, *cmp_params)
        o_cmp, q_sel = _cmp_select(nq, kc, vct, gates_t)
        o_sel = _sel_attention(q_sel, nks, nvst, gates_t)
        o_win = _win_attention(nq, nkw, nvwt, gates_t)
        o_sb = _sb_attention(sbq, sbk, sbv)
        x = _out_proj(x, o_mla, o_cmp, o_sel, o_win, o_sb, l, w_out)
        x2d = _ffn(x.reshape(b * s, d), ffn2_norm, *ffn2, l, fg, l == depth - 1)
        x = x2d.reshape(b, s, d)
    return x
```

```python
import functools
import math

import numpy as np
import jax
import jax.numpy as jnp
from jax import lax
from jax.experimental import pallas as pl
from jax.experimental.pallas import tpu as pltpu

D_MODEL = 1024
HEAD_DIM = 64
MLA_HEADS = 6
MLA_NOPE = 64
MLA_ROPE = 32
MLA_V = 64
MLA_Q_LORA = 256
MLA_KV_LORA = 128
NSA_HEADS = 6
NSA_KV_HEADS = 2
NSA_GROUP = NSA_HEADS // NSA_KV_HEADS
NSA_BRANCHES = 3
CMP_LEN = 32
CMP_STRIDE = 16
CMP_HIDDEN = 128
SEL_LEN = 64
SEL_TOPK = 16
WINDOW = 512
SB_HEADS = 4
D_FF = 2816
ROPE_THETA = 500000.0
PARTIAL_ROT = HEAD_DIM // 4
EPS = 1e-6
NEG_INF = -1e30
M_FLOOR = 0.1 * NEG_INF
FORCE_SCORE = 1e4
PICKED = -3e38
F32_EXP2_ZERO = -151.0
LOG2_E = math.log2(math.e)
N_GATES = NSA_HEADS * NSA_BRANCHES

LANES = 128
FFN_CHUNK = 256
SWEEP_UNROLL = 4
TOPK_ROW_STEPS = (32, 64)
SB_KEY_BLOCK = 256
PROJ_ROWS = 512
TOKEN_CHUNK = 256
ONES_PAD = 16
VMEM_LIMIT = 56 * 1024 * 1024

_MXU = jnp.bfloat16
_F32 = jnp.float32

_IN_WIDTHS = (MLA_Q_LORA, MLA_KV_LORA, MLA_ROPE, NSA_HEADS * HEAD_DIM) + (NSA_KV_HEADS * HEAD_DIM,) * 6 + (
    N_GATES, SB_HEADS * HEAD_DIM, SB_HEADS * HEAD_DIM, SB_HEADS * HEAD_DIM)
_IN_OFF = np.concatenate([[0], np.cumsum(_IN_WIDTHS)])
(_O_CQ, _O_CKV, _O_KR, _O_NQ, _O_NKC, _O_NVC, _O_NKS, _O_NVS, _O_NKW, _O_NVW, _O_GATE, _O_SBQ, _O_SBK,
 _O_SBV) = [int(v) for v in _IN_OFF[:-1]]

_S_CQ, _S_CKV, _S_KR, _S_KRS = 0, 2, 3, 4
_H_NQ, _H_NQS = 10, 16
_H_KC, _H_KCS, _H_VC = 22, 24, 26
_H_KS, _H_KSS = 28, 30
_H_KW, _H_KWS = 32, 34
_H_SBQ, _H_SBK = 36, 40
_N_HEAD_COLS = 44
_T_VS, _T_VW, _T_GATE = 0, NSA_KV_HEADS * HEAD_DIM, 2 * NSA_KV_HEADS * HEAD_DIM
GATE_ROWS = 24
_T_SBV = _T_GATE + 2 * ONES_PAD
_T_ROWS = _T_SBV + SB_HEADS * HEAD_DIM


def _dot(a, b):
    return jnp.dot(a.astype(_MXU), b.astype(_MXU), preferred_element_type=_F32)


def _dot_nt(a, b):
    return lax.dot_general(a.astype(_MXU), b.astype(_MXU), (((1,), (1,)), ((), ())),
                           preferred_element_type=_F32)


def _dot_split_rhs(a, b):
    hi = b.astype(_MXU)
    lo = (b - hi.astype(_F32)).astype(_MXU)
    return (jnp.dot(a, hi, preferred_element_type=_F32) + jnp.dot(a, lo, preferred_element_type=_F32))


def _rms(x, g):
    return x * lax.rsqrt(jnp.mean(x * x, axis=-1, keepdims=True) + EPS) * g


def _params(*sem):
    return pltpu.CompilerParams(dimension_semantics=sem, vmem_limit_bytes=VMEM_LIMIT)


def _layer_spec(a, layer):
    return pl.BlockSpec((None,) + a.shape[1:], lambda *_: (layer,) + (0,) * (a.ndim - 1))


def _ffn_kernel(x_ref, g_ref, wg_ref, wu_ref, wd_ref, fg_ref, o_ref, act_ref, *, final_norm):
    x = x_ref[...]
    h = _rms(x, g_ref[...]).astype(_MXU)
    tf = act_ref.shape[1]
    for c0 in range(0, tf, FFN_CHUNK):
        c1 = min(c0 + FFN_CHUNK, tf)
        gate = jnp.dot(h, wg_ref[:, c0:c1], preferred_element_type=_F32)
        up = jnp.dot(h, wu_ref[:, c0:c1], preferred_element_type=_F32)
        act_ref[:, c0:c1] = (gate * jax.nn.sigmoid(gate) * up).astype(act_ref.dtype)
    y = x + 0.5 * jnp.dot(act_ref[...], wd_ref[...], preferred_element_type=_F32)
    if final_norm:
        y = _rms(y, fg_ref[...])
    o_ref[...] = y


def _ffn(x2d, g, wg, wu, wd, layer, fg, final_norm):
    rows = x2d.shape[0]
    tm = min(1024, rows)

    def resident(a):
        return pl.BlockSpec((None,) + a.shape[1:], lambda i: (layer, 0, 0), pipeline_mode=pl.Buffered(1))

    return pl.pallas_call(
        functools.partial(_ffn_kernel, final_norm=final_norm),
        grid=(rows // tm,),
        in_specs=[
            pl.BlockSpec((tm, D_MODEL), lambda i: (i, 0)),
            _layer_spec(g, layer),
            resident(wg), resident(wu), resident(wd),
            pl.BlockSpec((1, D_MODEL), lambda i: (0, 0)),
        ],
        out_specs=pl.BlockSpec((tm, D_MODEL), lambda i: (i, 0)),
        out_shape=jax.ShapeDtypeStruct((rows, D_MODEL), _F32),
        scratch_shapes=[pltpu.VMEM((tm, D_FF), _MXU)],
        compiler_params=_params("parallel"),
        name="ffn",
    )(x2d, g, wg, wu, wd, fg)


def _proj_kernel(x_ref, g_ref, w_ref, wt_ref, qn_ref, wuq_ref, wuqs_ref, kvn_ref, wuk_ref, wuv_ref, gb_ref,
                 cq_ref, sq_ref, ck_ref, sk_ref, c64_ref, s64_ref, oh_ref,
                 mq_ref, mk_ref, mv_ref, nq_ref, nkc_ref, nvc_ref, nks_ref, nvs_ref, nkw_ref, nvw_ref,
                 gate_ref, sbq_ref, sbk_ref, sbv_ref, stage_ref):
    hn = _rms(x_ref[0], g_ref[...]).astype(_MXU)

    def proj(h0, h1):
        return jnp.dot(hn, w_ref[:, h0 * HEAD_DIM:h1 * HEAD_DIM], preferred_element_type=_F32)

    def slot(p, s):
        return p[:, s * LANES:(s + 1) * LANES]

    def head(p, i):
        return p[:, i * HEAD_DIM:(i + 1) * HEAD_DIM]

    def write_value_chunks(o_ref, h, vt):
        width = o_ref.shape[-1]
        for c in range(o_ref.shape[2]):
            o_ref[0, h, c] = vt[:, c * width:(c + 1) * width].astype(o_ref.dtype)

    p = proj(0, _H_NQ)
    cq = _rms(p[:, :MLA_Q_LORA], qn_ref[...])
    ckv = _rms(slot(p, _S_CKV), kvn_ref[...])
    q = _dot(cq, wuq_ref[...])
    q_partner = _dot(cq, wuqs_ref[...])
    kpe = slot(p, _S_KR) * ck_ref[...] + slot(p, _S_KRS) * sk_ref[...]
    kn = _dot(ckv, wuk_ref[...])
    for h in range(MLA_HEADS):
        mq_ref[0, h] = (slot(q, h) * cq_ref[...] + slot(q_partner, h) * sq_ref[...]).astype(mq_ref.dtype)
        mk_ref[0, h] = (slot(kn, h) + kpe).astype(mk_ref.dtype)
        write_value_chunks(mv_ref, h, _ones_row_pad(_dot_nt(wuv_ref[h], ckv)))

    c64 = c64_ref[...]
    s64 = s64_ref[...]
    scale = HEAD_DIM ** -0.5

    p = proj(_H_NQ, _H_KC)
    for h in range(NSA_HEADS):
        nq_ref[0, h] = ((head(p, h) * c64 + head(p, NSA_HEADS + h) * s64) * (scale * LOG2_E)).astype(nq_ref.dtype)

    p = proj(_H_KC, _H_SBQ)
    base = _H_KC
    ns = oh_ref.shape[-1]

    def roped(hk, hks, g):
        return head(p, hk - base + g) * c64 + head(p, hks - base + g) * s64

    def write_chunked(o_ref, g, val):
        stage_ref[...] = val
        for t in range(CMP_STRIDE):
            piece = stage_ref[pl.ds(t, val.shape[0] // CMP_STRIDE, stride=CMP_STRIDE), :]
            o_ref[0, g, :, t * HEAD_DIM:(t + 1) * HEAD_DIM] = piece.astype(o_ref.dtype)

    for g in range(NSA_KV_HEADS):
        write_chunked(nkc_ref, g, roped(_H_KC, _H_KCS, g))
        write_chunked(nvc_ref, g, head(p, _H_VC - base + g))
        nks_ref[0, g, :, 0:ns] = oh_ref[...]
        nks_ref[0, g, :, ns:ns + HEAD_DIM] = roped(_H_KS, _H_KSS, g).astype(nks_ref.dtype)
        nkw_ref[0, g] = roped(_H_KW, _H_KWS, g).astype(nkw_ref.dtype)

    pt = _dot_nt(wt_ref[...], hn)
    for g in range(NSA_KV_HEADS):
        lo = g * HEAD_DIM
        write_value_chunks(nvs_ref, g, _ones_row_pad(pt[_T_VS + lo:_T_VS + lo + HEAD_DIM]))
        write_value_chunks(nvw_ref, g, _ones_row_pad(pt[_T_VW + lo:_T_VW + lo + HEAD_DIM]))
    gate_ref[0] = jax.nn.sigmoid(pt[_T_GATE:_T_GATE + GATE_ROWS] + gb_ref[...])

    p = proj(_H_SBQ, _N_HEAD_COLS)
    for h in range(SB_HEADS):
        sbq_ref[0, h] = (head(p, h) * (scale * LOG2_E)).astype(sbq_ref.dtype)
        sbk_ref[0, h] = head(p, SB_HEADS + h).astype(sbk_ref.dtype)
        write_value_chunks(sbv_ref, h, pt[_T_SBV + h * HEAD_DIM:_T_SBV + (h + 1) * HEAD_DIM])


def _proj(x, layer, g, w_ext, w_t, qn, wuq, wuqs, kvn, wuk, wuv, gb, tabs):
    b, s, _ = x.shape
    ts = min(PROJ_ROWS, s)
    tv = min(TOKEN_CHUNK, ts)
    cq, sq, ck, sk, c64, s64, onehot = tabs
    full = functools.partial(_layer_spec, layer=layer)

    def tab(a):
        return pl.BlockSpec((ts, a.shape[1]), lambda bi, i: (i, 0))

    def heads(n, d):
        return (pl.BlockSpec((1, n, ts, d), lambda bi, i: (bi, 0, i, 0)),
                jax.ShapeDtypeStruct((b, n, s, d), _MXU))

    def values_t(n):
        return (pl.BlockSpec((1, n, ts // tv, HEAD_DIM + ONES_PAD, tv), lambda bi, i: (bi, 0, i, 0, 0)),
                jax.ShapeDtypeStruct((b, n, s // tv, HEAD_DIM + ONES_PAD, tv), _MXU))

    def chunked():
        return (pl.BlockSpec((1, NSA_KV_HEADS, ts // CMP_STRIDE, CMP_STRIDE * HEAD_DIM), lambda bi, i: (bi, 0, i, 0)),
                jax.ShapeDtypeStruct((b, NSA_KV_HEADS, s // CMP_STRIDE, CMP_STRIDE * HEAD_DIM), _MXU))

    outs = [heads(MLA_HEADS, LANES), heads(MLA_HEADS, LANES), values_t(MLA_HEADS), heads(NSA_HEADS, HEAD_DIM),
            chunked(), chunked(), heads(NSA_KV_HEADS, onehot.shape[1] + HEAD_DIM), values_t(NSA_KV_HEADS),
            heads(NSA_KV_HEADS, HEAD_DIM), values_t(NSA_KV_HEADS),
            (pl.BlockSpec((1, GATE_ROWS, ts), lambda bi, i: (bi, 0, i)), jax.ShapeDtypeStruct((b, GATE_ROWS, s), _F32)),
            heads(SB_HEADS, HEAD_DIM), heads(SB_HEADS, HEAD_DIM),
            (pl.BlockSpec((1, SB_HEADS, ts // SB_KEY_BLOCK, HEAD_DIM, SB_KEY_BLOCK), lambda bi, i: (bi, 0, i, 0, 0)),
             jax.ShapeDtypeStruct((b, SB_HEADS, s // SB_KEY_BLOCK, HEAD_DIM, SB_KEY_BLOCK), _MXU))]
    return pl.pallas_call(
        _proj_kernel,
        grid=(b, s // ts),
        in_specs=[pl.BlockSpec((1, ts, D_MODEL), lambda bi, i: (bi, i, 0)), full(g), full(w_ext), full(w_t), full(qn),
                  full(wuq), full(wuqs), full(kvn), full(wuk), full(wuv), full(gb),
                  tab(cq), tab(sq), tab(ck), tab(sk), tab(c64), tab(s64), tab(onehot)],
        out_specs=[o[0] for o in outs],
        out_shape=[o[1] for o in outs],
        scratch_shapes=[pltpu.VMEM((ts, HEAD_DIM), _F32)],
        compiler_params=_params("parallel", "parallel"),
        name="proj",
    )(x, g, w_ext, w_t, qn, wuq, wuqs, kvn, wuk, wuv, gb, cq, sq, ck, sk, c64, s64, onehot)


def _ones_row_pad(vt):
    first = lax.broadcasted_iota(jnp.int32, (ONES_PAD, vt.shape[1]), 0) == 0
    return jnp.concatenate([vt, jnp.where(first, 1.0, 0.0).astype(vt.dtype)], axis=0)


def _softmax_step_t(carry, st, vt_chunks):
    m, acc = carry
    m_new = jnp.maximum(m, jnp.max(st, axis=0, keepdims=True))
    alpha = jnp.exp2(m - m_new)
    pt = jnp.exp2(st - m_new).astype(_MXU)
    n = st.shape[0] // len(vt_chunks)
    pv = sum(jnp.dot(vt, pt[c * n:(c + 1) * n], preferred_element_type=_F32) for c, vt in enumerate(vt_chunks))
    return m_new, alpha * acc + pv


def _softmax_init_t(d, cols):
    return (jnp.full((1, cols), M_FLOOR, _F32), jnp.zeros((d + ONES_PAD, cols), _F32))


def _softmax_finish_t(carry, d):
    _, acc = carry
    return acc[:d] * (1.0 / acc[d:d + 1])


def _two_chain_sweep(n_full, qk, soft, init):
    def body(j, carry, diag=False):
        c0, c1 = carry
        qk(0, j)
        c1 = soft(1, j, c1, diag)
        qk(1, jnp.zeros_like(j) if diag else j + 1)
        c0 = soft(0, j, c0, diag)
        return c0, c1

    def unrolled(i, carry):
        for u in range(SWEEP_UNROLL):
            carry = body(SWEEP_UNROLL * i + u, carry)
        return carry

    qk(1, n_full)
    carry = body(n_full, init, True)
    trips = n_full // SWEEP_UNROLL
    carry = lax.fori_loop(0, trips, unrolled, carry)
    return lax.fori_loop(SWEEP_UNROLL * trips, n_full, body, carry)


def _mla_kernel(q_ref, k_ref, vt_ref, o_ref, s0_ref, s1_ref, *, t, nsub):
    qi = pl.program_id(2)
    s_refs = (s0_ref, s1_ref)

    def qk(hh, j):
        off = pl.multiple_of(j * t, t)
        s_refs[hh][...] = _dot_nt(k_ref[0, hh, pl.ds(off, t), :], q_ref[0, hh])

    def soft(hh, j, carry, diag):
        st = s_refs[hh][...]
        if diag:
            key = lax.broadcasted_iota(jnp.int32, (t, t), 0)
            qry = lax.broadcasted_iota(jnp.int32, (t, t), 1)
            st = jnp.where(key <= qry, st, NEG_INF)
        return _softmax_step_t(carry, st, [vt_ref[0, hh, j * nsub + c] for c in range(nsub)])

    carry = _two_chain_sweep(qi, qk, soft, tuple(_softmax_init_t(MLA_V, t) for _ in range(2)))
    ot = jnp.concatenate([_softmax_finish_t(c, MLA_V) for c in carry], axis=0)
    o_ref[0] = ot.T.astype(o_ref.dtype)


def _mla_attention(q, k, vt):
    b, h, s, _ = q.shape
    tv = vt.shape[-1]
    dv = vt.shape[-2]
    t = min(512, s)
    assert h % 2 == 0 and 2 * MLA_V == LANES and t % tv == 0 and s % t == 0
    return pl.pallas_call(
        functools.partial(_mla_kernel, t=t, nsub=t // tv),
        grid=(b, h // 2, s // t),
        in_specs=[pl.BlockSpec((1, 2, t, LANES), lambda bi, hi, i: (bi, hi, i, 0)),
                  pl.BlockSpec((1, 2, s, LANES), lambda bi, hi, i: (bi, hi, 0, 0)),
                  pl.BlockSpec((1, 2, s // tv, dv, tv), lambda bi, hi, i: (bi, hi, 0, 0, 0))],
        out_specs=pl.BlockSpec((1, t, LANES), lambda bi, hi, i: (bi, i, hi)),
        out_shape=jax.ShapeDtypeStruct((b, s, h * MLA_V), _MXU),
        scratch_shapes=[pltpu.VMEM((t, t), _F32), pltpu.VMEM((t, t), _F32)],
        compiler_params=_params("parallel", "parallel", "arbitrary"),
        name="mla_attn",
    )(q, k, vt)


def _sb_kernel(q_ref, k_ref, vt_ref, u_ref, o_ref, *scratch_refs, tq, tk):
    qi = pl.program_id(1)
    u = u_ref[...]
    n_heads = q_ref.shape[1]
    scratch = [scratch_refs[6 * hh:6 * (hh + 1)] for hh in range(n_heads)]
    per_tile = tq // tk

    def step(j, carry, key_offset=None):
        diag = key_offset is not None
        off = pl.multiple_of(j * tk, tk)
        if diag:
            key = key_offset + lax.broadcasted_iota(jnp.int32, (tk, tq), 0)
            qry = lax.broadcasted_iota(jnp.int32, (tk, tq), 1)
            strict = key < qry

        def logits(hh):
            z_ref, _, _, _, _, _ = scratch[hh]
            z_ref[...] = _dot_nt(k_ref[0, hh, pl.ds(off, tk), :], q_ref[0, hh])

        def log_terms(hh):
            z_ref, lb_ref, hi_ref, lo_ref, _, _ = scratch[hh]
            z = z_ref[...]
            log_beta = jnp.minimum(z, 0.0) - jnp.log2(1.0 + jnp.exp2(-jnp.abs(z)))
            log_rem = log_beta - z
            if diag:
                log_rem = jnp.where(strict, log_rem, 0.0)
            hi = log_rem.astype(_MXU)
            lb_ref[...] = log_beta
            hi_ref[...] = hi
            lo_ref[...] = (log_rem - hi.astype(_F32)).astype(_MXU)
            return log_rem[0:1, :]

        def suffix_sums(hh):
            _, _, hi_ref, lo_ref, sfx_ref, _ = scratch[hh]
            sfx_ref[...] = (jnp.dot(u, hi_ref[...], preferred_element_type=_F32)
                            + jnp.dot(u, lo_ref[...], preferred_element_type=_F32))

        def weights(hh, first_rem):
            _, lb_ref, _, _, sfx_ref, a_ref = scratch[hh]
            rem = carry[hh][0]
            suffix = sfx_ref[...]
            a = jnp.exp2(lb_ref[...] + suffix + rem)
            if diag:
                a = jnp.where(strict, a, 0.0)
            a_ref[...] = a.astype(_MXU)
            return rem + suffix[0:1, :] + first_rem

        def values(hh):
            a_ref = scratch[hh][5]
            return carry[hh][1] + jnp.dot(vt_ref[0, hh, j], a_ref[...], preferred_element_type=_F32)

        heads = range(n_heads)
        for hh in heads:
            logits(hh)
        first = []
        for hh in heads:
            first.append(log_terms(hh))
            suffix_sums(hh)
        rems = [weights(hh, first[hh]) for hh in heads]
        alive = jnp.max(functools.reduce(jnp.maximum, rems)) > F32_EXP2_ZERO
        return alive, tuple((rems[hh], values(hh)) for hh in heads)

    carry = tuple((jnp.zeros((1, tq), _F32), jnp.zeros((HEAD_DIM, tq), _F32)) for _ in range(n_heads))
    first = qi * per_tile
    for i in reversed(range(per_tile)):
        _, carry = step(first + i, carry, key_offset=i * tk)
    has_past = first > 0
    carry = tuple((jnp.where(has_past, rem, NEG_INF), acc) for rem, acc in carry)
    alive, carry = step(jnp.maximum(first - 1, 0), carry)

    def earlier(c):
        return (c[0] - 1,) + step(c[0], c[2])

    _, _, carry = lax.while_loop(lambda c: jnp.logical_and(c[0] >= 0, c[1]), earlier, (first - 2, alive, carry))
    o_ref[0] = jnp.concatenate([acc for _, acc in carry], axis=0).T.astype(o_ref.dtype)


def _sb_attention(q, k, vt):
    b, h, s, d = q.shape
    tk = vt.shape[-1]
    tq = tk
    assert (h * d) % LANES == 0 and s % tq == 0 and tq % tk == 0
    idx = np.arange(tk)
    u = jnp.asarray(idx[None, :] > idx[:, None], _MXU)
    return pl.pallas_call(
        functools.partial(_sb_kernel, tq=tq, tk=tk),
        grid=(b, s // tq),
        in_specs=[pl.BlockSpec((1, h, tq, d), lambda bi, i: (bi, 0, i, 0)),
                  pl.BlockSpec((1, h, s, d), lambda bi, i: (bi, 0, 0, 0)),
                  pl.BlockSpec((1, h, s // tk, d, tk), lambda bi, i: (bi, 0, 0, 0, 0)),
                  pl.BlockSpec((tk, tk), lambda bi, i: (0, 0))],
        out_specs=pl.BlockSpec((1, tq, h * d), lambda bi, i: (bi, i, 0)),
        out_shape=jax.ShapeDtypeStruct((b, s, h * d), _MXU),
        scratch_shapes=[pltpu.VMEM((tk, tq), dt) for _ in range(h) for dt in (_F32, _F32, _MXU, _MXU, _F32, _MXU)],
        compiler_params=_params("parallel", "arbitrary"),
        name="sb_attn",
    )(q, k, vt, u)


def _compress_kernel(xk_ref, xv_ref, w1k_ref, w2k_ref, pk_ref, w1v_ref, w2v_ref, pv_ref, ok_ref, ov_ref):
    def hidden(x_ref, w1_ref, p_ref):
        x = x_ref[0, 0]
        n = x.shape[0]
        first = jnp.dot(x, w1_ref[0], preferred_element_type=_F32)
        second = jnp.dot(x, w1_ref[1], preferred_element_type=_F32)
        pos = _dot(p_ref[0], w1_ref[0]) + _dot(p_ref[1], w1_ref[1])
        hid = first + pltpu.roll(second, n - 1, 0) + pos[0:1]
        return 0.5 * hid * (1.0 + jnp.tanh(math.sqrt(2.0 / math.pi) * (hid + 0.044715 * hid * hid * hid)))

    ok_ref[0, 0] = _dot(hidden(xk_ref, w1k_ref, pk_ref), w2k_ref[...]).astype(ok_ref.dtype)
    ov_ref[0, 0] = _dot_nt(w2v_ref[...], hidden(xv_ref, w1v_ref, pv_ref)).astype(ov_ref.dtype)


def _compress(xk, xv, layer, w1k, w2k, pk, w1v, w2v, pv):
    b, g, n, _ = xk.shape
    d = HEAD_DIM
    full = functools.partial(_layer_spec, layer=layer)

    xspec = pl.BlockSpec((1, 1, n, CMP_STRIDE * d), lambda bi, gi: (bi, gi, 0, 0))
    return pl.pallas_call(
        _compress_kernel,
        grid=(b, g),
        in_specs=[xspec, xspec, full(w1k), full(w2k), full(pk), full(w1v), full(w2v), full(pv)],
        out_specs=[pl.BlockSpec((1, 1, n, d), lambda bi, gi: (bi, gi, 0, 0)),
                   pl.BlockSpec((1, 1, d, n), lambda bi, gi: (bi, gi, 0, 0))],
        out_shape=[jax.ShapeDtypeStruct((b, g, n, d), _MXU), jax.ShapeDtypeStruct((b, g, d, n), _MXU)],
        compiler_params=_params("parallel", "parallel"),
        name="nsa_compress",
    )(xk, xv, w1k, w2k, pk, w1v, w2v, pv)


def _group_queries(q_ref, g, tq):
    return q_ref[0, g * NSA_GROUP:(g + 1) * NSA_GROUP].reshape(NSA_GROUP * tq, q_ref.shape[-1])


def _gated_heads(ot, gt_ref, g, branch, tq):
    out = []
    for r in range(NSA_GROUP):
        row = NSA_BRANCHES * (g * NSA_GROUP + r) + branch
        out.append(ot[:, r * tq:(r + 1) * tq] * gt_ref[0, row:row + 1, :])
    return out


def _cmp_kernel(q_ref, kc_ref, vct_ref, ov_ref, gt_ref, o_ref, qa_ref, s0_ref, s1_ref, *, tq, n_top):
    q0 = pl.program_id(1) * tq
    ncp = kc_ref.shape[2]
    ns = ov_ref.shape[0]
    lanes = NSA_GROUP * tq
    s_refs = (s0_ref, s1_ref)
    for g in range(NSA_KV_HEADS):
        s_refs[g][...] = _dot_nt(kc_ref[0, g], _group_queries(q_ref, g, tq))
    qpos = q0 + (lax.broadcasted_iota(jnp.int32, (1, lanes), 1) & (tq - 1))
    cmp_end = lax.broadcasted_iota(jnp.int32, (ncp, 1), 0) * CMP_STRIDE + (CMP_LEN - 1)
    visible = cmp_end <= qpos
    cur = jnp.right_shift(q0 + lax.broadcasted_iota(jnp.int32, (1, tq), 1), int(math.log2(SEL_LEN)))
    blk = lax.broadcasted_iota(jnp.int32, (ns, 1), 0)
    forced = (blk == 0) | (blk == cur) | (blk == cur - 1)
    future = blk > cur
    blk_f = blk.astype(_F32)
    heads = []
    scores = []
    for g in range(NSA_KV_HEADS):
        st = jnp.where(visible, s_refs[g][...], NEG_INF)
        e = jnp.exp2(st - jnp.max(st, axis=0, keepdims=True))
        inv = jnp.where(qpos >= CMP_LEN - 1, 1.0 / jnp.sum(e, axis=0, keepdims=True), 0.0)
        pt = e * inv
        heads += _gated_heads(_dot(vct_ref[0, g], pt), gt_ref, g, 0, tq)
        p_sum = sum(pt[:, r * tq:(r + 1) * tq] for r in range(NSA_GROUP))
        score = _dot_split_rhs(ov_ref[...], p_sum)
        scores.append(jnp.where(forced, FORCE_SCORE, jnp.where(future, -1.0, score)))
    o_ref[0] = jnp.concatenate(heads, axis=0).T.astype(o_ref.dtype)
    def select(rows):
        sc = [s[:rows] for s in scores]
        idx = blk_f[:rows]
        for _ in range(n_top):
            for g in range(NSA_KV_HEADS):
                top = jnp.max(sc[g], axis=0, keepdims=True)
                first = jnp.min(jnp.where(sc[g] == top, idx, float(ns)), axis=0, keepdims=True)
                sc[g] = jnp.where(idx == first, PICKED, sc[g])
        for g in range(NSA_KV_HEADS):
            sel_m1 = jnp.where(sc[g] < 0.5 * PICKED, 0.0, -1.0)
            if rows < ns:
                sel_m1 = jnp.concatenate([sel_m1, jnp.full((ns - rows, tq), -1.0, _F32)], axis=0)
            sel_m1 = sel_m1.T.astype(qa_ref.dtype)
            for h in range(g * NSA_GROUP, (g + 1) * NSA_GROUP):
                qa_ref[0, h, :, 0:ns] = sel_m1
                qa_ref[0, h, :, ns:ns + HEAD_DIM] = q_ref[0, h]

    visible_blocks = (q0 + tq) // SEL_LEN
    lower = 0
    for rows in sorted({min(ns, r) for r in TOPK_ROW_STEPS} | {ns}):
        in_range = visible_blocks > lower
        if rows < ns:
            in_range = jnp.logical_and(in_range, visible_blocks <= rows)
        pl.when(in_range)(functools.partial(select, rows))
        lower = rows


def _cmp_select(q, kc, vct, gates_t):
    b, h, s, d = q.shape
    g = kc.shape[1]
    ncp = kc.shape[2]
    ns = s // SEL_LEN
    n_top = min(SEL_TOPK, ns)
    tq = min(256, s)
    assert tq & (tq - 1) == 0 and g == 2
    c0 = np.arange(ncp)[:, None] * CMP_STRIDE
    n0 = np.arange(ns)[None, :] * SEL_LEN
    overlap = jnp.asarray(((c0 < n0 + SEL_LEN) & (c0 + CMP_LEN > n0)).T, _MXU)
    return pl.pallas_call(
        functools.partial(_cmp_kernel, tq=tq, n_top=n_top),
        grid=(b, s // tq),
        in_specs=[pl.BlockSpec((1, h, tq, d), lambda bi, i: (bi, 0, i, 0)),
                  pl.BlockSpec((1, g, ncp, d), lambda bi, i: (bi, 0, 0, 0)),
                  pl.BlockSpec((1, g, d, ncp), lambda bi, i: (bi, 0, 0, 0)),
                  pl.BlockSpec((ns, ncp), lambda bi, i: (0, 0)),
                  pl.BlockSpec((1, GATE_ROWS, tq), lambda bi, i: (bi, 0, i))],
        out_specs=[pl.BlockSpec((1, tq, h * d), lambda bi, i: (bi, i, 0)),
                   pl.BlockSpec((1, h, tq, ns + d), lambda bi, i: (bi, 0, i, 0))],
        out_shape=[jax.ShapeDtypeStruct((b, s, h * d), _MXU), jax.ShapeDtypeStruct((b, h, s, ns + d), _MXU)],
        scratch_shapes=[pltpu.VMEM((ncp, NSA_GROUP * tq), _F32) for _ in range(g)],
        compiler_params=_params("parallel", "arbitrary"),
        name="nsa_cmp_select",
    )(q, kc, vct, overlap, gates_t)


def _key_minus_query(keys, tq):
    return jnp.asarray(np.arange(keys)[:, None] - np.arange(NSA_GROUP * tq)[None, :] % tq, jnp.int32)


def _sel_kernel(q_ref, k_ref, vt_ref, gt_ref, rel_ref, o_ref, s0_ref, s1_ref, *, tq, tk, nsub):
    q0 = pl.program_id(1) * tq
    last = (q0 + tq - 1) // tk
    lanes = NSA_GROUP * tq
    s_refs = (s0_ref, s1_ref)

    def qk(g, j):
        off = pl.multiple_of(j * tk, tk)
        s_refs[g][...] = _dot_nt(k_ref[0, g, pl.ds(off, tk), :], _group_queries(q_ref, g, tq))

    def soft(g, j, carry, causal):
        st = s_refs[g][...]
        if causal:
            st = jnp.where(rel_ref[...] <= q0 - j * tk, st, NEG_INF)
        return _softmax_step_t(carry, st, [vt_ref[0, g, j * nsub + c] for c in range(nsub)])

    init = tuple(_softmax_init_t(HEAD_DIM, lanes) for _ in range(NSA_KV_HEADS))
    carry = _two_chain_sweep(last, qk, soft, init)
    heads = []
    for g in range(NSA_KV_HEADS):
        heads += _gated_heads(_softmax_finish_t(carry[g], HEAD_DIM), gt_ref, g, 1, tq)
    o_ref[0] = jnp.concatenate(heads, axis=0).T.astype(o_ref.dtype)


def _sel_attention(q, k, vt, gates_t):
    b, h, s, da = q.shape
    g = k.shape[1]
    d = HEAD_DIM
    tv = vt.shape[-1]
    tq = min(256, s)
    tk = min(512, s)
    assert tq & (tq - 1) == 0 and s % tk == 0 and tk % tv == 0 and g == 2
    return pl.pallas_call(
        functools.partial(_sel_kernel, tq=tq, tk=tk, nsub=tk // tv),
        grid=(b, s // tq),
        in_specs=[pl.BlockSpec((1, h, tq, da), lambda bi, i: (bi, 0, i, 0)),
                  pl.BlockSpec((1, g, s, da), lambda bi, i: (bi, 0, 0, 0)),
                  pl.BlockSpec((1, g) + vt.shape[2:], lambda bi, i: (bi, 0, 0, 0, 0)),
                  pl.BlockSpec((1, GATE_ROWS, tq), lambda bi, i: (bi, 0, i)),
                  pl.BlockSpec((tk, NSA_GROUP * tq), lambda bi, i: (0, 0))],
        out_specs=pl.BlockSpec((1, tq, h * d), lambda bi, i: (bi, i, 0)),
        out_shape=jax.ShapeDtypeStruct((b, s, h * d), _MXU),
        scratch_shapes=[pltpu.VMEM((tk, NSA_GROUP * tq), _F32) for _ in range(g)],
        compiler_params=_params("parallel", "arbitrary"),
        name="nsa_selected",
    )(q, k, vt, gates_t, _key_minus_query(tk, tq))


def _win_kernel(q_ref, k_ref, vt_ref, gt_ref, rel_ref, o_ref, *s_refs, tq, subs, span, tv):
    lanes = NSA_GROUP * tq
    rel = rel_ref[...]
    chains = [(sub, g) for sub in range(subs) for g in range(NSA_KV_HEADS)]

    def origin(sub):
        q0 = (pl.program_id(1) * subs + sub) * tq
        return q0, pl.multiple_of(jnp.maximum(q0 - WINDOW, 0), tq)

    for c, (sub, g) in enumerate(chains):
        _, start = origin(sub)
        q = q_ref[0, g * NSA_GROUP:(g + 1) * NSA_GROUP, sub * tq:(sub + 1) * tq].reshape(lanes, q_ref.shape[-1])
        s_refs[c][...] = _dot_nt(k_ref[0, g, pl.ds(start, span), :], q)
    for sub in range(subs):
        q0, start = origin(sub)
        offset = q0 - start
        heads = []
        for g in range(NSA_KV_HEADS):
            st = jnp.where(rel <= offset, s_refs[chains.index((sub, g))][...], NEG_INF)
            st = jnp.where(rel > offset - WINDOW, st, NEG_INF)
            carry = _softmax_step_t(_softmax_init_t(HEAD_DIM, lanes), st,
                                    [vt_ref[0, g, start // tv + c] for c in range(span // tv)])
            ot = _softmax_finish_t(carry, HEAD_DIM)
            for r in range(NSA_GROUP):
                row = NSA_BRANCHES * (g * NSA_GROUP + r) + 2
                heads.append(ot[:, r * tq:(r + 1) * tq] * gt_ref[0, row:row + 1, sub * tq:(sub + 1) * tq])
        o_ref[0, sub * tq:(sub + 1) * tq, :] = jnp.concatenate(heads, axis=0).T.astype(o_ref.dtype)


def _win_attention(q, k, vt, gates_t):
    b, h, s, d = q.shape
    g = k.shape[1]
    tv = vt.shape[-1]
    tq = min(256, s)
    subs = 2 if s % (2 * tq) == 0 else 1
    span = WINDOW + tq
    assert tq & (tq - 1) == 0 and s >= span and tq % tv == 0 and WINDOW % tv == 0 and g == NSA_KV_HEADS
    return pl.pallas_call(
        functools.partial(_win_kernel, tq=tq, subs=subs, span=span, tv=tv),
        grid=(b, s // (subs * tq)),
        in_specs=[pl.BlockSpec((1, h, subs * tq, d), lambda bi, i: (bi, 0, i, 0)),
                  pl.BlockSpec((1, g, s, d), lambda bi, i: (bi, 0, 0, 0)),
                  pl.BlockSpec((1, g) + vt.shape[2:], lambda bi, i: (bi, 0, 0, 0, 0)),
                  pl.BlockSpec((1, GATE_ROWS, subs * tq), lambda bi, i: (bi, 0, i)),
                  pl.BlockSpec((span, NSA_GROUP * tq), lambda bi, i: (0, 0))],
        out_specs=pl.BlockSpec((1, subs * tq, h * d), lambda bi, i: (bi, i, 0)),
        out_shape=jax.ShapeDtypeStruct((b, s, h * d), _MXU),
        scratch_shapes=[pltpu.VMEM((span, NSA_GROUP * tq), _F32) for _ in range(subs * g)],
        compiler_params=_params("parallel", "arbitrary"),
        name="nsa_window",
    )(q, k, vt, gates_t, _key_minus_query(span, tq))


def _out_kernel(x_ref, mla_ref, cmp_ref, sel_ref, win_ref, sb_ref, w_ref, o_ref):
    def w_rows(first_head, n_heads):
        return w_ref[first_head * HEAD_DIM:(first_head + n_heads) * HEAD_DIM, :]

    acc = x_ref[0] + jnp.dot(mla_ref[0], w_rows(0, MLA_HEADS), preferred_element_type=_F32)
    nsa = cmp_ref[0].astype(_F32) + sel_ref[0].astype(_F32) + win_ref[0].astype(_F32)
    acc = acc + _dot(nsa, w_rows(MLA_HEADS, NSA_HEADS))
    o_ref[0] = acc + jnp.dot(sb_ref[0], w_rows(MLA_HEADS + NSA_HEADS, SB_HEADS), preferred_element_type=_F32)


def _out_proj(x, o_mla, o_cmp, o_sel, o_win, o_sb, layer, w_heads):
    b, s, _ = x.shape
    ts = min(512, s)

    def rows(a):
        return pl.BlockSpec((1, ts, a.shape[2]), lambda bi, i: (bi, i, 0))

    xspec = pl.BlockSpec((1, ts, D_MODEL), lambda bi, i: (bi, i, 0))
    return pl.pallas_call(
        _out_kernel,
        grid=(b, s // ts),
        in_specs=[xspec, rows(o_mla), rows(o_cmp), rows(o_sel), rows(o_win), rows(o_sb),
                  _layer_spec(w_heads, layer)],
        out_specs=xspec,
        out_shape=jax.ShapeDtypeStruct(x.shape, _F32),
        compiler_params=_params("parallel", "parallel"),
        name="out_proj",
    )(x, o_mla, o_cmp, o_sel, o_win, o_sb, w_heads)


def _gather_cols(w, idx):
    idx = np.asarray(idx)
    cuts = [0] + [i for i in range(1, len(idx)) if idx[i] != idx[i - 1] + (idx[i - 1] >= 0)] + [len(idx)]
    pieces = []
    for a, b in zip(cuts[:-1], cuts[1:]):
        if idx[a] < 0:
            pieces.append(jnp.zeros(w.shape[:-1] + (b - a,), _MXU))
        else:
            pieces.append(w[..., int(idx[a]):int(idx[a]) + b - a].astype(_MXU))
    return jnp.concatenate(pieces, axis=-1)


def _swap_halves(rot):
    return (np.arange(rot) + rot // 2) % rot


def _w_in_index():
    idx = np.full((_N_HEAD_COLS * HEAD_DIM,), -1, np.int64)

    def put(col, src):
        src = np.asarray(src)
        idx[col:col + len(src)] = src

    def put_head(pos, src):
        put(pos * HEAD_DIM, src)

    put(_S_CQ * LANES, _O_CQ + np.arange(MLA_Q_LORA))
    put(_S_CKV * LANES, _O_CKV + np.arange(MLA_KV_LORA))
    put(_S_KR * LANES + MLA_NOPE, _O_KR + np.arange(MLA_ROPE))
    put(_S_KRS * LANES + MLA_NOPE, _O_KR + _swap_halves(MLA_ROPE))
    for h in range(NSA_HEADS):
        put_head(_H_NQ + h, _O_NQ + h * HEAD_DIM + np.arange(HEAD_DIM))
        put_head(_H_NQS + h, _O_NQ + h * HEAD_DIM + _swap_halves(PARTIAL_ROT))
    for hk, hks, ok in ((_H_KC, _H_KCS, _O_NKC), (_H_KS, _H_KSS, _O_NKS), (_H_KW, _H_KWS, _O_NKW)):
        for g in range(NSA_KV_HEADS):
            put_head(hk + g, ok + g * HEAD_DIM + np.arange(HEAD_DIM))
            put_head(hks + g, ok + g * HEAD_DIM + _swap_halves(PARTIAL_ROT))
    for g in range(NSA_KV_HEADS):
        put_head(_H_VC + g, _O_NVC + g * HEAD_DIM + np.arange(HEAD_DIM))
    for h in range(SB_HEADS):
        put_head(_H_SBQ + h, _O_SBQ + h * HEAD_DIM + np.arange(HEAD_DIM))
        put_head(_H_SBK + h, _O_SBK + h * HEAD_DIM + np.arange(HEAD_DIM))
    return idx


def _mla_up_index():
    qd = MLA_NOPE + MLA_ROPE
    kd = MLA_NOPE + MLA_V
    uq = np.full((MLA_HEADS * LANES,), -1, np.int64)
    uqs = uq.copy()
    uk = uq.copy()
    for h in range(MLA_HEADS):
        uq[h * LANES:h * LANES + qd] = h * qd + np.arange(qd)
        uqs[h * LANES + MLA_NOPE:h * LANES + qd] = h * qd + MLA_NOPE + _swap_halves(MLA_ROPE)
        uk[h * LANES:h * LANES + MLA_NOPE] = h * kd + np.arange(MLA_NOPE)
    return uq, uqs, uk


def _transposed_weights(w_in, gate_bias):
    width = NSA_KV_HEADS * HEAD_DIM
    gate_rows = jnp.pad(w_in[..., _O_GATE:_O_GATE + N_GATES], ((0, 0), (0, 0), (0, _T_SBV - _T_GATE - N_GATES)))
    rows = jnp.concatenate([w_in[..., _O_NVS:_O_NVS + width], w_in[..., _O_NVW:_O_NVW + width], gate_rows,
                            w_in[..., _O_SBV:_O_SBV + SB_HEADS * HEAD_DIM]], axis=-1)
    bias = jnp.pad(gate_bias, ((0, 0), (0, GATE_ROWS - N_GATES)))[..., None]
    return jnp.swapaxes(rows, -1, -2).astype(_MXU), bias


def _rope_tables(s):
    pos = np.arange(s, dtype=np.float64)

    def cs(rot):
        half = rot // 2
        ang = pos[:, None] * (ROPE_THETA ** (-np.arange(half, dtype=np.float64) / half))[None, :]
        c, sn = np.cos(ang), np.sin(ang)
        return np.concatenate([c, c], axis=1), np.concatenate([-sn, sn], axis=1)

    c, sn = cs(MLA_ROPE)
    pad = np.zeros((s, LANES - MLA_NOPE - MLA_ROPE))
    ck = np.concatenate([np.ones((s, MLA_NOPE)), c, pad], axis=1)
    sk = np.concatenate([np.zeros((s, MLA_NOPE)), sn, pad], axis=1)
    q_scale = (MLA_NOPE + MLA_ROPE) ** -0.5 * LOG2_E
    c, sn = cs(PARTIAL_ROT)
    c64 = np.concatenate([c, np.ones((s, HEAD_DIM - PARTIAL_ROT))], axis=1)
    s64 = np.concatenate([sn, np.zeros((s, HEAD_DIM - PARTIAL_ROT))], axis=1)
    ns = s // SEL_LEN
    onehot = (np.arange(s)[:, None] // SEL_LEN == np.arange(ns)[None, :]) * -NEG_INF
    tables = [jnp.asarray(t, _F32) for t in (ck * q_scale, sk * q_scale, ck, sk, c64, s64)]
    return tables + [jnp.asarray(onehot, _MXU)]


def kernel(x, ffn1_norm, ffn1_w_gate, ffn1_w_up, ffn1_w_down, mix_norm, w_in, mla_q_norm, mla_w_uq, mla_kv_norm,
           mla_w_ukv, nsa_gate_bias, nsa_cmp_pos_k, nsa_cmp_w1_k, nsa_cmp_w2_k, nsa_cmp_pos_v, nsa_cmp_w1_v,
           nsa_cmp_w2_v, w_out, ffn2_norm, ffn2_w_gate, ffn2_w_up, ffn2_w_down, final_norm):
    b, s, d = x.shape
    depth = w_in.shape[0]
    tabs = _rope_tables(s)
    in_idx = _w_in_index()
    uq_idx, uqs_idx, uk_idx = _mla_up_index()
    half = CMP_LEN * HEAD_DIM // 2
    fg = final_norm.reshape(1, d)

    def row(p):
        return p[:, None, :]

    def cmp_weights(w1, w2, pos, transpose_out):
        pos = jnp.broadcast_to(pos.reshape(depth, 2, 1, half), (depth, 2, 8, half)).astype(_MXU)
        w2 = jnp.swapaxes(w2, -1, -2) if transpose_out else w2
        return w1.reshape(depth, 2, half, CMP_HIDDEN).astype(_MXU), w2.astype(_MXU), pos

    ffn1 = [w.astype(_MXU) for w in (ffn1_w_gate, ffn1_w_up, ffn1_w_down)]
    ffn2 = [w.astype(_MXU) for w in (ffn2_w_gate, ffn2_w_up, ffn2_w_down)]
    w_t, gate_bias = _transposed_weights(w_in, nsa_gate_bias)
    wuv_t = mla_w_ukv.reshape(depth, MLA_KV_LORA, MLA_HEADS, 2, MLA_V)[:, :, :, 1].transpose(0, 2, 3, 1).astype(_MXU)
    proj_params = (row(mix_norm), _gather_cols(w_in, in_idx), w_t,
                   row(mla_q_norm), _gather_cols(mla_w_uq, uq_idx), _gather_cols(mla_w_uq, uqs_idx),
                   row(mla_kv_norm), _gather_cols(mla_w_ukv, uk_idx), wuv_t, gate_bias)
    cmp_params = (cmp_weights(nsa_cmp_w1_k, nsa_cmp_w2_k, nsa_cmp_pos_k, False)
                  + cmp_weights(nsa_cmp_w1_v, nsa_cmp_w2_v, nsa_cmp_pos_v, True))
    w_out = w_out.astype(_MXU)
    ffn1_norm, ffn2_norm = row(ffn1_norm), row(ffn2_norm)

    for l in range(depth):
        x2d = _ffn(x.reshape(b * s, d), ffn1_norm, *ffn1, l, fg, False)
        x = x2d.reshape(b, s, d)
        (mq, mk, mvt, nq, nkc, nvc, nks, nvst, nkw, nvwt, gates_t, sbq, sbk, sbv) = _proj(x, l, *proj_params, tabs)
        o_mla = _mla_attention(mq, mk, mvt)
        kc, vct = _compress(nkc, nvc, l, *cmp_params)
        o_cmp, q_sel = _cmp_select(nq, kc, vct, gates_t)
        o_sel = _sel_attention(q_sel, nks, nvst, gates_t)
        o_win = _win_attention(nq, nkw, nvwt, gates_t)
        o_sb = _sb_attention(sbq, sbk, sbv)
        x = _out_proj(x, o_mla, o_cmp, o_sel, o_win, o_sb, l, w_out)
        x2d = _ffn(x.reshape(b * s, d), ffn2_norm, *ffn2, l, fg, l == depth - 1)
        x = x2d.reshape(b, s, d)
    return x
```

```python
import functools
import math

import numpy as np
import jax
import jax.numpy as jnp
from jax import lax
from jax.experimental import pallas as pl
from jax.experimental.pallas import tpu as pltpu

D_MODEL = 1024
HEAD_DIM = 64
MLA_HEADS = 6
MLA_NOPE = 64
MLA_ROPE = 32
MLA_V = 64
MLA_Q_LORA = 256
MLA_KV_LORA = 128
NSA_HEADS = 6
NSA_KV_HEADS = 2
NSA_GROUP = NSA_HEADS // NSA_KV_HEADS
NSA_BRANCHES = 3
CMP_LEN = 32
CMP_STRIDE = 16
CMP_HIDDEN = 128
SEL_LEN = 64
SEL_TOPK = 16
WINDOW = 512
SB_HEADS = 4
D_FF = 2816
ROPE_THETA = 500000.0
PARTIAL_ROT = HEAD_DIM // 4
EPS = 1e-6
NEG_INF = -1e30
M_FLOOR = 0.1 * NEG_INF
FORCE_SCORE = 1e4
PICKED = -3e38
F32_EXP2_ZERO = -151.0
LOG2_E = math.log2(math.e)
N_GATES = NSA_HEADS * NSA_BRANCHES

LANES = 128
FFN_CHUNK = 256
SWEEP_UNROLL = 4
TOPK_ROW_STEPS = (32, 64)
SB_KEY_BLOCK = 256
PROJ_ROWS = 512
TOKEN_CHUNK = 256
ONES_PAD = 16
VMEM_LIMIT = 56 * 1024 * 1024

_MXU = jnp.bfloat16
_F32 = jnp.float32

_IN_WIDTHS = (MLA_Q_LORA, MLA_KV_LORA, MLA_ROPE, NSA_HEADS * HEAD_DIM) + (NSA_KV_HEADS * HEAD_DIM,) * 6 + (
    N_GATES, SB_HEADS * HEAD_DIM, SB_HEADS * HEAD_DIM, SB_HEADS * HEAD_DIM)
_IN_OFF = np.concatenate([[0], np.cumsum(_IN_WIDTHS)])
(_O_CQ, _O_CKV, _O_KR, _O_NQ, _O_NKC, _O_NVC, _O_NKS, _O_NVS, _O_NKW, _O_NVW, _O_GATE, _O_SBQ, _O_SBK,
 _O_SBV) = [int(v) for v in _IN_OFF[:-1]]

_S_CQ, _S_CKV, _S_KR, _S_KRS = 0, 2, 3, 4
_H_NQ, _H_NQS = 10, 16
_H_KC, _H_KCS, _H_VC = 22, 24, 26
_H_KS, _H_KSS = 28, 30
_H_KW, _H_KWS = 32, 34
_H_SBQ, _H_SBK = 36, 40
_N_HEAD_COLS = 44
_T_VS, _T_VW, _T_GATE = 0, NSA_KV_HEADS * HEAD_DIM, 2 * NSA_KV_HEADS * HEAD_DIM
GATE_ROWS = 24
_T_SBV = _T_GATE + 2 * ONES_PAD
_T_ROWS = _T_SBV + SB_HEADS * HEAD_DIM


def _dot(a, b):
    return jnp.dot(a.astype(_MXU), b.astype(_MXU), preferred_element_type=_F32)


def _dot_nt(a, b):
    return lax.dot_general(a.astype(_MXU), b.astype(_MXU), (((1,), (1,)), ((), ())),
                           preferred_element_type=_F32)


def _dot_split_rhs(a, b):
    hi = b.astype(_MXU)
    lo = (b - hi.astype(_F32)).astype(_MXU)
    return (jnp.dot(a, hi, preferred_element_type=_F32) + jnp.dot(a, lo, preferred_element_type=_F32))


def _rms(x, g):
    return x * lax.rsqrt(jnp.mean(x * x, axis=-1, keepdims=True) + EPS) * g


def _params(*sem):
    return pltpu.CompilerParams(dimension_semantics=sem, vmem_limit_bytes=VMEM_LIMIT)


def _layer_spec(a, layer):
    return pl.BlockSpec((None,) + a.shape[1:], lambda *_: (layer,) + (0,) * (a.ndim - 1))


def _mixer_output(mla_ref, cmp_ref, sel_ref, win_ref, sb_ref, w_ref):
    def w_rows(first_head, n_heads):
        return w_ref[first_head * HEAD_DIM:(first_head + n_heads) * HEAD_DIM, :]

    nsa = cmp_ref[...].astype(_F32) + sel_ref[...].astype(_F32) + win_ref[...].astype(_F32)
    return (jnp.dot(mla_ref[...], w_rows(0, MLA_HEADS), preferred_element_type=_F32)
            + _dot(nsa, w_rows(MLA_HEADS, NSA_HEADS))
            + jnp.dot(sb_ref[...], w_rows(MLA_HEADS + NSA_HEADS, SB_HEADS), preferred_element_type=_F32))


def _ffn_kernel(x_ref, g_ref, wg_ref, wu_ref, wd_ref, fg_ref, *rest, final_norm, with_mixer):
    *mixer_refs, o_ref, act_ref = rest
    x = x_ref[...]
    if with_mixer:
        x = x + _mixer_output(*mixer_refs)
    h = _rms(x, g_ref[...]).astype(_MXU)
    tf = act_ref.shape[1]
    for c0 in range(0, tf, FFN_CHUNK):
        c1 = min(c0 + FFN_CHUNK, tf)
        gate = jnp.dot(h, wg_ref[:, c0:c1], preferred_element_type=_F32)
        up = jnp.dot(h, wu_ref[:, c0:c1], preferred_element_type=_F32)
        act_ref[:, c0:c1] = (gate * jax.nn.sigmoid(gate) * up).astype(act_ref.dtype)
    y = x + 0.5 * jnp.dot(act_ref[...], wd_ref[...], preferred_element_type=_F32)
    if final_norm:
        y = _rms(y, fg_ref[...])
    o_ref[...] = y


def _ffn(x2d, g, wg, wu, wd, layer, fg, final_norm, mixer=None):
    rows = x2d.shape[0]
    tm = min(1024, rows)

    def resident(a):
        return pl.BlockSpec((None,) + a.shape[1:], lambda i: (layer, 0, 0), pipeline_mode=pl.Buffered(1))

    def row_tile(a):
        return pl.BlockSpec((tm, a.shape[1]), lambda i: (i, 0))

    mixer_args, mixer_specs = (), []
    if mixer is not None:
        *heads, w_out = mixer
        mixer_args = (*heads, w_out)
        mixer_specs = [row_tile(a) for a in heads] + [resident(w_out)]
    return pl.pallas_call(
        functools.partial(_ffn_kernel, final_norm=final_norm, with_mixer=mixer is not None),
        grid=(rows // tm,),
        in_specs=[row_tile(x2d), _layer_spec(g, layer), resident(wg), resident(wu), resident(wd),
                  pl.BlockSpec((1, D_MODEL), lambda i: (0, 0))] + mixer_specs,
        out_specs=row_tile(x2d),
        out_shape=jax.ShapeDtypeStruct((rows, D_MODEL), _F32),
        scratch_shapes=[pltpu.VMEM((tm, D_FF), _MXU)],
        compiler_params=_params("parallel"),
        name="ffn",
    )(x2d, g, wg, wu, wd, fg, *mixer_args)


def _proj_kernel(x_ref, g_ref, w_ref, wt_ref, qn_ref, wuq_ref, wuqs_ref, kvn_ref, wuk_ref, wuv_ref, gb_ref,
                 cq_ref, sq_ref, ck_ref, sk_ref, c64_ref, s64_ref, oh_ref,
                 mq_ref, mk_ref, mv_ref, nq_ref, nkc_ref, nvc_ref, nks_ref, nvs_ref, nkw_ref, nvw_ref,
                 gate_ref, sbq_ref, sbk_ref, sbv_ref, stage_ref):
    hn = _rms(x_ref[0], g_ref[...]).astype(_MXU)

    def proj(h0, h1):
        return jnp.dot(hn, w_ref[:, h0 * HEAD_DIM:h1 * HEAD_DIM], preferred_element_type=_F32)

    def slot(p, s):
        return p[:, s * LANES:(s + 1) * LANES]

    def head(p, i):
        return p[:, i * HEAD_DIM:(i + 1) * HEAD_DIM]

    def write_value_chunks(o_ref, h, vt):
        width = o_ref.shape[-1]
        for c in range(o_ref.shape[2]):
            o_ref[0, h, c] = vt[:, c * width:(c + 1) * width].astype(o_ref.dtype)

    p = proj(0, _H_NQ)
    cq = _rms(p[:, :MLA_Q_LORA], qn_ref[...])
    ckv = _rms(slot(p, _S_CKV), kvn_ref[...])
    q = _dot(cq, wuq_ref[...])
    q_partner = _dot(cq, wuqs_ref[...])
    kpe = slot(p, _S_KR) * ck_ref[...] + slot(p, _S_KRS) * sk_ref[...]
    kn = _dot(ckv, wuk_ref[...])
    for h in range(MLA_HEADS):
        mq_ref[0, h] = (slot(q, h) * cq_ref[...] + slot(q_partner, h) * sq_ref[...]).astype(mq_ref.dtype)
        mk_ref[0, h] = (slot(kn, h) + kpe).astype(mk_ref.dtype)
        write_value_chunks(mv_ref, h, _ones_row_pad(_dot_nt(wuv_ref[h], ckv)))

    c64 = c64_ref[...]
    s64 = s64_ref[...]
    scale = HEAD_DIM ** -0.5

    p = proj(_H_NQ, _H_KC)
    for h in range(NSA_HEADS):
        nq_ref[0, h] = ((head(p, h) * c64 + head(p, NSA_HEADS + h) * s64) * (scale * LOG2_E)).astype(nq_ref.dtype)

    p = proj(_H_KC, _H_SBQ)
    base = _H_KC
    ns = oh_ref.shape[-1]

    def roped(hk, hks, g):
        return head(p, hk - base + g) * c64 + head(p, hks - base + g) * s64

    def write_chunked(o_ref, g, val):
        stage_ref[...] = val
        for t in range(CMP_STRIDE):
            piece = stage_ref[pl.ds(t, val.shape[0] // CMP_STRIDE, stride=CMP_STRIDE), :]
            o_ref[0, g, :, t * HEAD_DIM:(t + 1) * HEAD_DIM] = piece.astype(o_ref.dtype)

    for g in range(NSA_KV_HEADS):
        write_chunked(nkc_ref, g, roped(_H_KC, _H_KCS, g))
        write_chunked(nvc_ref, g, head(p, _H_VC - base + g))
        nks_ref[0, g, :, 0:ns] = oh_ref[...]
        nks_ref[0, g, :, ns:ns + HEAD_DIM] = roped(_H_KS, _H_KSS, g).astype(nks_ref.dtype)
        nkw_ref[0, g] = roped(_H_KW, _H_KWS, g).astype(nkw_ref.dtype)

    pt = _dot_nt(wt_ref[...], hn)
    for g in range(NSA_KV_HEADS):
        lo = g * HEAD_DIM
        write_value_chunks(nvs_ref, g, _ones_row_pad(pt[_T_VS + lo:_T_VS + lo + HEAD_DIM]))
        write_value_chunks(nvw_ref, g, _ones_row_pad(pt[_T_VW + lo:_T_VW + lo + HEAD_DIM]))
    gate_ref[0] = jax.nn.sigmoid(pt[_T_GATE:_T_GATE + GATE_ROWS] + gb_ref[...])

    p = proj(_H_SBQ, _N_HEAD_COLS)
    for h in range(SB_HEADS):
        sbq_ref[0, h] = (head(p, h) * (scale * LOG2_E)).astype(sbq_ref.dtype)
        sbk_ref[0, h] = head(p, SB_HEADS + h).astype(sbk_ref.dtype)
        write_value_chunks(sbv_ref, h, pt[_T_SBV + h * HEAD_DIM:_T_SBV + (h + 1) * HEAD_DIM])


def _proj(x, layer, g, w_ext, w_t, qn, wuq, wuqs, kvn, wuk, wuv, gb, tabs):
    b, s, _ = x.shape
    ts = min(PROJ_ROWS, s)
    tv = min(TOKEN_CHUNK, ts)
    cq, sq, ck, sk, c64, s64, onehot = tabs
    full = functools.partial(_layer_spec, layer=layer)

    def tab(a):
        return pl.BlockSpec((ts, a.shape[1]), lambda bi, i: (i, 0))

    def heads(n, d):
        return (pl.BlockSpec((1, n, ts, d), lambda bi, i: (bi, 0, i, 0)),
                jax.ShapeDtypeStruct((b, n, s, d), _MXU))

    def values_t(n):
        return (pl.BlockSpec((1, n, ts // tv, HEAD_DIM + ONES_PAD, tv), lambda bi, i: (bi, 0, i, 0, 0)),
                jax.ShapeDtypeStruct((b, n, s // tv, HEAD_DIM + ONES_PAD, tv), _MXU))

    def chunked():
        return (pl.BlockSpec((1, NSA_KV_HEADS, ts // CMP_STRIDE, CMP_STRIDE * HEAD_DIM), lambda bi, i: (bi, 0, i, 0)),
                jax.ShapeDtypeStruct((b, NSA_KV_HEADS, s // CMP_STRIDE, CMP_STRIDE * HEAD_DIM), _MXU))

    outs = [heads(MLA_HEADS, LANES), heads(MLA_HEADS, LANES), values_t(MLA_HEADS), heads(NSA_HEADS, HEAD_DIM),
            chunked(), chunked(), heads(NSA_KV_HEADS, onehot.shape[1] + HEAD_DIM), values_t(NSA_KV_HEADS),
            heads(NSA_KV_HEADS, HEAD_DIM), values_t(NSA_KV_HEADS),
            (pl.BlockSpec((1, GATE_ROWS, ts), lambda bi, i: (bi, 0, i)), jax.ShapeDtypeStruct((b, GATE_ROWS, s), _F32)),
            heads(SB_HEADS, HEAD_DIM), heads(SB_HEADS, HEAD_DIM),
            (pl.BlockSpec((1, SB_HEADS, ts // SB_KEY_BLOCK, HEAD_DIM, SB_KEY_BLOCK), lambda bi, i: (bi, 0, i, 0, 0)),
             jax.ShapeDtypeStruct((b, SB_HEADS, s // SB_KEY_BLOCK, HEAD_DIM, SB_KEY_BLOCK), _MXU))]
    return pl.pallas_call(
        _proj_kernel,
        grid=(b, s // ts),
        in_specs=[pl.BlockSpec((1, ts, D_MODEL), lambda bi, i: (bi, i, 0)), full(g), full(w_ext), full(w_t), full(qn),
                  full(wuq), full(wuqs), full(kvn), full(wuk), full(wuv), full(gb),
                  tab(cq), tab(sq), tab(ck), tab(sk), tab(c64), tab(s64), tab(onehot)],
        out_specs=[o[0] for o in outs],
        out_shape=[o[1] for o in outs],
        scratch_shapes=[pltpu.VMEM((ts, HEAD_DIM), _F32)],
        compiler_params=_params("parallel", "parallel"),
        name="proj",
    )(x, g, w_ext, w_t, qn, wuq, wuqs, kvn, wuk, wuv, gb, cq, sq, ck, sk, c64, s64, onehot)


def _ones_row_pad(vt):
    first = lax.broadcasted_iota(jnp.int32, (ONES_PAD, vt.shape[1]), 0) == 0
    return jnp.concatenate([vt, jnp.where(first, 1.0, 0.0).astype(vt.dtype)], axis=0)


def _softmax_step_t(carry, st, vt_chunks):
    m, acc = carry
    m_new = jnp.maximum(m, jnp.max(st, axis=0, keepdims=True))
    alpha = jnp.exp2(m - m_new)
    pt = jnp.exp2(st - m_new).astype(_MXU)
    n = st.shape[0] // len(vt_chunks)
    pv = sum(jnp.dot(vt, pt[c * n:(c + 1) * n], preferred_element_type=_F32) for c, vt in enumerate(vt_chunks))
    return m_new, alpha * acc + pv


def _softmax_init_t(d, cols):
    return (jnp.full((1, cols), M_FLOOR, _F32), jnp.zeros((d + ONES_PAD, cols), _F32))


def _softmax_finish_t(carry, d):
    _, acc = carry
    return acc[:d] * (1.0 / acc[d:d + 1])


def _two_chain_sweep(n_full, qk, soft, init):
    def body(j, carry, diag=False):
        c0, c1 = carry
        qk(0, j)
        c1 = soft(1, j, c1, diag)
        qk(1, jnp.zeros_like(j) if diag else j + 1)
        c0 = soft(0, j, c0, diag)
        return c0, c1

    def unrolled(i, carry):
        for u in range(SWEEP_UNROLL):
            carry = body(SWEEP_UNROLL * i + u, carry)
        return carry

    qk(1, n_full)
    carry = body(n_full, init, True)
    trips = n_full // SWEEP_UNROLL
    carry = lax.fori_loop(0, trips, unrolled, carry)
    return lax.fori_loop(SWEEP_UNROLL * trips, n_full, body, carry)


def _mla_kernel(q_ref, k_ref, vt_ref, o_ref, s0_ref, s1_ref, *, t, nsub):
    qi = pl.program_id(2)
    s_refs = (s0_ref, s1_ref)

    def qk(hh, j):
        off = pl.multiple_of(j * t, t)
        s_refs[hh][...] = _dot_nt(k_ref[0, hh, pl.ds(off, t), :], q_ref[0, hh])

    def soft(hh, j, carry, diag):
        st = s_refs[hh][...]
        if diag:
            key = lax.broadcasted_iota(jnp.int32, (t, t), 0)
            qry = lax.broadcasted_iota(jnp.int32, (t, t), 1)
            st = jnp.where(key <= qry, st, NEG_INF)
        return _softmax_step_t(carry, st, [vt_ref[0, hh, j * nsub + c] for c in range(nsub)])

    carry = _two_chain_sweep(qi, qk, soft, tuple(_softmax_init_t(MLA_V, t) for _ in range(2)))
    ot = jnp.concatenate([_softmax_finish_t(c, MLA_V) for c in carry], axis=0)
    o_ref[0] = ot.T.astype(o_ref.dtype)


def _mla_attention(q, k, vt):
    b, h, s, _ = q.shape
    tv = vt.shape[-1]
    dv = vt.shape[-2]
    t = min(512, s)
    assert h % 2 == 0 and 2 * MLA_V == LANES and t % tv == 0 and s % t == 0
    return pl.pallas_call(
        functools.partial(_mla_kernel, t=t, nsub=t // tv),
        grid=(b, h // 2, s // t),
        in_specs=[pl.BlockSpec((1, 2, t, LANES), lambda bi, hi, i: (bi, hi, i, 0)),
                  pl.BlockSpec((1, 2, s, LANES), lambda bi, hi, i: (bi, hi, 0, 0)),
                  pl.BlockSpec((1, 2, s // tv, dv, tv), lambda bi, hi, i: (bi, hi, 0, 0, 0))],
        out_specs=pl.BlockSpec((1, t, LANES), lambda bi, hi, i: (bi, i, hi)),
        out_shape=jax.ShapeDtypeStruct((b, s, h * MLA_V), _MXU),
        scratch_shapes=[pltpu.VMEM((t, t), _F32), pltpu.VMEM((t, t), _F32)],
        compiler_params=_params("parallel", "parallel", "arbitrary"),
        name="mla_attn",
    )(q, k, vt)


def _sb_kernel(q_ref, k_ref, vt_ref, u_ref, o_ref, *scratch_refs, tq, tk):
    qi = pl.program_id(1)
    u = u_ref[...]
    n_heads = q_ref.shape[1]
    scratch = [scratch_refs[6 * hh:6 * (hh + 1)] for hh in range(n_heads)]
    per_tile = tq // tk

    def step(j, carry, key_offset=None):
        diag = key_offset is not None
        off = pl.multiple_of(j * tk, tk)
        if diag:
            key = key_offset + lax.broadcasted_iota(jnp.int32, (tk, tq), 0)
            qry = lax.broadcasted_iota(jnp.int32, (tk, tq), 1)
            strict = key < qry

        def logits(hh):
            z_ref, _, _, _, _, _ = scratch[hh]
            z_ref[...] = _dot_nt(k_ref[0, hh, pl.ds(off, tk), :], q_ref[0, hh])

        def log_terms(hh):
            z_ref, lb_ref, hi_ref, lo_ref, _, _ = scratch[hh]
            z = z_ref[...]
            log_beta = jnp.minimum(z, 0.0) - jnp.log2(1.0 + jnp.exp2(-jnp.abs(z)))
            log_rem = log_beta - z
            if diag:
                log_rem = jnp.where(strict, log_rem, 0.0)
            hi = log_rem.astype(_MXU)
            lb_ref[...] = log_beta
            hi_ref[...] = hi
            lo_ref[...] = (log_rem - hi.astype(_F32)).astype(_MXU)
            return log_rem[0:1, :]

        def suffix_sums(hh):
            _, _, hi_ref, lo_ref, sfx_ref, _ = scratch[hh]
            sfx_ref[...] = (jnp.dot(u, hi_ref[...], preferred_element_type=_F32)
                            + jnp.dot(u, lo_ref[...], preferred_element_type=_F32))

        def weights(hh, first_rem):
            _, lb_ref, _, _, sfx_ref, a_ref = scratch[hh]
            rem = carry[hh][0]
            suffix = sfx_ref[...]
            a = jnp.exp2(lb_ref[...] + suffix + rem)
            if diag:
                a = jnp.where(strict, a, 0.0)
            a_ref[...] = a.astype(_MXU)
            return rem + suffix[0:1, :] + first_rem

        def values(hh):
            a_ref = scratch[hh][5]
            return carry[hh][1] + jnp.dot(vt_ref[0, hh, j], a_ref[...], preferred_element_type=_F32)

        heads = range(n_heads)
        for hh in heads:
            logits(hh)
        first = []
        for hh in heads:
            first.append(log_terms(hh))
            suffix_sums(hh)
        rems = [weights(hh, first[hh]) for hh in heads]
        alive = jnp.max(functools.reduce(jnp.maximum, rems)) > F32_EXP2_ZERO
        return alive, tuple((rems[hh], values(hh)) for hh in heads)

    carry = tuple((jnp.zeros((1, tq), _F32), jnp.zeros((HEAD_DIM, tq), _F32)) for _ in range(n_heads))
    first = qi * per_tile
    for i in reversed(range(per_tile)):
        _, carry = step(first + i, carry, key_offset=i * tk)
    has_past = first > 0
    carry = tuple((jnp.where(has_past, rem, NEG_INF), acc) for rem, acc in carry)
    alive, carry = step(jnp.maximum(first - 1, 0), carry)

    def earlier(c):
        return (c[0] - 1,) + step(c[0], c[2])

    _, _, carry = lax.while_loop(lambda c: jnp.logical_and(c[0] >= 0, c[1]), earlier, (first - 2, alive, carry))
    o_ref[0] = jnp.concatenate([acc for _, acc in carry], axis=0).T.astype(o_ref.dtype)


def _sb_attention(q, k, vt):
    b, h, s, d = q.shape
    tk = vt.shape[-1]
    tq = tk
    assert (h * d) % LANES == 0 and s % tq == 0 and tq % tk == 0
    idx = np.arange(tk)
    u = jnp.asarray(idx[None, :] > idx[:, None], _MXU)
    return pl.pallas_call(
        functools.partial(_sb_kernel, tq=tq, tk=tk),
        grid=(b, s // tq),
        in_specs=[pl.BlockSpec((1, h, tq, d), lambda bi, i: (bi, 0, i, 0)),
                  pl.BlockSpec((1, h, s, d), lambda bi, i: (bi, 0, 0, 0)),
                  pl.BlockSpec((1, h, s // tk, d, tk), lambda bi, i: (bi, 0, 0, 0, 0)),
                  pl.BlockSpec((tk, tk), lambda bi, i: (0, 0))],
        out_specs=pl.BlockSpec((1, tq, h * d), lambda bi, i: (bi, i, 0)),
        out_shape=jax.ShapeDtypeStruct((b, s, h * d), _MXU),
        scratch_shapes=[pltpu.VMEM((tk, tq), dt) for _ in range(h) for dt in (_F32, _F32, _MXU, _MXU, _F32, _MXU)],
        compiler_params=_params("parallel", "arbitrary"),
        name="sb_attn",
    )(q, k, vt, u)


def _compress_kernel(xk_ref, xv_ref, w1k_ref, w2k_ref, pk_ref, w1v_ref, w2v_ref, pv_ref, ok_ref, ov_ref):
    def hidden(x_ref, w1_ref, p_ref):
        x = x_ref[0, 0]
        n = x.shape[0]
        first = jnp.dot(x, w1_ref[0], preferred_element_type=_F32)
        second = jnp.dot(x, w1_ref[1], preferred_element_type=_F32)
        pos = _dot(p_ref[0], w1_ref[0]) + _dot(p_ref[1], w1_ref[1])
        hid = first + pltpu.roll(second, n - 1, 0) + pos[0:1]
        return 0.5 * hid * (1.0 + jnp.tanh(math.sqrt(2.0 / math.pi) * (hid + 0.044715 * hid * hid * hid)))

    ok_ref[0, 0] = _dot(hidden(xk_ref, w1k_ref, pk_ref), w2k_ref[...]).astype(ok_ref.dtype)
    ov_ref[0, 0] = _dot_nt(w2v_ref[...], hidden(xv_ref, w1v_ref, pv_ref)).astype(ov_ref.dtype)


def _compress(xk, xv, layer, w1k, w2k, pk, w1v, w2v, pv):
    b, g, n, _ = xk.shape
    d = HEAD_DIM
    full = functools.partial(_layer_spec, layer=layer)

    xspec = pl.BlockSpec((1, 1, n, CMP_STRIDE * d), lambda bi, gi: (bi, gi, 0, 0))
    return pl.pallas_call(
        _compress_kernel,
        grid=(b, g),
        in_specs=[xspec, xspec, full(w1k), full(w2k), full(pk), full(w1v), full(w2v), full(pv)],
        out_specs=[pl.BlockSpec((1, 1, n, d), lambda bi, gi: (bi, gi, 0, 0)),
                   pl.BlockSpec((1, 1, d, n), lambda bi, gi: (bi, gi, 0, 0))],
        out_shape=[jax.ShapeDtypeStruct((b, g, n, d), _MXU), jax.ShapeDtypeStruct((b, g, d, n), _MXU)],
        compiler_params=_params("parallel", "parallel"),
        name="nsa_compress",
    )(xk, xv, w1k, w2k, pk, w1v, w2v, pv)


def _group_queries(q_ref, g, tq):
    return q_ref[0, g * NSA_GROUP:(g + 1) * NSA_GROUP].reshape(NSA_GROUP * tq, q_ref.shape[-1])


def _gated_heads(ot, gt_ref, g, branch, tq):
    out = []
    for r in range(NSA_GROUP):
        row = NSA_BRANCHES * (g * NSA_GROUP + r) + branch
        out.append(ot[:, r * tq:(r + 1) * tq] * gt_ref[0, row:row + 1, :])
    return out


def _cmp_kernel(q_ref, kc_ref, vct_ref, ov_ref, gt_ref, o_ref, qa_ref, s0_ref, s1_ref, *, tq, n_top):
    q0 = pl.program_id(1) * tq
    ncp = kc_ref.shape[2]
    ns = ov_ref.shape[0]
    lanes = NSA_GROUP * tq
    s_refs = (s0_ref, s1_ref)
    for g in range(NSA_KV_HEADS):
        s_refs[g][...] = _dot_nt(kc_ref[0, g], _group_queries(q_ref, g, tq))
    qpos = q0 + (lax.broadcasted_iota(jnp.int32, (1, lanes), 1) & (tq - 1))
    cmp_end = lax.broadcasted_iota(jnp.int32, (ncp, 1), 0) * CMP_STRIDE + (CMP_LEN - 1)
    visible = cmp_end <= qpos
    cur = jnp.right_shift(q0 + lax.broadcasted_iota(jnp.int32, (1, tq), 1), int(math.log2(SEL_LEN)))
    blk = lax.broadcasted_iota(jnp.int32, (ns, 1), 0)
    forced = (blk == 0) | (blk == cur) | (blk == cur - 1)
    future = blk > cur
    blk_f = blk.astype(_F32)
    heads = []
    scores = []
    for g in range(NSA_KV_HEADS):
        st = jnp.where(visible, s_refs[g][...], NEG_INF)
        e = jnp.exp2(st - jnp.max(st, axis=0, keepdims=True))
        inv = jnp.where(qpos >= CMP_LEN - 1, 1.0 / jnp.sum(e, axis=0, keepdims=True), 0.0)
        pt = e * inv
        heads += _gated_heads(_dot(vct_ref[0, g], pt), gt_ref, g, 0, tq)
        p_sum = sum(pt[:, r * tq:(r + 1) * tq] for r in range(NSA_GROUP))
        score = _dot_split_rhs(ov_ref[...], p_sum)
        scores.append(jnp.where(forced, FORCE_SCORE, jnp.where(future, -1.0, score)))
    o_ref[0] = jnp.concatenate(heads, axis=0).T.astype(o_ref.dtype)
    def select(rows):
        sc = [s[:rows] for s in scores]
        idx = blk_f[:rows]
        for _ in range(n_top):
            for g in range(NSA_KV_HEADS):
                top = jnp.max(sc[g], axis=0, keepdims=True)
                first = jnp.min(jnp.where(sc[g] == top, idx, float(ns)), axis=0, keepdims=True)
                sc[g] = jnp.where(idx == first, PICKED, sc[g])
        for g in range(NSA_KV_HEADS):
            sel_m1 = jnp.where(sc[g] < 0.5 * PICKED, 0.0, -1.0)
            if rows < ns:
                sel_m1 = jnp.concatenate([sel_m1, jnp.full((ns - rows, tq), -1.0, _F32)], axis=0)
            sel_m1 = sel_m1.T.astype(qa_ref.dtype)
            for h in range(g * NSA_GROUP, (g + 1) * NSA_GROUP):
                qa_ref[0, h, :, 0:ns] = sel_m1
                qa_ref[0, h, :, ns:ns + HEAD_DIM] = q_ref[0, h]

    visible_blocks = (q0 + tq) // SEL_LEN
    lower = 0
    for rows in sorted({min(ns, r) for r in TOPK_ROW_STEPS} | {ns}):
        in_range = visible_blocks > lower
        if rows < ns:
            in_range = jnp.logical_and(in_range, visible_blocks <= rows)
        pl.when(in_range)(functools.partial(select, rows))
        lower = rows


def _cmp_select(q, kc, vct, gates_t):
    b, h, s, d = q.shape
    g = kc.shape[1]
    ncp = kc.shape[2]
    ns = s // SEL_LEN
    n_top = min(SEL_TOPK, ns)
    tq = min(256, s)
    assert tq & (tq - 1) == 0 and g == 2
    c0 = np.arange(ncp)[:, None] * CMP_STRIDE
    n0 = np.arange(ns)[None, :] * SEL_LEN
    overlap = jnp.asarray(((c0 < n0 + SEL_LEN) & (c0 + CMP_LEN > n0)).T, _MXU)
    return pl.pallas_call(
        functools.partial(_cmp_kernel, tq=tq, n_top=n_top),
        grid=(b, s // tq),
        in_specs=[pl.BlockSpec((1, h, tq, d), lambda bi, i: (bi, 0, i, 0)),
                  pl.BlockSpec((1, g, ncp, d), lambda bi, i: (bi, 0, 0, 0)),
                  pl.BlockSpec((1, g, d, ncp), lambda bi, i: (bi, 0, 0, 0)),
                  pl.BlockSpec((ns, ncp), lambda bi, i: (0, 0)),
                  pl.BlockSpec((1, GATE_ROWS, tq), lambda bi, i: (bi, 0, i))],
        out_specs=[pl.BlockSpec((1, tq, h * d), lambda bi, i: (bi, i, 0)),
                   pl.BlockSpec((1, h, tq, ns + d), lambda bi, i: (bi, 0, i, 0))],
        out_shape=[jax.ShapeDtypeStruct((b, s, h * d), _MXU), jax.ShapeDtypeStruct((b, h, s, ns + d), _MXU)],
        scratch_shapes=[pltpu.VMEM((ncp, NSA_GROUP * tq), _F32) for _ in range(g)],
        compiler_params=_params("parallel", "arbitrary"),
        name="nsa_cmp_select",
    )(q, kc, vct, overlap, gates_t)


def _key_minus_query(keys, tq):
    return jnp.asarray(np.arange(keys)[:, None] - np.arange(NSA_GROUP * tq)[None, :] % tq, jnp.int32)


def _sel_kernel(q_ref, k_ref, vt_ref, gt_ref, rel_ref, o_ref, s0_ref, s1_ref, *, tq, tk, nsub):
    q0 = pl.program_id(1) * tq
    last = (q0 + tq - 1) // tk
    lanes = NSA_GROUP * tq
    s_refs = (s0_ref, s1_ref)

    def qk(g, j):
        off = pl.multiple_of(j * tk, tk)
        s_refs[g][...] = _dot_nt(k_ref[0, g, pl.ds(off, tk), :], _group_queries(q_ref, g, tq))

    def soft(g, j, carry, causal):
        st = s_refs[g][...]
        if causal:
            st = jnp.where(rel_ref[...] <= q0 - j * tk, st, NEG_INF)
        return _softmax_step_t(carry, st, [vt_ref[0, g, j * nsub + c] for c in range(nsub)])

    init = tuple(_softmax_init_t(HEAD_DIM, lanes) for _ in range(NSA_KV_HEADS))
    carry = _two_chain_sweep(last, qk, soft, init)
    heads = []
    for g in range(NSA_KV_HEADS):
        heads += _gated_heads(_softmax_finish_t(carry[g], HEAD_DIM), gt_ref, g, 1, tq)
    o_ref[0] = jnp.concatenate(heads, axis=0).T.astype(o_ref.dtype)


def _sel_attention(q, k, vt, gates_t):
    b, h, s, da = q.shape
    g = k.shape[1]
    d = HEAD_DIM
    tv = vt.shape[-1]
    tq = min(256, s)
    tk = min(512, s)
    assert tq & (tq - 1) == 0 and s % tk == 0 and tk % tv == 0 and g == 2
    return pl.pallas_call(
        functools.partial(_sel_kernel, tq=tq, tk=tk, nsub=tk // tv),
        grid=(b, s // tq),
        in_specs=[pl.BlockSpec((1, h, tq, da), lambda bi, i: (bi, 0, i, 0)),
                  pl.BlockSpec((1, g, s, da), lambda bi, i: (bi, 0, 0, 0)),
                  pl.BlockSpec((1, g) + vt.shape[2:], lambda bi, i: (bi, 0, 0, 0, 0)),
                  pl.BlockSpec((1, GATE_ROWS, tq), lambda bi, i: (bi, 0, i)),
                  pl.BlockSpec((tk, NSA_GROUP * tq), lambda bi, i: (0, 0))],
        out_specs=pl.BlockSpec((1, tq, h * d), lambda bi, i: (bi, i, 0)),
        out_shape=jax.ShapeDtypeStruct((b, s, h * d), _MXU),
        scratch_shapes=[pltpu.VMEM((tk, NSA_GROUP * tq), _F32) for _ in range(g)],
        compiler_params=_params("parallel", "arbitrary"),
        name="nsa_selected",
    )(q, k, vt, gates_t, _key_minus_query(tk, tq))


def _win_kernel(q_ref, k_ref, vt_ref, gt_ref, rel_ref, o_ref, *s_refs, tq, subs, span, tv):
    lanes = NSA_GROUP * tq
    rel = rel_ref[...]
    chains = [(sub, g) for sub in range(subs) for g in range(NSA_KV_HEADS)]

    def origin(sub):
        q0 = (pl.program_id(1) * subs + sub) * tq
        return q0, pl.multiple_of(jnp.maximum(q0 - WINDOW, 0), tq)

    for c, (sub, g) in enumerate(chains):
        _, start = origin(sub)
        q = q_ref[0, g * NSA_GROUP:(g + 1) * NSA_GROUP, sub * tq:(sub + 1) * tq].reshape(lanes, q_ref.shape[-1])
        s_refs[c][...] = _dot_nt(k_ref[0, g, pl.ds(start, span), :], q)
    for sub in range(subs):
        q0, start = origin(sub)
        offset = q0 - start
        heads = []
        for g in range(NSA_KV_HEADS):
            st = jnp.where(rel <= offset, s_refs[chains.index((sub, g))][...], NEG_INF)
            st = jnp.where(rel > offset - WINDOW, st, NEG_INF)
            carry = _softmax_step_t(_softmax_init_t(HEAD_DIM, lanes), st,
                                    [vt_ref[0, g, start // tv + c] for c in range(span // tv)])
            ot = _softmax_finish_t(carry, HEAD_DIM)
            for r in range(NSA_GROUP):
                row = NSA_BRANCHES * (g * NSA_GROUP + r) + 2
                heads.append(ot[:, r * tq:(r + 1) * tq] * gt_ref[0, row:row + 1, sub * tq:(sub + 1) * tq])
        o_ref[0, sub * tq:(sub + 1) * tq, :] = jnp.concatenate(heads, axis=0).T.astype(o_ref.dtype)


def _win_attention(q, k, vt, gates_t):
    b, h, s, d = q.shape
    g = k.shape[1]
    tv = vt.shape[-1]
    tq = min(256, s)
    subs = 2 if s % (2 * tq) == 0 else 1
    span = WINDOW + tq
    assert tq & (tq - 1) == 0 and s >= span and tq % tv == 0 and WINDOW % tv == 0 and g == NSA_KV_HEADS
    return pl.pallas_call(
        functools.partial(_win_kernel, tq=tq, subs=subs, span=span, tv=tv),
        grid=(b, s // (subs * tq)),
        in_specs=[pl.BlockSpec((1, h, subs * tq, d), lambda bi, i: (bi, 0, i, 0)),
                  pl.BlockSpec((1, g, s, d), lambda bi, i: (bi, 0, 0, 0)),
                  pl.BlockSpec((1, g) + vt.shape[2:], lambda bi, i: (bi, 0, 0, 0, 0)),
                  pl.BlockSpec((1, GATE_ROWS, subs * tq), lambda bi, i: (bi, 0, i)),
                  pl.BlockSpec((span, NSA_GROUP * tq), lambda bi, i: (0, 0))],
        out_specs=pl.BlockSpec((1, subs * tq, h * d), lambda bi, i: (bi, i, 0)),
        out_shape=jax.ShapeDtypeStruct((b, s, h * d), _MXU),
        scratch_shapes=[pltpu.VMEM((span, NSA_GROUP * tq), _F32) for _ in range(subs * g)],
        compiler_params=_params("parallel", "arbitrary"),
        name="nsa_window",
    )(q, k, vt, gates_t, _key_minus_query(span, tq))


def _gather_cols(w, idx):
    idx = np.asarray(idx)
    cuts = [0] + [i for i in range(1, len(idx)) if idx[i] != idx[i - 1] + (idx[i - 1] >= 0)] + [len(idx)]
    pieces = []
    for a, b in zip(cuts[:-1], cuts[1:]):
        if idx[a] < 0:
            pieces.append(jnp.zeros(w.shape[:-1] + (b - a,), _MXU))
        else:
            pieces.append(w[..., int(idx[a]):int(idx[a]) + b - a].astype(_MXU))
    return jnp.concatenate(pieces, axis=-1)


def _swap_halves(rot):
    return (np.arange(rot) + rot // 2) % rot


def _w_in_index():
    idx = np.full((_N_HEAD_COLS * HEAD_DIM,), -1, np.int64)

    def put(col, src):
        src = np.asarray(src)
        idx[col:col + len(src)] = src

    def put_head(pos, src):
        put(pos * HEAD_DIM, src)

    put(_S_CQ * LANES, _O_CQ + np.arange(MLA_Q_LORA))
    put(_S_CKV * LANES, _O_CKV + np.arange(MLA_KV_LORA))
    put(_S_KR * LANES + MLA_NOPE, _O_KR + np.arange(MLA_ROPE))
    put(_S_KRS * LANES + MLA_NOPE, _O_KR + _swap_halves(MLA_ROPE))
    for h in range(NSA_HEADS):
        put_head(_H_NQ + h, _O_NQ + h * HEAD_DIM + np.arange(HEAD_DIM))
        put_head(_H_NQS + h, _O_NQ + h * HEAD_DIM + _swap_halves(PARTIAL_ROT))
    for hk, hks, ok in ((_H_KC, _H_KCS, _O_NKC), (_H_KS, _H_KSS, _O_NKS), (_H_KW, _H_KWS, _O_NKW)):
        for g in range(NSA_KV_HEADS):
            put_head(hk + g, ok + g * HEAD_DIM + np.arange(HEAD_DIM))
            put_head(hks + g, ok + g * HEAD_DIM + _swap_halves(PARTIAL_ROT))
    for g in range(NSA_KV_HEADS):
        put_head(_H_VC + g, _O_NVC + g * HEAD_DIM + np.arange(HEAD_DIM))
    for h in range(SB_HEADS):
        put_head(_H_SBQ + h, _O_SBQ + h * HEAD_DIM + np.arange(HEAD_DIM))
        put_head(_H_SBK + h, _O_SBK + h * HEAD_DIM + np.arange(HEAD_DIM))
    return idx


def _mla_up_index():
    qd = MLA_NOPE + MLA_ROPE
    kd = MLA_NOPE + MLA_V
    uq = np.full((MLA_HEADS * LANES,), -1, np.int64)
    uqs = uq.copy()
    uk = uq.copy()
    for h in range(MLA_HEADS):
        uq[h * LANES:h * LANES + qd] = h * qd + np.arange(qd)
        uqs[h * LANES + MLA_NOPE:h * LANES + qd] = h * qd + MLA_NOPE + _swap_halves(MLA_ROPE)
        uk[h * LANES:h * LANES + MLA_NOPE] = h * kd + np.arange(MLA_NOPE)
    return uq, uqs, uk


def _transposed_weights(w_in, gate_bias):
    width = NSA_KV_HEADS * HEAD_DIM
    gate_rows = jnp.pad(w_in[..., _O_GATE:_O_GATE + N_GATES], ((0, 0), (0, 0), (0, _T_SBV - _T_GATE - N_GATES)))
    rows = jnp.concatenate([w_in[..., _O_NVS:_O_NVS + width], w_in[..., _O_NVW:_O_NVW + width], gate_rows,
                            w_in[..., _O_SBV:_O_SBV + SB_HEADS * HEAD_DIM]], axis=-1)
    bias = jnp.pad(gate_bias, ((0, 0), (0, GATE_ROWS - N_GATES)))[..., None]
    return jnp.swapaxes(rows, -1, -2).astype(_MXU), bias


def _rope_tables(s):
    pos = np.arange(s, dtype=np.float64)

    def cs(rot):
        half = rot // 2
        ang = pos[:, None] * (ROPE_THETA ** (-np.arange(half, dtype=np.float64) / half))[None, :]
        c, sn = np.cos(ang), np.sin(ang)
        return np.concatenate([c, c], axis=1), np.concatenate([-sn, sn], axis=1)

    c, sn = cs(MLA_ROPE)
    pad = np.zeros((s, LANES - MLA_NOPE - MLA_ROPE))
    ck = np.concatenate([np.ones((s, MLA_NOPE)), c, pad], axis=1)
    sk = np.concatenate([np.zeros((s, MLA_NOPE)), sn, pad], axis=1)
    q_scale = (MLA_NOPE + MLA_ROPE) ** -0.5 * LOG2_E
    c, sn = cs(PARTIAL_ROT)
    c64 = np.concatenate([c, np.ones((s, HEAD_DIM - PARTIAL_ROT))], axis=1)
    s64 = np.concatenate([sn, np.zeros((s, HEAD_DIM - PARTIAL_ROT))], axis=1)
    ns = s // SEL_LEN
    onehot = (np.arange(s)[:, None] // SEL_LEN == np.arange(ns)[None, :]) * -NEG_INF
    tables = [jnp.asarray(t, _F32) for t in (ck * q_scale, sk * q_scale, ck, sk, c64, s64)]
    return tables + [jnp.asarray(onehot, _MXU)]


def kernel(x, ffn1_norm, ffn1_w_gate, ffn1_w_up, ffn1_w_down, mix_norm, w_in, mla_q_norm, mla_w_uq, mla_kv_norm,
           mla_w_ukv, nsa_gate_bias, nsa_cmp_pos_k, nsa_cmp_w1_k, nsa_cmp_w2_k, nsa_cmp_pos_v, nsa_cmp_w1_v,
           nsa_cmp_w2_v, w_out, ffn2_norm, ffn2_w_gate, ffn2_w_up, ffn2_w_down, final_norm):
    b, s, d = x.shape
    depth = w_in.shape[0]
    tabs = _rope_tables(s)
    in_idx = _w_in_index()
    uq_idx, uqs_idx, uk_idx = _mla_up_index()
    half = CMP_LEN * HEAD_DIM // 2
    fg = final_norm.reshape(1, d)

    def row(p):
        return p[:, None, :]

    def cmp_weights(w1, w2, pos, transpose_out):
        pos = jnp.broadcast_to(pos.reshape(depth, 2, 1, half), (depth, 2, 8, half)).astype(_MXU)
        w2 = jnp.swapaxes(w2, -1, -2) if transpose_out else w2
        return w1.reshape(depth, 2, half, CMP_HIDDEN).astype(_MXU), w2.astype(_MXU), pos

    ffn1 = [w.astype(_MXU) for w in (ffn1_w_gate, ffn1_w_up, ffn1_w_down)]
    ffn2 = [w.astype(_MXU) for w in (ffn2_w_gate, ffn2_w_up, ffn2_w_down)]
    w_t, gate_bias = _transposed_weights(w_in, nsa_gate_bias)
    wuv_t = mla_w_ukv.reshape(depth, MLA_KV_LORA, MLA_HEADS, 2, MLA_V)[:, :, :, 1].transpose(0, 2, 3, 1).astype(_MXU)
    proj_params = (row(mix_norm), _gather_cols(w_in, in_idx), w_t,
                   row(mla_q_norm), _gather_cols(mla_w_uq, uq_idx), _gather_cols(mla_w_uq, uqs_idx),
                   row(mla_kv_norm), _gather_cols(mla_w_ukv, uk_idx), wuv_t, gate_bias)
    cmp_params = (cmp_weights(nsa_cmp_w1_k, nsa_cmp_w2_k, nsa_cmp_pos_k, False)
                  + cmp_weights(nsa_cmp_w1_v, nsa_cmp_w2_v, nsa_cmp_pos_v, True))
    w_out = w_out.astype(_MXU)
    ffn1_norm, ffn2_norm = row(ffn1_norm), row(ffn2_norm)

    for l in range(depth):
        x2d = _ffn(x.reshape(b * s, d), ffn1_norm, *ffn1, l, fg, False)
        x = x2d.reshape(b, s, d)
        (mq, mk, mvt, nq, nkc, nvc, nks, nvst, nkw, nvwt, gates_t, sbq, sbk, sbv) = _proj(x, l, *proj_params, tabs)
        o_mla = _mla_attention(mq, mk, mvt)
        kc, vct = _compress(nkc, nvc, l, *cmp_params)
        o_cmp, q_sel = _cmp_select(nq, kc, vct, gates_t)
        o_sel = _sel_attention(q_sel, nks, nvst, gates_t)
        o_win = _win_attention(nq, nkw, nvwt, gates_t)
        o_sb = _sb_attention(sbq, sbk, sbv)
        heads = [o.reshape(b * s, -1) for o in (o_mla, o_cmp, o_sel, o_win, o_sb)]
        x2d = _ffn(x.reshape(b * s, d), ffn2_norm, *ffn2, l, fg, l == depth - 1, mixer=(*heads, w_out))
        x = x2d.reshape(b, s, d)
    return x
```

```python
import functools
import math

import numpy as np
import jax
import jax.numpy as jnp
from jax import lax
from jax.experimental import pallas as pl
from jax.experimental.pallas import tpu as pltpu

D_MODEL = 1024
HEAD_DIM = 64
MLA_HEADS = 6
MLA_NOPE = 64
MLA_ROPE = 32
MLA_V = 64
MLA_Q_LORA = 256
MLA_KV_LORA = 128
NSA_HEADS = 6
NSA_KV_HEADS = 2
NSA_GROUP = NSA_HEADS // NSA_KV_HEADS
NSA_BRANCHES = 3
CMP_LEN = 32
CMP_STRIDE = 16
CMP_HIDDEN = 128
SEL_LEN = 64
SEL_TOPK = 16
WINDOW = 512
SB_HEADS = 4
D_FF = 2816
ROPE_THETA = 500000.0
PARTIAL_ROT = HEAD_DIM // 4
EPS = 1e-6
NEG_INF = -1e30
M_FLOOR = 0.1 * NEG_INF
FORCE_SCORE = 1e4
PICKED = -3e38
F32_EXP2_ZERO = -151.0
LOG2_E = math.log2(math.e)
N_GATES = NSA_HEADS * NSA_BRANCHES

LANES = 128
FFN_CHUNK = 256
SWEEP_UNROLL = 4
TOPK_ROW_STEPS = (32, 64)
SB_KEY_BLOCK = 256
PROJ_ROWS = 512
TOKEN_CHUNK = 256
ONES_PAD = 16
VMEM_LIMIT = 56 * 1024 * 1024

_MXU = jnp.bfloat16
_F32 = jnp.float32

_IN_WIDTHS = (MLA_Q_LORA, MLA_KV_LORA, MLA_ROPE, NSA_HEADS * HEAD_DIM) + (NSA_KV_HEADS * HEAD_DIM,) * 6 + (
    N_GATES, SB_HEADS * HEAD_DIM, SB_HEADS * HEAD_DIM, SB_HEADS * HEAD_DIM)
_IN_OFF = np.concatenate([[0], np.cumsum(_IN_WIDTHS)])
(_O_CQ, _O_CKV, _O_KR, _O_NQ, _O_NKC, _O_NVC, _O_NKS, _O_NVS, _O_NKW, _O_NVW, _O_GATE, _O_SBQ, _O_SBK,
 _O_SBV) = [int(v) for v in _IN_OFF[:-1]]

_S_CQ, _S_CKV, _S_KR, _S_KRS = 0, 2, 3, 4
_H_NQ, _H_NQS = 10, 16
_H_KC, _H_KCS, _H_VC = 22, 24, 26
_H_KS, _H_KSS = 28, 30
_H_KW, _H_KWS = 32, 34
_H_SBQ, _H_SBK = 36, 40
_N_HEAD_COLS = 44
_T_VS, _T_VW, _T_GATE = 0, NSA_KV_HEADS * HEAD_DIM, 2 * NSA_KV_HEADS * HEAD_DIM
GATE_ROWS = 24
_T_SBV = _T_GATE + 2 * ONES_PAD
_T_ROWS = _T_SBV + SB_HEADS * HEAD_DIM


def _dot(a, b):
    return jnp.dot(a.astype(_MXU), b.astype(_MXU), preferred_element_type=_F32)


def _dot_nt(a, b):
    return lax.dot_general(a.astype(_MXU), b.astype(_MXU), (((1,), (1,)), ((), ())),
                           preferred_element_type=_F32)


def _dot_split_rhs(a, b):
    hi = b.astype(_MXU)
    lo = (b - hi.astype(_F32)).astype(_MXU)
    return (jnp.dot(a, hi, preferred_element_type=_F32) + jnp.dot(a, lo, preferred_element_type=_F32))


def _rms(x, g):
    return x * lax.rsqrt(jnp.mean(x * x, axis=-1, keepdims=True) + EPS) * g


def _params(*sem):
    return pltpu.CompilerParams(dimension_semantics=sem, vmem_limit_bytes=VMEM_LIMIT)


def _layer_spec(a, layer):
    return pl.BlockSpec((None,) + a.shape[1:], lambda *_: (layer,) + (0,) * (a.ndim - 1))


def _mixer_output(mla_ref, cmp_ref, sel_ref, win_ref, sb_ref, w_ref):
    def w_rows(first_head, n_heads):
        return w_ref[first_head * HEAD_DIM:(first_head + n_heads) * HEAD_DIM, :]

    nsa = cmp_ref[...].astype(_F32) + sel_ref[...].astype(_F32) + win_ref[...].astype(_F32)
    return (jnp.dot(mla_ref[...], w_rows(0, MLA_HEADS), preferred_element_type=_F32)
            + _dot(nsa, w_rows(MLA_HEADS, NSA_HEADS))
            + jnp.dot(sb_ref[...], w_rows(MLA_HEADS + NSA_HEADS, SB_HEADS), preferred_element_type=_F32))


def _ffn_kernel(x_ref, g_ref, wg_ref, wu_ref, wd_ref, fg_ref, *rest, final_norm, with_mixer):
    *mixer_refs, o_ref, act_ref = rest
    x = x_ref[...]
    if with_mixer:
        x = x + _mixer_output(*mixer_refs)
    h = _rms(x, g_ref[...]).astype(_MXU)
    tf = act_ref.shape[1]
    for c0 in range(0, tf, FFN_CHUNK):
        c1 = min(c0 + FFN_CHUNK, tf)
        gate = jnp.dot(h, wg_ref[:, c0:c1], preferred_element_type=_F32)
        up = jnp.dot(h, wu_ref[:, c0:c1], preferred_element_type=_F32)
        act_ref[:, c0:c1] = (gate * jax.nn.sigmoid(gate) * up).astype(act_ref.dtype)
    y = x + 0.5 * jnp.dot(act_ref[...], wd_ref[...], preferred_element_type=_F32)
    if final_norm:
        y = _rms(y, fg_ref[...])
    o_ref[...] = y


def _ffn(x2d, g, wg, wu, wd, layer, fg, final_norm, mixer=None):
    rows = x2d.shape[0]
    tm = min(1024, rows)

    def resident(a):
        return pl.BlockSpec((None,) + a.shape[1:], lambda i: (layer, 0, 0), pipeline_mode=pl.Buffered(1))

    def row_tile(a):
        return pl.BlockSpec((tm, a.shape[1]), lambda i: (i, 0))

    mixer_args, mixer_specs = (), []
    if mixer is not None:
        *heads, w_out = mixer
        mixer_args = (*heads, w_out)
        mixer_specs = [row_tile(a) for a in heads] + [resident(w_out)]
    return pl.pallas_call(
        functools.partial(_ffn_kernel, final_norm=final_norm, with_mixer=mixer is not None),
        grid=(rows // tm,),
        in_specs=[row_tile(x2d), _layer_spec(g, layer), resident(wg), resident(wu), resident(wd),
                  pl.BlockSpec((1, D_MODEL), lambda i: (0, 0))] + mixer_specs,
        out_specs=row_tile(x2d),
        out_shape=jax.ShapeDtypeStruct((rows, D_MODEL), _F32),
        scratch_shapes=[pltpu.VMEM((tm, D_FF), _MXU)],
        compiler_params=_params("parallel"),
        name="ffn",
    )(x2d, g, wg, wu, wd, fg, *mixer_args)


def _proj_kernel(x_ref, g_ref, w_ref, wt_ref, qn_ref, wuq_ref, wuqs_ref, kvn_ref, wuk_ref, wuv_ref, gb_ref,
                 cq_ref, sq_ref, ck_ref, sk_ref, c64_ref, s64_ref, oh_ref,
                 mq_ref, mk_ref, mv_ref, nq_ref, nkc_ref, nvc_ref, nks_ref, nvs_ref, nkw_ref, nvw_ref,
                 gate_ref, sbq_ref, sbk_ref, sbv_ref, stage_ref):
    hn = _rms(x_ref[0], g_ref[...]).astype(_MXU)

    def proj(h0, h1):
        return jnp.dot(hn, w_ref[:, h0 * HEAD_DIM:h1 * HEAD_DIM], preferred_element_type=_F32)

    def slot(p, s):
        return p[:, s * LANES:(s + 1) * LANES]

    def head(p, i):
        return p[:, i * HEAD_DIM:(i + 1) * HEAD_DIM]

    def write_value_chunks(o_ref, h, vt):
        width = o_ref.shape[-1]
        for c in range(o_ref.shape[2]):
            o_ref[0, h, c] = vt[:, c * width:(c + 1) * width].astype(o_ref.dtype)

    p = proj(0, _H_NQ)
    cq = _rms(p[:, :MLA_Q_LORA], qn_ref[...])
    ckv = _rms(slot(p, _S_CKV), kvn_ref[...])
    q = _dot(cq, wuq_ref[...])
    q_partner = _dot(cq, wuqs_ref[...])
    kpe = slot(p, _S_KR) * ck_ref[...] + slot(p, _S_KRS) * sk_ref[...]
    kn = _dot(ckv, wuk_ref[...])
    for h in range(MLA_HEADS):
        mq_ref[0, h] = (slot(q, h) * cq_ref[...] + slot(q_partner, h) * sq_ref[...]).astype(mq_ref.dtype)
        mk_ref[0, h] = (slot(kn, h) + kpe).astype(mk_ref.dtype)
        write_value_chunks(mv_ref, h, _ones_row_pad(_dot_nt(wuv_ref[h], ckv)))

    c64 = c64_ref[...]
    s64 = s64_ref[...]
    scale = HEAD_DIM ** -0.5

    p = proj(_H_NQ, _H_KC)
    for h in range(NSA_HEADS):
        nq_ref[0, h] = ((head(p, h) * c64 + head(p, NSA_HEADS + h) * s64) * (scale * LOG2_E)).astype(nq_ref.dtype)

    p = proj(_H_KC, _H_SBQ)
    base = _H_KC
    ns = oh_ref.shape[-1]

    def roped(hk, hks, g):
        return head(p, hk - base + g) * c64 + head(p, hks - base + g) * s64

    def write_chunked(o_ref, g, val):
        stage_ref[...] = val
        for t in range(CMP_STRIDE):
            piece = stage_ref[pl.ds(t, val.shape[0] // CMP_STRIDE, stride=CMP_STRIDE), :]
            o_ref[0, g, :, t * HEAD_DIM:(t + 1) * HEAD_DIM] = piece.astype(o_ref.dtype)

    for g in range(NSA_KV_HEADS):
        write_chunked(nkc_ref, g, roped(_H_KC, _H_KCS, g))
        write_chunked(nvc_ref, g, head(p, _H_VC - base + g))
        nks_ref[0, g, :, 0:ns] = oh_ref[...]
        nks_ref[0, g, :, ns:ns + HEAD_DIM] = roped(_H_KS, _H_KSS, g).astype(nks_ref.dtype)
        nkw_ref[0, g] = roped(_H_KW, _H_KWS, g).astype(nkw_ref.dtype)

    pt = _dot_nt(wt_ref[...], hn)
    for g in range(NSA_KV_HEADS):
        lo = g * HEAD_DIM
        write_value_chunks(nvs_ref, g, _ones_row_pad(pt[_T_VS + lo:_T_VS + lo + HEAD_DIM]))
        write_value_chunks(nvw_ref, g, _ones_row_pad(pt[_T_VW + lo:_T_VW + lo + HEAD_DIM]))
    gate_ref[0] = jax.nn.sigmoid(pt[_T_GATE:_T_GATE + GATE_ROWS] + gb_ref[...])

    p = proj(_H_SBQ, _N_HEAD_COLS)
    for h in range(SB_HEADS):
        sbq_ref[0, h] = (head(p, h) * (scale * LOG2_E)).astype(sbq_ref.dtype)
        sbk_ref[0, h] = head(p, SB_HEADS + h).astype(sbk_ref.dtype)
        write_value_chunks(sbv_ref, h, pt[_T_SBV + h * HEAD_DIM:_T_SBV + (h + 1) * HEAD_DIM])


def _proj(x, layer, g, w_ext, w_t, qn, wuq, wuqs, kvn, wuk, wuv, gb, tabs):
    b, s, _ = x.shape
    ts = min(PROJ_ROWS, s)
    tv = min(TOKEN_CHUNK, ts)
    cq, sq, ck, sk, c64, s64, onehot = tabs
    full = functools.partial(_layer_spec, layer=layer)

    def tab(a):
        return pl.BlockSpec((ts, a.shape[1]), lambda bi, i: (i, 0))

    def heads(n, d):
        return (pl.BlockSpec((1, n, ts, d), lambda bi, i: (bi, 0, i, 0)),
                jax.ShapeDtypeStruct((b, n, s, d), _MXU))

    def values_t(n):
        return (pl.BlockSpec((1, n, ts // tv, HEAD_DIM + ONES_PAD, tv), lambda bi, i: (bi, 0, i, 0, 0)),
                jax.ShapeDtypeStruct((b, n, s // tv, HEAD_DIM + ONES_PAD, tv), _MXU))

    def chunked():
        return (pl.BlockSpec((1, NSA_KV_HEADS, ts // CMP_STRIDE, CMP_STRIDE * HEAD_DIM), lambda bi, i: (bi, 0, i, 0)),
                jax.ShapeDtypeStruct((b, NSA_KV_HEADS, s // CMP_STRIDE, CMP_STRIDE * HEAD_DIM), _MXU))

    outs = [heads(MLA_HEADS, LANES), heads(MLA_HEADS, LANES), values_t(MLA_HEADS), heads(NSA_HEADS, HEAD_DIM),
            chunked(), chunked(), heads(NSA_KV_HEADS, onehot.shape[1] + HEAD_DIM), values_t(NSA_KV_HEADS),
            heads(NSA_KV_HEADS, HEAD_DIM), values_t(NSA_KV_HEADS),
            (pl.BlockSpec((1, GATE_ROWS, ts), lambda bi, i: (bi, 0, i)), jax.ShapeDtypeStruct((b, GATE_ROWS, s), _F32)),
            heads(SB_HEADS, HEAD_DIM), heads(SB_HEADS, HEAD_DIM),
            (pl.BlockSpec((1, SB_HEADS, ts // SB_KEY_BLOCK, HEAD_DIM, SB_KEY_BLOCK), lambda bi, i: (bi, 0, i, 0, 0)),
             jax.ShapeDtypeStruct((b, SB_HEADS, s // SB_KEY_BLOCK, HEAD_DIM, SB_KEY_BLOCK), _MXU))]
    return pl.pallas_call(
        _proj_kernel,
        grid=(b, s // ts),
        in_specs=[pl.BlockSpec((1, ts, D_MODEL), lambda bi, i: (bi, i, 0)), full(g), full(w_ext), full(w_t), full(qn),
                  full(wuq), full(wuqs), full(kvn), full(wuk), full(wuv), full(gb),
                  tab(cq), tab(sq), tab(ck), tab(sk), tab(c64), tab(s64), tab(onehot)],
        out_specs=[o[0] for o in outs],
        out_shape=[o[1] for o in outs],
        scratch_shapes=[pltpu.VMEM((ts, HEAD_DIM), _F32)],
        compiler_params=_params("parallel", "parallel"),
        name="proj",
    )(x, g, w_ext, w_t, qn, wuq, wuqs, kvn, wuk, wuv, gb, cq, sq, ck, sk, c64, s64, onehot)


def _ones_row_pad(vt):
    first = lax.broadcasted_iota(jnp.int32, (ONES_PAD, vt.shape[1]), 0) == 0
    return jnp.concatenate([vt, jnp.where(first, 1.0, 0.0).astype(vt.dtype)], axis=0)


def _softmax_step_t(carry, st, vt_chunks):
    m, acc = carry
    m_new = jnp.maximum(m, jnp.max(st, axis=0, keepdims=True))
    alpha = jnp.exp2(m - m_new)
    pt = jnp.exp2(st - m_new).astype(_MXU)
    n = st.shape[0] // len(vt_chunks)
    pv = sum(jnp.dot(vt, pt[c * n:(c + 1) * n], preferred_element_type=_F32) for c, vt in enumerate(vt_chunks))
    return m_new, alpha * acc + pv


def _softmax_init_t(d, cols):
    return (jnp.full((1, cols), M_FLOOR, _F32), jnp.zeros((d + ONES_PAD, cols), _F32))


def _softmax_finish_t(carry, d):
    _, acc = carry
    return acc[:d] * (1.0 / acc[d:d + 1])


def _two_chain_sweep(n_full, qk, soft, init):
    def body(j, carry, diag=False):
        c0, c1 = carry
        qk(0, j)
        c1 = soft(1, j, c1, diag)
        qk(1, jnp.zeros_like(j) if diag else j + 1)
        c0 = soft(0, j, c0, diag)
        return c0, c1

    qk(1, n_full)
    carry = body(n_full, init, True)
    done = jnp.zeros_like(n_full)
    unroll = SWEEP_UNROLL
    while unroll >= 1:
        def unrolled(i, carry, first=done, unroll=unroll):
            for u in range(unroll):
                carry = body(first + unroll * i + u, carry)
            return carry

        trips = (n_full - done) // unroll
        carry = lax.fori_loop(0, trips, unrolled, carry)
        done = done + trips * unroll
        unroll //= 2
    return carry


def _mla_kernel(q_ref, k_ref, vt_ref, o_ref, s0_ref, s1_ref, *, t, nsub):
    qi = pl.program_id(2)
    s_refs = (s0_ref, s1_ref)

    def qk(hh, j):
        off = pl.multiple_of(j * t, t)
        s_refs[hh][...] = _dot_nt(k_ref[0, hh, pl.ds(off, t), :], q_ref[0, hh])

    def soft(hh, j, carry, diag):
        st = s_refs[hh][...]
        if diag:
            key = lax.broadcasted_iota(jnp.int32, (t, t), 0)
            qry = lax.broadcasted_iota(jnp.int32, (t, t), 1)
            st = jnp.where(key <= qry, st, NEG_INF)
        return _softmax_step_t(carry, st, [vt_ref[0, hh, j * nsub + c] for c in range(nsub)])

    carry = _two_chain_sweep(qi, qk, soft, tuple(_softmax_init_t(MLA_V, t) for _ in range(2)))
    ot = jnp.concatenate([_softmax_finish_t(c, MLA_V) for c in carry], axis=0)
    o_ref[0] = ot.T.astype(o_ref.dtype)


def _mla_attention(q, k, vt):
    b, h, s, _ = q.shape
    tv = vt.shape[-1]
    dv = vt.shape[-2]
    t = min(512, s)
    assert h % 2 == 0 and 2 * MLA_V == LANES and t % tv == 0 and s % t == 0
    return pl.pallas_call(
        functools.partial(_mla_kernel, t=t, nsub=t // tv),
        grid=(b, h // 2, s // t),
        in_specs=[pl.BlockSpec((1, 2, t, LANES), lambda bi, hi, i: (bi, hi, i, 0)),
                  pl.BlockSpec((1, 2, s, LANES), lambda bi, hi, i: (bi, hi, 0, 0)),
                  pl.BlockSpec((1, 2, s // tv, dv, tv), lambda bi, hi, i: (bi, hi, 0, 0, 0))],
        out_specs=pl.BlockSpec((1, t, LANES), lambda bi, hi, i: (bi, i, hi)),
        out_shape=jax.ShapeDtypeStruct((b, s, h * MLA_V), _MXU),
        scratch_shapes=[pltpu.VMEM((t, t), _F32), pltpu.VMEM((t, t), _F32)],
        compiler_params=_params("parallel", "parallel", "arbitrary"),
        name="mla_attn",
    )(q, k, vt)


def _sb_kernel(q_ref, k_ref, vt_ref, u_ref, o_ref, *scratch_refs, tq, tk):
    qi = pl.program_id(1)
    u = u_ref[...]
    n_heads = q_ref.shape[1]
    scratch = [scratch_refs[6 * hh:6 * (hh + 1)] for hh in range(n_heads)]
    per_tile = tq // tk

    def step(j, carry, key_offset=None):
        diag = key_offset is not None
        off = pl.multiple_of(j * tk, tk)
        if diag:
            key = key_offset + lax.broadcasted_iota(jnp.int32, (tk, tq), 0)
            qry = lax.broadcasted_iota(jnp.int32, (tk, tq), 1)
            strict = key < qry

        def logits(hh):
            z_ref, _, _, _, _, _ = scratch[hh]
            z_ref[...] = _dot_nt(k_ref[0, hh, pl.ds(off, tk), :], q_ref[0, hh])

        def log_terms(hh):
            z_ref, lb_ref, hi_ref, lo_ref, _, _ = scratch[hh]
            z = z_ref[...]
            log_beta = jnp.minimum(z, 0.0) - jnp.log2(1.0 + jnp.exp2(-jnp.abs(z)))
            log_rem = log_beta - z
            if diag:
                log_rem = jnp.where(strict, log_rem, 0.0)
            hi = log_rem.astype(_MXU)
            lb_ref[...] = log_beta
            hi_ref[...] = hi
            lo_ref[...] = (log_rem - hi.astype(_F32)).astype(_MXU)
            return log_rem[0:1, :]

        def suffix_sums(hh):
            _, _, hi_ref, lo_ref, sfx_ref, _ = scratch[hh]
            sfx_ref[...] = (jnp.dot(u, hi_ref[...], preferred_element_type=_F32)
                            + jnp.dot(u, lo_ref[...], preferred_element_type=_F32))

        def weights(hh, first_rem):
            _, lb_ref, _, _, sfx_ref, a_ref = scratch[hh]
            rem = carry[hh][0]
            suffix = sfx_ref[...]
            a = jnp.exp2(lb_ref[...] + suffix + rem)
            if diag:
                a = jnp.where(strict, a, 0.0)
            a_ref[...] = a.astype(_MXU)
            return rem + suffix[0:1, :] + first_rem

        def values(hh):
            a_ref = scratch[hh][5]
            return carry[hh][1] + jnp.dot(vt_ref[0, hh, j], a_ref[...], preferred_element_type=_F32)

        heads = range(n_heads)
        for hh in heads:
            logits(hh)
        first = []
        for hh in heads:
            first.append(log_terms(hh))
            suffix_sums(hh)
        rems = [weights(hh, first[hh]) for hh in heads]
        alive = jnp.max(functools.reduce(jnp.maximum, rems)) > F32_EXP2_ZERO
        return alive, tuple((rems[hh], values(hh)) for hh in heads)

    carry = tuple((jnp.zeros((1, tq), _F32), jnp.zeros((HEAD_DIM, tq), _F32)) for _ in range(n_heads))
    first = qi * per_tile
    for i in reversed(range(per_tile)):
        _, carry = step(first + i, carry, key_offset=i * tk)
    has_past = first > 0
    carry = tuple((jnp.where(has_past, rem, NEG_INF), acc) for rem, acc in carry)
    alive, carry = step(jnp.maximum(first - 1, 0), carry)

    def earlier(c):
        return (c[0] - 1,) + step(c[0], c[2])

    _, _, carry = lax.while_loop(lambda c: jnp.logical_and(c[0] >= 0, c[1]), earlier, (first - 2, alive, carry))
    o_ref[0] = jnp.concatenate([acc for _, acc in carry], axis=0).T.astype(o_ref.dtype)


def _sb_attention(q, k, vt):
    b, h, s, d = q.shape
    tk = vt.shape[-1]
    tq = tk
    assert (h * d) % LANES == 0 and s % tq == 0 and tq % tk == 0
    idx = np.arange(tk)
    u = jnp.asarray(idx[None, :] > idx[:, None], _MXU)
    return pl.pallas_call(
        functools.partial(_sb_kernel, tq=tq, tk=tk),
        grid=(b, s // tq),
        in_specs=[pl.BlockSpec((1, h, tq, d), lambda bi, i: (bi, 0, i, 0)),
                  pl.BlockSpec((1, h, s, d), lambda bi, i: (bi, 0, 0, 0)),
                  pl.BlockSpec((1, h, s // tk, d, tk), lambda bi, i: (bi, 0, 0, 0, 0)),
                  pl.BlockSpec((tk, tk), lambda bi, i: (0, 0))],
        out_specs=pl.BlockSpec((1, tq, h * d), lambda bi, i: (bi, i, 0)),
        out_shape=jax.ShapeDtypeStruct((b, s, h * d), _MXU),
        scratch_shapes=[pltpu.VMEM((tk, tq), dt) for _ in range(h) for dt in (_F32, _F32, _MXU, _MXU, _F32, _MXU)],
        compiler_params=_params("parallel", "arbitrary"),
        name="sb_attn",
    )(q, k, vt, u)


def _compress_kernel(xk_ref, xv_ref, w1k_ref, w2k_ref, pk_ref, w1v_ref, w2v_ref, pv_ref, ok_ref, ov_ref):
    def hidden(x_ref, w1_ref, p_ref):
        x = x_ref[0, 0]
        n = x.shape[0]
        first = jnp.dot(x, w1_ref[0], preferred_element_type=_F32)
        second = jnp.dot(x, w1_ref[1], preferred_element_type=_F32)
        pos = _dot(p_ref[0], w1_ref[0]) + _dot(p_ref[1], w1_ref[1])
        hid = first + pltpu.roll(second, n - 1, 0) + pos[0:1]
        return 0.5 * hid * (1.0 + jnp.tanh(math.sqrt(2.0 / math.pi) * (hid + 0.044715 * hid * hid * hid)))

    ok_ref[0, 0] = _dot(hidden(xk_ref, w1k_ref, pk_ref), w2k_ref[...]).astype(ok_ref.dtype)
    ov_ref[0, 0] = _dot_nt(w2v_ref[...], hidden(xv_ref, w1v_ref, pv_ref)).astype(ov_ref.dtype)


def _compress(xk, xv, layer, w1k, w2k, pk, w1v, w2v, pv):
    b, g, n, _ = xk.shape
    d = HEAD_DIM
    full = functools.partial(_layer_spec, layer=layer)

    xspec = pl.BlockSpec((1, 1, n, CMP_STRIDE * d), lambda bi, gi: (bi, gi, 0, 0))
    return pl.pallas_call(
        _compress_kernel,
        grid=(b, g),
        in_specs=[xspec, xspec, full(w1k), full(w2k), full(pk), full(w1v), full(w2v), full(pv)],
        out_specs=[pl.BlockSpec((1, 1, n, d), lambda bi, gi: (bi, gi, 0, 0)),
                   pl.BlockSpec((1, 1, d, n), lambda bi, gi: (bi, gi, 0, 0))],
        out_shape=[jax.ShapeDtypeStruct((b, g, n, d), _MXU), jax.ShapeDtypeStruct((b, g, d, n), _MXU)],
        compiler_params=_params("parallel", "parallel"),
        name="nsa_compress",
    )(xk, xv, w1k, w2k, pk, w1v, w2v, pv)


def _group_queries(q_ref, g, tq):
    return q_ref[0, g * NSA_GROUP:(g + 1) * NSA_GROUP].reshape(NSA_GROUP * tq, q_ref.shape[-1])


def _gated_heads(ot, gt_ref, g, branch, tq):
    out = []
    for r in range(NSA_GROUP):
        row = NSA_BRANCHES * (g * NSA_GROUP + r) + branch
        out.append(ot[:, r * tq:(r + 1) * tq] * gt_ref[0, row:row + 1, :])
    return out


def _cmp_kernel(q_ref, kc_ref, vct_ref, ov_ref, gt_ref, o_ref, qa_ref, s0_ref, s1_ref, *, tq, n_top):
    q0 = pl.program_id(1) * tq
    ncp = kc_ref.shape[2]
    ns = ov_ref.shape[0]
    lanes = NSA_GROUP * tq
    s_refs = (s0_ref, s1_ref)
    for g in range(NSA_KV_HEADS):
        s_refs[g][...] = _dot_nt(kc_ref[0, g], _group_queries(q_ref, g, tq))
    qpos = q0 + (lax.broadcasted_iota(jnp.int32, (1, lanes), 1) & (tq - 1))
    cmp_end = lax.broadcasted_iota(jnp.int32, (ncp, 1), 0) * CMP_STRIDE + (CMP_LEN - 1)
    visible = cmp_end <= qpos
    cur = jnp.right_shift(q0 + lax.broadcasted_iota(jnp.int32, (1, tq), 1), int(math.log2(SEL_LEN)))
    blk = lax.broadcasted_iota(jnp.int32, (ns, 1), 0)
    forced = (blk == 0) | (blk == cur) | (blk == cur - 1)
    future = blk > cur
    blk_f = blk.astype(_F32)
    heads = []
    scores = []
    for g in range(NSA_KV_HEADS):
        st = jnp.where(visible, s_refs[g][...], NEG_INF)
        e = jnp.exp2(st - jnp.max(st, axis=0, keepdims=True))
        inv = jnp.where(qpos >= CMP_LEN - 1, 1.0 / jnp.sum(e, axis=0, keepdims=True), 0.0)
        pt = e * inv
        heads += _gated_heads(_dot(vct_ref[0, g], pt), gt_ref, g, 0, tq)
        p_sum = sum(pt[:, r * tq:(r + 1) * tq] for r in range(NSA_GROUP))
        score = _dot_split_rhs(ov_ref[...], p_sum)
        scores.append(jnp.where(forced, FORCE_SCORE, jnp.where(future, -1.0, score)))
    o_ref[0] = jnp.concatenate(heads, axis=0).T.astype(o_ref.dtype)
    def select(rows):
        sc = [s[:rows] for s in scores]
        idx = blk_f[:rows]
        for _ in range(n_top):
            for g in range(NSA_KV_HEADS):
                top = jnp.max(sc[g], axis=0, keepdims=True)
                first = jnp.min(jnp.where(sc[g] == top, idx, float(ns)), axis=0, keepdims=True)
                sc[g] = jnp.where(idx == first, PICKED, sc[g])
        for g in range(NSA_KV_HEADS):
            sel_m1 = jnp.where(sc[g] < 0.5 * PICKED, 0.0, -1.0)
            if rows < ns:
                sel_m1 = jnp.concatenate([sel_m1, jnp.full((ns - rows, tq), -1.0, _F32)], axis=0)
            sel_m1 = sel_m1.T.astype(qa_ref.dtype)
            for h in range(g * NSA_GROUP, (g + 1) * NSA_GROUP):
                qa_ref[0, h, :, 0:ns] = sel_m1
                qa_ref[0, h, :, ns:ns + HEAD_DIM] = q_ref[0, h]

    visible_blocks = (q0 + tq) // SEL_LEN
    lower = 0
    for rows in sorted({min(ns, r) for r in TOPK_ROW_STEPS} | {ns}):
        in_range = visible_blocks > lower
        if rows < ns:
            in_range = jnp.logical_and(in_range, visible_blocks <= rows)
        pl.when(in_range)(functools.partial(select, rows))
        lower = rows


def _cmp_select(q, kc, vct, gates_t):
    b, h, s, d = q.shape
    g = kc.shape[1]
    ncp = kc.shape[2]
    ns = s // SEL_LEN
    n_top = min(SEL_TOPK, ns)
    tq = min(256, s)
    assert tq & (tq - 1) == 0 and g == 2
    c0 = np.arange(ncp)[:, None] * CMP_STRIDE
    n0 = np.arange(ns)[None, :] * SEL_LEN
    overlap = jnp.asarray(((c0 < n0 + SEL_LEN) & (c0 + CMP_LEN > n0)).T, _MXU)
    return pl.pallas_call(
        functools.partial(_cmp_kernel, tq=tq, n_top=n_top),
        grid=(b, s // tq),
        in_specs=[pl.BlockSpec((1, h, tq, d), lambda bi, i: (bi, 0, i, 0)),
                  pl.BlockSpec((1, g, ncp, d), lambda bi, i: (bi, 0, 0, 0)),
                  pl.BlockSpec((1, g, d, ncp), lambda bi, i: (bi, 0, 0, 0)),
                  pl.BlockSpec((ns, ncp), lambda bi, i: (0, 0)),
                  pl.BlockSpec((1, GATE_ROWS, tq), lambda bi, i: (bi, 0, i))],
        out_specs=[pl.BlockSpec((1, tq, h * d), lambda bi, i: (bi, i, 0)),
                   pl.BlockSpec((1, h, tq, ns + d), lambda bi, i: (bi, 0, i, 0))],
        out_shape=[jax.ShapeDtypeStruct((b, s, h * d), _MXU), jax.ShapeDtypeStruct((b, h, s, ns + d), _MXU)],
        scratch_shapes=[pltpu.VMEM((ncp, NSA_GROUP * tq), _F32) for _ in range(g)],
        compiler_params=_params("parallel", "arbitrary"),
        name="nsa_cmp_select",
    )(q, kc, vct, overlap, gates_t)


def _key_minus_query(keys, tq):
    return jnp.asarray(np.arange(keys)[:, None] - np.arange(NSA_GROUP * tq)[None, :] % tq, jnp.int32)


def _sel_kernel(q_ref, k_ref, vt_ref, gt_ref, rel_ref, o_ref, s0_ref, s1_ref, *, tq, tk, nsub):
    q0 = pl.program_id(1) * tq
    last = (q0 + tq - 1) // tk
    lanes = NSA_GROUP * tq
    s_refs = (s0_ref, s1_ref)

    def qk(g, j):
        off = pl.multiple_of(j * tk, tk)
        s_refs[g][...] = _dot_nt(k_ref[0, g, pl.ds(off, tk), :], _group_queries(q_ref, g, tq))

    def soft(g, j, carry, causal):
        st = s_refs[g][...]
        if causal:
            st = jnp.where(rel_ref[...] <= q0 - j * tk, st, NEG_INF)
        return _softmax_step_t(carry, st, [vt_ref[0, g, j * nsub + c] for c in range(nsub)])

    init = tuple(_softmax_init_t(HEAD_DIM, lanes) for _ in range(NSA_KV_HEADS))
    carry = _two_chain_sweep(last, qk, soft, init)
    heads = []
    for g in range(NSA_KV_HEADS):
        heads += _gated_heads(_softmax_finish_t(carry[g], HEAD_DIM), gt_ref, g, 1, tq)
    o_ref[0] = jnp.concatenate(heads, axis=0).T.astype(o_ref.dtype)


def _sel_attention(q, k, vt, gates_t):
    b, h, s, da = q.shape
    g = k.shape[1]
    d = HEAD_DIM
    tv = vt.shape[-1]
    tq = min(256, s)
    tk = min(512, s)
    assert tq & (tq - 1) == 0 and s % tk == 0 and tk % tv == 0 and g == 2
    return pl.pallas_call(
        functools.partial(_sel_kernel, tq=tq, tk=tk, nsub=tk // tv),
        grid=(b, s // tq),
        in_specs=[pl.BlockSpec((1, h, tq, da), lambda bi, i: (bi, 0, i, 0)),
                  pl.BlockSpec((1, g, s, da), lambda bi, i: (bi, 0, 0, 0)),
                  pl.BlockSpec((1, g) + vt.shape[2:], lambda bi, i: (bi, 0, 0, 0, 0)),
                  pl.BlockSpec((1, GATE_ROWS, tq), lambda bi, i: (bi, 0, i)),
                  pl.BlockSpec((tk, NSA_GROUP * tq), lambda bi, i: (0, 0))],
        out_specs=pl.BlockSpec((1, tq, h * d), lambda bi, i: (bi, i, 0)),
        out_shape=jax.ShapeDtypeStruct((b, s, h * d), _MXU),
        scratch_shapes=[pltpu.VMEM((tk, NSA_GROUP * tq), _F32) for _ in range(g)],
        compiler_params=_params("parallel", "arbitrary"),
        name="nsa_selected",
    )(q, k, vt, gates_t, _key_minus_query(tk, tq))


def _win_kernel(q_ref, k_ref, vt_ref, gt_ref, rel_ref, o_ref, *s_refs, tq, subs, span, tv):
    lanes = NSA_GROUP * tq
    rel = rel_ref[...]
    chains = [(sub, g) for sub in range(subs) for g in range(NSA_KV_HEADS)]

    def origin(sub):
        q0 = (pl.program_id(1) * subs + sub) * tq
        return q0, pl.multiple_of(jnp.maximum(q0 - WINDOW, 0), tq)

    for c, (sub, g) in enumerate(chains):
        _, start = origin(sub)
        q = q_ref[0, g * NSA_GROUP:(g + 1) * NSA_GROUP, sub * tq:(sub + 1) * tq].reshape(lanes, q_ref.shape[-1])
        s_refs[c][...] = _dot_nt(k_ref[0, g, pl.ds(start, span), :], q)
    for sub in range(subs):
        q0, start = origin(sub)
        offset = q0 - start
        heads = []
        for g in range(NSA_KV_HEADS):
            st = jnp.where(rel <= offset, s_refs[chains.index((sub, g))][...], NEG_INF)
            st = jnp.where(rel > offset - WINDOW, st, NEG_INF)
            carry = _softmax_step_t(_softmax_init_t(HEAD_DIM, lanes), st,
                                    [vt_ref[0, g, start // tv + c] for c in range(span // tv)])
            ot = _softmax_finish_t(carry, HEAD_DIM)
            for r in range(NSA_GROUP):
                row = NSA_BRANCHES * (g * NSA_GROUP + r) + 2
                heads.append(ot[:, r * tq:(r + 1) * tq] * gt_ref[0, row:row + 1, sub * tq:(sub + 1) * tq])
        o_ref[0, sub * tq:(sub + 1) * tq, :] = jnp.concatenate(heads, axis=0).T.astype(o_ref.dtype)


def _win_attention(q, k, vt, gates_t):
    b, h, s, d = q.shape
    g = k.shape[1]
    tv = vt.shape[-1]
    tq = min(256, s)
    subs = 2 if s % (2 * tq) == 0 else 1
    span = WINDOW + tq
    assert tq & (tq - 1) == 0 and s >= span and tq % tv == 0 and WINDOW % tv == 0 and g == NSA_KV_HEADS
    return pl.pallas_call(
        functools.partial(_win_kernel, tq=tq, subs=subs, span=span, tv=tv),
        grid=(b, s // (subs * tq)),
        in_specs=[pl.BlockSpec((1, h, subs * tq, d), lambda bi, i: (bi, 0, i, 0)),
                  pl.BlockSpec((1, g, s, d), lambda bi, i: (bi, 0, 0, 0)),
                  pl.BlockSpec((1, g) + vt.shape[2:], lambda bi, i: (bi, 0, 0, 0, 0)),
                  pl.BlockSpec((1, GATE_ROWS, subs * tq), lambda bi, i: (bi, 0, i)),
                  pl.BlockSpec((span, NSA_GROUP * tq), lambda bi, i: (0, 0))],
        out_specs=pl.BlockSpec((1, subs * tq, h * d), lambda bi, i: (bi, i, 0)),
        out_shape=jax.ShapeDtypeStruct((b, s, h * d), _MXU),
        scratch_shapes=[pltpu.VMEM((span, NSA_GROUP * tq), _F32) for _ in range(subs * g)],
        compiler_params=_params("parallel", "arbitrary"),
        name="nsa_window",
    )(q, k, vt, gates_t, _key_minus_query(span, tq))


def _gather_cols(w, idx):
    idx = np.asarray(idx)
    cuts = [0] + [i for i in range(1, len(idx)) if idx[i] != idx[i - 1] + (idx[i - 1] >= 0)] + [len(idx)]
    pieces = []
    for a, b in zip(cuts[:-1], cuts[1:]):
        if idx[a] < 0:
            pieces.append(jnp.zeros(w.shape[:-1] + (b - a,), _MXU))
        else:
            pieces.append(w[..., int(idx[a]):int(idx[a]) + b - a].astype(_MXU))
    return jnp.concatenate(pieces, axis=-1)


def _swap_halves(rot):
    return (np.arange(rot) + rot // 2) % rot


def _w_in_index():
    idx = np.full((_N_HEAD_COLS * HEAD_DIM,), -1, np.int64)

    def put(col, src):
        src = np.asarray(src)
        idx[col:col + len(src)] = src

    def put_head(pos, src):
        put(pos * HEAD_DIM, src)

    put(_S_CQ * LANES, _O_CQ + np.arange(MLA_Q_LORA))
    put(_S_CKV * LANES, _O_CKV + np.arange(MLA_KV_LORA))
    put(_S_KR * LANES + MLA_NOPE, _O_KR + np.arange(MLA_ROPE))
    put(_S_KRS * LANES + MLA_NOPE, _O_KR + _swap_halves(MLA_ROPE))
    for h in range(NSA_HEADS):
        put_head(_H_NQ + h, _O_NQ + h * HEAD_DIM + np.arange(HEAD_DIM))
        put_head(_H_NQS + h, _O_NQ + h * HEAD_DIM + _swap_halves(PARTIAL_ROT))
    for hk, hks, ok in ((_H_KC, _H_KCS, _O_NKC), (_H_KS, _H_KSS, _O_NKS), (_H_KW, _H_KWS, _O_NKW)):
        for g in range(NSA_KV_HEADS):
            put_head(hk + g, ok + g * HEAD_DIM + np.arange(HEAD_DIM))
            put_head(hks + g, ok + g * HEAD_DIM + _swap_halves(PARTIAL_ROT))
    for g in range(NSA_KV_HEADS):
        put_head(_H_VC + g, _O_NVC + g * HEAD_DIM + np.arange(HEAD_DIM))
    for h in range(SB_HEADS):
        put_head(_H_SBQ + h, _O_SBQ + h * HEAD_DIM + np.arange(HEAD_DIM))
        put_head(_H_SBK + h, _O_SBK + h * HEAD_DIM + np.arange(HEAD_DIM))
    return idx


def _mla_up_index():
    qd = MLA_NOPE + MLA_ROPE
    kd = MLA_NOPE + MLA_V
    uq = np.full((MLA_HEADS * LANES,), -1, np.int64)
    uqs = uq.copy()
    uk = uq.copy()
    for h in range(MLA_HEADS):
        uq[h * LANES:h * LANES + qd] = h * qd + np.arange(qd)
        uqs[h * LANES + MLA_NOPE:h * LANES + qd] = h * qd + MLA_NOPE + _swap_halves(MLA_ROPE)
        uk[h * LANES:h * LANES + MLA_NOPE] = h * kd + np.arange(MLA_NOPE)
    return uq, uqs, uk


def _transposed_weights(w_in, gate_bias):
    width = NSA_KV_HEADS * HEAD_DIM
    gate_rows = jnp.pad(w_in[..., _O_GATE:_O_GATE + N_GATES], ((0, 0), (0, 0), (0, _T_SBV - _T_GATE - N_GATES)))
    rows = jnp.concatenate([w_in[..., _O_NVS:_O_NVS + width], w_in[..., _O_NVW:_O_NVW + width], gate_rows,
                            w_in[..., _O_SBV:_O_SBV + SB_HEADS * HEAD_DIM]], axis=-1)
    bias = jnp.pad(gate_bias, ((0, 0), (0, GATE_ROWS - N_GATES)))[..., None]
    return jnp.swapaxes(rows, -1, -2).astype(_MXU), bias


def _rope_tables(s):
    pos = np.arange(s, dtype=np.float64)

    def cs(rot):
        half = rot // 2
        ang = pos[:, None] * (ROPE_THETA ** (-np.arange(half, dtype=np.float64) / half))[None, :]
        c, sn = np.cos(ang), np.sin(ang)
        return np.concatenate([c, c], axis=1), np.concatenate([-sn, sn], axis=1)

    c, sn = cs(MLA_ROPE)
    pad = np.zeros((s, LANES - MLA_NOPE - MLA_ROPE))
    ck = np.concatenate([np.ones((s, MLA_NOPE)), c, pad], axis=1)
    sk = np.concatenate([np.zeros((s, MLA_NOPE)), sn, pad], axis=1)
    q_scale = (MLA_NOPE + MLA_ROPE) ** -0.5 * LOG2_E
    c, sn = cs(PARTIAL_ROT)
    c64 = np.concatenate([c, np.ones((s, HEAD_DIM - PARTIAL_ROT))], axis=1)
    s64 = np.concatenate([sn, np.zeros((s, HEAD_DIM - PARTIAL_ROT))], axis=1)
    ns = s // SEL_LEN
    onehot = (np.arange(s)[:, None] // SEL_LEN == np.arange(ns)[None, :]) * -NEG_INF
    tables = [jnp.asarray(t, _F32) for t in (ck * q_scale, sk * q_scale, ck, sk, c64, s64)]
    return tables + [jnp.asarray(onehot, _MXU)]


def kernel(x, ffn1_norm, ffn1_w_gate, ffn1_w_up, ffn1_w_down, mix_norm, w_in, mla_q_norm, mla_w_uq, mla_kv_norm,
           mla_w_ukv, nsa_gate_bias, nsa_cmp_pos_k, nsa_cmp_w1_k, nsa_cmp_w2_k, nsa_cmp_pos_v, nsa_cmp_w1_v,
           nsa_cmp_w2_v, w_out, ffn2_norm, ffn2_w_gate, ffn2_w_up, ffn2_w_down, final_norm):
    b, s, d = x.shape
    depth = w_in.shape[0]
    tabs = _rope_tables(s)
    in_idx = _w_in_index()
    uq_idx, uqs_idx, uk_idx = _mla_up_index()
    half = CMP_LEN * HEAD_DIM // 2
    fg = final_norm.reshape(1, d)

    def row(p):
        return p[:, None, :]

    def cmp_weights(w1, w2, pos, transpose_out):
        pos = jnp.broadcast_to(pos.reshape(depth, 2, 1, half), (depth, 2, 8, half)).astype(_MXU)
        w2 = jnp.swapaxes(w2, -1, -2) if transpose_out else w2
        return w1.reshape(depth, 2, half, CMP_HIDDEN).astype(_MXU), w2.astype(_MXU), pos

    ffn1 = [w.astype(_MXU) for w in (ffn1_w_gate, ffn1_w_up, ffn1_w_down)]
    ffn2 = [w.astype(_MXU) for w in (ffn2_w_gate, ffn2_w_up, ffn2_w_down)]
    w_t, gate_bias = _transposed_weights(w_in, nsa_gate_bias)
    wuv_t = mla_w_ukv.reshape(depth, MLA_KV_LORA, MLA_HEADS, 2, MLA_V)[:, :, :, 1].transpose(0, 2, 3, 1).astype(_MXU)
    proj_params = (row(mix_norm), _gather_cols(w_in, in_idx), w_t,
                   row(mla_q_norm), _gather_cols(mla_w_uq, uq_idx), _gather_cols(mla_w_uq, uqs_idx),
                   row(mla_kv_norm), _gather_cols(mla_w_ukv, uk_idx), wuv_t, gate_bias)
    cmp_params = (cmp_weights(nsa_cmp_w1_k, nsa_cmp_w2_k, nsa_cmp_pos_k, False)
                  + cmp_weights(nsa_cmp_w1_v, nsa_cmp_w2_v, nsa_cmp_pos_v, True))
    w_out = w_out.astype(_MXU)
    ffn1_norm, ffn2_norm = row(ffn1_norm), row(ffn2_norm)

    for l in range(depth):
        x2d = _ffn(x.reshape(b * s, d), ffn1_norm, *ffn1, l, fg, False)
        x = x2d.reshape(b, s, d)
        (mq, mk, mvt, nq, nkc, nvc, nks, nvst, nkw, nvwt, gates_t, sbq, sbk, sbv) = _proj(x, l, *proj_params, tabs)
        o_mla = _mla_attention(mq, mk, mvt)
        kc, vct = _compress(nkc, nvc, l, *cmp_params)
        o_cmp, q_sel = _cmp_select(nq, kc, vct, gates_t)
        o_sel = _sel_attention(q_sel, nks, nvst, gates_t)
        o_win = _win_attention(nq, nkw, nvwt, gates_t)
        o_sb = _sb_attention(sbq, sbk, sbv)
        heads = [o.reshape(b * s, -1) for o in (o_mla, o_cmp, o_sel, o_win, o_sb)]
        x2d = _ffn(x.reshape(b * s, d), ffn2_norm, *ffn2, l, fg, l == depth - 1, mixer=(*heads, w_out))
        x = x2d.reshape(b, s, d)
    return x
```

```python
import functools
import math

import numpy as np
import jax
import jax.numpy as jnp
from jax import lax
from jax.experimental import pallas as pl
from jax.experimental.pallas import tpu as pltpu

D_MODEL = 1024
HEAD_DIM = 64
MLA_HEADS = 6
MLA_NOPE = 64
MLA_ROPE = 32
MLA_V = 64
MLA_Q_LORA = 256
MLA_KV_LORA = 128
NSA_HEADS = 6
NSA_KV_HEADS = 2
NSA_GROUP = NSA_HEADS // NSA_KV_HEADS
NSA_BRANCHES = 3
CMP_LEN = 32
CMP_STRIDE = 16
CMP_HIDDEN = 128
SEL_LEN = 64
SEL_TOPK = 16
WINDOW = 512
SB_HEADS = 4
D_FF = 2816
ROPE_THETA = 500000.0
PARTIAL_ROT = HEAD_DIM // 4
EPS = 1e-6
NEG_INF = -1e30
M_FLOOR = 0.1 * NEG_INF
N_FORCED = 3
PICKED = -3e38
F32_EXP2_ZERO = -151.0
LOG2_E = math.log2(math.e)
N_GATES = NSA_HEADS * NSA_BRANCHES

LANES = 128
FFN_CHUNK = 256
SWEEP_UNROLL = 4
TOPK_ROW_STEPS = (32, 64)
SB_KEY_BLOCK = 256
PROJ_ROWS = 512
TOKEN_CHUNK = 256
ONES_PAD = 16
VMEM_LIMIT = 56 * 1024 * 1024

_MXU = jnp.bfloat16
_F32 = jnp.float32

_IN_WIDTHS = (MLA_Q_LORA, MLA_KV_LORA, MLA_ROPE, NSA_HEADS * HEAD_DIM) + (NSA_KV_HEADS * HEAD_DIM,) * 6 + (
    N_GATES, SB_HEADS * HEAD_DIM, SB_HEADS * HEAD_DIM, SB_HEADS * HEAD_DIM)
_IN_OFF = np.concatenate([[0], np.cumsum(_IN_WIDTHS)])
(_O_CQ, _O_CKV, _O_KR, _O_NQ, _O_NKC, _O_NVC, _O_NKS, _O_NVS, _O_NKW, _O_NVW, _O_GATE, _O_SBQ, _O_SBK,
 _O_SBV) = [int(v) for v in _IN_OFF[:-1]]

_S_CQ, _S_CKV, _S_KR, _S_KRS = 0, 2, 3, 4
_H_NQ, _H_NQS = 10, 16
_H_KC, _H_KCS, _H_VC = 22, 24, 26
_H_KS, _H_KSS = 28, 30
_H_KW, _H_KWS = 32, 34
_H_SBQ, _H_SBK = 36, 40
_N_HEAD_COLS = 44
_T_VS, _T_VW, _T_GATE = 0, NSA_KV_HEADS * HEAD_DIM, 2 * NSA_KV_HEADS * HEAD_DIM
GATE_ROWS = 24
_T_SBV = _T_GATE + 2 * ONES_PAD
_T_ROWS = _T_SBV + SB_HEADS * HEAD_DIM


def _dot(a, b):
    return jnp.dot(a.astype(_MXU), b.astype(_MXU), preferred_element_type=_F32)


def _dot_nt(a, b):
    return lax.dot_general(a.astype(_MXU), b.astype(_MXU), (((1,), (1,)), ((), ())),
                           preferred_element_type=_F32)


def _dot_split_rhs(a, b):
    hi = b.astype(_MXU)
    lo = (b - hi.astype(_F32)).astype(_MXU)
    return (jnp.dot(a, hi, preferred_element_type=_F32) + jnp.dot(a, lo, preferred_element_type=_F32))


def _rms(x, g):
    return x * lax.rsqrt(jnp.mean(x * x, axis=-1, keepdims=True) + EPS) * g


def _params(*sem):
    return pltpu.CompilerParams(dimension_semantics=sem, vmem_limit_bytes=VMEM_LIMIT)


def _layer_spec(a, layer):
    return pl.BlockSpec((None,) + a.shape[1:], lambda *_: (layer,) + (0,) * (a.ndim - 1))


def _mixer_output(mla_ref, cmp_ref, sel_ref, win_ref, sb_ref, w_ref):
    def w_rows(first_head, n_heads):
        return w_ref[first_head * HEAD_DIM:(first_head + n_heads) * HEAD_DIM, :]

    nsa = cmp_ref[...].astype(_F32) + sel_ref[...].astype(_F32) + win_ref[...].astype(_F32)
    return (jnp.dot(mla_ref[...], w_rows(0, MLA_HEADS), preferred_element_type=_F32)
            + _dot(nsa, w_rows(MLA_HEADS, NSA_HEADS))
            + jnp.dot(sb_ref[...], w_rows(MLA_HEADS + NSA_HEADS, SB_HEADS), preferred_element_type=_F32))


def _ffn_kernel(x_ref, g_ref, wg_ref, wu_ref, wd_ref, fg_ref, *rest, final_norm, with_mixer):
    *mixer_refs, o_ref, act_ref = rest
    x = x_ref[...]
    if with_mixer:
        x = x + _mixer_output(*mixer_refs)
    h = _rms(x, g_ref[...]).astype(_MXU)
    tf = act_ref.shape[1]
    for c0 in range(0, tf, FFN_CHUNK):
        c1 = min(c0 + FFN_CHUNK, tf)
        gate = jnp.dot(h, wg_ref[:, c0:c1], preferred_element_type=_F32)
        up = jnp.dot(h, wu_ref[:, c0:c1], preferred_element_type=_F32)
        act_ref[:, c0:c1] = (gate * jax.nn.sigmoid(gate) * up).astype(act_ref.dtype)
    y = x + 0.5 * jnp.dot(act_ref[...], wd_ref[...], preferred_element_type=_F32)
    if final_norm:
        y = _rms(y, fg_ref[...])
    o_ref[...] = y


def _ffn(x2d, g, wg, wu, wd, layer, fg, final_norm, mixer=None):
    rows = x2d.shape[0]
    tm = min(1024, rows)

    def resident(a):
        return pl.BlockSpec((None,) + a.shape[1:], lambda i: (layer, 0, 0), pipeline_mode=pl.Buffered(1))

    def row_tile(a):
        return pl.BlockSpec((tm, a.shape[1]), lambda i: (i, 0))

    mixer_args, mixer_specs = (), []
    if mixer is not None:
        *heads, w_out = mixer
        mixer_args = (*heads, w_out)
        mixer_specs = [row_tile(a) for a in heads] + [resident(w_out)]
    return pl.pallas_call(
        functools.partial(_ffn_kernel, final_norm=final_norm, with_mixer=mixer is not None),
        grid=(rows // tm,),
        in_specs=[row_tile(x2d), _layer_spec(g, layer), resident(wg), resident(wu), resident(wd),
                  pl.BlockSpec((1, D_MODEL), lambda i: (0, 0))] + mixer_specs,
        out_specs=row_tile(x2d),
        out_shape=jax.ShapeDtypeStruct((rows, D_MODEL), _F32),
        scratch_shapes=[pltpu.VMEM((tm, D_FF), _MXU)],
        compiler_params=_params("parallel"),
        name="ffn",
    )(x2d, g, wg, wu, wd, fg, *mixer_args)


def _proj_kernel(x_ref, g_ref, w_ref, wt_ref, qn_ref, wuq_ref, wuqs_ref, kvn_ref, wuk_ref, wuv_ref, gb_ref,
                 cq_ref, sq_ref, ck_ref, sk_ref, c64_ref, s64_ref, oh_ref,
                 mq_ref, mk_ref, mv_ref, nq_ref, nkc_ref, nvc_ref, nks_ref, nvs_ref, nkw_ref, nvw_ref,
                 gate_ref, sbq_ref, sbk_ref, sbv_ref, stage_ref):
    hn = _rms(x_ref[0], g_ref[...]).astype(_MXU)

    def proj(h0, h1):
        return jnp.dot(hn, w_ref[:, h0 * HEAD_DIM:h1 * HEAD_DIM], preferred_element_type=_F32)

    def slot(p, s):
        return p[:, s * LANES:(s + 1) * LANES]

    def head(p, i):
        return p[:, i * HEAD_DIM:(i + 1) * HEAD_DIM]

    def write_value_chunks(o_ref, h, vt):
        width = o_ref.shape[-1]
        for c in range(o_ref.shape[2]):
            o_ref[0, h, c] = vt[:, c * width:(c + 1) * width].astype(o_ref.dtype)

    p = proj(0, _H_NQ)
    cq = _rms(p[:, :MLA_Q_LORA], qn_ref[...])
    ckv = _rms(slot(p, _S_CKV), kvn_ref[...])
    q = _dot(cq, wuq_ref[...])
    q_partner = _dot(cq, wuqs_ref[...])
    kpe = slot(p, _S_KR) * ck_ref[...] + slot(p, _S_KRS) * sk_ref[...]
    kn = _dot(ckv, wuk_ref[...])
    for h in range(MLA_HEADS):
        mq_ref[0, h] = (slot(q, h) * cq_ref[...] + slot(q_partner, h) * sq_ref[...]).astype(mq_ref.dtype)
        mk_ref[0, h] = (slot(kn, h) + kpe).astype(mk_ref.dtype)
        write_value_chunks(mv_ref, h, _ones_row_pad(_dot_nt(wuv_ref[h], ckv)))

    c64 = c64_ref[...]
    s64 = s64_ref[...]
    scale = HEAD_DIM ** -0.5

    p = proj(_H_NQ, _H_KC)
    for h in range(NSA_HEADS):
        nq_ref[0, h] = ((head(p, h) * c64 + head(p, NSA_HEADS + h) * s64) * (scale * LOG2_E)).astype(nq_ref.dtype)

    p = proj(_H_KC, _H_SBQ)
    base = _H_KC
    ns = oh_ref.shape[-1]

    def roped(hk, hks, g):
        return head(p, hk - base + g) * c64 + head(p, hks - base + g) * s64

    def write_chunked(o_ref, g, val):
        stage_ref[...] = val
        for t in range(CMP_STRIDE):
            piece = stage_ref[pl.ds(t, val.shape[0] // CMP_STRIDE, stride=CMP_STRIDE), :]
            o_ref[0, g, :, t * HEAD_DIM:(t + 1) * HEAD_DIM] = piece.astype(o_ref.dtype)

    for g in range(NSA_KV_HEADS):
        write_chunked(nkc_ref, g, roped(_H_KC, _H_KCS, g))
        write_chunked(nvc_ref, g, head(p, _H_VC - base + g))
        nks_ref[0, g, :, 0:ns] = oh_ref[...]
        nks_ref[0, g, :, ns:ns + HEAD_DIM] = roped(_H_KS, _H_KSS, g).astype(nks_ref.dtype)
        nkw_ref[0, g] = roped(_H_KW, _H_KWS, g).astype(nkw_ref.dtype)

    pt = _dot_nt(wt_ref[...], hn)
    for g in range(NSA_KV_HEADS):
        lo = g * HEAD_DIM
        write_value_chunks(nvs_ref, g, _ones_row_pad(pt[_T_VS + lo:_T_VS + lo + HEAD_DIM]))
        write_value_chunks(nvw_ref, g, _ones_row_pad(pt[_T_VW + lo:_T_VW + lo + HEAD_DIM]))
    gate_ref[0] = jax.nn.sigmoid(pt[_T_GATE:_T_GATE + GATE_ROWS] + gb_ref[...])

    p = proj(_H_SBQ, _N_HEAD_COLS)
    for h in range(SB_HEADS):
        sbq_ref[0, h] = (head(p, h) * (scale * LOG2_E)).astype(sbq_ref.dtype)
        sbk_ref[0, h] = head(p, SB_HEADS + h).astype(sbk_ref.dtype)
        write_value_chunks(sbv_ref, h, pt[_T_SBV + h * HEAD_DIM:_T_SBV + (h + 1) * HEAD_DIM])


def _proj(x, layer, g, w_ext, w_t, qn, wuq, wuqs, kvn, wuk, wuv, gb, tabs):
    b, s, _ = x.shape
    ts = min(PROJ_ROWS, s)
    tv = min(TOKEN_CHUNK, ts)
    cq, sq, ck, sk, c64, s64, onehot = tabs
    full = functools.partial(_layer_spec, layer=layer)

    def tab(a):
        return pl.BlockSpec((ts, a.shape[1]), lambda bi, i: (i, 0))

    def heads(n, d):
        return (pl.BlockSpec((1, n, ts, d), lambda bi, i: (bi, 0, i, 0)),
                jax.ShapeDtypeStruct((b, n, s, d), _MXU))

    def values_t(n):
        return (pl.BlockSpec((1, n, ts // tv, HEAD_DIM + ONES_PAD, tv), lambda bi, i: (bi, 0, i, 0, 0)),
                jax.ShapeDtypeStruct((b, n, s // tv, HEAD_DIM + ONES_PAD, tv), _MXU))

    def chunked():
        return (pl.BlockSpec((1, NSA_KV_HEADS, ts // CMP_STRIDE, CMP_STRIDE * HEAD_DIM), lambda bi, i: (bi, 0, i, 0)),
                jax.ShapeDtypeStruct((b, NSA_KV_HEADS, s // CMP_STRIDE, CMP_STRIDE * HEAD_DIM), _MXU))

    outs = [heads(MLA_HEADS, LANES), heads(MLA_HEADS, LANES), values_t(MLA_HEADS), heads(NSA_HEADS, HEAD_DIM),
            chunked(), chunked(), heads(NSA_KV_HEADS, onehot.shape[1] + HEAD_DIM), values_t(NSA_KV_HEADS),
            heads(NSA_KV_HEADS, HEAD_DIM), values_t(NSA_KV_HEADS),
            (pl.BlockSpec((1, GATE_ROWS, ts), lambda bi, i: (bi, 0, i)), jax.ShapeDtypeStruct((b, GATE_ROWS, s), _F32)),
            heads(SB_HEADS, HEAD_DIM), heads(SB_HEADS, HEAD_DIM),
            (pl.BlockSpec((1, SB_HEADS, ts // SB_KEY_BLOCK, HEAD_DIM, SB_KEY_BLOCK), lambda bi, i: (bi, 0, i, 0, 0)),
             jax.ShapeDtypeStruct((b, SB_HEADS, s // SB_KEY_BLOCK, HEAD_DIM, SB_KEY_BLOCK), _MXU))]
    return pl.pallas_call(
        _proj_kernel,
        grid=(b, s // ts),
        in_specs=[pl.BlockSpec((1, ts, D_MODEL), lambda bi, i: (bi, i, 0)), full(g), full(w_ext), full(w_t), full(qn),
                  full(wuq), full(wuqs), full(kvn), full(wuk), full(wuv), full(gb),
                  tab(cq), tab(sq), tab(ck), tab(sk), tab(c64), tab(s64), tab(onehot)],
        out_specs=[o[0] for o in outs],
        out_shape=[o[1] for o in outs],
        scratch_shapes=[pltpu.VMEM((ts, HEAD_DIM), _F32)],
        compiler_params=_params("parallel", "parallel"),
        name="proj",
    )(x, g, w_ext, w_t, qn, wuq, wuqs, kvn, wuk, wuv, gb, cq, sq, ck, sk, c64, s64, onehot)


def _ones_row_pad(vt):
    first = lax.broadcasted_iota(jnp.int32, (ONES_PAD, vt.shape[1]), 0) == 0
    return jnp.concatenate([vt, jnp.where(first, 1.0, 0.0).astype(vt.dtype)], axis=0)


def _softmax_step_t(carry, st, vt_chunks):
    m, acc = carry
    m_new = jnp.maximum(m, jnp.max(st, axis=0, keepdims=True))
    alpha = jnp.exp2(m - m_new)
    pt = jnp.exp2(st - m_new).astype(_MXU)
    n = st.shape[0] // len(vt_chunks)
    pv = sum(jnp.dot(vt, pt[c * n:(c + 1) * n], preferred_element_type=_F32) for c, vt in enumerate(vt_chunks))
    return m_new, alpha * acc + pv


def _softmax_init_t(d, cols):
    return (jnp.full((1, cols), M_FLOOR, _F32), jnp.zeros((d + ONES_PAD, cols), _F32))


def _softmax_finish_t(carry, d):
    _, acc = carry
    return acc[:d] * (1.0 / acc[d:d + 1])


def _two_chain_sweep(n_full, qk, soft, init):
    def body(j, carry, diag=False):
        c0, c1 = carry
        qk(0, j)
        c1 = soft(1, j, c1, diag)
        qk(1, jnp.zeros_like(j) if diag else j + 1)
        c0 = soft(0, j, c0, diag)
        return c0, c1

    qk(1, n_full)
    carry = body(n_full, init, True)
    done = jnp.zeros_like(n_full)
    unroll = SWEEP_UNROLL
    while unroll >= 1:
        def unrolled(i, carry, first=done, unroll=unroll):
            for u in range(unroll):
                carry = body(first + unroll * i + u, carry)
            return carry

        trips = (n_full - done) // unroll
        carry = lax.fori_loop(0, trips, unrolled, carry)
        done = done + trips * unroll
        unroll //= 2
    return carry


def _mla_kernel(q_ref, k_ref, vt_ref, o_ref, s0_ref, s1_ref, *, t, nsub):
    qi = pl.program_id(2)
    s_refs = (s0_ref, s1_ref)

    def qk(hh, j):
        off = pl.multiple_of(j * t, t)
        s_refs[hh][...] = _dot_nt(k_ref[0, hh, pl.ds(off, t), :], q_ref[0, hh])

    def soft(hh, j, carry, diag):
        st = s_refs[hh][...]
        if diag:
            key = lax.broadcasted_iota(jnp.int32, (t, t), 0)
            qry = lax.broadcasted_iota(jnp.int32, (t, t), 1)
            st = jnp.where(key <= qry, st, NEG_INF)
        return _softmax_step_t(carry, st, [vt_ref[0, hh, j * nsub + c] for c in range(nsub)])

    carry = _two_chain_sweep(qi, qk, soft, tuple(_softmax_init_t(MLA_V, t) for _ in range(2)))
    ot = jnp.concatenate([_softmax_finish_t(c, MLA_V) for c in carry], axis=0)
    o_ref[0] = ot.T.astype(o_ref.dtype)


def _mla_attention(q, k, vt):
    b, h, s, _ = q.shape
    tv = vt.shape[-1]
    dv = vt.shape[-2]
    t = min(512, s)
    assert h % 2 == 0 and 2 * MLA_V == LANES and t % tv == 0 and s % t == 0
    return pl.pallas_call(
        functools.partial(_mla_kernel, t=t, nsub=t // tv),
        grid=(b, h // 2, s // t),
        in_specs=[pl.BlockSpec((1, 2, t, LANES), lambda bi, hi, i: (bi, hi, i, 0)),
                  pl.BlockSpec((1, 2, s, LANES), lambda bi, hi, i: (bi, hi, 0, 0)),
                  pl.BlockSpec((1, 2, s // tv, dv, tv), lambda bi, hi, i: (bi, hi, 0, 0, 0))],
        out_specs=pl.BlockSpec((1, t, LANES), lambda bi, hi, i: (bi, i, hi)),
        out_shape=jax.ShapeDtypeStruct((b, s, h * MLA_V), _MXU),
        scratch_shapes=[pltpu.VMEM((t, t), _F32), pltpu.VMEM((t, t), _F32)],
        compiler_params=_params("parallel", "parallel", "arbitrary"),
        name="mla_attn",
    )(q, k, vt)


def _sb_kernel(q_ref, k_ref, vt_ref, u_ref, o_ref, *scratch_refs, tq, tk):
    qi = pl.program_id(1)
    u = u_ref[...]
    n_heads = q_ref.shape[1]
    scratch = [scratch_refs[6 * hh:6 * (hh + 1)] for hh in range(n_heads)]
    per_tile = tq // tk

    def step(j, carry, key_offset=None):
        diag = key_offset is not None
        off = pl.multiple_of(j * tk, tk)
        if diag:
            key = key_offset + lax.broadcasted_iota(jnp.int32, (tk, tq), 0)
            qry = lax.broadcasted_iota(jnp.int32, (tk, tq), 1)
            strict = key < qry

        def logits(hh):
            z_ref, _, _, _, _, _ = scratch[hh]
            z_ref[...] = _dot_nt(k_ref[0, hh, pl.ds(off, tk), :], q_ref[0, hh])

        def log_terms(hh):
            z_ref, lb_ref, hi_ref, lo_ref, _, _ = scratch[hh]
            z = z_ref[...]
            log_beta = jnp.minimum(z, 0.0) - jnp.log2(1.0 + jnp.exp2(-jnp.abs(z)))
            log_rem = log_beta - z
            if diag:
                log_rem = jnp.where(strict, log_rem, 0.0)
            hi = log_rem.astype(_MXU)
            lb_ref[...] = log_beta
            hi_ref[...] = hi
            lo_ref[...] = (log_rem - hi.astype(_F32)).astype(_MXU)
            return log_rem[0:1, :]

        def suffix_sums(hh):
            _, _, hi_ref, lo_ref, sfx_ref, _ = scratch[hh]
            sfx_ref[...] = (jnp.dot(u, hi_ref[...], preferred_element_type=_F32)
                            + jnp.dot(u, lo_ref[...], preferred_element_type=_F32))

        def weights(hh, first_rem):
            _, lb_ref, _, _, sfx_ref, a_ref = scratch[hh]
            rem = carry[hh][0]
            suffix = sfx_ref[...]
            a = jnp.exp2(lb_ref[...] + suffix + rem)
            if diag:
                a = jnp.where(strict, a, 0.0)
            a_ref[...] = a.astype(_MXU)
            return rem + suffix[0:1, :] + first_rem

        def values(hh):
            a_ref = scratch[hh][5]
            return carry[hh][1] + jnp.dot(vt_ref[0, hh, j], a_ref[...], preferred_element_type=_F32)

        heads = range(n_heads)
        for hh in heads:
            logits(hh)
        first = []
        for hh in heads:
            first.append(log_terms(hh))
            suffix_sums(hh)
        rems = [weights(hh, first[hh]) for hh in heads]
        alive = jnp.max(functools.reduce(jnp.maximum, rems)) > F32_EXP2_ZERO
        return alive, tuple((rems[hh], values(hh)) for hh in heads)

    carry = tuple((jnp.zeros((1, tq), _F32), jnp.zeros((HEAD_DIM, tq), _F32)) for _ in range(n_heads))
    first = qi * per_tile
    for i in reversed(range(per_tile)):
        _, carry = step(first + i, carry, key_offset=i * tk)
    has_past = first > 0
    carry = tuple((jnp.where(has_past, rem, NEG_INF), acc) for rem, acc in carry)
    alive, carry = step(jnp.maximum(first - 1, 0), carry)

    def earlier(c):
        return (c[0] - 1,) + step(c[0], c[2])

    _, _, carry = lax.while_loop(lambda c: jnp.logical_and(c[0] >= 0, c[1]), earlier, (first - 2, alive, carry))
    o_ref[0] = jnp.concatenate([acc for _, acc in carry], axis=0).T.astype(o_ref.dtype)


def _sb_attention(q, k, vt):
    b, h, s, d = q.shape
    tk = vt.shape[-1]
    tq = tk
    assert (h * d) % LANES == 0 and s % tq == 0 and tq % tk == 0
    idx = np.arange(tk)
    u = jnp.asarray(idx[None, :] > idx[:, None], _MXU)
    return pl.pallas_call(
        functools.partial(_sb_kernel, tq=tq, tk=tk),
        grid=(b, s // tq),
        in_specs=[pl.BlockSpec((1, h, tq, d), lambda bi, i: (bi, 0, i, 0)),
                  pl.BlockSpec((1, h, s, d), lambda bi, i: (bi, 0, 0, 0)),
                  pl.BlockSpec((1, h, s // tk, d, tk), lambda bi, i: (bi, 0, 0, 0, 0)),
                  pl.BlockSpec((tk, tk), lambda bi, i: (0, 0))],
        out_specs=pl.BlockSpec((1, tq, h * d), lambda bi, i: (bi, i, 0)),
        out_shape=jax.ShapeDtypeStruct((b, s, h * d), _MXU),
        scratch_shapes=[pltpu.VMEM((tk, tq), dt) for _ in range(h) for dt in (_F32, _F32, _MXU, _MXU, _F32, _MXU)],
        compiler_params=_params("parallel", "arbitrary"),
        name="sb_attn",
    )(q, k, vt, u)


def _compress_kernel(xk_ref, xv_ref, w1k_ref, w2k_ref, pk_ref, w1v_ref, w2v_ref, pv_ref, ok_ref, ov_ref):
    def hidden(x_ref, w1_ref, p_ref):
        x = x_ref[0, 0]
        n = x.shape[0]
        first = jnp.dot(x, w1_ref[0], preferred_element_type=_F32)
        second = jnp.dot(x, w1_ref[1], preferred_element_type=_F32)
        pos = _dot(p_ref[0], w1_ref[0]) + _dot(p_ref[1], w1_ref[1])
        hid = first + pltpu.roll(second, n - 1, 0) + pos[0:1]
        return 0.5 * hid * (1.0 + jnp.tanh(math.sqrt(2.0 / math.pi) * (hid + 0.044715 * hid * hid * hid)))

    ok_ref[0, 0] = _dot(hidden(xk_ref, w1k_ref, pk_ref), w2k_ref[...]).astype(ok_ref.dtype)
    ov_ref[0, 0] = _dot_nt(w2v_ref[...], hidden(xv_ref, w1v_ref, pv_ref)).astype(ov_ref.dtype)


def _compress(xk, xv, layer, w1k, w2k, pk, w1v, w2v, pv):
    b, g, n, _ = xk.shape
    d = HEAD_DIM
    full = functools.partial(_layer_spec, layer=layer)

    xspec = pl.BlockSpec((1, 1, n, CMP_STRIDE * d), lambda bi, gi: (bi, gi, 0, 0))
    return pl.pallas_call(
        _compress_kernel,
        grid=(b, g),
        in_specs=[xspec, xspec, full(w1k), full(w2k), full(pk), full(w1v), full(w2v), full(pv)],
        out_specs=[pl.BlockSpec((1, 1, n, d), lambda bi, gi: (bi, gi, 0, 0)),
                   pl.BlockSpec((1, 1, d, n), lambda bi, gi: (bi, gi, 0, 0))],
        out_shape=[jax.ShapeDtypeStruct((b, g, n, d), _MXU), jax.ShapeDtypeStruct((b, g, d, n), _MXU)],
        compiler_params=_params("parallel", "parallel"),
        name="nsa_compress",
    )(xk, xv, w1k, w2k, pk, w1v, w2v, pv)


def _group_queries(q_ref, g, tq):
    return q_ref[0, g * NSA_GROUP:(g + 1) * NSA_GROUP].reshape(NSA_GROUP * tq, q_ref.shape[-1])


def _gated_heads(ot, gt_ref, g, branch, tq):
    out = []
    for r in range(NSA_GROUP):
        row = NSA_BRANCHES * (g * NSA_GROUP + r) + branch
        out.append(ot[:, r * tq:(r + 1) * tq] * gt_ref[0, row:row + 1, :])
    return out


def _cmp_kernel(q_ref, kc_ref, vct_ref, ov_ref, gt_ref, o_ref, qa_ref, s0_ref, s1_ref, *, tq, n_top):
    q0 = pl.program_id(1) * tq
    ncp = kc_ref.shape[2]
    ns = ov_ref.shape[0]
    lanes = NSA_GROUP * tq
    s_refs = (s0_ref, s1_ref)
    for g in range(NSA_KV_HEADS):
        s_refs[g][...] = _dot_nt(kc_ref[0, g], _group_queries(q_ref, g, tq))
    qpos = q0 + (lax.broadcasted_iota(jnp.int32, (1, lanes), 1) & (tq - 1))
    cmp_end = lax.broadcasted_iota(jnp.int32, (ncp, 1), 0) * CMP_STRIDE + (CMP_LEN - 1)
    visible = cmp_end <= qpos
    cur = jnp.right_shift(q0 + lax.broadcasted_iota(jnp.int32, (1, tq), 1), int(math.log2(SEL_LEN)))
    blk = lax.broadcasted_iota(jnp.int32, (ns, 1), 0)
    forced = (blk == 0) | (blk == cur) | (blk == cur - 1)
    future = blk > cur
    blk_f = blk.astype(_F32)
    heads = []
    scores = []
    for g in range(NSA_KV_HEADS):
        st = jnp.where(visible, s_refs[g][...], NEG_INF)
        e = jnp.exp2(st - jnp.max(st, axis=0, keepdims=True))
        inv = jnp.where(qpos >= CMP_LEN - 1, 1.0 / jnp.sum(e, axis=0, keepdims=True), 0.0)
        pt = e * inv
        heads += _gated_heads(_dot(vct_ref[0, g], pt), gt_ref, g, 0, tq)
        p_sum = sum(pt[:, r * tq:(r + 1) * tq] for r in range(NSA_GROUP))
        score = _dot_split_rhs(ov_ref[...], p_sum)
        scores.append(jnp.where(forced, PICKED, jnp.where(future, -1.0, score)))
    o_ref[0] = jnp.concatenate(heads, axis=0).T.astype(o_ref.dtype)
    def select(rows):
        sc = [s[:rows] for s in scores]
        idx = blk_f[:rows]
        for _ in range(n_top - N_FORCED):
            for g in range(NSA_KV_HEADS):
                top = jnp.max(sc[g], axis=0, keepdims=True)
                first = jnp.min(jnp.where(sc[g] == top, idx, float(ns)), axis=0, keepdims=True)
                sc[g] = jnp.where(idx == first, PICKED, sc[g])
        for g in range(NSA_KV_HEADS):
            sel_m1 = jnp.where(sc[g] < 0.5 * PICKED, 0.0, -1.0)
            if rows < ns:
                sel_m1 = jnp.concatenate([sel_m1, jnp.full((ns - rows, tq), -1.0, _F32)], axis=0)
            sel_m1 = sel_m1.T.astype(qa_ref.dtype)
            for h in range(g * NSA_GROUP, (g + 1) * NSA_GROUP):
                qa_ref[0, h, :, 0:ns] = sel_m1
                qa_ref[0, h, :, ns:ns + HEAD_DIM] = q_ref[0, h]

    visible_blocks = (q0 + tq) // SEL_LEN
    lower = 0
    for rows in sorted({min(ns, r) for r in TOPK_ROW_STEPS} | {ns}):
        in_range = visible_blocks > lower
        if rows < ns:
            in_range = jnp.logical_and(in_range, visible_blocks <= rows)
        pl.when(in_range)(functools.partial(select, rows))
        lower = rows


def _cmp_select(q, kc, vct, gates_t):
    b, h, s, d = q.shape
    g = kc.shape[1]
    ncp = kc.shape[2]
    ns = s // SEL_LEN
    n_top = min(SEL_TOPK, ns)
    tq = min(256, s)
    assert tq & (tq - 1) == 0 and g == 2 and n_top >= N_FORCED
    c0 = np.arange(ncp)[:, None] * CMP_STRIDE
    n0 = np.arange(ns)[None, :] * SEL_LEN
    overlap = jnp.asarray(((c0 < n0 + SEL_LEN) & (c0 + CMP_LEN > n0)).T, _MXU)
    return pl.pallas_call(
        functools.partial(_cmp_kernel, tq=tq, n_top=n_top),
        grid=(b, s // tq),
        in_specs=[pl.BlockSpec((1, h, tq, d), lambda bi, i: (bi, 0, i, 0)),
                  pl.BlockSpec((1, g, ncp, d), lambda bi, i: (bi, 0, 0, 0)),
                  pl.BlockSpec((1, g, d, ncp), lambda bi, i: (bi, 0, 0, 0)),
                  pl.BlockSpec((ns, ncp), lambda bi, i: (0, 0)),
                  pl.BlockSpec((1, GATE_ROWS, tq), lambda bi, i: (bi, 0, i))],
        out_specs=[pl.BlockSpec((1, tq, h * d), lambda bi, i: (bi, i, 0)),
                   pl.BlockSpec((1, h, tq, ns + d), lambda bi, i: (bi, 0, i, 0))],
        out_shape=[jax.ShapeDtypeStruct((b, s, h * d), _MXU), jax.ShapeDtypeStruct((b, h, s, ns + d), _MXU)],
        scratch_shapes=[pltpu.VMEM((ncp, NSA_GROUP * tq), _F32) for _ in range(g)],
        compiler_params=_params("parallel", "arbitrary"),
        name="nsa_cmp_select",
    )(q, kc, vct, overlap, gates_t)


def _key_minus_query(keys, tq):
    return jnp.asarray(np.arange(keys)[:, None] - np.arange(NSA_GROUP * tq)[None, :] % tq, jnp.int32)


def _sel_kernel(q_ref, k_ref, vt_ref, gt_ref, rel_ref, o_ref, s0_ref, s1_ref, *, tq, tk, nsub):
    q0 = pl.program_id(1) * tq
    last = (q0 + tq - 1) // tk
    lanes = NSA_GROUP * tq
    s_refs = (s0_ref, s1_ref)

    def qk(g, j):
        off = pl.multiple_of(j * tk, tk)
        s_refs[g][...] = _dot_nt(k_ref[0, g, pl.ds(off, tk), :], _group_queries(q_ref, g, tq))

    def soft(g, j, carry, causal):
        st = s_refs[g][...]
        if causal:
            st = jnp.where(rel_ref[...] <= q0 - j * tk, st, NEG_INF)
        return _softmax_step_t(carry, st, [vt_ref[0, g, j * nsub + c] for c in range(nsub)])

    init = tuple(_softmax_init_t(HEAD_DIM, lanes) for _ in range(NSA_KV_HEADS))
    carry = _two_chain_sweep(last, qk, soft, init)
    heads = []
    for g in range(NSA_KV_HEADS):
        heads += _gated_heads(_softmax_finish_t(carry[g], HEAD_DIM), gt_ref, g, 1, tq)
    o_ref[0] = jnp.concatenate(heads, axis=0).T.astype(o_ref.dtype)


def _sel_attention(q, k, vt, gates_t):
    b, h, s, da = q.shape
    g = k.shape[1]
    d = HEAD_DIM
    tv = vt.shape[-1]
    tq = min(256, s)
    tk = min(512, s)
    assert tq & (tq - 1) == 0 and s % tk == 0 and tk % tv == 0 and g == 2
    return pl.pallas_call(
        functools.partial(_sel_kernel, tq=tq, tk=tk, nsub=tk // tv),
        grid=(b, s // tq),
        in_specs=[pl.BlockSpec((1, h, tq, da), lambda bi, i: (bi, 0, i, 0)),
                  pl.BlockSpec((1, g, s, da), lambda bi, i: (bi, 0, 0, 0)),
                  pl.BlockSpec((1, g) + vt.shape[2:], lambda bi, i: (bi, 0, 0, 0, 0)),
                  pl.BlockSpec((1, GATE_ROWS, tq), lambda bi, i: (bi, 0, i)),
                  pl.BlockSpec((tk, NSA_GROUP * tq), lambda bi, i: (0, 0))],
        out_specs=pl.BlockSpec((1, tq, h * d), lambda bi, i: (bi, i, 0)),
        out_shape=jax.ShapeDtypeStruct((b, s, h * d), _MXU),
        scratch_shapes=[pltpu.VMEM((tk, NSA_GROUP * tq), _F32) for _ in range(g)],
        compiler_params=_params("parallel", "arbitrary"),
        name="nsa_selected",
    )(q, k, vt, gates_t, _key_minus_query(tk, tq))


def _win_kernel(q_ref, k_ref, vt_ref, gt_ref, rel_ref, o_ref, *s_refs, tq, subs, span, tv):
    lanes = NSA_GROUP * tq
    rel = rel_ref[...]
    chains = [(sub, g) for sub in range(subs) for g in range(NSA_KV_HEADS)]

    def origin(sub):
        q0 = (pl.program_id(1) * subs + sub) * tq
        return q0, pl.multiple_of(jnp.maximum(q0 - WINDOW, 0), tq)

    for c, (sub, g) in enumerate(chains):
        _, start = origin(sub)
        q = q_ref[0, g * NSA_GROUP:(g + 1) * NSA_GROUP, sub * tq:(sub + 1) * tq].reshape(lanes, q_ref.shape[-1])
        s_refs[c][...] = _dot_nt(k_ref[0, g, pl.ds(start, span), :], q)
    for sub in range(subs):
        q0, start = origin(sub)
        offset = q0 - start
        heads = []
        for g in range(NSA_KV_HEADS):
            st = jnp.where(rel <= offset, s_refs[chains.index((sub, g))][...], NEG_INF)
            st = jnp.where(rel > offset - WINDOW, st, NEG_INF)
            carry = _softmax_step_t(_softmax_init_t(HEAD_DIM, lanes), st,
                                    [vt_ref[0, g, start // tv + c] for c in range(span // tv)])
            ot = _softmax_finish_t(carry, HEAD_DIM)
            for r in range(NSA_GROUP):
                row = NSA_BRANCHES * (g * NSA_GROUP + r) + 2
                heads.append(ot[:, r * tq:(r + 1) * tq] * gt_ref[0, row:row + 1, sub * tq:(sub + 1) * tq])
        o_ref[0, sub * tq:(sub + 1) * tq, :] = jnp.concatenate(heads, axis=0).T.astype(o_ref.dtype)


def _win_attention(q, k, vt, gates_t):
    b, h, s, d = q.shape
    g = k.shape[1]
    tv = vt.shape[-1]
    tq = min(256, s)
    subs = 2 if s % (2 * tq) == 0 else 1
    span = WINDOW + tq
    assert tq & (tq - 1) == 0 and s >= span and tq % tv == 0 and WINDOW % tv == 0 and g == NSA_KV_HEADS
    return pl.pallas_call(
        functools.partial(_win_kernel, tq=tq, subs=subs, span=span, tv=tv),
        grid=(b, s // (subs * tq)),
        in_specs=[pl.BlockSpec((1, h, subs * tq, d), lambda bi, i: (bi, 0, i, 0)),
                  pl.BlockSpec((1, g, s, d), lambda bi, i: (bi, 0, 0, 0)),
                  pl.BlockSpec((1, g) + vt.shape[2:], lambda bi, i: (bi, 0, 0, 0, 0)),
                  pl.BlockSpec((1, GATE_ROWS, subs * tq), lambda bi, i: (bi, 0, i)),
                  pl.BlockSpec((span, NSA_GROUP * tq), lambda bi, i: (0, 0))],
        out_specs=pl.BlockSpec((1, subs * tq, h * d), lambda bi, i: (bi, i, 0)),
        out_shape=jax.ShapeDtypeStruct((b, s, h * d), _MXU),
        scratch_shapes=[pltpu.VMEM((span, NSA_GROUP * tq), _F32) for _ in range(subs * g)],
        compiler_params=_params("parallel", "arbitrary"),
        name="nsa_window",
    )(q, k, vt, gates_t, _key_minus_query(span, tq))


def _gather_cols(w, idx):
    idx = np.asarray(idx)
    cuts = [0] + [i for i in range(1, len(idx)) if idx[i] != idx[i - 1] + (idx[i - 1] >= 0)] + [len(idx)]
    pieces = []
    for a, b in zip(cuts[:-1], cuts[1:]):
        if idx[a] < 0:
            pieces.append(jnp.zeros(w.shape[:-1] + (b - a,), _MXU))
        else:
            pieces.append(w[..., int(idx[a]):int(idx[a]) + b - a].astype(_MXU))
    return jnp.concatenate(pieces, axis=-1)


def _swap_halves(rot):
    return (np.arange(rot) + rot // 2) % rot


def _w_in_index():
    idx = np.full((_N_HEAD_COLS * HEAD_DIM,), -1, np.int64)

    def put(col, src):
        src = np.asarray(src)
        idx[col:col + len(src)] = src

    def put_head(pos, src):
        put(pos * HEAD_DIM, src)

    put(_S_CQ * LANES, _O_CQ + np.arange(MLA_Q_LORA))
    put(_S_CKV * LANES, _O_CKV + np.arange(MLA_KV_LORA))
    put(_S_KR * LANES + MLA_NOPE, _O_KR + np.arange(MLA_ROPE))
    put(_S_KRS * LANES + MLA_NOPE, _O_KR + _swap_halves(MLA_ROPE))
    for h in range(NSA_HEADS):
        put_head(_H_NQ + h, _O_NQ + h * HEAD_DIM + np.arange(HEAD_DIM))
        put_head(_H_NQS + h, _O_NQ + h * HEAD_DIM + _swap_halves(PARTIAL_ROT))
    for hk, hks, ok in ((_H_KC, _H_KCS, _O_NKC), (_H_KS, _H_KSS, _O_NKS), (_H_KW, _H_KWS, _O_NKW)):
        for g in range(NSA_KV_HEADS):
            put_head(hk + g, ok + g * HEAD_DIM + np.arange(HEAD_DIM))
            put_head(hks + g, ok + g * HEAD_DIM + _swap_halves(PARTIAL_ROT))
    for g in range(NSA_KV_HEADS):
        put_head(_H_VC + g, _O_NVC + g * HEAD_DIM + np.arange(HEAD_DIM))
    for h in range(SB_HEADS):
        put_head(_H_SBQ + h, _O_SBQ + h * HEAD_DIM + np.arange(HEAD_DIM))
        put_head(_H_SBK + h, _O_SBK + h * HEAD_DIM + np.arange(HEAD_DIM))
    return idx


def _mla_up_index():
    qd = MLA_NOPE + MLA_ROPE
    kd = MLA_NOPE + MLA_V
    uq = np.full((MLA_HEADS * LANES,), -1, np.int64)
    uqs = uq.copy()
    uk = uq.copy()
    for h in range(MLA_HEADS):
        uq[h * LANES:h * LANES + qd] = h * qd + np.arange(qd)
        uqs[h * LANES + MLA_NOPE:h * LANES + qd] = h * qd + MLA_NOPE + _swap_halves(MLA_ROPE)
        uk[h * LANES:h * LANES + MLA_NOPE] = h * kd + np.arange(MLA_NOPE)
    return uq, uqs, uk


def _transposed_weights(w_in, gate_bias):
    width = NSA_KV_HEADS * HEAD_DIM
    gate_rows = jnp.pad(w_in[..., _O_GATE:_O_GATE + N_GATES], ((0, 0), (0, 0), (0, _T_SBV - _T_GATE - N_GATES)))
    rows = jnp.concatenate([w_in[..., _O_NVS:_O_NVS + width], w_in[..., _O_NVW:_O_NVW + width], gate_rows,
                            w_in[..., _O_SBV:_O_SBV + SB_HEADS * HEAD_DIM]], axis=-1)
    bias = jnp.pad(gate_bias, ((0, 0), (0, GATE_ROWS - N_GATES)))[..., None]
    return jnp.swapaxes(rows, -1, -2).astype(_MXU), bias


def _rope_tables(s):
    pos = np.arange(s, dtype=np.float64)

    def cs(rot):
        half = rot // 2
        ang = pos[:, None] * (ROPE_THETA ** (-np.arange(half, dtype=np.float64) / half))[None, :]
        c, sn = np.cos(ang), np.sin(ang)
        return np.concatenate([c, c], axis=1), np.concatenate([-sn, sn], axis=1)

    c, sn = cs(MLA_ROPE)
    pad = np.zeros((s, LANES - MLA_NOPE - MLA_ROPE))
    ck = np.concatenate([np.ones((s, MLA_NOPE)), c, pad], axis=1)
    sk = np.concatenate([np.zeros((s, MLA_NOPE)), sn, pad], axis=1)
    q_scale = (MLA_NOPE + MLA_ROPE) ** -0.5 * LOG2_E
    c, sn = cs(PARTIAL_ROT)
    c64 = np.concatenate([c, np.ones((s, HEAD_DIM - PARTIAL_ROT))], axis=1)
    s64 = np.concatenate([sn, np.zeros((s, HEAD_DIM - PARTIAL_ROT))], axis=1)
    ns = s // SEL_LEN
    onehot = (np.arange(s)[:, None] // SEL_LEN == np.arange(ns)[None, :]) * -NEG_INF
    tables = [jnp.asarray(t, _F32) for t in (ck * q_scale, sk * q_scale, ck, sk, c64, s64)]
    return tables + [jnp.asarray(onehot, _MXU)]


def kernel(x, ffn1_norm, ffn1_w_gate, ffn1_w_up, ffn1_w_down, mix_norm, w_in, mla_q_norm, mla_w_uq, mla_kv_norm,
           mla_w_ukv, nsa_gate_bias, nsa_cmp_pos_k, nsa_cmp_w1_k, nsa_cmp_w2_k, nsa_cmp_pos_v, nsa_cmp_w1_v,
           nsa_cmp_w2_v, w_out, ffn2_norm, ffn2_w_gate, ffn2_w_up, ffn2_w_down, final_norm):
    b, s, d = x.shape
    depth = w_in.shape[0]
    tabs = _rope_tables(s)
    in_idx = _w_in_index()
    uq_idx, uqs_idx, uk_idx = _mla_up_index()
    half = CMP_LEN * HEAD_DIM // 2
    fg = final_norm.reshape(1, d)

    def row(p):
        return p[:, None, :]

    def cmp_weights(w1, w2, pos, transpose_out):
        pos = jnp.broadcast_to(pos.reshape(depth, 2, 1, half), (depth, 2, 8, half)).astype(_MXU)
        w2 = jnp.swapaxes(w2, -1, -2) if transpose_out else w2
        return w1.reshape(depth, 2, half, CMP_HIDDEN).astype(_MXU), w2.astype(_MXU), pos

    ffn1 = [w.astype(_MXU) for w in (ffn1_w_gate, ffn1_w_up, ffn1_w_down)]
    ffn2 = [w.astype(_MXU) for w in (ffn2_w_gate, ffn2_w_up, ffn2_w_down)]
    w_t, gate_bias = _transposed_weights(w_in, nsa_gate_bias)
    wuv_t = mla_w_ukv.reshape(depth, MLA_KV_LORA, MLA_HEADS, 2, MLA_V)[:, :, :, 1].transpose(0, 2, 3, 1).astype(_MXU)
    proj_params = (row(mix_norm), _gather_cols(w_in, in_idx), w_t,
                   row(mla_q_norm), _gather_cols(mla_w_uq, uq_idx), _gather_cols(mla_w_uq, uqs_idx),
                   row(mla_kv_norm), _gather_cols(mla_w_ukv, uk_idx), wuv_t, gate_bias)
    cmp_params = (cmp_weights(nsa_cmp_w1_k, nsa_cmp_w2_k, nsa_cmp_pos_k, False)
                  + cmp_weights(nsa_cmp_w1_v, nsa_cmp_w2_v, nsa_cmp_pos_v, True))
    w_out = w_out.astype(_MXU)
    ffn1_norm, ffn2_norm = row(ffn1_norm), row(ffn2_norm)

    for l in range(depth):
        x2d = _ffn(x.reshape(b * s, d), ffn1_norm, *ffn1, l, fg, False)
        x = x2d.reshape(b, s, d)
        (mq, mk, mvt, nq, nkc, nvc, nks, nvst, nkw, nvwt, gates_t, sbq, sbk, sbv) = _proj(x, l, *proj_params, tabs)
        o_mla = _mla_attention(mq, mk, mvt)
        kc, vct = _compress(nkc, nvc, l, *cmp_params)
        o_cmp, q_sel = _cmp_select(nq, kc, vct, gates_t)
        o_sel = _sel_attention(q_sel, nks, nvst, gates_t)
        o_win = _win_attention(nq, nkw, nvwt, gates_t)
        o_sb = _sb_attention(sbq, sbk, sbv)
        heads = [o.reshape(b * s, -1) for o in (o_mla, o_cmp, o_sel, o_win, o_sb)]
        x2d = _ffn(x.reshape(b * s, d), ffn2_norm, *ffn2, l, fg, l == depth - 1, mixer=(*heads, w_out))
        x = x2d.reshape(b, s, d)
    return x
```

```python
import functools
import math

import numpy as np
import jax
import jax.numpy as jnp
from jax import lax
from jax.experimental import pallas as pl
from jax.experimental.pallas import tpu as pltpu

D_MODEL = 1024
HEAD_DIM = 64
MLA_HEADS = 6
MLA_NOPE = 64
MLA_ROPE = 32
MLA_V = 64
MLA_Q_LORA = 256
MLA_KV_LORA = 128
NSA_HEADS = 6
NSA_KV_HEADS = 2
NSA_GROUP = NSA_HEADS // NSA_KV_HEADS
NSA_BRANCHES = 3
CMP_LEN = 32
CMP_STRIDE = 16
CMP_HIDDEN = 128
SEL_LEN = 64
SEL_TOPK = 16
WINDOW = 512
SB_HEADS = 4
D_FF = 2816
ROPE_THETA = 500000.0
PARTIAL_ROT = HEAD_DIM // 4
EPS = 1e-6
NEG_INF = -1e30
M_FLOOR = 0.1 * NEG_INF
N_FORCED = 3
PICKED = -3e38
F32_EXP2_ZERO = -151.0
LOG2_E = math.log2(math.e)
N_GATES = NSA_HEADS * NSA_BRANCHES

LANES = 128
FFN_CHUNK = 256
SWEEP_UNROLL = 4
TOPK_ROW_STEPS = (32, 64)
SB_KEY_BLOCK = 256
PROJ_ROWS = 512
TOKEN_CHUNK = 256
ONES_PAD = 16
VMEM_LIMIT = 56 * 1024 * 1024

_MXU = jnp.bfloat16
_F32 = jnp.float32

_IN_WIDTHS = (MLA_Q_LORA, MLA_KV_LORA, MLA_ROPE, NSA_HEADS * HEAD_DIM) + (NSA_KV_HEADS * HEAD_DIM,) * 6 + (
    N_GATES, SB_HEADS * HEAD_DIM, SB_HEADS * HEAD_DIM, SB_HEADS * HEAD_DIM)
_IN_OFF = np.concatenate([[0], np.cumsum(_IN_WIDTHS)])
(_O_CQ, _O_CKV, _O_KR, _O_NQ, _O_NKC, _O_NVC, _O_NKS, _O_NVS, _O_NKW, _O_NVW, _O_GATE, _O_SBQ, _O_SBK,
 _O_SBV) = [int(v) for v in _IN_OFF[:-1]]

_S_CQ, _S_CKV, _S_KR, _S_KRS = 0, 2, 3, 4
_H_NQ, _H_NQS = 10, 16
_H_KC, _H_KCS, _H_VC = 22, 24, 26
_H_KS, _H_KSS = 28, 30
_H_KW, _H_KWS = 32, 34
_H_SBQ, _H_SBK = 36, 40
_N_HEAD_COLS = 44
_T_VS, _T_VW, _T_GATE = 0, NSA_KV_HEADS * HEAD_DIM, 2 * NSA_KV_HEADS * HEAD_DIM
GATE_ROWS = 24
_T_SBV = _T_GATE + 2 * ONES_PAD
_T_ROWS = _T_SBV + SB_HEADS * HEAD_DIM


def _dot(a, b):
    return jnp.dot(a.astype(_MXU), b.astype(_MXU), preferred_element_type=_F32)


def _dot_nt(a, b):
    return lax.dot_general(a.astype(_MXU), b.astype(_MXU), (((1,), (1,)), ((), ())),
                           preferred_element_type=_F32)


def _dot_split_rhs(a, b):
    hi = b.astype(_MXU)
    lo = (b - hi.astype(_F32)).astype(_MXU)
    return (jnp.dot(a, hi, preferred_element_type=_F32) + jnp.dot(a, lo, preferred_element_type=_F32))


def _rms(x, g):
    return x * lax.rsqrt(jnp.mean(x * x, axis=-1, keepdims=True) + EPS) * g


def _params(*sem):
    return pltpu.CompilerParams(dimension_semantics=sem, vmem_limit_bytes=VMEM_LIMIT)


def _layer_spec(a, layer):
    return pl.BlockSpec((None,) + a.shape[1:], lambda *_: (layer,) + (0,) * (a.ndim - 1))


def _mixer_output(mla_ref, cmp_ref, sel_ref, win_ref, sb_ref, w_ref):
    def w_rows(first_head, n_heads):
        return w_ref[first_head * HEAD_DIM:(first_head + n_heads) * HEAD_DIM, :]

    nsa = cmp_ref[...].astype(_F32) + sel_ref[...].astype(_F32) + win_ref[...].astype(_F32)
    return (jnp.dot(mla_ref[...], w_rows(0, MLA_HEADS), preferred_element_type=_F32)
            + _dot(nsa, w_rows(MLA_HEADS, NSA_HEADS))
            + jnp.dot(sb_ref[...], w_rows(MLA_HEADS + NSA_HEADS, SB_HEADS), preferred_element_type=_F32))


def _ffn_kernel(x_ref, g_ref, wg_ref, wu_ref, wd_ref, fg_ref, *rest, final_norm, with_mixer):
    *mixer_refs, o_ref, act_ref = rest
    x = x_ref[...]
    if with_mixer:
        x = x + _mixer_output(*mixer_refs)
    h = _rms(x, g_ref[...]).astype(_MXU)
    tf = act_ref.shape[1]
    for c0 in range(0, tf, FFN_CHUNK):
        c1 = min(c0 + FFN_CHUNK, tf)
        gate = jnp.dot(h, wg_ref[:, c0:c1], preferred_element_type=_F32)
        up = jnp.dot(h, wu_ref[:, c0:c1], preferred_element_type=_F32)
        act_ref[:, c0:c1] = (gate * jax.nn.sigmoid(gate) * up).astype(act_ref.dtype)
    y = x + 0.5 * jnp.dot(act_ref[...], wd_ref[...], preferred_element_type=_F32)
    if final_norm:
        y = _rms(y, fg_ref[...])
    o_ref[...] = y


def _ffn(x2d, g, wg, wu, wd, layer, fg, final_norm, mixer=None):
    rows = x2d.shape[0]
    tm = min(1024, rows)

    def resident(a):
        return pl.BlockSpec((None,) + a.shape[1:], lambda i: (layer, 0, 0), pipeline_mode=pl.Buffered(1))

    def row_tile(a):
        return pl.BlockSpec((tm, a.shape[1]), lambda i: (i, 0))

    mixer_args, mixer_specs = (), []
    if mixer is not None:
        *heads, w_out = mixer
        mixer_args = (*heads, w_out)
        mixer_specs = [row_tile(a) for a in heads] + [resident(w_out)]
    return pl.pallas_call(
        functools.partial(_ffn_kernel, final_norm=final_norm, with_mixer=mixer is not None),
        grid=(rows // tm,),
        in_specs=[row_tile(x2d), _layer_spec(g, layer), resident(wg), resident(wu), resident(wd),
                  pl.BlockSpec((1, D_MODEL), lambda i: (0, 0))] + mixer_specs,
        out_specs=row_tile(x2d),
        out_shape=jax.ShapeDtypeStruct((rows, D_MODEL), _F32),
        scratch_shapes=[pltpu.VMEM((tm, D_FF), _MXU)],
        compiler_params=_params("parallel"),
        name="ffn",
    )(x2d, g, wg, wu, wd, fg, *mixer_args)


def _proj_kernel(x_ref, g_ref, w_ref, wt_ref, qn_ref, wuq_ref, wuqs_ref, kvn_ref, wuk_ref, wuv_ref, gb_ref,
                 cq_ref, sq_ref, ck_ref, sk_ref, c64_ref, s64_ref, oh_ref,
                 mq_ref, mk_ref, mv_ref, nq_ref, nkc_ref, nvc_ref, nks_ref, nvs_ref, nkw_ref, nvw_ref,
                 gate_ref, sbq_ref, sbk_ref, sbv_ref, stage_ref):
    hn = _rms(x_ref[0], g_ref[...]).astype(_MXU)

    def proj(h0, h1):
        return jnp.dot(hn, w_ref[:, h0 * HEAD_DIM:h1 * HEAD_DIM], preferred_element_type=_F32)

    def slot(p, s):
        return p[:, s * LANES:(s + 1) * LANES]

    def head(p, i):
        return p[:, i * HEAD_DIM:(i + 1) * HEAD_DIM]

    def write_value_chunks(o_ref, h, vt):
        width = o_ref.shape[-1]
        for c in range(o_ref.shape[2]):
            o_ref[0, h, c] = vt[:, c * width:(c + 1) * width].astype(o_ref.dtype)

    p = proj(0, _H_NQ)
    cq = _rms(p[:, :MLA_Q_LORA], qn_ref[...])
    ckv = _rms(slot(p, _S_CKV), kvn_ref[...])
    q = _dot(cq, wuq_ref[...])
    q_partner = _dot(cq, wuqs_ref[...])
    kpe = slot(p, _S_KR) * ck_ref[...] + slot(p, _S_KRS) * sk_ref[...]
    kn = _dot(ckv, wuk_ref[...])
    for h in range(MLA_HEADS):
        mq_ref[0, h] = (slot(q, h) * cq_ref[...] + slot(q_partner, h) * sq_ref[...]).astype(mq_ref.dtype)
        mk_ref[0, h] = (slot(kn, h) + kpe).astype(mk_ref.dtype)
        write_value_chunks(mv_ref, h, _ones_row_pad(_dot_nt(wuv_ref[h], ckv)))

    c64 = c64_ref[...]
    s64 = s64_ref[...]
    scale = HEAD_DIM ** -0.5

    p = proj(_H_NQ, _H_KC)
    for h in range(NSA_HEADS):
        nq_ref[0, h] = ((head(p, h) * c64 + head(p, NSA_HEADS + h) * s64) * (scale * LOG2_E)).astype(nq_ref.dtype)

    p = proj(_H_KC, _H_SBQ)
    base = _H_KC
    ns = oh_ref.shape[-1]

    def roped(hk, hks, g):
        return head(p, hk - base + g) * c64 + head(p, hks - base + g) * s64

    def write_chunked(o_ref, g, val):
        stage_ref[...] = val
        for t in range(CMP_STRIDE):
            piece = stage_ref[pl.ds(t, val.shape[0] // CMP_STRIDE, stride=CMP_STRIDE), :]
            o_ref[0, g, :, t * HEAD_DIM:(t + 1) * HEAD_DIM] = piece.astype(o_ref.dtype)

    for g in range(NSA_KV_HEADS):
        write_chunked(nkc_ref, g, roped(_H_KC, _H_KCS, g))
        write_chunked(nvc_ref, g, head(p, _H_VC - base + g))
        nks_ref[0, g, :, 0:ns] = oh_ref[...]
        nks_ref[0, g, :, ns:ns + HEAD_DIM] = roped(_H_KS, _H_KSS, g).astype(nks_ref.dtype)
        nkw_ref[0, g] = roped(_H_KW, _H_KWS, g).astype(nkw_ref.dtype)

    pt = _dot_nt(wt_ref[...], hn)
    for g in range(NSA_KV_HEADS):
        lo = g * HEAD_DIM
        write_value_chunks(nvs_ref, g, _ones_row_pad(pt[_T_VS + lo:_T_VS + lo + HEAD_DIM]))
        write_value_chunks(nvw_ref, g, _ones_row_pad(pt[_T_VW + lo:_T_VW + lo + HEAD_DIM]))
    gate_ref[0] = jax.nn.sigmoid(pt[_T_GATE:_T_GATE + GATE_ROWS] + gb_ref[...])

    p = proj(_H_SBQ, _N_HEAD_COLS)
    for h in range(SB_HEADS):
        sbq_ref[0, h] = (head(p, h) * (scale * LOG2_E)).astype(sbq_ref.dtype)
        sbk_ref[0, h] = head(p, SB_HEADS + h).astype(sbk_ref.dtype)
        write_value_chunks(sbv_ref, h, pt[_T_SBV + h * HEAD_DIM:_T_SBV + (h + 1) * HEAD_DIM])


def _proj(x, layer, g, w_ext, w_t, qn, wuq, wuqs, kvn, wuk, wuv, gb, tabs):
    b, s, _ = x.shape
    ts = min(PROJ_ROWS, s)
    tv = min(TOKEN_CHUNK, ts)
    cq, sq, ck, sk, c64, s64, onehot = tabs
    full = functools.partial(_layer_spec, layer=layer)

    def tab(a):
        return pl.BlockSpec((ts, a.shape[1]), lambda bi, i: (i, 0))

    def heads(n, d):
        return (pl.BlockSpec((1, n, ts, d), lambda bi, i: (bi, 0, i, 0)),
                jax.ShapeDtypeStruct((b, n, s, d), _MXU))

    def values_t(n):
        return (pl.BlockSpec((1, n, ts // tv, HEAD_DIM + ONES_PAD, tv), lambda bi, i: (bi, 0, i, 0, 0)),
                jax.ShapeDtypeStruct((b, n, s // tv, HEAD_DIM + ONES_PAD, tv), _MXU))

    def chunked():
        return (pl.BlockSpec((1, NSA_KV_HEADS, ts // CMP_STRIDE, CMP_STRIDE * HEAD_DIM), lambda bi, i: (bi, 0, i, 0)),
                jax.ShapeDtypeStruct((b, NSA_KV_HEADS, s // CMP_STRIDE, CMP_STRIDE * HEAD_DIM), _MXU))

    outs = [heads(MLA_HEADS, LANES), heads(MLA_HEADS, LANES), values_t(MLA_HEADS), heads(NSA_HEADS, HEAD_DIM),
            chunked(), chunked(), heads(NSA_KV_HEADS, onehot.shape[1] + HEAD_DIM), values_t(NSA_KV_HEADS),
            heads(NSA_KV_HEADS, HEAD_DIM), values_t(NSA_KV_HEADS),
            (pl.BlockSpec((1, GATE_ROWS, ts), lambda bi, i: (bi, 0, i)), jax.ShapeDtypeStruct((b, GATE_ROWS, s), _F32)),
            heads(SB_HEADS, HEAD_DIM), heads(SB_HEADS, HEAD_DIM),
            (pl.BlockSpec((1, SB_HEADS, ts // SB_KEY_BLOCK, HEAD_DIM, SB_KEY_BLOCK), lambda bi, i: (bi, 0, i, 0, 0)),
             jax.ShapeDtypeStruct((b, SB_HEADS, s // SB_KEY_BLOCK, HEAD_DIM, SB_KEY_BLOCK), _MXU))]
    return pl.pallas_call(
        _proj_kernel,
        grid=(b, s // ts),
        in_specs=[pl.BlockSpec((1, ts, D_MODEL), lambda bi, i: (bi, i, 0)), full(g), full(w_ext), full(w_t), full(qn),
                  full(wuq), full(wuqs), full(kvn), full(wuk), full(wuv), full(gb),
                  tab(cq), tab(sq), tab(ck), tab(sk), tab(c64), tab(s64), tab(onehot)],
        out_specs=[o[0] for o in outs],
        out_shape=[o[1] for o in outs],
        scratch_shapes=[pltpu.VMEM((ts, HEAD_DIM), _F32)],
        compiler_params=_params("parallel", "parallel"),
        name="proj",
    )(x, g, w_ext, w_t, qn, wuq, wuqs, kvn, wuk, wuv, gb, cq, sq, ck, sk, c64, s64, onehot)


def _ones_row_pad(vt):
    first = lax.broadcasted_iota(jnp.int32, (ONES_PAD, vt.shape[1]), 0) == 0
    return jnp.concatenate([vt, jnp.where(first, 1.0, 0.0).astype(vt.dtype)], axis=0)


def _softmax_step_t(carry, st, vt_chunks):
    m, acc = carry
    m_new = jnp.maximum(m, jnp.max(st, axis=0, keepdims=True))
    alpha = jnp.exp2(m - m_new)
    pt = jnp.exp2(st - m_new).astype(_MXU)
    n = st.shape[0] // len(vt_chunks)
    pv = sum(jnp.dot(vt, pt[c * n:(c + 1) * n], preferred_element_type=_F32) for c, vt in enumerate(vt_chunks))
    return m_new, alpha * acc + pv


def _softmax_init_t(d, cols):
    return (jnp.full((1, cols), M_FLOOR, _F32), jnp.zeros((d + ONES_PAD, cols), _F32))


def _softmax_finish_t(carry, d):
    _, acc = carry
    return acc[:d] * (1.0 / acc[d:d + 1])


def _two_chain_sweep(n_full, qk, soft, init):
    def body(j, carry, diag=False):
        c0, c1 = carry
        qk(0, j)
        c1 = soft(1, j, c1, diag)
        qk(1, jnp.zeros_like(j) if diag else j + 1)
        c0 = soft(0, j, c0, diag)
        return c0, c1

    qk(1, n_full)
    carry = body(n_full, init, True)
    done = jnp.zeros_like(n_full)
    unroll = SWEEP_UNROLL
    while unroll >= 1:
        def unrolled(i, carry, first=done, unroll=unroll):
            for u in range(unroll):
                carry = body(first + unroll * i + u, carry)
            return carry

        trips = (n_full - done) // unroll
        carry = lax.fori_loop(0, trips, unrolled, carry)
        done = done + trips * unroll
        unroll //= 2
    return carry


def _mla_kernel(q_ref, k_ref, vt_ref, o_ref, s0_ref, s1_ref, *, t, nsub):
    qi = pl.program_id(2)
    s_refs = (s0_ref, s1_ref)

    def qk(hh, j):
        off = pl.multiple_of(j * t, t)
        s_refs[hh][...] = _dot_nt(k_ref[0, hh, pl.ds(off, t), :], q_ref[0, hh])

    def soft(hh, j, carry, diag):
        st = s_refs[hh][...]
        if diag:
            key = lax.broadcasted_iota(jnp.int32, (t, t), 0)
            qry = lax.broadcasted_iota(jnp.int32, (t, t), 1)
            st = jnp.where(key <= qry, st, NEG_INF)
        return _softmax_step_t(carry, st, [vt_ref[0, hh, j * nsub + c] for c in range(nsub)])

    carry = _two_chain_sweep(qi, qk, soft, tuple(_softmax_init_t(MLA_V, t) for _ in range(2)))
    ot = jnp.concatenate([_softmax_finish_t(c, MLA_V) for c in carry], axis=0)
    o_ref[0] = ot.T.astype(o_ref.dtype)


def _mla_attention(q, k, vt):
    b, h, s, _ = q.shape
    tv = vt.shape[-1]
    dv = vt.shape[-2]
    t = min(512, s)
    assert h % 2 == 0 and 2 * MLA_V == LANES and t % tv == 0 and s % t == 0
    return pl.pallas_call(
        functools.partial(_mla_kernel, t=t, nsub=t // tv),
        grid=(b, h // 2, s // t),
        in_specs=[pl.BlockSpec((1, 2, t, LANES), lambda bi, hi, i: (bi, hi, i, 0)),
                  pl.BlockSpec((1, 2, s, LANES), lambda bi, hi, i: (bi, hi, 0, 0)),
                  pl.BlockSpec((1, 2, s // tv, dv, tv), lambda bi, hi, i: (bi, hi, 0, 0, 0))],
        out_specs=pl.BlockSpec((1, t, LANES), lambda bi, hi, i: (bi, i, hi)),
        out_shape=jax.ShapeDtypeStruct((b, s, h * MLA_V), _MXU),
        scratch_shapes=[pltpu.VMEM((t, t), _F32), pltpu.VMEM((t, t), _F32)],
        compiler_params=_params("parallel", "parallel", "arbitrary"),
        name="mla_attn",
    )(q, k, vt)


def _sb_kernel(q_ref, k_ref, vt_ref, u_ref, o_ref, *scratch_refs, tq, tk):
    qi = pl.program_id(1)
    u = u_ref[...]
    n_heads = q_ref.shape[1]
    scratch = [scratch_refs[6 * hh:6 * (hh + 1)] for hh in range(n_heads)]
    per_tile = tq // tk

    def step(j, carry, key_offset=None):
        diag = key_offset is not None
        off = pl.multiple_of(j * tk, tk)
        if diag:
            key = key_offset + lax.broadcasted_iota(jnp.int32, (tk, tq), 0)
            qry = lax.broadcasted_iota(jnp.int32, (tk, tq), 1)
            strict = key < qry

        def logits(hh):
            z_ref, _, _, _, _, _ = scratch[hh]
            z_ref[...] = _dot_nt(k_ref[0, hh, pl.ds(off, tk), :], q_ref[0, hh])

        def log_terms(hh):
            z_ref, lb_ref, hi_ref, lo_ref, _, _ = scratch[hh]
            z = z_ref[...]
            log_beta = jnp.minimum(z, 0.0) - jnp.log2(1.0 + jnp.exp2(-jnp.abs(z)))
            log_rem = log_beta - z
            if diag:
                log_rem = jnp.where(strict, log_rem, 0.0)
            hi = log_rem.astype(_MXU)
            lb_ref[...] = log_beta
            hi_ref[...] = hi
            lo_ref[...] = (log_rem - hi.astype(_F32)).astype(_MXU)
            return log_rem[0:1, :]

        def suffix_sums(hh):
            _, _, hi_ref, lo_ref, sfx_ref, _ = scratch[hh]
            sfx_ref[...] = (jnp.dot(u, hi_ref[...], preferred_element_type=_F32)
                            + jnp.dot(u, lo_ref[...], preferred_element_type=_F32))

        def weights(hh, first_rem):
            _, lb_ref, _, _, sfx_ref, a_ref = scratch[hh]
            rem = carry[hh][0]
            suffix = sfx_ref[...]
            a = jnp.exp2(lb_ref[...] + suffix + rem)
            if diag:
                a = jnp.where(strict, a, 0.0)
            a_ref[...] = a.astype(_MXU)
            return rem + suffix[0:1, :] + first_rem

        def values(hh):
            a_ref = scratch[hh][5]
            return carry[hh][1] + jnp.dot(vt_ref[0, hh, j], a_ref[...], preferred_element_type=_F32)

        heads = range(n_heads)
        for hh in heads:
            logits(hh)
        first = []
        for hh in heads:
            first.append(log_terms(hh))
            suffix_sums(hh)
        rems = [weights(hh, first[hh]) for hh in heads]
        alive = jnp.max(functools.reduce(jnp.maximum, rems)) > F32_EXP2_ZERO
        return alive, tuple((rems[hh], values(hh)) for hh in heads)

    carry = tuple((jnp.zeros((1, tq), _F32), jnp.zeros((HEAD_DIM, tq), _F32)) for _ in range(n_heads))
    first = qi * per_tile
    for i in reversed(range(per_tile)):
        _, carry = step(first + i, carry, key_offset=i * tk)
    has_past = first > 0
    carry = tuple((jnp.where(has_past, rem, NEG_INF), acc) for rem, acc in carry)
    alive, carry = step(jnp.maximum(first - 1, 0), carry)

    def earlier(c):
        return (c[0] - 1,) + step(c[0], c[2])

    _, _, carry = lax.while_loop(lambda c: jnp.logical_and(c[0] >= 0, c[1]), earlier, (first - 2, alive, carry))
    o_ref[0] = jnp.concatenate([acc for _, acc in carry], axis=0).T.astype(o_ref.dtype)


def _sb_attention(q, k, vt):
    b, h, s, d = q.shape
    tk = vt.shape[-1]
    tq = tk
    assert (h * d) % LANES == 0 and s % tq == 0 and tq % tk == 0
    idx = np.arange(tk)
    u = jnp.asarray(idx[None, :] > idx[:, None], _MXU)
    return pl.pallas_call(
        functools.partial(_sb_kernel, tq=tq, tk=tk),
        grid=(b, s // tq),
        in_specs=[pl.BlockSpec((1, h, tq, d), lambda bi, i: (bi, 0, i, 0)),
                  pl.BlockSpec((1, h, s, d), lambda bi, i: (bi, 0, 0, 0)),
                  pl.BlockSpec((1, h, s // tk, d, tk), lambda bi, i: (bi, 0, 0, 0, 0)),
                  pl.BlockSpec((tk, tk), lambda bi, i: (0, 0))],
        out_specs=pl.BlockSpec((1, tq, h * d), lambda bi, i: (bi, i, 0)),
        out_shape=jax.ShapeDtypeStruct((b, s, h * d), _MXU),
        scratch_shapes=[pltpu.VMEM((tk, tq), dt) for _ in range(h) for dt in (_F32, _F32, _MXU, _MXU, _F32, _MXU)],
        compiler_params=_params("parallel", "arbitrary"),
        name="sb_attn",
    )(q, k, vt, u)


def _compress_kernel(xk_ref, xv_ref, w1k_ref, w2k_ref, pk_ref, w1v_ref, w2v_ref, pv_ref, ok_ref, ov_ref):
    def hidden(x_ref, w1_ref, p_ref):
        x = x_ref[0, 0]
        n = x.shape[0]
        first = jnp.dot(x, w1_ref[0], preferred_element_type=_F32)
        second = jnp.dot(x, w1_ref[1], preferred_element_type=_F32)
        pos = _dot(p_ref[0], w1_ref[0]) + _dot(p_ref[1], w1_ref[1])
        hid = first + pltpu.roll(second, n - 1, 0) + pos[0:1]
        return 0.5 * hid * (1.0 + jnp.tanh(math.sqrt(2.0 / math.pi) * (hid + 0.044715 * hid * hid * hid)))

    ok_ref[0, 0] = _dot(hidden(xk_ref, w1k_ref, pk_ref), w2k_ref[...]).astype(ok_ref.dtype)
    ov_ref[0, 0] = _dot_nt(w2v_ref[...], hidden(xv_ref, w1v_ref, pv_ref)).astype(ov_ref.dtype)


def _compress(xk, xv, layer, w1k, w2k, pk, w1v, w2v, pv):
    b, g, n, _ = xk.shape
    d = HEAD_DIM
    full = functools.partial(_layer_spec, layer=layer)

    xspec = pl.BlockSpec((1, 1, n, CMP_STRIDE * d), lambda bi, gi: (bi, gi, 0, 0))
    return pl.pallas_call(
        _compress_kernel,
        grid=(b, g),
        in_specs=[xspec, xspec, full(w1k), full(w2k), full(pk), full(w1v), full(w2v), full(pv)],
        out_specs=[pl.BlockSpec((1, 1, n, d), lambda bi, gi: (bi, gi, 0, 0)),
                   pl.BlockSpec((1, 1, d, n), lambda bi, gi: (bi, gi, 0, 0))],
        out_shape=[jax.ShapeDtypeStruct((b, g, n, d), _MXU), jax.ShapeDtypeStruct((b, g, d, n), _MXU)],
        compiler_params=_params("parallel", "parallel"),
        name="nsa_compress",
    )(xk, xv, w1k, w2k, pk, w1v, w2v, pv)


def _group_queries(q_ref, g, tq):
    return q_ref[0, g * NSA_GROUP:(g + 1) * NSA_GROUP].reshape(NSA_GROUP * tq, q_ref.shape[-1])


def _gated_heads(ot, gt_ref, g, branch, tq):
    out = []
    for r in range(NSA_GROUP):
        row = NSA_BRANCHES * (g * NSA_GROUP + r) + branch
        out.append(ot[:, r * tq:(r + 1) * tq] * gt_ref[0, row:row + 1, :])
    return out


def _cmp_kernel(q_ref, kc_ref, vct_ref, ov_ref, gt_ref, o_ref, qa_ref, s0_ref, s1_ref, *, tq, n_top):
    q0 = pl.program_id(1) * tq
    ncp = kc_ref.shape[2]
    ns = ov_ref.shape[0]
    lanes = NSA_GROUP * tq
    s_refs = (s0_ref, s1_ref)
    for g in range(NSA_KV_HEADS):
        s_refs[g][...] = _dot_nt(kc_ref[0, g], _group_queries(q_ref, g, tq))
    qpos = q0 + (lax.broadcasted_iota(jnp.int32, (1, lanes), 1) & (tq - 1))
    cmp_end = lax.broadcasted_iota(jnp.int32, (ncp, 1), 0) * CMP_STRIDE + (CMP_LEN - 1)
    visible = cmp_end <= qpos
    cur = jnp.right_shift(q0 + lax.broadcasted_iota(jnp.int32, (1, tq), 1), int(math.log2(SEL_LEN)))
    blk = lax.broadcasted_iota(jnp.int32, (ns, 1), 0)
    forced = (blk == 0) | (blk == cur) | (blk == cur - 1)
    future = blk > cur
    blk_f = blk.astype(_F32)
    heads = []
    scores = []
    for g in range(NSA_KV_HEADS):
        st = jnp.where(visible, s_refs[g][...], NEG_INF)
        e = jnp.exp2(st - jnp.max(st, axis=0, keepdims=True))
        inv = jnp.where(qpos >= CMP_LEN - 1, 1.0 / jnp.sum(e, axis=0, keepdims=True), 0.0)
        pt = e * inv
        heads += _gated_heads(_dot(vct_ref[0, g], pt), gt_ref, g, 0, tq)
        p_sum = sum(pt[:, r * tq:(r + 1) * tq] for r in range(NSA_GROUP))
        score = _dot_split_rhs(ov_ref[...], p_sum)
        scores.append(jnp.where(forced, PICKED, jnp.where(future, -1.0, score)))
    o_ref[0] = jnp.concatenate(heads, axis=0).T.astype(o_ref.dtype)
    def select(rows):
        sc = [s[:rows] for s in scores]
        idx = blk_f[:rows]
        for _ in range(n_top - N_FORCED):
            for g in range(NSA_KV_HEADS):
                first = jnp.argmax(sc[g], axis=0, keepdims=True).astype(_F32)
                sc[g] = jnp.where(idx == first, PICKED, sc[g])
        for g in range(NSA_KV_HEADS):
            sel_m1 = jnp.where(sc[g] < 0.5 * PICKED, 0.0, -1.0)
            if rows < ns:
                sel_m1 = jnp.concatenate([sel_m1, jnp.full((ns - rows, tq), -1.0, _F32)], axis=0)
            sel_m1 = sel_m1.T.astype(qa_ref.dtype)
            for h in range(g * NSA_GROUP, (g + 1) * NSA_GROUP):
                qa_ref[0, h, :, 0:ns] = sel_m1
                qa_ref[0, h, :, ns:ns + HEAD_DIM] = q_ref[0, h]

    visible_blocks = (q0 + tq) // SEL_LEN
    lower = 0
    for rows in sorted({min(ns, r) for r in TOPK_ROW_STEPS} | {ns}):
        in_range = visible_blocks > lower
        if rows < ns:
            in_range = jnp.logical_and(in_range, visible_blocks <= rows)
        pl.when(in_range)(functools.partial(select, rows))
        lower = rows


def _cmp_select(q, kc, vct, gates_t):
    b, h, s, d = q.shape
    g = kc.shape[1]
    ncp = kc.shape[2]
    ns = s // SEL_LEN
    n_top = min(SEL_TOPK, ns)
    tq = min(256, s)
    assert tq & (tq - 1) == 0 and g == 2 and n_top >= N_FORCED
    c0 = np.arange(ncp)[:, None] * CMP_STRIDE
    n0 = np.arange(ns)[None, :] * SEL_LEN
    overlap = jnp.asarray(((c0 < n0 + SEL_LEN) & (c0 + CMP_LEN > n0)).T, _MXU)
    return pl.pallas_call(
        functools.partial(_cmp_kernel, tq=tq, n_top=n_top),
        grid=(b, s // tq),
        in_specs=[pl.BlockSpec((1, h, tq, d), lambda bi, i: (bi, 0, i, 0)),
                  pl.BlockSpec((1, g, ncp, d), lambda bi, i: (bi, 0, 0, 0)),
                  pl.BlockSpec((1, g, d, ncp), lambda bi, i: (bi, 0, 0, 0)),
                  pl.BlockSpec((ns, ncp), lambda bi, i: (0, 0)),
                  pl.BlockSpec((1, GATE_ROWS, tq), lambda bi, i: (bi, 0, i))],
        out_specs=[pl.BlockSpec((1, tq, h * d), lambda bi, i: (bi, i, 0)),
                   pl.BlockSpec((1, h, tq, ns + d), lambda bi, i: (bi, 0, i, 0))],
        out_shape=[jax.ShapeDtypeStruct((b, s, h * d), _MXU), jax.ShapeDtypeStruct((b, h, s, ns + d), _MXU)],
        scratch_shapes=[pltpu.VMEM((ncp, NSA_GROUP * tq), _F32) for _ in range(g)],
        compiler_params=_params("parallel", "arbitrary"),
        name="nsa_cmp_select",
    )(q, kc, vct, overlap, gates_t)


def _key_minus_query(keys, tq):
    return jnp.asarray(np.arange(keys)[:, None] - np.arange(NSA_GROUP * tq)[None, :] % tq, jnp.int32)


def _sel_kernel(q_ref, k_ref, vt_ref, gt_ref, rel_ref, o_ref, s0_ref, s1_ref, *, tq, tk, nsub):
    q0 = pl.program_id(1) * tq
    last = (q0 + tq - 1) // tk
    lanes = NSA_GROUP * tq
    s_refs = (s0_ref, s1_ref)

    def qk(g, j):
        off = pl.multiple_of(j * tk, tk)
        s_refs[g][...] = _dot_nt(k_ref[0, g, pl.ds(off, tk), :], _group_queries(q_ref, g, tq))

    def soft(g, j, carry, causal):
        st = s_refs[g][...]
        if causal:
            st = jnp.where(rel_ref[...] <= q0 - j * tk, st, NEG_INF)
        return _softmax_step_t(carry, st, [vt_ref[0, g, j * nsub + c] for c in range(nsub)])

    init = tuple(_softmax_init_t(HEAD_DIM, lanes) for _ in range(NSA_KV_HEADS))
    carry = _two_chain_sweep(last, qk, soft, init)
    heads = []
    for g in range(NSA_KV_HEADS):
        heads += _gated_heads(_softmax_finish_t(carry[g], HEAD_DIM), gt_ref, g, 1, tq)
    o_ref[0] = jnp.concatenate(heads, axis=0).T.astype(o_ref.dtype)


def _sel_attention(q, k, vt, gates_t):
    b, h, s, da = q.shape
    g = k.shape[1]
    d = HEAD_DIM
    tv = vt.shape[-1]
    tq = min(256, s)
    tk = min(512, s)
    assert tq & (tq - 1) == 0 and s % tk == 0 and tk % tv == 0 and g == 2
    return pl.pallas_call(
        functools.partial(_sel_kernel, tq=tq, tk=tk, nsub=tk // tv),
        grid=(b, s // tq),
        in_specs=[pl.BlockSpec((1, h, tq, da), lambda bi, i: (bi, 0, i, 0)),
                  pl.BlockSpec((1, g, s, da), lambda bi, i: (bi, 0, 0, 0)),
                  pl.BlockSpec((1, g) + vt.shape[2:], lambda bi, i: (bi, 0, 0, 0, 0)),
                  pl.BlockSpec((1, GATE_ROWS, tq), lambda bi, i: (bi, 0, i)),
                  pl.BlockSpec((tk, NSA_GROUP * tq), lambda bi, i: (0, 0))],
        out_specs=pl.BlockSpec((1, tq, h * d), lambda bi, i: (bi, i, 0)),
        out_shape=jax.ShapeDtypeStruct((b, s, h * d), _MXU),
        scratch_shapes=[pltpu.VMEM((tk, NSA_GROUP * tq), _F32) for _ in range(g)],
        compiler_params=_params("parallel", "arbitrary"),
        name="nsa_selected",
    )(q, k, vt, gates_t, _key_minus_query(tk, tq))


def _win_kernel(q_ref, k_ref, vt_ref, gt_ref, rel_ref, o_ref, *s_refs, tq, subs, span, tv):
    lanes = NSA_GROUP * tq
    rel = rel_ref[...]
    chains = [(sub, g) for sub in range(subs) for g in range(NSA_KV_HEADS)]

    def origin(sub):
        q0 = (pl.program_id(1) * subs + sub) * tq
        return q0, pl.multiple_of(jnp.maximum(q0 - WINDOW, 0), tq)

    for c, (sub, g) in enumerate(chains):
        _, start = origin(sub)
        q = q_ref[0, g * NSA_GROUP:(g + 1) * NSA_GROUP, sub * tq:(sub + 1) * tq].reshape(lanes, q_ref.shape[-1])
        s_refs[c][...] = _dot_nt(k_ref[0, g, pl.ds(start, span), :], q)
    for sub in range(subs):
        q0, start = origin(sub)
        offset = q0 - start
        heads = []
        for g in range(NSA_KV_HEADS):
            st = jnp.where(rel <= offset, s_refs[chains.index((sub, g))][...], NEG_INF)
            st = jnp.where(rel > offset - WINDOW, st, NEG_INF)
            carry = _softmax_step_t(_softmax_init_t(HEAD_DIM, lanes), st,
                                    [vt_ref[0, g, start // tv + c] for c in range(span // tv)])
            ot = _softmax_finish_t(carry, HEAD_DIM)
            for r in range(NSA_GROUP):
                row = NSA_BRANCHES * (g * NSA_GROUP + r) + 2
                heads.append(ot[:, r * tq:(r + 1) * tq] * gt_ref[0, row:row + 1, sub * tq:(sub + 1) * tq])
        o_ref[0, sub * tq:(sub + 1) * tq, :] = jnp.concatenate(heads, axis=0).T.astype(o_ref.dtype)


def _win_attention(q, k, vt, gates_t):
    b, h, s, d = q.shape
    g = k.shape[1]
    tv = vt.shape[-1]
    tq = min(256, s)
    subs = 2 if s % (2 * tq) == 0 else 1
    span = WINDOW + tq
    assert tq & (tq - 1) == 0 and s >= span and tq % tv == 0 and WINDOW % tv == 0 and g == NSA_KV_HEADS
    return pl.pallas_call(
        functools.partial(_win_kernel, tq=tq, subs=subs, span=span, tv=tv),
        grid=(b, s // (subs * tq)),
        in_specs=[pl.BlockSpec((1, h, subs * tq, d), lambda bi, i: (bi, 0, i, 0)),
                  pl.BlockSpec((1, g, s, d), lambda bi, i: (bi, 0, 0, 0)),
                  pl.BlockSpec((1, g) + vt.shape[2:], lambda bi, i: (bi, 0, 0, 0, 0)),
                  pl.BlockSpec((1, GATE_ROWS, subs * tq), lambda bi, i: (bi, 0, i)),
                  pl.BlockSpec((span, NSA_GROUP * tq), lambda bi, i: (0, 0))],
        out_specs=pl.BlockSpec((1, subs * tq, h * d), lambda bi, i: (bi, i, 0)),
        out_shape=jax.ShapeDtypeStruct((b, s, h * d), _MXU),
        scratch_shapes=[pltpu.VMEM((span, NSA_GROUP * tq), _F32) for _ in range(subs * g)],
        compiler_params=_params("parallel", "arbitrary"),
        name="nsa_window",
    )(q, k, vt, gates_t, _key_minus_query(span, tq))


def _gather_cols(w, idx):
    idx = np.asarray(idx)
    cuts = [0] + [i for i in range(1, len(idx)) if idx[i] != idx[i - 1] + (idx[i - 1] >= 0)] + [len(idx)]
    pieces = []
    for a, b in zip(cuts[:-1], cuts[1:]):
        if idx[a] < 0:
            pieces.append(jnp.zeros(w.shape[:-1] + (b - a,), _MXU))
        else:
            pieces.append(w[..., int(idx[a]):int(idx[a]) + b - a].astype(_MXU))
    return jnp.concatenate(pieces, axis=-1)


def _swap_halves(rot):
    return (np.arange(rot) + rot // 2) % rot


def _w_in_index():
    idx = np.full((_N_HEAD_COLS * HEAD_DIM,), -1, np.int64)

    def put(col, src):
        src = np.asarray(src)
        idx[col:col + len(src)] = src

    def put_head(pos, src):
        put(pos * HEAD_DIM, src)

    put(_S_CQ * LANES, _O_CQ + np.arange(MLA_Q_LORA))
    put(_S_CKV * LANES, _O_CKV + np.arange(MLA_KV_LORA))
    put(_S_KR * LANES + MLA_NOPE, _O_KR + np.arange(MLA_ROPE))
    put(_S_KRS * LANES + MLA_NOPE, _O_KR + _swap_halves(MLA_ROPE))
    for h in range(NSA_HEADS):
        put_head(_H_NQ + h, _O_NQ + h * HEAD_DIM + np.arange(HEAD_DIM))
        put_head(_H_NQS + h, _O_NQ + h * HEAD_DIM + _swap_halves(PARTIAL_ROT))
    for hk, hks, ok in ((_H_KC, _H_KCS, _O_NKC), (_H_KS, _H_KSS, _O_NKS), (_H_KW, _H_KWS, _O_NKW)):
        for g in range(NSA_KV_HEADS):
            put_head(hk + g, ok + g * HEAD_DIM + np.arange(HEAD_DIM))
            put_head(hks + g, ok + g * HEAD_DIM + _swap_halves(PARTIAL_ROT))
    for g in range(NSA_KV_HEADS):
        put_head(_H_VC + g, _O_NVC + g * HEAD_DIM + np.arange(HEAD_DIM))
    for h in range(SB_HEADS):
        put_head(_H_SBQ + h, _O_SBQ + h * HEAD_DIM + np.arange(HEAD_DIM))
        put_head(_H_SBK + h, _O_SBK + h * HEAD_DIM + np.arange(HEAD_DIM))
    return idx


def _mla_up_index():
    qd = MLA_NOPE + MLA_ROPE
    kd = MLA_NOPE + MLA_V
    uq = np.full((MLA_HEADS * LANES,), -1, np.int64)
    uqs = uq.copy()
    uk = uq.copy()
    for h in range(MLA_HEADS):
        uq[h * LANES:h * LANES + qd] = h * qd + np.arange(qd)
        uqs[h * LANES + MLA_NOPE:h * LANES + qd] = h * qd + MLA_NOPE + _swap_halves(MLA_ROPE)
        uk[h * LANES:h * LANES + MLA_NOPE] = h * kd + np.arange(MLA_NOPE)
    return uq, uqs, uk


def _transposed_weights(w_in, gate_bias):
    width = NSA_KV_HEADS * HEAD_DIM
    gate_rows = jnp.pad(w_in[..., _O_GATE:_O_GATE + N_GATES], ((0, 0), (0, 0), (0, _T_SBV - _T_GATE - N_GATES)))
    rows = jnp.concatenate([w_in[..., _O_NVS:_O_NVS + width], w_in[..., _O_NVW:_O_NVW + width], gate_rows,
                            w_in[..., _O_SBV:_O_SBV + SB_HEADS * HEAD_DIM]], axis=-1)
    bias = jnp.pad(gate_bias, ((0, 0), (0, GATE_ROWS - N_GATES)))[..., None]
    return jnp.swapaxes(rows, -1, -2).astype(_MXU), bias


def _rope_tables(s):
    pos = np.arange(s, dtype=np.float64)

    def cs(rot):
        half = rot // 2
        ang = pos[:, None] * (ROPE_THETA ** (-np.arange(half, dtype=np.float64) / half))[None, :]
        c, sn = np.cos(ang), np.sin(ang)
        return np.concatenate([c, c], axis=1), np.concatenate([-sn, sn], axis=1)

    c, sn = cs(MLA_ROPE)
    pad = np.zeros((s, LANES - MLA_NOPE - MLA_ROPE))
    ck = np.concatenate([np.ones((s, MLA_NOPE)), c, pad], axis=1)
    sk = np.concatenate([np.zeros((s, MLA_NOPE)), sn, pad], axis=1)
    q_scale = (MLA_NOPE + MLA_ROPE) ** -0.5 * LOG2_E
    c, sn = cs(PARTIAL_ROT)
    c64 = np.concatenate([c, np.ones((s, HEAD_DIM - PARTIAL_ROT))], axis=1)
    s64 = np.concatenate([sn, np.zeros((s, HEAD_DIM - PARTIAL_ROT))], axis=1)
    ns = s // SEL_LEN
    onehot = (np.arange(s)[:, None] // SEL_LEN == np.arange(ns)[None, :]) * -NEG_INF
    tables = [jnp.asarray(t, _F32) for t in (ck * q_scale, sk * q_scale, ck, sk, c64, s64)]
    return tables + [jnp.asarray(onehot, _MXU)]


def kernel(x, ffn1_norm, ffn1_w_gate, ffn1_w_up, ffn1_w_down, mix_norm, w_in, mla_q_norm, mla_w_uq, mla_kv_norm,
           mla_w_ukv, nsa_gate_bias, nsa_cmp_pos_k, nsa_cmp_w1_k, nsa_cmp_w2_k, nsa_cmp_pos_v, nsa_cmp_w1_v,
           nsa_cmp_w2_v, w_out, ffn2_norm, ffn2_w_gate, ffn2_w_up, ffn2_w_down, final_norm):
    b, s, d = x.shape
    depth = w_in.shape[0]
    tabs = _rope_tables(s)
    in_idx = _w_in_index()
    uq_idx, uqs_idx, uk_idx = _mla_up_index()
    half = CMP_LEN * HEAD_DIM // 2
    fg = final_norm.reshape(1, d)

    def row(p):
        return p[:, None, :]

    def cmp_weights(w1, w2, pos, transpose_out):
        pos = jnp.broadcast_to(pos.reshape(depth, 2, 1, half), (depth, 2, 8, half)).astype(_MXU)
        w2 = jnp.swapaxes(w2, -1, -2) if transpose_out else w2
        return w1.reshape(depth, 2, half, CMP_HIDDEN).astype(_MXU), w2.astype(_MXU), pos

    ffn1 = [w.astype(_MXU) for w in (ffn1_w_gate, ffn1_w_up, ffn1_w_down)]
    ffn2 = [w.astype(_MXU) for w in (ffn2_w_gate, ffn2_w_up, ffn2_w_down)]
    w_t, gate_bias = _transposed_weights(w_in, nsa_gate_bias)
    wuv_t = mla_w_ukv.reshape(depth, MLA_KV_LORA, MLA_HEADS, 2, MLA_V)[:, :, :, 1].transpose(0, 2, 3, 1).astype(_MXU)
    proj_params = (row(mix_norm), _gather_cols(w_in, in_idx), w_t,
                   row(mla_q_norm), _gather_cols(mla_w_uq, uq_idx), _gather_cols(mla_w_uq, uqs_idx),
                   row(mla_kv_norm), _gather_cols(mla_w_ukv, uk_idx), wuv_t, gate_bias)
    cmp_params = (cmp_weights(nsa_cmp_w1_k, nsa_cmp_w2_k, nsa_cmp_pos_k, False)
                  + cmp_weights(nsa_cmp_w1_v, nsa_cmp_w2_v, nsa_cmp_pos_v, True))
    w_out = w_out.astype(_MXU)
    ffn1_norm, ffn2_norm = row(ffn1_norm), row(ffn2_norm)

    for l in range(depth):
        x2d = _ffn(x.reshape(b * s, d), ffn1_norm, *ffn1, l, fg, False)
        x = x2d.reshape(b, s, d)
        (mq, mk, mvt, nq, nkc, nvc, nks, nvst, nkw, nvwt, gates_t, sbq, sbk, sbv) = _proj(x, l, *proj_params, tabs)
        o_mla = _mla_attention(mq, mk, mvt)
        kc, vct = _compress(nkc, nvc, l, *cmp_params)
        o_cmp, q_sel = _cmp_select(nq, kc, vct, gates_t)
        o_sel = _sel_attention(q_sel, nks, nvst, gates_t)
        o_win = _win_attention(nq, nkw, nvwt, gates_t)
        o_sb = _sb_attention(sbq, sbk, sbv)
        heads = [o.reshape(b * s, -1) for o in (o_mla, o_cmp, o_sel, o_win, o_sb)]
        x2d = _ffn(x.reshape(b * s, d), ffn2_norm, *ffn2, l, fg, l == depth - 1, mixer=(*heads, w_out))
        x = x2d.reshape(b, s, d)
    return x
```

```python
import functools
import math

import numpy as np
import jax
import jax.numpy as jnp
from jax import lax
from jax.experimental import pallas as pl
from jax.experimental.pallas import tpu as pltpu

D_MODEL = 1024
HEAD_DIM = 64
MLA_HEADS = 6
MLA_NOPE = 64
MLA_ROPE = 32
MLA_V = 64
MLA_Q_LORA = 256
MLA_KV_LORA = 128
NSA_HEADS = 6
NSA_KV_HEADS = 2
NSA_GROUP = NSA_HEADS // NSA_KV_HEADS
NSA_BRANCHES = 3
CMP_LEN = 32
CMP_STRIDE = 16
CMP_HIDDEN = 128
SEL_LEN = 64
SEL_TOPK = 16
WINDOW = 512
SB_HEADS = 4
D_FF = 2816
ROPE_THETA = 500000.0
PARTIAL_ROT = HEAD_DIM // 4
EPS = 1e-6
NEG_INF = -1e30
M_FLOOR = 0.1 * NEG_INF
N_FORCED = 3
PICKED = -3e38
F32_EXP2_ZERO = -151.0
LOG2_E = math.log2(math.e)
N_GATES = NSA_HEADS * NSA_BRANCHES

LANES = 128
FFN_CHUNK = 256
SWEEP_UNROLL = 4
TOPK_ROW_STEPS = (32, 64)
SB_KEY_BLOCK = 256
PROJ_ROWS = 512
TOKEN_CHUNK = 256
ONES_PAD = 16
VMEM_LIMIT = 56 * 1024 * 1024

_MXU = jnp.bfloat16
_F32 = jnp.float32

_IN_WIDTHS = (MLA_Q_LORA, MLA_KV_LORA, MLA_ROPE, NSA_HEADS * HEAD_DIM) + (NSA_KV_HEADS * HEAD_DIM,) * 6 + (
    N_GATES, SB_HEADS * HEAD_DIM, SB_HEADS * HEAD_DIM, SB_HEADS * HEAD_DIM)
_IN_OFF = np.concatenate([[0], np.cumsum(_IN_WIDTHS)])
(_O_CQ, _O_CKV, _O_KR, _O_NQ, _O_NKC, _O_NVC, _O_NKS, _O_NVS, _O_NKW, _O_NVW, _O_GATE, _O_SBQ, _O_SBK,
 _O_SBV) = [int(v) for v in _IN_OFF[:-1]]

_S_CQ, _S_CKV, _S_KR, _S_KRS = 0, 2, 3, 4
_H_NQ, _H_NQS = 10, 16
_H_KC, _H_KCS, _H_VC = 22, 24, 26
_H_KS, _H_KSS = 28, 30
_H_KW, _H_KWS = 32, 34
_H_SBQ, _H_SBK = 36, 40
_N_HEAD_COLS = 44
_T_VS, _T_VW, _T_GATE = 0, NSA_KV_HEADS * HEAD_DIM, 2 * NSA_KV_HEADS * HEAD_DIM
GATE_ROWS = 24
_T_SBV = _T_GATE + 2 * ONES_PAD


def _dot(a, b):
    return jnp.dot(a.astype(_MXU), b.astype(_MXU), preferred_element_type=_F32)


def _dot_nt(a, b):
    return lax.dot_general(a.astype(_MXU), b.astype(_MXU), (((1,), (1,)), ((), ())),
                           preferred_element_type=_F32)


def _dot_split_rhs(a, b):
    hi = b.astype(_MXU)
    lo = (b - hi.astype(_F32)).astype(_MXU)
    return (jnp.dot(a, hi, preferred_element_type=_F32) + jnp.dot(a, lo, preferred_element_type=_F32))


def _rms(x, g):
    return x * lax.rsqrt(jnp.mean(x * x, axis=-1, keepdims=True) + EPS) * g


def _params(*sem):
    return pltpu.CompilerParams(dimension_semantics=sem, vmem_limit_bytes=VMEM_LIMIT)


def _layer_spec(a, layer):
    return pl.BlockSpec((None,) + a.shape[1:], lambda *_: (layer,) + (0,) * (a.ndim - 1))


def _mixer_output(mla_ref, cmp_ref, sel_ref, win_ref, sb_ref, w_ref):
    def w_rows(first_head, n_heads):
        return w_ref[first_head * HEAD_DIM:(first_head + n_heads) * HEAD_DIM, :]

    nsa = cmp_ref[...].astype(_F32) + sel_ref[...].astype(_F32) + win_ref[...].astype(_F32)
    return (jnp.dot(mla_ref[...], w_rows(0, MLA_HEADS), preferred_element_type=_F32)
            + _dot(nsa, w_rows(MLA_HEADS, NSA_HEADS))
            + jnp.dot(sb_ref[...], w_rows(MLA_HEADS + NSA_HEADS, SB_HEADS), preferred_element_type=_F32))


def _ffn_kernel(x_ref, g_ref, wg_ref, wu_ref, wd_ref, fg_ref, *rest, final_norm, with_mixer):
    *mixer_refs, o_ref, act_ref = rest
    x = x_ref[...]
    if with_mixer:
        x = x + _mixer_output(*mixer_refs)
    h = _rms(x, g_ref[...]).astype(_MXU)
    tf = act_ref.shape[1]
    for c0 in range(0, tf, FFN_CHUNK):
        c1 = min(c0 + FFN_CHUNK, tf)
        gate = jnp.dot(h, wg_ref[:, c0:c1], preferred_element_type=_F32)
        up = jnp.dot(h, wu_ref[:, c0:c1], preferred_element_type=_F32)
        act_ref[:, c0:c1] = (gate * jax.nn.sigmoid(gate) * up).astype(act_ref.dtype)
    y = x + 0.5 * jnp.dot(act_ref[...], wd_ref[...], preferred_element_type=_F32)
    if final_norm:
        y = _rms(y, fg_ref[...])
    o_ref[...] = y


def _ffn(x2d, g, wg, wu, wd, layer, fg, final_norm, mixer=None):
    rows = x2d.shape[0]
    tm = min(1024, rows)

    def resident(a):
        return pl.BlockSpec((None,) + a.shape[1:], lambda i: (layer, 0, 0), pipeline_mode=pl.Buffered(1))

    def row_tile(a):
        return pl.BlockSpec((tm, a.shape[1]), lambda i: (i, 0))

    mixer_args, mixer_specs = (), []
    if mixer is not None:
        *heads, w_out = mixer
        mixer_args = (*heads, w_out)
        mixer_specs = [row_tile(a) for a in heads] + [resident(w_out)]
    return pl.pallas_call(
        functools.partial(_ffn_kernel, final_norm=final_norm, with_mixer=mixer is not None),
        grid=(rows // tm,),
        in_specs=[row_tile(x2d), _layer_spec(g, layer), resident(wg), resident(wu), resident(wd),
                  pl.BlockSpec((1, D_MODEL), lambda i: (0, 0))] + mixer_specs,
        out_specs=row_tile(x2d),
        out_shape=jax.ShapeDtypeStruct((rows, D_MODEL), _F32),
        scratch_shapes=[pltpu.VMEM((tm, D_FF), _MXU)],
        compiler_params=_params("parallel"),
        name="ffn",
    )(x2d, g, wg, wu, wd, fg, *mixer_args)


def _proj_kernel(x_ref, g_ref, w_ref, wt_ref, qn_ref, wuq_ref, wuqs_ref, kvn_ref, wuk_ref, wuv_ref, gb_ref,
                 cq_ref, sq_ref, ck_ref, sk_ref, c64_ref, s64_ref, oh_ref,
                 mq_ref, mk_ref, mv_ref, nq_ref, nkc_ref, nvc_ref, nks_ref, nvs_ref, nkw_ref, nvw_ref,
                 gate_ref, sbq_ref, sbk_ref, sbv_ref, stage_ref):
    hn = _rms(x_ref[0], g_ref[...]).astype(_MXU)

    def proj(h0, h1):
        return jnp.dot(hn, w_ref[:, h0 * HEAD_DIM:h1 * HEAD_DIM], preferred_element_type=_F32)

    def slot(p, s):
        return p[:, s * LANES:(s + 1) * LANES]

    def head(p, i):
        return p[:, i * HEAD_DIM:(i + 1) * HEAD_DIM]

    def write_value_chunks(o_ref, h, vt):
        width = o_ref.shape[-1]
        for c in range(o_ref.shape[2]):
            o_ref[0, h, c] = vt[:, c * width:(c + 1) * width].astype(o_ref.dtype)

    p = proj(0, _H_NQ)
    cq = _rms(p[:, :MLA_Q_LORA], qn_ref[...])
    ckv = _rms(slot(p, _S_CKV), kvn_ref[...])
    q = _dot(cq, wuq_ref[...])
    q_partner = _dot(cq, wuqs_ref[...])
    kpe = slot(p, _S_KR) * ck_ref[...] + slot(p, _S_KRS) * sk_ref[...]
    kn = _dot(ckv, wuk_ref[...])
    for h in range(MLA_HEADS):
        mq_ref[0, h] = (slot(q, h) * cq_ref[...] + slot(q_partner, h) * sq_ref[...]).astype(mq_ref.dtype)
        mk_ref[0, h] = (slot(kn, h) + kpe).astype(mk_ref.dtype)
        write_value_chunks(mv_ref, h, _ones_row_pad(_dot_nt(wuv_ref[h], ckv)))

    c64 = c64_ref[...]
    s64 = s64_ref[...]
    scale = HEAD_DIM ** -0.5

    p = proj(_H_NQ, _H_KC)
    for h in range(NSA_HEADS):
        nq_ref[0, h] = ((head(p, h) * c64 + head(p, NSA_HEADS + h) * s64) * (scale * LOG2_E)).astype(nq_ref.dtype)

    p = proj(_H_KC, _H_SBQ)
    base = _H_KC
    ns = oh_ref.shape[-1]

    def roped(hk, hks, g):
        return head(p, hk - base + g) * c64 + head(p, hks - base + g) * s64

    def write_chunked(o_ref, g, val):
        stage_ref[...] = val
        for t in range(CMP_STRIDE):
            piece = stage_ref[pl.ds(t, val.shape[0] // CMP_STRIDE, stride=CMP_STRIDE), :]
            o_ref[0, g, :, t * HEAD_DIM:(t + 1) * HEAD_DIM] = piece.astype(o_ref.dtype)

    for g in range(NSA_KV_HEADS):
        write_chunked(nkc_ref, g, roped(_H_KC, _H_KCS, g))
        write_chunked(nvc_ref, g, head(p, _H_VC - base + g))
        nks_ref[0, g, :, 0:ns] = oh_ref[...]
        nks_ref[0, g, :, ns:ns + HEAD_DIM] = roped(_H_KS, _H_KSS, g).astype(nks_ref.dtype)
        nkw_ref[0, g] = roped(_H_KW, _H_KWS, g).astype(nkw_ref.dtype)

    pt = _dot_nt(wt_ref[...], hn)
    for g in range(NSA_KV_HEADS):
        lo = g * HEAD_DIM
        write_value_chunks(nvs_ref, g, _ones_row_pad(pt[_T_VS + lo:_T_VS + lo + HEAD_DIM]))
        write_value_chunks(nvw_ref, g, _ones_row_pad(pt[_T_VW + lo:_T_VW + lo + HEAD_DIM]))
    gate_ref[0] = jax.nn.sigmoid(pt[_T_GATE:_T_GATE + GATE_ROWS] + gb_ref[...])

    p = proj(_H_SBQ, _N_HEAD_COLS)
    for h in range(SB_HEADS):
        sbq_ref[0, h] = (head(p, h) * (scale * LOG2_E)).astype(sbq_ref.dtype)
        sbk_ref[0, h] = head(p, SB_HEADS + h).astype(sbk_ref.dtype)
        write_value_chunks(sbv_ref, h, pt[_T_SBV + h * HEAD_DIM:_T_SBV + (h + 1) * HEAD_DIM])


def _proj(x, layer, g, w_ext, w_t, qn, wuq, wuqs, kvn, wuk, wuv, gb, tabs):
    b, s, _ = x.shape
    ts = min(PROJ_ROWS, s)
    tv = min(TOKEN_CHUNK, ts)
    cq, sq, ck, sk, c64, s64, onehot = tabs
    full = functools.partial(_layer_spec, layer=layer)

    def tab(a):
        return pl.BlockSpec((ts, a.shape[1]), lambda bi, i: (i, 0))

    def heads(n, d):
        return (pl.BlockSpec((1, n, ts, d), lambda bi, i: (bi, 0, i, 0)),
                jax.ShapeDtypeStruct((b, n, s, d), _MXU))

    def values_t(n):
        return (pl.BlockSpec((1, n, ts // tv, HEAD_DIM + ONES_PAD, tv), lambda bi, i: (bi, 0, i, 0, 0)),
                jax.ShapeDtypeStruct((b, n, s // tv, HEAD_DIM + ONES_PAD, tv), _MXU))

    def chunked():
        return (pl.BlockSpec((1, NSA_KV_HEADS, ts // CMP_STRIDE, CMP_STRIDE * HEAD_DIM), lambda bi, i: (bi, 0, i, 0)),
                jax.ShapeDtypeStruct((b, NSA_KV_HEADS, s // CMP_STRIDE, CMP_STRIDE * HEAD_DIM), _MXU))

    outs = [heads(MLA_HEADS, LANES), heads(MLA_HEADS, LANES), values_t(MLA_HEADS), heads(NSA_HEADS, HEAD_DIM),
            chunked(), chunked(), heads(NSA_KV_HEADS, onehot.shape[1] + HEAD_DIM), values_t(NSA_KV_HEADS),
            heads(NSA_KV_HEADS, HEAD_DIM), values_t(NSA_KV_HEADS),
            (pl.BlockSpec((1, GATE_ROWS, ts), lambda bi, i: (bi, 0, i)), jax.ShapeDtypeStruct((b, GATE_ROWS, s), _F32)),
            heads(SB_HEADS, HEAD_DIM), heads(SB_HEADS, HEAD_DIM),
            (pl.BlockSpec((1, SB_HEADS, ts // SB_KEY_BLOCK, HEAD_DIM, SB_KEY_BLOCK), lambda bi, i: (bi, 0, i, 0, 0)),
             jax.ShapeDtypeStruct((b, SB_HEADS, s // SB_KEY_BLOCK, HEAD_DIM, SB_KEY_BLOCK), _MXU))]
    return pl.pallas_call(
        _proj_kernel,
        grid=(b, s // ts),
        in_specs=[pl.BlockSpec((1, ts, D_MODEL), lambda bi, i: (bi, i, 0)), full(g), full(w_ext), full(w_t), full(qn),
                  full(wuq), full(wuqs), full(kvn), full(wuk), full(wuv), full(gb),
                  tab(cq), tab(sq), tab(ck), tab(sk), tab(c64), tab(s64), tab(onehot)],
        out_specs=[o[0] for o in outs],
        out_shape=[o[1] for o in outs],
        scratch_shapes=[pltpu.VMEM((ts, HEAD_DIM), _F32)],
        compiler_params=_params("parallel", "parallel"),
        name="proj",
    )(x, g, w_ext, w_t, qn, wuq, wuqs, kvn, wuk, wuv, gb, cq, sq, ck, sk, c64, s64, onehot)


def _ones_row_pad(vt):
    first = lax.broadcasted_iota(jnp.int32, (ONES_PAD, vt.shape[1]), 0) == 0
    return jnp.concatenate([vt, jnp.where(first, 1.0, 0.0).astype(vt.dtype)], axis=0)


def _softmax_step_t(carry, st, vt_chunks):
    m, acc = carry
    m_new = jnp.maximum(m, jnp.max(st, axis=0, keepdims=True))
    alpha = jnp.exp2(m - m_new)
    pt = jnp.exp2(st - m_new).astype(_MXU)
    n = st.shape[0] // len(vt_chunks)
    pv = sum(jnp.dot(vt, pt[c * n:(c + 1) * n], preferred_element_type=_F32) for c, vt in enumerate(vt_chunks))
    return m_new, alpha * acc + pv


def _softmax_init_t(d, cols):
    return (jnp.full((1, cols), M_FLOOR, _F32), jnp.zeros((d + ONES_PAD, cols), _F32))


def _softmax_finish_t(carry, d):
    _, acc = carry
    return acc[:d] * (1.0 / acc[d:d + 1])


def _two_chain_sweep(n_full, qk, soft, init):
    def body(j, carry, diag=False):
        c0, c1 = carry
        qk(0, j)
        c1 = soft(1, j, c1, diag)
        qk(1, jnp.zeros_like(j) if diag else j + 1)
        c0 = soft(0, j, c0, diag)
        return c0, c1

    qk(1, n_full)
    carry = body(n_full, init, True)
    done = jnp.zeros_like(n_full)
    unroll = SWEEP_UNROLL
    while unroll >= 1:
        def unrolled(i, carry, first=done, unroll=unroll):
            for u in range(unroll):
                carry = body(first + unroll * i + u, carry)
            return carry

        trips = (n_full - done) // unroll
        carry = lax.fori_loop(0, trips, unrolled, carry)
        done = done + trips * unroll
        unroll //= 2
    return carry


def _mla_kernel(q_ref, k_ref, vt_ref, o_ref, s0_ref, s1_ref, *, t, nsub):
    qi = pl.program_id(2)
    s_refs = (s0_ref, s1_ref)

    def qk(hh, j):
        off = pl.multiple_of(j * t, t)
        s_refs[hh][...] = _dot_nt(k_ref[0, hh, pl.ds(off, t), :], q_ref[0, hh])

    def soft(hh, j, carry, diag):
        st = s_refs[hh][...]
        if diag:
            key = lax.broadcasted_iota(jnp.int32, (t, t), 0)
            qry = lax.broadcasted_iota(jnp.int32, (t, t), 1)
            st = jnp.where(key <= qry, st, NEG_INF)
        return _softmax_step_t(carry, st, [vt_ref[0, hh, j * nsub + c] for c in range(nsub)])

    carry = _two_chain_sweep(qi, qk, soft, tuple(_softmax_init_t(MLA_V, t) for _ in range(2)))
    ot = jnp.concatenate([_softmax_finish_t(c, MLA_V) for c in carry], axis=0)
    o_ref[0] = ot.T.astype(o_ref.dtype)


def _mla_attention(q, k, vt):
    b, h, s, _ = q.shape
    tv = vt.shape[-1]
    dv = vt.shape[-2]
    t = min(512, s)
    assert h % 2 == 0 and 2 * MLA_V == LANES and t % tv == 0 and s % t == 0
    return pl.pallas_call(
        functools.partial(_mla_kernel, t=t, nsub=t // tv),
        grid=(b, h // 2, s // t),
        in_specs=[pl.BlockSpec((1, 2, t, LANES), lambda bi, hi, i: (bi, hi, i, 0)),
                  pl.BlockSpec((1, 2, s, LANES), lambda bi, hi, i: (bi, hi, 0, 0)),
                  pl.BlockSpec((1, 2, s // tv, dv, tv), lambda bi, hi, i: (bi, hi, 0, 0, 0))],
        out_specs=pl.BlockSpec((1, t, LANES), lambda bi, hi, i: (bi, i, hi)),
        out_shape=jax.ShapeDtypeStruct((b, s, h * MLA_V), _MXU),
        scratch_shapes=[pltpu.VMEM((t, t), _F32), pltpu.VMEM((t, t), _F32)],
        compiler_params=_params("parallel", "parallel", "arbitrary"),
        name="mla_attn",
    )(q, k, vt)


def _sb_kernel(q_ref, k_ref, vt_ref, u_ref, o_ref, *scratch_refs, tq, tk):
    qi = pl.program_id(1)
    u = u_ref[...]
    n_heads = q_ref.shape[1]
    scratch = [scratch_refs[6 * hh:6 * (hh + 1)] for hh in range(n_heads)]
    per_tile = tq // tk

    def step(j, carry, key_offset=None):
        diag = key_offset is not None
        off = pl.multiple_of(j * tk, tk)
        if diag:
            key = key_offset + lax.broadcasted_iota(jnp.int32, (tk, tq), 0)
            qry = lax.broadcasted_iota(jnp.int32, (tk, tq), 1)
            strict = key < qry

        def logits(hh):
            z_ref, _, _, _, _, _ = scratch[hh]
            z_ref[...] = _dot_nt(k_ref[0, hh, pl.ds(off, tk), :], q_ref[0, hh])

        def log_terms(hh):
            z_ref, lb_ref, hi_ref, lo_ref, _, _ = scratch[hh]
            z = z_ref[...]
            log_beta = jnp.minimum(z, 0.0) - jnp.log2(1.0 + jnp.exp2(-jnp.abs(z)))
            log_rem = log_beta - z
            if diag:
                log_rem = jnp.where(strict, log_rem, 0.0)
            hi = log_rem.astype(_MXU)
            lb_ref[...] = log_beta
            hi_ref[...] = hi
            lo_ref[...] = (log_rem - hi.astype(_F32)).astype(_MXU)
            return log_rem[0:1, :]

        def suffix_sums(hh):
            _, _, hi_ref, lo_ref, sfx_ref, _ = scratch[hh]
            sfx_ref[...] = (jnp.dot(u, hi_ref[...], preferred_element_type=_F32)
                            + jnp.dot(u, lo_ref[...], preferred_element_type=_F32))

        def weights(hh, first_rem):
            _, lb_ref, _, _, sfx_ref, a_ref = scratch[hh]
            rem = carry[hh][0]
            suffix = sfx_ref[...]
            a = jnp.exp2(lb_ref[...] + suffix + rem)
            if diag:
                a = jnp.where(strict, a, 0.0)
            a_ref[...] = a.astype(_MXU)
            return rem + suffix[0:1, :] + first_rem

        def values(hh):
            a_ref = scratch[hh][5]
            return carry[hh][1] + jnp.dot(vt_ref[0, hh, j], a_ref[...], preferred_element_type=_F32)

        heads = range(n_heads)
        for hh in heads:
            logits(hh)
        first = []
        for hh in heads:
            first.append(log_terms(hh))
            suffix_sums(hh)
        rems = [weights(hh, first[hh]) for hh in heads]
        alive = jnp.max(functools.reduce(jnp.maximum, rems)) > F32_EXP2_ZERO
        return alive, tuple((rems[hh], values(hh)) for hh in heads)

    carry = tuple((jnp.zeros((1, tq), _F32), jnp.zeros((HEAD_DIM, tq), _F32)) for _ in range(n_heads))
    first = qi * per_tile
    for i in reversed(range(per_tile)):
        _, carry = step(first + i, carry, key_offset=i * tk)
    has_past = first > 0
    carry = tuple((jnp.where(has_past, rem, NEG_INF), acc) for rem, acc in carry)
    alive, carry = step(jnp.maximum(first - 1, 0), carry)

    def earlier(c):
        return (c[0] - 1,) + step(c[0], c[2])

    _, _, carry = lax.while_loop(lambda c: jnp.logical_and(c[0] >= 0, c[1]), earlier, (first - 2, alive, carry))
    o_ref[0] = jnp.concatenate([acc for _, acc in carry], axis=0).T.astype(o_ref.dtype)


def _sb_attention(q, k, vt):
    b, h, s, d = q.shape
    tk = vt.shape[-1]
    tq = tk
    assert (h * d) % LANES == 0 and s % tq == 0 and tq % tk == 0
    idx = np.arange(tk)
    u = jnp.asarray(idx[None, :] > idx[:, None], _MXU)
    return pl.pallas_call(
        functools.partial(_sb_kernel, tq=tq, tk=tk),
        grid=(b, s // tq),
        in_specs=[pl.BlockSpec((1, h, tq, d), lambda bi, i: (bi, 0, i, 0)),
                  pl.BlockSpec((1, h, s, d), lambda bi, i: (bi, 0, 0, 0)),
                  pl.BlockSpec((1, h, s // tk, d, tk), lambda bi, i: (bi, 0, 0, 0, 0)),
                  pl.BlockSpec((tk, tk), lambda bi, i: (0, 0))],
        out_specs=pl.BlockSpec((1, tq, h * d), lambda bi, i: (bi, i, 0)),
        out_shape=jax.ShapeDtypeStruct((b, s, h * d), _MXU),
        scratch_shapes=[pltpu.VMEM((tk, tq), dt) for _ in range(h) for dt in (_F32, _F32, _MXU, _MXU, _F32, _MXU)],
        compiler_params=_params("parallel", "arbitrary"),
        name="sb_attn",
    )(q, k, vt, u)


def _compress_kernel(xk_ref, xv_ref, w1k_ref, w2k_ref, pk_ref, w1v_ref, w2v_ref, pv_ref, ok_ref, ov_ref):
    def hidden(x_ref, w1_ref, p_ref):
        x = x_ref[0, 0]
        n = x.shape[0]
        first = jnp.dot(x, w1_ref[0], preferred_element_type=_F32)
        second = jnp.dot(x, w1_ref[1], preferred_element_type=_F32)
        pos = _dot(p_ref[0], w1_ref[0]) + _dot(p_ref[1], w1_ref[1])
        hid = first + pltpu.roll(second, n - 1, 0) + pos[0:1]
        return 0.5 * hid * (1.0 + jnp.tanh(math.sqrt(2.0 / math.pi) * (hid + 0.044715 * hid * hid * hid)))

    ok_ref[0, 0] = _dot(hidden(xk_ref, w1k_ref, pk_ref), w2k_ref[...]).astype(ok_ref.dtype)
    ov_ref[0, 0] = _dot_nt(w2v_ref[...], hidden(xv_ref, w1v_ref, pv_ref)).astype(ov_ref.dtype)


def _compress(xk, xv, layer, w1k, w2k, pk, w1v, w2v, pv):
    b, g, n, _ = xk.shape
    d = HEAD_DIM
    full = functools.partial(_layer_spec, layer=layer)

    xspec = pl.BlockSpec((1, 1, n, CMP_STRIDE * d), lambda bi, gi: (bi, gi, 0, 0))
    return pl.pallas_call(
        _compress_kernel,
        grid=(b, g),
        in_specs=[xspec, xspec, full(w1k), full(w2k), full(pk), full(w1v), full(w2v), full(pv)],
        out_specs=[pl.BlockSpec((1, 1, n, d), lambda bi, gi: (bi, gi, 0, 0)),
                   pl.BlockSpec((1, 1, d, n), lambda bi, gi: (bi, gi, 0, 0))],
        out_shape=[jax.ShapeDtypeStruct((b, g, n, d), _MXU), jax.ShapeDtypeStruct((b, g, d, n), _MXU)],
        compiler_params=_params("parallel", "parallel"),
        name="nsa_compress",
    )(xk, xv, w1k, w2k, pk, w1v, w2v, pv)


def _group_queries(q_ref, g, tq):
    return q_ref[0, g * NSA_GROUP:(g + 1) * NSA_GROUP].reshape(NSA_GROUP * tq, q_ref.shape[-1])


def _gated_heads(ot, gt_ref, g, branch, tq):
    out = []
    for r in range(NSA_GROUP):
        row = NSA_BRANCHES * (g * NSA_GROUP + r) + branch
        out.append(ot[:, r * tq:(r + 1) * tq] * gt_ref[0, row:row + 1, :])
    return out


def _cmp_kernel(q_ref, kc_ref, vct_ref, ov_ref, gt_ref, o_ref, qa_ref, s0_ref, s1_ref, *, tq, n_top):
    q0 = pl.program_id(1) * tq
    ncp = kc_ref.shape[2]
    ns = ov_ref.shape[0]
    lanes = NSA_GROUP * tq
    s_refs = (s0_ref, s1_ref)
    for g in range(NSA_KV_HEADS):
        s_refs[g][...] = _dot_nt(kc_ref[0, g], _group_queries(q_ref, g, tq))
    qpos = q0 + (lax.broadcasted_iota(jnp.int32, (1, lanes), 1) & (tq - 1))
    cmp_end = lax.broadcasted_iota(jnp.int32, (ncp, 1), 0) * CMP_STRIDE + (CMP_LEN - 1)
    visible = cmp_end <= qpos
    cur = jnp.right_shift(q0 + lax.broadcasted_iota(jnp.int32, (1, tq), 1), int(math.log2(SEL_LEN)))
    blk = lax.broadcasted_iota(jnp.int32, (ns, 1), 0)
    forced = (blk == 0) | (blk == cur) | (blk == cur - 1)
    future = blk > cur
    blk_f = blk.astype(_F32)
    heads = []
    scores = []
    for g in range(NSA_KV_HEADS):
        st = jnp.where(visible, s_refs[g][...], NEG_INF)
        e = jnp.exp2(st - jnp.max(st, axis=0, keepdims=True))
        inv = jnp.where(qpos >= CMP_LEN - 1, 1.0 / jnp.sum(e, axis=0, keepdims=True), 0.0)
        pt = e * inv
        heads += _gated_heads(_dot(vct_ref[0, g], pt), gt_ref, g, 0, tq)
        p_sum = sum(pt[:, r * tq:(r + 1) * tq] for r in range(NSA_GROUP))
        score = _dot_split_rhs(ov_ref[...], p_sum)
        scores.append(jnp.where(forced, PICKED, jnp.where(future, -1.0, score)))
    o_ref[0] = jnp.concatenate(heads, axis=0).T.astype(o_ref.dtype)
    def select(rows):
        sc = [s[:rows] for s in scores]
        idx = blk_f[:rows]
        for _ in range(n_top - N_FORCED):
            for g in range(NSA_KV_HEADS):
                first = jnp.argmax(sc[g], axis=0, keepdims=True).astype(_F32)
                sc[g] = jnp.where(idx == first, PICKED, sc[g])
        for g in range(NSA_KV_HEADS):
            sel_m1 = jnp.where(sc[g] < 0.5 * PICKED, 0.0, -1.0)
            if rows < ns:
                sel_m1 = jnp.concatenate([sel_m1, jnp.full((ns - rows, tq), -1.0, _F32)], axis=0)
            sel_m1 = sel_m1.T.astype(qa_ref.dtype)
            for h in range(g * NSA_GROUP, (g + 1) * NSA_GROUP):
                qa_ref[0, h, :, 0:ns] = sel_m1
                qa_ref[0, h, :, ns:ns + HEAD_DIM] = q_ref[0, h]

    visible_blocks = (q0 + tq) // SEL_LEN
    lower = 0
    for rows in sorted({min(ns, r) for r in TOPK_ROW_STEPS} | {ns}):
        in_range = visible_blocks > lower
        if rows < ns:
            in_range = jnp.logical_and(in_range, visible_blocks <= rows)
        pl.when(in_range)(functools.partial(select, rows))
        lower = rows


def _cmp_select(q, kc, vct, gates_t):
    b, h, s, d = q.shape
    g = kc.shape[1]
    ncp = kc.shape[2]
    ns = s // SEL_LEN
    n_top = min(SEL_TOPK, ns)
    tq = min(256, s)
    assert tq & (tq - 1) == 0 and g == 2 and n_top >= N_FORCED
    c0 = np.arange(ncp)[:, None] * CMP_STRIDE
    n0 = np.arange(ns)[None, :] * SEL_LEN
    overlap = jnp.asarray(((c0 < n0 + SEL_LEN) & (c0 + CMP_LEN > n0)).T, _MXU)
    return pl.pallas_call(
        functools.partial(_cmp_kernel, tq=tq, n_top=n_top),
        grid=(b, s // tq),
        in_specs=[pl.BlockSpec((1, h, tq, d), lambda bi, i: (bi, 0, i, 0)),
                  pl.BlockSpec((1, g, ncp, d), lambda bi, i: (bi, 0, 0, 0)),
                  pl.BlockSpec((1, g, d, ncp), lambda bi, i: (bi, 0, 0, 0)),
                  pl.BlockSpec((ns, ncp), lambda bi, i: (0, 0)),
                  pl.BlockSpec((1, GATE_ROWS, tq), lambda bi, i: (bi, 0, i))],
        out_specs=[pl.BlockSpec((1, tq, h * d), lambda bi, i: (bi, i, 0)),
                   pl.BlockSpec((1, h, tq, ns + d), lambda bi, i: (bi, 0, i, 0))],
        out_shape=[jax.ShapeDtypeStruct((b, s, h * d), _MXU), jax.ShapeDtypeStruct((b, h, s, ns + d), _MXU)],
        scratch_shapes=[pltpu.VMEM((ncp, NSA_GROUP * tq), _F32) for _ in range(g)],
        compiler_params=_params("parallel", "arbitrary"),
        name="nsa_cmp_select",
    )(q, kc, vct, overlap, gates_t)


def _key_minus_query(keys, tq):
    return jnp.asarray(np.arange(keys)[:, None] - np.arange(NSA_GROUP * tq)[None, :] % tq, jnp.int32)


def _sel_kernel(q_ref, k_ref, vt_ref, gt_ref, rel_ref, o_ref, s0_ref, s1_ref, *, tq, tk, nsub):
    q0 = pl.program_id(1) * tq
    last = (q0 + tq - 1) // tk
    lanes = NSA_GROUP * tq
    s_refs = (s0_ref, s1_ref)

    def qk(g, j):
        off = pl.multiple_of(j * tk, tk)
        s_refs[g][...] = _dot_nt(k_ref[0, g, pl.ds(off, tk), :], _group_queries(q_ref, g, tq))

    def soft(g, j, carry, causal):
        st = s_refs[g][...]
        if causal:
            st = jnp.where(rel_ref[...] <= q0 - j * tk, st, NEG_INF)
        return _softmax_step_t(carry, st, [vt_ref[0, g, j * nsub + c] for c in range(nsub)])

    init = tuple(_softmax_init_t(HEAD_DIM, lanes) for _ in range(NSA_KV_HEADS))
    carry = _two_chain_sweep(last, qk, soft, init)
    heads = []
    for g in range(NSA_KV_HEADS):
        heads += _gated_heads(_softmax_finish_t(carry[g], HEAD_DIM), gt_ref, g, 1, tq)
    o_ref[0] = jnp.concatenate(heads, axis=0).T.astype(o_ref.dtype)


def _sel_attention(q, k, vt, gates_t):
    b, h, s, da = q.shape
    g = k.shape[1]
    d = HEAD_DIM
    tv = vt.shape[-1]
    tq = min(256, s)
    tk = min(512, s)
    assert tq & (tq - 1) == 0 and s % tk == 0 and tk % tv == 0 and g == 2
    return pl.pallas_call(
        functools.partial(_sel_kernel, tq=tq, tk=tk, nsub=tk // tv),
        grid=(b, s // tq),
        in_specs=[pl.BlockSpec((1, h, tq, da), lambda bi, i: (bi, 0, i, 0)),
                  pl.BlockSpec((1, g, s, da), lambda bi, i: (bi, 0, 0, 0)),
                  pl.BlockSpec((1, g) + vt.shape[2:], lambda bi, i: (bi, 0, 0, 0, 0)),
                  pl.BlockSpec((1, GATE_ROWS, tq), lambda bi, i: (bi, 0, i)),
                  pl.BlockSpec((tk, NSA_GROUP * tq), lambda bi, i: (0, 0))],
        out_specs=pl.BlockSpec((1, tq, h * d), lambda bi, i: (bi, i, 0)),
        out_shape=jax.ShapeDtypeStruct((b, s, h * d), _MXU),
        scratch_shapes=[pltpu.VMEM((tk, NSA_GROUP * tq), _F32) for _ in range(g)],
        compiler_params=_params("parallel", "arbitrary"),
        name="nsa_selected",
    )(q, k, vt, gates_t, _key_minus_query(tk, tq))


def _win_kernel(q_ref, k_ref, vt_ref, gt_ref, rel_ref, o_ref, *s_refs, tq, subs, span, tv):
    lanes = NSA_GROUP * tq
    rel = rel_ref[...]
    chains = [(sub, g) for sub in range(subs) for g in range(NSA_KV_HEADS)]

    def origin(sub):
        q0 = (pl.program_id(1) * subs + sub) * tq
        return q0, pl.multiple_of(jnp.maximum(q0 - WINDOW, 0), tq)

    for c, (sub, g) in enumerate(chains):
        _, start = origin(sub)
        q = q_ref[0, g * NSA_GROUP:(g + 1) * NSA_GROUP, sub * tq:(sub + 1) * tq].reshape(lanes, q_ref.shape[-1])
        s_refs[c][...] = _dot_nt(k_ref[0, g, pl.ds(start, span), :], q)
    for sub in range(subs):
        q0, start = origin(sub)
        offset = q0 - start
        heads = []
        for g in range(NSA_KV_HEADS):
            st = jnp.where(rel <= offset, s_refs[chains.index((sub, g))][...], NEG_INF)
            st = jnp.where(rel > offset - WINDOW, st, NEG_INF)
            carry = _softmax_step_t(_softmax_init_t(HEAD_DIM, lanes), st,
                                    [vt_ref[0, g, start // tv + c] for c in range(span // tv)])
            ot = _softmax_finish_t(carry, HEAD_DIM)
            for r in range(NSA_GROUP):
                row = NSA_BRANCHES * (g * NSA_GROUP + r) + 2
                heads.append(ot[:, r * tq:(r + 1) * tq] * gt_ref[0, row:row + 1, sub * tq:(sub + 1) * tq])
        o_ref[0, sub * tq:(sub + 1) * tq, :] = jnp.concatenate(heads, axis=0).T.astype(o_ref.dtype)


def _win_attention(q, k, vt, gates_t):
    b, h, s, d = q.shape
    g = k.shape[1]
    tv = vt.shape[-1]
    tq = min(256, s)
    subs = 2 if s % (2 * tq) == 0 else 1
    span = WINDOW + tq
    assert tq & (tq - 1) == 0 and s >= span and tq % tv == 0 and WINDOW % tv == 0 and g == NSA_KV_HEADS
    return pl.pallas_call(
        functools.partial(_win_kernel, tq=tq, subs=subs, span=span, tv=tv),
        grid=(b, s // (subs * tq)),
        in_specs=[pl.BlockSpec((1, h, subs * tq, d), lambda bi, i: (bi, 0, i, 0)),
                  pl.BlockSpec((1, g, s, d), lambda bi, i: (bi, 0, 0, 0)),
                  pl.BlockSpec((1, g) + vt.shape[2:], lambda bi, i: (bi, 0, 0, 0, 0)),
                  pl.BlockSpec((1, GATE_ROWS, subs * tq), lambda bi, i: (bi, 0, i)),
                  pl.BlockSpec((span, NSA_GROUP * tq), lambda bi, i: (0, 0))],
        out_specs=pl.BlockSpec((1, subs * tq, h * d), lambda bi, i: (bi, i, 0)),
        out_shape=jax.ShapeDtypeStruct((b, s, h * d), _MXU),
        scratch_shapes=[pltpu.VMEM((span, NSA_GROUP * tq), _F32) for _ in range(subs * g)],
        compiler_params=_params("parallel", "arbitrary"),
        name="nsa_window",
    )(q, k, vt, gates_t, _key_minus_query(span, tq))


def _gather_cols(w, idx):
    idx = np.asarray(idx)
    cuts = [0] + [i for i in range(1, len(idx)) if idx[i] != idx[i - 1] + (idx[i - 1] >= 0)] + [len(idx)]
    pieces = []
    for a, b in zip(cuts[:-1], cuts[1:]):
        if idx[a] < 0:
            pieces.append(jnp.zeros(w.shape[:-1] + (b - a,), _MXU))
        else:
            pieces.append(w[..., int(idx[a]):int(idx[a]) + b - a].astype(_MXU))
    return jnp.concatenate(pieces, axis=-1)


def _swap_halves(rot):
    return (np.arange(rot) + rot // 2) % rot


def _w_in_index():
    idx = np.full((_N_HEAD_COLS * HEAD_DIM,), -1, np.int64)

    def put(col, src):
        src = np.asarray(src)
        idx[col:col + len(src)] = src

    def put_head(pos, src):
        put(pos * HEAD_DIM, src)

    put(_S_CQ * LANES, _O_CQ + np.arange(MLA_Q_LORA))
    put(_S_CKV * LANES, _O_CKV + np.arange(MLA_KV_LORA))
    put(_S_KR * LANES + MLA_NOPE, _O_KR + np.arange(MLA_ROPE))
    put(_S_KRS * LANES + MLA_NOPE, _O_KR + _swap_halves(MLA_ROPE))
    for h in range(NSA_HEADS):
        put_head(_H_NQ + h, _O_NQ + h * HEAD_DIM + np.arange(HEAD_DIM))
        put_head(_H_NQS + h, _O_NQ + h * HEAD_DIM + _swap_halves(PARTIAL_ROT))
    for hk, hks, ok in ((_H_KC, _H_KCS, _O_NKC), (_H_KS, _H_KSS, _O_NKS), (_H_KW, _H_KWS, _O_NKW)):
        for g in range(NSA_KV_HEADS):
            put_head(hk + g, ok + g * HEAD_DIM + np.arange(HEAD_DIM))
            put_head(hks + g, ok + g * HEAD_DIM + _swap_halves(PARTIAL_ROT))
    for g in range(NSA_KV_HEADS):
        put_head(_H_VC + g, _O_NVC + g * HEAD_DIM + np.arange(HEAD_DIM))
    for h in range(SB_HEADS):
        put_head(_H_SBQ + h, _O_SBQ + h * HEAD_DIM + np.arange(HEAD_DIM))
        put_head(_H_SBK + h, _O_SBK + h * HEAD_DIM + np.arange(HEAD_DIM))
    return idx


def _mla_up_index():
    qd = MLA_NOPE + MLA_ROPE
    kd = MLA_NOPE + MLA_V
    uq = np.full((MLA_HEADS * LANES,), -1, np.int64)
    uqs = uq.copy()
    uk = uq.copy()
    for h in range(MLA_HEADS):
        uq[h * LANES:h * LANES + qd] = h * qd + np.arange(qd)
        uqs[h * LANES + MLA_NOPE:h * LANES + qd] = h * qd + MLA_NOPE + _swap_halves(MLA_ROPE)
        uk[h * LANES:h * LANES + MLA_NOPE] = h * kd + np.arange(MLA_NOPE)
    return uq, uqs, uk


def _transposed_weights(w_in, gate_bias):
    width = NSA_KV_HEADS * HEAD_DIM
    gate_rows = jnp.pad(w_in[..., _O_GATE:_O_GATE + N_GATES], ((0, 0), (0, 0), (0, _T_SBV - _T_GATE - N_GATES)))
    rows = jnp.concatenate([w_in[..., _O_NVS:_O_NVS + width], w_in[..., _O_NVW:_O_NVW + width], gate_rows,
                            w_in[..., _O_SBV:_O_SBV + SB_HEADS * HEAD_DIM]], axis=-1)
    bias = jnp.pad(gate_bias, ((0, 0), (0, GATE_ROWS - N_GATES)))[..., None]
    return jnp.swapaxes(rows, -1, -2).astype(_MXU), bias


def _rope_tables(s):
    pos = np.arange(s, dtype=np.float64)

    def cs(rot):
        half = rot // 2
        ang = pos[:, None] * (ROPE_THETA ** (-np.arange(half, dtype=np.float64) / half))[None, :]
        c, sn = np.cos(ang), np.sin(ang)
        return np.concatenate([c, c], axis=1), np.concatenate([-sn, sn], axis=1)

    c, sn = cs(MLA_ROPE)
    pad = np.zeros((s, LANES - MLA_NOPE - MLA_ROPE))
    ck = np.concatenate([np.ones((s, MLA_NOPE)), c, pad], axis=1)
    sk = np.concatenate([np.zeros((s, MLA_NOPE)), sn, pad], axis=1)
    q_scale = (MLA_NOPE + MLA_ROPE) ** -0.5 * LOG2_E
    c, sn = cs(PARTIAL_ROT)
    c64 = np.concatenate([c, np.ones((s, HEAD_DIM - PARTIAL_ROT))], axis=1)
    s64 = np.concatenate([sn, np.zeros((s, HEAD_DIM - PARTIAL_ROT))], axis=1)
    ns = s // SEL_LEN
    onehot = (np.arange(s)[:, None] // SEL_LEN == np.arange(ns)[None, :]) * -NEG_INF
    tables = [jnp.asarray(t, _F32) for t in (ck * q_scale, sk * q_scale, ck, sk, c64, s64)]
    return tables + [jnp.asarray(onehot, _MXU)]


def kernel(x, ffn1_norm, ffn1_w_gate, ffn1_w_up, ffn1_w_down, mix_norm, w_in, mla_q_norm, mla_w_uq, mla_kv_norm,
           mla_w_ukv, nsa_gate_bias, nsa_cmp_pos_k, nsa_cmp_w1_k, nsa_cmp_w2_k, nsa_cmp_pos_v, nsa_cmp_w1_v,
           nsa_cmp_w2_v, w_out, ffn2_norm, ffn2_w_gate, ffn2_w_up, ffn2_w_down, final_norm):
    b, s, d = x.shape
    depth = w_in.shape[0]
    tabs = _rope_tables(s)
    in_idx = _w_in_index()
    uq_idx, uqs_idx, uk_idx = _mla_up_index()
    half = CMP_LEN * HEAD_DIM // 2
    fg = final_norm.reshape(1, d)

    def row(p):
        return p[:, None, :]

    def cmp_weights(w1, w2, pos, transpose_out):
        pos = jnp.broadcast_to(pos.reshape(depth, 2, 1, half), (depth, 2, 8, half)).astype(_MXU)
        w2 = jnp.swapaxes(w2, -1, -2) if transpose_out else w2
        return w1.reshape(depth, 2, half, CMP_HIDDEN).astype(_MXU), w2.astype(_MXU), pos

    ffn1 = [w.astype(_MXU) for w in (ffn1_w_gate, ffn1_w_up, ffn1_w_down)]
    ffn2 = [w.astype(_MXU) for w in (ffn2_w_gate, ffn2_w_up, ffn2_w_down)]
    w_t, gate_bias = _transposed_weights(w_in, nsa_gate_bias)
    wuv_t = mla_w_ukv.reshape(depth, MLA_KV_LORA, MLA_HEADS, 2, MLA_V)[:, :, :, 1].transpose(0, 2, 3, 1).astype(_MXU)
    proj_params = (row(mix_norm), _gather_cols(w_in, in_idx), w_t,
                   row(mla_q_norm), _gather_cols(mla_w_uq, uq_idx), _gather_cols(mla_w_uq, uqs_idx),
                   row(mla_kv_norm), _gather_cols(mla_w_ukv, uk_idx), wuv_t, gate_bias)
    cmp_params = (cmp_weights(nsa_cmp_w1_k, nsa_cmp_w2_k, nsa_cmp_pos_k, False)
                  + cmp_weights(nsa_cmp_w1_v, nsa_cmp_w2_v, nsa_cmp_pos_v, True))
    w_out = w_out.astype(_MXU)
    ffn1_norm, ffn2_norm = row(ffn1_norm), row(ffn2_norm)

    for l in range(depth):
        x2d = _ffn(x.reshape(b * s, d), ffn1_norm, *ffn1, l, fg, False)
        x = x2d.reshape(b, s, d)
        (mq, mk, mvt, nq, nkc, nvc, nks, nvst, nkw, nvwt, gates_t, sbq, sbk, sbv) = _proj(x, l, *proj_params, tabs)
        o_mla = _mla_attention(mq, mk, mvt)
        kc, vct = _compress(nkc, nvc, l, *cmp_params)
        o_cmp, q_sel = _cmp_select(nq, kc, vct, gates_t)
        o_sel = _sel_attention(q_sel, nks, nvst, gates_t)
        o_win = _win_attention(nq, nkw, nvwt, gates_t)
        o_sb = _sb_attention(sbq, sbk, sbv)
        heads = [o.reshape(b * s, -1) for o in (o_mla, o_cmp, o_sel, o_win, o_sb)]
        x2d = _ffn(x.reshape(b * s, d), ffn2_norm, *ffn2, l, fg, l == depth - 1, mixer=(*heads, w_out))
        x = x2d.reshape(b, s, d)
    return x
```

```python
import functools
import math

import numpy as np
import jax
import jax.numpy as jnp
from jax import lax
from jax.experimental import pallas as pl
from jax.experimental.pallas import tpu as pltpu

D_MODEL = 1024
HEAD_DIM = 64
MLA_HEADS = 6
MLA_NOPE = 64
MLA_ROPE = 32
MLA_V = 64
MLA_Q_LORA = 256
MLA_KV_LORA = 128
NSA_HEADS = 6
NSA_KV_HEADS = 2
NSA_GROUP = NSA_HEADS // NSA_KV_HEADS
NSA_BRANCHES = 3
CMP_LEN = 32
CMP_STRIDE = 16
CMP_HIDDEN = 128
SEL_LEN = 64
SEL_TOPK = 16
WINDOW = 512
SB_HEADS = 4
D_FF = 2816
ROPE_THETA = 500000.0
PARTIAL_ROT = HEAD_DIM // 4
EPS = 1e-6
NEG_INF = -1e30
M_FLOOR = 0.1 * NEG_INF
N_FORCED = 3
PICKED = -3e38
F32_EXP2_ZERO = -151.0
LOG2_E = math.log2(math.e)
N_GATES = NSA_HEADS * NSA_BRANCHES

LANES = 128
FFN_CHUNK = 256
SWEEP_UNROLL = 4
TOPK_ROW_STEPS = (32, 64)
SB_KEY_BLOCK = 256
PROJ_ROWS = 512
TOKEN_CHUNK = 256
ONES_PAD = 16
VMEM_LIMIT = 56 * 1024 * 1024

_MXU = jnp.bfloat16
_F32 = jnp.float32

_IN_WIDTHS = (MLA_Q_LORA, MLA_KV_LORA, MLA_ROPE, NSA_HEADS * HEAD_DIM) + (NSA_KV_HEADS * HEAD_DIM,) * 6 + (
    N_GATES, SB_HEADS * HEAD_DIM, SB_HEADS * HEAD_DIM, SB_HEADS * HEAD_DIM)
_IN_OFF = np.concatenate([[0], np.cumsum(_IN_WIDTHS)])
(_O_CQ, _O_CKV, _O_KR, _O_NQ, _O_NKC, _O_NVC, _O_NKS, _O_NVS, _O_NKW, _O_NVW, _O_GATE, _O_SBQ, _O_SBK,
 _O_SBV) = [int(v) for v in _IN_OFF[:-1]]

_S_CQ, _S_CKV, _S_KR, _S_KRS = 0, 2, 3, 4
_H_NQ, _H_NQS = 10, 16
_H_KC, _H_KCS, _H_VC = 22, 24, 26
_H_KS, _H_KSS = 28, 30
_H_KW, _H_KWS = 32, 34
_H_SBQ, _H_SBK = 36, 40
_N_HEAD_COLS = 44
_T_VS, _T_VW, _T_GATE = 0, NSA_KV_HEADS * HEAD_DIM, 2 * NSA_KV_HEADS * HEAD_DIM
GATE_ROWS = 24
_T_SBV = _T_GATE + 2 * ONES_PAD


def _dot(a, b):
    return jnp.dot(a.astype(_MXU), b.astype(_MXU), preferred_element_type=_F32)


def _dot_nt(a, b):
    return lax.dot_general(a.astype(_MXU), b.astype(_MXU), (((1,), (1,)), ((), ())),
                           preferred_element_type=_F32)


def _dot_split_rhs(a, b):
    hi = b.astype(_MXU)
    lo = (b - hi.astype(_F32)).astype(_MXU)
    return (jnp.dot(a, hi, preferred_element_type=_F32) + jnp.dot(a, lo, preferred_element_type=_F32))


def _rms(x, g):
    return x * lax.rsqrt(jnp.mean(x * x, axis=-1, keepdims=True) + EPS) * g


def _params(*sem):
    return pltpu.CompilerParams(dimension_semantics=sem, vmem_limit_bytes=VMEM_LIMIT)


def _layer_spec(a, layer):
    return pl.BlockSpec((None,) + a.shape[1:], lambda *_: (layer,) + (0,) * (a.ndim - 1))


def _mixer_output(mla_ref, cmp_ref, sel_ref, win_ref, sb_ref, w_ref):
    def w_rows(first_head, n_heads):
        return w_ref[first_head * HEAD_DIM:(first_head + n_heads) * HEAD_DIM, :]

    nsa = cmp_ref[...].astype(_F32) + sel_ref[...].astype(_F32) + win_ref[...].astype(_F32)
    return (jnp.dot(mla_ref[...], w_rows(0, MLA_HEADS), preferred_element_type=_F32)
            + _dot(nsa, w_rows(MLA_HEADS, NSA_HEADS))
            + jnp.dot(sb_ref[...], w_rows(MLA_HEADS + NSA_HEADS, SB_HEADS), preferred_element_type=_F32))


def _ffn_kernel(x_ref, g_ref, wg_ref, wu_ref, wd_ref, fg_ref, *rest, final_norm, with_mixer):
    *mixer_refs, o_ref, act_ref = rest
    x = x_ref[...]
    if with_mixer:
        x = x + _mixer_output(*mixer_refs)
    h = _rms(x, g_ref[...]).astype(_MXU)
    tf = act_ref.shape[1]
    for c0 in range(0, tf, FFN_CHUNK):
        c1 = min(c0 + FFN_CHUNK, tf)
        gate = jnp.dot(h, wg_ref[:, c0:c1], preferred_element_type=_F32)
        up = jnp.dot(h, wu_ref[:, c0:c1], preferred_element_type=_F32)
        act_ref[:, c0:c1] = (gate * jax.nn.sigmoid(gate) * up).astype(act_ref.dtype)
    y = x + 0.5 * jnp.dot(act_ref[...], wd_ref[...], preferred_element_type=_F32)
    if final_norm:
        y = _rms(y, fg_ref[...])
    o_ref[...] = y


def _ffn(x2d, g, wg, wu, wd, layer, fg, final_norm, mixer=None):
    rows = x2d.shape[0]
    tm = min(1024, rows)

    def resident(a):
        return pl.BlockSpec((None,) + a.shape[1:], lambda i: (layer, 0, 0), pipeline_mode=pl.Buffered(1))

    def row_tile(a):
        return pl.BlockSpec((tm, a.shape[1]), lambda i: (i, 0))

    mixer_args, mixer_specs = (), []
    if mixer is not None:
        *heads, w_out = mixer
        mixer_args = (*heads, w_out)
        mixer_specs = [row_tile(a) for a in heads] + [resident(w_out)]
    return pl.pallas_call(
        functools.partial(_ffn_kernel, final_norm=final_norm, with_mixer=mixer is not None),
        grid=(rows // tm,),
        in_specs=[row_tile(x2d), _layer_spec(g, layer), resident(wg), resident(wu), resident(wd),
                  pl.BlockSpec((1, D_MODEL), lambda i: (0, 0))] + mixer_specs,
        out_specs=row_tile(x2d),
        out_shape=jax.ShapeDtypeStruct((rows, D_MODEL), _F32),
        scratch_shapes=[pltpu.VMEM((tm, D_FF), _MXU)],
        compiler_params=_params("parallel"),
        name="ffn",
    )(x2d, g, wg, wu, wd, fg, *mixer_args)


def _proj_kernel(x_ref, g_ref, w_ref, wt_ref, qn_ref, wuq_ref, wuqs_ref, kvn_ref, wuk_ref, wuv_ref, gb_ref,
                 cq_ref, sq_ref, ck_ref, sk_ref, c64_ref, s64_ref, oh_ref,
                 mq_ref, mk_ref, mv_ref, nq_ref, nkc_ref, nvc_ref, nks_ref, nvs_ref, nkw_ref, nvw_ref,
                 gate_ref, sbq_ref, sbk_ref, sbv_ref, stage_ref):
    hn = _rms(x_ref[0], g_ref[...]).astype(_MXU)

    def proj(h0, h1):
        return jnp.dot(hn, w_ref[:, h0 * HEAD_DIM:h1 * HEAD_DIM], preferred_element_type=_F32)

    def slot(p, s):
        return p[:, s * LANES:(s + 1) * LANES]

    def head(p, i):
        return p[:, i * HEAD_DIM:(i + 1) * HEAD_DIM]

    def write_value_chunks(o_ref, h, vt):
        width = o_ref.shape[-1]
        for c in range(o_ref.shape[2]):
            o_ref[0, h, c] = vt[:, c * width:(c + 1) * width].astype(o_ref.dtype)

    p = proj(0, _H_NQ)
    cq = _rms(p[:, :MLA_Q_LORA], qn_ref[...])
    ckv = _rms(slot(p, _S_CKV), kvn_ref[...])
    q = _dot(cq, wuq_ref[...])
    q_partner = _dot(cq, wuqs_ref[...])
    kpe = slot(p, _S_KR) * ck_ref[...] + slot(p, _S_KRS) * sk_ref[...]
    kn = _dot(ckv, wuk_ref[...])
    for h in range(MLA_HEADS):
        mq_ref[0, h] = (slot(q, h) * cq_ref[...] + slot(q_partner, h) * sq_ref[...]).astype(mq_ref.dtype)
        mk_ref[0, h] = (slot(kn, h) + kpe).astype(mk_ref.dtype)
        write_value_chunks(mv_ref, h, _ones_row_pad(_dot_nt(wuv_ref[h], ckv)))

    c64 = c64_ref[...]
    s64 = s64_ref[...]
    scale = HEAD_DIM ** -0.5

    p = proj(_H_NQ, _H_KC)
    for h in range(NSA_HEADS):
        nq_ref[0, h] = ((head(p, h) * c64 + head(p, NSA_HEADS + h) * s64) * (scale * LOG2_E)).astype(nq_ref.dtype)

    p = proj(_H_KC, _H_SBQ)
    base = _H_KC
    ns = oh_ref.shape[-1]

    def roped(hk, hks, g):
        return head(p, hk - base + g) * c64 + head(p, hks - base + g) * s64

    def write_chunked(o_ref, g, val):
        stage_ref[...] = val
        for t in range(CMP_STRIDE):
            piece = stage_ref[pl.ds(t, val.shape[0] // CMP_STRIDE, stride=CMP_STRIDE), :]
            o_ref[0, g, :, t * HEAD_DIM:(t + 1) * HEAD_DIM] = piece.astype(o_ref.dtype)

    for g in range(NSA_KV_HEADS):
        write_chunked(nkc_ref, g, roped(_H_KC, _H_KCS, g))
        write_chunked(nvc_ref, g, head(p, _H_VC - base + g))
        nks_ref[0, g, :, 0:ns] = oh_ref[...]
        nks_ref[0, g, :, ns:ns + HEAD_DIM] = roped(_H_KS, _H_KSS, g).astype(nks_ref.dtype)
        nkw_ref[0, g] = roped(_H_KW, _H_KWS, g).astype(nkw_ref.dtype)

    pt = _dot_nt(wt_ref[...], hn)
    for g in range(NSA_KV_HEADS):
        lo = g * HEAD_DIM
        write_value_chunks(nvs_ref, g, _ones_row_pad(pt[_T_VS + lo:_T_VS + lo + HEAD_DIM]))
        write_value_chunks(nvw_ref, g, _ones_row_pad(pt[_T_VW + lo:_T_VW + lo + HEAD_DIM]))
    gate_ref[0] = jax.nn.sigmoid(pt[_T_GATE:_T_GATE + GATE_ROWS] + gb_ref[...])

    p = proj(_H_SBQ, _N_HEAD_COLS)
    for h in range(SB_HEADS):
        sbq_ref[0, h] = (head(p, h) * (scale * LOG2_E)).astype(sbq_ref.dtype)
        sbk_ref[0, h] = head(p, SB_HEADS + h).astype(sbk_ref.dtype)
        write_value_chunks(sbv_ref, h, pt[_T_SBV + h * HEAD_DIM:_T_SBV + (h + 1) * HEAD_DIM])


def _proj(x, layer, g, w_ext, w_t, qn, wuq, wuqs, kvn, wuk, wuv, gb, tabs):
    b, s, _ = x.shape
    ts = min(PROJ_ROWS, s)
    tv = min(TOKEN_CHUNK, ts)
    cq, sq, ck, sk, c64, s64, onehot = tabs
    full = functools.partial(_layer_spec, layer=layer)

    def tab(a):
        return pl.BlockSpec((ts, a.shape[1]), lambda bi, i: (i, 0))

    def heads(n, d):
        return (pl.BlockSpec((1, n, ts, d), lambda bi, i: (bi, 0, i, 0)),
                jax.ShapeDtypeStruct((b, n, s, d), _MXU))

    def values_t(n):
        return (pl.BlockSpec((1, n, ts // tv, HEAD_DIM + ONES_PAD, tv), lambda bi, i: (bi, 0, i, 0, 0)),
                jax.ShapeDtypeStruct((b, n, s // tv, HEAD_DIM + ONES_PAD, tv), _MXU))

    def chunked():
        return (pl.BlockSpec((1, NSA_KV_HEADS, ts // CMP_STRIDE, CMP_STRIDE * HEAD_DIM), lambda bi, i: (bi, 0, i, 0)),
                jax.ShapeDtypeStruct((b, NSA_KV_HEADS, s // CMP_STRIDE, CMP_STRIDE * HEAD_DIM), _MXU))

    outs = [heads(MLA_HEADS, LANES), heads(MLA_HEADS, LANES), values_t(MLA_HEADS), heads(NSA_HEADS, HEAD_DIM),
            chunked(), chunked(), heads(NSA_KV_HEADS, onehot.shape[1] + HEAD_DIM), values_t(NSA_KV_HEADS),
            heads(NSA_KV_HEADS, HEAD_DIM), values_t(NSA_KV_HEADS),
            (pl.BlockSpec((1, GATE_ROWS, ts), lambda bi, i: (bi, 0, i)), jax.ShapeDtypeStruct((b, GATE_ROWS, s), _F32)),
            heads(SB_HEADS, HEAD_DIM), heads(SB_HEADS, HEAD_DIM),
            (pl.BlockSpec((1, SB_HEADS, ts // SB_KEY_BLOCK, HEAD_DIM, SB_KEY_BLOCK), lambda bi, i: (bi, 0, i, 0, 0)),
             jax.ShapeDtypeStruct((b, SB_HEADS, s // SB_KEY_BLOCK, HEAD_DIM, SB_KEY_BLOCK), _MXU))]
    return pl.pallas_call(
        _proj_kernel,
        grid=(b, s // ts),
        in_specs=[pl.BlockSpec((1, ts, D_MODEL), lambda bi, i: (bi, i, 0)), full(g), full(w_ext), full(w_t), full(qn),
                  full(wuq), full(wuqs), full(kvn), full(wuk), full(wuv), full(gb),
                  tab(cq), tab(sq), tab(ck), tab(sk), tab(c64), tab(s64), tab(onehot)],
        out_specs=[o[0] for o in outs],
        out_shape=[o[1] for o in outs],
        scratch_shapes=[pltpu.VMEM((ts, HEAD_DIM), _F32)],
        compiler_params=_params("parallel", "parallel"),
        name="proj",
    )(x, g, w_ext, w_t, qn, wuq, wuqs, kvn, wuk, wuv, gb, cq, sq, ck, sk, c64, s64, onehot)


def _ones_row_pad(vt):
    first = lax.broadcasted_iota(jnp.int32, (ONES_PAD, vt.shape[1]), 0) == 0
    return jnp.concatenate([vt, jnp.where(first, 1.0, 0.0).astype(vt.dtype)], axis=0)


def _softmax_step_t(carry, st, vt_chunks):
    m, acc = carry
    m_new = jnp.maximum(m, jnp.max(st, axis=0, keepdims=True))
    alpha = jnp.exp2(m - m_new)
    pt = jnp.exp2(st - m_new).astype(_MXU)
    n = st.shape[0] // len(vt_chunks)
    pv = sum(jnp.dot(vt, pt[c * n:(c + 1) * n], preferred_element_type=_F32) for c, vt in enumerate(vt_chunks))
    return m_new, alpha * acc + pv


def _softmax_init_t(d, cols):
    return (jnp.full((1, cols), M_FLOOR, _F32), jnp.zeros((d + ONES_PAD, cols), _F32))


def _softmax_finish_t(carry, d):
    _, acc = carry
    return acc[:d] * (1.0 / acc[d:d + 1])


def _two_chain_sweep(n_full, qk, soft, init):
    def body(j, carry, diag=False):
        c0, c1 = carry
        qk(0, j)
        c1 = soft(1, j, c1, diag)
        qk(1, jnp.zeros_like(j) if diag else j + 1)
        c0 = soft(0, j, c0, diag)
        return c0, c1

    qk(1, n_full)
    carry = body(n_full, init, True)
    done = jnp.zeros_like(n_full)
    unroll = SWEEP_UNROLL
    while unroll >= 1:
        def unrolled(i, carry, first=done, unroll=unroll):
            for u in range(unroll):
                carry = body(first + unroll * i + u, carry)
            return carry

        trips = (n_full - done) // unroll
        carry = lax.fori_loop(0, trips, unrolled, carry)
        done = done + trips * unroll
        unroll //= 2
    return carry


def _mla_kernel(q_ref, k_ref, vt_ref, o_ref, s0_ref, s1_ref, *, t, nsub):
    qi = pl.program_id(2)
    s_refs = (s0_ref, s1_ref)

    def qk(hh, j):
        off = pl.multiple_of(j * t, t)
        s_refs[hh][...] = _dot_nt(k_ref[0, hh, pl.ds(off, t), :], q_ref[0, hh])

    def soft(hh, j, carry, diag):
        st = s_refs[hh][...]
        if diag:
            key = lax.broadcasted_iota(jnp.int32, (t, t), 0)
            qry = lax.broadcasted_iota(jnp.int32, (t, t), 1)
            st = jnp.where(key <= qry, st, NEG_INF)
        return _softmax_step_t(carry, st, [vt_ref[0, hh, j * nsub + c] for c in range(nsub)])

    carry = _two_chain_sweep(qi, qk, soft, tuple(_softmax_init_t(MLA_V, t) for _ in range(2)))
    ot = jnp.concatenate([_softmax_finish_t(c, MLA_V) for c in carry], axis=0)
    o_ref[0] = ot.T.astype(o_ref.dtype)


def _mla_attention(q, k, vt):
    b, h, s, _ = q.shape
    tv = vt.shape[-1]
    dv = vt.shape[-2]
    t = min(512, s)
    assert h % 2 == 0 and 2 * MLA_V == LANES and t % tv == 0 and s % t == 0
    return pl.pallas_call(
        functools.partial(_mla_kernel, t=t, nsub=t // tv),
        grid=(b, h // 2, s // t),
        in_specs=[pl.BlockSpec((1, 2, t, LANES), lambda bi, hi, i: (bi, hi, i, 0)),
                  pl.BlockSpec((1, 2, s, LANES), lambda bi, hi, i: (bi, hi, 0, 0)),
                  pl.BlockSpec((1, 2, s // tv, dv, tv), lambda bi, hi, i: (bi, hi, 0, 0, 0))],
        out_specs=pl.BlockSpec((1, t, LANES), lambda bi, hi, i: (bi, i, hi)),
        out_shape=jax.ShapeDtypeStruct((b, s, h * MLA_V), _MXU),
        scratch_shapes=[pltpu.VMEM((t, t), _F32), pltpu.VMEM((t, t), _F32)],
        compiler_params=_params("parallel", "parallel", "arbitrary"),
        name="mla_attn",
    )(q, k, vt)


def _sb_kernel(q_ref, k_ref, vt_ref, u_ref, o_ref, *scratch_refs, tq, tk):
    qi = pl.program_id(1)
    u = u_ref[...]
    n_heads = q_ref.shape[1]
    scratch = [scratch_refs[6 * hh:6 * (hh + 1)] for hh in range(n_heads)]
    per_tile = tq // tk

    def step(j, carry, key_offset=None):
        diag = key_offset is not None
        off = pl.multiple_of(j * tk, tk)
        if diag:
            key = key_offset + lax.broadcasted_iota(jnp.int32, (tk, tq), 0)
            qry = lax.broadcasted_iota(jnp.int32, (tk, tq), 1)
            strict = key < qry

        def logits(hh):
            z_ref, _, _, _, _, _ = scratch[hh]
            z_ref[...] = _dot_nt(k_ref[0, hh, pl.ds(off, tk), :], q_ref[0, hh])

        def log_terms(hh):
            z_ref, lb_ref, hi_ref, lo_ref, _, _ = scratch[hh]
            z = z_ref[...]
            log_beta = jnp.minimum(z, 0.0) - jnp.log2(1.0 + jnp.exp2(-jnp.abs(z)))
            log_rem = log_beta - z
            if diag:
                log_rem = jnp.where(strict, log_rem, 0.0)
            hi = log_rem.astype(_MXU)
            lb_ref[...] = log_beta
            hi_ref[...] = hi
            lo_ref[...] = (log_rem - hi.astype(_F32)).astype(_MXU)
            return log_rem[0:1, :]

        def suffix_sums(hh):
            _, _, hi_ref, lo_ref, sfx_ref, _ = scratch[hh]
            sfx_ref[...] = (jnp.dot(u, hi_ref[...], preferred_element_type=_F32)
                            + jnp.dot(u, lo_ref[...], preferred_element_type=_F32))

        def weights(hh, first_rem):
            _, lb_ref, _, _, sfx_ref, a_ref = scratch[hh]
            rem = carry[hh][0]
            suffix = sfx_ref[...]
            a = jnp.exp2(lb_ref[...] + suffix + rem)
            if diag:
                a = jnp.where(strict, a, 0.0)
            a_ref[...] = a.astype(_MXU)
            return rem + suffix[0:1, :] + first_rem

        def values(hh):
            a_ref = scratch[hh][5]
            return carry[hh][1] + jnp.dot(vt_ref[0, hh, j], a_ref[...], preferred_element_type=_F32)

        heads = range(n_heads)
        for hh in heads:
            logits(hh)
        first = []
        for hh in heads:
            first.append(log_terms(hh))
            suffix_sums(hh)
        rems = [weights(hh, first[hh]) for hh in heads]
        alive = jnp.max(functools.reduce(jnp.maximum, rems)) > F32_EXP2_ZERO
        return alive, tuple((rems[hh], values(hh)) for hh in heads)

    carry = tuple((jnp.zeros((1, tq), _F32), jnp.zeros((HEAD_DIM, tq), _F32)) for _ in range(n_heads))
    first = qi * per_tile
    for i in reversed(range(per_tile)):
        _, carry = step(first + i, carry, key_offset=i * tk)
    has_past = first > 0
    carry = tuple((jnp.where(has_past, rem, NEG_INF), acc) for rem, acc in carry)
    alive, carry = step(jnp.maximum(first - 1, 0), carry)

    def earlier(c):
        return (c[0] - 1,) + step(c[0], c[2])

    _, _, carry = lax.while_loop(lambda c: jnp.logical_and(c[0] >= 0, c[1]), earlier, (first - 2, alive, carry))
    o_ref[0] = jnp.concatenate([acc for _, acc in carry], axis=0).T.astype(o_ref.dtype)


def _sb_attention(q, k, vt):
    b, h, s, d = q.shape
    tk = vt.shape[-1]
    tq = tk
    assert (h * d) % LANES == 0 and s % tq == 0 and tq % tk == 0
    idx = np.arange(tk)
    u = jnp.asarray(idx[None, :] > idx[:, None], _MXU)
    return pl.pallas_call(
        functools.partial(_sb_kernel, tq=tq, tk=tk),
        grid=(b, s // tq),
        in_specs=[pl.BlockSpec((1, h, tq, d), lambda bi, i: (bi, 0, i, 0)),
                  pl.BlockSpec((1, h, s, d), lambda bi, i: (bi, 0, 0, 0)),
                  pl.BlockSpec((1, h, s // tk, d, tk), lambda bi, i: (bi, 0, 0, 0, 0)),
                  pl.BlockSpec((tk, tk), lambda bi, i: (0, 0))],
        out_specs=pl.BlockSpec((1, tq, h * d), lambda bi, i: (bi, i, 0)),
        out_shape=jax.ShapeDtypeStruct((b, s, h * d), _MXU),
        scratch_shapes=[pltpu.VMEM((tk, tq), dt) for _ in range(h) for dt in (_F32, _F32, _MXU, _MXU, _F32, _MXU)],
        compiler_params=_params("parallel", "arbitrary"),
        name="sb_attn",
    )(q, k, vt, u)


def _compress_kernel(xk_ref, xv_ref, w1k_ref, w2k_ref, pk_ref, w1v_ref, w2v_ref, pv_ref, ok_ref, ov_ref):
    def hidden(x_ref, w1_ref, p_ref):
        x = x_ref[0, 0]
        n = x.shape[0]
        first = jnp.dot(x, w1_ref[0], preferred_element_type=_F32)
        second = jnp.dot(x, w1_ref[1], preferred_element_type=_F32)
        pos = _dot(p_ref[0], w1_ref[0]) + _dot(p_ref[1], w1_ref[1])
        hid = first + pltpu.roll(second, n - 1, 0) + pos[0:1]
        return 0.5 * hid * (1.0 + jnp.tanh(math.sqrt(2.0 / math.pi) * (hid + 0.044715 * hid * hid * hid)))

    ok_ref[0, 0] = _dot(hidden(xk_ref, w1k_ref, pk_ref), w2k_ref[...]).astype(ok_ref.dtype)
    ov_ref[0, 0] = _dot_nt(w2v_ref[...], hidden(xv_ref, w1v_ref, pv_ref)).astype(ov_ref.dtype)


def _compress(xk, xv, layer, w1k, w2k, pk, w1v, w2v, pv):
    b, g, n, _ = xk.shape
    d = HEAD_DIM
    full = functools.partial(_layer_spec, layer=layer)

    xspec = pl.BlockSpec((1, 1, n, CMP_STRIDE * d), lambda bi, gi: (bi, gi, 0, 0))
    return pl.pallas_call(
        _compress_kernel,
        grid=(b, g),
        in_specs=[xspec, xspec, full(w1k), full(w2k), full(pk), full(w1v), full(w2v), full(pv)],
        out_specs=[pl.BlockSpec((1, 1, n, d), lambda bi, gi: (bi, gi, 0, 0)),
                   pl.BlockSpec((1, 1, d, n), lambda bi, gi: (bi, gi, 0, 0))],
        out_shape=[jax.ShapeDtypeStruct((b, g, n, d), _MXU), jax.ShapeDtypeStruct((b, g, d, n), _MXU)],
        compiler_params=_params("parallel", "parallel"),
        name="nsa_compress",
    )(xk, xv, w1k, w2k, pk, w1v, w2v, pv)


def _group_queries(q_ref, g, tq):
    return q_ref[0, g * NSA_GROUP:(g + 1) * NSA_GROUP].reshape(NSA_GROUP * tq, q_ref.shape[-1])


def _gated_heads(ot, gt_ref, g, branch, tq):
    out = []
    for r in range(NSA_GROUP):
        row = NSA_BRANCHES * (g * NSA_GROUP + r) + branch
        out.append(ot[:, r * tq:(r + 1) * tq] * gt_ref[0, row:row + 1, :])
    return out


def _cmp_kernel(q_ref, kc_ref, vct_ref, ov_ref, gt_ref, o_ref, qa_ref, s0_ref, s1_ref, *, tq, n_top):
    q0 = pl.program_id(1) * tq
    ncp = kc_ref.shape[2]
    ns = ov_ref.shape[0]
    lanes = NSA_GROUP * tq
    s_refs = (s0_ref, s1_ref)
    for g in range(NSA_KV_HEADS):
        s_refs[g][...] = _dot_nt(kc_ref[0, g], _group_queries(q_ref, g, tq))
    qpos = q0 + (lax.broadcasted_iota(jnp.int32, (1, lanes), 1) & (tq - 1))
    cmp_end = lax.broadcasted_iota(jnp.int32, (ncp, 1), 0) * CMP_STRIDE + (CMP_LEN - 1)
    visible = cmp_end <= qpos
    cur = jnp.right_shift(q0 + lax.broadcasted_iota(jnp.int32, (1, tq), 1), int(math.log2(SEL_LEN)))
    blk = lax.broadcasted_iota(jnp.int32, (ns, 1), 0)
    forced = (blk == 0) | (blk == cur) | (blk == cur - 1)
    future = blk > cur
    blk_f = blk.astype(_F32)
    heads = []
    scores = []
    for g in range(NSA_KV_HEADS):
        st = jnp.where(visible, s_refs[g][...], NEG_INF)
        e = jnp.exp2(st - jnp.max(st, axis=0, keepdims=True))
        inv = jnp.where(qpos >= CMP_LEN - 1, 1.0 / jnp.sum(e, axis=0, keepdims=True), 0.0)
        pt = e * inv
        heads += _gated_heads(_dot(vct_ref[0, g], pt), gt_ref, g, 0, tq)
        p_sum = sum(pt[:, r * tq:(r + 1) * tq] for r in range(NSA_GROUP))
        score = _dot_split_rhs(ov_ref[...], p_sum)
        scores.append(jnp.where(forced, PICKED, jnp.where(future, -1.0, score)))
    o_ref[0] = jnp.concatenate(heads, axis=0).T.astype(o_ref.dtype)
    def select(rows):
        sc = [s[:rows] for s in scores]
        idx = blk_f[:rows]
        for _ in range(n_top - N_FORCED):
            for g in range(NSA_KV_HEADS):
                top = jnp.max(sc[g], axis=0, keepdims=True)
                first = jnp.min(jnp.where(sc[g] == top, idx, float(ns)), axis=0, keepdims=True)
                sc[g] = jnp.where(idx == first, PICKED, sc[g])
        for g in range(NSA_KV_HEADS):
            sel_m1 = jnp.where(sc[g] < 0.5 * PICKED, 0.0, -1.0)
            if rows < ns:
                sel_m1 = jnp.concatenate([sel_m1, jnp.full((ns - rows, tq), -1.0, _F32)], axis=0)
            sel_m1 = sel_m1.T.astype(qa_ref.dtype)
            for h in range(g * NSA_GROUP, (g + 1) * NSA_GROUP):
                qa_ref[0, h, :, 0:ns] = sel_m1
                qa_ref[0, h, :, ns:ns + HEAD_DIM] = q_ref[0, h]

    visible_blocks = (q0 + tq) // SEL_LEN
    lower = 0
    for rows in sorted({min(ns, r) for r in TOPK_ROW_STEPS} | {ns}):
        in_range = visible_blocks > lower
        if rows < ns:
            in_range = jnp.logical_and(in_range, visible_blocks <= rows)
        pl.when(in_range)(functools.partial(select, rows))
        lower = rows


def _cmp_select(q, kc, vct, gates_t):
    b, h, s, d = q.shape
    g = kc.shape[1]
    ncp = kc.shape[2]
    ns = s // SEL_LEN
    n_top = min(SEL_TOPK, ns)
    tq = min(256, s)
    assert tq & (tq - 1) == 0 and g == 2 and n_top >= N_FORCED
    c0 = np.arange(ncp)[:, None] * CMP_STRIDE
    n0 = np.arange(ns)[None, :] * SEL_LEN
    overlap = jnp.asarray(((c0 < n0 + SEL_LEN) & (c0 + CMP_LEN > n0)).T, _MXU)
    return pl.pallas_call(
        functools.partial(_cmp_kernel, tq=tq, n_top=n_top),
        grid=(b, s // tq),
        in_specs=[pl.BlockSpec((1, h, tq, d), lambda bi, i: (bi, 0, i, 0)),
                  pl.BlockSpec((1, g, ncp, d), lambda bi, i: (bi, 0, 0, 0)),
                  pl.BlockSpec((1, g, d, ncp), lambda bi, i: (bi, 0, 0, 0)),
                  pl.BlockSpec((ns, ncp), lambda bi, i: (0, 0)),
                  pl.BlockSpec((1, GATE_ROWS, tq), lambda bi, i: (bi, 0, i))],
        out_specs=[pl.BlockSpec((1, tq, h * d), lambda bi, i: (bi, i, 0)),
                   pl.BlockSpec((1, h, tq, ns + d), lambda bi, i: (bi, 0, i, 0))],
        out_shape=[jax.ShapeDtypeStruct((b, s, h * d), _MXU), jax.ShapeDtypeStruct((b, h, s, ns + d), _MXU)],
        scratch_shapes=[pltpu.VMEM((ncp, NSA_GROUP * tq), _F32) for _ in range(g)],
        compiler_params=_params("parallel", "arbitrary"),
        name="nsa_cmp_select",
    )(q, kc, vct, overlap, gates_t)


def _key_minus_query(keys, tq):
    return jnp.asarray(np.arange(keys)[:, None] - np.arange(NSA_GROUP * tq)[None, :] % tq, jnp.int32)


def _sel_kernel(q_ref, k_ref, vt_ref, gt_ref, rel_ref, o_ref, s0_ref, s1_ref, *, tq, tk, nsub):
    q0 = pl.program_id(1) * tq
    last = (q0 + tq - 1) // tk
    lanes = NSA_GROUP * tq
    s_refs = (s0_ref, s1_ref)

    def qk(g, j):
        off = pl.multiple_of(j * tk, tk)
        s_refs[g][...] = _dot_nt(k_ref[0, g, pl.ds(off, tk), :], _group_queries(q_ref, g, tq))

    def soft(g, j, carry, causal):
        st = s_refs[g][...]
        if causal:
            st = jnp.where(rel_ref[...] <= q0 - j * tk, st, NEG_INF)
        return _softmax_step_t(carry, st, [vt_ref[0, g, j * nsub + c] for c in range(nsub)])

    init = tuple(_softmax_init_t(HEAD_DIM, lanes) for _ in range(NSA_KV_HEADS))
    carry = _two_chain_sweep(last, qk, soft, init)
    heads = []
    for g in range(NSA_KV_HEADS):
        heads += _gated_heads(_softmax_finish_t(carry[g], HEAD_DIM), gt_ref, g, 1, tq)
    o_ref[0] = jnp.concatenate(heads, axis=0).T.astype(o_ref.dtype)


def _sel_attention(q, k, vt, gates_t):
    b, h, s, da = q.shape
    g = k.shape[1]
    d = HEAD_DIM
    tv = vt.shape[-1]
    tq = min(256, s)
    tk = min(512, s)
    assert tq & (tq - 1) == 0 and s % tk == 0 and tk % tv == 0 and g == 2
    return pl.pallas_call(
        functools.partial(_sel_kernel, tq=tq, tk=tk, nsub=tk // tv),
        grid=(b, s // tq),
        in_specs=[pl.BlockSpec((1, h, tq, da), lambda bi, i: (bi, 0, i, 0)),
                  pl.BlockSpec((1, g, s, da), lambda bi, i: (bi, 0, 0, 0)),
                  pl.BlockSpec((1, g) + vt.shape[2:], lambda bi, i: (bi, 0, 0, 0, 0)),
                  pl.BlockSpec((1, GATE_ROWS, tq), lambda bi, i: (bi, 0, i)),
                  pl.BlockSpec((tk, NSA_GROUP * tq), lambda bi, i: (0, 0))],
        out_specs=pl.BlockSpec((1, tq, h * d), lambda bi, i: (bi, i, 0)),
        out_shape=jax.ShapeDtypeStruct((b, s, h * d), _MXU),
        scratch_shapes=[pltpu.VMEM((tk, NSA_GROUP * tq), _F32) for _ in range(g)],
        compiler_params=_params("parallel", "arbitrary"),
        name="nsa_selected",
    )(q, k, vt, gates_t, _key_minus_query(tk, tq))


def _win_kernel(q_ref, k_ref, vt_ref, gt_ref, rel_ref, o_ref, *s_refs, tq, subs, span, tv):
    lanes = NSA_GROUP * tq
    rel = rel_ref[...]
    chains = [(sub, g) for sub in range(subs) for g in range(NSA_KV_HEADS)]

    def origin(sub):
        q0 = (pl.program_id(1) * subs + sub) * tq
        return q0, pl.multiple_of(jnp.maximum(q0 - WINDOW, 0), tq)

    for c, (sub, g) in enumerate(chains):
        _, start = origin(sub)
        q = q_ref[0, g * NSA_GROUP:(g + 1) * NSA_GROUP, sub * tq:(sub + 1) * tq].reshape(lanes, q_ref.shape[-1])
        s_refs[c][...] = _dot_nt(k_ref[0, g, pl.ds(start, span), :], q)
    for sub in range(subs):
        q0, start = origin(sub)
        offset = q0 - start
        heads = []
        for g in range(NSA_KV_HEADS):
            st = jnp.where(rel <= offset, s_refs[chains.index((sub, g))][...], NEG_INF)
            st = jnp.where(rel > offset - WINDOW, st, NEG_INF)
            carry = _softmax_step_t(_softmax_init_t(HEAD_DIM, lanes), st,
                                    [vt_ref[0, g, start // tv + c] for c in range(span // tv)])
            ot = _softmax_finish_t(carry, HEAD_DIM)
            for r in range(NSA_GROUP):
                row = NSA_BRANCHES * (g * NSA_GROUP + r) + 2
                heads.append(ot[:, r * tq:(r + 1) * tq] * gt_ref[0, row:row + 1, sub * tq:(sub + 1) * tq])
        o_ref[0, sub * tq:(sub + 1) * tq, :] = jnp.concatenate(heads, axis=0).T.astype(o_ref.dtype)


def _win_attention(q, k, vt, gates_t):
    b, h, s, d = q.shape
    g = k.shape[1]
    tv = vt.shape[-1]
    tq = min(256, s)
    subs = 2 if s % (2 * tq) == 0 else 1
    span = WINDOW + tq
    assert tq & (tq - 1) == 0 and s >= span and tq % tv == 0 and WINDOW % tv == 0 and g == NSA_KV_HEADS
    return pl.pallas_call(
        functools.partial(_win_kernel, tq=tq, subs=subs, span=span, tv=tv),
        grid=(b, s // (subs * tq)),
        in_specs=[pl.BlockSpec((1, h, subs * tq, d), lambda bi, i: (bi, 0, i, 0)),
                  pl.BlockSpec((1, g, s, d), lambda bi, i: (bi, 0, 0, 0)),
                  pl.BlockSpec((1, g) + vt.shape[2:], lambda bi, i: (bi, 0, 0, 0, 0)),
                  pl.BlockSpec((1, GATE_ROWS, subs * tq), lambda bi, i: (bi, 0, i)),
                  pl.BlockSpec((span, NSA_GROUP * tq), lambda bi, i: (0, 0))],
        out_specs=pl.BlockSpec((1, subs * tq, h * d), lambda bi, i: (bi, i, 0)),
        out_shape=jax.ShapeDtypeStruct((b, s, h * d), _MXU),
        scratch_shapes=[pltpu.VMEM((span, NSA_GROUP * tq), _F32) for _ in range(subs * g)],
        compiler_params=_params("parallel", "arbitrary"),
        name="nsa_window",
    )(q, k, vt, gates_t, _key_minus_query(span, tq))


def _gather_cols(w, idx):
    idx = np.asarray(idx)
    cuts = [0] + [i for i in range(1, len(idx)) if idx[i] != idx[i - 1] + (idx[i - 1] >= 0)] + [len(idx)]
    pieces = []
    for a, b in zip(cuts[:-1], cuts[1:]):
        if idx[a] < 0:
            pieces.append(jnp.zeros(w.shape[:-1] + (b - a,), _MXU))
        else:
            pieces.append(w[..., int(idx[a]):int(idx[a]) + b - a].astype(_MXU))
    return jnp.concatenate(pieces, axis=-1)


def _swap_halves(rot):
    return (np.arange(rot) + rot // 2) % rot


def _w_in_index():
    idx = np.full((_N_HEAD_COLS * HEAD_DIM,), -1, np.int64)

    def put(col, src):
        src = np.asarray(src)
        idx[col:col + len(src)] = src

    def put_head(pos, src):
        put(pos * HEAD_DIM, src)

    put(_S_CQ * LANES, _O_CQ + np.arange(MLA_Q_LORA))
    put(_S_CKV * LANES, _O_CKV + np.arange(MLA_KV_LORA))
    put(_S_KR * LANES + MLA_NOPE, _O_KR + np.arange(MLA_ROPE))
    put(_S_KRS * LANES + MLA_NOPE, _O_KR + _swap_halves(MLA_ROPE))
    for h in range(NSA_HEADS):
        put_head(_H_NQ + h, _O_NQ + h * HEAD_DIM + np.arange(HEAD_DIM))
        put_head(_H_NQS + h, _O_NQ + h * HEAD_DIM + _swap_halves(PARTIAL_ROT))
    for hk, hks, ok in ((_H_KC, _H_KCS, _O_NKC), (_H_KS, _H_KSS, _O_NKS), (_H_KW, _H_KWS, _O_NKW)):
        for g in range(NSA_KV_HEADS):
            put_head(hk + g, ok + g * HEAD_DIM + np.arange(HEAD_DIM))
            put_head(hks + g, ok + g * HEAD_DIM + _swap_halves(PARTIAL_ROT))
    for g in range(NSA_KV_HEADS):
        put_head(_H_VC + g, _O_NVC + g * HEAD_DIM + np.arange(HEAD_DIM))
    for h in range(SB_HEADS):
        put_head(_H_SBQ + h, _O_SBQ + h * HEAD_DIM + np.arange(HEAD_DIM))
        put_head(_H_SBK + h, _O_SBK + h * HEAD_DIM + np.arange(HEAD_DIM))
    return idx


def _mla_up_index():
    qd = MLA_NOPE + MLA_ROPE
    kd = MLA_NOPE + MLA_V
    uq = np.full((MLA_HEADS * LANES,), -1, np.int64)
    uqs = uq.copy()
    uk = uq.copy()
    for h in range(MLA_HEADS):
        uq[h * LANES:h * LANES + qd] = h * qd + np.arange(qd)
        uqs[h * LANES + MLA_NOPE:h * LANES + qd] = h * qd + MLA_NOPE + _swap_halves(MLA_ROPE)
        uk[h * LANES:h * LANES + MLA_NOPE] = h * kd + np.arange(MLA_NOPE)
    return uq, uqs, uk


def _transposed_weights(w_in, gate_bias):
    width = NSA_KV_HEADS * HEAD_DIM
    gate_rows = jnp.pad(w_in[..., _O_GATE:_O_GATE + N_GATES], ((0, 0), (0, 0), (0, _T_SBV - _T_GATE - N_GATES)))
    rows = jnp.concatenate([w_in[..., _O_NVS:_O_NVS + width], w_in[..., _O_NVW:_O_NVW + width], gate_rows,
                            w_in[..., _O_SBV:_O_SBV + SB_HEADS * HEAD_DIM]], axis=-1)
    bias = jnp.pad(gate_bias, ((0, 0), (0, GATE_ROWS - N_GATES)))[..., None]
    return jnp.swapaxes(rows, -1, -2).astype(_MXU), bias


def _rope_tables(s):
    pos = np.arange(s, dtype=np.float64)

    def cs(rot):
        half = rot // 2
        ang = pos[:, None] * (ROPE_THETA ** (-np.arange(half, dtype=np.float64) / half))[None, :]
        c, sn = np.cos(ang), np.sin(ang)
        return np.concatenate([c, c], axis=1), np.concatenate([-sn, sn], axis=1)

    c, sn = cs(MLA_ROPE)
    pad = np.zeros((s, LANES - MLA_NOPE - MLA_ROPE))
    ck = np.concatenate([np.ones((s, MLA_NOPE)), c, pad], axis=1)
    sk = np.concatenate([np.zeros((s, MLA_NOPE)), sn, pad], axis=1)
    q_scale = (MLA_NOPE + MLA_ROPE) ** -0.5 * LOG2_E
    c, sn = cs(PARTIAL_ROT)
    c64 = np.concatenate([c, np.ones((s, HEAD_DIM - PARTIAL_ROT))], axis=1)
    s64 = np.concatenate([sn, np.zeros((s, HEAD_DIM - PARTIAL_ROT))], axis=1)
    ns = s // SEL_LEN
    onehot = (np.arange(s)[:, None] // SEL_LEN == np.arange(ns)[None, :]) * -NEG_INF
    tables = [jnp.asarray(t, _F32) for t in (ck * q_scale, sk * q_scale, ck, sk, c64, s64)]
    return tables + [jnp.asarray(onehot, _MXU)]


def kernel(x, ffn1_norm, ffn1_w_gate, ffn1_w_up, ffn1_w_down, mix_norm, w_in, mla_q_norm, mla_w_uq, mla_kv_norm,
           mla_w_ukv, nsa_gate_bias, nsa_cmp_pos_k, nsa_cmp_w1_k, nsa_cmp_w2_k, nsa_cmp_pos_v, nsa_cmp_w1_v,
           nsa_cmp_w2_v, w_out, ffn2_norm, ffn2_w_gate, ffn2_w_up, ffn2_w_down, final_norm):
    b, s, d = x.shape
    depth = w_in.shape[0]
    tabs = _rope_tables(s)
    in_idx = _w_in_index()
    uq_idx, uqs_idx, uk_idx = _mla_up_index()
    half = CMP_LEN * HEAD_DIM // 2
    fg = final_norm.reshape(1, d)

    def row(p):
        return p[:, None, :]

    def cmp_weights(w1, w2, pos, transpose_out):
        pos = jnp.broadcast_to(pos.reshape(depth, 2, 1, half), (depth, 2, 8, half)).astype(_MXU)
        w2 = jnp.swapaxes(w2, -1, -2) if transpose_out else w2
        return w1.reshape(depth, 2, half, CMP_HIDDEN).astype(_MXU), w2.astype(_MXU), pos

    ffn1 = [w.astype(_MXU) for w in (ffn1_w_gate, ffn1_w_up, ffn1_w_down)]
    ffn2 = [w.astype(_MXU) for w in (ffn2_w_gate, ffn2_w_up, ffn2_w_down)]
    w_t, gate_bias = _transposed_weights(w_in, nsa_gate_bias)
    wuv_t = mla_w_ukv.reshape(depth, MLA_KV_LORA, MLA_HEADS, 2, MLA_V)[:, :, :, 1].transpose(0, 2, 3, 1).astype(_MXU)
    proj_params = (row(mix_norm), _gather_cols(w_in, in_idx), w_t,
                   row(mla_q_norm), _gather_cols(mla_w_uq, uq_idx), _gather_cols(mla_w_uq, uqs_idx),
                   row(mla_kv_norm), _gather_cols(mla_w_ukv, uk_idx), wuv_t, gate_bias)
    cmp_params = (cmp_weights(nsa_cmp_w1_k, nsa_cmp_w2_k, nsa_cmp_pos_k, False)
                  + cmp_weights(nsa_cmp_w1_v, nsa_cmp_w2_v, nsa_cmp_pos_v, True))
    w_out = w_out.astype(_MXU)
    ffn1_norm, ffn2_norm = row(ffn1_norm), row(ffn2_norm)

    for l in range(depth):
        x2d = _ffn(x.reshape(b * s, d), ffn1_norm, *ffn1, l, fg, False)
        x = x2d.reshape(b, s, d)
        (mq, mk, mvt, nq, nkc, nvc, nks, nvst, nkw, nvwt, gates_t, sbq, sbk, sbv) = _proj(x, l, *proj_params, tabs)
        o_mla = _mla_attention(mq, mk, mvt)
        kc, vct = _compress(nkc, nvc, l, *cmp_params)
        o_cmp, q_sel = _cmp_select(nq, kc, vct, gates_t)
        o_sel = _sel_attention(q_sel, nks, nvst, gates_t)
        o_win = _win_attention(nq, nkw, nvwt, gates_t)
        o_sb = _sb_attention(sbq, sbk, sbv)
        heads = [o.reshape(b * s, -1) for o in (o_mla, o_cmp, o_sel, o_win, o_sb)]
        x2d = _ffn(x.reshape(b * s, d), ffn2_norm, *ffn2, l, fg, l == depth - 1, mixer=(*heads, w_out))
        x = x2d.reshape(b, s, d)
    return x
```

```python
import functools
import math

import numpy as np
import jax
import jax.numpy as jnp
from jax import lax
from jax.experimental import pallas as pl
from jax.experimental.pallas import tpu as pltpu

D_MODEL = 1024
HEAD_DIM = 64
MLA_HEADS = 6
MLA_NOPE = 64
MLA_ROPE = 32
MLA_V = 64
MLA_Q_LORA = 256
MLA_KV_LORA = 128
NSA_HEADS = 6
NSA_KV_HEADS = 2
NSA_GROUP = NSA_HEADS // NSA_KV_HEADS
NSA_BRANCHES = 3
CMP_LEN = 32
CMP_STRIDE = 16
CMP_HIDDEN = 128
SEL_LEN = 64
SEL_TOPK = 16
WINDOW = 512
SB_HEADS = 4
D_FF = 2816
ROPE_THETA = 500000.0
PARTIAL_ROT = HEAD_DIM // 4
EPS = 1e-6
NEG_INF = -1e30
M_FLOOR = 0.1 * NEG_INF
N_FORCED = 3
PICKED = -3e38
F32_EXP2_ZERO = -151.0
LOG2_E = math.log2(math.e)
N_GATES = NSA_HEADS * NSA_BRANCHES

LANES = 128
FFN_CHUNK = 256
SWEEP_UNROLL = 4
TOPK_ROW_STEPS = (32, 64)
SB_KEY_BLOCK = 256
PROJ_ROWS = 512
TOKEN_CHUNK = 256
ONES_PAD = 16
VMEM_LIMIT = 56 * 1024 * 1024

_MXU = jnp.bfloat16
_F32 = jnp.float32

_IN_WIDTHS = (MLA_Q_LORA, MLA_KV_LORA, MLA_ROPE, NSA_HEADS * HEAD_DIM) + (NSA_KV_HEADS * HEAD_DIM,) * 6 + (
    N_GATES, SB_HEADS * HEAD_DIM, SB_HEADS * HEAD_DIM, SB_HEADS * HEAD_DIM)
_IN_OFF = np.concatenate([[0], np.cumsum(_IN_WIDTHS)])
(_O_CQ, _O_CKV, _O_KR, _O_NQ, _O_NKC, _O_NVC, _O_NKS, _O_NVS, _O_NKW, _O_NVW, _O_GATE, _O_SBQ, _O_SBK,
 _O_SBV) = [int(v) for v in _IN_OFF[:-1]]

_S_CQ, _S_CKV, _S_KR, _S_KRS = 0, 2, 3, 4
_H_NQ, _H_NQS = 10, 16
_H_KC, _H_KCS, _H_VC = 22, 24, 26
_H_KS, _H_KSS = 28, 30
_H_KW, _H_KWS = 32, 34
_H_SBQ, _H_SBK = 36, 40
_N_HEAD_COLS = 44
_T_VS, _T_VW, _T_GATE = 0, NSA_KV_HEADS * HEAD_DIM, 2 * NSA_KV_HEADS * HEAD_DIM
GATE_ROWS = 24
_T_SBV = _T_GATE + 2 * ONES_PAD


def _dot(a, b):
    return jnp.dot(a.astype(_MXU), b.astype(_MXU), preferred_element_type=_F32)


def _dot_nt(a, b):
    return lax.dot_general(a.astype(_MXU), b.astype(_MXU), (((1,), (1,)), ((), ())),
                           preferred_element_type=_F32)


def _dot_split_rhs(a, b):
    hi = b.astype(_MXU)
    lo = (b - hi.astype(_F32)).astype(_MXU)
    return (jnp.dot(a, hi, preferred_element_type=_F32) + jnp.dot(a, lo, preferred_element_type=_F32))


def _rms(x, g):
    return x * lax.rsqrt(jnp.mean(x * x, axis=-1, keepdims=True) + EPS) * g


def _params(*sem):
    return pltpu.CompilerParams(dimension_semantics=sem, vmem_limit_bytes=VMEM_LIMIT)


def _layer_spec(a, layer):
    return pl.BlockSpec((None,) + a.shape[1:], lambda *_: (layer,) + (0,) * (a.ndim - 1))


def _mixer_output(mla_ref, cmp_ref, sel_ref, win_ref, sb_ref, w_ref):
    def w_rows(first_head, n_heads):
        return w_ref[first_head * HEAD_DIM:(first_head + n_heads) * HEAD_DIM, :]

    nsa = cmp_ref[...].astype(_F32) + sel_ref[...].astype(_F32) + win_ref[...].astype(_F32)
    return (jnp.dot(mla_ref[...], w_rows(0, MLA_HEADS), preferred_element_type=_F32)
            + _dot(nsa, w_rows(MLA_HEADS, NSA_HEADS))
            + jnp.dot(sb_ref[...], w_rows(MLA_HEADS + NSA_HEADS, SB_HEADS), preferred_element_type=_F32))


def _ffn_kernel(x_ref, g_ref, wg_ref, wu_ref, wd_ref, fg_ref, *rest, final_norm, with_mixer):
    *mixer_refs, o_ref, act_ref = rest
    x = x_ref[...]
    if with_mixer:
        x = x + _mixer_output(*mixer_refs)
    h = _rms(x, g_ref[...]).astype(_MXU)
    tf = act_ref.shape[1]
    for c0 in range(0, tf, FFN_CHUNK):
        c1 = min(c0 + FFN_CHUNK, tf)
        gate = jnp.dot(h, wg_ref[:, c0:c1], preferred_element_type=_F32)
        up = jnp.dot(h, wu_ref[:, c0:c1], preferred_element_type=_F32)
        act_ref[:, c0:c1] = (gate * jax.nn.sigmoid(gate) * up).astype(act_ref.dtype)
    y = x + 0.5 * jnp.dot(act_ref[...], wd_ref[...], preferred_element_type=_F32)
    if final_norm:
        y = _rms(y, fg_ref[...])
    o_ref[...] = y


def _ffn(x2d, g, wg, wu, wd, layer, fg, final_norm, mixer=None):
    rows = x2d.shape[0]
    tm = min(1024, rows)

    def resident(a):
        return pl.BlockSpec((None,) + a.shape[1:], lambda i: (layer, 0, 0), pipeline_mode=pl.Buffered(1))

    def row_tile(a):
        return pl.BlockSpec((tm, a.shape[1]), lambda i: (i, 0))

    mixer_args, mixer_specs = (), []
    if mixer is not None:
        *heads, w_out = mixer
        mixer_args = (*heads, w_out)
        mixer_specs = [row_tile(a) for a in heads] + [resident(w_out)]
    return pl.pallas_call(
        functools.partial(_ffn_kernel, final_norm=final_norm, with_mixer=mixer is not None),
        grid=(rows // tm,),
        in_specs=[row_tile(x2d), _layer_spec(g, layer), resident(wg), resident(wu), resident(wd),
                  pl.BlockSpec((1, D_MODEL), lambda i: (0, 0))] + mixer_specs,
        out_specs=row_tile(x2d),
        out_shape=jax.ShapeDtypeStruct((rows, D_MODEL), _F32),
        scratch_shapes=[pltpu.VMEM((tm, D_FF), _MXU)],
        compiler_params=_params("parallel"),
        name="ffn",
    )(x2d, g, wg, wu, wd, fg, *mixer_args)


def _proj_kernel(x_ref, g_ref, w_ref, wt_ref, qn_ref, wuq_ref, wuqs_ref, kvn_ref, wuk_ref, wuv_ref, gb_ref,
                 cq_ref, sq_ref, ck_ref, sk_ref, c64_ref, s64_ref, oh_ref,
                 mq_ref, mk_ref, mv_ref, nq_ref, nkc_ref, nvc_ref, nks_ref, nvs_ref, nkw_ref, nvw_ref,
                 gate_ref, sbq_ref, sbk_ref, sbv_ref, stage_ref):
    hn = _rms(x_ref[0], g_ref[...]).astype(_MXU)

    def proj(h0, h1):
        return jnp.dot(hn, w_ref[:, h0 * HEAD_DIM:h1 * HEAD_DIM], preferred_element_type=_F32)

    def slot(p, s):
        return p[:, s * LANES:(s + 1) * LANES]

    def head(p, i):
        return p[:, i * HEAD_DIM:(i + 1) * HEAD_DIM]

    def write_value_chunks(o_ref, h, vt):
        width = o_ref.shape[-1]
        for c in range(o_ref.shape[2]):
            o_ref[0, h, c] = vt[:, c * width:(c + 1) * width].astype(o_ref.dtype)

    p = proj(0, _H_NQ)
    cq = _rms(p[:, :MLA_Q_LORA], qn_ref[...])
    ckv = _rms(slot(p, _S_CKV), kvn_ref[...])
    q = _dot(cq, wuq_ref[...])
    q_partner = _dot(cq, wuqs_ref[...])
    kpe = slot(p, _S_KR) * ck_ref[...] + slot(p, _S_KRS) * sk_ref[...]
    kn = _dot(ckv, wuk_ref[...])
    for h in range(MLA_HEADS):
        mq_ref[0, h] = (slot(q, h) * cq_ref[...] + slot(q_partner, h) * sq_ref[...]).astype(mq_ref.dtype)
        mk_ref[0, h] = (slot(kn, h) + kpe).astype(mk_ref.dtype)
        write_value_chunks(mv_ref, h, _ones_row_pad(_dot_nt(wuv_ref[h], ckv)))

    c64 = c64_ref[...]
    s64 = s64_ref[...]
    scale = HEAD_DIM ** -0.5

    p = proj(_H_NQ, _H_KC)
    for h in range(NSA_HEADS):
        nq_ref[0, h] = ((head(p, h) * c64 + head(p, NSA_HEADS + h) * s64) * (scale * LOG2_E)).astype(nq_ref.dtype)

    p = proj(_H_KC, _H_SBQ)
    base = _H_KC
    ns = oh_ref.shape[-1]

    def roped(hk, hks, g):
        return head(p, hk - base + g) * c64 + head(p, hks - base + g) * s64

    def write_chunked(o_ref, g, val):
        stage_ref[...] = val
        for t in range(CMP_STRIDE):
            piece = stage_ref[pl.ds(t, val.shape[0] // CMP_STRIDE, stride=CMP_STRIDE), :]
            o_ref[0, g, :, t * HEAD_DIM:(t + 1) * HEAD_DIM] = piece.astype(o_ref.dtype)

    for g in range(NSA_KV_HEADS):
        write_chunked(nkc_ref, g, roped(_H_KC, _H_KCS, g))
        write_chunked(nvc_ref, g, head(p, _H_VC - base + g))
        nks_ref[0, g, :, 0:ns] = oh_ref[...]
        nks_ref[0, g, :, ns:ns + HEAD_DIM] = roped(_H_KS, _H_KSS, g).astype(nks_ref.dtype)
        nkw_ref[0, g] = roped(_H_KW, _H_KWS, g).astype(nkw_ref.dtype)

    pt = _dot_nt(wt_ref[...], hn)
    for g in range(NSA_KV_HEADS):
        lo = g * HEAD_DIM
        write_value_chunks(nvs_ref, g, _ones_row_pad(pt[_T_VS + lo:_T_VS + lo + HEAD_DIM]))
        write_value_chunks(nvw_ref, g, _ones_row_pad(pt[_T_VW + lo:_T_VW + lo + HEAD_DIM]))
    gate_ref[0] = jax.nn.sigmoid(pt[_T_GATE:_T_GATE + GATE_ROWS] + gb_ref[...])

    p = proj(_H_SBQ, _N_HEAD_COLS)
    for h in range(SB_HEADS):
        sbq_ref[0, h] = (head(p, h) * (scale * LOG2_E)).astype(sbq_ref.dtype)
        sbk_ref[0, h] = head(p, SB_HEADS + h).astype(sbk_ref.dtype)
        write_value_chunks(sbv_ref, h, pt[_T_SBV + h * HEAD_DIM:_T_SBV + (h + 1) * HEAD_DIM])


def _proj(x, layer, g, w_ext, w_t, qn, wuq, wuqs, kvn, wuk, wuv, gb, tabs):
    b, s, _ = x.shape
    ts = min(PROJ_ROWS, s)
    tv = min(TOKEN_CHUNK, ts)
    cq, sq, ck, sk, c64, s64, onehot = tabs
    full = functools.partial(_layer_spec, layer=layer)

    def tab(a):
        return pl.BlockSpec((ts, a.shape[1]), lambda bi, i: (i, 0))

    def heads(n, d):
        return (pl.BlockSpec((1, n, ts, d), lambda bi, i: (bi, 0, i, 0)),
                jax.ShapeDtypeStruct((b, n, s, d), _MXU))

    def values_t(n):
        return (pl.BlockSpec((1, n, ts // tv, HEAD_DIM + ONES_PAD, tv), lambda bi, i: (bi, 0, i, 0, 0)),
                jax.ShapeDtypeStruct((b, n, s // tv, HEAD_DIM + ONES_PAD, tv), _MXU))

    def chunked():
        return (pl.BlockSpec((1, NSA_KV_HEADS, ts // CMP_STRIDE, CMP_STRIDE * HEAD_DIM), lambda bi, i: (bi, 0, i, 0)),
                jax.ShapeDtypeStruct((b, NSA_KV_HEADS, s // CMP_STRIDE, CMP_STRIDE * HEAD_DIM), _MXU))

    outs = [heads(MLA_HEADS, LANES), heads(MLA_HEADS, LANES), values_t(MLA_HEADS), heads(NSA_HEADS, HEAD_DIM),
            chunked(), chunked(), heads(NSA_KV_HEADS, onehot.shape[1] + HEAD_DIM), values_t(NSA_KV_HEADS),
            heads(NSA_KV_HEADS, HEAD_DIM), values_t(NSA_KV_HEADS),
            (pl.BlockSpec((1, GATE_ROWS, ts), lambda bi, i: (bi, 0, i)), jax.ShapeDtypeStruct((b, GATE_ROWS, s), _F32)),
            heads(SB_HEADS, HEAD_DIM), heads(SB_HEADS, HEAD_DIM),
            (pl.BlockSpec((1, SB_HEADS, ts // SB_KEY_BLOCK, HEAD_DIM, SB_KEY_BLOCK), lambda bi, i: (bi, 0, i, 0, 0)),
             jax.ShapeDtypeStruct((b, SB_HEADS, s // SB_KEY_BLOCK, HEAD_DIM, SB_KEY_BLOCK), _MXU))]
    return pl.pallas_call(
        _proj_kernel,
        grid=(b, s // ts),
        in_specs=[pl.BlockSpec((1, ts, D_MODEL), lambda bi, i: (bi, i, 0)), full(g), full(w_ext), full(w_t), full(qn),
                  full(wuq), full(wuqs), full(kvn), full(wuk), full(wuv), full(gb),
                  tab(cq), tab(sq), tab(ck), tab(sk), tab(c64), tab(s64), tab(onehot)],
        out_specs=[o[0] for o in outs],
        out_shape=[o[1] for o in outs],
        scratch_shapes=[pltpu.VMEM((ts, HEAD_DIM), _F32)],
        compiler_params=_params("parallel", "parallel"),
        name="proj",
    )(x, g, w_ext, w_t, qn, wuq, wuqs, kvn, wuk, wuv, gb, cq, sq, ck, sk, c64, s64, onehot)


def _ones_row_pad(vt):
    first = lax.broadcasted_iota(jnp.int32, (ONES_PAD, vt.shape[1]), 0) == 0
    return jnp.concatenate([vt, jnp.where(first, 1.0, 0.0).astype(vt.dtype)], axis=0)


def _softmax_step_t(carry, st, vt_chunks):
    m, acc = carry
    m_new = jnp.maximum(m, jnp.max(st, axis=0, keepdims=True))
    alpha = jnp.exp2(m - m_new)
    n = st.shape[0] // len(vt_chunks)
    pv = None
    for c, vt in enumerate(vt_chunks):
        pt = jnp.exp2(st[c * n:(c + 1) * n] - m_new).astype(_MXU)
        part = jnp.dot(vt, pt, preferred_element_type=_F32)
        pv = part if pv is None else pv + part
    return m_new, alpha * acc + pv


def _softmax_init_t(d, cols):
    return (jnp.full((1, cols), M_FLOOR, _F32), jnp.zeros((d + ONES_PAD, cols), _F32))


def _softmax_finish_t(carry, d):
    _, acc = carry
    return acc[:d] * (1.0 / acc[d:d + 1])


def _two_chain_sweep(n_full, qk, soft, init):
    def body(j, carry, diag=False):
        c0, c1 = carry
        qk(0, j)
        c1 = soft(1, j, c1, diag)
        qk(1, jnp.zeros_like(j) if diag else j + 1)
        c0 = soft(0, j, c0, diag)
        return c0, c1

    qk(1, n_full)
    carry = body(n_full, init, True)
    done = jnp.zeros_like(n_full)
    unroll = SWEEP_UNROLL
    while unroll >= 1:
        def unrolled(i, carry, first=done, unroll=unroll):
            for u in range(unroll):
                carry = body(first + unroll * i + u, carry)
            return carry

        trips = (n_full - done) // unroll
        carry = lax.fori_loop(0, trips, unrolled, carry)
        done = done + trips * unroll
        unroll //= 2
    return carry


def _mla_kernel(q_ref, k_ref, vt_ref, o_ref, s0_ref, s1_ref, *, t, nsub):
    qi = pl.program_id(2)
    s_refs = (s0_ref, s1_ref)

    def qk(hh, j):
        off = pl.multiple_of(j * t, t)
        s_refs[hh][...] = _dot_nt(k_ref[0, hh, pl.ds(off, t), :], q_ref[0, hh])

    def soft(hh, j, carry, diag):
        st = s_refs[hh][...]
        if diag:
            key = lax.broadcasted_iota(jnp.int32, (t, t), 0)
            qry = lax.broadcasted_iota(jnp.int32, (t, t), 1)
            st = jnp.where(key <= qry, st, NEG_INF)
        return _softmax_step_t(carry, st, [vt_ref[0, hh, j * nsub + c] for c in range(nsub)])

    carry = _two_chain_sweep(qi, qk, soft, tuple(_softmax_init_t(MLA_V, t) for _ in range(2)))
    ot = jnp.concatenate([_softmax_finish_t(c, MLA_V) for c in carry], axis=0)
    o_ref[0] = ot.T.astype(o_ref.dtype)


def _mla_attention(q, k, vt):
    b, h, s, _ = q.shape
    tv = vt.shape[-1]
    dv = vt.shape[-2]
    t = min(512, s)
    assert h % 2 == 0 and 2 * MLA_V == LANES and t % tv == 0 and s % t == 0
    return pl.pallas_call(
        functools.partial(_mla_kernel, t=t, nsub=t // tv),
        grid=(b, h // 2, s // t),
        in_specs=[pl.BlockSpec((1, 2, t, LANES), lambda bi, hi, i: (bi, hi, i, 0)),
                  pl.BlockSpec((1, 2, s, LANES), lambda bi, hi, i: (bi, hi, 0, 0)),
                  pl.BlockSpec((1, 2, s // tv, dv, tv), lambda bi, hi, i: (bi, hi, 0, 0, 0))],
        out_specs=pl.BlockSpec((1, t, LANES), lambda bi, hi, i: (bi, i, hi)),
        out_shape=jax.ShapeDtypeStruct((b, s, h * MLA_V), _MXU),
        scratch_shapes=[pltpu.VMEM((t, t), _F32), pltpu.VMEM((t, t), _F32)],
        compiler_params=_params("parallel", "parallel", "arbitrary"),
        name="mla_attn",
    )(q, k, vt)


def _sb_kernel(q_ref, k_ref, vt_ref, u_ref, o_ref, *scratch_refs, tq, tk):
    qi = pl.program_id(1)
    u = u_ref[...]
    n_heads = q_ref.shape[1]
    scratch = [scratch_refs[6 * hh:6 * (hh + 1)] for hh in range(n_heads)]
    per_tile = tq // tk

    def step(j, carry, key_offset=None):
        diag = key_offset is not None
        off = pl.multiple_of(j * tk, tk)
        if diag:
            key = key_offset + lax.broadcasted_iota(jnp.int32, (tk, tq), 0)
            qry = lax.broadcasted_iota(jnp.int32, (tk, tq), 1)
            strict = key < qry

        def logits(hh):
            z_ref, _, _, _, _, _ = scratch[hh]
            z_ref[...] = _dot_nt(k_ref[0, hh, pl.ds(off, tk), :], q_ref[0, hh])

        def log_terms(hh):
            z_ref, lb_ref, hi_ref, lo_ref, _, _ = scratch[hh]
            z = z_ref[...]
            log_beta = jnp.minimum(z, 0.0) - jnp.log2(1.0 + jnp.exp2(-jnp.abs(z)))
            log_rem = log_beta - z
            if diag:
                log_rem = jnp.where(strict, log_rem, 0.0)
            hi = log_rem.astype(_MXU)
            lb_ref[...] = log_beta
            hi_ref[...] = hi
            lo_ref[...] = (log_rem - hi.astype(_F32)).astype(_MXU)
            return log_rem[0:1, :]

        def suffix_sums(hh):
            _, _, hi_ref, lo_ref, sfx_ref, _ = scratch[hh]
            sfx_ref[...] = (jnp.dot(u, hi_ref[...], preferred_element_type=_F32)
                            + jnp.dot(u, lo_ref[...], preferred_element_type=_F32))

        def weights(hh, first_rem):
            _, lb_ref, _, _, sfx_ref, a_ref = scratch[hh]
            rem = carry[hh][0]
            suffix = sfx_ref[...]
            a = jnp.exp2(lb_ref[...] + suffix + rem)
            if diag:
                a = jnp.where(strict, a, 0.0)
            a_ref[...] = a.astype(_MXU)
            return rem + suffix[0:1, :] + first_rem

        def values(hh):
            a_ref = scratch[hh][5]
            return carry[hh][1] + jnp.dot(vt_ref[0, hh, j], a_ref[...], preferred_element_type=_F32)

        heads = range(n_heads)
        for hh in heads:
            logits(hh)
        first = []
        for hh in heads:
            first.append(log_terms(hh))
            suffix_sums(hh)
        rems = [weights(hh, first[hh]) for hh in heads]
        alive = jnp.max(functools.reduce(jnp.maximum, rems)) > F32_EXP2_ZERO
        return alive, tuple((rems[hh], values(hh)) for hh in heads)

    carry = tuple((jnp.zeros((1, tq), _F32), jnp.zeros((HEAD_DIM, tq), _F32)) for _ in range(n_heads))
    first = qi * per_tile
    for i in reversed(range(per_tile)):
        _, carry = step(first + i, carry, key_offset=i * tk)
    has_past = first > 0
    carry = tuple((jnp.where(has_past, rem, NEG_INF), acc) for rem, acc in carry)
    alive, carry = step(jnp.maximum(first - 1, 0), carry)

    def earlier(c):
        return (c[0] - 1,) + step(c[0], c[2])

    _, _, carry = lax.while_loop(lambda c: jnp.logical_and(c[0] >= 0, c[1]), earlier, (first - 2, alive, carry))
    o_ref[0] = jnp.concatenate([acc for _, acc in carry], axis=0).T.astype(o_ref.dtype)


def _sb_attention(q, k, vt):
    b, h, s, d = q.shape
    tk = vt.shape[-1]
    tq = tk
    assert (h * d) % LANES == 0 and s % tq == 0 and tq % tk == 0
    idx = np.arange(tk)
    u = jnp.asarray(idx[None, :] > idx[:, None], _MXU)
    return pl.pallas_call(
        functools.partial(_sb_kernel, tq=tq, tk=tk),
        grid=(b, s // tq),
        in_specs=[pl.BlockSpec((1, h, tq, d), lambda bi, i: (bi, 0, i, 0)),
                  pl.BlockSpec((1, h, s, d), lambda bi, i: (bi, 0, 0, 0)),
                  pl.BlockSpec((1, h, s // tk, d, tk), lambda bi, i: (bi, 0, 0, 0, 0)),
                  pl.BlockSpec((tk, tk), lambda bi, i: (0, 0))],
        out_specs=pl.BlockSpec((1, tq, h * d), lambda bi, i: (bi, i, 0)),
        out_shape=jax.ShapeDtypeStruct((b, s, h * d), _MXU),
        scratch_shapes=[pltpu.VMEM((tk, tq), dt) for _ in range(h) for dt in (_F32, _F32, _MXU, _MXU, _F32, _MXU)],
        compiler_params=_params("parallel", "arbitrary"),
        name="sb_attn",
    )(q, k, vt, u)


def _compress_kernel(xk_ref, xv_ref, w1k_ref, w2k_ref, pk_ref, w1v_ref, w2v_ref, pv_ref, ok_ref, ov_ref):
    def hidden(x_ref, w1_ref, p_ref):
        x = x_ref[0, 0]
        n = x.shape[0]
        first = jnp.dot(x, w1_ref[0], preferred_element_type=_F32)
        second = jnp.dot(x, w1_ref[1], preferred_element_type=_F32)
        pos = _dot(p_ref[0], w1_ref[0]) + _dot(p_ref[1], w1_ref[1])
        hid = first + pltpu.roll(second, n - 1, 0) + pos[0:1]
        return 0.5 * hid * (1.0 + jnp.tanh(math.sqrt(2.0 / math.pi) * (hid + 0.044715 * hid * hid * hid)))

    ok_ref[0, 0] = _dot(hidden(xk_ref, w1k_ref, pk_ref), w2k_ref[...]).astype(ok_ref.dtype)
    ov_ref[0, 0] = _dot_nt(w2v_ref[...], hidden(xv_ref, w1v_ref, pv_ref)).astype(ov_ref.dtype)


def _compress(xk, xv, layer, w1k, w2k, pk, w1v, w2v, pv):
    b, g, n, _ = xk.shape
    d = HEAD_DIM
    full = functools.partial(_layer_spec, layer=layer)

    xspec = pl.BlockSpec((1, 1, n, CMP_STRIDE * d), lambda bi, gi: (bi, gi, 0, 0))
    return pl.pallas_call(
        _compress_kernel,
        grid=(b, g),
        in_specs=[xspec, xspec, full(w1k), full(w2k), full(pk), full(w1v), full(w2v), full(pv)],
        out_specs=[pl.BlockSpec((1, 1, n, d), lambda bi, gi: (bi, gi, 0, 0)),
                   pl.BlockSpec((1, 1, d, n), lambda bi, gi: (bi, gi, 0, 0))],
        out_shape=[jax.ShapeDtypeStruct((b, g, n, d), _MXU), jax.ShapeDtypeStruct((b, g, d, n), _MXU)],
        compiler_params=_params("parallel", "parallel"),
        name="nsa_compress",
    )(xk, xv, w1k, w2k, pk, w1v, w2v, pv)


def _group_queries(q_ref, g, tq):
    return q_ref[0, g * NSA_GROUP:(g + 1) * NSA_GROUP].reshape(NSA_GROUP * tq, q_ref.shape[-1])


def _gated_heads(ot, gt_ref, g, branch, tq):
    out = []
    for r in range(NSA_GROUP):
        row = NSA_BRANCHES * (g * NSA_GROUP + r) + branch
        out.append(ot[:, r * tq:(r + 1) * tq] * gt_ref[0, row:row + 1, :])
    return out


def _cmp_kernel(q_ref, kc_ref, vct_ref, ov_ref, gt_ref, o_ref, qa_ref, s0_ref, s1_ref, *, tq, n_top):
    q0 = pl.program_id(1) * tq
    ncp = kc_ref.shape[2]
    ns = ov_ref.shape[0]
    lanes = NSA_GROUP * tq
    s_refs = (s0_ref, s1_ref)
    for g in range(NSA_KV_HEADS):
        s_refs[g][...] = _dot_nt(kc_ref[0, g], _group_queries(q_ref, g, tq))
    qpos = q0 + (lax.broadcasted_iota(jnp.int32, (1, lanes), 1) & (tq - 1))
    cmp_end = lax.broadcasted_iota(jnp.int32, (ncp, 1), 0) * CMP_STRIDE + (CMP_LEN - 1)
    visible = cmp_end <= qpos
    cur = jnp.right_shift(q0 + lax.broadcasted_iota(jnp.int32, (1, tq), 1), int(math.log2(SEL_LEN)))
    blk = lax.broadcasted_iota(jnp.int32, (ns, 1), 0)
    forced = (blk == 0) | (blk == cur) | (blk == cur - 1)
    future = blk > cur
    blk_f = blk.astype(_F32)
    heads = []
    scores = []
    for g in range(NSA_KV_HEADS):
        st = jnp.where(visible, s_refs[g][...], NEG_INF)
        e = jnp.exp2(st - jnp.max(st, axis=0, keepdims=True))
        inv = jnp.where(qpos >= CMP_LEN - 1, 1.0 / jnp.sum(e, axis=0, keepdims=True), 0.0)
        pt = e * inv
        heads += _gated_heads(_dot(vct_ref[0, g], pt), gt_ref, g, 0, tq)
        p_sum = sum(pt[:, r * tq:(r + 1) * tq] for r in range(NSA_GROUP))
        score = _dot_split_rhs(ov_ref[...], p_sum)
        scores.append(jnp.where(forced, PICKED, jnp.where(future, -1.0, score)))
    o_ref[0] = jnp.concatenate(heads, axis=0).T.astype(o_ref.dtype)
    def select(rows):
        sc = [s[:rows] for s in scores]
        idx = blk_f[:rows]
        for _ in range(n_top - N_FORCED):
            for g in range(NSA_KV_HEADS):
                top = jnp.max(sc[g], axis=0, keepdims=True)
                first = jnp.min(jnp.where(sc[g] == top, idx, float(ns)), axis=0, keepdims=True)
                sc[g] = jnp.where(idx == first, PICKED, sc[g])
        for g in range(NSA_KV_HEADS):
            sel_m1 = jnp.where(sc[g] < 0.5 * PICKED, 0.0, -1.0)
            if rows < ns:
                sel_m1 = jnp.concatenate([sel_m1, jnp.full((ns - rows, tq), -1.0, _F32)], axis=0)
            sel_m1 = sel_m1.T.astype(qa_ref.dtype)
            for h in range(g * NSA_GROUP, (g + 1) * NSA_GROUP):
                qa_ref[0, h, :, 0:ns] = sel_m1
                qa_ref[0, h, :, ns:ns + HEAD_DIM] = q_ref[0, h]

    visible_blocks = (q0 + tq) // SEL_LEN
    lower = 0
    for rows in sorted({min(ns, r) for r in TOPK_ROW_STEPS} | {ns}):
        in_range = visible_blocks > lower
        if rows < ns:
            in_range = jnp.logical_and(in_range, visible_blocks <= rows)
        pl.when(in_range)(functools.partial(select, rows))
        lower = rows


def _cmp_select(q, kc, vct, gates_t):
    b, h, s, d = q.shape
    g = kc.shape[1]
    ncp = kc.shape[2]
    ns = s // SEL_LEN
    n_top = min(SEL_TOPK, ns)
    tq = min(256, s)
    assert tq & (tq - 1) == 0 and g == 2 and n_top >= N_FORCED
    c0 = np.arange(ncp)[:, None] * CMP_STRIDE
    n0 = np.arange(ns)[None, :] * SEL_LEN
    overlap = jnp.asarray(((c0 < n0 + SEL_LEN) & (c0 + CMP_LEN > n0)).T, _MXU)
    return pl.pallas_call(
        functools.partial(_cmp_kernel, tq=tq, n_top=n_top),
        grid=(b, s // tq),
        in_specs=[pl.BlockSpec((1, h, tq, d), lambda bi, i: (bi, 0, i, 0)),
                  pl.BlockSpec((1, g, ncp, d), lambda bi, i: (bi, 0, 0, 0)),
                  pl.BlockSpec((1, g, d, ncp), lambda bi, i: (bi, 0, 0, 0)),
                  pl.BlockSpec((ns, ncp), lambda bi, i: (0, 0)),
                  pl.BlockSpec((1, GATE_ROWS, tq), lambda bi, i: (bi, 0, i))],
        out_specs=[pl.BlockSpec((1, tq, h * d), lambda bi, i: (bi, i, 0)),
                   pl.BlockSpec((1, h, tq, ns + d), lambda bi, i: (bi, 0, i, 0))],
        out_shape=[jax.ShapeDtypeStruct((b, s, h * d), _MXU), jax.ShapeDtypeStruct((b, h, s, ns + d), _MXU)],
        scratch_shapes=[pltpu.VMEM((ncp, NSA_GROUP * tq), _F32) for _ in range(g)],
        compiler_params=_params("parallel", "arbitrary"),
        name="nsa_cmp_select",
    )(q, kc, vct, overlap, gates_t)


def _key_minus_query(keys, tq):
    return jnp.asarray(np.arange(keys)[:, None] - np.arange(NSA_GROUP * tq)[None, :] % tq, jnp.int32)


def _sel_kernel(q_ref, k_ref, vt_ref, gt_ref, rel_ref, o_ref, s0_ref, s1_ref, *, tq, tk, nsub):
    q0 = pl.program_id(1) * tq
    last = (q0 + tq - 1) // tk
    lanes = NSA_GROUP * tq
    s_refs = (s0_ref, s1_ref)

    def qk(g, j):
        off = pl.multiple_of(j * tk, tk)
        s_refs[g][...] = _dot_nt(k_ref[0, g, pl.ds(off, tk), :], _group_queries(q_ref, g, tq))

    def soft(g, j, carry, causal):
        st = s_refs[g][...]
        if causal:
            st = jnp.where(rel_ref[...] <= q0 - j * tk, st, NEG_INF)
        return _softmax_step_t(carry, st, [vt_ref[0, g, j * nsub + c] for c in range(nsub)])

    init = tuple(_softmax_init_t(HEAD_DIM, lanes) for _ in range(NSA_KV_HEADS))
    carry = _two_chain_sweep(last, qk, soft, init)
    heads = []
    for g in range(NSA_KV_HEADS):
        heads += _gated_heads(_softmax_finish_t(carry[g], HEAD_DIM), gt_ref, g, 1, tq)
    o_ref[0] = jnp.concatenate(heads, axis=0).T.astype(o_ref.dtype)


def _sel_attention(q, k, vt, gates_t):
    b, h, s, da = q.shape
    g = k.shape[1]
    d = HEAD_DIM
    tv = vt.shape[-1]
    tq = min(256, s)
    tk = min(512, s)
    assert tq & (tq - 1) == 0 and s % tk == 0 and tk % tv == 0 and g == 2
    return pl.pallas_call(
        functools.partial(_sel_kernel, tq=tq, tk=tk, nsub=tk // tv),
        grid=(b, s // tq),
        in_specs=[pl.BlockSpec((1, h, tq, da), lambda bi, i: (bi, 0, i, 0)),
                  pl.BlockSpec((1, g, s, da), lambda bi, i: (bi, 0, 0, 0)),
                  pl.BlockSpec((1, g) + vt.shape[2:], lambda bi, i: (bi, 0, 0, 0, 0)),
                  pl.BlockSpec((1, GATE_ROWS, tq), lambda bi, i: (bi, 0, i)),
                  pl.BlockSpec((tk, NSA_GROUP * tq), lambda bi, i: (0, 0))],
        out_specs=pl.BlockSpec((1, tq, h * d), lambda bi, i: (bi, i, 0)),
        out_shape=jax.ShapeDtypeStruct((b, s, h * d), _MXU),
        scratch_shapes=[pltpu.VMEM((tk, NSA_GROUP * tq), _F32) for _ in range(g)],
        compiler_params=_params("parallel", "arbitrary"),
        name="nsa_selected",
    )(q, k, vt, gates_t, _key_minus_query(tk, tq))


def _win_kernel(q_ref, k_ref, vt_ref, gt_ref, rel_ref, o_ref, *s_refs, tq, subs, span, tv):
    lanes = NSA_GROUP * tq
    rel = rel_ref[...]
    chains = [(sub, g) for sub in range(subs) for g in range(NSA_KV_HEADS)]

    def origin(sub):
        q0 = (pl.program_id(1) * subs + sub) * tq
        return q0, pl.multiple_of(jnp.maximum(q0 - WINDOW, 0), tq)

    for c, (sub, g) in enumerate(chains):
        _, start = origin(sub)
        q = q_ref[0, g * NSA_GROUP:(g + 1) * NSA_GROUP, sub * tq:(sub + 1) * tq].reshape(lanes, q_ref.shape[-1])
        s_refs[c][...] = _dot_nt(k_ref[0, g, pl.ds(start, span), :], q)
    for sub in range(subs):
        q0, start = origin(sub)
        offset = q0 - start
        heads = []
        for g in range(NSA_KV_HEADS):
            st = jnp.where(rel <= offset, s_refs[chains.index((sub, g))][...], NEG_INF)
            st = jnp.where(rel > offset - WINDOW, st, NEG_INF)
            carry = _softmax_step_t(_softmax_init_t(HEAD_DIM, lanes), st,
                                    [vt_ref[0, g, start // tv + c] for c in range(span // tv)])
            ot = _softmax_finish_t(carry, HEAD_DIM)
            for r in range(NSA_GROUP):
                row = NSA_BRANCHES * (g * NSA_GROUP + r) + 2
                heads.append(ot[:, r * tq:(r + 1) * tq] * gt_ref[0, row:row + 1, sub * tq:(sub + 1) * tq])
        o_ref[0, sub * tq:(sub + 1) * tq, :] = jnp.concatenate(heads, axis=0).T.astype(o_ref.dtype)


def _win_attention(q, k, vt, gates_t):
    b, h, s, d = q.shape
    g = k.shape[1]
    tv = vt.shape[-1]
    tq = min(256, s)
    subs = 2 if s % (2 * tq) == 0 else 1
    span = WINDOW + tq
    assert tq & (tq - 1) == 0 and s >= span and tq % tv == 0 and WINDOW % tv == 0 and g == NSA_KV_HEADS
    return pl.pallas_call(
        functools.partial(_win_kernel, tq=tq, subs=subs, span=span, tv=tv),
        grid=(b, s // (subs * tq)),
        in_specs=[pl.BlockSpec((1, h, subs * tq, d), lambda bi, i: (bi, 0, i, 0)),
                  pl.BlockSpec((1, g, s, d), lambda bi, i: (bi, 0, 0, 0)),
                  pl.BlockSpec((1, g) + vt.shape[2:], lambda bi, i: (bi, 0, 0, 0, 0)),
                  pl.BlockSpec((1, GATE_ROWS, subs * tq), lambda bi, i: (bi, 0, i)),
                  pl.BlockSpec((span, NSA_GROUP * tq), lambda bi, i: (0, 0))],
        out_specs=pl.BlockSpec((1, subs * tq, h * d), lambda bi, i: (bi, i, 0)),
        out_shape=jax.ShapeDtypeStruct((b, s, h * d), _MXU),
        scratch_shapes=[pltpu.VMEM((span, NSA_GROUP * tq), _F32) for _ in range(subs * g)],
        compiler_params=_params("parallel", "arbitrary"),
        name="nsa_window",
    )(q, k, vt, gates_t, _key_minus_query(span, tq))


def _gather_cols(w, idx):
    idx = np.asarray(idx)
    cuts = [0] + [i for i in range(1, len(idx)) if idx[i] != idx[i - 1] + (idx[i - 1] >= 0)] + [len(idx)]
    pieces = []
    for a, b in zip(cuts[:-1], cuts[1:]):
        if idx[a] < 0:
            pieces.append(jnp.zeros(w.shape[:-1] + (b - a,), _MXU))
        else:
            pieces.append(w[..., int(idx[a]):int(idx[a]) + b - a].astype(_MXU))
    return jnp.concatenate(pieces, axis=-1)


def _swap_halves(rot):
    return (np.arange(rot) + rot // 2) % rot


def _w_in_index():
    idx = np.full((_N_HEAD_COLS * HEAD_DIM,), -1, np.int64)

    def put(col, src):
        src = np.asarray(src)
        idx[col:col + len(src)] = src

    def put_head(pos, src):
        put(pos * HEAD_DIM, src)

    put(_S_CQ * LANES, _O_CQ + np.arange(MLA_Q_LORA))
    put(_S_CKV * LANES, _O_CKV + np.arange(MLA_KV_LORA))
    put(_S_KR * LANES + MLA_NOPE, _O_KR + np.arange(MLA_ROPE))
    put(_S_KRS * LANES + MLA_NOPE, _O_KR + _swap_halves(MLA_ROPE))
    for h in range(NSA_HEADS):
        put_head(_H_NQ + h, _O_NQ + h * HEAD_DIM + np.arange(HEAD_DIM))
        put_head(_H_NQS + h, _O_NQ + h * HEAD_DIM + _swap_halves(PARTIAL_ROT))
    for hk, hks, ok in ((_H_KC, _H_KCS, _O_NKC), (_H_KS, _H_KSS, _O_NKS), (_H_KW, _H_KWS, _O_NKW)):
        for g in range(NSA_KV_HEADS):
            put_head(hk + g, ok + g * HEAD_DIM + np.arange(HEAD_DIM))
            put_head(hks + g, ok + g * HEAD_DIM + _swap_halves(PARTIAL_ROT))
    for g in range(NSA_KV_HEADS):
        put_head(_H_VC + g, _O_NVC + g * HEAD_DIM + np.arange(HEAD_DIM))
    for h in range(SB_HEADS):
        put_head(_H_SBQ + h, _O_SBQ + h * HEAD_DIM + np.arange(HEAD_DIM))
        put_head(_H_SBK + h, _O_SBK + h * HEAD_DIM + np.arange(HEAD_DIM))
    return idx


def _mla_up_index():
    qd = MLA_NOPE + MLA_ROPE
    kd = MLA_NOPE + MLA_V
    uq = np.full((MLA_HEADS * LANES,), -1, np.int64)
    uqs = uq.copy()
    uk = uq.copy()
    for h in range(MLA_HEADS):
        uq[h * LANES:h * LANES + qd] = h * qd + np.arange(qd)
        uqs[h * LANES + MLA_NOPE:h * LANES + qd] = h * qd + MLA_NOPE + _swap_halves(MLA_ROPE)
        uk[h * LANES:h * LANES + MLA_NOPE] = h * kd + np.arange(MLA_NOPE)
    return uq, uqs, uk


def _transposed_weights(w_in, gate_bias):
    width = NSA_KV_HEADS * HEAD_DIM
    gate_rows = jnp.pad(w_in[..., _O_GATE:_O_GATE + N_GATES], ((0, 0), (0, 0), (0, _T_SBV - _T_GATE - N_GATES)))
    rows = jnp.concatenate([w_in[..., _O_NVS:_O_NVS + width], w_in[..., _O_NVW:_O_NVW + width], gate_rows,
                            w_in[..., _O_SBV:_O_SBV + SB_HEADS * HEAD_DIM]], axis=-1)
    bias = jnp.pad(gate_bias, ((0, 0), (0, GATE_ROWS - N_GATES)))[..., None]
    return jnp.swapaxes(rows, -1, -2).astype(_MXU), bias


def _rope_tables(s):
    pos = np.arange(s, dtype=np.float64)

    def cs(rot):
        half = rot // 2
        ang = pos[:, None] * (ROPE_THETA ** (-np.arange(half, dtype=np.float64) / half))[None, :]
        c, sn = np.cos(ang), np.sin(ang)
        return np.concatenate([c, c], axis=1), np.concatenate([-sn, sn], axis=1)

    c, sn = cs(MLA_ROPE)
    pad = np.zeros((s, LANES - MLA_NOPE - MLA_ROPE))
    ck = np.concatenate([np.ones((s, MLA_NOPE)), c, pad], axis=1)
    sk = np.concatenate([np.zeros((s, MLA_NOPE)), sn, pad], axis=1)
    q_scale = (MLA_NOPE + MLA_ROPE) ** -0.5 * LOG2_E
    c, sn = cs(PARTIAL_ROT)
    c64 = np.concatenate([c, np.ones((s, HEAD_DIM - PARTIAL_ROT))], axis=1)
    s64 = np.concatenate([sn, np.zeros((s, HEAD_DIM - PARTIAL_ROT))], axis=1)
    ns = s // SEL_LEN
    onehot = (np.arange(s)[:, None] // SEL_LEN == np.arange(ns)[None, :]) * -NEG_INF
    tables = [jnp.asarray(t, _F32) for t in (ck * q_scale, sk * q_scale, ck, sk, c64, s64)]
    return tables + [jnp.asarray(onehot, _MXU)]


def kernel(x, ffn1_norm, ffn1_w_gate, ffn1_w_up, ffn1_w_down, mix_norm, w_in, mla_q_norm, mla_w_uq, mla_kv_norm,
           mla_w_ukv, nsa_gate_bias, nsa_cmp_pos_k, nsa_cmp_w1_k, nsa_cmp_w2_k, nsa_cmp_pos_v, nsa_cmp_w1_v,
           nsa_cmp_w2_v, w_out, ffn2_norm, ffn2_w_gate, ffn2_w_up, ffn2_w_down, final_norm):
    b, s, d = x.shape
    depth = w_in.shape[0]
    tabs = _rope_tables(s)
    in_idx = _w_in_index()
    uq_idx, uqs_idx, uk_idx = _mla_up_index()
    half = CMP_LEN * HEAD_DIM // 2
    fg = final_norm.reshape(1, d)

    def row(p):
        return p[:, None, :]

    def cmp_weights(w1, w2, pos, transpose_out):
        pos = jnp.broadcast_to(pos.reshape(depth, 2, 1, half), (depth, 2, 8, half)).astype(_MXU)
        w2 = jnp.swapaxes(w2, -1, -2) if transpose_out else w2
        return w1.reshape(depth, 2, half, CMP_HIDDEN).astype(_MXU), w2.astype(_MXU), pos

    ffn1 = [w.astype(_MXU) for w in (ffn1_w_gate, ffn1_w_up, ffn1_w_down)]
    ffn2 = [w.astype(_MXU) for w in (ffn2_w_gate, ffn2_w_up, ffn2_w_down)]
    w_t, gate_bias = _transposed_weights(w_in, nsa_gate_bias)
    wuv_t = mla_w_ukv.reshape(depth, MLA_KV_LORA, MLA_HEADS, 2, MLA_V)[:, :, :, 1].transpose(0, 2, 3, 1).astype(_MXU)
    proj_params = (row(mix_norm), _gather_cols(w_in, in_idx), w_t,
                   row(mla_q_norm), _gather_cols(mla_w_uq, uq_idx), _gather_cols(mla_w_uq, uqs_idx),
                   row(mla_kv_norm), _gather_cols(mla_w_ukv, uk_idx), wuv_t, gate_bias)
    cmp_params = (cmp_weights(nsa_cmp_w1_k, nsa_cmp_w2_k, nsa_cmp_pos_k, False)
                  + cmp_weights(nsa_cmp_w1_v, nsa_cmp_w2_v, nsa_cmp_pos_v, True))
    w_out = w_out.astype(_MXU)
    ffn1_norm, ffn2_norm = row(ffn1_norm), row(ffn2_norm)

    for l in range(depth):
        x2d = _ffn(x.reshape(b * s, d), ffn1_norm, *ffn1, l, fg, False)
        x = x2d.reshape(b, s, d)
        (mq, mk, mvt, nq, nkc, nvc, nks, nvst, nkw, nvwt, gates_t, sbq, sbk, sbv) = _proj(x, l, *proj_params, tabs)
        o_mla = _mla_attention(mq, mk, mvt)
        kc, vct = _compress(nkc, nvc, l, *cmp_params)
        o_cmp, q_sel = _cmp_select(nq, kc, vct, gates_t)
        o_sel = _sel_attention(q_sel, nks, nvst, gates_t)
        o_win = _win_attention(nq, nkw, nvwt, gates_t)
        o_sb = _sb_attention(sbq, sbk, sbv)
        heads = [o.reshape(b * s, -1) for o in (o_mla, o_cmp, o_sel, o_win, o_sb)]
        x2d = _ffn(x.reshape(b * s, d), ffn2_norm, *ffn2, l, fg, l == depth - 1, mixer=(*heads, w_out))
        x = x2d.reshape(b, s, d)
    return x
```
